```python
import jax, jax.numpy as jnp
from jax import lax
import numpy as np

D_MODEL = 2048
BATCH = 8
SEQ = 4096
DEPTH = 2

POOL_WINDOWS = (2, 4, 8, 16)
POOL_GROUPS = 4
POOL_GROUP_DIM = D_MODEL // 16
POOL_WIDTH = POOL_GROUPS * POOL_GROUP_DIM
GDN_HEAD_DIM = 128
GDN_HEADS = (3 * D_MODEL // 8) // GDN_HEAD_DIM
GDN_WIDTH = GDN_HEADS * GDN_HEAD_DIM
GDN_CONV = 4
GDN_CHUNK = 64
LRU_BLOCK_DIM = 128
LRU_WIDTH = D_MODEL - POOL_WIDTH - GDN_WIDTH
LRU_BLOCKS = LRU_WIDTH // LRU_BLOCK_DIM
LRU_CONV = 4
LRU_C = 8.0
IN_SIZES = (POOL_WIDTH, GDN_WIDTH, GDN_WIDTH, GDN_WIDTH, GDN_WIDTH, GDN_HEADS, GDN_HEADS, LRU_WIDTH, LRU_WIDTH)
IN_COLS = POOL_WIDTH + 4 * GDN_WIDTH + 2 * GDN_HEADS + 2 * LRU_WIDTH
D_FF = 3 * D_MODEL
FFN_CONV = 3
EPS = 1e-6

kernel_name = 'hymba_pool_gdn_rglru_convffn'


def rmsnorm(x, w):
    xf = x.astype(jnp.float32)
    y = xf * lax.rsqrt(jnp.mean(xf * xf, axis=-1, keepdims=True) + EPS)
    return (y * w.astype(jnp.float32)).astype(x.dtype)


def l2norm(t):
    return t * lax.rsqrt(jnp.sum(t * t, axis=-1, keepdims=True) + EPS)


def causal_dwconv(x, w):
    k = w.shape[0]
    return lax.conv_general_dilated(
        x, w[:, None, :].astype(x.dtype), window_strides=(1,), padding=[(k - 1, 0)],
        dimension_numbers=('NWC', 'WIO', 'NWC'), feature_group_count=x.shape[-1])


def split_points():
    return np.cumsum(np.array(IN_SIZES))[:-1].tolist()


def pool_mixer(u, w, b, scale):
    bsz, s, _ = u.shape
    uf = u.astype(jnp.float32).reshape(bsz, s, POOL_GROUPS, POOL_GROUP_DIM)
    cs = jnp.cumsum(uf, axis=1)
    pos = jnp.arange(s)
    outs = []
    for g, win in enumerate(POOL_WINDOWS):
        c = cs[:, :, g]
        lag = jnp.pad(c, ((0, 0), (win, 0), (0, 0)))[:, :s]
        cnt = jnp.minimum(pos + 1, win).astype(jnp.float32)[None, :, None]
        outs.append((c - lag) / cnt - uf[:, :, g])
    d = jnp.stack(outs, axis=2).astype(u.dtype)
    y = jnp.einsum('bsgc,gcd->bsgd', d, w) + b
    return y.reshape(bsz, s, POOL_WIDTH) * scale


def gated_deltanet(q, k, v, z, a, bt, conv_w, a_log, dt_bias, norm_w):
    bsz, s, _ = q.shape
    H, Dh, C = GDN_HEADS, GDN_HEAD_DIM, GDN_CHUNK
    N = s // C
    out_dtype = z.dtype
    qkv = jax.nn.silu(causal_dwconv(jnp.concatenate([q, k, v], axis=-1), conv_w)).astype(jnp.float32)
    q, k, v = jnp.split(qkv, 3, axis=-1)

    def heads(t):
        return t.reshape(bsz, N, C, H, Dh).transpose(0, 3, 1, 2, 4)

    def per_head(t):
        return t.reshape(bsz, N, C, H).transpose(0, 3, 1, 2)

    q = l2norm(heads(q)) * (Dh ** -0.5)
    k = l2norm(heads(k))
    v = heads(v)
    beta = jax.nn.sigmoid(per_head(bt.astype(jnp.float32)))
    g = -jnp.exp(a_log.astype(jnp.float32)) * jax.nn.softplus(a.astype(jnp.float32) + dt_bias.astype(jnp.float32))
    g = jnp.cumsum(per_head(g), axis=-1)

    causal = jnp.tril(jnp.ones((C, C), dtype=bool))
    strict = jnp.tril(jnp.ones((C, C), dtype=bool), -1)
    decay = jnp.exp(jnp.where(causal, g[..., :, None] - g[..., None, :], -jnp.inf))
    kk = jnp.einsum('bhncd,bhnsd->bhncs', k, k)
    m = jnp.where(strict, beta[..., None] * kk * decay, 0.0) + jnp.eye(C, dtype=jnp.float32)
    rhs = jnp.concatenate([k * (beta * jnp.exp(g))[..., None], v * beta[..., None]], axis=-1)
    wu = lax.linalg.triangular_solve(m, rhs, left_side=True, lower=True)
    w_c, u_c = wu[..., :Dh], wu[..., Dh:]
    attn = jnp.einsum('bhncd,bhnsd->bhncs', q, k) * decay
    g_last = g[..., -1]
    q_dec = q * jnp.exp(g)[..., None]
    k_dec = k * jnp.exp(g_last[..., None] - g)[..., None]

    def step(state, xs):
        qd, kd, wc, uc, at, gl = xs
        v_new = uc - jnp.einsum('bhcd,bhde->bhce', wc, state)
        o = jnp.einsum('bhcd,bhde->bhce', qd, state) + jnp.einsum('bhcs,bhse->bhce', at, v_new)
        state = state * jnp.exp(gl)[..., None, None] + jnp.einsum('bhcd,bhce->bhde', kd, v_new)
        return state, o

    xs = tuple(jnp.moveaxis(t, 2, 0) for t in (q_dec, k_dec, w_c, u_c, attn, g_last))
    s0 = jnp.zeros((bsz, H, Dh, Dh), jnp.float32)
    _, o = lax.scan(step, s0, xs)
    o = o.transpose(1, 0, 3, 2, 4).reshape(bsz, s, H, Dh)
    zf = z.astype(jnp.float32).reshape(bsz, s, H, Dh)
    o = o * lax.rsqrt(jnp.mean(o * o, axis=-1, keepdims=True) + EPS) * norm_w.astype(jnp.float32) * jax.nn.silu(zf)
    return o.reshape(bsz, s, GDN_WIDTH).astype(out_dtype)


def rglru_mixer(xb, gate, conv_w, conv_b, wa, ba, wx, bx, lam):
    bsz, s, _ = xb.shape
    xc = causal_dwconv(xb, conv_w) + conv_b
    xh = xc.reshape(bsz, s, LRU_BLOCKS, LRU_BLOCK_DIM)
    r = jax.nn.sigmoid((jnp.einsum('bshc,hcd->bshd', xh, wa).reshape(bsz, s, LRU_WIDTH) + ba).astype(jnp.float32))
    i = jax.nn.sigmoid((jnp.einsum('bshc,hcd->bshd', xh, wx).reshape(bsz, s, LRU_WIDTH) + bx).astype(jnp.float32))
    log_a = -LRU_C * r * jax.nn.softplus(-lam.astype(jnp.float32))
    a = jnp.exp(log_a)
    mult = jnp.sqrt(-jnp.expm1(2.0 * log_a))
    pos = jnp.arange(s)[None, :, None]
    mult = jnp.where(pos == 0, 1.0, mult)
    b_in = mult * i * xc.astype(jnp.float32)

    def combine(lhs, rhs):
        return (lhs[0] * rhs[0], rhs[0] * lhs[1] + rhs[1])

    _, h = lax.associative_scan(combine, (a, b_in), axis=1)
    y = h * jax.nn.gelu(gate.astype(jnp.float32))
    return y.astype(xb.dtype)


def conv_ffn(h, w_up, conv_w, w_down):
    up = h @ w_up
    gate, val = jnp.split(up, 2, axis=-1)
    gate = causal_dwconv(gate, conv_w)
    return (jax.nn.gelu(gate) * val) @ w_down


def _fwd_setup_inputs(seed: int = 0) -> dict:
    key = jax.random.key(seed)
    ks = jax.random.split(key, 24)
    f32 = jnp.float32
    L = DEPTH
    nrm = lambda k, shape, sc: jax.random.normal(k, shape, f32) * sc
    gain = lambda k, shape: 1.0 + 0.02 * jax.random.normal(k, shape, f32)
    dt = jnp.exp(jax.random.uniform(ks[8], (L, GDN_HEADS), f32, np.log(1e-3), np.log(1e-1)))
    a0 = jax.random.uniform(ks[17], (L, LRU_WIDTH), f32, 0.9, 0.999) ** (1.0 / LRU_C)
    return {
        'x': nrm(ks[0], (BATCH, SEQ, D_MODEL), 1.0),
        'norm1_w': gain(ks[1], (L, D_MODEL)),
        'w_in': nrm(ks[2], (L, D_MODEL, IN_COLS), D_MODEL ** -0.5),
        'pool_w': nrm(ks[3], (L, POOL_GROUPS, POOL_GROUP_DIM, POOL_GROUP_DIM), POOL_GROUP_DIM ** -0.5),
        'pool_b': nrm(ks[4], (L, POOL_GROUPS, POOL_GROUP_DIM), 0.01),
        'pool_scale': 0.5 + 0.05 * jax.random.normal(ks[5], (L, POOL_WIDTH), f32),
        'gdn_conv_w': nrm(ks[6], (L, GDN_CONV, 3 * GDN_WIDTH), GDN_CONV ** -0.5),
        'gdn_a_log': jnp.log(jax.random.uniform(ks[7], (L, GDN_HEADS), f32, 1.0, 16.0)),
        'gdn_dt_bias': dt + jnp.log(-jnp.expm1(-dt)),
        'gdn_norm_w': gain(ks[9], (L, GDN_HEAD_DIM)),
        'lru_conv_w': nrm(ks[10], (L, LRU_CONV, LRU_WIDTH), LRU_CONV ** -0.5),
        'lru_conv_b': nrm(ks[11], (L, LRU_WIDTH), 0.01),
        'lru_wa': nrm(ks[12], (L, LRU_BLOCKS, LRU_BLOCK_DIM, LRU_BLOCK_DIM), LRU_BLOCK_DIM ** -0.5),
        'lru_ba': nrm(ks[13], (L, LRU_WIDTH), 0.01),
        'lru_wx': nrm(ks[14], (L, LRU_BLOCKS, LRU_BLOCK_DIM, LRU_BLOCK_DIM), LRU_BLOCK_DIM ** -0.5),
        'lru_bx': nrm(ks[15], (L, LRU_WIDTH), 0.01),
        'lru_lambda': jnp.log(a0 / (1.0 - a0)),
        'w_out': nrm(ks[16], (L, D_MODEL, D_MODEL), D_MODEL ** -0.5),
        'norm2_w': gain(ks[18], (L, D_MODEL)),
        'ffn_up': nrm(ks[19], (L, D_MODEL, 2 * D_FF), D_MODEL ** -0.5),
        'ffn_conv_w': nrm(ks[20], (L, FFN_CONV, D_FF), FFN_CONV ** -0.5),
        'ffn_down': nrm(ks[21], (L, D_FF, D_MODEL), D_FF ** -0.5),
        'final_norm_w': gain(ks[22], (D_MODEL,)),
    }


def _fwd_reference(x, norm1_w, w_in, pool_w, pool_b, pool_scale, gdn_conv_w, gdn_a_log, gdn_dt_bias,
              gdn_norm_w, lru_conv_w, lru_conv_b, lru_wa, lru_ba, lru_wx, lru_bx, lru_lambda,
              w_out, norm2_w, ffn_up, ffn_conv_w, ffn_down, final_norm_w):
    cuts = split_points()
    for l in range(DEPTH):
        h = rmsnorm(x, norm1_w[l])
        proj = h @ w_in[l]
        u_pool, q, k, v, z, a, bt, xr, gr = jnp.split(proj, cuts, axis=-1)
        y_pool = pool_mixer(u_pool, pool_w[l], pool_b[l], pool_scale[l])
        y_gdn = gated_deltanet(q, k, v, z, a, bt, gdn_conv_w[l], gdn_a_log[l], gdn_dt_bias[l], gdn_norm_w[l])
        y_lru = rglru_mixer(xr, gr, lru_conv_w[l], lru_conv_b[l], lru_wa[l], lru_ba[l],
                            lru_wx[l], lru_bx[l], lru_lambda[l])
        mixed = jnp.concatenate([y_pool.astype(x.dtype), y_gdn.astype(x.dtype), y_lru.astype(x.dtype)], axis=-1)
        x = x + mixed @ w_out[l]
        x = x + conv_ffn(rmsnorm(x, norm2_w[l]), ffn_up[l], ffn_conv_w[l], ffn_down[l])
    return rmsnorm(x, final_norm_w)


import jax as _jax
import jax.numpy as _jnp

TWIN_FORMAT = 'train_step'
FWD_PARAMS = ['x', 'norm1_w', 'w_in', 'pool_w', 'pool_b', 'pool_scale', 'gdn_conv_w', 'gdn_a_log', 'gdn_dt_bias', 'gdn_norm_w', 'lru_conv_w', 'lru_conv_b', 'lru_wa', 'lru_ba', 'lru_wx', 'lru_bx', 'lru_lambda', 'w_out', 'norm2_w', 'ffn_up', 'ffn_conv_w', 'ffn_down', 'final_norm_w']
TWIN_WEIGHTS = ['norm1_w', 'w_in', 'pool_w', 'pool_b', 'pool_scale', 'gdn_conv_w', 'gdn_a_log', 'gdn_dt_bias', 'gdn_norm_w', 'lru_conv_w', 'lru_conv_b', 'lru_wa', 'lru_ba', 'lru_wx', 'lru_bx', 'lru_lambda', 'w_out', 'norm2_w', 'ffn_up', 'ffn_conv_w', 'ffn_down', 'final_norm_w']
TWIN_DIFF_INPUT = 'x'
TWIN_INPUTS = ['x', 'norm1_w', 'w_in', 'pool_w', 'pool_b', 'pool_scale', 'gdn_conv_w', 'gdn_a_log', 'gdn_dt_bias', 'gdn_norm_w', 'lru_conv_w', 'lru_conv_b', 'lru_wa', 'lru_ba', 'lru_wx', 'lru_bx', 'lru_lambda', 'w_out', 'norm2_w', 'ffn_up', 'ffn_conv_w', 'ffn_down', 'final_norm_w', 'loss_target', 'm_norm1_w', 'm_w_in', 'm_pool_w', 'm_pool_b', 'm_pool_scale', 'm_gdn_conv_w', 'm_gdn_a_log', 'm_gdn_dt_bias', 'm_gdn_norm_w', 'm_lru_conv_w', 'm_lru_conv_b', 'm_lru_wa', 'm_lru_ba', 'm_lru_wx', 'm_lru_bx', 'm_lru_lambda', 'm_w_out', 'm_norm2_w', 'm_ffn_up', 'm_ffn_conv_w', 'm_ffn_down', 'm_final_norm_w', 'v_norm1_w', 'v_w_in', 'v_pool_w', 'v_pool_b', 'v_pool_scale', 'v_gdn_conv_w', 'v_gdn_a_log', 'v_gdn_dt_bias', 'v_gdn_norm_w', 'v_lru_conv_w', 'v_lru_conv_b', 'v_lru_wa', 'v_lru_ba', 'v_lru_wx', 'v_lru_bx', 'v_lru_lambda', 'v_w_out', 'v_norm2_w', 'v_ffn_up', 'v_ffn_conv_w', 'v_ffn_down', 'v_final_norm_w']
TWIN_OUTPUTS = ['loss', 'grad_x', 'grad_norm1_w', 'grad_w_in', 'grad_pool_w', 'grad_pool_b', 'grad_pool_scale', 'grad_gdn_conv_w', 'grad_gdn_a_log', 'grad_gdn_dt_bias', 'grad_gdn_norm_w', 'grad_lru_conv_w', 'grad_lru_conv_b', 'grad_lru_wa', 'grad_lru_ba', 'grad_lru_wx', 'grad_lru_bx', 'grad_lru_lambda', 'grad_w_out', 'grad_norm2_w', 'grad_ffn_up', 'grad_ffn_conv_w', 'grad_ffn_down', 'grad_final_norm_w', 'delta_norm1_w', 'delta_w_in', 'delta_pool_w', 'delta_pool_b', 'delta_pool_scale', 'delta_gdn_conv_w', 'delta_gdn_a_log', 'delta_gdn_dt_bias', 'delta_gdn_norm_w', 'delta_lru_conv_w', 'delta_lru_conv_b', 'delta_lru_wa', 'delta_lru_ba', 'delta_lru_wx', 'delta_lru_bx', 'delta_lru_lambda', 'delta_w_out', 'delta_norm2_w', 'delta_ffn_up', 'delta_ffn_conv_w', 'delta_ffn_down', 'delta_final_norm_w', 'new_m_norm1_w', 'new_m_w_in', 'new_m_pool_w', 'new_m_pool_b', 'new_m_pool_scale', 'new_m_gdn_conv_w', 'new_m_gdn_a_log', 'new_m_gdn_dt_bias', 'new_m_gdn_norm_w', 'new_m_lru_conv_w', 'new_m_lru_conv_b', 'new_m_lru_wa', 'new_m_lru_ba', 'new_m_lru_wx', 'new_m_lru_bx', 'new_m_lru_lambda', 'new_m_w_out', 'new_m_norm2_w', 'new_m_ffn_up', 'new_m_ffn_conv_w', 'new_m_ffn_down', 'new_m_final_norm_w', 'new_v_norm1_w', 'new_v_w_in', 'new_v_pool_w', 'new_v_pool_b', 'new_v_pool_scale', 'new_v_gdn_conv_w', 'new_v_gdn_a_log', 'new_v_gdn_dt_bias', 'new_v_gdn_norm_w', 'new_v_lru_conv_w', 'new_v_lru_conv_b', 'new_v_lru_wa', 'new_v_lru_ba', 'new_v_lru_wx', 'new_v_lru_bx', 'new_v_lru_lambda', 'new_v_w_out', 'new_v_norm2_w', 'new_v_ffn_up', 'new_v_ffn_conv_w', 'new_v_ffn_down', 'new_v_final_norm_w']
TWIN_LEAF_KINDS = {'loss': 'loss', 'grad_x': 'grad_x', 'grad_norm1_w': 'grad_w', 'grad_w_in': 'grad_w', 'grad_pool_w': 'grad_w', 'grad_pool_b': 'grad_w', 'grad_pool_scale': 'grad_w', 'grad_gdn_conv_w': 'grad_w', 'grad_gdn_a_log': 'grad_w', 'grad_gdn_dt_bias': 'grad_w', 'grad_gdn_norm_w': 'grad_w', 'grad_lru_conv_w': 'grad_w', 'grad_lru_conv_b': 'grad_w', 'grad_lru_wa': 'grad_w', 'grad_lru_ba': 'grad_w', 'grad_lru_wx': 'grad_w', 'grad_lru_bx': 'grad_w', 'grad_lru_lambda': 'grad_w', 'grad_w_out': 'grad_w', 'grad_norm2_w': 'grad_w', 'grad_ffn_up': 'grad_w', 'grad_ffn_conv_w': 'grad_w', 'grad_ffn_down': 'grad_w', 'grad_final_norm_w': 'grad_w', 'delta_norm1_w': 'delta_w', 'delta_w_in': 'delta_w', 'delta_pool_w': 'delta_w', 'delta_pool_b': 'delta_w', 'delta_pool_scale': 'delta_w', 'delta_gdn_conv_w': 'delta_w', 'delta_gdn_a_log': 'delta_w', 'delta_gdn_dt_bias': 'delta_w', 'delta_gdn_norm_w': 'delta_w', 'delta_lru_conv_w': 'delta_w', 'delta_lru_conv_b': 'delta_w', 'delta_lru_wa': 'delta_w', 'delta_lru_ba': 'delta_w', 'delta_lru_wx': 'delta_w', 'delta_lru_bx': 'delta_w', 'delta_lru_lambda': 'delta_w', 'delta_w_out': 'delta_w', 'delta_norm2_w': 'delta_w', 'delta_ffn_up': 'delta_w', 'delta_ffn_conv_w': 'delta_w', 'delta_ffn_down': 'delta_w', 'delta_final_norm_w': 'delta_w', 'new_m_norm1_w': 'new_m', 'new_m_w_in': 'new_m', 'new_m_pool_w': 'new_m', 'new_m_pool_b': 'new_m', 'new_m_pool_scale': 'new_m', 'new_m_gdn_conv_w': 'new_m', 'new_m_gdn_a_log': 'new_m', 'new_m_gdn_dt_bias': 'new_m', 'new_m_gdn_norm_w': 'new_m', 'new_m_lru_conv_w': 'new_m', 'new_m_lru_conv_b': 'new_m', 'new_m_lru_wa': 'new_m', 'new_m_lru_ba': 'new_m', 'new_m_lru_wx': 'new_m', 'new_m_lru_bx': 'new_m', 'new_m_lru_lambda': 'new_m', 'new_m_w_out': 'new_m', 'new_m_norm2_w': 'new_m', 'new_m_ffn_up': 'new_m', 'new_m_ffn_conv_w': 'new_m', 'new_m_ffn_down': 'new_m', 'new_m_final_norm_w': 'new_m', 'new_v_norm1_w': 'new_v', 'new_v_w_in': 'new_v', 'new_v_pool_w': 'new_v', 'new_v_pool_b': 'new_v', 'new_v_pool_scale': 'new_v', 'new_v_gdn_conv_w': 'new_v', 'new_v_gdn_a_log': 'new_v', 'new_v_gdn_dt_bias': 'new_v', 'new_v_gdn_norm_w': 'new_v', 'new_v_lru_conv_w': 'new_v', 'new_v_lru_conv_b': 'new_v', 'new_v_lru_wa': 'new_v', 'new_v_lru_ba': 'new_v', 'new_v_lru_wx': 'new_v', 'new_v_lru_bx': 'new_v', 'new_v_lru_lambda': 'new_v', 'new_v_w_out': 'new_v', 'new_v_norm2_w': 'new_v', 'new_v_ffn_up': 'new_v', 'new_v_ffn_conv_w': 'new_v', 'new_v_ffn_down': 'new_v', 'new_v_final_norm_w': 'new_v'}


def _forward(args):
    return _fwd_reference(*[args[k] for k in FWD_PARAMS])


def _output_shape():
    def fwd():
        inp = _fwd_setup_inputs(0)
        return _fwd_reference(*[inp[k] for k in FWD_PARAMS])
    out = _jax.eval_shape(fwd)
    return out.shape, out.dtype

N_MICROBATCH = 1
ADAM_LR = 0.001
ADAM_B1 = 0.9
ADAM_B2 = 0.999
ADAM_EPS = 1e-08
ADAM_WD = 0.01
ADAM_STEP = 10
PER_EXAMPLE_BATCH_AXIS = {'x': 0, 'loss_target': 0}
SHARED_INPUTS = []
_WEIGHT_DTYPES = {'norm1_w': _jnp.float32, 'w_in': _jnp.float32, 'pool_w': _jnp.float32, 'pool_b': _jnp.float32, 'pool_scale': _jnp.float32, 'gdn_conv_w': _jnp.float32, 'gdn_a_log': _jnp.float32, 'gdn_dt_bias': _jnp.float32, 'gdn_norm_w': _jnp.float32, 'lru_conv_w': _jnp.float32, 'lru_conv_b': _jnp.float32, 'lru_wa': _jnp.float32, 'lru_ba': _jnp.float32, 'lru_wx': _jnp.float32, 'lru_bx': _jnp.float32, 'lru_lambda': _jnp.float32, 'w_out': _jnp.float32, 'norm2_w': _jnp.float32, 'ffn_up': _jnp.float32, 'ffn_conv_w': _jnp.float32, 'ffn_down': _jnp.float32, 'final_norm_w': _jnp.float32}
MOMENT_SCALE = {'norm1_w': 6.938091e-02, 'w_in': 4.288033e-02, 'pool_w': 3.919392e-02, 'pool_b': 6.448572e-02, 'pool_scale': 7.913117e-02, 'gdn_conv_w': 4.177818e-02, 'gdn_a_log': 1.777422e-01, 'gdn_dt_bias': 1.742044e-01, 'gdn_norm_w': 1.443786e-01, 'lru_conv_w': 4.537600e-02, 'lru_conv_b': 4.564789e-01, 'lru_wa': 1.316028e-02, 'lru_ba': 1.095716e-02, 'lru_wx': 2.383376e-02, 'lru_bx': 1.644819e-02, 'lru_lambda': 2.116389e-02, 'w_out': 4.679996e-02, 'norm2_w': 6.671191e-02, 'ffn_up': 2.700489e-02, 'ffn_conv_w': 2.765467e-02, 'ffn_down': 4.601608e-02, 'final_norm_w': 1.598782e+01}


def _to_microbatches(a, axis):
    t = _jnp.moveaxis(a, axis, 0)
    t = t.reshape((N_MICROBATCH, t.shape[0] // N_MICROBATCH) + t.shape[1:])
    return _jnp.moveaxis(t, 1, axis + 1)


def setup_inputs(seed: int = 0) -> dict:
    inp = _fwd_setup_inputs(seed)
    key = _jax.random.fold_in(_jax.random.key(seed), 7919)
    shape, _ = _output_shape()
    out = dict(inp)
    out["loss_target"] = _jax.random.normal(_jax.random.fold_in(key, 0), shape, _jnp.float32)
    for i, name in enumerate(TWIN_WEIGHTS):
        w = inp[name].astype(_jnp.float32)
        if MOMENT_SCALE is None:
            s = _jnp.sqrt(_jnp.mean(_jnp.square(w)) + 1e-30)
        else:
            s = MOMENT_SCALE[name]
        km, kv = _jax.random.split(_jax.random.fold_in(key, i + 1))
        out[name] = w
        out["m_" + name] = s * _jax.random.normal(km, w.shape, _jnp.float32)
        out["v_" + name] = (s * s) * _jax.random.uniform(kv, w.shape, _jnp.float32, 0.5, 1.5)
    if N_MICROBATCH > 1:
        for name, axis in PER_EXAMPLE_BATCH_AXIS.items():
            out[name] = _to_microbatches(out[name], axis)
    return {'x': out['x'], 'norm1_w': out['norm1_w'], 'w_in': out['w_in'], 'pool_w': out['pool_w'], 'pool_b': out['pool_b'], 'pool_scale': out['pool_scale'], 'gdn_conv_w': out['gdn_conv_w'], 'gdn_a_log': out['gdn_a_log'], 'gdn_dt_bias': out['gdn_dt_bias'], 'gdn_norm_w': out['gdn_norm_w'], 'lru_conv_w': out['lru_conv_w'], 'lru_conv_b': out['lru_conv_b'], 'lru_wa': out['lru_wa'], 'lru_ba': out['lru_ba'], 'lru_wx': out['lru_wx'], 'lru_bx': out['lru_bx'], 'lru_lambda': out['lru_lambda'], 'w_out': out['w_out'], 'norm2_w': out['norm2_w'], 'ffn_up': out['ffn_up'], 'ffn_conv_w': out['ffn_conv_w'], 'ffn_down': out['ffn_down'], 'final_norm_w': out['final_norm_w'], 'loss_target': out['loss_target'], 'm_norm1_w': out['m_norm1_w'], 'm_w_in': out['m_w_in'], 'm_pool_w': out['m_pool_w'], 'm_pool_b': out['m_pool_b'], 'm_pool_scale': out['m_pool_scale'], 'm_gdn_conv_w': out['m_gdn_conv_w'], 'm_gdn_a_log': out['m_gdn_a_log'], 'm_gdn_dt_bias': out['m_gdn_dt_bias'], 'm_gdn_norm_w': out['m_gdn_norm_w'], 'm_lru_conv_w': out['m_lru_conv_w'], 'm_lru_conv_b': out['m_lru_conv_b'], 'm_lru_wa': out['m_lru_wa'], 'm_lru_ba': out['m_lru_ba'], 'm_lru_wx': out['m_lru_wx'], 'm_lru_bx': out['m_lru_bx'], 'm_lru_lambda': out['m_lru_lambda'], 'm_w_out': out['m_w_out'], 'm_norm2_w': out['m_norm2_w'], 'm_ffn_up': out['m_ffn_up'], 'm_ffn_conv_w': out['m_ffn_conv_w'], 'm_ffn_down': out['m_ffn_down'], 'm_final_norm_w': out['m_final_norm_w'], 'v_norm1_w': out['v_norm1_w'], 'v_w_in': out['v_w_in'], 'v_pool_w': out['v_pool_w'], 'v_pool_b': out['v_pool_b'], 'v_pool_scale': out['v_pool_scale'], 'v_gdn_conv_w': out['v_gdn_conv_w'], 'v_gdn_a_log': out['v_gdn_a_log'], 'v_gdn_dt_bias': out['v_gdn_dt_bias'], 'v_gdn_norm_w': out['v_gdn_norm_w'], 'v_lru_conv_w': out['v_lru_conv_w'], 'v_lru_conv_b': out['v_lru_conv_b'], 'v_lru_wa': out['v_lru_wa'], 'v_lru_ba': out['v_lru_ba'], 'v_lru_wx': out['v_lru_wx'], 'v_lru_bx': out['v_lru_bx'], 'v_lru_lambda': out['v_lru_lambda'], 'v_w_out': out['v_w_out'], 'v_norm2_w': out['v_norm2_w'], 'v_ffn_up': out['v_ffn_up'], 'v_ffn_conv_w': out['v_ffn_conv_w'], 'v_ffn_down': out['v_ffn_down'], 'v_final_norm_w': out['v_final_norm_w']}


def _loss(weights, diff, rest, loss_target):
    with _jax.named_scope("forward"):
        args = {**rest, TWIN_DIFF_INPUT: diff, **{k: w.astype(_WEIGHT_DTYPES[k]) for k, w in weights.items()}}
        y = _forward(args)
    with _jax.named_scope("loss_head"):
        err = _jnp.square(y.astype(_jnp.float32) - loss_target)
        return 0.5 * _jnp.sum(_jnp.mean(err, axis=-1)) if err.ndim else 0.5 * err


def _adamw(w, g, m, v):
    m = ADAM_B1 * m + (1.0 - ADAM_B1) * g
    v = ADAM_B2 * v + (1.0 - ADAM_B2) * _jnp.square(g)
    m_hat = m / (1.0 - ADAM_B1 ** ADAM_STEP)
    v_hat = v / (1.0 - ADAM_B2 ** ADAM_STEP)
    delta = -ADAM_LR * (m_hat / (_jnp.sqrt(v_hat) + ADAM_EPS) + ADAM_WD * w)
    return delta, m, v


def reference(x, norm1_w, w_in, pool_w, pool_b, pool_scale, gdn_conv_w, gdn_a_log, gdn_dt_bias, gdn_norm_w, lru_conv_w, lru_conv_b, lru_wa, lru_ba, lru_wx, lru_bx, lru_lambda, w_out, norm2_w, ffn_up, ffn_conv_w, ffn_down, final_norm_w, loss_target, m_norm1_w, m_w_in, m_pool_w, m_pool_b, m_pool_scale, m_gdn_conv_w, m_gdn_a_log, m_gdn_dt_bias, m_gdn_norm_w, m_lru_conv_w, m_lru_conv_b, m_lru_wa, m_lru_ba, m_lru_wx, m_lru_bx, m_lru_lambda, m_w_out, m_norm2_w, m_ffn_up, m_ffn_conv_w, m_ffn_down, m_final_norm_w, v_norm1_w, v_w_in, v_pool_w, v_pool_b, v_pool_scale, v_gdn_conv_w, v_gdn_a_log, v_gdn_dt_bias, v_gdn_norm_w, v_lru_conv_w, v_lru_conv_b, v_lru_wa, v_lru_ba, v_lru_wx, v_lru_bx, v_lru_lambda, v_w_out, v_norm2_w, v_ffn_up, v_ffn_conv_w, v_ffn_down, v_final_norm_w):
    given = dict(x=x, norm1_w=norm1_w, w_in=w_in, pool_w=pool_w, pool_b=pool_b, pool_scale=pool_scale, gdn_conv_w=gdn_conv_w, gdn_a_log=gdn_a_log, gdn_dt_bias=gdn_dt_bias, gdn_norm_w=gdn_norm_w, lru_conv_w=lru_conv_w, lru_conv_b=lru_conv_b, lru_wa=lru_wa, lru_ba=lru_ba, lru_wx=lru_wx, lru_bx=lru_bx, lru_lambda=lru_lambda, w_out=w_out, norm2_w=norm2_w, ffn_up=ffn_up, ffn_conv_w=ffn_conv_w, ffn_down=ffn_down, final_norm_w=final_norm_w, loss_target=loss_target, m_norm1_w=m_norm1_w, m_w_in=m_w_in, m_pool_w=m_pool_w, m_pool_b=m_pool_b, m_pool_scale=m_pool_scale, m_gdn_conv_w=m_gdn_conv_w, m_gdn_a_log=m_gdn_a_log, m_gdn_dt_bias=m_gdn_dt_bias, m_gdn_norm_w=m_gdn_norm_w, m_lru_conv_w=m_lru_conv_w, m_lru_conv_b=m_lru_conv_b, m_lru_wa=m_lru_wa, m_lru_ba=m_lru_ba, m_lru_wx=m_lru_wx, m_lru_bx=m_lru_bx, m_lru_lambda=m_lru_lambda, m_w_out=m_w_out, m_norm2_w=m_norm2_w, m_ffn_up=m_ffn_up, m_ffn_conv_w=m_ffn_conv_w, m_ffn_down=m_ffn_down, m_final_norm_w=m_final_norm_w, v_norm1_w=v_norm1_w, v_w_in=v_w_in, v_pool_w=v_pool_w, v_pool_b=v_pool_b, v_pool_scale=v_pool_scale, v_gdn_conv_w=v_gdn_conv_w, v_gdn_a_log=v_gdn_a_log, v_gdn_dt_bias=v_gdn_dt_bias, v_gdn_norm_w=v_gdn_norm_w, v_lru_conv_w=v_lru_conv_w, v_lru_conv_b=v_lru_conv_b, v_lru_wa=v_lru_wa, v_lru_ba=v_lru_ba, v_lru_wx=v_lru_wx, v_lru_bx=v_lru_bx, v_lru_lambda=v_lru_lambda, v_w_out=v_w_out, v_norm2_w=v_norm2_w, v_ffn_up=v_ffn_up, v_ffn_conv_w=v_ffn_conv_w, v_ffn_down=v_ffn_down, v_final_norm_w=v_final_norm_w)
    weights = {n: given[n] for n in TWIN_WEIGHTS}
    shared = {n: given[n] for n in SHARED_INPUTS}
    per_example = {n: given[n] for n in ['x']}
    grad_fn = _jax.value_and_grad(_loss, argnums=(0, 1))

    def one_microbatch(ex, loss_target):
        ex = dict(ex)
        diff = ex.pop(TWIN_DIFF_INPUT)
        return grad_fn(weights, diff, {**shared, **ex}, loss_target)

    if N_MICROBATCH == 1:
        loss, (grad_w, grad_x) = one_microbatch(per_example, given["loss_target"])
    else:
        def body(carry, xs):
            loss_sum, grad_sum = carry
            l_k, (gw_k, gx_k) = one_microbatch(xs[0], xs[1])
            with _jax.named_scope("update"):
                return (loss_sum + l_k, _jax.tree.map(_jnp.add, grad_sum, gw_k)), gx_k

        init = (_jnp.zeros((), _jnp.float32), _jax.tree.map(_jnp.zeros_like, weights))
        (loss, grad_w), grad_x = _jax.lax.scan(body, init, (per_example, given["loss_target"]))
    with _jax.named_scope("update"):
        delta_w, new_m, new_v = {}, {}, {}
        for n in TWIN_WEIGHTS:
            delta_w[n], new_m[n], new_v[n] = _adamw(weights[n], grad_w[n], given["m_" + n], given["v_" + n])
    return (loss, grad_x, *[grad_w[n] for n in TWIN_WEIGHTS], *[delta_w[n] for n in TWIN_WEIGHTS],
            *[new_m[n] for n in TWIN_WEIGHTS], *[new_v[n] for n in TWIN_WEIGHTS])
```

```python
import functools

import jax
import jax.numpy as jnp
from jax import lax
from jax.experimental import pallas as pl
from jax.experimental.pallas import tpu as pltpu

F32 = jnp.float32
BF16 = jnp.bfloat16
HI = lax.Precision.HIGHEST
MESH_IDS = pl.DeviceIdType.MESH

N_DEV = 8
D_MODEL = 2048
DEPTH = 2
POOL_WINDOWS = (2, 4, 8, 16)
POOL_W = 512
HEADS = 6
HD = 128
GDN_W = HEADS * HD
CHUNK = 64
LRU_W = 768
LRU_C = 8.0
D_FF = 3 * D_MODEL
EPS = 1e-6
IN_COLS = 5132
PCOLS = 5376
C_QKV, C_Z, C_AB, C_XR, C_GR = 512, 2816, 3584, 3712, 4480
AB_ORIG_END = 3596
M_GDN, M_LRU = 512, 1280

ADAM_LR, ADAM_B1, ADAM_B2, ADAM_EPS, ADAM_WD, ADAM_STEP = 0.001, 0.9, 0.999, 1e-08, 0.01, 10

VMEM_LIMIT = 56 * 1024 * 1024


def _cp(sem):
    return pltpu.CompilerParams(dimension_semantics=sem, vmem_limit_bytes=VMEM_LIMIT)


def _mm(a, b, ca=1, cb=0, prec=None, cast=True):
    if cast:
        a = a.astype(BF16)
        b = b.astype(BF16)
    return lax.dot_general(a, b, (((ca,), (cb,)), ((), ())), preferred_element_type=F32, precision=prec)


def _bmm(a, b, ca=2, cb=1, prec=None, cast=True):
    if cast:
        a = a.astype(BF16)
        b = b.astype(BF16)
    return lax.dot_general(a, b, (((ca,), (cb,)), ((0,), (0,))), preferred_element_type=F32, precision=prec)


def _sigmoid(x):
    return 1.0 / (1.0 + jnp.exp(-x))


def _log1p(e):
    u = 1.0 + e
    return jnp.where(u == 1.0, e, jnp.log(u) * e / jnp.where(u == 1.0, 1.0, u - 1.0))


def _softplus(x):
    return jnp.maximum(x, 0.0) + _log1p(jnp.exp(-jnp.abs(x)))


def _expm1(x):
    u = jnp.exp(x)
    um = u - 1.0
    safe = jnp.where((u == 1.0) | (um == -1.0), 1.0, jnp.log(u))
    return jnp.where(u == 1.0, x, jnp.where(um == -1.0, -1.0, um * x / safe))


_G0 = 0.7978845608028654
_G1 = 0.044715


def _gelu(x):
    return 0.5 * x * (1.0 + jnp.tanh(_G0 * (x + _G1 * x * x * x)))


def _gelu_and_grad(x):
    th = jnp.tanh(_G0 * (x + _G1 * x * x * x))
    g = 0.5 * x * (1.0 + th)
    dg = 0.5 * (1.0 + th) + 0.5 * x * (1.0 - th * th) * _G0 * (1.0 + 3.0 * _G1 * x * x)
    return g, dg


def _tile(T):
    return min(T, 512)


def _matmul(a, b, *, grid, a_spec, b_spec, out_shape, out_spec, dims, acc_shape, name, add=None, add_spec=None):
    nk = grid[2]
    has_add = add is not None

    def body(*refs):
        if has_add:
            a_ref, b_ref, add_ref, o_ref, acc_ref = refs
        else:
            a_ref, b_ref, o_ref, acc_ref = refs
            add_ref = None
        k = pl.program_id(2)
        p = lax.dot_general(a_ref[...].astype(BF16), b_ref[...].astype(BF16), (dims, ((), ())),
                            preferred_element_type=F32)

        def finish(r):
            if has_add:
                r = r + add_ref[...]
            o_ref[...] = r.astype(o_ref.dtype)

        if nk == 1:
            finish(p)
        else:
            @pl.when(k == 0)
            def _():
                acc_ref[...] = p

            @pl.when(k > 0)
            def _():
                acc_ref[...] += p

            @pl.when(k == nk - 1)
            def _():
                finish(acc_ref[...])

    in_specs = [a_spec, b_spec] + ([add_spec] if has_add else [])
    args = (a, b) + ((add,) if has_add else ())
    return pl.pallas_call(
        body, name=name, grid=grid, in_specs=in_specs, out_specs=out_spec, out_shape=out_shape,
        scratch_shapes=[pltpu.VMEM(acc_shape, F32)],
        compiler_params=_cp(("parallel", "parallel", "arbitrary")),
    )(*args)


def _pick(n, cands):
    for c in cands:
        if n % c == 0:
            return c
    raise ValueError(f"no tile for {n}")


def _mm_nn(a, b, out_dtype, name, add=None, tn_c=(1024, 768, 512)):
    M, K = a.shape
    N = b.shape[1]
    tm = _pick(M, (1024, 512, 256))
    tn = _pick(N, tn_c)
    tk = _pick(K, (2048, 1536, 1024, 512, 256))
    return _matmul(
        a, b, grid=(M // tm, N // tn, K // tk),
        a_spec=pl.BlockSpec((tm, tk), lambda i, j, k: (i, k)),
        b_spec=pl.BlockSpec((tk, tn), lambda i, j, k: (k, j)),
        out_shape=jax.ShapeDtypeStruct((M, N), out_dtype),
        out_spec=pl.BlockSpec((tm, tn), lambda i, j, k: (i, j)),
        dims=((1,), (0,)), acc_shape=(tm, tn), name=name, add=add,
        add_spec=pl.BlockSpec((tm, tn), lambda i, j, k: (i, j)))


def _mm_nt(a, b, out_dtype, name, tk_c=(2048, 1536, 1024, 768, 512)):
    M, K = a.shape
    N = b.shape[0]
    tm = _pick(M, (1024, 512, 256))
    tn = _pick(N, (1024, 768, 512))
    tk = _pick(K, tk_c)
    return _matmul(
        a, b, grid=(M // tm, N // tn, K // tk),
        a_spec=pl.BlockSpec((tm, tk), lambda i, j, k: (i, k)),
        b_spec=pl.BlockSpec((tn, tk), lambda i, j, k: (j, k)),
        out_shape=jax.ShapeDtypeStruct((M, N), out_dtype),
        out_spec=pl.BlockSpec((tm, tn), lambda i, j, k: (i, j)),
        dims=((1,), (1,)), acc_shape=(tm, tn), name=name)


def _mm_tn(a, b, out_dtype, name, tn_c=(1024, 768, 512)):
    K, M = a.shape
    N = b.shape[1]
    tm = _pick(M, (1024, 768, 512))
    tn = _pick(N, tn_c)
    tk = _pick(K, (1024, 512, 256))
    return _matmul(
        a, b, grid=(M // tm, N // tn, K // tk),
        a_spec=pl.BlockSpec((tk, tm), lambda i, j, k: (k, i)),
        b_spec=pl.BlockSpec((tk, tn), lambda i, j, k: (k, j)),
        out_shape=jax.ShapeDtypeStruct((M, N), out_dtype),
        out_spec=pl.BlockSpec((tm, tn), lambda i, j, k: (i, j)),
        dims=((0,), (0,)), acc_shape=(tm, tn), name=name)


def _mm_up(h, wup, name):
    M, K = h.shape
    ns = wup.shape[2]
    tm = _pick(M, (1024, 512, 256))
    tn = 768
    per = ns // tn
    return _matmul(
        h, wup, grid=(M // tm, N_DEV * per, 1),
        a_spec=pl.BlockSpec((tm, K), lambda i, j, k: (i, 0)),
        b_spec=pl.BlockSpec((None, K, tn), lambda i, j, k: (j // per, 0, j % per)),
        out_shape=jax.ShapeDtypeStruct((M, N_DEV * ns), F32),
        out_spec=pl.BlockSpec((tm, tn), lambda i, j, k: (i, j)),
        dims=((1,), (0,)), acc_shape=(tm, tn), name=name)


def _mm_up_t(dup, wup, name):
    M = dup.shape[0]
    D, ns = wup.shape[1], wup.shape[2]
    tm = _pick(M, (1024, 512, 256))
    tn = 1024
    tk = ns
    return _matmul(
        dup, wup, grid=(M // tm, D // tn, N_DEV),
        a_spec=pl.BlockSpec((tm, tk), lambda i, j, k: (i, k)),
        b_spec=pl.BlockSpec((None, tn, tk), lambda i, j, k: (k, j, 0)),
        out_shape=jax.ShapeDtypeStruct((M, D), F32),
        out_spec=pl.BlockSpec((tm, tn), lambda i, j, k: (i, j)),
        dims=((1,), (1,)), acc_shape=(tm, tn), name=name)


def _mm_dup(h, dup, ns, name):
    K, M = h.shape
    tm = 1024
    tn = 768
    per = ns // tn
    tk = _pick(K, (1024, 512, 256))
    return _matmul(
        h, dup, grid=(M // tm, N_DEV * per, K // tk),
        a_spec=pl.BlockSpec((tk, tm), lambda i, j, k: (k, i)),
        b_spec=pl.BlockSpec((tk, tn), lambda i, j, k: (k, j)),
        out_shape=jax.ShapeDtypeStruct((N_DEV, M, ns), BF16),
        out_spec=pl.BlockSpec((None, tm, tn), lambda i, j, k: (j // per, i, j % per)),
        dims=((0,), (0,)), acc_shape=(tm, tn), name=name)


def _norm_fwd(x, w, name):
    T, D = x.shape
    tt = _tile(T)

    def body(x_ref, w_ref, h_ref):
        xv = x_ref[...]
        r = lax.rsqrt(jnp.mean(xv * xv, axis=1, keepdims=True) + EPS)
        h_ref[...] = (xv * r * w_ref[...]).astype(BF16)

    return pl.pallas_call(
        body, name=name, grid=(T // tt,),
        in_specs=[pl.BlockSpec((tt, D), lambda t: (t, 0)), pl.BlockSpec((1, D), lambda t: (0, 0))],
        out_specs=pl.BlockSpec((tt, D), lambda t: (t, 0)),
        out_shape=jax.ShapeDtypeStruct((T, D), BF16), compiler_params=_cp(("parallel",)))(x, w)


def _norm_bwd(x, w, dh, dres, name):
    T, D = x.shape
    tt = _tile(T)

    def body(x_ref, w_ref, dh_ref, dres_ref, dx_ref, dw_ref):
        t = pl.program_id(0)
        xv = x_ref[...]
        r = lax.rsqrt(jnp.mean(xv * xv, axis=1, keepdims=True) + EPS)
        xh = xv * r
        dh_v = dh_ref[...]
        dxh = dh_v * w_ref[...]
        dx_ref[...] = dres_ref[...] + r * (dxh - xh * jnp.mean(dxh * xh, axis=1, keepdims=True))
        part = jnp.sum(dh_v * xh, axis=0, keepdims=True)

        @pl.when(t == 0)
        def _():
            dw_ref[...] = part

        @pl.when(t > 0)
        def _():
            dw_ref[...] += part

    row = pl.BlockSpec((tt, D), lambda t: (t, 0))
    vec = pl.BlockSpec((1, D), lambda t: (0, 0))
    return pl.pallas_call(
        body, name=name, grid=(T // tt,), in_specs=[row, vec, row, row], out_specs=[row, vec],
        out_shape=[jax.ShapeDtypeStruct((T, D), F32), jax.ShapeDtypeStruct((1, D), F32)],
        compiler_params=_cp(("arbitrary",)))(x, w, dh, dres)


def _loss_head(x, w, target):
    T, D = x.shape
    tt = _tile(T)

    def body(x_ref, w_ref, t_ref, loss_ref, dx_ref, dw_ref):
        t = pl.program_id(0)
        xv = x_ref[...]
        r = lax.rsqrt(jnp.mean(xv * xv, axis=1, keepdims=True) + EPS)
        xh = xv * r
        err = xh * w_ref[...] - t_ref[...]
        lp = 0.5 * jnp.sum(jnp.mean(err * err, axis=1, keepdims=True), axis=0, keepdims=True)
        dy = err * (1.0 / D)
        dxh = dy * w_ref[...]
        dx_ref[...] = r * (dxh - xh * jnp.mean(dxh * xh, axis=1, keepdims=True))
        part = jnp.sum(dy * xh, axis=0, keepdims=True)
        lpb = jnp.broadcast_to(lp, (1, 128))

        @pl.when(t == 0)
        def _():
            dw_ref[...] = part
            loss_ref[...] = lpb

        @pl.when(t > 0)
        def _():
            dw_ref[...] += part
            loss_ref[...] += lpb

    row = pl.BlockSpec((tt, D), lambda t: (t, 0))
    vec = pl.BlockSpec((1, D), lambda t: (0, 0))
    return pl.pallas_call(
        body, name="loss_head", grid=(T // tt,), in_specs=[row, vec, row],
        out_specs=[pl.BlockSpec((1, 128), lambda t: (0, 0)), row, vec],
        out_shape=[jax.ShapeDtypeStruct((1, 128), F32), jax.ShapeDtypeStruct((T, D), F32),
                   jax.ShapeDtypeStruct((1, D), F32)],
        compiler_params=_cp(("arbitrary",)))(x, w, target)


def _conv_fwd(x, col0, C, w, b, cb, name):
    T = x.shape[0]
    K = w.shape[0]
    tt = _tile(T)
    nt, nc, c0 = T // tt, C // cb, col0 // cb
    has_b = b is not None

    def body(*refs):
        if has_b:
            x_ref, halo_ref, w_ref, b_ref, y_ref = refs
        else:
            x_ref, halo_ref, w_ref, y_ref = refs
        t = pl.program_id(1)
        halo = jnp.where(t == 0, 0.0, halo_ref[...])
        xe = jnp.concatenate([halo, x_ref[...]], axis=0)
        acc = xe * w_ref[K - 1:K, :]
        for j in range(K - 1):
            acc = acc + pltpu.roll(xe, K - 1 - j, 0) * w_ref[j:j + 1, :]
        if has_b:
            acc = acc + b_ref[...]
        y_ref[...] = acc[8:, :]

    in_specs = [pl.BlockSpec((tt, cb), lambda j, t: (t, c0 + j)),
                pl.BlockSpec((8, cb), lambda j, t: (jnp.maximum(t * (tt // 8) - 1, 0), c0 + j)),
                pl.BlockSpec((K, cb), lambda j, t: (0, j))]
    args = [x, x, w]
    if has_b:
        in_specs.append(pl.BlockSpec((1, cb), lambda j, t: (0, j)))
        args.append(b)
    return pl.pallas_call(
        body, name=name, grid=(nc, nt), in_specs=in_specs,
        out_specs=pl.BlockSpec((tt, cb), lambda j, t: (t, j)),
        out_shape=jax.ShapeDtypeStruct((T, C), F32), compiler_params=_cp(("parallel", "parallel")))(*args)


def _conv_bwd(dy, x, col0, w, cb, name, want_db=False):
    T, C = dy.shape
    K = w.shape[0]
    tt = _tile(T)
    nt, nc, c0 = T // tt, C // cb, col0 // cb

    def body(*refs):
        if want_db:
            dy_ref, dyn_ref, x_ref, xp_ref, w_ref, dx_ref, dw_ref, db_ref = refs
        else:
            dy_ref, dyn_ref, x_ref, xp_ref, w_ref, dx_ref, dw_ref = refs
        t = pl.program_id(1)
        dyv = dy_ref[...]
        nxt = jnp.where(t == nt - 1, 0.0, dyn_ref[...])
        dye = jnp.concatenate([dyv, nxt], axis=0)
        n = tt + 8
        acc = dye * w_ref[K - 1:K, :]
        for j in range(K - 1):
            acc = acc + pltpu.roll(dye, n - (K - 1 - j), 0) * w_ref[j:j + 1, :]
        dx_ref[...] = acc[:tt, :].astype(dx_ref.dtype)
        prev = jnp.where(t == 0, 0.0, xp_ref[...])
        xe = jnp.concatenate([prev, x_ref[...]], axis=0)

        @pl.when(t == 0)
        def _():
            dw_ref[...] = jnp.zeros_like(dw_ref)
            if want_db:
                db_ref[...] = jnp.zeros_like(db_ref)

        for j in range(K):
            sh = K - 1 - j
            xs = xe[8:, :] if sh == 0 else pltpu.roll(xe, sh, 0)[8:, :]
            dw_ref[j:j + 1, :] += jnp.sum(dyv * xs, axis=0, keepdims=True)
        if want_db:
            db_ref[...] += jnp.sum(dyv, axis=0, keepdims=True)

    h8 = tt // 8
    in_specs = [pl.BlockSpec((tt, cb), lambda j, t: (t, j)),
                pl.BlockSpec((8, cb), lambda j, t: (jnp.minimum((t + 1) * h8, T // 8 - 1), j)),
                pl.BlockSpec((tt, cb), lambda j, t: (t, c0 + j)),
                pl.BlockSpec((8, cb), lambda j, t: (jnp.maximum(t * h8 - 1, 0), c0 + j)),
                pl.BlockSpec((K, cb), lambda j, t: (0, j))]
    out_specs = [pl.BlockSpec((tt, cb), lambda j, t: (t, j)), pl.BlockSpec((K, cb), lambda j, t: (0, j))]
    out_shape = [jax.ShapeDtypeStruct((T, C), BF16), jax.ShapeDtypeStruct((K, C), F32)]
    if want_db:
        out_specs.append(pl.BlockSpec((1, cb), lambda j, t: (0, j)))
        out_shape.append(jax.ShapeDtypeStruct((1, C), F32))
    return pl.pallas_call(
        body, name=name, grid=(nc, nt), in_specs=in_specs, out_specs=out_specs, out_shape=out_shape,
        compiler_params=_cp(("parallel", "arbitrary")))(dy, dy, x, x, w)


def _pool_d(ue, g, pos, tt):
    win = POOL_WINDOWS[g]
    ug = ue[:, g * 128:(g + 1) * 128]
    s = ug
    k = 1
    while k < win:
        s = s + pltpu.roll(s, k, 0)
        k *= 2
    cnt = jnp.minimum(pos + 1, win).astype(F32)
    return s[16:, :] / cnt - ug[16:, :]


def _pool_fwd(proj, pw, pb, ps):
    T = proj.shape[0]
    tt = _tile(T)

    def body(u_ref, halo_ref, w_ref, b_ref, s_ref, y_ref):
        t = pl.program_id(0)
        halo = jnp.where(t == 0, 0.0, halo_ref[...])
        ue = jnp.concatenate([halo, u_ref[...]], axis=0)
        pos = t * tt + lax.broadcasted_iota(jnp.int32, (tt, 1), 0)
        for g in range(4):
            sl = slice(g * 128, (g + 1) * 128)
            d = _pool_d(ue, g, pos, tt)
            yg = _mm(d, w_ref[g]) + b_ref[:, sl]
            y_ref[:, sl] = (yg * s_ref[:, sl]).astype(BF16)

    vec = pl.BlockSpec((1, POOL_W), lambda t: (0, 0))
    return pl.pallas_call(
        body, name="pool_fwd", grid=(T // tt,),
        in_specs=[pl.BlockSpec((tt, POOL_W), lambda t: (t, 0)),
                  pl.BlockSpec((16, POOL_W), lambda t: (jnp.maximum(t * (tt // 16) - 1, 0), 0)),
                  pl.BlockSpec((4, 128, 128), lambda t: (0, 0, 0)), vec, vec],
        out_specs=pl.BlockSpec((tt, POOL_W), lambda t: (t, 0)),
        out_shape=jax.ShapeDtypeStruct((T, POOL_W), BF16), compiler_params=_cp(("parallel",)))(
            proj, proj, pw, pb, ps)


def _pool_bwd(dmixed, proj, pw, pb, ps):
    T = proj.shape[0]
    tt = _tile(T)
    nt = T // tt

    def body(dy_ref, dyn_ref, u_ref, halo_ref, w_ref, b_ref, s_ref, du_ref, dw_ref, db_ref, ds_ref):
        t = pl.program_id(0)
        halo = jnp.where(t == 0, 0.0, halo_ref[...])
        ue = jnp.concatenate([halo, u_ref[...]], axis=0)
        dyv = dy_ref[...]
        nxt = jnp.where(t == nt - 1, 0.0, dyn_ref[...])
        dye = jnp.concatenate([dyv, nxt], axis=0)
        n = tt + 16
        pos = t * tt + lax.broadcasted_iota(jnp.int32, (tt, 1), 0)
        pos_e = t * tt + lax.broadcasted_iota(jnp.int32, (n, 1), 0)

        @pl.when(t == 0)
        def _():
            dw_ref[...] = jnp.zeros_like(dw_ref)
            db_ref[...] = jnp.zeros_like(db_ref)
            ds_ref[...] = jnp.zeros_like(ds_ref)

        for g in range(4):
            win = POOL_WINDOWS[g]
            sl = slice(g * 128, (g + 1) * 128)
            d = _pool_d(ue, g, pos, tt)
            wg = w_ref[g]
            ypre = _mm(d, wg) + b_ref[:, sl]
            sc = s_ref[:, sl]
            ds_ref[:, sl] += jnp.sum(dyv[:, sl] * ypre, axis=0, keepdims=True)
            dyp_e = dye[:, sl] * sc
            dyp = dyp_e[:tt, :]
            db_ref[:, sl] += jnp.sum(dyp, axis=0, keepdims=True)
            dw_ref[g] += _mm(d, dyp, 0, 0)
            dd_e = _mm(dyp_e, wg, 1, 1)
            cnt_e = jnp.minimum(pos_e + 1, win).astype(F32)
            s = dd_e / cnt_e
            k = 1
            while k < win:
                s = s + pltpu.roll(s, n - k, 0)
                k *= 2
            du_ref[:, sl] = (s[:tt, :] - dd_e[:tt, :]).astype(BF16)

    vec = pl.BlockSpec((1, POOL_W), lambda t: (0, 0))
    h16 = tt // 16
    return pl.pallas_call(
        body, name="pool_bwd", grid=(nt,),
        in_specs=[pl.BlockSpec((tt, POOL_W), lambda t: (t, 0)),
                  pl.BlockSpec((16, POOL_W), lambda t: (jnp.minimum((t + 1) * h16, T // 16 - 1), 0)),
                  pl.BlockSpec((tt, POOL_W), lambda t: (t, 0)),
                  pl.BlockSpec((16, POOL_W), lambda t: (jnp.maximum(t * h16 - 1, 0), 0)),
                  pl.BlockSpec((4, 128, 128), lambda t: (0, 0, 0)), vec, vec],
        out_specs=[pl.BlockSpec((tt, POOL_W), lambda t: (t, 0)),
                   pl.BlockSpec((4, 128, 128), lambda t: (0, 0, 0)), vec, vec],
        out_shape=[jax.ShapeDtypeStruct((T, POOL_W), BF16), jax.ShapeDtypeStruct((4, 128, 128), F32),
                   jax.ShapeDtypeStruct((1, POOL_W), F32), jax.ShapeDtypeStruct((1, POOL_W), F32)],
        compiler_params=_cp(("arbitrary",)))(dmixed, dmixed, proj, proj, pw, pb, ps)


def _gdn_pre_fwd(cpre, proj, alog, dtb):
    T = cpre.shape[0]
    tt = _tile(T)

    def body(c_ref, ab_ref, alog_ref, dtb_ref, qkv_ref, bb_ref, gb_ref):
        for p in range(3):
            for h in range(HEADS):
                cc = c_ref[:, (p * HEADS + h) * HD:(p * HEADS + h + 1) * HD]
                s = cc * _sigmoid(cc)
                if p < 2:
                    s = s * lax.rsqrt(jnp.sum(s * s, axis=1, keepdims=True) + EPS)
                if p == 0:
                    s = s * (HD ** -0.5)
                qkv_ref[p, h] = s
        ab = ab_ref[...]
        g = -jnp.exp(alog_ref[...]) * _softplus(ab + dtb_ref[...])
        r64 = lax.broadcasted_iota(jnp.int32, (tt, 1), 0) & (CHUNK - 1)
        k = 1
        while k < CHUNK:
            g = g + jnp.where(r64 >= k, pltpu.roll(g, k, 0), 0.0)
            k *= 2
        sb = _sigmoid(ab)
        for h in range(HEADS):
            gb_ref[h] = jnp.broadcast_to(g[:, h:h + 1], (tt, HD))
            bb_ref[h] = jnp.broadcast_to(sb[:, HEADS + h:HEADS + h + 1], (tt, HD))

    vec = pl.BlockSpec((1, 128), lambda t: (0, 0))
    hb = pl.BlockSpec((HEADS, tt, HD), lambda t: (0, t, 0))
    return pl.pallas_call(
        body, name="gdn_pre_fwd", grid=(T // tt,),
        in_specs=[pl.BlockSpec((tt, 3 * GDN_W), lambda t: (t, 0)),
                  pl.BlockSpec((tt, 128), lambda t: (t, C_AB // 128)), vec, vec],
        out_specs=[pl.BlockSpec((3, HEADS, tt, HD), lambda t: (0, 0, t, 0)), hb, hb],
        out_shape=[jax.ShapeDtypeStruct((3, HEADS, T, HD), F32), jax.ShapeDtypeStruct((HEADS, T, HD), F32),
                   jax.ShapeDtypeStruct((HEADS, T, HD), F32)],
        compiler_params=_cp(("parallel",)))(cpre, proj, alog, dtb)


def _gdn_pre_bwd(dq, dk, dv, cpre, dbb, dgb, proj, alog, dtb):
    T = cpre.shape[0]
    tt = _tile(T)

    def body(dq_ref, dk_ref, dv_ref, c_ref, dbb_ref, dgb_ref, ab_ref, alog_ref, dtb_ref,
             dc_ref, dab_ref, dalog_ref, ddtb_ref):
        t = pl.program_id(0)
        srcs = (dq_ref, dk_ref, dv_ref)
        for p in range(3):
            for h in range(HEADS):
                sl = slice((p * HEADS + h) * HD, (p * HEADS + h + 1) * HD)
                cc = c_ref[:, sl]
                sg = _sigmoid(cc)
                s = cc * sg
                dyv = srcs[p][h]
                if p < 2:
                    r = lax.rsqrt(jnp.sum(s * s, axis=1, keepdims=True) + EPS)
                    y = s * r
                    if p == 0:
                        dyv = dyv * (HD ** -0.5)
                    ds = r * (dyv - y * jnp.sum(dyv * y, axis=1, keepdims=True))
                else:
                    ds = dyv
                dc_ref[:, sl] = ds * sg * (1.0 + cc * (1.0 - sg))
        lane = lax.broadcasted_iota(jnp.int32, (tt, 128), 1)
        dg = jnp.zeros((tt, 128), F32)
        dbeta = jnp.zeros((tt, 128), F32)
        for h in range(HEADS):
            dg = jnp.where(lane == h, dgb_ref[h], dg)
            dbeta = jnp.where(lane == HEADS + h, dbb_ref[h], dbeta)
        r64 = lax.broadcasted_iota(jnp.int32, (tt, 1), 0) & (CHUNK - 1)
        k = 1
        while k < CHUNK:
            dg = dg + jnp.where(r64 < CHUNK - k, pltpu.roll(dg, tt - k, 0), 0.0)
            k *= 2
        ab = ab_ref[...]
        e = jnp.exp(alog_ref[...])
        xx = ab + dtb_ref[...]
        g = -e * _softplus(xx)
        da = jnp.where(lane < HEADS, dg * (-e) * _sigmoid(xx), 0.0)
        pa = jnp.sum(jnp.where(lane < HEADS, dg * g, 0.0), axis=0, keepdims=True)
        pd = jnp.sum(da, axis=0, keepdims=True)

        @pl.when(t == 0)
        def _():
            dalog_ref[...] = pa
            ddtb_ref[...] = pd

        @pl.when(t > 0)
        def _():
            dalog_ref[...] += pa
            ddtb_ref[...] += pd

        sb = _sigmoid(ab)
        dab_ref[...] = jnp.where(lane < HEADS, da, dbeta * sb * (1.0 - sb)).astype(BF16)

    vec = pl.BlockSpec((1, 128), lambda t: (0, 0))
    hb = pl.BlockSpec((HEADS, tt, HD), lambda t: (0, t, 0))
    return pl.pallas_call(
        body, name="gdn_pre_bwd", grid=(T // tt,),
        in_specs=[hb, hb, hb, pl.BlockSpec((tt, 3 * GDN_W), lambda t: (t, 0)), hb, hb,
                  pl.BlockSpec((tt, 128), lambda t: (t, C_AB // 128)), vec, vec],
        out_specs=[pl.BlockSpec((tt, 3 * GDN_W), lambda t: (t, 0)), pl.BlockSpec((tt, 128), lambda t: (t, 0)),
                   vec, vec],
        out_shape=[jax.ShapeDtypeStruct((T, 3 * GDN_W), F32), jax.ShapeDtypeStruct((T, 128), BF16),
                   jax.ShapeDtypeStruct((1, 128), F32), jax.ShapeDtypeStruct((1, 128), F32)],
        compiler_params=_cp(("arbitrary",)))(dq, dk, dv, cpre, dbb, dgb, proj, alog, dtb)


def _tri_inv(a):
    nb = a.shape[0]
    ri = lax.broadcasted_iota(jnp.int32, (nb, CHUNK, CHUNK), 1)
    ci = lax.broadcasted_iota(jnp.int32, (nb, CHUNK, CHUNK), 2)
    n = -a
    p = jnp.where(ri == ci, 1.0, 0.0) + n
    for _ in range(5):
        n = _bmm(n, n, 2, 1, HI, False)
        p = p + _bmm(p, n, 2, 1, HI, False)
    return p


def _gdn_chunk_common(q, k, v, bb3, gb3, need_t):
    nb = q.shape[0]
    beta = bb3[:, :, 0:1]
    gcol = gb3[:, :, 0:1]
    bcol = bb3[:, :, :CHUNK]
    gcm = gb3[:, :, :CHUNK]
    oh = jnp.where(lax.broadcasted_iota(jnp.int32, (nb, CHUNK, HD), 2) == 0, 1.0, 0.0)
    grow = _bmm(oh, gb3, 2, 2, HI, False)
    ri = lax.broadcasted_iota(jnp.int32, (nb, CHUNK, CHUNK), 1)
    ci = lax.broadcasted_iota(jnp.int32, (nb, CHUNK, CHUNK), 2)
    tril, stl = ri >= ci, ri > ci
    dg = gcm - grow
    dec = jnp.where(tril, jnp.exp(jnp.where(tril, dg, 0.0)), 0.0)
    kk = _bmm(k, k, 2, 2)
    qk = _bmm(q, k, 2, 2)
    a = jnp.where(stl, bcol * kk * dec, 0.0)
    tm = _tri_inv(a)
    gam = jnp.exp(gcol)
    glast = gb3[:, CHUNK - 1:CHUNK, 0:1]
    egl = jnp.exp(glast)
    rw = k * (beta * gam)
    ru = v * beta
    wu = _bmm(tm, jnp.concatenate([rw, ru], axis=2), 2, 1, HI, False)
    kdf = jnp.exp(glast - gcol)
    out = dict(beta=beta, bcol=bcol, tril=tril, stl=stl, dec=dec, kk=kk, qk=qk, tm=tm, gam=gam, egl=egl,
               rw=rw, wu=wu, at=qk * dec, qd=q * gam, kdf=kdf, kd=k * kdf)
    if need_t:
        brow = _bmm(oh, bb3, 2, 2, HI, False)
        triu, stu = ri <= ci, ri < ci
        dect = jnp.where(triu, jnp.exp(jnp.where(triu, -dg, 0.0)), 0.0)
        qkt = _bmm(k, q, 2, 2)
        at_t = jnp.where(stu, brow * kk * dect, 0.0)
        out.update(brow=brow, triu=triu, stu=stu, dect=dect, qkt=qkt, tmt=_tri_inv(at_t), att=qkt * dect)
    return out


def _gdn_rows(T):
    return min(T, 512)


def _gdn_fwd(qkv, bb, gb, proj, nw):
    T = proj.shape[0]
    R = _gdn_rows(T)
    nb = R // CHUNK

    def body(q_ref, k_ref, v_ref, bb_ref, gb_ref, z_ref, nw_ref, y_ref, st_ref, s_ref, w_s, u_s, at_s, qd_s, kd_s):
        t = pl.program_id(1)

        @pl.when(t == 0)
        def _():
            s_ref[...] = jnp.zeros_like(s_ref)

        sh = (nb, CHUNK, HD)
        q, k, v = q_ref[...].reshape(sh), k_ref[...].reshape(sh), v_ref[...].reshape(sh)
        c = _gdn_chunk_common(q, k, v, bb_ref[...].reshape(sh), gb_ref[...].reshape(sh), False)
        w_s[...] = c["wu"][:, :, :HD]
        u_s[...] = c["wu"][:, :, HD:]
        at_s[...] = c["at"]
        qd_s[...] = c["qd"]
        kd_s[...] = c["kd"]
        egl = c["egl"]
        nwv = nw_ref[...]
        for n in range(nb):
            s = s_ref[...]
            st_ref[n] = s
            vn = u_s[n] - _mm(w_s[n], s)
            o = _mm(qd_s[n], s) + _mm(at_s[n], vn)
            s_ref[...] = s * egl[n] + _mm(kd_s[n], vn, 0, 0)
            rows = slice(n * CHUNK, (n + 1) * CHUNK)
            zz = z_ref[rows, :]
            on = o * lax.rsqrt(jnp.mean(o * o, axis=1, keepdims=True) + EPS)
            y_ref[rows, :] = (on * nwv * (zz * _sigmoid(zz))).astype(BF16)

    def hm(p):
        return pl.BlockSpec((None, None, R, HD), lambda h, t: (p, h, t, 0))

    hb = pl.BlockSpec((None, R, HD), lambda h, t: (h, t, 0))
    cs = pltpu.VMEM((nb, CHUNK, HD), F32)
    return pl.pallas_call(
        body, name="gdn_fwd", grid=(HEADS, T // R),
        in_specs=[hm(0), hm(1), hm(2), hb, hb, pl.BlockSpec((R, HD), lambda h, t: (t, C_Z // HD + h)),
                  pl.BlockSpec((1, HD), lambda h, t: (0, 0))],
        out_specs=[pl.BlockSpec((R, HD), lambda h, t: (t, h)),
                   pl.BlockSpec((None, nb, HD, HD), lambda h, t: (h, t, 0, 0))],
        out_shape=[jax.ShapeDtypeStruct((T, GDN_W), BF16), jax.ShapeDtypeStruct((HEADS, T // CHUNK, HD, HD), F32)],
        scratch_shapes=[pltpu.VMEM((HD, HD), F32), cs, cs, pltpu.VMEM((nb, CHUNK, CHUNK), F32), cs, cs],
        compiler_params=_cp(("parallel", "arbitrary")))(qkv, qkv, qkv, bb, gb, proj, nw)


def _gdn_bwd(qkv, bb, gb, proj, nw, states, dmixed):
    T = proj.shape[0]
    R = _gdn_rows(T)
    nb = R // CHUNK
    ntb = T // R

    def body(q_ref, k_ref, v_ref, bb_ref, gb_ref, z_ref, nw_ref, st_ref, dy_ref,
             dq_ref, dk_ref, dv_ref, dbb_ref, dgb_ref, dz_ref, dnw_ref,
             ds_ref, att_s, do_s, kd_s, vn_s, qd_s, w_s, dvn_s, dkd_s, dgl_s):
        hh = pl.program_id(0)
        t = pl.program_id(1)

        @pl.when(t == 0)
        def _():
            ds_ref[...] = jnp.zeros_like(ds_ref)

        @pl.when((t == 0) & (hh == 0))
        def _():
            dnw_ref[...] = jnp.zeros_like(dnw_ref)

        sh = (nb, CHUNK, HD)
        q, k, v = q_ref[...].reshape(sh), k_ref[...].reshape(sh), v_ref[...].reshape(sh)
        c = _gdn_chunk_common(q, k, v, bb_ref[...].reshape(sh), gb_ref[...].reshape(sh), True)
        w, u = c["wu"][:, :, :HD], c["wu"][:, :, HD:]
        sall = st_ref[...]
        vn = u - _bmm(w, sall, 2, 1)
        o = _bmm(c["qd"], sall, 2, 1) + _bmm(c["at"], vn, 2, 1)
        z = z_ref[...].reshape(sh)
        dy = dy_ref[...].reshape(sh)
        nwv = nw_ref[...].reshape(1, 1, HD)
        rs = lax.rsqrt(jnp.mean(o * o, axis=2, keepdims=True) + EPS)
        on = o * rs
        sg = _sigmoid(z)
        sz = z * sg
        dnw_ref[...] += jnp.sum(jnp.sum(dy * on * sz, axis=0), axis=0, keepdims=True)
        dz_ref[...] = (dy * on * nwv * (sg * (1.0 + z * (1.0 - sg)))).reshape(R, HD).astype(BF16)
        don = dy * nwv * sz
        do = rs * (don - on * jnp.mean(don * on, axis=2, keepdims=True))
        dqd = _bmm(do, sall, 2, 2)
        dat = jnp.where(c["tril"], _bmm(do, vn, 2, 2), 0.0)
        datt = jnp.where(c["triu"], _bmm(vn, do, 2, 2), 0.0)
        att_s[...] = c["att"]
        do_s[...] = do
        kd_s[...] = c["kd"]
        vn_s[...] = vn
        qd_s[...] = c["qd"]
        w_s[...] = w
        egl = c["egl"]
        for n in reversed(range(nb)):
            dso = ds_ref[...]
            dvn_n = _mm(att_s[n], do_s[n]) + _mm(kd_s[n], dso)
            dkd_s[n] = _mm(vn_s[n], dso, 1, 1)
            dgl = egl[n] * jnp.sum(jnp.sum(st_ref[n] * dso, axis=1, keepdims=True), axis=0, keepdims=True)
            dgl_s[n] = jnp.broadcast_to(dgl, (8, HD))
            ds_ref[...] = egl[n] * dso + _mm(qd_s[n], do_s[n], 0, 0) - _mm(w_s[n], dvn_n, 0, 0)
            dvn_s[n] = dvn_n
        dvn = dvn_s[...]
        dkd = dkd_s[...]
        dgl = dgl_s[...][:, 0:1, 0:1]
        dw = -_bmm(dvn, sall, 2, 2)
        dr = _bmm(c["tmt"], jnp.concatenate([dw, dvn], axis=2), 2, 1, HI, False)
        drw, dru = dr[:, :, :HD], dr[:, :, HD:]
        wu = c["wu"]
        da = -jnp.where(c["stl"], _bmm(dr, wu, 2, 2, HI, False), 0.0)
        da_t = -jnp.where(c["stu"], _bmm(wu, dr, 2, 2, HI, False), 0.0)
        beta, gam, dec, dect, kk = c["beta"], c["gam"], c["dec"], c["dect"], c["kk"]
        bcol, brow = c["bcol"], c["brow"]
        dbeta = (jnp.sum(da * kk * dec, axis=2, keepdims=True)
                 + jnp.sum(drw * k * gam + dru * v, axis=2, keepdims=True))
        dkk = bcol * da * dec
        dkk_t = brow * da_t * dect
        e = (bcol * da * kk + dat * c["qk"]) * dec
        e_t = (brow * da_t * kk + datt * c["qkt"]) * dect
        kd = c["kd"]
        dq_ref[...] = (_bmm(dat * dec, k, 2, 1) + dqd * gam).reshape(R, HD)
        dk_ref[...] = (_bmm(datt * dect, q, 2, 1) + _bmm(dkk + dkk_t, k, 2, 1) + dkd * c["kdf"]
                       + drw * (beta * gam)).reshape(R, HD)
        dv_ref[...] = (dru * beta).reshape(R, HD)
        skd = jnp.sum(dkd * kd, axis=2, keepdims=True)
        dgc = (jnp.sum(e, axis=2, keepdims=True) - jnp.sum(e_t, axis=2, keepdims=True)
               + jnp.sum(drw * c["rw"] + dqd * c["qd"], axis=2, keepdims=True) - skd)
        tot = jnp.sum(skd, axis=1, keepdims=True) + dgl
        rowi = lax.broadcasted_iota(jnp.int32, (nb, CHUNK, 1), 1)
        dgc = dgc + jnp.where(rowi == CHUNK - 1, tot, 0.0)
        dbb_ref[...] = jnp.broadcast_to(dbeta, sh).reshape(R, HD)
        dgb_ref[...] = jnp.broadcast_to(dgc, sh).reshape(R, HD)

    def rt(t):
        return ntb - 1 - t

    def hm(p):
        return pl.BlockSpec((None, None, R, HD), lambda h, t: (p, h, rt(t), 0))

    hb = pl.BlockSpec((None, R, HD), lambda h, t: (h, rt(t), 0))
    cs = pltpu.VMEM((nb, CHUNK, HD), F32)
    ob = jax.ShapeDtypeStruct((HEADS, T, HD), F32)
    return pl.pallas_call(
        body, name="gdn_bwd", grid=(HEADS, ntb),
        in_specs=[hm(0), hm(1), hm(2), hb, hb, pl.BlockSpec((R, HD), lambda h, t: (rt(t), C_Z // HD + h)),
                  pl.BlockSpec((1, HD), lambda h, t: (0, 0)),
                  pl.BlockSpec((None, nb, HD, HD), lambda h, t: (h, rt(t), 0, 0)),
                  pl.BlockSpec((R, HD), lambda h, t: (rt(t), M_GDN // HD + h))],
        out_specs=[hb, hb, hb, hb, hb, pl.BlockSpec((R, HD), lambda h, t: (rt(t), h)),
                   pl.BlockSpec((1, HD), lambda h, t: (0, 0))],
        out_shape=[ob, ob, ob, ob, ob, jax.ShapeDtypeStruct((T, GDN_W), BF16), jax.ShapeDtypeStruct((1, HD), F32)],
        scratch_shapes=[pltpu.VMEM((HD, HD), F32), pltpu.VMEM((nb, CHUNK, CHUNK), F32), cs, cs, cs, cs, cs, cs, cs,
                        pltpu.VMEM((nb, 8, HD), F32)],
        compiler_params=_cp(("arbitrary", "arbitrary")))(qkv, qkv, qkv, bb, gb, proj, nw, states, dmixed)


def _lru_gates(xc, wa, ba, wx, bx, lam, gpos):
    xb = xc.astype(BF16)
    r = _sigmoid(_mm(xb, wa) + ba)
    i = _sigmoid(_mm(xb, wx) + bx)
    sp = _softplus(-lam)
    log_a = -LRU_C * r * sp
    a = jnp.exp(log_a)
    mult = jnp.where(gpos == 0, 1.0, jnp.sqrt(-_expm1(2.0 * log_a)))
    return r, i, sp, a, mult


def _lru_fwd(xc, proj, wa, ba, wx, bx, lam):
    T = xc.shape[0]
    tt = _tile(T)

    def body(xc_ref, gr_ref, wa_ref, ba_ref, wx_ref, bx_ref, lam_ref, y_ref, h_ref, carry_ref):
        t = pl.program_id(1)

        @pl.when(t == 0)
        def _():
            carry_ref[...] = jnp.zeros_like(carry_ref)

        row = lax.broadcasted_iota(jnp.int32, (tt, 1), 0)
        xcv = xc_ref[...]
        r, i, sp, a, mult = _lru_gates(xcv, wa_ref[...], ba_ref[...], wx_ref[...], bx_ref[...], lam_ref[...],
                                       t * tt + row)
        av, bv = a, mult * i * xcv
        k = 1
        while k < tt:
            a_s = jnp.where(row >= k, pltpu.roll(av, k, 0), 1.0)
            b_s = jnp.where(row >= k, pltpu.roll(bv, k, 0), 0.0)
            bv = bv + av * b_s
            av = av * a_s
            k *= 2
        h = bv + av * carry_ref[0:1, :]
        carry_ref[...] = jnp.broadcast_to(h[tt - 1:tt, :], (8, 128))
        h_ref[...] = h
        y_ref[...] = (h * _gelu(gr_ref[...])).astype(BF16)

    blk = pl.BlockSpec((tt, 128), lambda j, t: (t, j))
    vec = pl.BlockSpec((1, 128), lambda j, t: (0, j))
    mat = pl.BlockSpec((None, 128, 128), lambda j, t: (j, 0, 0))
    return pl.pallas_call(
        body, name="lru_fwd", grid=(LRU_W // 128, T // tt),
        in_specs=[blk, pl.BlockSpec((tt, 128), lambda j, t: (t, C_GR // 128 + j)), mat, vec, mat, vec, vec],
        out_specs=[blk, blk],
        out_shape=[jax.ShapeDtypeStruct((T, LRU_W), BF16), jax.ShapeDtypeStruct((T, LRU_W), F32)],
        scratch_shapes=[pltpu.VMEM((8, 128), F32)],
        compiler_params=_cp(("parallel", "arbitrary")))(xc, proj, wa, ba, wx, bx, lam)


def _lru_bwd(dmixed, xc, proj, hst, wa, ba, wx, bx, lam):
    T = xc.shape[0]
    tt = _tile(T)
    nt = T // tt

    def body(dy_ref, xc_ref, gr_ref, h_ref, hp_ref, wa_ref, ba_ref, wx_ref, bx_ref, lam_ref,
             dxc_ref, dgr_ref, dwa_ref, dwx_ref, dba_ref, dbx_ref, dlam_ref, lc_ref, ac_ref):
        t = pl.program_id(1)
        tr = nt - 1 - t

        @pl.when(t == 0)
        def _():
            lc_ref[...] = jnp.zeros_like(lc_ref)
            ac_ref[...] = jnp.zeros_like(ac_ref)
            dwa_ref[...] = jnp.zeros_like(dwa_ref)
            dwx_ref[...] = jnp.zeros_like(dwx_ref)
            dba_ref[...] = jnp.zeros_like(dba_ref)
            dbx_ref[...] = jnp.zeros_like(dbx_ref)
            dlam_ref[...] = jnp.zeros_like(dlam_ref)

        row = lax.broadcasted_iota(jnp.int32, (tt, 1), 0)
        gpos = tr * tt + row
        xcv = xc_ref[...]
        wav, wxv, lamv = wa_ref[...], wx_ref[...], lam_ref[...]
        r, i, sp, a, mult = _lru_gates(xcv, wav, ba_ref[...], wxv, bx_ref[...], lamv, gpos)
        h = h_ref[...]
        dy = dy_ref[...]
        gg, dgg = _gelu_and_grad(gr_ref[...])
        dgr_ref[...] = (dy * h * dgg).astype(BF16)
        bv = dy * gg
        cv = jnp.where(row < tt - 1, pltpu.roll(a, tt - 1, 0), ac_ref[0:1, :])
        k = 1
        while k < tt:
            c_s = jnp.where(row < tt - k, pltpu.roll(cv, tt - k, 0), 1.0)
            b_s = jnp.where(row < tt - k, pltpu.roll(bv, tt - k, 0), 0.0)
            bv = bv + cv * b_s
            cv = cv * c_s
            k *= 2
        lm = bv + cv * lc_ref[0:1, :]
        lc_ref[...] = jnp.broadcast_to(lm[0:1, :], (8, 128))
        ac_ref[...] = jnp.broadcast_to(a[0:1, :], (8, 128))
        hp = jnp.where(tr == 0, 0.0, hp_ref[...])
        hs = pltpu.roll(jnp.concatenate([hp, h], axis=0), 1, 0)[8:, :]
        da = lm * hs
        dmult = lm * i * xcv
        di = lm * mult * xcv
        dxc = lm * mult * i
        dlog_a = a * da - jnp.where(gpos == 0, 0.0, dmult * a * a / mult)
        dr = dlog_a * (-LRU_C * sp)
        dsp = jnp.sum(dlog_a * (-LRU_C * r), axis=0, keepdims=True)
        dlam_ref[...] += dsp * (-_sigmoid(-lamv))
        dpr = dr * r * (1.0 - r)
        dpi = di * i * (1.0 - i)
        dba_ref[...] += jnp.sum(dpr, axis=0, keepdims=True)
        dbx_ref[...] += jnp.sum(dpi, axis=0, keepdims=True)
        dwa_ref[...] += _mm(xcv, dpr, 0, 0)
        dwx_ref[...] += _mm(xcv, dpi, 0, 0)
        dxc_ref[...] = dxc + _mm(dpr, wav, 1, 1) + _mm(dpi, wxv, 1, 1)

    def rt(t):
        return nt - 1 - t

    blk = pl.BlockSpec((tt, 128), lambda j, t: (rt(t), j))
    vec = pl.BlockSpec((1, 128), lambda j, t: (0, j))
    mat = pl.BlockSpec((None, 128, 128), lambda j, t: (j, 0, 0))
    h8 = tt // 8
    mshape = jax.ShapeDtypeStruct((LRU_W // 128, 128, 128), F32)
    vshape = jax.ShapeDtypeStruct((1, LRU_W), F32)
    return pl.pallas_call(
        body, name="lru_bwd", grid=(LRU_W // 128, nt),
        in_specs=[pl.BlockSpec((tt, 128), lambda j, t: (rt(t), M_LRU // 128 + j)), blk,
                  pl.BlockSpec((tt, 128), lambda j, t: (rt(t), C_GR // 128 + j)), blk,
                  pl.BlockSpec((8, 128), lambda j, t: (jnp.maximum(rt(t) * h8 - 1, 0), j)),
                  mat, vec, mat, vec, vec],
        out_specs=[blk, blk, mat, mat, vec, vec, vec],
        out_shape=[jax.ShapeDtypeStruct((T, LRU_W), F32), jax.ShapeDtypeStruct((T, LRU_W), BF16),
                   mshape, mshape, vshape, vshape, vshape],
        scratch_shapes=[pltpu.VMEM((8, 128), F32), pltpu.VMEM((8, 128), F32)],
        compiler_params=_cp(("parallel", "arbitrary")))(dmixed, xc, proj, hst, hst, wa, ba, wx, bx, lam)


def _ffn_act_fwd(gconv, up):
    T = up.shape[0]
    tt = _tile(T)
    cb = 512
    nc = D_FF // cb

    def body(g_ref, v_ref, o_ref):
        o_ref[...] = (_gelu(g_ref[...]) * v_ref[...]).astype(BF16)

    blk = pl.BlockSpec((tt, cb), lambda t, j: (t, j))
    return pl.pallas_call(
        body, name="ffn_act_fwd", grid=(T // tt, nc),
        in_specs=[blk, pl.BlockSpec((tt, cb), lambda t, j: (t, nc + j))], out_specs=blk,
        out_shape=jax.ShapeDtypeStruct((T, D_FF), BF16), compiler_params=_cp(("parallel", "parallel")))(gconv, up)


def _ffn_act_bwd(dact, gconv, up):
    T = up.shape[0]
    tt = _tile(T)
    cb = 512
    nc = D_FF // cb

    def body(d_ref, g_ref, v_ref, dg_ref, dv_ref):
        gg, dgg = _gelu_and_grad(g_ref[...])
        d = d_ref[...]
        dg_ref[...] = d * v_ref[...] * dgg
        dv_ref[...] = (d * gg).astype(BF16)

    blk = pl.BlockSpec((tt, cb), lambda t, j: (t, j))
    return pl.pallas_call(
        body, name="ffn_act_bwd", grid=(T // tt, nc),
        in_specs=[blk, blk, pl.BlockSpec((tt, cb), lambda t, j: (t, nc + j))], out_specs=[blk, blk],
        out_shape=[jax.ShapeDtypeStruct((T, D_FF), F32), jax.ShapeDtypeStruct((T, D_FF), BF16)],
        compiler_params=_cp(("parallel", "parallel")))(dact, gconv, up)


def _row_tile(rows, cap):
    best = 8
    for r in range(8, min(rows, cap) + 1, 8):
        if rows % r == 0:
            best = r
    return best


def _adamw(parts, w, m, v, rt, name):
    P, R, C = parts.shape

    def body(p_ref, w_ref, m_ref, v_ref, g_ref, d_ref, mo_ref, vo_ref):
        g = p_ref[0].astype(F32)
        for i in range(1, P):
            g = g + p_ref[i].astype(F32)
        wv = w_ref[...]
        mn = ADAM_B1 * m_ref[...] + (1.0 - ADAM_B1) * g
        vn = ADAM_B2 * v_ref[...] + (1.0 - ADAM_B2) * (g * g)
        m_hat = mn / (1.0 - ADAM_B1 ** ADAM_STEP)
        v_hat = vn / (1.0 - ADAM_B2 ** ADAM_STEP)
        g_ref[...] = g
        d_ref[...] = -ADAM_LR * (m_hat / (jnp.sqrt(v_hat) + ADAM_EPS) + ADAM_WD * wv)
        mo_ref[...] = mn
        vo_ref[...] = vn

    blk = pl.BlockSpec((rt, C), lambda r: (r, 0))
    sh = jax.ShapeDtypeStruct((R, C), F32)
    return pl.pallas_call(
        body, name=name, grid=(R // rt,),
        in_specs=[pl.BlockSpec((P, rt, C), lambda r: (0, r, 0)), blk, blk, blk],
        out_specs=[blk, blk, blk, blk], out_shape=[sh, sh, sh, sh],
        compiler_params=_cp(("parallel",)))(parts, w, m, v)


def _peer(k):
    x, y, c = lax.axis_index("x"), lax.axis_index("y"), lax.axis_index("c")
    px = 1 - x if k & 4 else x
    py = 1 - y if k & 2 else y
    pc = 1 - c if k & 1 else c
    return (px, py, pc), 4 * px + 2 * py + pc


def _all_gather(x, name):
    R, C = x.shape

    def body(x_ref, o_ref, send_sems, recv_sems, local_sem):
        me = 4 * lax.axis_index("x") + 2 * lax.axis_index("y") + lax.axis_index("c")
        mine = pltpu.make_async_copy(x_ref, o_ref.at[me], local_sem)
        mine.start()
        sends = []
        for k in range(1, N_DEV):
            dev, _ = _peer(k)
            cp = pltpu.make_async_remote_copy(src_ref=x_ref, dst_ref=o_ref.at[me], send_sem=send_sems.at[k - 1],
                                              recv_sem=recv_sems.at[k - 1], device_id=dev, device_id_type=MESH_IDS)
            cp.start()
            sends.append(cp)
        for k in range(1, N_DEV):
            dev, idx = _peer(k)
            pltpu.make_async_remote_copy(src_ref=x_ref, dst_ref=o_ref.at[idx], send_sem=send_sems.at[k - 1],
                                         recv_sem=recv_sems.at[k - 1], device_id=dev,
                                         device_id_type=MESH_IDS).wait_recv()
        for cp in sends:
            cp.wait_send()
        mine.wait()

    return pl.pallas_call(
        body, name=name, in_specs=[pl.BlockSpec(memory_space=pl.ANY)], out_specs=pl.BlockSpec(memory_space=pl.ANY),
        out_shape=jax.ShapeDtypeStruct((N_DEV, R, C), x.dtype),
        scratch_shapes=[pltpu.SemaphoreType.DMA((N_DEV - 1,)), pltpu.SemaphoreType.DMA((N_DEV - 1,)),
                        pltpu.SemaphoreType.DMA],
        compiler_params=pltpu.CompilerParams(has_side_effects=True))(x)


def _all_to_all(x, name):
    _, R, C = x.shape

    def body(x_ref, o_ref, send_sems, recv_sems, local_sem):
        me = 4 * lax.axis_index("x") + 2 * lax.axis_index("y") + lax.axis_index("c")
        mine = pltpu.make_async_copy(x_ref.at[me], o_ref.at[me], local_sem)
        mine.start()
        sends = []
        for k in range(1, N_DEV):
            dev, idx = _peer(k)
            cp = pltpu.make_async_remote_copy(src_ref=x_ref.at[idx], dst_ref=o_ref.at[me],
                                              send_sem=send_sems.at[k - 1], recv_sem=recv_sems.at[k - 1],
                                              device_id=dev, device_id_type=MESH_IDS)
            cp.start()
            sends.append(cp)
        for k in range(1, N_DEV):
            dev, idx = _peer(k)
            pltpu.make_async_remote_copy(src_ref=x_ref.at[idx], dst_ref=o_ref.at[idx], send_sem=send_sems.at[k - 1],
                                         recv_sem=recv_sems.at[k - 1], device_id=dev,
                                         device_id_type=MESH_IDS).wait_recv()
        for cp in sends:
            cp.wait_send()
        mine.wait()

    return pl.pallas_call(
        body, name=name, in_specs=[pl.BlockSpec(memory_space=pl.ANY)], out_specs=pl.BlockSpec(memory_space=pl.ANY),
        out_shape=jax.ShapeDtypeStruct(x.shape, x.dtype),
        scratch_shapes=[pltpu.SemaphoreType.DMA((N_DEV - 1,)), pltpu.SemaphoreType.DMA((N_DEV - 1,)),
                        pltpu.SemaphoreType.DMA],
        compiler_params=pltpu.CompilerParams(has_side_effects=True))(x)


def _layer_fwd(x, W, l):
    T = x.shape[0]
    n = f"l{l}_"
    h1 = _norm_fwd(x, W["norm1"], n + "norm1_fwd")
    proj = _mm_nn(h1, W["win"], F32, n + "mm_in", tn_c=(768,))
    y_pool = _pool_fwd(proj, W["pool_w"], W["pool_b"], W["pool_s"])
    cpre = _conv_fwd(proj, C_QKV, 3 * GDN_W, W["gconv_w"], None, 256, n + "gdn_conv_fwd")
    qkv, bb, gb = _gdn_pre_fwd(cpre, proj, W["alog"], W["dtb"])
    y_gdn, states = _gdn_fwd(qkv, bb, gb, proj, W["gnorm"])
    xc = _conv_fwd(proj, C_XR, LRU_W, W["lconv_w"], W["lconv_b"], 128, n + "lru_conv_fwd")
    y_lru, hst = _lru_fwd(xc, proj, W["wa"], W["ba"], W["wx"], W["bx"], W["lam"])
    mixed = jnp.concatenate([y_pool, y_gdn, y_lru], axis=1)
    x1 = _mm_nn(mixed, W["wout"], F32, n + "mm_out", add=x)
    h2 = _norm_fwd(x1, W["norm2"], n + "norm2_fwd")
    up = _mm_up(h2, W["wup"], n + "mm_up")
    gconv = _conv_fwd(up, 0, D_FF, W["fconv_w"], None, 512, n + "ffn_conv_fwd")
    act = _ffn_act_fwd(gconv, up)
    x2 = _mm_nn(act, W["wdown"], F32, n + "mm_down", add=x1)
    saved = dict(x=x, h1=h1, proj=proj, cpre=cpre, qkv=qkv, bb=bb, gb=gb, states=states, xc=xc, hst=hst,
                 mixed=mixed, x1=x1, h2=h2, up=up, gconv=gconv, act=act)
    return x2, saved


def _layer_bwd(dx2, W, S, l):
    T = dx2.shape[0]
    n = f"l{l}_"
    dact = _mm_nt(dx2, W["wdown"], F32, n + "mm_down_dx", tk_c=(2048,))
    g_wdown = _mm_tn(S["act"], dx2, BF16, n + "mm_down_dw")
    dgconv, dval = _ffn_act_bwd(dact, S["gconv"], S["up"])
    dgate, g_fconv = _conv_bwd(dgconv, S["up"], 0, W["fconv_w"], 512, n + "ffn_conv_bwd")
    dup = jnp.concatenate([dgate, dval], axis=1)
    ns = W["wup"].shape[2]
    dh2 = _mm_up_t(dup, W["wup"], n + "mm_up_dx")
    g_wup = _mm_dup(S["h2"], dup, ns, n + "mm_up_dw")
    dx1, g_norm2 = _norm_bwd(S["x1"], W["norm2"], dh2, dx2, n + "norm2_bwd")
    dmixed = _mm_nt(dx1, W["wout"], F32, n + "mm_out_dx", tk_c=(2048,))
    g_wout = _mm_tn(S["mixed"], dx1, BF16, n + "mm_out_dw")
    proj = S["proj"]
    du, g_pool_w, g_pool_b, g_pool_s = _pool_bwd(dmixed, proj, W["pool_w"], W["pool_b"], W["pool_s"])
    dq, dk, dv, dbb, dgb, dz, g_gnorm = _gdn_bwd(S["qkv"], S["bb"], S["gb"], proj, W["gnorm"], S["states"], dmixed)
    dc, dab, g_alog, g_dtb = _gdn_pre_bwd(dq, dk, dv, S["cpre"], dbb, dgb, proj, W["alog"], W["dtb"])
    dqkv, g_gconv = _conv_bwd(dc, proj, C_QKV, W["gconv_w"], 256, n + "gdn_conv_bwd")
    dxc, dgr, g_wa, g_wx, g_ba, g_bx, g_lam = _lru_bwd(dmixed, S["xc"], proj, S["hst"], W["wa"], W["ba"], W["wx"],
                                                        W["bx"], W["lam"])
    dxr, g_lconv, g_lconv_b = _conv_bwd(dxc, proj, C_XR, W["lconv_w"], 128, n + "lru_conv_bwd", want_db=True)
    dproj = jnp.concatenate([du, dqkv, dz, dab, dxr, dgr, jnp.zeros((T, PCOLS - C_GR - LRU_W), BF16)], axis=1)
    dh1 = _mm_nt(dproj, W["win"], F32, n + "mm_in_dx", tk_c=(768,))
    g_win = _mm_tn(S["h1"], dproj, BF16, n + "mm_in_dw", tn_c=(768,))
    dx, g_norm1 = _norm_bwd(S["x"], W["norm1"], dh1, dx1, n + "norm1_bwd")
    big = dict(w_in=g_win, w_out=g_wout, ffn_up=g_wup, ffn_down=g_wdown)
    small = dict(norm1_w=g_norm1[0], pool_w=g_pool_w, pool_b=g_pool_b.reshape(4, 128), pool_scale=g_pool_s[0],
                 gdn_conv_w=g_gconv, gdn_a_log=g_alog[0, :HEADS], gdn_dt_bias=g_dtb[0, :HEADS],
                 gdn_norm_w=g_gnorm[0], lru_conv_w=g_lconv, lru_conv_b=g_lconv_b[0], lru_wa=g_wa, lru_ba=g_ba[0],
                 lru_wx=g_wx, lru_bx=g_bx[0], lru_lambda=g_lam[0], norm2_w=g_norm2[0], ffn_conv_w=g_fconv)
    return dx, big, small


def _pad_lane(v):
    return jnp.pad(v, (0, 128 - v.shape[0])).reshape(1, 128)


def _layer_weights(l, big, P, conv_full):
    return dict(
        win=big["w_in"], wout=big["w_out"], wup=big["ffn_up"], wdown=big["ffn_down"],
        norm1=P["norm1_w"][l].reshape(1, D_MODEL), norm2=P["norm2_w"][l].reshape(1, D_MODEL),
        pool_w=P["pool_w"][l], pool_b=P["pool_b"][l].reshape(1, POOL_W), pool_s=P["pool_scale"][l].reshape(1, POOL_W),
        gconv_w=conv_full["gdn_conv_w"][l], alog=_pad_lane(P["gdn_a_log"][l]), dtb=_pad_lane(P["gdn_dt_bias"][l]),
        gnorm=P["gdn_norm_w"][l].reshape(1, HD),
        lconv_w=conv_full["lru_conv_w"][l], lconv_b=P["lru_conv_b"][l].reshape(1, LRU_W),
        wa=P["lru_wa"][l], ba=P["lru_ba"][l].reshape(1, LRU_W), wx=P["lru_wx"][l],
        bx=P["lru_bx"][l].reshape(1, LRU_W), lam=P["lru_lambda"][l].reshape(1, LRU_W),
        fconv_w=conv_full["ffn_conv_w"][l])


def _local_step(x, target, Ws, final_norm_w):
    saved = []
    for l in range(DEPTH):
        x, s = _layer_fwd(x, Ws[l], l)
        saved.append(s)
    loss, dx, g_final = _loss_head(x, final_norm_w.reshape(1, D_MODEL), target)
    bigs, smalls = [None] * DEPTH, [None] * DEPTH
    for l in reversed(range(DEPTH)):
        dx, bigs[l], smalls[l] = _layer_bwd(dx, Ws[l], saved[l], l)
    return loss, dx, g_final[0], bigs, smalls


SMALL_REPL = ("norm1_w", "pool_w", "pool_b", "pool_scale", "gdn_a_log", "gdn_dt_bias", "gdn_norm_w", "lru_conv_b",
              "lru_wa", "lru_ba", "lru_wx", "lru_bx", "lru_lambda", "norm2_w", "final_norm_w")
SMALL_SHARD = ("gdn_conv_w", "lru_conv_w", "ffn_conv_w")
BIG = ("w_in", "w_out", "ffn_up", "ffn_down")
WEIGHTS = ("norm1_w", "w_in", "pool_w", "pool_b", "pool_scale", "gdn_conv_w", "gdn_a_log", "gdn_dt_bias",
           "gdn_norm_w", "lru_conv_w", "lru_conv_b", "lru_wa", "lru_ba", "lru_wx", "lru_bx", "lru_lambda", "w_out",
           "norm2_w", "ffn_up", "ffn_conv_w", "ffn_down", "final_norm_w")
SEG = 1024


def _pack(arrs):
    pieces, table, off = [], [], 0
    for a in arrs:
        n = a.size
        npad = -(-n // SEG) * SEG
        pieces.append(jnp.pad(a.reshape(-1).astype(F32), (0, npad - n)))
        table.append((off, n, a.shape))
        off += npad
    return jnp.concatenate(pieces).reshape(off // 128, 128), table


def _unpack(buf, table):
    flat = buf.reshape(-1)
    return [flat[off:off + n].reshape(shape) for off, n, shape in table]


def _pad_in(w):
    z1 = jnp.zeros(w.shape[:-1] + (C_XR - AB_ORIG_END,), w.dtype)
    z2 = jnp.zeros(w.shape[:-1] + (PCOLS - C_GR - LRU_W,), w.dtype)
    return jnp.concatenate([w[..., :AB_ORIG_END], z1, w[..., AB_ORIG_END:], z2], axis=-1)


def _unpad_in(w):
    return jnp.concatenate([w[..., :AB_ORIG_END], w[..., C_XR:C_GR + LRU_W]], axis=-1)


def kernel(x, norm1_w, w_in, pool_w, pool_b, pool_scale, gdn_conv_w, gdn_a_log, gdn_dt_bias, gdn_norm_w, lru_conv_w, lru_conv_b, lru_wa, lru_ba, lru_wx, lru_bx, lru_lambda, w_out, norm2_w, ffn_up, ffn_conv_w, ffn_down, final_norm_w, loss_target, m_norm1_w, m_w_in, m_pool_w, m_pool_b, m_pool_scale, m_gdn_conv_w, m_gdn_a_log, m_gdn_dt_bias, m_gdn_norm_w, m_lru_conv_w, m_lru_conv_b, m_lru_wa, m_lru_ba, m_lru_wx, m_lru_bx, m_lru_lambda, m_w_out, m_norm2_w, m_ffn_up, m_ffn_conv_w, m_ffn_down, m_final_norm_w, v_norm1_w, v_w_in, v_pool_w, v_pool_b, v_pool_scale, v_gdn_conv_w, v_gdn_a_log, v_gdn_dt_bias, v_gdn_norm_w, v_lru_conv_w, v_lru_conv_b, v_lru_wa, v_lru_ba, v_lru_wx, v_lru_bx, v_lru_lambda, v_w_out, v_norm2_w, v_ffn_up, v_ffn_conv_w, v_ffn_down, v_final_norm_w):
    loc = dict(locals())
    Wp = {n: loc[n] for n in WEIGHTS}
    Mp = {n: loc["m_" + n] for n in WEIGHTS}
    Vp = {n: loc["v_" + n] for n in WEIGHTS}
    me = 4 * lax.axis_index("x") + 2 * lax.axis_index("y") + lax.axis_index("c")
    T = x.shape[1]

    def shard2d(d, name, l):
        a = d[name][l]
        return _pad_in(a) if name == "w_in" else a

    bigs = []
    for l in range(DEPTH):
        g = {}
        for name in BIG:
            full = _all_gather(shard2d(Wp, name, l).astype(BF16), f"ag_{name}_{l}")
            g[name] = full if name == "ffn_up" else full.reshape(-1, full.shape[2])
        bigs.append(g)

    cbuf, ctable = _pack([Wp[n] for n in SMALL_SHARD])
    call = _all_gather(cbuf, "ag_conv_w")
    parts = [_unpack(call[i], ctable) for i in range(N_DEV)]
    conv_full = {n: jnp.concatenate([parts[i][j] for i in range(N_DEV)], axis=-1) for j, n in enumerate(SMALL_SHARD)}

    Ws = [_layer_weights(l, bigs[l], Wp, conv_full) for l in range(DEPTH)]
    loss, dx, g_final, gbig, gsmall = _local_step(x[0], loss_target[0], Ws, final_norm_w)

    out_g, out_d, out_m, out_v = {}, {}, {}, {}

    rts = dict(w_in=64, w_out=128, ffn_up=256, ffn_down=128)
    for name in BIG:
        res = []
        for l in range(DEPTH):
            g = gbig[l][name]
            if name != "ffn_up":
                g = g.reshape(N_DEV, g.shape[0] // N_DEV, g.shape[1])
            recv = _all_to_all(g, f"a2a_{name}_{l}")
            res.append(_adamw(recv, shard2d(Wp, name, l), shard2d(Mp, name, l), shard2d(Vp, name, l), rts[name],
                              f"adamw_{name}_{l}"))
        for i, dst in enumerate((out_g, out_d, out_m, out_v)):
            a = jnp.stack([res[l][i] for l in range(DEPTH)])
            dst[name] = _unpad_in(a) if name == "w_in" else a

    small_names = SMALL_REPL + SMALL_SHARD

    def small_grad(nm):
        if nm == "final_norm_w":
            return g_final
        return jnp.stack([gsmall[l][nm] for l in range(DEPTH)])

    gbuf, gtable = _pack([small_grad(nm) for nm in small_names] + [loss[0, :1]])

    def small_state(d):
        arrs = [d[nm] for nm in SMALL_REPL] + [jnp.zeros(small_grad(nm).shape, F32) for nm in SMALL_SHARD]
        return _pack(arrs + [jnp.zeros((1,), F32)])[0]

    gall = _all_gather(gbuf, "ag_small_grads")
    rs = gbuf.shape[0]
    res = _adamw(gall, small_state(Wp), small_state(Mp), small_state(Vp), _row_tile(rs, 512), "adamw_small")
    unp = [_unpack(r, gtable) for r in res]
    for j, nm in enumerate(SMALL_REPL):
        for i, dst in enumerate((out_g, out_d, out_m, out_v)):
            dst[nm] = unp[i][j]
    loss_total = unp[0][len(small_names)][0]

    nrep = len(SMALL_REPL)
    gsh = []
    for j, nm in enumerate(SMALL_SHARD):
        full = unp[0][nrep + j]
        width = Wp[nm].shape[-1]
        gsh.append(lax.dynamic_slice_in_dim(full, me * width, width, axis=full.ndim - 1))
    sbuf, stable = _pack(gsh)
    res = _adamw(sbuf[None], _pack([Wp[n] for n in SMALL_SHARD])[0], _pack([Mp[n] for n in SMALL_SHARD])[0],
                 _pack([Vp[n] for n in SMALL_SHARD])[0], sbuf.shape[0], "adamw_conv_w")
    unp2 = [_unpack(r, stable) for r in res]
    for j, nm in enumerate(SMALL_SHARD):
        for i, dst in enumerate((out_g, out_d, out_m, out_v)):
            dst[nm] = unp2[i][j]

    return (loss_total, dx[None], *[out_g[n] for n in WEIGHTS], *[out_d[n] for n in WEIGHTS],
            *[out_m[n] for n in WEIGHTS], *[out_v[n] for n in WEIGHTS])
```

```python
import functools

import jax
import jax.numpy as jnp
from jax import lax
from jax.experimental import pallas as pl
from jax.experimental.pallas import tpu as pltpu

F32 = jnp.float32
BF16 = jnp.bfloat16
HI = lax.Precision.HIGHEST
MESH_IDS = pl.DeviceIdType.MESH

N_DEV = 8
D_MODEL = 2048
DEPTH = 2
POOL_WINDOWS = (2, 4, 8, 16)
POOL_W = 512
HEADS = 6
HD = 128
GDN_W = HEADS * HD
CHUNK = 64
LRU_W = 768
LRU_C = 8.0
D_FF = 3 * D_MODEL
EPS = 1e-6
IN_COLS = 5132
PCOLS = 5376
C_QKV, C_Z, C_AB, C_XR, C_GR = 512, 2816, 3584, 3712, 4480
AB_ORIG_END = 3596
M_GDN, M_LRU = 512, 1280

ADAM_LR, ADAM_B1, ADAM_B2, ADAM_EPS, ADAM_WD, ADAM_STEP = 0.001, 0.9, 0.999, 1e-08, 0.01, 10

VMEM_LIMIT = 56 * 1024 * 1024


def _cp(sem):
    return pltpu.CompilerParams(dimension_semantics=sem, vmem_limit_bytes=VMEM_LIMIT)


def _mm(a, b, ca=1, cb=0, prec=None, cast=True):
    if cast:
        a = a.astype(BF16)
        b = b.astype(BF16)
    return lax.dot_general(a, b, (((ca,), (cb,)), ((), ())), preferred_element_type=F32, precision=prec)


def _bmm(a, b, ca=2, cb=1, prec=None, cast=True):
    if cast:
        a = a.astype(BF16)
        b = b.astype(BF16)
    return lax.dot_general(a, b, (((ca,), (cb,)), ((0,), (0,))), preferred_element_type=F32, precision=prec)


def _sigmoid(x):
    return 1.0 / (1.0 + jnp.exp(-x))


def _log1p(e):
    u = 1.0 + e
    return jnp.where(u == 1.0, e, jnp.log(u) * e / jnp.where(u == 1.0, 1.0, u - 1.0))


def _softplus(x):
    return jnp.maximum(x, 0.0) + _log1p(jnp.exp(-jnp.abs(x)))


def _expm1(x):
    u = jnp.exp(x)
    um = u - 1.0
    safe = jnp.where((u == 1.0) | (um == -1.0), 1.0, jnp.log(u))
    return jnp.where(u == 1.0, x, jnp.where(um == -1.0, -1.0, um * x / safe))


_G0 = 0.7978845608028654
_G1 = 0.044715


def _gelu(x):
    return 0.5 * x * (1.0 + jnp.tanh(_G0 * (x + _G1 * x * x * x)))


def _gelu_and_grad(x):
    th = jnp.tanh(_G0 * (x + _G1 * x * x * x))
    g = 0.5 * x * (1.0 + th)
    dg = 0.5 * (1.0 + th) + 0.5 * x * (1.0 - th * th) * _G0 * (1.0 + 3.0 * _G1 * x * x)
    return g, dg


def _tile(T):
    return min(T, 512)


def _matmul(a, b, *, grid, a_spec, b_spec, out_shape, out_spec, dims, acc_shape, name, add=None, add_spec=None):
    nk = grid[2]
    has_add = add is not None

    def body(*refs):
        if has_add:
            a_ref, b_ref, add_ref, o_ref, acc_ref = refs
        else:
            a_ref, b_ref, o_ref, acc_ref = refs
            add_ref = None
        k = pl.program_id(2)
        p = lax.dot_general(a_ref[...].astype(BF16), b_ref[...].astype(BF16), (dims, ((), ())),
                            preferred_element_type=F32)

        def finish(r):
            if has_add:
                r = r + add_ref[...]
            o_ref[...] = r.astype(o_ref.dtype)

        if nk == 1:
            finish(p)
        else:
            @pl.when(k == 0)
            def _():
                acc_ref[...] = p

            @pl.when(k > 0)
            def _():
                acc_ref[...] += p

            @pl.when(k == nk - 1)
            def _():
                finish(acc_ref[...])

    in_specs = [a_spec, b_spec] + ([add_spec] if has_add else [])
    args = (a, b) + ((add,) if has_add else ())
    return pl.pallas_call(
        body, name=name, grid=grid, in_specs=in_specs, out_specs=out_spec, out_shape=out_shape,
        scratch_shapes=[pltpu.VMEM(acc_shape, F32)],
        compiler_params=_cp(("parallel", "parallel", "arbitrary")),
    )(*args)


def _pick(n, cands):
    for c in cands:
        if n % c == 0:
            return c
    raise ValueError(f"no tile for {n}")


def _mm_nn(a, b, out_dtype, name, add=None, tn_c=(1024, 768, 512)):
    M, K = a.shape
    N = b.shape[1]
    tm = _pick(M, (1024, 512, 256))
    tn = _pick(N, tn_c)
    tk = _pick(K, (2048, 1536, 1024, 512, 256))
    return _matmul(
        a, b, grid=(M // tm, N // tn, K // tk),
        a_spec=pl.BlockSpec((tm, tk), lambda i, j, k: (i, k)),
        b_spec=pl.BlockSpec((tk, tn), lambda i, j, k: (k, j)),
        out_shape=jax.ShapeDtypeStruct((M, N), out_dtype),
        out_spec=pl.BlockSpec((tm, tn), lambda i, j, k: (i, j)),
        dims=((1,), (0,)), acc_shape=(tm, tn), name=name, add=add,
        add_spec=pl.BlockSpec((tm, tn), lambda i, j, k: (i, j)))


def _mm_nt(a, b, out_dtype, name, tk_c=(2048, 1536, 1024, 768, 512)):
    M, K = a.shape
    N = b.shape[0]
    tm = _pick(M, (1024, 512, 256))
    tn = _pick(N, (1024, 768, 512))
    tk = _pick(K, tk_c)
    return _matmul(
        a, b, grid=(M // tm, N // tn, K // tk),
        a_spec=pl.BlockSpec((tm, tk), lambda i, j, k: (i, k)),
        b_spec=pl.BlockSpec((tn, tk), lambda i, j, k: (j, k)),
        out_shape=jax.ShapeDtypeStruct((M, N), out_dtype),
        out_spec=pl.BlockSpec((tm, tn), lambda i, j, k: (i, j)),
        dims=((1,), (1,)), acc_shape=(tm, tn), name=name)


def _mm_tn(a, b, out_dtype, name, tn_c=(1024, 768, 512)):
    K, M = a.shape
    N = b.shape[1]
    tm = _pick(M, (1024, 768, 512))
    tn = _pick(N, tn_c)
    tk = _pick(K, (1024, 512, 256))
    return _matmul(
        a, b, grid=(M // tm, N // tn, K // tk),
        a_spec=pl.BlockSpec((tk, tm), lambda i, j, k: (k, i)),
        b_spec=pl.BlockSpec((tk, tn), lambda i, j, k: (k, j)),
        out_shape=jax.ShapeDtypeStruct((M, N), out_dtype),
        out_spec=pl.BlockSpec((tm, tn), lambda i, j, k: (i, j)),
        dims=((0,), (0,)), acc_shape=(tm, tn), name=name)


def _mm_up(h, wup, name):
    M, K = h.shape
    ns = wup.shape[2]
    tm = _pick(M, (1024, 512, 256))
    tn = 768
    per = ns // tn
    return _matmul(
        h, wup, grid=(M // tm, N_DEV * per, 1),
        a_spec=pl.BlockSpec((tm, K), lambda i, j, k: (i, 0)),
        b_spec=pl.BlockSpec((None, K, tn), lambda i, j, k: (j // per, 0, j % per)),
        out_shape=jax.ShapeDtypeStruct((M, N_DEV * ns), F32),
        out_spec=pl.BlockSpec((tm, tn), lambda i, j, k: (i, j)),
        dims=((1,), (0,)), acc_shape=(tm, tn), name=name)


def _mm_up_t(dup, wup, name):
    M = dup.shape[0]
    D, ns = wup.shape[1], wup.shape[2]
    tm = _pick(M, (1024, 512, 256))
    tn = 1024
    tk = ns
    return _matmul(
        dup, wup, grid=(M // tm, D // tn, N_DEV),
        a_spec=pl.BlockSpec((tm, tk), lambda i, j, k: (i, k)),
        b_spec=pl.BlockSpec((None, tn, tk), lambda i, j, k: (k, j, 0)),
        out_shape=jax.ShapeDtypeStruct((M, D), F32),
        out_spec=pl.BlockSpec((tm, tn), lambda i, j, k: (i, j)),
        dims=((1,), (1,)), acc_shape=(tm, tn), name=name)


def _mm_dup(h, dup, ns, name):
    K, M = h.shape
    tm = 1024
    tn = 768
    per = ns // tn
    tk = _pick(K, (1024, 512, 256))
    return _matmul(
        h, dup, grid=(M // tm, N_DEV * per, K // tk),
        a_spec=pl.BlockSpec((tk, tm), lambda i, j, k: (k, i)),
        b_spec=pl.BlockSpec((tk, tn), lambda i, j, k: (k, j)),
        out_shape=jax.ShapeDtypeStruct((N_DEV, M, ns), BF16),
        out_spec=pl.BlockSpec((None, tm, tn), lambda i, j, k: (j // per, i, j % per)),
        dims=((0,), (0,)), acc_shape=(tm, tn), name=name)


def _norm_fwd(x, w, name):
    T, D = x.shape
    tt = _tile(T)

    def body(x_ref, w_ref, h_ref):
        xv = x_ref[...]
        r = lax.rsqrt(jnp.mean(xv * xv, axis=1, keepdims=True) + EPS)
        h_ref[...] = (xv * r * w_ref[...]).astype(BF16)

    return pl.pallas_call(
        body, name=name, grid=(T // tt,),
        in_specs=[pl.BlockSpec((tt, D), lambda t: (t, 0)), pl.BlockSpec((1, D), lambda t: (0, 0))],
        out_specs=pl.BlockSpec((tt, D), lambda t: (t, 0)),
        out_shape=jax.ShapeDtypeStruct((T, D), BF16), compiler_params=_cp(("parallel",)))(x, w)


def _norm_bwd(x, w, dh, dres, name):
    T, D = x.shape
    tt = _tile(T)

    def body(x_ref, w_ref, dh_ref, dres_ref, dx_ref, dw_ref):
        t = pl.program_id(0)
        xv = x_ref[...]
        r = lax.rsqrt(jnp.mean(xv * xv, axis=1, keepdims=True) + EPS)
        xh = xv * r
        dh_v = dh_ref[...]
        dxh = dh_v * w_ref[...]
        dx_ref[...] = dres_ref[...] + r * (dxh - xh * jnp.mean(dxh * xh, axis=1, keepdims=True))
        part = jnp.sum(dh_v * xh, axis=0, keepdims=True)

        @pl.when(t == 0)
        def _():
            dw_ref[...] = part

        @pl.when(t > 0)
        def _():
            dw_ref[...] += part

    row = pl.BlockSpec((tt, D), lambda t: (t, 0))
    vec = pl.BlockSpec((1, D), lambda t: (0, 0))
    return pl.pallas_call(
        body, name=name, grid=(T // tt,), in_specs=[row, vec, row, row], out_specs=[row, vec],
        out_shape=[jax.ShapeDtypeStruct((T, D), F32), jax.ShapeDtypeStruct((1, D), F32)],
        compiler_params=_cp(("arbitrary",)))(x, w, dh, dres)


def _loss_head(x, w, target):
    T, D = x.shape
    tt = _tile(T)

    def body(x_ref, w_ref, t_ref, loss_ref, dx_ref, dw_ref):
        t = pl.program_id(0)
        xv = x_ref[...]
        r = lax.rsqrt(jnp.mean(xv * xv, axis=1, keepdims=True) + EPS)
        xh = xv * r
        err = xh * w_ref[...] - t_ref[...]
        lp = 0.5 * jnp.sum(jnp.mean(err * err, axis=1, keepdims=True), axis=0, keepdims=True)
        dy = err * (1.0 / D)
        dxh = dy * w_ref[...]
        dx_ref[...] = r * (dxh - xh * jnp.mean(dxh * xh, axis=1, keepdims=True))
        part = jnp.sum(dy * xh, axis=0, keepdims=True)
        lpb = jnp.broadcast_to(lp, (1, 128))

        @pl.when(t == 0)
        def _():
            dw_ref[...] = part
            loss_ref[...] = lpb

        @pl.when(t > 0)
        def _():
            dw_ref[...] += part
            loss_ref[...] += lpb

    row = pl.BlockSpec((tt, D), lambda t: (t, 0))
    vec = pl.BlockSpec((1, D), lambda t: (0, 0))
    return pl.pallas_call(
        body, name="loss_head", grid=(T // tt,), in_specs=[row, vec, row],
        out_specs=[pl.BlockSpec((1, 128), lambda t: (0, 0)), row, vec],
        out_shape=[jax.ShapeDtypeStruct((1, 128), F32), jax.ShapeDtypeStruct((T, D), F32),
                   jax.ShapeDtypeStruct((1, D), F32)],
        compiler_params=_cp(("arbitrary",)))(x, w, target)


def _conv_fwd(x, col0, C, w, b, cb, name):
    T = x.shape[0]
    K = w.shape[0]
    tt = _tile(T)
    nt, nc, c0 = T // tt, C // cb, col0 // cb
    has_b = b is not None

    def body(*refs):
        if has_b:
            x_ref, halo_ref, w_ref, b_ref, y_ref = refs
        else:
            x_ref, halo_ref, w_ref, y_ref = refs
        t = pl.program_id(1)
        halo = jnp.where(t == 0, 0.0, halo_ref[...])
        xe = jnp.concatenate([halo, x_ref[...]], axis=0)
        acc = xe * w_ref[K - 1:K, :]
        for j in range(K - 1):
            acc = acc + pltpu.roll(xe, K - 1 - j, 0) * w_ref[j:j + 1, :]
        if has_b:
            acc = acc + b_ref[...]
        y_ref[...] = acc[8:, :]

    in_specs = [pl.BlockSpec((tt, cb), lambda j, t: (t, c0 + j)),
                pl.BlockSpec((8, cb), lambda j, t: (jnp.maximum(t * (tt // 8) - 1, 0), c0 + j)),
                pl.BlockSpec((K, cb), lambda j, t: (0, j))]
    args = [x, x, w]
    if has_b:
        in_specs.append(pl.BlockSpec((1, cb), lambda j, t: (0, j)))
        args.append(b)
    return pl.pallas_call(
        body, name=name, grid=(nc, nt), in_specs=in_specs,
        out_specs=pl.BlockSpec((tt, cb), lambda j, t: (t, j)),
        out_shape=jax.ShapeDtypeStruct((T, C), F32), compiler_params=_cp(("parallel", "parallel")))(*args)


def _conv_bwd(dy, x, col0, w, cb, name, want_db=False):
    T, C = dy.shape
    K = w.shape[0]
    tt = _tile(T)
    nt, nc, c0 = T // tt, C // cb, col0 // cb

    def body(*refs):
        if want_db:
            dy_ref, dyn_ref, x_ref, xp_ref, w_ref, dx_ref, dw_ref, db_ref = refs
        else:
            dy_ref, dyn_ref, x_ref, xp_ref, w_ref, dx_ref, dw_ref = refs
        t = pl.program_id(1)
        dyv = dy_ref[...]
        nxt = jnp.where(t == nt - 1, 0.0, dyn_ref[...])
        dye = jnp.concatenate([dyv, nxt], axis=0)
        n = tt + 8
        acc = dye * w_ref[K - 1:K, :]
        for j in range(K - 1):
            acc = acc + pltpu.roll(dye, n - (K - 1 - j), 0) * w_ref[j:j + 1, :]
        dx_ref[...] = acc[:tt, :].astype(dx_ref.dtype)
        prev = jnp.where(t == 0, 0.0, xp_ref[...])
        xe = jnp.concatenate([prev, x_ref[...]], axis=0)

        @pl.when(t == 0)
        def _():
            dw_ref[...] = jnp.zeros_like(dw_ref)
            if want_db:
                db_ref[...] = jnp.zeros_like(db_ref)

        for j in range(K):
            sh = K - 1 - j
            xs = xe[8:, :] if sh == 0 else pltpu.roll(xe, sh, 0)[8:, :]
            dw_ref[j:j + 1, :] += jnp.sum(dyv * xs, axis=0, keepdims=True)
        if want_db:
            db_ref[...] += jnp.sum(dyv, axis=0, keepdims=True)

    h8 = tt // 8
    in_specs = [pl.BlockSpec((tt, cb), lambda j, t: (t, j)),
                pl.BlockSpec((8, cb), lambda j, t: (jnp.minimum((t + 1) * h8, T // 8 - 1), j)),
                pl.BlockSpec((tt, cb), lambda j, t: (t, c0 + j)),
                pl.BlockSpec((8, cb), lambda j, t: (jnp.maximum(t * h8 - 1, 0), c0 + j)),
                pl.BlockSpec((K, cb), lambda j, t: (0, j))]
    out_specs = [pl.BlockSpec((tt, cb), lambda j, t: (t, j)), pl.BlockSpec((K, cb), lambda j, t: (0, j))]
    out_shape = [jax.ShapeDtypeStruct((T, C), BF16), jax.ShapeDtypeStruct((K, C), F32)]
    if want_db:
        out_specs.append(pl.BlockSpec((1, cb), lambda j, t: (0, j)))
        out_shape.append(jax.ShapeDtypeStruct((1, C), F32))
    return pl.pallas_call(
        body, name=name, grid=(nc, nt), in_specs=in_specs, out_specs=out_specs, out_shape=out_shape,
        compiler_params=_cp(("parallel", "arbitrary")))(dy, dy, x, x, w)


def _pool_d(ue, g, pos, tt):
    win = POOL_WINDOWS[g]
    ug = ue[:, g * 128:(g + 1) * 128]
    s = ug
    k = 1
    while k < win:
        s = s + pltpu.roll(s, k, 0)
        k *= 2
    cnt = jnp.minimum(pos + 1, win).astype(F32)
    return s[16:, :] / cnt - ug[16:, :]


def _pool_fwd(proj, pw, pb, ps):
    T = proj.shape[0]
    tt = _tile(T)

    def body(u_ref, halo_ref, w_ref, b_ref, s_ref, y_ref):
        t = pl.program_id(0)
        halo = jnp.where(t == 0, 0.0, halo_ref[...])
        ue = jnp.concatenate([halo, u_ref[...]], axis=0)
        pos = t * tt + lax.broadcasted_iota(jnp.int32, (tt, 1), 0)
        for g in range(4):
            sl = slice(g * 128, (g + 1) * 128)
            d = _pool_d(ue, g, pos, tt)
            yg = _mm(d, w_ref[g]) + b_ref[:, sl]
            y_ref[:, sl] = (yg * s_ref[:, sl]).astype(BF16)

    vec = pl.BlockSpec((1, POOL_W), lambda t: (0, 0))
    return pl.pallas_call(
        body, name="pool_fwd", grid=(T // tt,),
        in_specs=[pl.BlockSpec((tt, POOL_W), lambda t: (t, 0)),
                  pl.BlockSpec((16, POOL_W), lambda t: (jnp.maximum(t * (tt // 16) - 1, 0), 0)),
                  pl.BlockSpec((4, 128, 128), lambda t: (0, 0, 0)), vec, vec],
        out_specs=pl.BlockSpec((tt, POOL_W), lambda t: (t, 0)),
        out_shape=jax.ShapeDtypeStruct((T, POOL_W), BF16), compiler_params=_cp(("parallel",)))(
            proj, proj, pw, pb, ps)


def _pool_bwd(dmixed, proj, pw, pb, ps):
    T = proj.shape[0]
    tt = _tile(T)
    nt = T // tt

    def body(dy_ref, dyn_ref, u_ref, halo_ref, w_ref, b_ref, s_ref, du_ref, dw_ref, db_ref, ds_ref):
        t = pl.program_id(0)
        halo = jnp.where(t == 0, 0.0, halo_ref[...])
        ue = jnp.concatenate([halo, u_ref[...]], axis=0)
        dyv = dy_ref[...]
        nxt = jnp.where(t == nt - 1, 0.0, dyn_ref[...])
        dye = jnp.concatenate([dyv, nxt], axis=0)
        n = tt + 16
        pos = t * tt + lax.broadcasted_iota(jnp.int32, (tt, 1), 0)
        pos_e = t * tt + lax.broadcasted_iota(jnp.int32, (n, 1), 0)

        @pl.when(t == 0)
        def _():
            dw_ref[...] = jnp.zeros_like(dw_ref)
            db_ref[...] = jnp.zeros_like(db_ref)
            ds_ref[...] = jnp.zeros_like(ds_ref)

        for g in range(4):
            win = POOL_WINDOWS[g]
            sl = slice(g * 128, (g + 1) * 128)
            d = _pool_d(ue, g, pos, tt)
            wg = w_ref[g]
            ypre = _mm(d, wg) + b_ref[:, sl]
            sc = s_ref[:, sl]
            ds_ref[:, sl] += jnp.sum(dyv[:, sl] * ypre, axis=0, keepdims=True)
            dyp_e = dye[:, sl] * sc
            dyp = dyp_e[:tt, :]
            db_ref[:, sl] += jnp.sum(dyp, axis=0, keepdims=True)
            dw_ref[g] += _mm(d, dyp, 0, 0)
            dd_e = _mm(dyp_e, wg, 1, 1)
            cnt_e = jnp.minimum(pos_e + 1, win).astype(F32)
            s = dd_e / cnt_e
            k = 1
            while k < win:
                s = s + pltpu.roll(s, n - k, 0)
                k *= 2
            du_ref[:, sl] = (s[:tt, :] - dd_e[:tt, :]).astype(BF16)

    vec = pl.BlockSpec((1, POOL_W), lambda t: (0, 0))
    h16 = tt // 16
    return pl.pallas_call(
        body, name="pool_bwd", grid=(nt,),
        in_specs=[pl.BlockSpec((tt, POOL_W), lambda t: (t, 0)),
                  pl.BlockSpec((16, POOL_W), lambda t: (jnp.minimum((t + 1) * h16, T // 16 - 1), 0)),
                  pl.BlockSpec((tt, POOL_W), lambda t: (t, 0)),
                  pl.BlockSpec((16, POOL_W), lambda t: (jnp.maximum(t * h16 - 1, 0), 0)),
                  pl.BlockSpec((4, 128, 128), lambda t: (0, 0, 0)), vec, vec],
        out_specs=[pl.BlockSpec((tt, POOL_W), lambda t: (t, 0)),
                   pl.BlockSpec((4, 128, 128), lambda t: (0, 0, 0)), vec, vec],
        out_shape=[jax.ShapeDtypeStruct((T, POOL_W), BF16), jax.ShapeDtypeStruct((4, 128, 128), F32),
                   jax.ShapeDtypeStruct((1, POOL_W), F32), jax.ShapeDtypeStruct((1, POOL_W), F32)],
        compiler_params=_cp(("arbitrary",)))(dmixed, dmixed, proj, proj, pw, pb, ps)


def _gdn_pre_fwd(cpre, proj, alog, dtb):
    T = cpre.shape[0]
    tt = _tile(T)

    def body(c_ref, ab_ref, alog_ref, dtb_ref, qkv_ref, bb_ref, gb_ref):
        for p in range(3):
            for h in range(HEADS):
                cc = c_ref[:, (p * HEADS + h) * HD:(p * HEADS + h + 1) * HD]
                s = cc * _sigmoid(cc)
                if p < 2:
                    s = s * lax.rsqrt(jnp.sum(s * s, axis=1, keepdims=True) + EPS)
                if p == 0:
                    s = s * (HD ** -0.5)
                qkv_ref[p, h] = s
        ab = ab_ref[...]
        g = -jnp.exp(alog_ref[...]) * _softplus(ab + dtb_ref[...])
        r64 = lax.broadcasted_iota(jnp.int32, (tt, 1), 0) & (CHUNK - 1)
        k = 1
        while k < CHUNK:
            g = g + jnp.where(r64 >= k, pltpu.roll(g, k, 0), 0.0)
            k *= 2
        sb = _sigmoid(ab)
        for h in range(HEADS):
            gb_ref[h] = jnp.broadcast_to(g[:, h:h + 1], (tt, HD))
            bb_ref[h] = jnp.broadcast_to(sb[:, HEADS + h:HEADS + h + 1], (tt, HD))

    vec = pl.BlockSpec((1, 128), lambda t: (0, 0))
    hb = pl.BlockSpec((HEADS, tt, HD), lambda t: (0, t, 0))
    return pl.pallas_call(
        body, name="gdn_pre_fwd", grid=(T // tt,),
        in_specs=[pl.BlockSpec((tt, 3 * GDN_W), lambda t: (t, 0)),
                  pl.BlockSpec((tt, 128), lambda t: (t, C_AB // 128)), vec, vec],
        out_specs=[pl.BlockSpec((3, HEADS, tt, HD), lambda t: (0, 0, t, 0)), hb, hb],
        out_shape=[jax.ShapeDtypeStruct((3, HEADS, T, HD), F32), jax.ShapeDtypeStruct((HEADS, T, HD), F32),
                   jax.ShapeDtypeStruct((HEADS, T, HD), F32)],
        compiler_params=_cp(("parallel",)))(cpre, proj, alog, dtb)


def _gdn_pre_bwd(dq, dk, dv, cpre, dbb, dgb, proj, alog, dtb):
    T = cpre.shape[0]
    tt = _tile(T)

    def body(dq_ref, dk_ref, dv_ref, c_ref, dbb_ref, dgb_ref, ab_ref, alog_ref, dtb_ref,
             dc_ref, dab_ref, dalog_ref, ddtb_ref):
        t = pl.program_id(0)
        srcs = (dq_ref, dk_ref, dv_ref)
        for p in range(3):
            for h in range(HEADS):
                sl = slice((p * HEADS + h) * HD, (p * HEADS + h + 1) * HD)
                cc = c_ref[:, sl]
                sg = _sigmoid(cc)
                s = cc * sg
                dyv = srcs[p][h]
                if p < 2:
                    r = lax.rsqrt(jnp.sum(s * s, axis=1, keepdims=True) + EPS)
                    y = s * r
                    if p == 0:
                        dyv = dyv * (HD ** -0.5)
                    ds = r * (dyv - y * jnp.sum(dyv * y, axis=1, keepdims=True))
                else:
                    ds = dyv
                dc_ref[:, sl] = ds * sg * (1.0 + cc * (1.0 - sg))
        lane = lax.broadcasted_iota(jnp.int32, (tt, 128), 1)
        dg = jnp.zeros((tt, 128), F32)
        dbeta = jnp.zeros((tt, 128), F32)
        for h in range(HEADS):
            dg = jnp.where(lane == h, dgb_ref[h], dg)
            dbeta = jnp.where(lane == HEADS + h, dbb_ref[h], dbeta)
        r64 = lax.broadcasted_iota(jnp.int32, (tt, 1), 0) & (CHUNK - 1)
        k = 1
        while k < CHUNK:
            dg = dg + jnp.where(r64 < CHUNK - k, pltpu.roll(dg, tt - k, 0), 0.0)
            k *= 2
        ab = ab_ref[...]
        e = jnp.exp(alog_ref[...])
        xx = ab + dtb_ref[...]
        g = -e * _softplus(xx)
        da = jnp.where(lane < HEADS, dg * (-e) * _sigmoid(xx), 0.0)
        pa = jnp.sum(jnp.where(lane < HEADS, dg * g, 0.0), axis=0, keepdims=True)
        pd = jnp.sum(da, axis=0, keepdims=True)

        @pl.when(t == 0)
        def _():
            dalog_ref[...] = pa
            ddtb_ref[...] = pd

        @pl.when(t > 0)
        def _():
            dalog_ref[...] += pa
            ddtb_ref[...] += pd

        sb = _sigmoid(ab)
        dab_ref[...] = jnp.where(lane < HEADS, da, dbeta * sb * (1.0 - sb)).astype(BF16)

    vec = pl.BlockSpec((1, 128), lambda t: (0, 0))
    hb = pl.BlockSpec((HEADS, tt, HD), lambda t: (0, t, 0))
    return pl.pallas_call(
        body, name="gdn_pre_bwd", grid=(T // tt,),
        in_specs=[hb, hb, hb, pl.BlockSpec((tt, 3 * GDN_W), lambda t: (t, 0)), hb, hb,
                  pl.BlockSpec((tt, 128), lambda t: (t, C_AB // 128)), vec, vec],
        out_specs=[pl.BlockSpec((tt, 3 * GDN_W), lambda t: (t, 0)), pl.BlockSpec((tt, 128), lambda t: (t, 0)),
                   vec, vec],
        out_shape=[jax.ShapeDtypeStruct((T, 3 * GDN_W), F32), jax.ShapeDtypeStruct((T, 128), BF16),
                   jax.ShapeDtypeStruct((1, 128), F32), jax.ShapeDtypeStruct((1, 128), F32)],
        compiler_params=_cp(("arbitrary",)))(dq, dk, dv, cpre, dbb, dgb, proj, alog, dtb)


def _tri_inv(a):
    nb = a.shape[0]
    ri = lax.broadcasted_iota(jnp.int32, (nb, CHUNK, CHUNK), 1)
    ci = lax.broadcasted_iota(jnp.int32, (nb, CHUNK, CHUNK), 2)
    n = -a
    p = jnp.where(ri == ci, 1.0, 0.0) + n
    for _ in range(5):
        n = _bmm(n, n, 2, 1, HI, False)
        p = p + _bmm(p, n, 2, 1, HI, False)
    return p


def _gdn_chunk_common(q, k, v, bb3, gb3, need_t):
    nb = q.shape[0]
    beta = bb3[:, :, 0:1]
    gcol = gb3[:, :, 0:1]
    bcol = bb3[:, :, :CHUNK]
    gcm = gb3[:, :, :CHUNK]
    oh = jnp.where(lax.broadcasted_iota(jnp.int32, (nb, CHUNK, HD), 2) == 0, 1.0, 0.0)
    grow = _bmm(oh, gb3, 2, 2, HI, False)
    ri = lax.broadcasted_iota(jnp.int32, (nb, CHUNK, CHUNK), 1)
    ci = lax.broadcasted_iota(jnp.int32, (nb, CHUNK, CHUNK), 2)
    tril, stl = ri >= ci, ri > ci
    dg = gcm - grow
    dec = jnp.where(tril, jnp.exp(jnp.where(tril, dg, 0.0)), 0.0)
    kk = _bmm(k, k, 2, 2)
    qk = _bmm(q, k, 2, 2)
    a = jnp.where(stl, bcol * kk * dec, 0.0)
    tm = _tri_inv(a)
    gam = jnp.exp(gcol)
    glast = gb3[:, CHUNK - 1:CHUNK, 0:1]
    egl = jnp.exp(glast)
    rw = k * (beta * gam)
    ru = v * beta
    wu = _bmm(tm, jnp.concatenate([rw, ru], axis=2), 2, 1, HI, False)
    kdf = jnp.exp(glast - gcol)
    out = dict(beta=beta, bcol=bcol, tril=tril, stl=stl, dec=dec, kk=kk, qk=qk, tm=tm, gam=gam, egl=egl,
               rw=rw, wu=wu, at=qk * dec, qd=q * gam, kdf=kdf, kd=k * kdf)
    if need_t:
        brow = _bmm(oh, bb3, 2, 2, HI, False)
        triu, stu = ri <= ci, ri < ci
        dect = jnp.where(triu, jnp.exp(jnp.where(triu, -dg, 0.0)), 0.0)
        qkt = _bmm(k, q, 2, 2)
        at_t = jnp.where(stu, brow * kk * dect, 0.0)
        out.update(brow=brow, triu=triu, stu=stu, dect=dect, qkt=qkt, tmt=_tri_inv(at_t), att=qkt * dect)
    return out


def _gdn_rows(T):
    return min(T, 512)


def _gdn_fwd(qkv, bb, gb, proj, nw):
    T = proj.shape[0]
    R = _gdn_rows(T)
    nb = R // CHUNK

    def body(q_ref, k_ref, v_ref, bb_ref, gb_ref, z_ref, nw_ref, y_ref, st_ref, s_ref, w_s, u_s, at_s, qd_s, kd_s):
        t = pl.program_id(1)

        @pl.when(t == 0)
        def _():
            s_ref[...] = jnp.zeros_like(s_ref)

        sh = (nb, CHUNK, HD)
        q, k, v = q_ref[...].reshape(sh), k_ref[...].reshape(sh), v_ref[...].reshape(sh)
        c = _gdn_chunk_common(q, k, v, bb_ref[...].reshape(sh), gb_ref[...].reshape(sh), False)
        w_s[...] = c["wu"][:, :, :HD]
        u_s[...] = c["wu"][:, :, HD:]
        at_s[...] = c["at"]
        qd_s[...] = c["qd"]
        kd_s[...] = c["kd"]
        egl = c["egl"]
        nwv = nw_ref[...]
        for n in range(nb):
            s = s_ref[...]
            st_ref[n] = s
            vn = u_s[n] - _mm(w_s[n], s)
            o = _mm(qd_s[n], s) + _mm(at_s[n], vn)
            s_ref[...] = s * egl[n] + _mm(kd_s[n], vn, 0, 0)
            rows = slice(n * CHUNK, (n + 1) * CHUNK)
            zz = z_ref[rows, :]
            on = o * lax.rsqrt(jnp.mean(o * o, axis=1, keepdims=True) + EPS)
            y_ref[rows, :] = (on * nwv * (zz * _sigmoid(zz))).astype(BF16)

    def hm(p):
        return pl.BlockSpec((None, None, R, HD), lambda h, t: (p, h, t, 0))

    hb = pl.BlockSpec((None, R, HD), lambda h, t: (h, t, 0))
    cs = pltpu.VMEM((nb, CHUNK, HD), F32)
    return pl.pallas_call(
        body, name="gdn_fwd", grid=(HEADS, T // R),
        in_specs=[hm(0), hm(1), hm(2), hb, hb, pl.BlockSpec((R, HD), lambda h, t: (t, C_Z // HD + h)),
                  pl.BlockSpec((1, HD), lambda h, t: (0, 0))],
        out_specs=[pl.BlockSpec((R, HD), lambda h, t: (t, h)),
                   pl.BlockSpec((None, nb, HD, HD), lambda h, t: (h, t, 0, 0))],
        out_shape=[jax.ShapeDtypeStruct((T, GDN_W), BF16), jax.ShapeDtypeStruct((HEADS, T // CHUNK, HD, HD), F32)],
        scratch_shapes=[pltpu.VMEM((HD, HD), F32), cs, cs, pltpu.VMEM((nb, CHUNK, CHUNK), F32), cs, cs],
        compiler_params=_cp(("parallel", "arbitrary")))(qkv, qkv, qkv, bb, gb, proj, nw)


def _gdn_bwd(qkv, bb, gb, proj, nw, states, dmixed):
    T = proj.shape[0]
    R = _gdn_rows(T)
    nb = R // CHUNK
    ntb = T // R

    def body(q_ref, k_ref, v_ref, bb_ref, gb_ref, z_ref, nw_ref, st_ref, dy_ref,
             dq_ref, dk_ref, dv_ref, dbb_ref, dgb_ref, dz_ref, dnw_ref,
             ds_ref, att_s, do_s, kd_s, vn_s, qd_s, w_s, dvn_s, dkd_s, dgl_s):
        hh = pl.program_id(0)
        t = pl.program_id(1)

        @pl.when(t == 0)
        def _():
            ds_ref[...] = jnp.zeros_like(ds_ref)

        @pl.when((t == 0) & (hh == 0))
        def _():
            dnw_ref[...] = jnp.zeros_like(dnw_ref)

        sh = (nb, CHUNK, HD)
        q, k, v = q_ref[...].reshape(sh), k_ref[...].reshape(sh), v_ref[...].reshape(sh)
        c = _gdn_chunk_common(q, k, v, bb_ref[...].reshape(sh), gb_ref[...].reshape(sh), True)
        w, u = c["wu"][:, :, :HD], c["wu"][:, :, HD:]
        sall = st_ref[...]
        vn = u - _bmm(w, sall, 2, 1)
        o = _bmm(c["qd"], sall, 2, 1) + _bmm(c["at"], vn, 2, 1)
        z = z_ref[...].reshape(sh)
        dy = dy_ref[...].reshape(sh)
        nwv = nw_ref[...].reshape(1, 1, HD)
        rs = lax.rsqrt(jnp.mean(o * o, axis=2, keepdims=True) + EPS)
        on = o * rs
        sg = _sigmoid(z)
        sz = z * sg
        dnw_ref[...] += jnp.sum(jnp.sum(dy * on * sz, axis=0), axis=0, keepdims=True)
        dz_ref[...] = (dy * on * nwv * (sg * (1.0 + z * (1.0 - sg)))).reshape(R, HD).astype(BF16)
        don = dy * nwv * sz
        do = rs * (don - on * jnp.mean(don * on, axis=2, keepdims=True))
        dqd = _bmm(do, sall, 2, 2)
        dat = jnp.where(c["tril"], _bmm(do, vn, 2, 2), 0.0)
        datt = jnp.where(c["triu"], _bmm(vn, do, 2, 2), 0.0)
        att_s[...] = c["att"]
        do_s[...] = do
        kd_s[...] = c["kd"]
        vn_s[...] = vn
        qd_s[...] = c["qd"]
        w_s[...] = w
        egl = c["egl"]
        for n in reversed(range(nb)):
            dso = ds_ref[...]
            dvn_n = _mm(att_s[n], do_s[n]) + _mm(kd_s[n], dso)
            dkd_s[n] = _mm(vn_s[n], dso, 1, 1)
            dgl = egl[n] * jnp.sum(jnp.sum(st_ref[n] * dso, axis=1, keepdims=True), axis=0, keepdims=True)
            dgl_s[n] = jnp.broadcast_to(dgl, (8, HD))
            ds_ref[...] = egl[n] * dso + _mm(qd_s[n], do_s[n], 0, 0) - _mm(w_s[n], dvn_n, 0, 0)
            dvn_s[n] = dvn_n
        dvn = dvn_s[...]
        dkd = dkd_s[...]
        dgl = dgl_s[...][:, 0:1, 0:1]
        dw = -_bmm(dvn, sall, 2, 2)
        dr = _bmm(c["tmt"], jnp.concatenate([dw, dvn], axis=2), 2, 1, HI, False)
        drw, dru = dr[:, :, :HD], dr[:, :, HD:]
        wu = c["wu"]
        da = -jnp.where(c["stl"], _bmm(dr, wu, 2, 2, HI, False), 0.0)
        da_t = -jnp.where(c["stu"], _bmm(wu, dr, 2, 2, HI, False), 0.0)
        beta, gam, dec, dect, kk = c["beta"], c["gam"], c["dec"], c["dect"], c["kk"]
        bcol, brow = c["bcol"], c["brow"]
        dbeta = (jnp.sum(da * kk * dec, axis=2, keepdims=True)
                 + jnp.sum(drw * k * gam + dru * v, axis=2, keepdims=True))
        dkk = bcol * da * dec
        dkk_t = brow * da_t * dect
        e = (bcol * da * kk + dat * c["qk"]) * dec
        e_t = (brow * da_t * kk + datt * c["qkt"]) * dect
        kd = c["kd"]
        dq_ref[...] = (_bmm(dat * dec, k, 2, 1) + dqd * gam).reshape(R, HD)
        dk_ref[...] = (_bmm(datt * dect, q, 2, 1) + _bmm(dkk + dkk_t, k, 2, 1) + dkd * c["kdf"]
                       + drw * (beta * gam)).reshape(R, HD)
        dv_ref[...] = (dru * beta).reshape(R, HD)
        skd = jnp.sum(dkd * kd, axis=2, keepdims=True)
        dgc = (jnp.sum(e, axis=2, keepdims=True) - jnp.sum(e_t, axis=2, keepdims=True)
               + jnp.sum(drw * c["rw"] + dqd * c["qd"], axis=2, keepdims=True) - skd)
        tot = jnp.sum(skd, axis=1, keepdims=True) + dgl
        rowi = lax.broadcasted_iota(jnp.int32, (nb, CHUNK, 1), 1)
        dgc = dgc + jnp.where(rowi == CHUNK - 1, tot, 0.0)
        dbb_ref[...] = jnp.broadcast_to(dbeta, sh).reshape(R, HD)
        dgb_ref[...] = jnp.broadcast_to(dgc, sh).reshape(R, HD)

    def rt(t):
        return ntb - 1 - t

    def hm(p):
        return pl.BlockSpec((None, None, R, HD), lambda h, t: (p, h, rt(t), 0))

    hb = pl.BlockSpec((None, R, HD), lambda h, t: (h, rt(t), 0))
    cs = pltpu.VMEM((nb, CHUNK, HD), F32)
    ob = jax.ShapeDtypeStruct((HEADS, T, HD), F32)
    return pl.pallas_call(
        body, name="gdn_bwd", grid=(HEADS, ntb),
        in_specs=[hm(0), hm(1), hm(2), hb, hb, pl.BlockSpec((R, HD), lambda h, t: (rt(t), C_Z // HD + h)),
                  pl.BlockSpec((1, HD), lambda h, t: (0, 0)),
                  pl.BlockSpec((None, nb, HD, HD), lambda h, t: (h, rt(t), 0, 0)),
                  pl.BlockSpec((R, HD), lambda h, t: (rt(t), M_GDN // HD + h))],
        out_specs=[hb, hb, hb, hb, hb, pl.BlockSpec((R, HD), lambda h, t: (rt(t), h)),
                   pl.BlockSpec((1, HD), lambda h, t: (0, 0))],
        out_shape=[ob, ob, ob, ob, ob, jax.ShapeDtypeStruct((T, GDN_W), BF16), jax.ShapeDtypeStruct((1, HD), F32)],
        scratch_shapes=[pltpu.VMEM((HD, HD), F32), pltpu.VMEM((nb, CHUNK, CHUNK), F32), cs, cs, cs, cs, cs, cs, cs,
                        pltpu.VMEM((nb, 8, HD), F32)],
        compiler_params=_cp(("arbitrary", "arbitrary")))(qkv, qkv, qkv, bb, gb, proj, nw, states, dmixed)


def _lru_gates(xc, wa, ba, wx, bx, lam, gpos):
    xb = xc.astype(BF16)
    r = _sigmoid(_mm(xb, wa) + ba)
    i = _sigmoid(_mm(xb, wx) + bx)
    sp = _softplus(-lam)
    log_a = -LRU_C * r * sp
    a = jnp.exp(log_a)
    mult = jnp.where(gpos == 0, 1.0, jnp.sqrt(-_expm1(2.0 * log_a)))
    return r, i, sp, a, mult


def _lru_fwd(xc, proj, wa, ba, wx, bx, lam):
    T = xc.shape[0]
    tt = _tile(T)

    def body(xc_ref, gr_ref, wa_ref, ba_ref, wx_ref, bx_ref, lam_ref, y_ref, h_ref, carry_ref):
        t = pl.program_id(1)

        @pl.when(t == 0)
        def _():
            carry_ref[...] = jnp.zeros_like(carry_ref)

        row = lax.broadcasted_iota(jnp.int32, (tt, 1), 0)
        xcv = xc_ref[...]
        r, i, sp, a, mult = _lru_gates(xcv, wa_ref[...], ba_ref[...], wx_ref[...], bx_ref[...], lam_ref[...],
                                       t * tt + row)
        av, bv = a, mult * i * xcv
        k = 1
        while k < tt:
            a_s = jnp.where(row >= k, pltpu.roll(av, k, 0), 1.0)
            b_s = jnp.where(row >= k, pltpu.roll(bv, k, 0), 0.0)
            bv = bv + av * b_s
            av = av * a_s
            k *= 2
        h = bv + av * carry_ref[0:1, :]
        carry_ref[...] = jnp.broadcast_to(h[tt - 1:tt, :], (8, 128))
        h_ref[...] = h
        y_ref[...] = (h * _gelu(gr_ref[...])).astype(BF16)

    blk = pl.BlockSpec((tt, 128), lambda j, t: (t, j))
    vec = pl.BlockSpec((1, 128), lambda j, t: (0, j))
    mat = pl.BlockSpec((None, 128, 128), lambda j, t: (j, 0, 0))
    return pl.pallas_call(
        body, name="lru_fwd", grid=(LRU_W // 128, T // tt),
        in_specs=[blk, pl.BlockSpec((tt, 128), lambda j, t: (t, C_GR // 128 + j)), mat, vec, mat, vec, vec],
        out_specs=[blk, blk],
        out_shape=[jax.ShapeDtypeStruct((T, LRU_W), BF16), jax.ShapeDtypeStruct((T, LRU_W), F32)],
        scratch_shapes=[pltpu.VMEM((8, 128), F32)],
        compiler_params=_cp(("parallel", "arbitrary")))(xc, proj, wa, ba, wx, bx, lam)


def _lru_bwd(dmixed, xc, proj, hst, wa, ba, wx, bx, lam):
    T = xc.shape[0]
    tt = _tile(T)
    nt = T // tt

    def body(dy_ref, xc_ref, gr_ref, h_ref, hp_ref, wa_ref, ba_ref, wx_ref, bx_ref, lam_ref,
             dxc_ref, dgr_ref, dwa_ref, dwx_ref, dba_ref, dbx_ref, dlam_ref, lc_ref, ac_ref):
        t = pl.program_id(1)
        tr = nt - 1 - t

        @pl.when(t == 0)
        def _():
            lc_ref[...] = jnp.zeros_like(lc_ref)
            ac_ref[...] = jnp.zeros_like(ac_ref)
            dwa_ref[...] = jnp.zeros_like(dwa_ref)
            dwx_ref[...] = jnp.zeros_like(dwx_ref)
            dba_ref[...] = jnp.zeros_like(dba_ref)
            dbx_ref[...] = jnp.zeros_like(dbx_ref)
            dlam_ref[...] = jnp.zeros_like(dlam_ref)

        row = lax.broadcasted_iota(jnp.int32, (tt, 1), 0)
        gpos = tr * tt + row
        xcv = xc_ref[...]
        wav, wxv, lamv = wa_ref[...], wx_ref[...], lam_ref[...]
        r, i, sp, a, mult = _lru_gates(xcv, wav, ba_ref[...], wxv, bx_ref[...], lamv, gpos)
        h = h_ref[...]
        dy = dy_ref[...]
        gg, dgg = _gelu_and_grad(gr_ref[...])
        dgr_ref[...] = (dy * h * dgg).astype(BF16)
        bv = dy * gg
        cv = jnp.where(row < tt - 1, pltpu.roll(a, tt - 1, 0), ac_ref[0:1, :])
        k = 1
        while k < tt:
            c_s = jnp.where(row < tt - k, pltpu.roll(cv, tt - k, 0), 1.0)
            b_s = jnp.where(row < tt - k, pltpu.roll(bv, tt - k, 0), 0.0)
            bv = bv + cv * b_s
            cv = cv * c_s
            k *= 2
        lm = bv + cv * lc_ref[0:1, :]
        lc_ref[...] = jnp.broadcast_to(lm[0:1, :], (8, 128))
        ac_ref[...] = jnp.broadcast_to(a[0:1, :], (8, 128))
        hp = jnp.where(tr == 0, 0.0, hp_ref[...])
        hs = pltpu.roll(jnp.concatenate([hp, h], axis=0), 1, 0)[8:, :]
        da = lm * hs
        dmult = lm * i * xcv
        di = lm * mult * xcv
        dxc = lm * mult * i
        dlog_a = a * da - jnp.where(gpos == 0, 0.0, dmult * a * a / mult)
        dr = dlog_a * (-LRU_C * sp)
        dsp = jnp.sum(dlog_a * (-LRU_C * r), axis=0, keepdims=True)
        dlam_ref[...] += dsp * (-_sigmoid(-lamv))
        dpr = dr * r * (1.0 - r)
        dpi = di * i * (1.0 - i)
        dba_ref[...] += jnp.sum(dpr, axis=0, keepdims=True)
        dbx_ref[...] += jnp.sum(dpi, axis=0, keepdims=True)
        dwa_ref[...] += _mm(xcv, dpr, 0, 0)
        dwx_ref[...] += _mm(xcv, dpi, 0, 0)
        dxc_ref[...] = dxc + _mm(dpr, wav, 1, 1) + _mm(dpi, wxv, 1, 1)

    def rt(t):
        return nt - 1 - t

    blk = pl.BlockSpec((tt, 128), lambda j, t: (rt(t), j))
    vec = pl.BlockSpec((1, 128), lambda j, t: (0, j))
    mat = pl.BlockSpec((None, 128, 128), lambda j, t: (j, 0, 0))
    h8 = tt // 8
    mshape = jax.ShapeDtypeStruct((LRU_W // 128, 128, 128), F32)
    vshape = jax.ShapeDtypeStruct((1, LRU_W), F32)
    return pl.pallas_call(
        body, name="lru_bwd", grid=(LRU_W // 128, nt),
        in_specs=[pl.BlockSpec((tt, 128), lambda j, t: (rt(t), M_LRU // 128 + j)), blk,
                  pl.BlockSpec((tt, 128), lambda j, t: (rt(t), C_GR // 128 + j)), blk,
                  pl.BlockSpec((8, 128), lambda j, t: (jnp.maximum(rt(t) * h8 - 1, 0), j)),
                  mat, vec, mat, vec, vec],
        out_specs=[blk, blk, mat, mat, vec, vec, vec],
        out_shape=[jax.ShapeDtypeStruct((T, LRU_W), F32), jax.ShapeDtypeStruct((T, LRU_W), BF16),
                   mshape, mshape, vshape, vshape, vshape],
        scratch_shapes=[pltpu.VMEM((8, 128), F32), pltpu.VMEM((8, 128), F32)],
        compiler_params=_cp(("parallel", "arbitrary")))(dmixed, xc, proj, hst, hst, wa, ba, wx, bx, lam)


def _ffn_act_fwd(gconv, up):
    T = up.shape[0]
    tt = _tile(T)
    cb = 512
    nc = D_FF // cb

    def body(g_ref, v_ref, o_ref):
        o_ref[...] = (_gelu(g_ref[...]) * v_ref[...]).astype(BF16)

    blk = pl.BlockSpec((tt, cb), lambda t, j: (t, j))
    return pl.pallas_call(
        body, name="ffn_act_fwd", grid=(T // tt, nc),
        in_specs=[blk, pl.BlockSpec((tt, cb), lambda t, j: (t, nc + j))], out_specs=blk,
        out_shape=jax.ShapeDtypeStruct((T, D_FF), BF16), compiler_params=_cp(("parallel", "parallel")))(gconv, up)


def _ffn_act_bwd(dact, gconv, up):
    T = up.shape[0]
    tt = _tile(T)
    cb = 512
    nc = D_FF // cb

    def body(d_ref, g_ref, v_ref, dg_ref, dv_ref):
        gg, dgg = _gelu_and_grad(g_ref[...])
        d = d_ref[...]
        dg_ref[...] = d * v_ref[...] * dgg
        dv_ref[...] = (d * gg).astype(BF16)

    blk = pl.BlockSpec((tt, cb), lambda t, j: (t, j))
    return pl.pallas_call(
        body, name="ffn_act_bwd", grid=(T // tt, nc),
        in_specs=[blk, blk, pl.BlockSpec((tt, cb), lambda t, j: (t, nc + j))], out_specs=[blk, blk],
        out_shape=[jax.ShapeDtypeStruct((T, D_FF), F32), jax.ShapeDtypeStruct((T, D_FF), BF16)],
        compiler_params=_cp(("parallel", "parallel")))(dact, gconv, up)


def _row_tile(rows, cap):
    best = 8
    for r in range(8, min(rows, cap) + 1, 8):
        if rows % r == 0:
            best = r
    return best


def _adamw(parts, w, m, v, rt, name):
    P, R, C = parts.shape

    def body(p_ref, w_ref, m_ref, v_ref, g_ref, d_ref, mo_ref, vo_ref):
        g = p_ref[0].astype(F32)
        for i in range(1, P):
            g = g + p_ref[i].astype(F32)
        wv = w_ref[...]
        mn = ADAM_B1 * m_ref[...] + (1.0 - ADAM_B1) * g
        vn = ADAM_B2 * v_ref[...] + (1.0 - ADAM_B2) * (g * g)
        m_hat = mn / (1.0 - ADAM_B1 ** ADAM_STEP)
        v_hat = vn / (1.0 - ADAM_B2 ** ADAM_STEP)
        g_ref[...] = g
        d_ref[...] = -ADAM_LR * (m_hat / (jnp.sqrt(v_hat) + ADAM_EPS) + ADAM_WD * wv)
        mo_ref[...] = mn
        vo_ref[...] = vn

    blk = pl.BlockSpec((rt, C), lambda r: (r, 0))
    sh = jax.ShapeDtypeStruct((R, C), F32)
    return pl.pallas_call(
        body, name=name, grid=(R // rt,),
        in_specs=[pl.BlockSpec((P, rt, C), lambda r: (0, r, 0)), blk, blk, blk],
        out_specs=[blk, blk, blk, blk], out_shape=[sh, sh, sh, sh],
        compiler_params=_cp(("parallel",)))(parts, w, m, v)


def _peer(k):
    x, y, c = lax.axis_index("x"), lax.axis_index("y"), lax.axis_index("c")
    px = 1 - x if k & 4 else x
    py = 1 - y if k & 2 else y
    pc = 1 - c if k & 1 else c
    return (px, py, pc), 4 * px + 2 * py + pc


def _all_gather(x, name):
    R, C = x.shape

    def body(x_ref, o_ref, send_sems, recv_sems, local_sem):
        me = 4 * lax.axis_index("x") + 2 * lax.axis_index("y") + lax.axis_index("c")
        mine = pltpu.make_async_copy(x_ref, o_ref.at[me], local_sem)
        mine.start()
        sends = []
        for k in range(1, N_DEV):
            dev, _ = _peer(k)
            cp = pltpu.make_async_remote_copy(src_ref=x_ref, dst_ref=o_ref.at[me], send_sem=send_sems.at[k - 1],
                                              recv_sem=recv_sems.at[k - 1], device_id=dev, device_id_type=MESH_IDS)
            cp.start()
            sends.append(cp)
        for k in range(1, N_DEV):
            dev, idx = _peer(k)
            pltpu.make_async_remote_copy(src_ref=x_ref, dst_ref=o_ref.at[idx], send_sem=send_sems.at[k - 1],
                                         recv_sem=recv_sems.at[k - 1], device_id=dev,
                                         device_id_type=MESH_IDS).wait_recv()
        for cp in sends:
            cp.wait_send()
        mine.wait()

    return pl.pallas_call(
        body, name=name, in_specs=[pl.BlockSpec(memory_space=pl.ANY)], out_specs=pl.BlockSpec(memory_space=pl.ANY),
        out_shape=jax.ShapeDtypeStruct((N_DEV, R, C), x.dtype),
        scratch_shapes=[pltpu.SemaphoreType.DMA((N_DEV - 1,)), pltpu.SemaphoreType.DMA((N_DEV - 1,)),
                        pltpu.SemaphoreType.DMA],
        compiler_params=pltpu.CompilerParams(has_side_effects=True))(x)


HBM_SPEC = pl.BlockSpec(memory_space=pltpu.HBM)
SEM_SPEC = pl.BlockSpec(memory_space=pltpu.SEMAPHORE)
EFFECT = pltpu.SideEffectType.DATAFLOW_SIDE_EFFECTING
OTHER_CHIPS = ((1, 0), (0, 1), (1, 1))


def _tie(x, token):
    return lax.optimization_barrier((x, token))[0]


def _split_start(bufs, plan, n, name):
    nb = len(bufs)

    def body(*refs):
        send_sems, recv_sems, token = refs[nb], refs[nb + 1], refs[2 * nb + 2]
        for i, (src, dst, _, dev) in enumerate(plan(refs[:nb])):
            pltpu.make_async_remote_copy(src_ref=src, dst_ref=dst, send_sem=send_sems.at[i],
                                         recv_sem=recv_sems.at[i], device_id=dev, device_id_type=MESH_IDS).start()
        token[...] = jnp.zeros_like(token)

    outs = pl.pallas_call(
        body, name=name,
        out_shape=(pltpu.SemaphoreType.DMA((n,)), pltpu.SemaphoreType.DMA((n,)),
                   *[pltpu.HBM(b.shape, b.dtype) for b in bufs], jax.ShapeDtypeStruct((8, 128), F32)),
        in_specs=[HBM_SPEC] * nb,
        out_specs=(SEM_SPEC, SEM_SPEC, *[HBM_SPEC] * nb, pl.BlockSpec(memory_space=pltpu.VMEM)),
        input_output_aliases={i: 2 + i for i in range(nb)},
        compiler_params=pltpu.CompilerParams(has_side_effects=EFFECT),
    )(*[pltpu.with_memory_space_constraint(b, pltpu.HBM) for b in bufs])
    return dict(send=outs[0], recv=outs[1], bufs=list(outs[2:2 + nb]), token=outs[2 + nb], plan=plan, n=n)


def _split_wait(st, after, name):
    bufs = st["bufs"]
    nb = len(bufs)
    plan = st["plan"]

    def body(*refs):
        send_sems, recv_sems = refs[nb], refs[nb + 1]
        for i, (src, dst, land, dev) in enumerate(plan(refs[:nb])):
            pltpu.make_async_remote_copy(src_ref=src, dst_ref=dst, send_sem=send_sems.at[i],
                                         recv_sem=recv_sems.at[i], device_id=dev,
                                         device_id_type=MESH_IDS).wait_send()
            pltpu.make_async_remote_copy(src_ref=src, dst_ref=land, send_sem=send_sems.at[i],
                                         recv_sem=recv_sems.at[i], device_id=dev,
                                         device_id_type=MESH_IDS).wait_recv()

    outs = pl.pallas_call(
        body, name=name, out_shape=tuple(pltpu.HBM(b.shape, b.dtype) for b in bufs),
        in_specs=[HBM_SPEC] * nb + [SEM_SPEC, SEM_SPEC, pl.BlockSpec(memory_space=pl.ANY)],
        out_specs=tuple([HBM_SPEC] * nb), input_output_aliases={i: i for i in range(nb)},
        compiler_params=pltpu.CompilerParams(has_side_effects=EFFECT),
    )(*bufs, st["send"], st["recv"], after)
    return list(outs)


def _xyc():
    return lax.axis_index("x"), lax.axis_index("y"), lax.axis_index("c")


def _flip(x, y, a, b):
    return (1 - x if a else x), (1 - y if b else y)


def _ag1_plan(outs):
    x, y, c = _xyc()
    me = 4 * x + 2 * y + c
    copies = []
    for o in outs:
        copies.append((o.at[me], o.at[me], o.at[4 * x + 2 * y + 1 - c], (x, y, 1 - c)))
        for a, b in OTHER_CHIPS:
            px, py = _flip(x, y, a, b)
            copies.append((o.at[me], o.at[me], o.at[4 * px + 2 * py + c], (px, py, c)))
    return copies


def _ag2_plan(outs):
    x, y, c = _xyc()
    copies = []
    for o in outs:
        for a, b in OTHER_CHIPS:
            px, py = _flip(x, y, a, b)
            mine, sibs = 4 * px + 2 * py + c, 4 * px + 2 * py + 1 - c
            copies.append((o.at[mine], o.at[mine], o.at[sibs], (x, y, 1 - c)))
    return copies


def _rs1_plan(refs):
    x, y, c = _xyc()
    copies = []
    for g, land in zip(refs[0::2], refs[1::2]):
        for j in range(4):
            copies.append((g.at[2 * j + 1 - c], land.at[j], land.at[j], (x, y, 1 - c)))
    return copies


def _rs2_plan(refs):
    x, y, c = _xyc()
    mychip = 2 * x + y
    copies = []
    for s, land in zip(refs[0::2], refs[1::2]):
        for a, b in OTHER_CHIPS:
            px, py = _flip(x, y, a, b)
            copies.append((s.at[2 * px + py], land.at[mychip], land.at[2 * px + py], (px, py, c)))
    return copies


def _empty_hbm(shape, dtype):
    return pltpu.with_memory_space_constraint(lax.empty(shape, dtype), pltpu.HBM)


def _gather_bufs(shards, me):
    return [lax.dynamic_update_slice(lax.empty((N_DEV,) + s.shape, s.dtype), s[None], (me, 0, 0)) for s in shards]


def _pair_sum(g, land, cidx, name):
    _, R, C = land.shape
    rt = _row_tile(R, 512)
    g4 = g.reshape(4, 2, R, C)

    def body(c_ref, g_ref, l_ref, o_ref):
        o_ref[...] = (g_ref[...].astype(F32) + l_ref[...].astype(F32)).astype(o_ref.dtype)

    grid_spec = pltpu.PrefetchScalarGridSpec(
        num_scalar_prefetch=1, grid=(4, R // rt),
        in_specs=[pl.BlockSpec((None, None, rt, C), lambda j, r, c_ref: (j, c_ref[0], r, 0)),
                  pl.BlockSpec((None, rt, C), lambda j, r, c_ref: (j, r, 0))],
        out_specs=pl.BlockSpec((None, rt, C), lambda j, r, c_ref: (j, r, 0)))
    return pl.pallas_call(body, name=name, grid_spec=grid_spec, out_shape=jax.ShapeDtypeStruct(land.shape, land.dtype),
                          compiler_params=_cp(("parallel", "parallel")))(cidx, g4, land)


def _no_hook(event, l, after, carry, payload=None):
    return carry


def _layer_fwd(x, W, l, hook=_no_hook):
    T = x.shape[0]
    n = f"l{l}_"
    h1 = _norm_fwd(x, W["norm1"], n + "norm1_fwd")
    proj = _mm_nn(h1, W["win"], F32, n + "mm_in", tn_c=(768,))
    y_pool = _pool_fwd(proj, W["pool_w"], W["pool_b"], W["pool_s"])
    cpre = _conv_fwd(proj, C_QKV, 3 * GDN_W, W["gconv_w"], None, 256, n + "gdn_conv_fwd")
    qkv, bb, gb = _gdn_pre_fwd(cpre, proj, W["alog"], W["dtb"])
    y_gdn, states = _gdn_fwd(qkv, bb, gb, proj, W["gnorm"])
    proj = hook("f_mix", l, y_gdn, proj)
    xc = _conv_fwd(proj, C_XR, LRU_W, W["lconv_w"], W["lconv_b"], 128, n + "lru_conv_fwd")
    y_lru, hst = _lru_fwd(xc, proj, W["wa"], W["ba"], W["wx"], W["bx"], W["lam"])
    y_lru = hook("f_out", l, y_lru, y_lru)
    mixed = jnp.concatenate([y_pool, y_gdn, y_lru], axis=1)
    x1 = _mm_nn(mixed, W["wout"], F32, n + "mm_out", add=x)
    h2 = _norm_fwd(x1, W["norm2"], n + "norm2_fwd")
    up = _mm_up(h2, W["wup"], n + "mm_up")
    gconv = _conv_fwd(up, 0, D_FF, W["fconv_w"], None, 512, n + "ffn_conv_fwd")
    act = _ffn_act_fwd(gconv, up)
    act = hook("f_act", l, act, act)
    x2 = _mm_nn(act, W["wdown"], F32, n + "mm_down", add=x1)
    x2 = hook("f_end", l, x2, x2)
    saved = dict(x=x, h1=h1, proj=proj, cpre=cpre, qkv=qkv, bb=bb, gb=gb, states=states, xc=xc, hst=hst,
                 mixed=mixed, x1=x1, h2=h2, up=up, gconv=gconv, act=act)
    return x2, saved


def _layer_bwd(dx2, W, S, l, hook=_no_hook):
    T = dx2.shape[0]
    n = f"l{l}_"
    dact = _mm_nt(dx2, W["wdown"], F32, n + "mm_down_dx", tk_c=(2048,))
    g_wdown = _mm_tn(S["act"], dx2, BF16, n + "mm_down_dw")
    dgconv, dval = _ffn_act_bwd(dact, S["gconv"], S["up"])
    dgate, g_fconv = _conv_bwd(dgconv, S["up"], 0, W["fconv_w"], 512, n + "ffn_conv_bwd")
    dup = jnp.concatenate([dgate, dval], axis=1)
    ns = W["wup"].shape[2]
    dh2 = _mm_up_t(dup, W["wup"], n + "mm_up_dx")
    g_wup = _mm_dup(S["h2"], dup, ns, n + "mm_up_dw")
    dh2 = hook("b_ffn", l, g_wup, dh2, dict(ffn_down=g_wdown, ffn_up=g_wup))
    dx1, g_norm2 = _norm_bwd(S["x1"], W["norm2"], dh2, dx2, n + "norm2_bwd")
    dmixed = _mm_nt(dx1, W["wout"], F32, n + "mm_out_dx", tk_c=(2048,))
    g_wout = _mm_tn(S["mixed"], dx1, BF16, n + "mm_out_dw")
    dmixed = hook("b_mid", l, g_wout, dmixed)
    proj = S["proj"]
    du, g_pool_w, g_pool_b, g_pool_s = _pool_bwd(dmixed, proj, W["pool_w"], W["pool_b"], W["pool_s"])
    dq, dk, dv, dbb, dgb, dz, g_gnorm = _gdn_bwd(S["qkv"], S["bb"], S["gb"], proj, W["gnorm"], S["states"], dmixed)
    dc, dab, g_alog, g_dtb = _gdn_pre_bwd(dq, dk, dv, S["cpre"], dbb, dgb, proj, W["alog"], W["dtb"])
    dqkv, g_gconv = _conv_bwd(dc, proj, C_QKV, W["gconv_w"], 256, n + "gdn_conv_bwd")
    dxc, dgr, g_wa, g_wx, g_ba, g_bx, g_lam = _lru_bwd(dmixed, S["xc"], proj, S["hst"], W["wa"], W["ba"], W["wx"],
                                                        W["bx"], W["lam"])
    dxr, g_lconv, g_lconv_b = _conv_bwd(dxc, proj, C_XR, W["lconv_w"], 128, n + "lru_conv_bwd", want_db=True)
    dproj = jnp.concatenate([du, dqkv, dz, dab, dxr, dgr, jnp.zeros((T, PCOLS - C_GR - LRU_W), BF16)], axis=1)
    dh1 = _mm_nt(dproj, W["win"], F32, n + "mm_in_dx", tk_c=(768,))
    g_win = _mm_tn(S["h1"], dproj, BF16, n + "mm_in_dw", tn_c=(768,))
    dh1 = hook("b_in", l, g_win, dh1, dict(w_out=g_wout, w_in=g_win))
    dx, g_norm1 = _norm_bwd(S["x"], W["norm1"], dh1, dx1, n + "norm1_bwd")
    big = dict(w_in=g_win, w_out=g_wout, ffn_up=g_wup, ffn_down=g_wdown)
    small = dict(norm1_w=g_norm1[0], pool_w=g_pool_w, pool_b=g_pool_b.reshape(4, 128), pool_scale=g_pool_s[0],
                 gdn_conv_w=g_gconv, gdn_a_log=g_alog[0, :HEADS], gdn_dt_bias=g_dtb[0, :HEADS],
                 gdn_norm_w=g_gnorm[0], lru_conv_w=g_lconv, lru_conv_b=g_lconv_b[0], lru_wa=g_wa, lru_ba=g_ba[0],
                 lru_wx=g_wx, lru_bx=g_bx[0], lru_lambda=g_lam[0], norm2_w=g_norm2[0], ffn_conv_w=g_fconv)
    dx = hook("b_end", l, dx, dx, small)
    return dx, big, small


def _pad_lane(v):
    return jnp.pad(v, (0, 128 - v.shape[0])).reshape(1, 128)


def _layer_weights(l, big, P, conv_full):
    return dict(
        win=big.get("w_in"), wout=big.get("w_out"), wup=big.get("ffn_up"), wdown=big.get("ffn_down"),
        norm1=P["norm1_w"][l].reshape(1, D_MODEL), norm2=P["norm2_w"][l].reshape(1, D_MODEL),
        pool_w=P["pool_w"][l], pool_b=P["pool_b"][l].reshape(1, POOL_W), pool_s=P["pool_scale"][l].reshape(1, POOL_W),
        gconv_w=conv_full["gdn_conv_w"][l], alog=_pad_lane(P["gdn_a_log"][l]), dtb=_pad_lane(P["gdn_dt_bias"][l]),
        gnorm=P["gdn_norm_w"][l].reshape(1, HD),
        lconv_w=conv_full["lru_conv_w"][l], lconv_b=P["lru_conv_b"][l].reshape(1, LRU_W),
        wa=P["lru_wa"][l], ba=P["lru_ba"][l].reshape(1, LRU_W), wx=P["lru_wx"][l],
        bx=P["lru_bx"][l].reshape(1, LRU_W), lam=P["lru_lambda"][l].reshape(1, LRU_W),
        fconv_w=conv_full["ffn_conv_w"][l])


def _local_step(x, target, Ws, final_norm_w, hook=_no_hook):
    saved = []
    for l in range(DEPTH):
        x, s = _layer_fwd(x, Ws[l], l, hook)
        saved.append(s)
    loss, dx, g_final = _loss_head(x, final_norm_w.reshape(1, D_MODEL), target)
    bigs, smalls = [None] * DEPTH, [None] * DEPTH
    for l in reversed(range(DEPTH)):
        dx, bigs[l], smalls[l] = _layer_bwd(dx, Ws[l], saved[l], l, hook)
    return loss, dx, g_final[0], bigs, smalls


SMALL_REPL = ("norm1_w", "pool_w", "pool_b", "pool_scale", "gdn_a_log", "gdn_dt_bias", "gdn_norm_w", "lru_conv_b",
              "lru_wa", "lru_ba", "lru_wx", "lru_bx", "lru_lambda", "norm2_w", "final_norm_w")
SMALL_SHARD = ("gdn_conv_w", "lru_conv_w", "ffn_conv_w")
BIG = ("w_in", "w_out", "ffn_up", "ffn_down")
WEIGHTS = ("norm1_w", "w_in", "pool_w", "pool_b", "pool_scale", "gdn_conv_w", "gdn_a_log", "gdn_dt_bias",
           "gdn_norm_w", "lru_conv_w", "lru_conv_b", "lru_wa", "lru_ba", "lru_wx", "lru_bx", "lru_lambda", "w_out",
           "norm2_w", "ffn_up", "ffn_conv_w", "ffn_down", "final_norm_w")
SEG = 1024


def _pack(arrs):
    pieces, table, off = [], [], 0
    for a in arrs:
        n = a.size
        npad = -(-n // SEG) * SEG
        pieces.append(jnp.pad(a.reshape(-1).astype(F32), (0, npad - n)))
        table.append((off, n, a.shape))
        off += npad
    return jnp.concatenate(pieces).reshape(off // 128, 128), table


def _unpack(buf, table):
    flat = buf.reshape(-1)
    return [flat[off:off + n].reshape(shape) for off, n, shape in table]


def _pad_in(w):
    z1 = jnp.zeros(w.shape[:-1] + (C_XR - AB_ORIG_END,), w.dtype)
    z2 = jnp.zeros(w.shape[:-1] + (PCOLS - C_GR - LRU_W,), w.dtype)
    return jnp.concatenate([w[..., :AB_ORIG_END], z1, w[..., AB_ORIG_END:], z2], axis=-1)


def _unpad_in(w):
    return jnp.concatenate([w[..., :AB_ORIG_END], w[..., C_XR:C_GR + LRU_W]], axis=-1)


def kernel(x, norm1_w, w_in, pool_w, pool_b, pool_scale, gdn_conv_w, gdn_a_log, gdn_dt_bias, gdn_norm_w, lru_conv_w, lru_conv_b, lru_wa, lru_ba, lru_wx, lru_bx, lru_lambda, w_out, norm2_w, ffn_up, ffn_conv_w, ffn_down, final_norm_w, loss_target, m_norm1_w, m_w_in, m_pool_w, m_pool_b, m_pool_scale, m_gdn_conv_w, m_gdn_a_log, m_gdn_dt_bias, m_gdn_norm_w, m_lru_conv_w, m_lru_conv_b, m_lru_wa, m_lru_ba, m_lru_wx, m_lru_bx, m_lru_lambda, m_w_out, m_norm2_w, m_ffn_up, m_ffn_conv_w, m_ffn_down, m_final_norm_w, v_norm1_w, v_w_in, v_pool_w, v_pool_b, v_pool_scale, v_gdn_conv_w, v_gdn_a_log, v_gdn_dt_bias, v_gdn_norm_w, v_lru_conv_w, v_lru_conv_b, v_lru_wa, v_lru_ba, v_lru_wx, v_lru_bx, v_lru_lambda, v_w_out, v_norm2_w, v_ffn_up, v_ffn_conv_w, v_ffn_down, v_final_norm_w):
    loc = dict(locals())
    Wp = {n: loc[n] for n in WEIGHTS}
    Mp = {n: loc["m_" + n] for n in WEIGHTS}
    Vp = {n: loc["v_" + n] for n in WEIGHTS}
    xi, yi, ci = _xyc()
    me = 4 * xi + 2 * yi + ci
    mychip = 2 * xi + yi
    cidx = ci.astype(jnp.int32).reshape(1)
    keys = dict(w_in="win", w_out="wout", ffn_up="wup", ffn_down="wdown")

    def shard2d(d, name, l):
        a = d[name][l]
        return _pad_in(a) if name == "w_in" else a

    def wshard(l, name):
        return shard2d(Wp, name, l).astype(BF16)

    def full2d(name, full):
        return full if name == "ffn_up" else full.reshape(-1, full.shape[2])

    def ag_start(shards, tag, token=None):
        if token is not None:
            shards = [_tie(shards[0], token)] + list(shards[1:])
        bufs = _gather_bufs(shards, me)
        return _split_start(bufs, _ag1_plan, 4 * len(bufs), f"ag1s_{tag}")

    def ag_mid(st, after, tag):
        bufs = _split_wait(st, after, f"ag1w_{tag}")
        return _split_start(bufs, _ag2_plan, 3 * len(bufs), f"ag2s_{tag}")

    def ag_end(st, after, tag):
        return _split_wait(st, after, f"ag2w_{tag}")

    def rs_start(gs, tag):
        bufs = []
        for nm, g in gs.items():
            if nm != "ffn_up":
                g = g.reshape(N_DEV, g.shape[0] // N_DEV, g.shape[1])
            bufs += [g, _empty_hbm((4,) + g.shape[1:], BF16)]
        st = _split_start(bufs, _rs1_plan, 4 * len(gs), f"rs1s_{tag}")
        st["names"] = list(gs)
        return st

    def rs_mid(st, after, tag):
        bufs = _split_wait(st, after, f"rs1w_{tag}")
        out = []
        for i, nm in enumerate(st["names"]):
            s = _pair_sum(bufs[2 * i], bufs[2 * i + 1], cidx, f"pairsum_{nm}_{tag}")
            own = lax.dynamic_slice_in_dim(s, mychip, 1, axis=0)
            out += [s, lax.dynamic_update_slice(lax.empty(s.shape, s.dtype), own, (mychip, 0, 0))]
        st2 = _split_start(out, _rs2_plan, 3 * len(st["names"]), f"rs2s_{tag}")
        st2["names"] = st["names"]
        return st2

    def rs_end(st, after, tag):
        bufs = _split_wait(st, after, f"rs2w_{tag}")
        return dict(zip(st["names"], bufs[1::2]))

    lnames = tuple(n for n in SMALL_REPL if n != "final_norm_w") + SMALL_SHARD

    def small_pack(l, gs, extra):
        return _pack([gs[nm] for nm in lnames] + extra)

    def small_state(d, l, gs):
        arrs = [d[nm][l] if nm in SMALL_REPL else jnp.zeros(gs[nm].shape, F32) for nm in lnames]
        if l == 0:
            arrs += [d["final_norm_w"], jnp.zeros((1,), F32)]
        return _pack(arrs)[0]

    stA = ag_start([wshard(0, "w_in")], "a")
    stB = ag_start([wshard(0, n) for n in BIG[1:]], "b", stA["token"])
    stC = ag_start([wshard(1, n) for n in BIG], "c", stB["token"])
    stA2 = ag_mid(stA, stC["token"], "a")
    (w_in0,) = ag_end(stA2, stA2["token"], "a")

    cbuf, ctable = _pack([Wp[n] for n in SMALL_SHARD])
    call = _all_gather(cbuf, "ag_conv_w")
    parts = [_unpack(call[i], ctable) for i in range(N_DEV)]
    conv_full = {n: jnp.concatenate([parts[i][j] for i in range(N_DEV)], axis=-1) for j, n in enumerate(SMALL_SHARD)}

    Ws = [_layer_weights(l, {}, Wp, conv_full) for l in range(DEPTH)]
    Ws[0]["win"] = full2d("w_in", w_in0)
    st = {}

    def hook(event, l, after, carry, payload=None):
        if event == "f_mix" and l == 0:
            st["b2"] = ag_mid(stB, after, "b")
            return _tie(carry, st["b2"]["token"])
        if event == "f_out" and l == 0:
            for n, b in zip(BIG[1:], ag_end(st["b2"], after, "b")):
                Ws[0][keys[n]] = full2d(n, b)
        if event == "f_act" and l == 0:
            st["c2"] = ag_mid(stC, after, "c")
            return _tie(carry, st["c2"]["token"])
        if event == "f_end" and l == 0:
            for n, b in zip(BIG, ag_end(st["c2"], after, "c")):
                Ws[1][keys[n]] = full2d(n, b)
        if event == "b_ffn":
            st["ffn", l] = rs_start(payload, f"ffn{l}")
            return _tie(carry, st["ffn", l]["token"])
        if event == "b_mid":
            st["ffn2", l] = rs_mid(st["ffn", l], after, f"ffn{l}")
            carry = _tie(carry, st["ffn2", l]["token"])
            if l == 0:
                st["sm1b"] = ag_mid(st["sm1"], after, "sm1")
                carry = _tie(carry, st["sm1b"]["token"])
            return carry
        if event == "b_in":
            st["io", l] = rs_start(payload, f"io{l}")
            if l == 0:
                st["sm1g"] = ag_end(st["sm1b"], after, "sm1")[0]
            return _tie(carry, st["io", l]["token"])
        if event == "b_end" and l == 1:
            st["io2", 1] = rs_mid(st["io", 1], after, "io1")
            gbuf1, st["table1"] = small_pack(1, payload, [])
            st["sm1"] = ag_start([gbuf1], "sm1", st["io2", 1]["token"])
            return _tie(carry, st["sm1"]["token"])
        return carry

    loss, dx, g_final, _, gsmall = _local_step(x[0], loss_target[0], Ws, final_norm_w, hook)

    out_g, out_d, out_m, out_v = {}, {}, {}, {}
    outs4 = (out_g, out_d, out_m, out_v)
    rts = dict(w_in=64, w_out=128, ffn_up=256, ffn_down=128)
    big_res = {}

    def adam_big(l, parts):
        for name, p in parts.items():
            big_res[l, name] = _adamw(p, shard2d(Wp, name, l), shard2d(Mp, name, l), shard2d(Vp, name, l),
                                      rts[name], f"adamw_{name}_{l}")
        return big_res[l, name][0]

    def adam_small(l, gall, gs):
        rs = gall.shape[1]
        return _adamw(gall, small_state(Wp, l, gs), small_state(Mp, l, gs), small_state(Vp, l, gs),
                      _row_tile(rs, 512), f"adamw_small_{l}")

    gbuf0, table0 = small_pack(0, gsmall[0], [g_final, loss[0, :1]])
    sm0 = ag_start([gbuf0], "sm0", st["io", 0]["token"])
    o = adam_big(1, rs_end(st["ffn2", 1], sm0["token"], "ffn1"))
    st["io2", 0] = rs_mid(st["io", 0], o, "io0")
    o = adam_big(1, rs_end(st["io2", 1], st["io2", 0]["token"], "io1"))
    o = adam_big(0, rs_end(st["ffn2", 0], o, "ffn0"))
    small_res = {1: adam_small(1, st["sm1g"], gsmall[1])}
    sm0b = ag_mid(sm0, small_res[1][0], "sm0")
    o = adam_big(0, rs_end(st["io2", 0], sm0b["token"], "io0"))
    small_res[0] = adam_small(0, ag_end(sm0b, o, "sm0")[0], gsmall[0])

    for name in BIG:
        for i, dst in enumerate(outs4):
            a = jnp.stack([big_res[l, name][i] for l in range(DEPTH)])
            dst[name] = _unpad_in(a) if name == "w_in" else a

    unp = {0: [_unpack(r, table0) for r in small_res[0]], 1: [_unpack(r, st["table1"]) for r in small_res[1]]}
    for j, nm in enumerate(lnames):
        if nm in SMALL_REPL:
            for i, dst in enumerate(outs4):
                dst[nm] = jnp.stack([unp[l][i][j] for l in range(DEPTH)])
    for i, dst in enumerate(outs4):
        dst["final_norm_w"] = unp[0][i][len(lnames)]
    loss_total = unp[0][0][len(lnames) + 1][0]

    gsh = []
    for nm in SMALL_SHARD:
        j = lnames.index(nm)
        width = Wp[nm].shape[-1]
        gsh.append(jnp.stack([lax.dynamic_slice_in_dim(unp[l][0][j], me * width, width, axis=1)
                              for l in range(DEPTH)]))
    sbuf, stable = _pack(gsh)
    res = _adamw(sbuf[None], _pack([Wp[n] for n in SMALL_SHARD])[0], _pack([Mp[n] for n in SMALL_SHARD])[0],
                 _pack([Vp[n] for n in SMALL_SHARD])[0], sbuf.shape[0], "adamw_conv_w")
    unp2 = [_unpack(r, stable) for r in res]
    for j, nm in enumerate(SMALL_SHARD):
        for i, dst in enumerate((out_g, out_d, out_m, out_v)):
            dst[nm] = unp2[i][j]

    return (loss_total, dx[None], *[out_g[n] for n in WEIGHTS], *[out_d[n] for n in WEIGHTS],
            *[out_m[n] for n in WEIGHTS], *[out_v[n] for n in WEIGHTS])
```

```python
import functools

import jax
import jax.numpy as jnp
from jax import lax
from jax.experimental import pallas as pl
from jax.experimental.pallas import tpu as pltpu

F32 = jnp.float32
BF16 = jnp.bfloat16
HI = lax.Precision.HIGHEST
MESH_IDS = pl.DeviceIdType.MESH

N_DEV = 8
D_MODEL = 2048
DEPTH = 2
POOL_WINDOWS = (2, 4, 8, 16)
POOL_W = 512
HEADS = 6
HD = 128
GDN_W = HEADS * HD
CHUNK = 64
LRU_W = 768
LRU_C = 8.0
D_FF = 3 * D_MODEL
EPS = 1e-6
IN_COLS = 5132
PCOLS = 5376
C_QKV, C_Z, C_AB, C_XR, C_GR = 512, 2816, 3584, 3712, 4480
AB_ORIG_END = 3596
M_GDN, M_LRU = 512, 1280

ADAM_LR, ADAM_B1, ADAM_B2, ADAM_EPS, ADAM_WD, ADAM_STEP = 0.001, 0.9, 0.999, 1e-08, 0.01, 10

VMEM_LIMIT = 56 * 1024 * 1024


def _cp(sem):
    return pltpu.CompilerParams(dimension_semantics=sem, vmem_limit_bytes=VMEM_LIMIT)


def _mm(a, b, ca=1, cb=0, prec=None, cast=True):
    if cast:
        a = a.astype(BF16)
        b = b.astype(BF16)
    return lax.dot_general(a, b, (((ca,), (cb,)), ((), ())), preferred_element_type=F32, precision=prec)


def _bmm(a, b, ca=2, cb=1, prec=None, cast=True):
    if cast:
        a = a.astype(BF16)
        b = b.astype(BF16)
    return lax.dot_general(a, b, (((ca,), (cb,)), ((0,), (0,))), preferred_element_type=F32, precision=prec)


def _sigmoid(x):
    return 1.0 / (1.0 + jnp.exp(-x))


def _log1p(e):
    u = 1.0 + e
    return jnp.where(u == 1.0, e, jnp.log(u) * e / jnp.where(u == 1.0, 1.0, u - 1.0))


def _softplus(x):
    return jnp.maximum(x, 0.0) + _log1p(jnp.exp(-jnp.abs(x)))


def _expm1(x):
    u = jnp.exp(x)
    um = u - 1.0
    safe = jnp.where((u == 1.0) | (um == -1.0), 1.0, jnp.log(u))
    return jnp.where(u == 1.0, x, jnp.where(um == -1.0, -1.0, um * x / safe))


_G0 = 0.7978845608028654
_G1 = 0.044715


def _gelu(x):
    return 0.5 * x * (1.0 + jnp.tanh(_G0 * (x + _G1 * x * x * x)))


def _gelu_and_grad(x):
    th = jnp.tanh(_G0 * (x + _G1 * x * x * x))
    g = 0.5 * x * (1.0 + th)
    dg = 0.5 * (1.0 + th) + 0.5 * x * (1.0 - th * th) * _G0 * (1.0 + 3.0 * _G1 * x * x)
    return g, dg


def _tile(T):
    return min(T, 512)


def _matmul(a, b, *, grid, a_spec, b_spec, out_shape, out_spec, dims, acc_shape, name, add=None, add_spec=None):
    nk = grid[2]
    has_add = add is not None

    def body(*refs):
        if has_add:
            a_ref, b_ref, add_ref, o_ref, acc_ref = refs
        else:
            a_ref, b_ref, o_ref, acc_ref = refs
            add_ref = None
        k = pl.program_id(2)
        p = lax.dot_general(a_ref[...].astype(BF16), b_ref[...].astype(BF16), (dims, ((), ())),
                            preferred_element_type=F32)

        def finish(r):
            if has_add:
                r = r + add_ref[...]
            o_ref[...] = r.astype(o_ref.dtype)

        if nk == 1:
            finish(p)
        else:
            @pl.when(k == 0)
            def _():
                acc_ref[...] = p

            @pl.when(k > 0)
            def _():
                acc_ref[...] += p

            @pl.when(k == nk - 1)
            def _():
                finish(acc_ref[...])

    in_specs = [a_spec, b_spec] + ([add_spec] if has_add else [])
    args = (a, b) + ((add,) if has_add else ())
    return pl.pallas_call(
        body, name=name, grid=grid, in_specs=in_specs, out_specs=out_spec, out_shape=out_shape,
        scratch_shapes=[pltpu.VMEM(acc_shape, F32)],
        compiler_params=_cp(("parallel", "parallel", "arbitrary")),
    )(*args)


def _pick(n, cands):
    for c in cands:
        if n % c == 0:
            return c
    raise ValueError(f"no tile for {n}")


def _mm_nn(a, b, out_dtype, name, add=None, tn_c=(1024, 768, 512)):
    M, K = a.shape
    N = b.shape[1]
    tm = _pick(M, (1024, 512, 256))
    tn = _pick(N, tn_c)
    tk = _pick(K, (2048, 1536, 1024, 512, 256))
    return _matmul(
        a, b, grid=(M // tm, N // tn, K // tk),
        a_spec=pl.BlockSpec((tm, tk), lambda i, j, k: (i, k)),
        b_spec=pl.BlockSpec((tk, tn), lambda i, j, k: (k, j)),
        out_shape=jax.ShapeDtypeStruct((M, N), out_dtype),
        out_spec=pl.BlockSpec((tm, tn), lambda i, j, k: (i, j)),
        dims=((1,), (0,)), acc_shape=(tm, tn), name=name, add=add,
        add_spec=pl.BlockSpec((tm, tn), lambda i, j, k: (i, j)))


def _mm_nt(a, b, out_dtype, name, tk_c=(2048, 1536, 1024, 768, 512)):
    M, K = a.shape
    N = b.shape[0]
    tm = _pick(M, (1024, 512, 256))
    tn = _pick(N, (1024, 768, 512))
    tk = _pick(K, tk_c)
    return _matmul(
        a, b, grid=(M // tm, N // tn, K // tk),
        a_spec=pl.BlockSpec((tm, tk), lambda i, j, k: (i, k)),
        b_spec=pl.BlockSpec((tn, tk), lambda i, j, k: (j, k)),
        out_shape=jax.ShapeDtypeStruct((M, N), out_dtype),
        out_spec=pl.BlockSpec((tm, tn), lambda i, j, k: (i, j)),
        dims=((1,), (1,)), acc_shape=(tm, tn), name=name)


def _mm_tn(a, b, out_dtype, name, tn_c=(1024, 768, 512)):
    K, M = a.shape
    N = b.shape[1]
    tm = _pick(M, (1024, 768, 512))
    tn = _pick(N, tn_c)
    tk = _pick(K, (1024, 512, 256))
    return _matmul(
        a, b, grid=(M // tm, N // tn, K // tk),
        a_spec=pl.BlockSpec((tk, tm), lambda i, j, k: (k, i)),
        b_spec=pl.BlockSpec((tk, tn), lambda i, j, k: (k, j)),
        out_shape=jax.ShapeDtypeStruct((M, N), out_dtype),
        out_spec=pl.BlockSpec((tm, tn), lambda i, j, k: (i, j)),
        dims=((0,), (0,)), acc_shape=(tm, tn), name=name)


def _mm_up(h, wup, name):
    M, K = h.shape
    ns = wup.shape[2]
    tm = _pick(M, (1024, 512, 256))
    tn = 768
    per = ns // tn
    return _matmul(
        h, wup, grid=(M // tm, N_DEV * per, 1),
        a_spec=pl.BlockSpec((tm, K), lambda i, j, k: (i, 0)),
        b_spec=pl.BlockSpec((None, K, tn), lambda i, j, k: (j // per, 0, j % per)),
        out_shape=jax.ShapeDtypeStruct((M, N_DEV * ns), F32),
        out_spec=pl.BlockSpec((tm, tn), lambda i, j, k: (i, j)),
        dims=((1,), (0,)), acc_shape=(tm, tn), name=name)


def _mm_up_t(dup, wup, name):
    M = dup.shape[0]
    D, ns = wup.shape[1], wup.shape[2]
    tm = _pick(M, (1024, 512, 256))
    tn = 1024
    tk = ns
    return _matmul(
        dup, wup, grid=(M // tm, D // tn, N_DEV),
        a_spec=pl.BlockSpec((tm, tk), lambda i, j, k: (i, k)),
        b_spec=pl.BlockSpec((None, tn, tk), lambda i, j, k: (k, j, 0)),
        out_shape=jax.ShapeDtypeStruct((M, D), F32),
        out_spec=pl.BlockSpec((tm, tn), lambda i, j, k: (i, j)),
        dims=((1,), (1,)), acc_shape=(tm, tn), name=name)


def _mm_dup(h, dup, ns, name):
    K, M = h.shape
    tm = 1024
    tn = 768
    per = ns // tn
    tk = _pick(K, (1024, 512, 256))
    return _matmul(
        h, dup, grid=(M // tm, N_DEV * per, K // tk),
        a_spec=pl.BlockSpec((tk, tm), lambda i, j, k: (k, i)),
        b_spec=pl.BlockSpec((tk, tn), lambda i, j, k: (k, j)),
        out_shape=jax.ShapeDtypeStruct((N_DEV, M, ns), BF16),
        out_spec=pl.BlockSpec((None, tm, tn), lambda i, j, k: (j // per, i, j % per)),
        dims=((0,), (0,)), acc_shape=(tm, tn), name=name)


def _norm_fwd(x, w, name):
    T, D = x.shape
    tt = _tile(T)

    def body(x_ref, w_ref, h_ref):
        xv = x_ref[...]
        r = lax.rsqrt(jnp.mean(xv * xv, axis=1, keepdims=True) + EPS)
        h_ref[...] = (xv * r * w_ref[...]).astype(BF16)

    return pl.pallas_call(
        body, name=name, grid=(T // tt,),
        in_specs=[pl.BlockSpec((tt, D), lambda t: (t, 0)), pl.BlockSpec((1, D), lambda t: (0, 0))],
        out_specs=pl.BlockSpec((tt, D), lambda t: (t, 0)),
        out_shape=jax.ShapeDtypeStruct((T, D), BF16), compiler_params=_cp(("parallel",)))(x, w)


def _norm_bwd(x, w, dh, dres, name):
    T, D = x.shape
    tt = _tile(T)

    def body(x_ref, w_ref, dh_ref, dres_ref, dx_ref, dw_ref):
        t = pl.program_id(0)
        xv = x_ref[...]
        r = lax.rsqrt(jnp.mean(xv * xv, axis=1, keepdims=True) + EPS)
        xh = xv * r
        dh_v = dh_ref[...]
        dxh = dh_v * w_ref[...]
        dx_ref[...] = dres_ref[...] + r * (dxh - xh * jnp.mean(dxh * xh, axis=1, keepdims=True))
        part = jnp.sum(dh_v * xh, axis=0, keepdims=True)

        @pl.when(t == 0)
        def _():
            dw_ref[...] = part

        @pl.when(t > 0)
        def _():
            dw_ref[...] += part

    row = pl.BlockSpec((tt, D), lambda t: (t, 0))
    vec = pl.BlockSpec((1, D), lambda t: (0, 0))
    return pl.pallas_call(
        body, name=name, grid=(T // tt,), in_specs=[row, vec, row, row], out_specs=[row, vec],
        out_shape=[jax.ShapeDtypeStruct((T, D), F32), jax.ShapeDtypeStruct((1, D), F32)],
        compiler_params=_cp(("arbitrary",)))(x, w, dh, dres)


def _loss_head(x, w, target):
    T, D = x.shape
    tt = _tile(T)

    def body(x_ref, w_ref, t_ref, loss_ref, dx_ref, dw_ref):
        t = pl.program_id(0)
        xv = x_ref[...]
        r = lax.rsqrt(jnp.mean(xv * xv, axis=1, keepdims=True) + EPS)
        xh = xv * r
        err = xh * w_ref[...] - t_ref[...]
        lp = 0.5 * jnp.sum(jnp.mean(err * err, axis=1, keepdims=True), axis=0, keepdims=True)
        dy = err * (1.0 / D)
        dxh = dy * w_ref[...]
        dx_ref[...] = r * (dxh - xh * jnp.mean(dxh * xh, axis=1, keepdims=True))
        part = jnp.sum(dy * xh, axis=0, keepdims=True)
        lpb = jnp.broadcast_to(lp, (1, 128))

        @pl.when(t == 0)
        def _():
            dw_ref[...] = part
            loss_ref[...] = lpb

        @pl.when(t > 0)
        def _():
            dw_ref[...] += part
            loss_ref[...] += lpb

    row = pl.BlockSpec((tt, D), lambda t: (t, 0))
    vec = pl.BlockSpec((1, D), lambda t: (0, 0))
    return pl.pallas_call(
        body, name="loss_head", grid=(T // tt,), in_specs=[row, vec, row],
        out_specs=[pl.BlockSpec((1, 128), lambda t: (0, 0)), row, vec],
        out_shape=[jax.ShapeDtypeStruct((1, 128), F32), jax.ShapeDtypeStruct((T, D), F32),
                   jax.ShapeDtypeStruct((1, D), F32)],
        compiler_params=_cp(("arbitrary",)))(x, w, target)


def _conv_fwd(x, col0, C, w, b, cb, name):
    T = x.shape[0]
    K = w.shape[0]
    tt = _tile(T)
    nt, nc, c0 = T // tt, C // cb, col0 // cb
    has_b = b is not None

    def body(*refs):
        if has_b:
            x_ref, halo_ref, w_ref, b_ref, y_ref = refs
        else:
            x_ref, halo_ref, w_ref, y_ref = refs
        t = pl.program_id(1)
        halo = jnp.where(t == 0, 0.0, halo_ref[...])
        xe = jnp.concatenate([halo, x_ref[...]], axis=0)
        acc = xe * w_ref[K - 1:K, :]
        for j in range(K - 1):
            acc = acc + pltpu.roll(xe, K - 1 - j, 0) * w_ref[j:j + 1, :]
        if has_b:
            acc = acc + b_ref[...]
        y_ref[...] = acc[8:, :]

    in_specs = [pl.BlockSpec((tt, cb), lambda j, t: (t, c0 + j)),
                pl.BlockSpec((8, cb), lambda j, t: (jnp.maximum(t * (tt // 8) - 1, 0), c0 + j)),
                pl.BlockSpec((K, cb), lambda j, t: (0, j))]
    args = [x, x, w]
    if has_b:
        in_specs.append(pl.BlockSpec((1, cb), lambda j, t: (0, j)))
        args.append(b)
    return pl.pallas_call(
        body, name=name, grid=(nc, nt), in_specs=in_specs,
        out_specs=pl.BlockSpec((tt, cb), lambda j, t: (t, j)),
        out_shape=jax.ShapeDtypeStruct((T, C), F32), compiler_params=_cp(("parallel", "parallel")))(*args)


def _conv_bwd(dy, x, col0, w, cb, name, want_db=False):
    T, C = dy.shape
    K = w.shape[0]
    tt = _tile(T)
    nt, nc, c0 = T // tt, C // cb, col0 // cb

    def body(*refs):
        if want_db:
            dy_ref, dyn_ref, x_ref, xp_ref, w_ref, dx_ref, dw_ref, db_ref = refs
        else:
            dy_ref, dyn_ref, x_ref, xp_ref, w_ref, dx_ref, dw_ref = refs
        t = pl.program_id(1)
        dyv = dy_ref[...]
        nxt = jnp.where(t == nt - 1, 0.0, dyn_ref[...])
        dye = jnp.concatenate([dyv, nxt], axis=0)
        n = tt + 8
        acc = dye * w_ref[K - 1:K, :]
        for j in range(K - 1):
            acc = acc + pltpu.roll(dye, n - (K - 1 - j), 0) * w_ref[j:j + 1, :]
        dx_ref[...] = acc[:tt, :].astype(dx_ref.dtype)
        prev = jnp.where(t == 0, 0.0, xp_ref[...])
        xe = jnp.concatenate([prev, x_ref[...]], axis=0)

        @pl.when(t == 0)
        def _():
            dw_ref[...] = jnp.zeros_like(dw_ref)
            if want_db:
                db_ref[...] = jnp.zeros_like(db_ref)

        for j in range(K):
            sh = K - 1 - j
            xs = xe[8:, :] if sh == 0 else pltpu.roll(xe, sh, 0)[8:, :]
            dw_ref[j:j + 1, :] += jnp.sum(dyv * xs, axis=0, keepdims=True)
        if want_db:
            db_ref[...] += jnp.sum(dyv, axis=0, keepdims=True)

    h8 = tt // 8
    in_specs = [pl.BlockSpec((tt, cb), lambda j, t: (t, j)),
                pl.BlockSpec((8, cb), lambda j, t: (jnp.minimum((t + 1) * h8, T // 8 - 1), j)),
                pl.BlockSpec((tt, cb), lambda j, t: (t, c0 + j)),
                pl.BlockSpec((8, cb), lambda j, t: (jnp.maximum(t * h8 - 1, 0), c0 + j)),
                pl.BlockSpec((K, cb), lambda j, t: (0, j))]
    out_specs = [pl.BlockSpec((tt, cb), lambda j, t: (t, j)), pl.BlockSpec((K, cb), lambda j, t: (0, j))]
    out_shape = [jax.ShapeDtypeStruct((T, C), BF16), jax.ShapeDtypeStruct((K, C), F32)]
    if want_db:
        out_specs.append(pl.BlockSpec((1, cb), lambda j, t: (0, j)))
        out_shape.append(jax.ShapeDtypeStruct((1, C), F32))
    return pl.pallas_call(
        body, name=name, grid=(nc, nt), in_specs=in_specs, out_specs=out_specs, out_shape=out_shape,
        compiler_params=_cp(("parallel", "arbitrary")))(dy, dy, x, x, w)


def _pool_d(ue, g, pos, tt):
    win = POOL_WINDOWS[g]
    ug = ue[:, g * 128:(g + 1) * 128]
    s = ug
    k = 1
    while k < win:
        s = s + pltpu.roll(s, k, 0)
        k *= 2
    cnt = jnp.minimum(pos + 1, win).astype(F32)
    return s[16:, :] / cnt - ug[16:, :]


def _pool_fwd(proj, pw, pb, ps):
    T = proj.shape[0]
    tt = _tile(T)

    def body(u_ref, halo_ref, w_ref, b_ref, s_ref, y_ref):
        t = pl.program_id(0)
        halo = jnp.where(t == 0, 0.0, halo_ref[...])
        ue = jnp.concatenate([halo, u_ref[...]], axis=0)
        pos = t * tt + lax.broadcasted_iota(jnp.int32, (tt, 1), 0)
        for g in range(4):
            sl = slice(g * 128, (g + 1) * 128)
            d = _pool_d(ue, g, pos, tt)
            yg = _mm(d, w_ref[g]) + b_ref[:, sl]
            y_ref[:, sl] = (yg * s_ref[:, sl]).astype(BF16)

    vec = pl.BlockSpec((1, POOL_W), lambda t: (0, 0))
    return pl.pallas_call(
        body, name="pool_fwd", grid=(T // tt,),
        in_specs=[pl.BlockSpec((tt, POOL_W), lambda t: (t, 0)),
                  pl.BlockSpec((16, POOL_W), lambda t: (jnp.maximum(t * (tt // 16) - 1, 0), 0)),
                  pl.BlockSpec((4, 128, 128), lambda t: (0, 0, 0)), vec, vec],
        out_specs=pl.BlockSpec((tt, POOL_W), lambda t: (t, 0)),
        out_shape=jax.ShapeDtypeStruct((T, POOL_W), BF16), compiler_params=_cp(("parallel",)))(
            proj, proj, pw, pb, ps)


def _pool_bwd(dmixed, proj, pw, pb, ps):
    T = proj.shape[0]
    tt = _tile(T)
    nt = T // tt

    def body(dy_ref, dyn_ref, u_ref, halo_ref, w_ref, b_ref, s_ref, du_ref, dw_ref, db_ref, ds_ref):
        t = pl.program_id(0)
        halo = jnp.where(t == 0, 0.0, halo_ref[...])
        ue = jnp.concatenate([halo, u_ref[...]], axis=0)
        dyv = dy_ref[...]
        nxt = jnp.where(t == nt - 1, 0.0, dyn_ref[...])
        dye = jnp.concatenate([dyv, nxt], axis=0)
        n = tt + 16
        pos = t * tt + lax.broadcasted_iota(jnp.int32, (tt, 1), 0)
        pos_e = t * tt + lax.broadcasted_iota(jnp.int32, (n, 1), 0)

        @pl.when(t == 0)
        def _():
            dw_ref[...] = jnp.zeros_like(dw_ref)
            db_ref[...] = jnp.zeros_like(db_ref)
            ds_ref[...] = jnp.zeros_like(ds_ref)

        for g in range(4):
            win = POOL_WINDOWS[g]
            sl = slice(g * 128, (g + 1) * 128)
            d = _pool_d(ue, g, pos, tt)
            wg = w_ref[g]
            ypre = _mm(d, wg) + b_ref[:, sl]
            sc = s_ref[:, sl]
            ds_ref[:, sl] += jnp.sum(dyv[:, sl] * ypre, axis=0, keepdims=True)
            dyp_e = dye[:, sl] * sc
            dyp = dyp_e[:tt, :]
            db_ref[:, sl] += jnp.sum(dyp, axis=0, keepdims=True)
            dw_ref[g] += _mm(d, dyp, 0, 0)
            dd_e = _mm(dyp_e, wg, 1, 1)
            cnt_e = jnp.minimum(pos_e + 1, win).astype(F32)
            s = dd_e / cnt_e
            k = 1
            while k < win:
                s = s + pltpu.roll(s, n - k, 0)
                k *= 2
            du_ref[:, sl] = (s[:tt, :] - dd_e[:tt, :]).astype(BF16)

    vec = pl.BlockSpec((1, POOL_W), lambda t: (0, 0))
    h16 = tt // 16
    return pl.pallas_call(
        body, name="pool_bwd", grid=(nt,),
        in_specs=[pl.BlockSpec((tt, POOL_W), lambda t: (t, 0)),
                  pl.BlockSpec((16, POOL_W), lambda t: (jnp.minimum((t + 1) * h16, T // 16 - 1), 0)),
                  pl.BlockSpec((tt, POOL_W), lambda t: (t, 0)),
                  pl.BlockSpec((16, POOL_W), lambda t: (jnp.maximum(t * h16 - 1, 0), 0)),
                  pl.BlockSpec((4, 128, 128), lambda t: (0, 0, 0)), vec, vec],
        out_specs=[pl.BlockSpec((tt, POOL_W), lambda t: (t, 0)),
                   pl.BlockSpec((4, 128, 128), lambda t: (0, 0, 0)), vec, vec],
        out_shape=[jax.ShapeDtypeStruct((T, POOL_W), BF16), jax.ShapeDtypeStruct((4, 128, 128), F32),
                   jax.ShapeDtypeStruct((1, POOL_W), F32), jax.ShapeDtypeStruct((1, POOL_W), F32)],
        compiler_params=_cp(("arbitrary",)))(dmixed, dmixed, proj, proj, pw, pb, ps)


def _gdn_pre_fwd(cpre, proj, alog, dtb):
    T = cpre.shape[0]
    tt = _tile(T)

    def body(c_ref, ab_ref, alog_ref, dtb_ref, qkv_ref, bb_ref, gb_ref):
        for p in range(3):
            for h in range(HEADS):
                cc = c_ref[:, (p * HEADS + h) * HD:(p * HEADS + h + 1) * HD]
                s = cc * _sigmoid(cc)
                if p < 2:
                    s = s * lax.rsqrt(jnp.sum(s * s, axis=1, keepdims=True) + EPS)
                if p == 0:
                    s = s * (HD ** -0.5)
                qkv_ref[p, h] = s
        ab = ab_ref[...]
        g = -jnp.exp(alog_ref[...]) * _softplus(ab + dtb_ref[...])
        r64 = lax.broadcasted_iota(jnp.int32, (tt, 1), 0) & (CHUNK - 1)
        k = 1
        while k < CHUNK:
            g = g + jnp.where(r64 >= k, pltpu.roll(g, k, 0), 0.0)
            k *= 2
        sb = _sigmoid(ab)
        for h in range(HEADS):
            gb_ref[h] = jnp.broadcast_to(g[:, h:h + 1], (tt, HD))
            bb_ref[h] = jnp.broadcast_to(sb[:, HEADS + h:HEADS + h + 1], (tt, HD))

    vec = pl.BlockSpec((1, 128), lambda t: (0, 0))
    hb = pl.BlockSpec((HEADS, tt, HD), lambda t: (0, t, 0))
    return pl.pallas_call(
        body, name="gdn_pre_fwd", grid=(T // tt,),
        in_specs=[pl.BlockSpec((tt, 3 * GDN_W), lambda t: (t, 0)),
                  pl.BlockSpec((tt, 128), lambda t: (t, C_AB // 128)), vec, vec],
        out_specs=[pl.BlockSpec((3, HEADS, tt, HD), lambda t: (0, 0, t, 0)), hb, hb],
        out_shape=[jax.ShapeDtypeStruct((3, HEADS, T, HD), F32), jax.ShapeDtypeStruct((HEADS, T, HD), F32),
                   jax.ShapeDtypeStruct((HEADS, T, HD), F32)],
        compiler_params=_cp(("parallel",)))(cpre, proj, alog, dtb)


def _gdn_pre_bwd(dq, dk, dv, cpre, dbb, dgb, proj, alog, dtb):
    T = cpre.shape[0]
    tt = _tile(T)

    def body(dq_ref, dk_ref, dv_ref, c_ref, dbb_ref, dgb_ref, ab_ref, alog_ref, dtb_ref,
             dc_ref, dab_ref, dalog_ref, ddtb_ref):
        t = pl.program_id(0)
        srcs = (dq_ref, dk_ref, dv_ref)
        for p in range(3):
            for h in range(HEADS):
                sl = slice((p * HEADS + h) * HD, (p * HEADS + h + 1) * HD)
                cc = c_ref[:, sl]
                sg = _sigmoid(cc)
                s = cc * sg
                dyv = srcs[p][h]
                if p < 2:
                    r = lax.rsqrt(jnp.sum(s * s, axis=1, keepdims=True) + EPS)
                    y = s * r
                    if p == 0:
                        dyv = dyv * (HD ** -0.5)
                    ds = r * (dyv - y * jnp.sum(dyv * y, axis=1, keepdims=True))
                else:
                    ds = dyv
                dc_ref[:, sl] = ds * sg * (1.0 + cc * (1.0 - sg))
        lane = lax.broadcasted_iota(jnp.int32, (tt, 128), 1)
        dg = jnp.zeros((tt, 128), F32)
        dbeta = jnp.zeros((tt, 128), F32)
        for h in range(HEADS):
            dg = jnp.where(lane == h, dgb_ref[h], dg)
            dbeta = jnp.where(lane == HEADS + h, dbb_ref[h], dbeta)
        r64 = lax.broadcasted_iota(jnp.int32, (tt, 1), 0) & (CHUNK - 1)
        k = 1
        while k < CHUNK:
            dg = dg + jnp.where(r64 < CHUNK - k, pltpu.roll(dg, tt - k, 0), 0.0)
            k *= 2
        ab = ab_ref[...]
        e = jnp.exp(alog_ref[...])
        xx = ab + dtb_ref[...]
        g = -e * _softplus(xx)
        da = jnp.where(lane < HEADS, dg * (-e) * _sigmoid(xx), 0.0)
        pa = jnp.sum(jnp.where(lane < HEADS, dg * g, 0.0), axis=0, keepdims=True)
        pd = jnp.sum(da, axis=0, keepdims=True)

        @pl.when(t == 0)
        def _():
            dalog_ref[...] = pa
            ddtb_ref[...] = pd

        @pl.when(t > 0)
        def _():
            dalog_ref[...] += pa
            ddtb_ref[...] += pd

        sb = _sigmoid(ab)
        dab_ref[...] = jnp.where(lane < HEADS, da, dbeta * sb * (1.0 - sb)).astype(BF16)

    vec = pl.BlockSpec((1, 128), lambda t: (0, 0))
    hb = pl.BlockSpec((HEADS, tt, HD), lambda t: (0, t, 0))
    return pl.pallas_call(
        body, name="gdn_pre_bwd", grid=(T // tt,),
        in_specs=[hb, hb, hb, pl.BlockSpec((tt, 3 * GDN_W), lambda t: (t, 0)), hb, hb,
                  pl.BlockSpec((tt, 128), lambda t: (t, C_AB // 128)), vec, vec],
        out_specs=[pl.BlockSpec((tt, 3 * GDN_W), lambda t: (t, 0)), pl.BlockSpec((tt, 128), lambda t: (t, 0)),
                   vec, vec],
        out_shape=[jax.ShapeDtypeStruct((T, 3 * GDN_W), F32), jax.ShapeDtypeStruct((T, 128), BF16),
                   jax.ShapeDtypeStruct((1, 128), F32), jax.ShapeDtypeStruct((1, 128), F32)],
        compiler_params=_cp(("arbitrary",)))(dq, dk, dv, cpre, dbb, dgb, proj, alog, dtb)


def _tri_inv(a):
    nb = a.shape[0]
    ri = lax.broadcasted_iota(jnp.int32, (nb, CHUNK, CHUNK), 1)
    ci = lax.broadcasted_iota(jnp.int32, (nb, CHUNK, CHUNK), 2)
    n = -a
    p = jnp.where(ri == ci, 1.0, 0.0) + n
    for _ in range(5):
        n = _bmm(n, n, 2, 1, HI, False)
        p = p + _bmm(p, n, 2, 1, HI, False)
    return p


def _gdn_chunk_common(q, k, v, bb3, gb3, tm_saved=None):
    nb = q.shape[0]
    need_t = tm_saved is not None
    beta = bb3[:, :, 0:1]
    gcol = gb3[:, :, 0:1]
    bcol = bb3[:, :, :CHUNK]
    gcm = gb3[:, :, :CHUNK]
    oh = jnp.where(lax.broadcasted_iota(jnp.int32, (nb, CHUNK, HD), 2) == 0, 1.0, 0.0)
    grow = _bmm(oh, gb3, 2, 2, HI, False)
    ri = lax.broadcasted_iota(jnp.int32, (nb, CHUNK, CHUNK), 1)
    ci = lax.broadcasted_iota(jnp.int32, (nb, CHUNK, CHUNK), 2)
    tril, stl = ri >= ci, ri > ci
    dg = gcm - grow
    dec = jnp.where(tril, jnp.exp(jnp.where(tril, dg, 0.0)), 0.0)
    kk = _bmm(k, k, 2, 2)
    qk = _bmm(q, k, 2, 2)
    tm = tm_saved if need_t else _tri_inv(jnp.where(stl, bcol * kk * dec, 0.0))
    gam = jnp.exp(gcol)
    glast = gb3[:, CHUNK - 1:CHUNK, 0:1]
    egl = jnp.exp(glast)
    rw = k * (beta * gam)
    ru = v * beta
    wu = _bmm(tm, jnp.concatenate([rw, ru], axis=2), 2, 1, HI, False)
    kdf = jnp.exp(glast - gcol)
    out = dict(beta=beta, bcol=bcol, tril=tril, stl=stl, dec=dec, kk=kk, qk=qk, tm=tm, gam=gam, egl=egl,
               rw=rw, wu=wu, at=qk * dec, qd=q * gam, kdf=kdf, kd=k * kdf)
    if need_t:
        brow = _bmm(oh, bb3, 2, 2, HI, False)
        triu, stu = ri <= ci, ri < ci
        dect = jnp.where(triu, jnp.exp(jnp.where(triu, -dg, 0.0)), 0.0)
        qkt = _bmm(k, q, 2, 2)
        eye = jnp.where(ri == ci, 1.0, 0.0)
        out.update(brow=brow, triu=triu, stu=stu, dect=dect, qkt=qkt, tmt=_bmm(eye, tm, 2, 2, HI, False),
                   att=qkt * dect)
    return out


def _gdn_rows(T):
    return min(T, 512)


def _gdn_fwd(qkv, bb, gb, proj, nw):
    T = proj.shape[0]
    R = _gdn_rows(T)
    nb = R // CHUNK

    def body(q_ref, k_ref, v_ref, bb_ref, gb_ref, z_ref, nw_ref, y_ref, st_ref, tm_ref,
             s_ref, w_s, u_s, at_s, qd_s, kd_s):
        t = pl.program_id(1)

        @pl.when(t == 0)
        def _():
            s_ref[...] = jnp.zeros_like(s_ref)

        sh = (nb, CHUNK, HD)
        q, k, v = q_ref[...].reshape(sh), k_ref[...].reshape(sh), v_ref[...].reshape(sh)
        c = _gdn_chunk_common(q, k, v, bb_ref[...].reshape(sh), gb_ref[...].reshape(sh))
        tm_ref[...] = c["tm"]
        w_s[...] = c["wu"][:, :, :HD]
        u_s[...] = c["wu"][:, :, HD:]
        at_s[...] = c["at"]
        qd_s[...] = c["qd"]
        kd_s[...] = c["kd"]
        egl = c["egl"]
        nwv = nw_ref[...]
        for n in range(nb):
            s = s_ref[...]
            st_ref[n] = s
            vn = u_s[n] - _mm(w_s[n], s)
            o = _mm(qd_s[n], s) + _mm(at_s[n], vn)
            s_ref[...] = s * egl[n] + _mm(kd_s[n], vn, 0, 0)
            rows = slice(n * CHUNK, (n + 1) * CHUNK)
            zz = z_ref[rows, :]
            on = o * lax.rsqrt(jnp.mean(o * o, axis=1, keepdims=True) + EPS)
            y_ref[rows, :] = (on * nwv * (zz * _sigmoid(zz))).astype(BF16)

    def hm(p):
        return pl.BlockSpec((None, None, R, HD), lambda h, t: (p, h, t, 0))

    hb = pl.BlockSpec((None, R, HD), lambda h, t: (h, t, 0))
    cs = pltpu.VMEM((nb, CHUNK, HD), F32)
    return pl.pallas_call(
        body, name="gdn_fwd", grid=(HEADS, T // R),
        in_specs=[hm(0), hm(1), hm(2), hb, hb, pl.BlockSpec((R, HD), lambda h, t: (t, C_Z // HD + h)),
                  pl.BlockSpec((1, HD), lambda h, t: (0, 0))],
        out_specs=[pl.BlockSpec((R, HD), lambda h, t: (t, h)),
                   pl.BlockSpec((None, nb, HD, HD), lambda h, t: (h, t, 0, 0)),
                   pl.BlockSpec((None, nb, CHUNK, CHUNK), lambda h, t: (h, t, 0, 0))],
        out_shape=[jax.ShapeDtypeStruct((T, GDN_W), BF16), jax.ShapeDtypeStruct((HEADS, T // CHUNK, HD, HD), F32),
                   jax.ShapeDtypeStruct((HEADS, T // CHUNK, CHUNK, CHUNK), F32)],
        scratch_shapes=[pltpu.VMEM((HD, HD), F32), cs, cs, pltpu.VMEM((nb, CHUNK, CHUNK), F32), cs, cs],
        compiler_params=_cp(("parallel", "arbitrary")))(qkv, qkv, qkv, bb, gb, proj, nw)


def _gdn_bwd(qkv, bb, gb, proj, nw, states, tms, dmixed):
    T = proj.shape[0]
    R = _gdn_rows(T)
    nb = R // CHUNK
    ntb = T // R

    def body(q_ref, k_ref, v_ref, bb_ref, gb_ref, z_ref, nw_ref, st_ref, tm_ref, dy_ref,
             dq_ref, dk_ref, dv_ref, dbb_ref, dgb_ref, dz_ref, dnw_ref,
             ds_ref, att_s, do_s, kd_s, vn_s, qd_s, w_s, dvn_s, dkd_s, dgl_s):
        hh = pl.program_id(0)
        t = pl.program_id(1)

        @pl.when(t == 0)
        def _():
            ds_ref[...] = jnp.zeros_like(ds_ref)

        @pl.when((t == 0) & (hh == 0))
        def _():
            dnw_ref[...] = jnp.zeros_like(dnw_ref)

        sh = (nb, CHUNK, HD)
        q, k, v = q_ref[...].reshape(sh), k_ref[...].reshape(sh), v_ref[...].reshape(sh)
        c = _gdn_chunk_common(q, k, v, bb_ref[...].reshape(sh), gb_ref[...].reshape(sh), tm_ref[...])
        w, u = c["wu"][:, :, :HD], c["wu"][:, :, HD:]
        sall = st_ref[...]
        vn = u - _bmm(w, sall, 2, 1)
        o = _bmm(c["qd"], sall, 2, 1) + _bmm(c["at"], vn, 2, 1)
        z = z_ref[...].reshape(sh)
        dy = dy_ref[...].reshape(sh)
        nwv = nw_ref[...].reshape(1, 1, HD)
        rs = lax.rsqrt(jnp.mean(o * o, axis=2, keepdims=True) + EPS)
        on = o * rs
        sg = _sigmoid(z)
        sz = z * sg
        dnw_ref[...] += jnp.sum(jnp.sum(dy * on * sz, axis=0), axis=0, keepdims=True)
        dz_ref[...] = (dy * on * nwv * (sg * (1.0 + z * (1.0 - sg)))).reshape(R, HD).astype(BF16)
        don = dy * nwv * sz
        do = rs * (don - on * jnp.mean(don * on, axis=2, keepdims=True))
        dqd = _bmm(do, sall, 2, 2)
        dat = jnp.where(c["tril"], _bmm(do, vn, 2, 2), 0.0)
        datt = jnp.where(c["triu"], _bmm(vn, do, 2, 2), 0.0)
        att_s[...] = c["att"]
        do_s[...] = do
        kd_s[...] = c["kd"]
        vn_s[...] = vn
        qd_s[...] = c["qd"]
        w_s[...] = w
        egl = c["egl"]
        for n in reversed(range(nb)):
            dso = ds_ref[...]
            dvn_n = _mm(att_s[n], do_s[n]) + _mm(kd_s[n], dso)
            dkd_s[n] = _mm(vn_s[n], dso, 1, 1)
            dgl = egl[n] * jnp.sum(jnp.sum(st_ref[n] * dso, axis=1, keepdims=True), axis=0, keepdims=True)
            dgl_s[n] = jnp.broadcast_to(dgl, (8, HD))
            ds_ref[...] = egl[n] * dso + _mm(qd_s[n], do_s[n], 0, 0) - _mm(w_s[n], dvn_n, 0, 0)
            dvn_s[n] = dvn_n
        dvn = dvn_s[...]
        dkd = dkd_s[...]
        dgl = dgl_s[...][:, 0:1, 0:1]
        dw = -_bmm(dvn, sall, 2, 2)
        dr = _bmm(c["tmt"], jnp.concatenate([dw, dvn], axis=2), 2, 1, HI, False)
        drw, dru = dr[:, :, :HD], dr[:, :, HD:]
        wu = c["wu"]
        da = -jnp.where(c["stl"], _bmm(dr, wu, 2, 2, HI, False), 0.0)
        da_t = -jnp.where(c["stu"], _bmm(wu, dr, 2, 2, HI, False), 0.0)
        beta, gam, dec, dect, kk = c["beta"], c["gam"], c["dec"], c["dect"], c["kk"]
        bcol, brow = c["bcol"], c["brow"]
        dbeta = (jnp.sum(da * kk * dec, axis=2, keepdims=True)
                 + jnp.sum(drw * k * gam + dru * v, axis=2, keepdims=True))
        dkk = bcol * da * dec
        dkk_t = brow * da_t * dect
        e = (bcol * da * kk + dat * c["qk"]) * dec
        e_t = (brow * da_t * kk + datt * c["qkt"]) * dect
        kd = c["kd"]
        dq_ref[...] = (_bmm(dat * dec, k, 2, 1) + dqd * gam).reshape(R, HD)
        dk_ref[...] = (_bmm(datt * dect, q, 2, 1) + _bmm(dkk + dkk_t, k, 2, 1) + dkd * c["kdf"]
                       + drw * (beta * gam)).reshape(R, HD)
        dv_ref[...] = (dru * beta).reshape(R, HD)
        skd = jnp.sum(dkd * kd, axis=2, keepdims=True)
        dgc = (jnp.sum(e, axis=2, keepdims=True) - jnp.sum(e_t, axis=2, keepdims=True)
               + jnp.sum(drw * c["rw"] + dqd * c["qd"], axis=2, keepdims=True) - skd)
        tot = jnp.sum(skd, axis=1, keepdims=True) + dgl
        rowi = lax.broadcasted_iota(jnp.int32, (nb, CHUNK, 1), 1)
        dgc = dgc + jnp.where(rowi == CHUNK - 1, tot, 0.0)
        dbb_ref[...] = jnp.broadcast_to(dbeta, sh).reshape(R, HD)
        dgb_ref[...] = jnp.broadcast_to(dgc, sh).reshape(R, HD)

    def rt(t):
        return ntb - 1 - t

    def hm(p):
        return pl.BlockSpec((None, None, R, HD), lambda h, t: (p, h, rt(t), 0))

    hb = pl.BlockSpec((None, R, HD), lambda h, t: (h, rt(t), 0))
    cs = pltpu.VMEM((nb, CHUNK, HD), F32)
    ob = jax.ShapeDtypeStruct((HEADS, T, HD), F32)
    return pl.pallas_call(
        body, name="gdn_bwd", grid=(HEADS, ntb),
        in_specs=[hm(0), hm(1), hm(2), hb, hb, pl.BlockSpec((R, HD), lambda h, t: (rt(t), C_Z // HD + h)),
                  pl.BlockSpec((1, HD), lambda h, t: (0, 0)),
                  pl.BlockSpec((None, nb, HD, HD), lambda h, t: (h, rt(t), 0, 0)),
                  pl.BlockSpec((None, nb, CHUNK, CHUNK), lambda h, t: (h, rt(t), 0, 0)),
                  pl.BlockSpec((R, HD), lambda h, t: (rt(t), M_GDN // HD + h))],
        out_specs=[hb, hb, hb, hb, hb, pl.BlockSpec((R, HD), lambda h, t: (rt(t), h)),
                   pl.BlockSpec((1, HD), lambda h, t: (0, 0))],
        out_shape=[ob, ob, ob, ob, ob, jax.ShapeDtypeStruct((T, GDN_W), BF16), jax.ShapeDtypeStruct((1, HD), F32)],
        scratch_shapes=[pltpu.VMEM((HD, HD), F32), pltpu.VMEM((nb, CHUNK, CHUNK), F32), cs, cs, cs, cs, cs, cs, cs,
                        pltpu.VMEM((nb, 8, HD), F32)],
        compiler_params=_cp(("arbitrary", "arbitrary")))(qkv, qkv, qkv, bb, gb, proj, nw, states, tms, dmixed)


def _lru_gates(xc, wa, ba, wx, bx, lam, gpos):
    xb = xc.astype(BF16)
    r = _sigmoid(_mm(xb, wa) + ba)
    i = _sigmoid(_mm(xb, wx) + bx)
    sp = _softplus(-lam)
    log_a = -LRU_C * r * sp
    a = jnp.exp(log_a)
    mult = jnp.where(gpos == 0, 1.0, jnp.sqrt(-_expm1(2.0 * log_a)))
    return r, i, sp, a, mult


def _lru_fwd(xc, proj, wa, ba, wx, bx, lam):
    T = xc.shape[0]
    tt = _tile(T)

    def body(xc_ref, gr_ref, wa_ref, ba_ref, wx_ref, bx_ref, lam_ref, y_ref, h_ref, carry_ref):
        t = pl.program_id(1)

        @pl.when(t == 0)
        def _():
            carry_ref[...] = jnp.zeros_like(carry_ref)

        row = lax.broadcasted_iota(jnp.int32, (tt, 1), 0)
        xcv = xc_ref[...]
        r, i, sp, a, mult = _lru_gates(xcv, wa_ref[...], ba_ref[...], wx_ref[...], bx_ref[...], lam_ref[...],
                                       t * tt + row)
        av, bv = a, mult * i * xcv
        k = 1
        while k < tt:
            a_s = jnp.where(row >= k, pltpu.roll(av, k, 0), 1.0)
            b_s = jnp.where(row >= k, pltpu.roll(bv, k, 0), 0.0)
            bv = bv + av * b_s
            av = av * a_s
            k *= 2
        h = bv + av * carry_ref[0:1, :]
        carry_ref[...] = jnp.broadcast_to(h[tt - 1:tt, :], (8, 128))
        h_ref[...] = h
        y_ref[...] = (h * _gelu(gr_ref[...])).astype(BF16)

    blk = pl.BlockSpec((tt, 128), lambda j, t: (t, j))
    vec = pl.BlockSpec((1, 128), lambda j, t: (0, j))
    mat = pl.BlockSpec((None, 128, 128), lambda j, t: (j, 0, 0))
    return pl.pallas_call(
        body, name="lru_fwd", grid=(LRU_W // 128, T // tt),
        in_specs=[blk, pl.BlockSpec((tt, 128), lambda j, t: (t, C_GR // 128 + j)), mat, vec, mat, vec, vec],
        out_specs=[blk, blk],
        out_shape=[jax.ShapeDtypeStruct((T, LRU_W), BF16), jax.ShapeDtypeStruct((T, LRU_W), F32)],
        scratch_shapes=[pltpu.VMEM((8, 128), F32)],
        compiler_params=_cp(("parallel", "arbitrary")))(xc, proj, wa, ba, wx, bx, lam)


def _lru_bwd(dmixed, xc, proj, hst, wa, ba, wx, bx, lam):
    T = xc.shape[0]
    tt = _tile(T)
    nt = T // tt

    def body(dy_ref, xc_ref, gr_ref, h_ref, hp_ref, wa_ref, ba_ref, wx_ref, bx_ref, lam_ref,
             dxc_ref, dgr_ref, dwa_ref, dwx_ref, dba_ref, dbx_ref, dlam_ref, lc_ref, ac_ref):
        t = pl.program_id(1)
        tr = nt - 1 - t

        @pl.when(t == 0)
        def _():
            lc_ref[...] = jnp.zeros_like(lc_ref)
            ac_ref[...] = jnp.zeros_like(ac_ref)
            dwa_ref[...] = jnp.zeros_like(dwa_ref)
            dwx_ref[...] = jnp.zeros_like(dwx_ref)
            dba_ref[...] = jnp.zeros_like(dba_ref)
            dbx_ref[...] = jnp.zeros_like(dbx_ref)
            dlam_ref[...] = jnp.zeros_like(dlam_ref)

        row = lax.broadcasted_iota(jnp.int32, (tt, 1), 0)
        gpos = tr * tt + row
        xcv = xc_ref[...]
        wav, wxv, lamv = wa_ref[...], wx_ref[...], lam_ref[...]
        r, i, sp, a, mult = _lru_gates(xcv, wav, ba_ref[...], wxv, bx_ref[...], lamv, gpos)
        h = h_ref[...]
        dy = dy_ref[...]
        gg, dgg = _gelu_and_grad(gr_ref[...])
        dgr_ref[...] = (dy * h * dgg).astype(BF16)
        bv = dy * gg
        cv = jnp.where(row < tt - 1, pltpu.roll(a, tt - 1, 0), ac_ref[0:1, :])
        k = 1
        while k < tt:
            c_s = jnp.where(row < tt - k, pltpu.roll(cv, tt - k, 0), 1.0)
            b_s = jnp.where(row < tt - k, pltpu.roll(bv, tt - k, 0), 0.0)
            bv = bv + cv * b_s
            cv = cv * c_s
            k *= 2
        lm = bv + cv * lc_ref[0:1, :]
        lc_ref[...] = jnp.broadcast_to(lm[0:1, :], (8, 128))
        ac_ref[...] = jnp.broadcast_to(a[0:1, :], (8, 128))
        hp = jnp.where(tr == 0, 0.0, hp_ref[...])
        hs = pltpu.roll(jnp.concatenate([hp, h], axis=0), 1, 0)[8:, :]
        da = lm * hs
        dmult = lm * i * xcv
        di = lm * mult * xcv
        dxc = lm * mult * i
        dlog_a = a * da - jnp.where(gpos == 0, 0.0, dmult * a * a / mult)
        dr = dlog_a * (-LRU_C * sp)
        dsp = jnp.sum(dlog_a * (-LRU_C * r), axis=0, keepdims=True)
        dlam_ref[...] += dsp * (-_sigmoid(-lamv))
        dpr = dr * r * (1.0 - r)
        dpi = di * i * (1.0 - i)
        dba_ref[...] += jnp.sum(dpr, axis=0, keepdims=True)
        dbx_ref[...] += jnp.sum(dpi, axis=0, keepdims=True)
        dwa_ref[...] += _mm(xcv, dpr, 0, 0)
        dwx_ref[...] += _mm(xcv, dpi, 0, 0)
        dxc_ref[...] = dxc + _mm(dpr, wav, 1, 1) + _mm(dpi, wxv, 1, 1)

    def rt(t):
        return nt - 1 - t

    blk = pl.BlockSpec((tt, 128), lambda j, t: (rt(t), j))
    vec = pl.BlockSpec((1, 128), lambda j, t: (0, j))
    mat = pl.BlockSpec((None, 128, 128), lambda j, t: (j, 0, 0))
    h8 = tt // 8
    mshape = jax.ShapeDtypeStruct((LRU_W // 128, 128, 128), F32)
    vshape = jax.ShapeDtypeStruct((1, LRU_W), F32)
    return pl.pallas_call(
        body, name="lru_bwd", grid=(LRU_W // 128, nt),
        in_specs=[pl.BlockSpec((tt, 128), lambda j, t: (rt(t), M_LRU // 128 + j)), blk,
                  pl.BlockSpec((tt, 128), lambda j, t: (rt(t), C_GR // 128 + j)), blk,
                  pl.BlockSpec((8, 128), lambda j, t: (jnp.maximum(rt(t) * h8 - 1, 0), j)),
                  mat, vec, mat, vec, vec],
        out_specs=[blk, blk, mat, mat, vec, vec, vec],
        out_shape=[jax.ShapeDtypeStruct((T, LRU_W), F32), jax.ShapeDtypeStruct((T, LRU_W), BF16),
                   mshape, mshape, vshape, vshape, vshape],
        scratch_shapes=[pltpu.VMEM((8, 128), F32), pltpu.VMEM((8, 128), F32)],
        compiler_params=_cp(("parallel", "arbitrary")))(dmixed, xc, proj, hst, hst, wa, ba, wx, bx, lam)


def _ffn_act_fwd(gconv, up):
    T = up.shape[0]
    tt = _tile(T)
    cb = 512
    nc = D_FF // cb

    def body(g_ref, v_ref, o_ref):
        o_ref[...] = (_gelu(g_ref[...]) * v_ref[...]).astype(BF16)

    blk = pl.BlockSpec((tt, cb), lambda t, j: (t, j))
    return pl.pallas_call(
        body, name="ffn_act_fwd", grid=(T // tt, nc),
        in_specs=[blk, pl.BlockSpec((tt, cb), lambda t, j: (t, nc + j))], out_specs=blk,
        out_shape=jax.ShapeDtypeStruct((T, D_FF), BF16), compiler_params=_cp(("parallel", "parallel")))(gconv, up)


def _ffn_act_bwd(dact, gconv, up):
    T = up.shape[0]
    tt = _tile(T)
    cb = 512
    nc = D_FF // cb

    def body(d_ref, g_ref, v_ref, dg_ref, dv_ref):
        gg, dgg = _gelu_and_grad(g_ref[...])
        d = d_ref[...]
        dg_ref[...] = d * v_ref[...] * dgg
        dv_ref[...] = (d * gg).astype(BF16)

    blk = pl.BlockSpec((tt, cb), lambda t, j: (t, j))
    return pl.pallas_call(
        body, name="ffn_act_bwd", grid=(T // tt, nc),
        in_specs=[blk, blk, pl.BlockSpec((tt, cb), lambda t, j: (t, nc + j))], out_specs=[blk, blk],
        out_shape=[jax.ShapeDtypeStruct((T, D_FF), F32), jax.ShapeDtypeStruct((T, D_FF), BF16)],
        compiler_params=_cp(("parallel", "parallel")))(dact, gconv, up)


def _row_tile(rows, cap):
    best = 8
    for r in range(8, min(rows, cap) + 1, 8):
        if rows % r == 0:
            best = r
    return best


def _adamw(parts, w, m, v, rt, name):
    P, R, C = parts.shape

    def body(p_ref, w_ref, m_ref, v_ref, g_ref, d_ref, mo_ref, vo_ref):
        g = p_ref[0].astype(F32)
        for i in range(1, P):
            g = g + p_ref[i].astype(F32)
        wv = w_ref[...]
        mn = ADAM_B1 * m_ref[...] + (1.0 - ADAM_B1) * g
        vn = ADAM_B2 * v_ref[...] + (1.0 - ADAM_B2) * (g * g)
        m_hat = mn / (1.0 - ADAM_B1 ** ADAM_STEP)
        v_hat = vn / (1.0 - ADAM_B2 ** ADAM_STEP)
        g_ref[...] = g
        d_ref[...] = -ADAM_LR * (m_hat / (jnp.sqrt(v_hat) + ADAM_EPS) + ADAM_WD * wv)
        mo_ref[...] = mn
        vo_ref[...] = vn

    blk = pl.BlockSpec((rt, C), lambda r: (r, 0))
    sh = jax.ShapeDtypeStruct((R, C), F32)
    return pl.pallas_call(
        body, name=name, grid=(R // rt,),
        in_specs=[pl.BlockSpec((P, rt, C), lambda r: (0, r, 0)), blk, blk, blk],
        out_specs=[blk, blk, blk, blk], out_shape=[sh, sh, sh, sh],
        compiler_params=_cp(("parallel",)))(parts, w, m, v)


def _peer(k):
    x, y, c = lax.axis_index("x"), lax.axis_index("y"), lax.axis_index("c")
    px = 1 - x if k & 4 else x
    py = 1 - y if k & 2 else y
    pc = 1 - c if k & 1 else c
    return (px, py, pc), 4 * px + 2 * py + pc


def _all_gather(x, name):
    R, C = x.shape

    def body(x_ref, o_ref, send_sems, recv_sems, local_sem):
        me = 4 * lax.axis_index("x") + 2 * lax.axis_index("y") + lax.axis_index("c")
        mine = pltpu.make_async_copy(x_ref, o_ref.at[me], local_sem)
        mine.start()
        sends = []
        for k in range(1, N_DEV):
            dev, _ = _peer(k)
            cp = pltpu.make_async_remote_copy(src_ref=x_ref, dst_ref=o_ref.at[me], send_sem=send_sems.at[k - 1],
                                              recv_sem=recv_sems.at[k - 1], device_id=dev, device_id_type=MESH_IDS)
            cp.start()
            sends.append(cp)
        for k in range(1, N_DEV):
            dev, idx = _peer(k)
            pltpu.make_async_remote_copy(src_ref=x_ref, dst_ref=o_ref.at[idx], send_sem=send_sems.at[k - 1],
                                         recv_sem=recv_sems.at[k - 1], device_id=dev,
                                         device_id_type=MESH_IDS).wait_recv()
        for cp in sends:
            cp.wait_send()
        mine.wait()

    return pl.pallas_call(
        body, name=name, in_specs=[pl.BlockSpec(memory_space=pl.ANY)], out_specs=pl.BlockSpec(memory_space=pl.ANY),
        out_shape=jax.ShapeDtypeStruct((N_DEV, R, C), x.dtype),
        scratch_shapes=[pltpu.SemaphoreType.DMA((N_DEV - 1,)), pltpu.SemaphoreType.DMA((N_DEV - 1,)),
                        pltpu.SemaphoreType.DMA],
        compiler_params=pltpu.CompilerParams(has_side_effects=True))(x)


HBM_SPEC = pl.BlockSpec(memory_space=pltpu.HBM)
SEM_SPEC = pl.BlockSpec(memory_space=pltpu.SEMAPHORE)
EFFECT = pltpu.SideEffectType.DATAFLOW_SIDE_EFFECTING
OTHER_CHIPS = ((1, 0), (0, 1), (1, 1))


def _split_start(bufs, plan, n, name):
    nb = len(bufs)

    def body(*refs):
        send_sems, recv_sems, token = refs[nb], refs[nb + 1], refs[2 * nb + 2]
        for i, (src, dst, _, dev) in enumerate(plan(refs[:nb])):
            pltpu.make_async_remote_copy(src_ref=src, dst_ref=dst, send_sem=send_sems.at[i],
                                         recv_sem=recv_sems.at[i], device_id=dev, device_id_type=MESH_IDS).start()
        token[...] = jnp.zeros_like(token)

    outs = pl.pallas_call(
        body, name=name,
        out_shape=(pltpu.SemaphoreType.DMA((n,)), pltpu.SemaphoreType.DMA((n,)),
                   *[pltpu.HBM(b.shape, b.dtype) for b in bufs], jax.ShapeDtypeStruct((8, 128), F32)),
        in_specs=[HBM_SPEC] * nb,
        out_specs=(SEM_SPEC, SEM_SPEC, *[HBM_SPEC] * nb, pl.BlockSpec(memory_space=pltpu.VMEM)),
        input_output_aliases={i: 2 + i for i in range(nb)},
        compiler_params=pltpu.CompilerParams(has_side_effects=EFFECT),
    )(*[pltpu.with_memory_space_constraint(b, pltpu.HBM) for b in bufs])
    return dict(send=outs[0], recv=outs[1], bufs=list(outs[2:2 + nb]), token=outs[2 + nb], plan=plan, n=n)


def _split_wait(st, after, name):
    bufs = st["bufs"]
    nb = len(bufs)
    plan = st["plan"]
    afters = list(after) if isinstance(after, (list, tuple)) else [after]

    def body(*refs):
        send_sems, recv_sems = refs[nb], refs[nb + 1]
        for i, (src, dst, land, dev) in enumerate(plan(refs[:nb])):
            pltpu.make_async_remote_copy(src_ref=src, dst_ref=dst, send_sem=send_sems.at[i],
                                         recv_sem=recv_sems.at[i], device_id=dev,
                                         device_id_type=MESH_IDS).wait_send()
            pltpu.make_async_remote_copy(src_ref=src, dst_ref=land, send_sem=send_sems.at[i],
                                         recv_sem=recv_sems.at[i], device_id=dev,
                                         device_id_type=MESH_IDS).wait_recv()

    outs = pl.pallas_call(
        body, name=name, out_shape=tuple(pltpu.HBM(b.shape, b.dtype) for b in bufs),
        in_specs=[HBM_SPEC] * nb + [SEM_SPEC, SEM_SPEC] + [pl.BlockSpec(memory_space=pl.ANY)] * len(afters),
        out_specs=tuple([HBM_SPEC] * nb), input_output_aliases={i: i for i in range(nb)},
        compiler_params=pltpu.CompilerParams(has_side_effects=EFFECT),
    )(*bufs, st["send"], st["recv"], *afters)
    return list(outs)


def _xyc():
    return lax.axis_index("x"), lax.axis_index("y"), lax.axis_index("c")


def _flip(x, y, a, b):
    return (1 - x if a else x), (1 - y if b else y)


def _ag1_plan(outs):
    x, y, c = _xyc()
    me = 4 * x + 2 * y + c
    copies = []
    for o in outs:
        copies.append((o.at[me], o.at[me], o.at[4 * x + 2 * y + 1 - c], (x, y, 1 - c)))
        for a, b in OTHER_CHIPS:
            px, py = _flip(x, y, a, b)
            copies.append((o.at[me], o.at[me], o.at[4 * px + 2 * py + c], (px, py, c)))
    return copies


def _ag2_plan(outs):
    x, y, c = _xyc()
    copies = []
    for o in outs:
        for a, b in OTHER_CHIPS:
            px, py = _flip(x, y, a, b)
            mine, sibs = 4 * px + 2 * py + c, 4 * px + 2 * py + 1 - c
            copies.append((o.at[mine], o.at[mine], o.at[sibs], (x, y, 1 - c)))
    return copies


def _rs1_plan(refs):
    x, y, c = _xyc()
    copies = []
    for g, land in zip(refs[0::2], refs[1::2]):
        for j in range(4):
            copies.append((g.at[2 * j + 1 - c], land.at[j], land.at[j], (x, y, 1 - c)))
    return copies


def _rs2_plan(refs):
    x, y, c = _xyc()
    mychip = 2 * x + y
    copies = []
    for s, land in zip(refs[0::2], refs[1::2]):
        for a, b in OTHER_CHIPS:
            px, py = _flip(x, y, a, b)
            copies.append((s.at[2 * px + py], land.at[mychip], land.at[2 * px + py], (px, py, c)))
    return copies


def _empty_hbm(shape, dtype):
    return pltpu.with_memory_space_constraint(lax.empty(shape, dtype), pltpu.HBM)


def _gather_bufs(shards, me):
    return [lax.dynamic_update_slice(lax.empty((N_DEV,) + s.shape, s.dtype), s[None], (me, 0, 0)) for s in shards]


def _pair_sum(g, land, cidx, name):
    _, R, C = land.shape
    rt = _row_tile(R, 512)
    g4 = g.reshape(4, 2, R, C)

    def body(c_ref, g_ref, l_ref, o_ref):
        o_ref[...] = (g_ref[...].astype(F32) + l_ref[...].astype(F32)).astype(o_ref.dtype)

    grid_spec = pltpu.PrefetchScalarGridSpec(
        num_scalar_prefetch=1, grid=(4, R // rt),
        in_specs=[pl.BlockSpec((None, None, rt, C), lambda j, r, c_ref: (j, c_ref[0], r, 0)),
                  pl.BlockSpec((None, rt, C), lambda j, r, c_ref: (j, r, 0))],
        out_specs=pl.BlockSpec((None, rt, C), lambda j, r, c_ref: (j, r, 0)))
    return pl.pallas_call(body, name=name, grid_spec=grid_spec, out_shape=jax.ShapeDtypeStruct(land.shape, land.dtype),
                          compiler_params=_cp(("parallel", "parallel")))(cidx, g4, land)


def _no_hook(event, l, after, payload=None):
    return None


def _tie(x, token):
    if token is None:
        return x

    def body(x_ref, t_ref, o_ref):
        del x_ref, t_ref, o_ref

    anyspec = pl.BlockSpec(memory_space=pl.ANY)
    return pl.pallas_call(body, name="tie", in_specs=[anyspec, anyspec], out_specs=anyspec,
                          out_shape=jax.ShapeDtypeStruct(x.shape, x.dtype), input_output_aliases={0: 0})(x, token)


def _layer_fwd(x, W, l, hook=_no_hook):
    T = x.shape[0]
    n = f"l{l}_"
    h1 = _norm_fwd(x, W["norm1"], n + "norm1_fwd")
    proj = _mm_nn(h1, W["win"], F32, n + "mm_in", tn_c=(768,))
    y_pool = _pool_fwd(proj, W["pool_w"], W["pool_b"], W["pool_s"])
    cpre = _conv_fwd(proj, C_QKV, 3 * GDN_W, W["gconv_w"], None, 256, n + "gdn_conv_fwd")
    qkv, bb, gb = _gdn_pre_fwd(cpre, proj, W["alog"], W["dtb"])
    y_gdn, states, tms = _gdn_fwd(qkv, bb, gb, proj, W["gnorm"])
    lconv_w = _tie(W["lconv_w"], hook("f_mix", l, y_gdn))
    xc = _conv_fwd(proj, C_XR, LRU_W, lconv_w, W["lconv_b"], 128, n + "lru_conv_fwd")
    y_lru, hst = _lru_fwd(xc, proj, W["wa"], W["ba"], W["wx"], W["bx"], W["lam"])
    hook("f_out", l, y_lru)
    mixed = jnp.concatenate([y_pool, y_gdn, y_lru], axis=1)
    x1 = _mm_nn(mixed, W["wout"], F32, n + "mm_out", add=x)
    h2 = _norm_fwd(x1, W["norm2"], n + "norm2_fwd")
    up = _mm_up(h2, W["wup"], n + "mm_up")
    gconv = _conv_fwd(up, 0, D_FF, W["fconv_w"], None, 512, n + "ffn_conv_fwd")
    act = _ffn_act_fwd(gconv, up)
    act = _tie(act, hook("f_act", l, act))
    x2 = _mm_nn(act, W["wdown"], F32, n + "mm_down", add=x1)
    hook("f_end", l, x2)
    saved = dict(x=x, h1=h1, proj=proj, cpre=cpre, qkv=qkv, bb=bb, gb=gb, states=states, tms=tms, xc=xc, hst=hst,
                 mixed=mixed, x1=x1, h2=h2, up=up, gconv=gconv, act=act)
    return x2, saved


def _layer_bwd(dx2, W, S, l, hook=_no_hook):
    T = dx2.shape[0]
    n = f"l{l}_"
    dact = _mm_nt(dx2, W["wdown"], F32, n + "mm_down_dx", tk_c=(2048,))
    g_wdown = _mm_tn(S["act"], dx2, BF16, n + "mm_down_dw")
    dgconv, dval = _ffn_act_bwd(dact, S["gconv"], S["up"])
    dgate, g_fconv = _conv_bwd(dgconv, S["up"], 0, W["fconv_w"], 512, n + "ffn_conv_bwd")
    dup = jnp.concatenate([dgate, dval], axis=1)
    ns = W["wup"].shape[2]
    dh2 = _mm_up_t(dup, W["wup"], n + "mm_up_dx")
    g_wup = _mm_dup(S["h2"], dup, ns, n + "mm_up_dw")
    tok = hook("b_ffn", l, g_wup, dict(ffn_down=g_wdown, ffn_up=g_wup))
    dx1, g_norm2 = _norm_bwd(S["x1"], _tie(W["norm2"], tok), dh2, dx2, n + "norm2_bwd")
    dmixed = _mm_nt(dx1, W["wout"], F32, n + "mm_out_dx", tk_c=(2048,))
    g_wout = _mm_tn(S["mixed"], dx1, BF16, n + "mm_out_dw")
    tok = hook("b_mid", l, g_wout)
    proj = S["proj"]
    du, g_pool_w, g_pool_b, g_pool_s = _pool_bwd(dmixed, proj, W["pool_w"], W["pool_b"], _tie(W["pool_s"], tok))
    dq, dk, dv, dbb, dgb, dz, g_gnorm = _gdn_bwd(S["qkv"], S["bb"], S["gb"], proj, _tie(W["gnorm"], tok),
                                                 S["states"], S["tms"], dmixed)
    dc, dab, g_alog, g_dtb = _gdn_pre_bwd(dq, dk, dv, S["cpre"], dbb, dgb, proj, W["alog"], W["dtb"])
    dqkv, g_gconv = _conv_bwd(dc, proj, C_QKV, W["gconv_w"], 256, n + "gdn_conv_bwd")
    dxc, dgr, g_wa, g_wx, g_ba, g_bx, g_lam = _lru_bwd(dmixed, S["xc"], proj, S["hst"], W["wa"], W["ba"], W["wx"],
                                                        W["bx"], _tie(W["lam"], tok))
    dxr, g_lconv, g_lconv_b = _conv_bwd(dxc, proj, C_XR, W["lconv_w"], 128, n + "lru_conv_bwd", want_db=True)
    dproj = jnp.concatenate([du, dqkv, dz, dab, dxr, dgr, jnp.zeros((T, PCOLS - C_GR - LRU_W), BF16)], axis=1)
    dh1 = _mm_nt(dproj, W["win"], F32, n + "mm_in_dx", tk_c=(768,))
    g_win = _mm_tn(S["h1"], dproj, BF16, n + "mm_in_dw", tn_c=(768,))
    tok = hook("b_in", l, g_win, dict(w_out=g_wout, w_in=g_win))
    dx, g_norm1 = _norm_bwd(S["x"], _tie(W["norm1"], tok), dh1, dx1, n + "norm1_bwd")
    big = dict(w_in=g_win, w_out=g_wout, ffn_up=g_wup, ffn_down=g_wdown)
    small = dict(norm1_w=g_norm1[0], pool_w=g_pool_w, pool_b=g_pool_b.reshape(4, 128), pool_scale=g_pool_s[0],
                 gdn_conv_w=g_gconv, gdn_a_log=g_alog[0, :HEADS], gdn_dt_bias=g_dtb[0, :HEADS],
                 gdn_norm_w=g_gnorm[0], lru_conv_w=g_lconv, lru_conv_b=g_lconv_b[0], lru_wa=g_wa, lru_ba=g_ba[0],
                 lru_wx=g_wx, lru_bx=g_bx[0], lru_lambda=g_lam[0], norm2_w=g_norm2[0], ffn_conv_w=g_fconv)
    dx = _tie(dx, hook("b_end", l, dx, small))
    return dx, big, small


def _pad_lane(v):
    return jnp.pad(v, (0, 128 - v.shape[0])).reshape(1, 128)


def _layer_weights(l, big, P, conv_full):
    return dict(
        win=big.get("w_in"), wout=big.get("w_out"), wup=big.get("ffn_up"), wdown=big.get("ffn_down"),
        norm1=P["norm1_w"][l].reshape(1, D_MODEL), norm2=P["norm2_w"][l].reshape(1, D_MODEL),
        pool_w=P["pool_w"][l], pool_b=P["pool_b"][l].reshape(1, POOL_W), pool_s=P["pool_scale"][l].reshape(1, POOL_W),
        gconv_w=conv_full["gdn_conv_w"][l], alog=_pad_lane(P["gdn_a_log"][l]), dtb=_pad_lane(P["gdn_dt_bias"][l]),
        gnorm=P["gdn_norm_w"][l].reshape(1, HD),
        lconv_w=conv_full["lru_conv_w"][l], lconv_b=P["lru_conv_b"][l].reshape(1, LRU_W),
        wa=P["lru_wa"][l], ba=P["lru_ba"][l].reshape(1, LRU_W), wx=P["lru_wx"][l],
        bx=P["lru_bx"][l].reshape(1, LRU_W), lam=P["lru_lambda"][l].reshape(1, LRU_W),
        fconv_w=conv_full["ffn_conv_w"][l])


def _local_step(x, target, Ws, final_norm_w, hook=_no_hook):
    saved = []
    for l in range(DEPTH):
        x, s = _layer_fwd(x, Ws[l], l, hook)
        saved.append(s)
    loss, dx, g_final = _loss_head(x, final_norm_w.reshape(1, D_MODEL), target)
    bigs, smalls = [None] * DEPTH, [None] * DEPTH
    for l in reversed(range(DEPTH)):
        dx, bigs[l], smalls[l] = _layer_bwd(dx, Ws[l], saved[l], l, hook)
    return loss, dx, g_final[0], bigs, smalls


SMALL_REPL = ("norm1_w", "pool_w", "pool_b", "pool_scale", "gdn_a_log", "gdn_dt_bias", "gdn_norm_w", "lru_conv_b",
              "lru_wa", "lru_ba", "lru_wx", "lru_bx", "lru_lambda", "norm2_w", "final_norm_w")
SMALL_SHARD = ("gdn_conv_w", "lru_conv_w", "ffn_conv_w")
BIG = ("w_in", "w_out", "ffn_up", "ffn_down")
WEIGHTS = ("norm1_w", "w_in", "pool_w", "pool_b", "pool_scale", "gdn_conv_w", "gdn_a_log", "gdn_dt_bias",
           "gdn_norm_w", "lru_conv_w", "lru_conv_b", "lru_wa", "lru_ba", "lru_wx", "lru_bx", "lru_lambda", "w_out",
           "norm2_w", "ffn_up", "ffn_conv_w", "ffn_down", "final_norm_w")
SEG = 1024


def _pack(arrs):
    pieces, table, off = [], [], 0
    for a in arrs:
        n = a.size
        npad = -(-n // SEG) * SEG
        pieces.append(jnp.pad(a.reshape(-1).astype(F32), (0, npad - n)))
        table.append((off, n, a.shape))
        off += npad
    return jnp.concatenate(pieces).reshape(off // 128, 128), table


def _unpack(buf, table):
    flat = buf.reshape(-1)
    return [flat[off:off + n].reshape(shape) for off, n, shape in table]


def _pad_in(w):
    z1 = jnp.zeros(w.shape[:-1] + (C_XR - AB_ORIG_END,), w.dtype)
    z2 = jnp.zeros(w.shape[:-1] + (PCOLS - C_GR - LRU_W,), w.dtype)
    return jnp.concatenate([w[..., :AB_ORIG_END], z1, w[..., AB_ORIG_END:], z2], axis=-1)


def _unpad_in(w):
    return jnp.concatenate([w[..., :AB_ORIG_END], w[..., C_XR:C_GR + LRU_W]], axis=-1)


def kernel(x, norm1_w, w_in, pool_w, pool_b, pool_scale, gdn_conv_w, gdn_a_log, gdn_dt_bias, gdn_norm_w, lru_conv_w, lru_conv_b, lru_wa, lru_ba, lru_wx, lru_bx, lru_lambda, w_out, norm2_w, ffn_up, ffn_conv_w, ffn_down, final_norm_w, loss_target, m_norm1_w, m_w_in, m_pool_w, m_pool_b, m_pool_scale, m_gdn_conv_w, m_gdn_a_log, m_gdn_dt_bias, m_gdn_norm_w, m_lru_conv_w, m_lru_conv_b, m_lru_wa, m_lru_ba, m_lru_wx, m_lru_bx, m_lru_lambda, m_w_out, m_norm2_w, m_ffn_up, m_ffn_conv_w, m_ffn_down, m_final_norm_w, v_norm1_w, v_w_in, v_pool_w, v_pool_b, v_pool_scale, v_gdn_conv_w, v_gdn_a_log, v_gdn_dt_bias, v_gdn_norm_w, v_lru_conv_w, v_lru_conv_b, v_lru_wa, v_lru_ba, v_lru_wx, v_lru_bx, v_lru_lambda, v_w_out, v_norm2_w, v_ffn_up, v_ffn_conv_w, v_ffn_down, v_final_norm_w):
    loc = dict(locals())
    Wp = {n: loc[n] for n in WEIGHTS}
    Mp = {n: loc["m_" + n] for n in WEIGHTS}
    Vp = {n: loc["v_" + n] for n in WEIGHTS}
    xi, yi, ci = _xyc()
    me = 4 * xi + 2 * yi + ci
    mychip = 2 * xi + yi
    cidx = ci.astype(jnp.int32).reshape(1)
    keys = dict(w_in="win", w_out="wout", ffn_up="wup", ffn_down="wdown")

    def shard2d(d, name, l):
        a = d[name][l]
        return _pad_in(a) if name == "w_in" else a

    def wshard(l, name):
        return shard2d(Wp, name, l).astype(BF16)

    def full2d(name, full):
        return full if name == "ffn_up" else full.reshape(-1, full.shape[2])

    def ag_start(shards, tag, token=None):
        if token is not None:
            shards = [_tie(shards[0], token)] + list(shards[1:])
        bufs = _gather_bufs(shards, me)
        return _split_start(bufs, _ag1_plan, 4 * len(bufs), f"ag1s_{tag}")

    def ag_mid(st, after, tag):
        bufs = _split_wait(st, after, f"ag1w_{tag}")
        return _split_start(bufs, _ag2_plan, 3 * len(bufs), f"ag2s_{tag}")

    def ag_end(st, after, tag):
        return _split_wait(st, after, f"ag2w_{tag}")

    def rs_start(gs, tag):
        bufs = []
        for nm, g in gs.items():
            if nm != "ffn_up":
                g = g.reshape(N_DEV, g.shape[0] // N_DEV, g.shape[1])
            bufs += [g, _empty_hbm((4,) + g.shape[1:], BF16)]
        st = _split_start(bufs, _rs1_plan, 4 * len(gs), f"rs1s_{tag}")
        st["names"] = list(gs)
        return st

    def rs_mid(st, after, tag):
        bufs = _split_wait(st, after, f"rs1w_{tag}")
        out = []
        for i, nm in enumerate(st["names"]):
            s = _pair_sum(bufs[2 * i], bufs[2 * i + 1], cidx, f"pairsum_{nm}_{tag}")
            own = lax.dynamic_slice_in_dim(s, mychip, 1, axis=0)
            out += [s, lax.dynamic_update_slice(lax.empty(s.shape, s.dtype), own, (mychip, 0, 0))]
        st2 = _split_start(out, _rs2_plan, 3 * len(st["names"]), f"rs2s_{tag}")
        st2["names"] = st["names"]
        return st2

    def rs_end(st, after, tag):
        bufs = _split_wait(st, after, f"rs2w_{tag}")
        return dict(zip(st["names"], bufs[1::2]))

    lnames = tuple(n for n in SMALL_REPL if n != "final_norm_w") + SMALL_SHARD

    def small_pack(l, gs, extra):
        return _pack([gs[nm] for nm in lnames] + extra)

    def small_state(d, l, gs):
        arrs = [d[nm][l] if nm in SMALL_REPL else jnp.zeros(gs[nm].shape, F32) for nm in lnames]
        if l == 0:
            arrs += [d["final_norm_w"], jnp.zeros((1,), F32)]
        return _pack(arrs)[0]

    stA = ag_start([wshard(0, "w_in")], "a")
    stA2 = ag_mid(stA, stA["token"], "a")
    stB = ag_start([wshard(0, n) for n in BIG[1:]], "b", stA2["token"])
    (w_in0,) = ag_end(stA2, stB["token"], "a")

    cbuf, ctable = _pack([Wp[n] for n in SMALL_SHARD])
    call = _all_gather(cbuf, "ag_conv_w")
    parts = [_unpack(call[i], ctable) for i in range(N_DEV)]
    conv_full = {n: jnp.concatenate([parts[i][j] for i in range(N_DEV)], axis=-1) for j, n in enumerate(SMALL_SHARD)}

    Ws = [_layer_weights(l, {}, Wp, conv_full) for l in range(DEPTH)]
    Ws[0]["win"] = full2d("w_in", w_in0)
    st = {}

    def hook(event, l, after, payload=None):
        if event == "f_mix" and l == 0:
            st["b2"] = ag_mid(stB, after, "b")
            st["c"] = ag_start([wshard(1, n) for n in BIG], "c", st["b2"]["token"])
            return st["c"]["token"]
        if event == "f_out" and l == 0:
            for n, b in zip(BIG[1:], ag_end(st["b2"], after, "b")):
                Ws[0][keys[n]] = full2d(n, b)
        if event == "f_act" and l == 0:
            st["c2"] = ag_mid(st["c"], after, "c")
            return st["c2"]["token"]
        if event == "f_end" and l == 0:
            for n, b in zip(BIG, ag_end(st["c2"], after, "c")):
                Ws[1][keys[n]] = full2d(n, b)
        if event == "b_ffn":
            st["ffn", l] = rs_start(payload, f"ffn{l}")
            return st["ffn", l]["token"]
        if event == "b_mid":
            st["ffn2", l] = rs_mid(st["ffn", l], after, f"ffn{l}")
            if l == 0:
                st["sm1b"] = ag_mid(st["sm1"], st["ffn2", l]["token"], "sm1")
                return st["sm1b"]["token"]
            return st["ffn2", l]["token"]
        if event == "b_in":
            st["io", l] = rs_start(payload, f"io{l}")
            if l == 0:
                st["sm1g"] = ag_end(st["sm1b"], st["io", l]["token"], "sm1")[0]
            return st["io", l]["token"]
        if event == "b_end" and l == 1:
            st["io2", 1] = rs_mid(st["io", 1], after, "io1")
            gbuf1, st["table1"] = small_pack(1, payload, [])
            st["sm1"] = ag_start([gbuf1], "sm1", st["io2", 1]["token"])
            return st["sm1"]["token"]
        return None

    loss, dx, g_final, _, gsmall = _local_step(x[0], loss_target[0], Ws, final_norm_w, hook)

    out_g, out_d, out_m, out_v = {}, {}, {}, {}
    outs4 = (out_g, out_d, out_m, out_v)
    rts = dict(w_in=64, w_out=128, ffn_up=256, ffn_down=128)
    big_res = {}

    def adam_big(l, parts):
        for name, p in parts.items():
            big_res[l, name] = _adamw(p, shard2d(Wp, name, l), shard2d(Mp, name, l), shard2d(Vp, name, l),
                                      rts[name], f"adamw_{name}_{l}")
        return [big_res[l, name][0] for name in parts]

    def adam_small(l, gall, gs):
        rs = gall.shape[1]
        return _adamw(gall, small_state(Wp, l, gs), small_state(Mp, l, gs), small_state(Vp, l, gs),
                      _row_tile(rs, 512), f"adamw_small_{l}")

    gbuf0, table0 = small_pack(0, gsmall[0], [g_final, loss[0, :1]])
    sm0 = ag_start([gbuf0], "sm0", st["io", 0]["token"])
    o = adam_big(1, rs_end(st["ffn2", 1], sm0["token"], "ffn1"))
    st["io2", 0] = rs_mid(st["io", 0], o, "io0")
    o = adam_big(1, rs_end(st["io2", 1], st["io2", 0]["token"], "io1"))
    o = adam_big(0, rs_end(st["ffn2", 0], o, "ffn0"))
    small_res = {1: adam_small(1, _tie(st["sm1g"], o[-1]), gsmall[1])}
    sm0b = ag_mid(sm0, o + [small_res[1][0]], "sm0")
    small_res[0] = adam_small(0, ag_end(sm0b, sm0b["token"], "sm0")[0], gsmall[0])
    adam_big(0, rs_end(st["io2", 0], small_res[0][0], "io0"))

    for name in BIG:
        for i, dst in enumerate(outs4):
            a = jnp.stack([big_res[l, name][i] for l in range(DEPTH)])
            dst[name] = _unpad_in(a) if name == "w_in" else a

    unp = {0: [_unpack(r, table0) for r in small_res[0]], 1: [_unpack(r, st["table1"]) for r in small_res[1]]}
    for j, nm in enumerate(lnames):
        if nm in SMALL_REPL:
            for i, dst in enumerate(outs4):
                dst[nm] = jnp.stack([unp[l][i][j] for l in range(DEPTH)])
    for i, dst in enumerate(outs4):
        dst["final_norm_w"] = unp[0][i][len(lnames)]
    loss_total = unp[0][0][len(lnames) + 1][0]

    gsh = []
    for nm in SMALL_SHARD:
        j = lnames.index(nm)
        width = Wp[nm].shape[-1]
        gsh.append(jnp.stack([lax.dynamic_slice_in_dim(unp[l][0][j], me * width, width, axis=1)
                              for l in range(DEPTH)]))
    sbuf, stable = _pack(gsh)
    res = _adamw(sbuf[None], _pack([Wp[n] for n in SMALL_SHARD])[0], _pack([Mp[n] for n in SMALL_SHARD])[0],
                 _pack([Vp[n] for n in SMALL_SHARD])[0], sbuf.shape[0], "adamw_conv_w")
    unp2 = [_unpack(r, stable) for r in res]
    for j, nm in enumerate(SMALL_SHARD):
        for i, dst in enumerate((out_g, out_d, out_m, out_v)):
            dst[nm] = unp2[i][j]

    return (loss_total, dx[None], *[out_g[n] for n in WEIGHTS], *[out_d[n] for n in WEIGHTS],
            *[out_m[n] for n in WEIGHTS], *[out_v[n] for n in WEIGHTS])
```

```python
import functools

import jax
import jax.numpy as jnp
from jax import lax
from jax.experimental import pallas as pl
from jax.experimental.pallas import tpu as pltpu

F32 = jnp.float32
BF16 = jnp.bfloat16
HI = lax.Precision.HIGHEST
MESH_IDS = pl.DeviceIdType.MESH

N_DEV = 8
D_MODEL = 2048
DEPTH = 2
POOL_WINDOWS = (2, 4, 8, 16)
POOL_W = 512
HEADS = 6
HD = 128
GDN_W = HEADS * HD
CHUNK = 64
LRU_W = 768
LRU_C = 8.0
D_FF = 3 * D_MODEL
EPS = 1e-6
IN_COLS = 5132
PCOLS = 5376
C_QKV, C_Z, C_AB, C_XR, C_GR = 512, 2816, 3584, 3840, 4608
AB_ORIG_END = 3596
M_GDN, M_LRU = 512, 1280

ADAM_LR, ADAM_B1, ADAM_B2, ADAM_EPS, ADAM_WD, ADAM_STEP = 0.001, 0.9, 0.999, 1e-08, 0.01, 10

VMEM_LIMIT = 56 * 1024 * 1024


def _cp(sem):
    return pltpu.CompilerParams(dimension_semantics=sem, vmem_limit_bytes=VMEM_LIMIT)


def _mm(a, b, ca=1, cb=0, prec=None, cast=True):
    if cast:
        a = a.astype(BF16)
        b = b.astype(BF16)
    return lax.dot_general(a, b, (((ca,), (cb,)), ((), ())), preferred_element_type=F32, precision=prec)


def _bmm(a, b, ca=2, cb=1, prec=None, cast=True):
    if cast:
        a = a.astype(BF16)
        b = b.astype(BF16)
    return lax.dot_general(a, b, (((ca,), (cb,)), ((0,), (0,))), preferred_element_type=F32, precision=prec)


def _sigmoid(x):
    return 1.0 / (1.0 + jnp.exp(-x))


def _log1p(e):
    u = 1.0 + e
    return jnp.where(u == 1.0, e, jnp.log(u) * e / jnp.where(u == 1.0, 1.0, u - 1.0))


def _softplus(x):
    return jnp.maximum(x, 0.0) + _log1p(jnp.exp(-jnp.abs(x)))


def _expm1(x):
    u = jnp.exp(x)
    um = u - 1.0
    safe = jnp.where((u == 1.0) | (um == -1.0), 1.0, jnp.log(u))
    return jnp.where(u == 1.0, x, jnp.where(um == -1.0, -1.0, um * x / safe))


_G0 = 0.7978845608028654
_G1 = 0.044715


def _gelu(x):
    return 0.5 * x * (1.0 + jnp.tanh(_G0 * (x + _G1 * x * x * x)))


def _gelu_and_grad(x):
    th = jnp.tanh(_G0 * (x + _G1 * x * x * x))
    g = 0.5 * x * (1.0 + th)
    dg = 0.5 * (1.0 + th) + 0.5 * x * (1.0 - th * th) * _G0 * (1.0 + 3.0 * _G1 * x * x)
    return g, dg


def _tile(T):
    return min(T, 512)


def _matmul(a, b, *, grid, a_spec, b_spec, out_shape, out_spec, dims, acc_shape, name, add=None, add_spec=None):
    nk = grid[2]
    has_add = add is not None

    def body(*refs):
        if has_add:
            a_ref, b_ref, add_ref, o_ref, acc_ref = refs
        else:
            a_ref, b_ref, o_ref, acc_ref = refs
            add_ref = None
        k = pl.program_id(2)
        p = lax.dot_general(a_ref[...].astype(BF16), b_ref[...].astype(BF16), (dims, ((), ())),
                            preferred_element_type=F32)

        def finish(r):
            if has_add:
                r = r + add_ref[...]
            o_ref[...] = r.astype(o_ref.dtype)

        if nk == 1:
            finish(p)
        else:
            @pl.when(k == 0)
            def _():
                acc_ref[...] = p

            @pl.when(k > 0)
            def _():
                acc_ref[...] += p

            @pl.when(k == nk - 1)
            def _():
                finish(acc_ref[...])

    in_specs = [a_spec, b_spec] + ([add_spec] if has_add else [])
    args = (a, b) + ((add,) if has_add else ())
    return pl.pallas_call(
        body, name=name, grid=grid, in_specs=in_specs, out_specs=out_spec, out_shape=out_shape,
        scratch_shapes=[pltpu.VMEM(acc_shape, F32)],
        compiler_params=_cp(("parallel", "parallel", "arbitrary")),
    )(*args)


def _pick(n, cands):
    for c in cands:
        if n % c == 0:
            return c
    raise ValueError(f"no tile for {n}")


def _mm_nn(a, b, out_dtype, name, add=None, tn_c=(1024, 768, 512)):
    M, K = a.shape
    N = b.shape[1]
    tm = _pick(M, (1024, 512, 256))
    tn = _pick(N, tn_c)
    tk = _pick(K, (2048, 1536, 1024, 512, 256))
    return _matmul(
        a, b, grid=(M // tm, N // tn, K // tk),
        a_spec=pl.BlockSpec((tm, tk), lambda i, j, k: (i, k)),
        b_spec=pl.BlockSpec((tk, tn), lambda i, j, k: (k, j)),
        out_shape=jax.ShapeDtypeStruct((M, N), out_dtype),
        out_spec=pl.BlockSpec((tm, tn), lambda i, j, k: (i, j)),
        dims=((1,), (0,)), acc_shape=(tm, tn), name=name, add=add,
        add_spec=pl.BlockSpec((tm, tn), lambda i, j, k: (i, j)))


def _mm_nt(a, b, out_dtype, name, tk_c=(2048, 1536, 1024, 768, 512)):
    M, K = a.shape
    N = b.shape[0]
    tm = _pick(M, (1024, 512, 256))
    tn = _pick(N, (1024, 768, 512))
    tk = _pick(K, tk_c)
    return _matmul(
        a, b, grid=(M // tm, N // tn, K // tk),
        a_spec=pl.BlockSpec((tm, tk), lambda i, j, k: (i, k)),
        b_spec=pl.BlockSpec((tn, tk), lambda i, j, k: (j, k)),
        out_shape=jax.ShapeDtypeStruct((M, N), out_dtype),
        out_spec=pl.BlockSpec((tm, tn), lambda i, j, k: (i, j)),
        dims=((1,), (1,)), acc_shape=(tm, tn), name=name)


def _mm_tn(a, b, out_dtype, name, tn_c=(1024, 768, 512)):
    K, M = a.shape
    N = b.shape[1]
    tm = _pick(M, (1024, 768, 512))
    tn = _pick(N, tn_c)
    tk = _pick(K, (1024, 512, 256))
    return _matmul(
        a, b, grid=(M // tm, N // tn, K // tk),
        a_spec=pl.BlockSpec((tk, tm), lambda i, j, k: (k, i)),
        b_spec=pl.BlockSpec((tk, tn), lambda i, j, k: (k, j)),
        out_shape=jax.ShapeDtypeStruct((M, N), out_dtype),
        out_spec=pl.BlockSpec((tm, tn), lambda i, j, k: (i, j)),
        dims=((0,), (0,)), acc_shape=(tm, tn), name=name)


def _mm_up(h, wup, name):
    M, K = h.shape
    ns = wup.shape[2]
    tm = _pick(M, (1024, 512, 256))
    tn = 768
    per = ns // tn
    return _matmul(
        h, wup, grid=(M // tm, N_DEV * per, 1),
        a_spec=pl.BlockSpec((tm, K), lambda i, j, k: (i, 0)),
        b_spec=pl.BlockSpec((None, K, tn), lambda i, j, k: (j // per, 0, j % per)),
        out_shape=jax.ShapeDtypeStruct((M, N_DEV * ns), F32),
        out_spec=pl.BlockSpec((tm, tn), lambda i, j, k: (i, j)),
        dims=((1,), (0,)), acc_shape=(tm, tn), name=name)


def _mm_up_t(dup, wup, name):
    M = dup.shape[1]
    D, ns = wup.shape[1], wup.shape[2]
    tm = _pick(M, (1024, 512, 256))
    tn = 1024
    tk = ns
    return _matmul(
        dup, wup, grid=(M // tm, D // tn, N_DEV),
        a_spec=pl.BlockSpec((None, tm, tk), lambda i, j, k: (k // 4, i, k % 4)),
        b_spec=pl.BlockSpec((None, tn, tk), lambda i, j, k: (k, j, 0)),
        out_shape=jax.ShapeDtypeStruct((M, D), F32),
        out_spec=pl.BlockSpec((tm, tn), lambda i, j, k: (i, j)),
        dims=((1,), (1,)), acc_shape=(tm, tn), name=name)


def _mm_dup(h, dup, ns, name):
    K, M = h.shape
    tm = 1024
    tn = 768
    per = ns // tn
    half = 4 * per
    tk = _pick(K, (1024, 512, 256))
    return _matmul(
        h, dup, grid=(M // tm, N_DEV * per, K // tk),
        a_spec=pl.BlockSpec((tk, tm), lambda i, j, k: (k, i)),
        b_spec=pl.BlockSpec((None, tk, tn), lambda i, j, k: (j // half, k, j % half)),
        out_shape=jax.ShapeDtypeStruct((N_DEV, M, ns), BF16),
        out_spec=pl.BlockSpec((None, tm, tn), lambda i, j, k: (j // per, i, j % per)),
        dims=((0,), (0,)), acc_shape=(tm, tn), name=name)


def _norm_fwd(x, w, name):
    T, D = x.shape
    tt = _tile(T)

    def body(x_ref, w_ref, h_ref):
        xv = x_ref[...]
        r = lax.rsqrt(jnp.mean(xv * xv, axis=1, keepdims=True) + EPS)
        h_ref[...] = (xv * r * w_ref[...]).astype(BF16)

    return pl.pallas_call(
        body, name=name, grid=(T // tt,),
        in_specs=[pl.BlockSpec((tt, D), lambda t: (t, 0)), pl.BlockSpec((1, D), lambda t: (0, 0))],
        out_specs=pl.BlockSpec((tt, D), lambda t: (t, 0)),
        out_shape=jax.ShapeDtypeStruct((T, D), BF16), compiler_params=_cp(("parallel",)))(x, w)


def _norm_bwd(x, w, dh, dres, name):
    T, D = x.shape
    tt = _tile(T)

    def body(x_ref, w_ref, dh_ref, dres_ref, dx_ref, dw_ref):
        t = pl.program_id(0)
        xv = x_ref[...]
        r = lax.rsqrt(jnp.mean(xv * xv, axis=1, keepdims=True) + EPS)
        xh = xv * r
        dh_v = dh_ref[...]
        dxh = dh_v * w_ref[...]
        dx_ref[...] = dres_ref[...] + r * (dxh - xh * jnp.mean(dxh * xh, axis=1, keepdims=True))
        part = jnp.sum(dh_v * xh, axis=0, keepdims=True)

        @pl.when(t == 0)
        def _():
            dw_ref[...] = part

        @pl.when(t > 0)
        def _():
            dw_ref[...] += part

    row = pl.BlockSpec((tt, D), lambda t: (t, 0))
    vec = pl.BlockSpec((1, D), lambda t: (0, 0))
    return pl.pallas_call(
        body, name=name, grid=(T // tt,), in_specs=[row, vec, row, row], out_specs=[row, vec],
        out_shape=[jax.ShapeDtypeStruct((T, D), F32), jax.ShapeDtypeStruct((1, D), F32)],
        compiler_params=_cp(("arbitrary",)))(x, w, dh, dres)


def _loss_head(x, w, target):
    T, D = x.shape
    tt = _tile(T)

    def body(x_ref, w_ref, t_ref, loss_ref, dx_ref, dw_ref):
        t = pl.program_id(0)
        xv = x_ref[...]
        r = lax.rsqrt(jnp.mean(xv * xv, axis=1, keepdims=True) + EPS)
        xh = xv * r
        err = xh * w_ref[...] - t_ref[...]
        lp = 0.5 * jnp.sum(jnp.mean(err * err, axis=1, keepdims=True), axis=0, keepdims=True)
        dy = err * (1.0 / D)
        dxh = dy * w_ref[...]
        dx_ref[...] = r * (dxh - xh * jnp.mean(dxh * xh, axis=1, keepdims=True))
        part = jnp.sum(dy * xh, axis=0, keepdims=True)
        lpb = jnp.broadcast_to(lp, (1, 128))

        @pl.when(t == 0)
        def _():
            dw_ref[...] = part
            loss_ref[...] = lpb

        @pl.when(t > 0)
        def _():
            dw_ref[...] += part
            loss_ref[...] += lpb

    row = pl.BlockSpec((tt, D), lambda t: (t, 0))
    vec = pl.BlockSpec((1, D), lambda t: (0, 0))
    return pl.pallas_call(
        body, name="loss_head", grid=(T // tt,), in_specs=[row, vec, row],
        out_specs=[pl.BlockSpec((1, 128), lambda t: (0, 0)), row, vec],
        out_shape=[jax.ShapeDtypeStruct((1, 128), F32), jax.ShapeDtypeStruct((T, D), F32),
                   jax.ShapeDtypeStruct((1, D), F32)],
        compiler_params=_cp(("arbitrary",)))(x, w, target)


def _conv_fwd(x, col0, C, w, b, cb, name):
    T = x.shape[0]
    K = w.shape[0]
    tt = _tile(T)
    nt, nc, c0 = T // tt, C // cb, col0 // cb
    has_b = b is not None

    def body(*refs):
        if has_b:
            x_ref, halo_ref, w_ref, b_ref, y_ref = refs
        else:
            x_ref, halo_ref, w_ref, y_ref = refs
        t = pl.program_id(1)
        halo = jnp.where(t == 0, 0.0, halo_ref[...])
        xe = jnp.concatenate([halo, x_ref[...]], axis=0)
        acc = xe * w_ref[K - 1:K, :]
        for j in range(K - 1):
            acc = acc + pltpu.roll(xe, K - 1 - j, 0) * w_ref[j:j + 1, :]
        if has_b:
            acc = acc + b_ref[...]
        y_ref[...] = acc[8:, :]

    in_specs = [pl.BlockSpec((tt, cb), lambda j, t: (t, c0 + j)),
                pl.BlockSpec((8, cb), lambda j, t: (jnp.maximum(t * (tt // 8) - 1, 0), c0 + j)),
                pl.BlockSpec((K, cb), lambda j, t: (0, j))]
    args = [x, x, w]
    if has_b:
        in_specs.append(pl.BlockSpec((1, cb), lambda j, t: (0, j)))
        args.append(b)
    return pl.pallas_call(
        body, name=name, grid=(nc, nt), in_specs=in_specs,
        out_specs=pl.BlockSpec((tt, cb), lambda j, t: (t, j)),
        out_shape=jax.ShapeDtypeStruct((T, C), F32), compiler_params=_cp(("parallel", "parallel")))(*args)


def _conv_bwd(dy, x, col0, w, cb, name, into, want_db=False):
    T, C = dy.shape
    K = w.shape[0]
    tt = _tile(T)
    nt, nc, c0 = T // tt, C // cb, col0 // cb

    def body(*refs):
        if want_db:
            dy_ref, dyn_ref, x_ref, xp_ref, w_ref, _, dx_ref, dw_ref, db_ref = refs
        else:
            dy_ref, dyn_ref, x_ref, xp_ref, w_ref, _, dx_ref, dw_ref = refs
        t = pl.program_id(1)
        dyv = dy_ref[...]
        nxt = jnp.where(t == nt - 1, 0.0, dyn_ref[...])
        dye = jnp.concatenate([dyv, nxt], axis=0)
        n = tt + 8
        acc = dye * w_ref[K - 1:K, :]
        for j in range(K - 1):
            acc = acc + pltpu.roll(dye, n - (K - 1 - j), 0) * w_ref[j:j + 1, :]
        dx_ref[...] = acc[:tt, :].astype(dx_ref.dtype)
        prev = jnp.where(t == 0, 0.0, xp_ref[...])
        xe = jnp.concatenate([prev, x_ref[...]], axis=0)

        @pl.when(t == 0)
        def _():
            dw_ref[...] = jnp.zeros_like(dw_ref)
            if want_db:
                db_ref[...] = jnp.zeros_like(db_ref)

        for j in range(K):
            sh = K - 1 - j
            xs = xe[8:, :] if sh == 0 else pltpu.roll(xe, sh, 0)[8:, :]
            dw_ref[j:j + 1, :] += jnp.sum(dyv * xs, axis=0, keepdims=True)
        if want_db:
            db_ref[...] += jnp.sum(dyv, axis=0, keepdims=True)

    h8 = tt // 8
    in_specs = [pl.BlockSpec((tt, cb), lambda j, t: (t, j)),
                pl.BlockSpec((8, cb), lambda j, t: (jnp.minimum((t + 1) * h8, T // 8 - 1), j)),
                pl.BlockSpec((tt, cb), lambda j, t: (t, c0 + j)),
                pl.BlockSpec((8, cb), lambda j, t: (jnp.maximum(t * h8 - 1, 0), c0 + j)),
                pl.BlockSpec((K, cb), lambda j, t: (0, j)), pl.BlockSpec(memory_space=pl.ANY)]
    out_specs = [pl.BlockSpec((tt, cb), lambda j, t: (t, c0 + j)), pl.BlockSpec((K, cb), lambda j, t: (0, j))]
    out_shape = [jax.ShapeDtypeStruct(into.shape, into.dtype), jax.ShapeDtypeStruct((K, C), F32)]
    if want_db:
        out_specs.append(pl.BlockSpec((1, cb), lambda j, t: (0, j)))
        out_shape.append(jax.ShapeDtypeStruct((1, C), F32))
    return pl.pallas_call(
        body, name=name, grid=(nc, nt), in_specs=in_specs, out_specs=out_specs, out_shape=out_shape,
        input_output_aliases={5: 0}, compiler_params=_cp(("parallel", "arbitrary")))(dy, dy, x, x, w, into)


def _pool_d(ue, g, pos, tt):
    win = POOL_WINDOWS[g]
    ug = ue[:, g * 128:(g + 1) * 128]
    s = ug
    k = 1
    while k < win:
        s = s + pltpu.roll(s, k, 0)
        k *= 2
    cnt = jnp.minimum(pos + 1, win).astype(F32)
    return s[16:, :] / cnt - ug[16:, :]


def _pool_fwd(proj, pw, pb, ps):
    T = proj.shape[0]
    tt = _tile(T)

    def body(u_ref, halo_ref, w_ref, b_ref, s_ref, y_ref):
        t = pl.program_id(0)
        halo = jnp.where(t == 0, 0.0, halo_ref[...])
        ue = jnp.concatenate([halo, u_ref[...]], axis=0)
        pos = t * tt + lax.broadcasted_iota(jnp.int32, (tt, 1), 0)
        for g in range(4):
            sl = slice(g * 128, (g + 1) * 128)
            d = _pool_d(ue, g, pos, tt)
            yg = _mm(d, w_ref[g]) + b_ref[:, sl]
            y_ref[:, sl] = (yg * s_ref[:, sl]).astype(BF16)

    vec = pl.BlockSpec((1, POOL_W), lambda t: (0, 0))
    return pl.pallas_call(
        body, name="pool_fwd", grid=(T // tt,),
        in_specs=[pl.BlockSpec((tt, POOL_W), lambda t: (t, 0)),
                  pl.BlockSpec((16, POOL_W), lambda t: (jnp.maximum(t * (tt // 16) - 1, 0), 0)),
                  pl.BlockSpec((4, 128, 128), lambda t: (0, 0, 0)), vec, vec],
        out_specs=pl.BlockSpec((tt, POOL_W), lambda t: (t, 0)),
        out_shape=jax.ShapeDtypeStruct((T, D_MODEL), BF16), compiler_params=_cp(("parallel",)))(
            proj, proj, pw, pb, ps)


def _pool_bwd(dmixed, proj, pw, pb, ps):
    T = proj.shape[0]
    tt = _tile(T)
    nt = T // tt

    def body(dy_ref, dyn_ref, u_ref, halo_ref, w_ref, b_ref, s_ref, du_ref, dw_ref, db_ref, ds_ref):
        t = pl.program_id(0)
        halo = jnp.where(t == 0, 0.0, halo_ref[...])
        ue = jnp.concatenate([halo, u_ref[...]], axis=0)
        dyv = dy_ref[...]
        nxt = jnp.where(t == nt - 1, 0.0, dyn_ref[...])
        dye = jnp.concatenate([dyv, nxt], axis=0)
        n = tt + 16
        pos = t * tt + lax.broadcasted_iota(jnp.int32, (tt, 1), 0)
        pos_e = t * tt + lax.broadcasted_iota(jnp.int32, (n, 1), 0)

        @pl.when(t == 0)
        def _():
            dw_ref[...] = jnp.zeros_like(dw_ref)
            db_ref[...] = jnp.zeros_like(db_ref)
            ds_ref[...] = jnp.zeros_like(ds_ref)

        for g in range(4):
            win = POOL_WINDOWS[g]
            sl = slice(g * 128, (g + 1) * 128)
            d = _pool_d(ue, g, pos, tt)
            wg = w_ref[g]
            ypre = _mm(d, wg) + b_ref[:, sl]
            sc = s_ref[:, sl]
            ds_ref[:, sl] += jnp.sum(dyv[:, sl] * ypre, axis=0, keepdims=True)
            dyp_e = dye[:, sl] * sc
            dyp = dyp_e[:tt, :]
            db_ref[:, sl] += jnp.sum(dyp, axis=0, keepdims=True)
            dw_ref[g] += _mm(d, dyp, 0, 0)
            dd_e = _mm(dyp_e, wg, 1, 1)
            cnt_e = jnp.minimum(pos_e + 1, win).astype(F32)
            s = dd_e / cnt_e
            k = 1
            while k < win:
                s = s + pltpu.roll(s, n - k, 0)
                k *= 2
            du_ref[:, sl] = (s[:tt, :] - dd_e[:tt, :]).astype(BF16)

    vec = pl.BlockSpec((1, POOL_W), lambda t: (0, 0))
    h16 = tt // 16
    return pl.pallas_call(
        body, name="pool_bwd", grid=(nt,),
        in_specs=[pl.BlockSpec((tt, POOL_W), lambda t: (t, 0)),
                  pl.BlockSpec((16, POOL_W), lambda t: (jnp.minimum((t + 1) * h16, T // 16 - 1), 0)),
                  pl.BlockSpec((tt, POOL_W), lambda t: (t, 0)),
                  pl.BlockSpec((16, POOL_W), lambda t: (jnp.maximum(t * h16 - 1, 0), 0)),
                  pl.BlockSpec((4, 128, 128), lambda t: (0, 0, 0)), vec, vec],
        out_specs=[pl.BlockSpec((tt, POOL_W), lambda t: (t, 0)),
                   pl.BlockSpec((4, 128, 128), lambda t: (0, 0, 0)), vec, vec],
        out_shape=[jax.ShapeDtypeStruct((T, PCOLS), BF16), jax.ShapeDtypeStruct((4, 128, 128), F32),
                   jax.ShapeDtypeStruct((1, POOL_W), F32), jax.ShapeDtypeStruct((1, POOL_W), F32)],
        compiler_params=_cp(("arbitrary",)))(dmixed, dmixed, proj, proj, pw, pb, ps)


def _gdn_pre_fwd(cpre, proj, alog, dtb):
    T = cpre.shape[0]
    tt = _tile(T)

    def body(c_ref, ab_ref, alog_ref, dtb_ref, qkv_ref, bb_ref, gb_ref):
        for p in range(3):
            for h in range(HEADS):
                cc = c_ref[:, (p * HEADS + h) * HD:(p * HEADS + h + 1) * HD]
                s = cc * _sigmoid(cc)
                if p < 2:
                    s = s * lax.rsqrt(jnp.sum(s * s, axis=1, keepdims=True) + EPS)
                if p == 0:
                    s = s * (HD ** -0.5)
                qkv_ref[p, h] = s
        ab = ab_ref[...]
        g = -jnp.exp(alog_ref[...]) * _softplus(ab + dtb_ref[...])
        r64 = lax.broadcasted_iota(jnp.int32, (tt, 1), 0) & (CHUNK - 1)
        k = 1
        while k < CHUNK:
            g = g + jnp.where(r64 >= k, pltpu.roll(g, k, 0), 0.0)
            k *= 2
        sb = _sigmoid(ab)
        for h in range(HEADS):
            gb_ref[h] = jnp.broadcast_to(g[:, h:h + 1], (tt, HD))
            bb_ref[h] = jnp.broadcast_to(sb[:, HEADS + h:HEADS + h + 1], (tt, HD))

    vec = pl.BlockSpec((1, 128), lambda t: (0, 0))
    hb = pl.BlockSpec((HEADS, tt, HD), lambda t: (0, t, 0))
    return pl.pallas_call(
        body, name="gdn_pre_fwd", grid=(T // tt,),
        in_specs=[pl.BlockSpec((tt, 3 * GDN_W), lambda t: (t, 0)),
                  pl.BlockSpec((tt, 128), lambda t: (t, C_AB // 128)), vec, vec],
        out_specs=[pl.BlockSpec((3, HEADS, tt, HD), lambda t: (0, 0, t, 0)), hb, hb],
        out_shape=[jax.ShapeDtypeStruct((3, HEADS, T, HD), F32), jax.ShapeDtypeStruct((HEADS, T, HD), F32),
                   jax.ShapeDtypeStruct((HEADS, T, HD), F32)],
        compiler_params=_cp(("parallel",)))(cpre, proj, alog, dtb)


def _gdn_pre_bwd(dq, dk, dv, cpre, dbb, dgb, proj, alog, dtb, dproj):
    T = cpre.shape[0]
    tt = _tile(T)

    def body(dq_ref, dk_ref, dv_ref, c_ref, dbb_ref, dgb_ref, ab_ref, alog_ref, dtb_ref, _,
             dc_ref, dab_ref, dalog_ref, ddtb_ref):
        t = pl.program_id(0)
        srcs = (dq_ref, dk_ref, dv_ref)
        for p in range(3):
            for h in range(HEADS):
                sl = slice((p * HEADS + h) * HD, (p * HEADS + h + 1) * HD)
                cc = c_ref[:, sl]
                sg = _sigmoid(cc)
                s = cc * sg
                dyv = srcs[p][h]
                if p < 2:
                    r = lax.rsqrt(jnp.sum(s * s, axis=1, keepdims=True) + EPS)
                    y = s * r
                    if p == 0:
                        dyv = dyv * (HD ** -0.5)
                    ds = r * (dyv - y * jnp.sum(dyv * y, axis=1, keepdims=True))
                else:
                    ds = dyv
                dc_ref[:, sl] = ds * sg * (1.0 + cc * (1.0 - sg))
        lane = lax.broadcasted_iota(jnp.int32, (tt, 128), 1)
        dg = jnp.zeros((tt, 128), F32)
        dbeta = jnp.zeros((tt, 128), F32)
        for h in range(HEADS):
            dg = jnp.where(lane == h, dgb_ref[h], dg)
            dbeta = jnp.where(lane == HEADS + h, dbb_ref[h], dbeta)
        r64 = lax.broadcasted_iota(jnp.int32, (tt, 1), 0) & (CHUNK - 1)
        k = 1
        while k < CHUNK:
            dg = dg + jnp.where(r64 < CHUNK - k, pltpu.roll(dg, tt - k, 0), 0.0)
            k *= 2
        ab = ab_ref[...]
        e = jnp.exp(alog_ref[...])
        xx = ab + dtb_ref[...]
        g = -e * _softplus(xx)
        da = jnp.where(lane < HEADS, dg * (-e) * _sigmoid(xx), 0.0)
        pa = jnp.sum(jnp.where(lane < HEADS, dg * g, 0.0), axis=0, keepdims=True)
        pd = jnp.sum(da, axis=0, keepdims=True)

        @pl.when(t == 0)
        def _():
            dalog_ref[...] = pa
            ddtb_ref[...] = pd

        @pl.when(t > 0)
        def _():
            dalog_ref[...] += pa
            ddtb_ref[...] += pd

        sb = _sigmoid(ab)
        dab_ref[:, :128] = jnp.where(lane < HEADS, da, dbeta * sb * (1.0 - sb)).astype(BF16)
        dab_ref[:, 128:] = jnp.zeros((tt, 128), BF16)

    vec = pl.BlockSpec((1, 128), lambda t: (0, 0))
    hb = pl.BlockSpec((HEADS, tt, HD), lambda t: (0, t, 0))
    return pl.pallas_call(
        body, name="gdn_pre_bwd", grid=(T // tt,),
        in_specs=[hb, hb, hb, pl.BlockSpec((tt, 3 * GDN_W), lambda t: (t, 0)), hb, hb,
                  pl.BlockSpec((tt, 128), lambda t: (t, C_AB // 128)), vec, vec, pl.BlockSpec(memory_space=pl.ANY)],
        out_specs=[pl.BlockSpec((tt, 3 * GDN_W), lambda t: (t, 0)),
                   pl.BlockSpec((tt, 256), lambda t: (t, C_AB // 256)), vec, vec],
        out_shape=[jax.ShapeDtypeStruct((T, 3 * GDN_W), F32), jax.ShapeDtypeStruct(dproj.shape, dproj.dtype),
                   jax.ShapeDtypeStruct((1, 128), F32), jax.ShapeDtypeStruct((1, 128), F32)],
        input_output_aliases={9: 1},
        compiler_params=_cp(("arbitrary",)))(dq, dk, dv, cpre, dbb, dgb, proj, alog, dtb, dproj)


def _tri_inv(a):
    nb = a.shape[0]
    ri = lax.broadcasted_iota(jnp.int32, (nb, CHUNK, CHUNK), 1)
    ci = lax.broadcasted_iota(jnp.int32, (nb, CHUNK, CHUNK), 2)
    n = -a
    p = jnp.where(ri == ci, 1.0, 0.0) + n
    for _ in range(5):
        n = _bmm(n, n, 2, 1, HI, False)
        p = p + _bmm(p, n, 2, 1, HI, False)
    return p


def _gdn_chunk_common(q, k, v, bb3, gb3, tm_saved=None):
    nb = q.shape[0]
    need_t = tm_saved is not None
    beta = bb3[:, :, 0:1]
    gcol = gb3[:, :, 0:1]
    bcol = bb3[:, :, :CHUNK]
    gcm = gb3[:, :, :CHUNK]
    oh = jnp.where(lax.broadcasted_iota(jnp.int32, (nb, CHUNK, HD), 2) == 0, 1.0, 0.0)
    grow = _bmm(oh, gb3, 2, 2, HI, False)
    ri = lax.broadcasted_iota(jnp.int32, (nb, CHUNK, CHUNK), 1)
    ci = lax.broadcasted_iota(jnp.int32, (nb, CHUNK, CHUNK), 2)
    tril, stl = ri >= ci, ri > ci
    dg = gcm - grow
    dec = jnp.where(tril, jnp.exp(jnp.where(tril, dg, 0.0)), 0.0)
    kk = _bmm(k, k, 2, 2)
    qk = _bmm(q, k, 2, 2)
    tm = tm_saved if need_t else _tri_inv(jnp.where(stl, bcol * kk * dec, 0.0))
    gam = jnp.exp(gcol)
    glast = gb3[:, CHUNK - 1:CHUNK, 0:1]
    egl = jnp.exp(glast)
    rw = k * (beta * gam)
    ru = v * beta
    wu = _bmm(tm, jnp.concatenate([rw, ru], axis=2), 2, 1, HI, False)
    kdf = jnp.exp(glast - gcol)
    out = dict(beta=beta, bcol=bcol, tril=tril, stl=stl, dec=dec, kk=kk, qk=qk, tm=tm, gam=gam, egl=egl,
               rw=rw, wu=wu, at=qk * dec, qd=q * gam, kdf=kdf, kd=k * kdf)
    if need_t:
        brow = _bmm(oh, bb3, 2, 2, HI, False)
        triu, stu = ri <= ci, ri < ci
        dect = jnp.where(triu, jnp.exp(jnp.where(triu, -dg, 0.0)), 0.0)
        qkt = _bmm(k, q, 2, 2)
        eye = jnp.where(ri == ci, 1.0, 0.0)
        out.update(brow=brow, triu=triu, stu=stu, dect=dect, qkt=qkt, tmt=_bmm(eye, tm, 2, 2, HI, False),
                   att=qkt * dect)
    return out


def _gdn_rows(T):
    return min(T, 512)


def _gdn_fwd(qkv, bb, gb, proj, nw, mixed):
    T = proj.shape[0]
    R = _gdn_rows(T)
    nb = R // CHUNK

    def body(q_ref, k_ref, v_ref, bb_ref, gb_ref, z_ref, nw_ref, _, y_ref, st_ref, tm_ref,
             s_ref, w_s, u_s, at_s, qd_s, kd_s):
        t = pl.program_id(1)

        @pl.when(t == 0)
        def _():
            s_ref[...] = jnp.zeros_like(s_ref)

        sh = (nb, CHUNK, HD)
        q, k, v = q_ref[...].reshape(sh), k_ref[...].reshape(sh), v_ref[...].reshape(sh)
        c = _gdn_chunk_common(q, k, v, bb_ref[...].reshape(sh), gb_ref[...].reshape(sh))
        tm_ref[...] = c["tm"]
        w_s[...] = c["wu"][:, :, :HD]
        u_s[...] = c["wu"][:, :, HD:]
        at_s[...] = c["at"]
        qd_s[...] = c["qd"]
        kd_s[...] = c["kd"]
        egl = c["egl"]
        nwv = nw_ref[...]
        for n in range(nb):
            s = s_ref[...]
            st_ref[n] = s
            vn = u_s[n] - _mm(w_s[n], s)
            o = _mm(qd_s[n], s) + _mm(at_s[n], vn)
            s_ref[...] = s * egl[n] + _mm(kd_s[n], vn, 0, 0)
            rows = slice(n * CHUNK, (n + 1) * CHUNK)
            zz = z_ref[rows, :]
            on = o * lax.rsqrt(jnp.mean(o * o, axis=1, keepdims=True) + EPS)
            y_ref[rows, :] = (on * nwv * (zz * _sigmoid(zz))).astype(BF16)

    def hm(p):
        return pl.BlockSpec((None, None, R, HD), lambda h, t: (p, h, t, 0))

    hb = pl.BlockSpec((None, R, HD), lambda h, t: (h, t, 0))
    cs = pltpu.VMEM((nb, CHUNK, HD), F32)
    return pl.pallas_call(
        body, name="gdn_fwd", grid=(HEADS, T // R),
        in_specs=[hm(0), hm(1), hm(2), hb, hb, pl.BlockSpec((R, HD), lambda h, t: (t, C_Z // HD + h)),
                  pl.BlockSpec((1, HD), lambda h, t: (0, 0)), pl.BlockSpec(memory_space=pl.ANY)],
        out_specs=[pl.BlockSpec((R, HD), lambda h, t: (t, M_GDN // HD + h)),
                   pl.BlockSpec((None, nb, HD, HD), lambda h, t: (h, t, 0, 0)),
                   pl.BlockSpec((None, nb, CHUNK, CHUNK), lambda h, t: (h, t, 0, 0))],
        out_shape=[jax.ShapeDtypeStruct(mixed.shape, mixed.dtype),
                   jax.ShapeDtypeStruct((HEADS, T // CHUNK, HD, HD), F32),
                   jax.ShapeDtypeStruct((HEADS, T // CHUNK, CHUNK, CHUNK), F32)],
        scratch_shapes=[pltpu.VMEM((HD, HD), F32), cs, cs, pltpu.VMEM((nb, CHUNK, CHUNK), F32), cs, cs],
        input_output_aliases={7: 0},
        compiler_params=_cp(("parallel", "arbitrary")))(qkv, qkv, qkv, bb, gb, proj, nw, mixed)


def _gdn_bwd(qkv, bb, gb, proj, nw, states, tms, dmixed, dproj):
    T = proj.shape[0]
    R = _gdn_rows(T)
    nb = R // CHUNK
    ntb = T // R

    def body(q_ref, k_ref, v_ref, bb_ref, gb_ref, z_ref, nw_ref, st_ref, tm_ref, dy_ref, _,
             dq_ref, dk_ref, dv_ref, dbb_ref, dgb_ref, dz_ref, dnw_ref,
             ds_ref, att_s, do_s, kd_s, vn_s, qd_s, w_s, dvn_s, dkd_s, dgl_s):
        hh = pl.program_id(0)
        t = pl.program_id(1)

        @pl.when(t == 0)
        def _():
            ds_ref[...] = jnp.zeros_like(ds_ref)

        @pl.when((t == 0) & (hh == 0))
        def _():
            dnw_ref[...] = jnp.zeros_like(dnw_ref)

        sh = (nb, CHUNK, HD)
        q, k, v = q_ref[...].reshape(sh), k_ref[...].reshape(sh), v_ref[...].reshape(sh)
        c = _gdn_chunk_common(q, k, v, bb_ref[...].reshape(sh), gb_ref[...].reshape(sh), tm_ref[...])
        w, u = c["wu"][:, :, :HD], c["wu"][:, :, HD:]
        sall = st_ref[...]
        vn = u - _bmm(w, sall, 2, 1)
        o = _bmm(c["qd"], sall, 2, 1) + _bmm(c["at"], vn, 2, 1)
        z = z_ref[...].reshape(sh)
        dy = dy_ref[...].reshape(sh)
        nwv = nw_ref[...].reshape(1, 1, HD)
        rs = lax.rsqrt(jnp.mean(o * o, axis=2, keepdims=True) + EPS)
        on = o * rs
        sg = _sigmoid(z)
        sz = z * sg
        dnw_ref[...] += jnp.sum(jnp.sum(dy * on * sz, axis=0), axis=0, keepdims=True)
        dz_ref[...] = (dy * on * nwv * (sg * (1.0 + z * (1.0 - sg)))).reshape(R, HD).astype(BF16)
        don = dy * nwv * sz
        do = rs * (don - on * jnp.mean(don * on, axis=2, keepdims=True))
        dqd = _bmm(do, sall, 2, 2)
        dat = jnp.where(c["tril"], _bmm(do, vn, 2, 2), 0.0)
        datt = jnp.where(c["triu"], _bmm(vn, do, 2, 2), 0.0)
        att_s[...] = c["att"]
        do_s[...] = do
        kd_s[...] = c["kd"]
        vn_s[...] = vn
        qd_s[...] = c["qd"]
        w_s[...] = w
        egl = c["egl"]
        for n in reversed(range(nb)):
            dso = ds_ref[...]
            dvn_n = _mm(att_s[n], do_s[n]) + _mm(kd_s[n], dso)
            dkd_s[n] = _mm(vn_s[n], dso, 1, 1)
            dgl = egl[n] * jnp.sum(jnp.sum(st_ref[n] * dso, axis=1, keepdims=True), axis=0, keepdims=True)
            dgl_s[n] = jnp.broadcast_to(dgl, (8, HD))
            ds_ref[...] = egl[n] * dso + _mm(qd_s[n], do_s[n], 0, 0) - _mm(w_s[n], dvn_n, 0, 0)
            dvn_s[n] = dvn_n
        dvn = dvn_s[...]
        dkd = dkd_s[...]
        dgl = dgl_s[...][:, 0:1, 0:1]
        dw = -_bmm(dvn, sall, 2, 2)
        dr = _bmm(c["tmt"], jnp.concatenate([dw, dvn], axis=2), 2, 1, HI, False)
        drw, dru = dr[:, :, :HD], dr[:, :, HD:]
        wu = c["wu"]
        da = -jnp.where(c["stl"], _bmm(dr, wu, 2, 2, HI, False), 0.0)
        da_t = -jnp.where(c["stu"], _bmm(wu, dr, 2, 2, HI, False), 0.0)
        beta, gam, dec, dect, kk = c["beta"], c["gam"], c["dec"], c["dect"], c["kk"]
        bcol, brow = c["bcol"], c["brow"]
        dbeta = (jnp.sum(da * kk * dec, axis=2, keepdims=True)
                 + jnp.sum(drw * k * gam + dru * v, axis=2, keepdims=True))
        dkk = bcol * da * dec
        dkk_t = brow * da_t * dect
        e = (bcol * da * kk + dat * c["qk"]) * dec
        e_t = (brow * da_t * kk + datt * c["qkt"]) * dect
        kd = c["kd"]
        dq_ref[...] = (_bmm(dat * dec, k, 2, 1) + dqd * gam).reshape(R, HD)
        dk_ref[...] = (_bmm(datt * dect, q, 2, 1) + _bmm(dkk + dkk_t, k, 2, 1) + dkd * c["kdf"]
                       + drw * (beta * gam)).reshape(R, HD)
        dv_ref[...] = (dru * beta).reshape(R, HD)
        skd = jnp.sum(dkd * kd, axis=2, keepdims=True)
        dgc = (jnp.sum(e, axis=2, keepdims=True) - jnp.sum(e_t, axis=2, keepdims=True)
               + jnp.sum(drw * c["rw"] + dqd * c["qd"], axis=2, keepdims=True) - skd)
        tot = jnp.sum(skd, axis=1, keepdims=True) + dgl
        rowi = lax.broadcasted_iota(jnp.int32, (nb, CHUNK, 1), 1)
        dgc = dgc + jnp.where(rowi == CHUNK - 1, tot, 0.0)
        dbb_ref[...] = jnp.broadcast_to(dbeta, sh).reshape(R, HD)
        dgb_ref[...] = jnp.broadcast_to(dgc, sh).reshape(R, HD)

    def rt(t):
        return ntb - 1 - t

    def hm(p):
        return pl.BlockSpec((None, None, R, HD), lambda h, t: (p, h, rt(t), 0))

    hb = pl.BlockSpec((None, R, HD), lambda h, t: (h, rt(t), 0))
    cs = pltpu.VMEM((nb, CHUNK, HD), F32)
    ob = jax.ShapeDtypeStruct((HEADS, T, HD), F32)
    return pl.pallas_call(
        body, name="gdn_bwd", grid=(HEADS, ntb),
        in_specs=[hm(0), hm(1), hm(2), hb, hb, pl.BlockSpec((R, HD), lambda h, t: (rt(t), C_Z // HD + h)),
                  pl.BlockSpec((1, HD), lambda h, t: (0, 0)),
                  pl.BlockSpec((None, nb, HD, HD), lambda h, t: (h, rt(t), 0, 0)),
                  pl.BlockSpec((None, nb, CHUNK, CHUNK), lambda h, t: (h, rt(t), 0, 0)),
                  pl.BlockSpec((R, HD), lambda h, t: (rt(t), M_GDN // HD + h)), pl.BlockSpec(memory_space=pl.ANY)],
        out_specs=[hb, hb, hb, hb, hb, pl.BlockSpec((R, HD), lambda h, t: (rt(t), C_Z // HD + h)),
                   pl.BlockSpec((1, HD), lambda h, t: (0, 0))],
        out_shape=[ob, ob, ob, ob, ob, jax.ShapeDtypeStruct(dproj.shape, dproj.dtype),
                   jax.ShapeDtypeStruct((1, HD), F32)],
        scratch_shapes=[pltpu.VMEM((HD, HD), F32), pltpu.VMEM((nb, CHUNK, CHUNK), F32), cs, cs, cs, cs, cs, cs, cs,
                        pltpu.VMEM((nb, 8, HD), F32)],
        input_output_aliases={10: 5},
        compiler_params=_cp(("arbitrary", "arbitrary")))(qkv, qkv, qkv, bb, gb, proj, nw, states, tms, dmixed,
                                                         dproj)


def _lru_gates(xc, wa, ba, wx, bx, lam, gpos):
    xb = xc.astype(BF16)
    r = _sigmoid(_mm(xb, wa) + ba)
    i = _sigmoid(_mm(xb, wx) + bx)
    sp = _softplus(-lam)
    log_a = -LRU_C * r * sp
    a = jnp.exp(log_a)
    mult = jnp.where(gpos == 0, 1.0, jnp.sqrt(-_expm1(2.0 * log_a)))
    return r, i, sp, a, mult


def _lru_fwd(xc, proj, wa, ba, wx, bx, lam, mixed):
    T = xc.shape[0]
    tt = _tile(T)

    def body(xc_ref, gr_ref, wa_ref, ba_ref, wx_ref, bx_ref, lam_ref, _, y_ref, h_ref, carry_ref):
        t = pl.program_id(1)

        @pl.when(t == 0)
        def _():
            carry_ref[...] = jnp.zeros_like(carry_ref)

        row = lax.broadcasted_iota(jnp.int32, (tt, 1), 0)
        xcv = xc_ref[...]
        r, i, sp, a, mult = _lru_gates(xcv, wa_ref[...], ba_ref[...], wx_ref[...], bx_ref[...], lam_ref[...],
                                       t * tt + row)
        av, bv = a, mult * i * xcv
        k = 1
        while k < tt:
            a_s = jnp.where(row >= k, pltpu.roll(av, k, 0), 1.0)
            b_s = jnp.where(row >= k, pltpu.roll(bv, k, 0), 0.0)
            bv = bv + av * b_s
            av = av * a_s
            k *= 2
        h = bv + av * carry_ref[0:1, :]
        carry_ref[...] = jnp.broadcast_to(h[tt - 1:tt, :], (8, 128))
        h_ref[...] = h
        y_ref[...] = (h * _gelu(gr_ref[...])).astype(BF16)

    blk = pl.BlockSpec((tt, 128), lambda j, t: (t, j))
    vec = pl.BlockSpec((1, 128), lambda j, t: (0, j))
    mat = pl.BlockSpec((None, 128, 128), lambda j, t: (j, 0, 0))
    return pl.pallas_call(
        body, name="lru_fwd", grid=(LRU_W // 128, T // tt),
        in_specs=[blk, pl.BlockSpec((tt, 128), lambda j, t: (t, C_GR // 128 + j)), mat, vec, mat, vec, vec,
                  pl.BlockSpec(memory_space=pl.ANY)],
        out_specs=[pl.BlockSpec((tt, 128), lambda j, t: (t, M_LRU // 128 + j)), blk],
        out_shape=[jax.ShapeDtypeStruct(mixed.shape, mixed.dtype), jax.ShapeDtypeStruct((T, LRU_W), F32)],
        scratch_shapes=[pltpu.VMEM((8, 128), F32)], input_output_aliases={7: 0},
        compiler_params=_cp(("parallel", "arbitrary")))(xc, proj, wa, ba, wx, bx, lam, mixed)


def _lru_bwd(dmixed, xc, proj, hst, wa, ba, wx, bx, lam, dproj):
    T = xc.shape[0]
    tt = _tile(T)
    nt = T // tt

    def body(dy_ref, xc_ref, gr_ref, h_ref, hp_ref, wa_ref, ba_ref, wx_ref, bx_ref, lam_ref, _,
             dxc_ref, dgr_ref, dwa_ref, dwx_ref, dba_ref, dbx_ref, dlam_ref, lc_ref, ac_ref):
        t = pl.program_id(1)
        tr = nt - 1 - t

        @pl.when(t == 0)
        def _():
            lc_ref[...] = jnp.zeros_like(lc_ref)
            ac_ref[...] = jnp.zeros_like(ac_ref)
            dwa_ref[...] = jnp.zeros_like(dwa_ref)
            dwx_ref[...] = jnp.zeros_like(dwx_ref)
            dba_ref[...] = jnp.zeros_like(dba_ref)
            dbx_ref[...] = jnp.zeros_like(dbx_ref)
            dlam_ref[...] = jnp.zeros_like(dlam_ref)

        row = lax.broadcasted_iota(jnp.int32, (tt, 1), 0)
        gpos = tr * tt + row
        xcv = xc_ref[...]
        wav, wxv, lamv = wa_ref[...], wx_ref[...], lam_ref[...]
        r, i, sp, a, mult = _lru_gates(xcv, wav, ba_ref[...], wxv, bx_ref[...], lamv, gpos)
        h = h_ref[...]
        dy = dy_ref[...]
        gg, dgg = _gelu_and_grad(gr_ref[...])
        dgr_ref[...] = (dy * h * dgg).astype(BF16)
        bv = dy * gg
        cv = jnp.where(row < tt - 1, pltpu.roll(a, tt - 1, 0), ac_ref[0:1, :])
        k = 1
        while k < tt:
            c_s = jnp.where(row < tt - k, pltpu.roll(cv, tt - k, 0), 1.0)
            b_s = jnp.where(row < tt - k, pltpu.roll(bv, tt - k, 0), 0.0)
            bv = bv + cv * b_s
            cv = cv * c_s
            k *= 2
        lm = bv + cv * lc_ref[0:1, :]
        lc_ref[...] = jnp.broadcast_to(lm[0:1, :], (8, 128))
        ac_ref[...] = jnp.broadcast_to(a[0:1, :], (8, 128))
        hp = jnp.where(tr == 0, 0.0, hp_ref[...])
        hs = pltpu.roll(jnp.concatenate([hp, h], axis=0), 1, 0)[8:, :]
        da = lm * hs
        dmult = lm * i * xcv
        di = lm * mult * xcv
        dxc = lm * mult * i
        dlog_a = a * da - jnp.where(gpos == 0, 0.0, dmult * a * a / mult)
        dr = dlog_a * (-LRU_C * sp)
        dsp = jnp.sum(dlog_a * (-LRU_C * r), axis=0, keepdims=True)
        dlam_ref[...] += dsp * (-_sigmoid(-lamv))
        dpr = dr * r * (1.0 - r)
        dpi = di * i * (1.0 - i)
        dba_ref[...] += jnp.sum(dpr, axis=0, keepdims=True)
        dbx_ref[...] += jnp.sum(dpi, axis=0, keepdims=True)
        dwa_ref[...] += _mm(xcv, dpr, 0, 0)
        dwx_ref[...] += _mm(xcv, dpi, 0, 0)
        dxc_ref[...] = dxc + _mm(dpr, wav, 1, 1) + _mm(dpi, wxv, 1, 1)

    def rt(t):
        return nt - 1 - t

    blk = pl.BlockSpec((tt, 128), lambda j, t: (rt(t), j))
    vec = pl.BlockSpec((1, 128), lambda j, t: (0, j))
    mat = pl.BlockSpec((None, 128, 128), lambda j, t: (j, 0, 0))
    h8 = tt // 8
    mshape = jax.ShapeDtypeStruct((LRU_W // 128, 128, 128), F32)
    vshape = jax.ShapeDtypeStruct((1, LRU_W), F32)
    return pl.pallas_call(
        body, name="lru_bwd", grid=(LRU_W // 128, nt),
        in_specs=[pl.BlockSpec((tt, 128), lambda j, t: (rt(t), M_LRU // 128 + j)), blk,
                  pl.BlockSpec((tt, 128), lambda j, t: (rt(t), C_GR // 128 + j)), blk,
                  pl.BlockSpec((8, 128), lambda j, t: (jnp.maximum(rt(t) * h8 - 1, 0), j)),
                  mat, vec, mat, vec, vec, pl.BlockSpec(memory_space=pl.ANY)],
        out_specs=[blk, pl.BlockSpec((tt, 128), lambda j, t: (rt(t), C_GR // 128 + j)), mat, mat, vec, vec, vec],
        out_shape=[jax.ShapeDtypeStruct((T, LRU_W), F32), jax.ShapeDtypeStruct(dproj.shape, dproj.dtype),
                   mshape, mshape, vshape, vshape, vshape],
        scratch_shapes=[pltpu.VMEM((8, 128), F32), pltpu.VMEM((8, 128), F32)], input_output_aliases={10: 1},
        compiler_params=_cp(("parallel", "arbitrary")))(dmixed, xc, proj, hst, hst, wa, ba, wx, bx, lam, dproj)


FFN_CB = 512
FFN_K = 3


def _ffn_conv(ge, w_ref):
    acc = ge * w_ref[FFN_K - 1:FFN_K, :]
    for j in range(FFN_K - 1):
        acc = acc + pltpu.roll(ge, FFN_K - 1 - j, 0) * w_ref[j:j + 1, :]
    return acc


def _ffn_gate_fwd(up, w):
    T = up.shape[0]
    tt = _tile(T)
    cb = FFN_CB
    nc = D_FF // cb
    h8 = tt // 8

    def body(g_ref, gp_ref, v_ref, w_ref, o_ref):
        t = pl.program_id(1)
        prev = jnp.where(t == 0, 0.0, gp_ref[...])
        gc = _ffn_conv(jnp.concatenate([prev, g_ref[...]], axis=0), w_ref)[8:, :]
        o_ref[...] = (_gelu(gc) * v_ref[...]).astype(BF16)

    return pl.pallas_call(
        body, name="ffn_gate_fwd", grid=(nc, T // tt),
        in_specs=[pl.BlockSpec((tt, cb), lambda j, t: (t, j)),
                  pl.BlockSpec((8, cb), lambda j, t: (jnp.maximum(t * h8 - 1, 0), j)),
                  pl.BlockSpec((tt, cb), lambda j, t: (t, nc + j)),
                  pl.BlockSpec((FFN_K, cb), lambda j, t: (0, j))],
        out_specs=pl.BlockSpec((tt, cb), lambda j, t: (t, j)),
        out_shape=jax.ShapeDtypeStruct((T, D_FF), BF16),
        compiler_params=_cp(("parallel", "parallel")))(up, up, up, w)


def _ffn_gate_bwd(dact, up, w):
    T = up.shape[0]
    tt = _tile(T)
    cb = FFN_CB
    nc = D_FF // cb
    nt = T // tt
    h8 = tt // 8

    def body(d_ref, dn_ref, g_ref, gp_ref, gn_ref, v_ref, vn_ref, w_ref, dup_ref, dw_ref):
        t = pl.program_id(1)
        prev = jnp.where(t == 0, 0.0, gp_ref[...])
        ge = jnp.concatenate([prev, g_ref[...], gn_ref[...]], axis=0)
        gc = _ffn_conv(ge, w_ref)[8:, :]
        gg, dgg = _gelu_and_grad(gc)
        de = jnp.concatenate([d_ref[...], jnp.where(t == nt - 1, 0.0, dn_ref[...])], axis=0)
        ve = jnp.concatenate([v_ref[...], vn_ref[...]], axis=0)
        dup_ref[1] = (de * gg)[:tt, :].astype(BF16)
        dgc = de * ve * dgg
        n = tt + 8
        acc = dgc * w_ref[FFN_K - 1:FFN_K, :]
        for j in range(FFN_K - 1):
            acc = acc + pltpu.roll(dgc, n - (FFN_K - 1 - j), 0) * w_ref[j:j + 1, :]
        dup_ref[0] = acc[:tt, :].astype(BF16)

        @pl.when(t == 0)
        def _():
            dw_ref[...] = jnp.zeros_like(dw_ref)

        dgm = dgc[:tt, :]
        for j in range(FFN_K):
            sh = FFN_K - 1 - j
            xs = ge[8:8 + tt, :] if sh == 0 else pltpu.roll(ge, sh, 0)[8:8 + tt, :]
            dw_ref[j:j + 1, :] += jnp.sum(dgm * xs, axis=0, keepdims=True)

    def nxt(t):
        return jnp.minimum((t + 1) * h8, T // 8 - 1)

    return pl.pallas_call(
        body, name="ffn_gate_bwd", grid=(nc, nt),
        in_specs=[pl.BlockSpec((tt, cb), lambda j, t: (t, j)),
                  pl.BlockSpec((8, cb), lambda j, t: (nxt(t), j)),
                  pl.BlockSpec((tt, cb), lambda j, t: (t, j)),
                  pl.BlockSpec((8, cb), lambda j, t: (jnp.maximum(t * h8 - 1, 0), j)),
                  pl.BlockSpec((8, cb), lambda j, t: (nxt(t), j)),
                  pl.BlockSpec((tt, cb), lambda j, t: (t, nc + j)),
                  pl.BlockSpec((8, cb), lambda j, t: (nxt(t), nc + j)),
                  pl.BlockSpec((FFN_K, cb), lambda j, t: (0, j))],
        out_specs=[pl.BlockSpec((2, tt, cb), lambda j, t: (0, t, j)),
                   pl.BlockSpec((FFN_K, cb), lambda j, t: (0, j))],
        out_shape=[jax.ShapeDtypeStruct((2, T, D_FF), BF16), jax.ShapeDtypeStruct((FFN_K, D_FF), F32)],
        compiler_params=_cp(("parallel", "arbitrary")))(dact, dact, up, up, up, up, up, w)


def _row_tile(rows, cap):
    best = 8
    for r in range(8, min(rows, cap) + 1, 8):
        if rows % r == 0:
            best = r
    return best


def _adamw(parts, w, m, v, rt, name, layer=None, prev=None):
    P, R, C = parts.shape

    def body(p_ref, w_ref, m_ref, v_ref, *rest):
        g_ref, d_ref, mo_ref, vo_ref = rest[-4:]
        g = p_ref[0].astype(F32)
        for i in range(1, P):
            g = g + p_ref[i].astype(F32)
        wv = w_ref[...]
        mn = ADAM_B1 * m_ref[...] + (1.0 - ADAM_B1) * g
        vn = ADAM_B2 * v_ref[...] + (1.0 - ADAM_B2) * (g * g)
        m_hat = mn / (1.0 - ADAM_B1 ** ADAM_STEP)
        v_hat = vn / (1.0 - ADAM_B2 ** ADAM_STEP)
        g_ref[...] = g
        d_ref[...] = -ADAM_LR * (m_hat / (jnp.sqrt(v_hat) + ADAM_EPS) + ADAM_WD * wv)
        mo_ref[...] = mn
        vo_ref[...] = vn

    if layer is None:
        blk = pl.BlockSpec((rt, C), lambda r: (r, 0))
        sh = jax.ShapeDtypeStruct((R, C), F32)
    else:
        blk = pl.BlockSpec((None, rt, C), lambda r: (layer, r, 0))
        sh = jax.ShapeDtypeStruct(w.shape, F32)
    extra = list(prev) if prev is not None else []
    return pl.pallas_call(
        body, name=name, grid=(R // rt,),
        in_specs=[pl.BlockSpec((P, rt, C), lambda r: (0, r, 0)), blk, blk, blk]
        + [pl.BlockSpec(memory_space=pl.ANY)] * len(extra),
        out_specs=[blk, blk, blk, blk], out_shape=[sh, sh, sh, sh],
        input_output_aliases={4 + i: i for i in range(len(extra))},
        compiler_params=_cp(("parallel",)))(parts, w, m, v, *extra)


def _peer(k):
    x, y, c = lax.axis_index("x"), lax.axis_index("y"), lax.axis_index("c")
    px = 1 - x if k & 4 else x
    py = 1 - y if k & 2 else y
    pc = 1 - c if k & 1 else c
    return (px, py, pc), 4 * px + 2 * py + pc


def _all_gather(x, name):
    R, C = x.shape

    def body(x_ref, o_ref, send_sems, recv_sems, local_sem):
        me = 4 * lax.axis_index("x") + 2 * lax.axis_index("y") + lax.axis_index("c")
        mine = pltpu.make_async_copy(x_ref, o_ref.at[me], local_sem)
        mine.start()
        sends = []
        for k in range(1, N_DEV):
            dev, _ = _peer(k)
            cp = pltpu.make_async_remote_copy(src_ref=x_ref, dst_ref=o_ref.at[me], send_sem=send_sems.at[k - 1],
                                              recv_sem=recv_sems.at[k - 1], device_id=dev, device_id_type=MESH_IDS)
            cp.start()
            sends.append(cp)
        for k in range(1, N_DEV):
            dev, idx = _peer(k)
            pltpu.make_async_remote_copy(src_ref=x_ref, dst_ref=o_ref.at[idx], send_sem=send_sems.at[k - 1],
                                         recv_sem=recv_sems.at[k - 1], device_id=dev,
                                         device_id_type=MESH_IDS).wait_recv()
        for cp in sends:
            cp.wait_send()
        mine.wait()

    return pl.pallas_call(
        body, name=name, in_specs=[pl.BlockSpec(memory_space=pl.ANY)], out_specs=pl.BlockSpec(memory_space=pl.ANY),
        out_shape=jax.ShapeDtypeStruct((N_DEV, R, C), x.dtype),
        scratch_shapes=[pltpu.SemaphoreType.DMA((N_DEV - 1,)), pltpu.SemaphoreType.DMA((N_DEV - 1,)),
                        pltpu.SemaphoreType.DMA],
        compiler_params=pltpu.CompilerParams(has_side_effects=True))(x)


HBM_SPEC = pl.BlockSpec(memory_space=pltpu.HBM)
SEM_SPEC = pl.BlockSpec(memory_space=pltpu.SEMAPHORE)
EFFECT = pltpu.SideEffectType.DATAFLOW_SIDE_EFFECTING
OTHER_CHIPS = ((1, 0), (0, 1), (1, 1))


def _split_start(bufs, plan, n, name):
    nb = len(bufs)

    def body(*refs):
        send_sems, recv_sems, token = refs[nb], refs[nb + 1], refs[2 * nb + 2]
        for i, (src, dst, _, dev) in enumerate(plan(refs[:nb])):
            pltpu.make_async_remote_copy(src_ref=src, dst_ref=dst, send_sem=send_sems.at[i],
                                         recv_sem=recv_sems.at[i], device_id=dev, device_id_type=MESH_IDS).start()
        token[...] = jnp.zeros_like(token)

    outs = pl.pallas_call(
        body, name=name,
        out_shape=(pltpu.SemaphoreType.DMA((n,)), pltpu.SemaphoreType.DMA((n,)),
                   *[pltpu.HBM(b.shape, b.dtype) for b in bufs], jax.ShapeDtypeStruct((8, 128), F32)),
        in_specs=[HBM_SPEC] * nb,
        out_specs=(SEM_SPEC, SEM_SPEC, *[HBM_SPEC] * nb, pl.BlockSpec(memory_space=pltpu.VMEM)),
        input_output_aliases={i: 2 + i for i in range(nb)},
        compiler_params=pltpu.CompilerParams(has_side_effects=EFFECT),
    )(*[pltpu.with_memory_space_constraint(b, pltpu.HBM) for b in bufs])
    return dict(send=outs[0], recv=outs[1], bufs=list(outs[2:2 + nb]), token=outs[2 + nb], plan=plan, n=n)


def _split_wait(st, after, name):
    bufs = st["bufs"]
    nb = len(bufs)
    plan = st["plan"]
    afters = list(after) if isinstance(after, (list, tuple)) else [after]

    def body(*refs):
        send_sems, recv_sems = refs[nb], refs[nb + 1]
        for i, (src, dst, land, dev) in enumerate(plan(refs[:nb])):
            pltpu.make_async_remote_copy(src_ref=src, dst_ref=dst, send_sem=send_sems.at[i],
                                         recv_sem=recv_sems.at[i], device_id=dev,
                                         device_id_type=MESH_IDS).wait_send()
            pltpu.make_async_remote_copy(src_ref=src, dst_ref=land, send_sem=send_sems.at[i],
                                         recv_sem=recv_sems.at[i], device_id=dev,
                                         device_id_type=MESH_IDS).wait_recv()

    outs = pl.pallas_call(
        body, name=name, out_shape=tuple(pltpu.HBM(b.shape, b.dtype) for b in bufs),
        in_specs=[HBM_SPEC] * nb + [SEM_SPEC, SEM_SPEC] + [pl.BlockSpec(memory_space=pl.ANY)] * len(afters),
        out_specs=tuple([HBM_SPEC] * nb), input_output_aliases={i: i for i in range(nb)},
        compiler_params=pltpu.CompilerParams(has_side_effects=EFFECT),
    )(*bufs, st["send"], st["recv"], *afters)
    return list(outs)


def _xyc():
    return lax.axis_index("x"), lax.axis_index("y"), lax.axis_index("c")


def _flip(x, y, a, b):
    return (1 - x if a else x), (1 - y if b else y)


def _ag1_plan(outs):
    x, y, c = _xyc()
    me = 4 * x + 2 * y + c
    copies = []
    for o in outs:
        copies.append((o.at[me], o.at[me], o.at[4 * x + 2 * y + 1 - c], (x, y, 1 - c)))
        for a, b in OTHER_CHIPS:
            px, py = _flip(x, y, a, b)
            copies.append((o.at[me], o.at[me], o.at[4 * px + 2 * py + c], (px, py, c)))
    return copies


def _ag2_plan(outs):
    x, y, c = _xyc()
    copies = []
    for o in outs:
        for a, b in OTHER_CHIPS:
            px, py = _flip(x, y, a, b)
            mine, sibs = 4 * px + 2 * py + c, 4 * px + 2 * py + 1 - c
            copies.append((o.at[mine], o.at[mine], o.at[sibs], (x, y, 1 - c)))
    return copies


def _rs1_plan(refs):
    x, y, c = _xyc()
    copies = []
    for g, land in zip(refs[0::2], refs[1::2]):
        for j in range(4):
            copies.append((g.at[2 * j + 1 - c], land.at[j], land.at[j], (x, y, 1 - c)))
    return copies


def _rs2_plan(refs):
    x, y, c = _xyc()
    mychip = 2 * x + y
    copies = []
    for s, land in zip(refs[0::2], refs[1::2]):
        for a, b in OTHER_CHIPS:
            px, py = _flip(x, y, a, b)
            copies.append((s.at[2 * px + py], land.at[mychip], land.at[2 * px + py], (px, py, c)))
    return copies


def _landing_like(g, name):
    def body(g_ref, o_ref):
        del g_ref, o_ref

    anyspec = pl.BlockSpec(memory_space=pl.ANY)
    return pl.pallas_call(body, name=name, in_specs=[anyspec], out_specs=anyspec,
                          out_shape=jax.ShapeDtypeStruct((4,) + g.shape[1:], g.dtype))(g)


def _place(x, slots, by_chip, name):
    def body(x_ref, o_ref, sem):
        xi, yi, ci = _xyc()
        idx = 2 * xi + yi if by_chip else 4 * xi + 2 * yi + ci
        cp = pltpu.make_async_copy(x_ref.at[idx] if by_chip else x_ref, o_ref.at[idx], sem)
        cp.start()
        cp.wait()

    anyspec = pl.BlockSpec(memory_space=pl.ANY)
    shape = (slots,) + (x.shape[1:] if by_chip else x.shape)
    return pl.pallas_call(body, name=name, in_specs=[anyspec], out_specs=anyspec,
                          out_shape=jax.ShapeDtypeStruct(shape, x.dtype),
                          scratch_shapes=[pltpu.SemaphoreType.DMA])(x)


def _pair_sum(g, land, cidx, name):
    _, R, C = land.shape
    rt = _row_tile(R, 512)
    g4 = g.reshape(4, 2, R, C)

    def body(c_ref, g_ref, l_ref, o_ref):
        o_ref[...] = (g_ref[...].astype(F32) + l_ref[...].astype(F32)).astype(o_ref.dtype)

    grid_spec = pltpu.PrefetchScalarGridSpec(
        num_scalar_prefetch=1, grid=(4, R // rt),
        in_specs=[pl.BlockSpec((None, None, rt, C), lambda j, r, c_ref: (j, c_ref[0], r, 0)),
                  pl.BlockSpec((None, rt, C), lambda j, r, c_ref: (j, r, 0))],
        out_specs=pl.BlockSpec((None, rt, C), lambda j, r, c_ref: (j, r, 0)))
    return pl.pallas_call(body, name=name, grid_spec=grid_spec, out_shape=jax.ShapeDtypeStruct(land.shape, land.dtype),
                          compiler_params=_cp(("parallel", "parallel")))(cidx, g4, land)


def _no_hook(event, l, after, payload=None):
    return None


def _tie(x, token):
    if token is None:
        return x

    def body(x_ref, t_ref, o_ref):
        del x_ref, t_ref, o_ref

    anyspec = pl.BlockSpec(memory_space=pl.ANY)
    return pl.pallas_call(body, name="tie", in_specs=[anyspec, anyspec], out_specs=anyspec,
                          out_shape=jax.ShapeDtypeStruct(x.shape, x.dtype), input_output_aliases={0: 0})(x, token)


def _layer_fwd(x, W, l, hook=_no_hook):
    T = x.shape[0]
    n = f"l{l}_"
    h1 = _norm_fwd(x, W["norm1"], n + "norm1_fwd")
    proj = _mm_nn(h1, W["win"], F32, n + "mm_in", tn_c=(768,))
    y_pool = _pool_fwd(proj, W["pool_w"], W["pool_b"], W["pool_s"])
    cpre = _conv_fwd(proj, C_QKV, 3 * GDN_W, W["gconv_w"], None, 256, n + "gdn_conv_fwd")
    qkv, bb, gb = _gdn_pre_fwd(cpre, proj, W["alog"], W["dtb"])
    mixed, states, tms = _gdn_fwd(qkv, bb, gb, proj, W["gnorm"], y_pool)
    lconv_w = _tie(W["lconv_w"], hook("f_mix", l, states))
    xc = _conv_fwd(proj, C_XR, LRU_W, lconv_w, W["lconv_b"], 128, n + "lru_conv_fwd")
    mixed, hst = _lru_fwd(xc, proj, W["wa"], W["ba"], W["wx"], W["bx"], W["lam"], mixed)
    hook("f_out", l, hst)
    x1 = _mm_nn(mixed, W["wout"], F32, n + "mm_out", add=x)
    h2 = _norm_fwd(x1, W["norm2"], n + "norm2_fwd")
    up = _mm_up(h2, W["wup"], n + "mm_up")
    act = _ffn_gate_fwd(up, W["fconv_w"])
    act = _tie(act, hook("f_act", l, act))
    x2 = _mm_nn(act, W["wdown"], F32, n + "mm_down", add=x1)
    hook("f_end", l, x2)
    saved = dict(x=x, h1=h1, proj=proj, cpre=cpre, qkv=qkv, bb=bb, gb=gb, states=states, tms=tms, xc=xc, hst=hst,
                 mixed=mixed, x1=x1, h2=h2, up=up, act=act)
    return x2, saved


def _layer_bwd(dx2, W, S, l, hook=_no_hook):
    T = dx2.shape[0]
    n = f"l{l}_"
    dact = _mm_nt(dx2, W["wdown"], F32, n + "mm_down_dx", tk_c=(2048,))
    g_wdown = _mm_tn(S["act"], dx2, BF16, n + "mm_down_dw")
    dup, g_fconv = _ffn_gate_bwd(dact, S["up"], W["fconv_w"])
    ns = W["wup"].shape[2]
    dh2 = _mm_up_t(dup, W["wup"], n + "mm_up_dx")
    g_wup = _mm_dup(S["h2"], dup, ns, n + "mm_up_dw")
    tok = hook("b_ffn", l, g_wup, dict(ffn_down=g_wdown, ffn_up=g_wup))
    dx1, g_norm2 = _norm_bwd(S["x1"], _tie(W["norm2"], tok), dh2, dx2, n + "norm2_bwd")
    dmixed = _mm_nt(dx1, W["wout"], F32, n + "mm_out_dx", tk_c=(2048,))
    g_wout = _mm_tn(S["mixed"], dx1, BF16, n + "mm_out_dw")
    tok = hook("b_mid", l, g_wout)
    proj = S["proj"]
    dproj, g_pool_w, g_pool_b, g_pool_s = _pool_bwd(dmixed, proj, W["pool_w"], W["pool_b"], _tie(W["pool_s"], tok))
    dq, dk, dv, dbb, dgb, dproj, g_gnorm = _gdn_bwd(S["qkv"], S["bb"], S["gb"], proj, W["gnorm"], S["states"],
                                                    S["tms"], dmixed, dproj)
    dc, dproj, g_alog, g_dtb = _gdn_pre_bwd(dq, dk, dv, S["cpre"], dbb, dgb, proj, W["alog"], W["dtb"], dproj)
    dproj, g_gconv = _conv_bwd(dc, proj, C_QKV, W["gconv_w"], 256, n + "gdn_conv_bwd", dproj)
    dxc, dproj, g_wa, g_wx, g_ba, g_bx, g_lam = _lru_bwd(dmixed, S["xc"], proj, S["hst"], W["wa"], W["ba"], W["wx"],
                                                          W["bx"], W["lam"], dproj)
    dproj, g_lconv, g_lconv_b = _conv_bwd(dxc, proj, C_XR, W["lconv_w"], 128, n + "lru_conv_bwd", dproj,
                                          want_db=True)
    dh1 = _mm_nt(dproj, W["win"], F32, n + "mm_in_dx", tk_c=(768,))
    g_win = _mm_tn(S["h1"], dproj, BF16, n + "mm_in_dw", tn_c=(768,))
    tok = hook("b_in", l, g_win, dict(w_out=g_wout, w_in=g_win))
    dx, g_norm1 = _norm_bwd(S["x"], _tie(W["norm1"], tok), dh1, dx1, n + "norm1_bwd")
    big = dict(w_in=g_win, w_out=g_wout, ffn_up=g_wup, ffn_down=g_wdown)
    small = dict(norm1_w=g_norm1[0], pool_w=g_pool_w, pool_b=g_pool_b.reshape(4, 128), pool_scale=g_pool_s[0],
                 gdn_conv_w=g_gconv, gdn_a_log=g_alog[0, :HEADS], gdn_dt_bias=g_dtb[0, :HEADS],
                 gdn_norm_w=g_gnorm[0], lru_conv_w=g_lconv, lru_conv_b=g_lconv_b[0], lru_wa=g_wa, lru_ba=g_ba[0],
                 lru_wx=g_wx, lru_bx=g_bx[0], lru_lambda=g_lam[0], norm2_w=g_norm2[0], ffn_conv_w=g_fconv)
    dx = _tie(dx, hook("b_end", l, dx, small))
    return dx, big, small


def _pad_lane(v):
    return jnp.pad(v, (0, 128 - v.shape[0])).reshape(1, 128)


def _layer_weights(l, big, P, conv_full):
    return dict(
        win=big.get("w_in"), wout=big.get("w_out"), wup=big.get("ffn_up"), wdown=big.get("ffn_down"),
        norm1=P["norm1_w"][l].reshape(1, D_MODEL), norm2=P["norm2_w"][l].reshape(1, D_MODEL),
        pool_w=P["pool_w"][l], pool_b=P["pool_b"][l].reshape(1, POOL_W), pool_s=P["pool_scale"][l].reshape(1, POOL_W),
        gconv_w=conv_full["gdn_conv_w"][l], alog=_pad_lane(P["gdn_a_log"][l]), dtb=_pad_lane(P["gdn_dt_bias"][l]),
        gnorm=P["gdn_norm_w"][l].reshape(1, HD),
        lconv_w=conv_full["lru_conv_w"][l], lconv_b=P["lru_conv_b"][l].reshape(1, LRU_W),
        wa=P["lru_wa"][l], ba=P["lru_ba"][l].reshape(1, LRU_W), wx=P["lru_wx"][l],
        bx=P["lru_bx"][l].reshape(1, LRU_W), lam=P["lru_lambda"][l].reshape(1, LRU_W),
        fconv_w=conv_full["ffn_conv_w"][l])


def _local_step(x, target, Ws, final_norm_w, hook=_no_hook):
    saved = []
    for l in range(DEPTH):
        x, s = _layer_fwd(x, Ws[l], l, hook)
        saved.append(s)
    loss, dx, g_final = _loss_head(x, final_norm_w.reshape(1, D_MODEL), target)
    bigs, smalls = [None] * DEPTH, [None] * DEPTH
    for l in reversed(range(DEPTH)):
        dx, bigs[l], smalls[l] = _layer_bwd(dx, Ws[l], saved[l], l, hook)
    return loss, dx, g_final[0], bigs, smalls


SMALL_REPL = ("norm1_w", "pool_w", "pool_b", "pool_scale", "gdn_a_log", "gdn_dt_bias", "gdn_norm_w", "lru_conv_b",
              "lru_wa", "lru_ba", "lru_wx", "lru_bx", "lru_lambda", "norm2_w", "final_norm_w")
SMALL_SHARD = ("gdn_conv_w", "lru_conv_w", "ffn_conv_w")
BIG = ("w_in", "w_out", "ffn_up", "ffn_down")
WEIGHTS = ("norm1_w", "w_in", "pool_w", "pool_b", "pool_scale", "gdn_conv_w", "gdn_a_log", "gdn_dt_bias",
           "gdn_norm_w", "lru_conv_w", "lru_conv_b", "lru_wa", "lru_ba", "lru_wx", "lru_bx", "lru_lambda", "w_out",
           "norm2_w", "ffn_up", "ffn_conv_w", "ffn_down", "final_norm_w")
SEG = 1024


def _pack(arrs):
    pieces, table, off = [], [], 0
    for a in arrs:
        n = a.size
        npad = -(-n // SEG) * SEG
        pieces.append(jnp.pad(a.reshape(-1).astype(F32), (0, npad - n)))
        table.append((off, n, a.shape))
        off += npad
    return jnp.concatenate(pieces).reshape(off // 128, 128), table


def _unpack(buf, table):
    flat = buf.reshape(-1)
    return [flat[off:off + n].reshape(shape) for off, n, shape in table]


def _pad_in(w):
    z1 = jnp.zeros(w.shape[:-1] + (C_XR - AB_ORIG_END,), w.dtype)
    return jnp.concatenate([w[..., :AB_ORIG_END], z1, w[..., AB_ORIG_END:]], axis=-1)


def _unpad_in(w):
    return jnp.concatenate([w[..., :AB_ORIG_END], w[..., C_XR:C_GR + LRU_W]], axis=-1)


def kernel(x, norm1_w, w_in, pool_w, pool_b, pool_scale, gdn_conv_w, gdn_a_log, gdn_dt_bias, gdn_norm_w, lru_conv_w, lru_conv_b, lru_wa, lru_ba, lru_wx, lru_bx, lru_lambda, w_out, norm2_w, ffn_up, ffn_conv_w, ffn_down, final_norm_w, loss_target, m_norm1_w, m_w_in, m_pool_w, m_pool_b, m_pool_scale, m_gdn_conv_w, m_gdn_a_log, m_gdn_dt_bias, m_gdn_norm_w, m_lru_conv_w, m_lru_conv_b, m_lru_wa, m_lru_ba, m_lru_wx, m_lru_bx, m_lru_lambda, m_w_out, m_norm2_w, m_ffn_up, m_ffn_conv_w, m_ffn_down, m_final_norm_w, v_norm1_w, v_w_in, v_pool_w, v_pool_b, v_pool_scale, v_gdn_conv_w, v_gdn_a_log, v_gdn_dt_bias, v_gdn_norm_w, v_lru_conv_w, v_lru_conv_b, v_lru_wa, v_lru_ba, v_lru_wx, v_lru_bx, v_lru_lambda, v_w_out, v_norm2_w, v_ffn_up, v_ffn_conv_w, v_ffn_down, v_final_norm_w):
    loc = dict(locals())
    Wp = {n: loc[n] for n in WEIGHTS}
    Mp = {n: loc["m_" + n] for n in WEIGHTS}
    Vp = {n: loc["v_" + n] for n in WEIGHTS}
    xi, yi, ci = _xyc()
    me = 4 * xi + 2 * yi + ci
    mychip = 2 * xi + yi
    cidx = ci.astype(jnp.int32).reshape(1)
    keys = dict(w_in="win", w_out="wout", ffn_up="wup", ffn_down="wdown")

    def shard2d(d, name, l):
        a = d[name][l]
        return _pad_in(a) if name == "w_in" else a

    def wshard(l, name):
        return shard2d(Wp, name, l).astype(BF16)

    def full2d(name, full):
        return full if name == "ffn_up" else full.reshape(-1, full.shape[2])

    def ag_start(shards, tag, token=None):
        if token is not None:
            shards = [_tie(shards[0], token)] + list(shards[1:])
        bufs = [_place(s, N_DEV, False, f"place_{tag}{i}") for i, s in enumerate(shards)]
        return _split_start(bufs, _ag1_plan, 4 * len(bufs), f"ag1s_{tag}")

    def ag_mid(st, after, tag):
        bufs = _split_wait(st, after, f"ag1w_{tag}")
        return _split_start(bufs, _ag2_plan, 3 * len(bufs), f"ag2s_{tag}")

    def ag_end(st, after, tag):
        return _split_wait(st, after, f"ag2w_{tag}")

    def rs_start(gs, tag):
        bufs = []
        for nm, g in gs.items():
            if nm != "ffn_up":
                g = g.reshape(N_DEV, g.shape[0] // N_DEV, g.shape[1])
            bufs += [g, _landing_like(g, f"land_{nm}_{tag}")]
        st = _split_start(bufs, _rs1_plan, 4 * len(gs), f"rs1s_{tag}")
        st["names"] = list(gs)
        return st

    def rs_mid(st, after, tag):
        bufs = _split_wait(st, after, f"rs1w_{tag}")
        out = []
        for i, nm in enumerate(st["names"]):
            s = _pair_sum(bufs[2 * i], bufs[2 * i + 1], cidx, f"pairsum_{nm}_{tag}")
            out += [s, _place(s, 4, True, f"place_{nm}_{tag}")]
        st2 = _split_start(out, _rs2_plan, 3 * len(st["names"]), f"rs2s_{tag}")
        st2["names"] = st["names"]
        return st2

    def rs_end(st, after, tag):
        bufs = _split_wait(st, after, f"rs2w_{tag}")
        return dict(zip(st["names"], bufs[1::2]))

    lnames = tuple(n for n in SMALL_REPL if n != "final_norm_w") + SMALL_SHARD

    def small_pack(l, gs, extra):
        return _pack([gs[nm] for nm in lnames] + extra)

    def small_state(d, l, gs):
        arrs = [d[nm][l] if nm in SMALL_REPL else jnp.zeros(gs[nm].shape, F32) for nm in lnames]
        if l == 0:
            arrs += [d["final_norm_w"], jnp.zeros((1,), F32)]
        return _pack(arrs)[0]

    stA = ag_start([wshard(0, "w_in")], "a")
    stA2 = ag_mid(stA, stA["token"], "a")
    stB = ag_start([wshard(0, n) for n in BIG[1:]], "b", stA2["token"])
    (w_in0,) = ag_end(stA2, stB["token"], "a")

    cbuf, ctable = _pack([Wp[n] for n in SMALL_SHARD])
    call = _all_gather(cbuf, "ag_conv_w")
    parts = [_unpack(call[i], ctable) for i in range(N_DEV)]
    conv_full = {n: jnp.concatenate([parts[i][j] for i in range(N_DEV)], axis=-1) for j, n in enumerate(SMALL_SHARD)}

    Ws = [_layer_weights(l, {}, Wp, conv_full) for l in range(DEPTH)]
    Ws[0]["win"] = full2d("w_in", w_in0)
    st = {}

    def hook(event, l, after, payload=None):
        if event == "f_mix" and l == 0:
            st["b2"] = ag_mid(stB, after, "b")
            st["c"] = ag_start([wshard(1, n) for n in BIG], "c", st["b2"]["token"])
            return st["c"]["token"]
        if event == "f_out" and l == 0:
            for n, b in zip(BIG[1:], ag_end(st["b2"], after, "b")):
                Ws[0][keys[n]] = full2d(n, b)
        if event == "f_act" and l == 0:
            st["c2"] = ag_mid(st["c"], after, "c")
            return st["c2"]["token"]
        if event == "f_end" and l == 0:
            for n, b in zip(BIG, ag_end(st["c2"], after, "c")):
                Ws[1][keys[n]] = full2d(n, b)
        if event == "b_ffn":
            st["ffn", l] = rs_start(payload, f"ffn{l}")
            return st["ffn", l]["token"]
        if event == "b_mid":
            st["ffn2", l] = rs_mid(st["ffn", l], after, f"ffn{l}")
            if l == 0:
                st["sm1b"] = ag_mid(st["sm1"], st["ffn2", l]["token"], "sm1")
                return st["sm1b"]["token"]
            return st["ffn2", l]["token"]
        if event == "b_in":
            st["io", l] = rs_start(payload, f"io{l}")
            if l == 0:
                st["sm1g"] = ag_end(st["sm1b"], st["io", l]["token"], "sm1")[0]
            return st["io", l]["token"]
        if event == "b_end" and l == 1:
            st["io2", 1] = rs_mid(st["io", 1], after, "io1")
            gbuf1, st["table1"] = small_pack(1, payload, [])
            st["sm1"] = ag_start([gbuf1], "sm1", st["io2", 1]["token"])
            return st["sm1"]["token"]
        return None

    loss, dx, g_final, _, gsmall = _local_step(x[0], loss_target[0], Ws, final_norm_w, hook)

    out_g, out_d, out_m, out_v = {}, {}, {}, {}
    outs4 = (out_g, out_d, out_m, out_v)
    rts = dict(w_in=64, w_out=128, ffn_up=256, ffn_down=128)
    big_res = {}

    def stacked(d, name):
        return _pad_in(d[name]) if name == "w_in" else d[name]

    wmv = {name: [stacked(d, name) for d in (Wp, Mp, Vp)] for name in BIG}

    def adam_big(l, parts):
        for name, p in parts.items():
            big_res[name] = _adamw(p, *wmv[name], rts[name], f"adamw_{name}_{l}", layer=l, prev=big_res.get(name))
        return [big_res[name][0] for name in parts]

    def adam_small(l, gall, gs):
        rs = gall.shape[1]
        return _adamw(gall, small_state(Wp, l, gs), small_state(Mp, l, gs), small_state(Vp, l, gs),
                      _row_tile(rs, 512), f"adamw_small_{l}")

    gbuf0, table0 = small_pack(0, gsmall[0], [g_final, loss[0, :1]])
    sm0 = ag_start([gbuf0], "sm0", st["io", 0]["token"])
    o = adam_big(1, rs_end(st["ffn2", 1], sm0["token"], "ffn1"))
    st["io2", 0] = rs_mid(st["io", 0], o, "io0")
    o = adam_big(1, rs_end(st["io2", 1], st["io2", 0]["token"], "io1"))
    o = adam_big(0, rs_end(st["ffn2", 0], o, "ffn0"))
    small_res = {1: adam_small(1, _tie(st["sm1g"], o[-1]), gsmall[1])}
    sm0b = ag_mid(sm0, o + [small_res[1][0]], "sm0")
    small_res[0] = adam_small(0, ag_end(sm0b, sm0b["token"], "sm0")[0], gsmall[0])
    adam_big(0, rs_end(st["io2", 0], small_res[0][0], "io0"))

    for name in BIG:
        for i, dst in enumerate(outs4):
            dst[name] = _unpad_in(big_res[name][i]) if name == "w_in" else big_res[name][i]

    unp = {0: [_unpack(r, table0) for r in small_res[0]], 1: [_unpack(r, st["table1"]) for r in small_res[1]]}
    for j, nm in enumerate(lnames):
        if nm in SMALL_REPL:
            for i, dst in enumerate(outs4):
                dst[nm] = jnp.stack([unp[l][i][j] for l in range(DEPTH)])
    for i, dst in enumerate(outs4):
        dst["final_norm_w"] = unp[0][i][len(lnames)]
    loss_total = unp[0][0][len(lnames) + 1][0]

    gsh = []
    for nm in SMALL_SHARD:
        j = lnames.index(nm)
        width = Wp[nm].shape[-1]
        gsh.append(jnp.stack([lax.dynamic_slice_in_dim(unp[l][0][j], me * width, width, axis=1)
                              for l in range(DEPTH)]))
    sbuf, stable = _pack(gsh)
    res = _adamw(sbuf[None], _pack([Wp[n] for n in SMALL_SHARD])[0], _pack([Mp[n] for n in SMALL_SHARD])[0],
                 _pack([Vp[n] for n in SMALL_SHARD])[0], sbuf.shape[0], "adamw_conv_w")
    unp2 = [_unpack(r, stable) for r in res]
    for j, nm in enumerate(SMALL_SHARD):
        for i, dst in enumerate((out_g, out_d, out_m, out_v)):
            dst[nm] = unp2[i][j]

    return (loss_total, dx[None], *[out_g[n] for n in WEIGHTS], *[out_d[n] for n in WEIGHTS],
            *[out_m[n] for n in WEIGHTS], *[out_v[n] for n in WEIGHTS])
```

```python
import functools

import jax
import jax.numpy as jnp
from jax import lax
from jax.experimental import pallas as pl
from jax.experimental.pallas import tpu as pltpu

F32 = jnp.float32
BF16 = jnp.bfloat16
HI = lax.Precision.HIGHEST
MESH_IDS = pl.DeviceIdType.MESH

N_DEV = 8
D_MODEL = 2048
DEPTH = 2
POOL_WINDOWS = (2, 4, 8, 16)
POOL_W = 512
HEADS = 6
HD = 128
GDN_W = HEADS * HD
CHUNK = 64
LRU_W = 768
LRU_C = 8.0
D_FF = 3 * D_MODEL
EPS = 1e-6
IN_COLS = 5132
PCOLS = 5376
C_QKV, C_Z, C_AB, C_XR, C_GR = 512, 2816, 3584, 3840, 4608
AB_ORIG_END = 3596
M_GDN, M_LRU = 512, 1280

ADAM_LR, ADAM_B1, ADAM_B2, ADAM_EPS, ADAM_WD, ADAM_STEP = 0.001, 0.9, 0.999, 1e-08, 0.01, 10

VMEM_LIMIT = 56 * 1024 * 1024


def _cp(sem):
    return pltpu.CompilerParams(dimension_semantics=sem, vmem_limit_bytes=VMEM_LIMIT)


def _mm(a, b, ca=1, cb=0, prec=None, cast=True):
    if cast:
        a = a.astype(BF16)
        b = b.astype(BF16)
    return lax.dot_general(a, b, (((ca,), (cb,)), ((), ())), preferred_element_type=F32, precision=prec)


def _bmm(a, b, ca=2, cb=1, prec=None, cast=True):
    if cast:
        a = a.astype(BF16)
        b = b.astype(BF16)
    return lax.dot_general(a, b, (((ca,), (cb,)), ((0,), (0,))), preferred_element_type=F32, precision=prec)


def _sigmoid(x):
    return 1.0 / (1.0 + jnp.exp(-x))


def _log1p(e):
    u = 1.0 + e
    return jnp.where(u == 1.0, e, jnp.log(u) * e / jnp.where(u == 1.0, 1.0, u - 1.0))


def _softplus(x):
    return jnp.maximum(x, 0.0) + _log1p(jnp.exp(-jnp.abs(x)))


def _expm1(x):
    u = jnp.exp(x)
    um = u - 1.0
    safe = jnp.where((u == 1.0) | (um == -1.0), 1.0, jnp.log(u))
    return jnp.where(u == 1.0, x, jnp.where(um == -1.0, -1.0, um * x / safe))


_G0 = 0.7978845608028654
_G1 = 0.044715


def _gelu(x):
    return 0.5 * x * (1.0 + jnp.tanh(_G0 * (x + _G1 * x * x * x)))


def _gelu_and_grad(x):
    th = jnp.tanh(_G0 * (x + _G1 * x * x * x))
    g = 0.5 * x * (1.0 + th)
    dg = 0.5 * (1.0 + th) + 0.5 * x * (1.0 - th * th) * _G0 * (1.0 + 3.0 * _G1 * x * x)
    return g, dg


def _tile(T):
    return min(T, 512)


def _matmul(a, b, *, grid, a_spec, b_spec, out_shape, out_spec, dims, acc_shape, name, add=None, add_spec=None):
    nk = grid[2]
    has_add = add is not None

    def body(*refs):
        if has_add:
            a_ref, b_ref, add_ref, o_ref, acc_ref = refs
        else:
            a_ref, b_ref, o_ref, acc_ref = refs
            add_ref = None
        k = pl.program_id(2)
        p = lax.dot_general(a_ref[...].astype(BF16), b_ref[...].astype(BF16), (dims, ((), ())),
                            preferred_element_type=F32)

        def finish(r):
            if has_add:
                r = r + add_ref[...]
            o_ref[...] = r.astype(o_ref.dtype)

        if nk == 1:
            finish(p)
        else:
            @pl.when(k == 0)
            def _():
                acc_ref[...] = p

            @pl.when(k > 0)
            def _():
                acc_ref[...] += p

            @pl.when(k == nk - 1)
            def _():
                finish(acc_ref[...])

    in_specs = [a_spec, b_spec] + ([add_spec] if has_add else [])
    args = (a, b) + ((add,) if has_add else ())
    return pl.pallas_call(
        body, name=name, grid=grid, in_specs=in_specs, out_specs=out_spec, out_shape=out_shape,
        scratch_shapes=[pltpu.VMEM(acc_shape, F32)],
        compiler_params=_cp(("parallel", "parallel", "arbitrary")),
    )(*args)


def _pick(n, cands):
    for c in cands:
        if n % c == 0:
            return c
    raise ValueError(f"no tile for {n}")


def _mm_nn(a, b, out_dtype, name, add=None, tn_c=(1024, 768, 512)):
    M, K = a.shape
    N = b.shape[1]
    tm = _pick(M, (1024, 512, 256))
    tn = _pick(N, tn_c)
    tk = _pick(K, (2048, 1536, 1024, 512, 256))
    return _matmul(
        a, b, grid=(M // tm, N // tn, K // tk),
        a_spec=pl.BlockSpec((tm, tk), lambda i, j, k: (i, k)),
        b_spec=pl.BlockSpec((tk, tn), lambda i, j, k: (k, j)),
        out_shape=jax.ShapeDtypeStruct((M, N), out_dtype),
        out_spec=pl.BlockSpec((tm, tn), lambda i, j, k: (i, j)),
        dims=((1,), (0,)), acc_shape=(tm, tn), name=name, add=add,
        add_spec=pl.BlockSpec((tm, tn), lambda i, j, k: (i, j)))


def _mm_nt(a, b, out_dtype, name, tk_c=(2048, 1536, 1024, 768, 512)):
    M, K = a.shape
    N = b.shape[0]
    tm = _pick(M, (1024, 512, 256))
    tn = _pick(N, (1024, 768, 512))
    tk = _pick(K, tk_c)
    return _matmul(
        a, b, grid=(M // tm, N // tn, K // tk),
        a_spec=pl.BlockSpec((tm, tk), lambda i, j, k: (i, k)),
        b_spec=pl.BlockSpec((tn, tk), lambda i, j, k: (j, k)),
        out_shape=jax.ShapeDtypeStruct((M, N), out_dtype),
        out_spec=pl.BlockSpec((tm, tn), lambda i, j, k: (i, j)),
        dims=((1,), (1,)), acc_shape=(tm, tn), name=name)


def _mm_tn(a, b, out_dtype, name, tn_c=(1024, 768, 512)):
    K, M = a.shape
    N = b.shape[1]
    tm = _pick(M, (1024, 768, 512))
    tn = _pick(N, tn_c)
    tk = _pick(K, (1024, 512, 256))
    return _matmul(
        a, b, grid=(M // tm, N // tn, K // tk),
        a_spec=pl.BlockSpec((tk, tm), lambda i, j, k: (k, i)),
        b_spec=pl.BlockSpec((tk, tn), lambda i, j, k: (k, j)),
        out_shape=jax.ShapeDtypeStruct((M, N), out_dtype),
        out_spec=pl.BlockSpec((tm, tn), lambda i, j, k: (i, j)),
        dims=((0,), (0,)), acc_shape=(tm, tn), name=name)


def _mm_up(h, wup, name):
    M, K = h.shape
    ns = wup.shape[2]
    tm = _pick(M, (1024, 512, 256))
    tn = 768
    per = ns // tn
    return _matmul(
        h, wup, grid=(M // tm, N_DEV * per, 1),
        a_spec=pl.BlockSpec((tm, K), lambda i, j, k: (i, 0)),
        b_spec=pl.BlockSpec((None, K, tn), lambda i, j, k: (j // per, 0, j % per)),
        out_shape=jax.ShapeDtypeStruct((M, N_DEV * ns), F32),
        out_spec=pl.BlockSpec((tm, tn), lambda i, j, k: (i, j)),
        dims=((1,), (0,)), acc_shape=(tm, tn), name=name)


def _mm_up_t(dup, wup, name):
    M = dup.shape[1]
    D, ns = wup.shape[1], wup.shape[2]
    tm = _pick(M, (1024, 512, 256))
    tn = 1024
    tk = ns
    return _matmul(
        dup, wup, grid=(M // tm, D // tn, N_DEV),
        a_spec=pl.BlockSpec((None, tm, tk), lambda i, j, k: (k // 4, i, k % 4)),
        b_spec=pl.BlockSpec((None, tn, tk), lambda i, j, k: (k, j, 0)),
        out_shape=jax.ShapeDtypeStruct((M, D), F32),
        out_spec=pl.BlockSpec((tm, tn), lambda i, j, k: (i, j)),
        dims=((1,), (1,)), acc_shape=(tm, tn), name=name)


def _mm_dup(ht, dup, ns, name):
    M, K = ht.shape
    tm = 1024
    tn = 768
    per = ns // tn
    half = 4 * per
    tk = _pick(K, (2048, 1024, 512, 256))
    return _matmul(
        ht, dup, grid=(M // tm, N_DEV * per, K // tk),
        a_spec=pl.BlockSpec((tm, tk), lambda i, j, k: (i, k)),
        b_spec=pl.BlockSpec((None, tk, tn), lambda i, j, k: (j // half, k, j % half)),
        out_shape=jax.ShapeDtypeStruct((N_DEV, M, ns), BF16),
        out_spec=pl.BlockSpec((None, tm, tn), lambda i, j, k: (j // per, i, j % per)),
        dims=((1,), (0,)), acc_shape=(tm, tn), name=name)


def _norm_fwd(x, w, name):
    T, D = x.shape
    tt = _tile(T)

    def body(x_ref, w_ref, h_ref, ht_ref):
        xv = x_ref[...]
        r = lax.rsqrt(jnp.mean(xv * xv, axis=1, keepdims=True) + EPS)
        hv = xv * r * w_ref[...]
        h_ref[...] = hv.astype(BF16)
        ht_ref[...] = hv.T.astype(BF16)

    return pl.pallas_call(
        body, name=name, grid=(T // tt,),
        in_specs=[pl.BlockSpec((tt, D), lambda t: (t, 0)), pl.BlockSpec((1, D), lambda t: (0, 0))],
        out_specs=[pl.BlockSpec((tt, D), lambda t: (t, 0)), pl.BlockSpec((D, tt), lambda t: (0, t))],
        out_shape=[jax.ShapeDtypeStruct((T, D), BF16), jax.ShapeDtypeStruct((D, T), BF16)],
        compiler_params=_cp(("parallel",)))(x, w)


def _norm_bwd(x, w, dh, dres, name):
    T, D = x.shape
    tt = _tile(T)

    def body(x_ref, w_ref, dh_ref, dres_ref, dx_ref, dw_ref):
        t = pl.program_id(0)
        xv = x_ref[...]
        r = lax.rsqrt(jnp.mean(xv * xv, axis=1, keepdims=True) + EPS)
        xh = xv * r
        dh_v = dh_ref[...]
        dxh = dh_v * w_ref[...]
        dx_ref[...] = dres_ref[...] + r * (dxh - xh * jnp.mean(dxh * xh, axis=1, keepdims=True))
        part = jnp.sum(dh_v * xh, axis=0, keepdims=True)

        @pl.when(t == 0)
        def _():
            dw_ref[...] = part

        @pl.when(t > 0)
        def _():
            dw_ref[...] += part

    row = pl.BlockSpec((tt, D), lambda t: (t, 0))
    vec = pl.BlockSpec((1, D), lambda t: (0, 0))
    return pl.pallas_call(
        body, name=name, grid=(T // tt,), in_specs=[row, vec, row, row], out_specs=[row, vec],
        out_shape=[jax.ShapeDtypeStruct((T, D), F32), jax.ShapeDtypeStruct((1, D), F32)],
        compiler_params=_cp(("arbitrary",)))(x, w, dh, dres)


def _loss_head(x, w, target):
    T, D = x.shape
    tt = _tile(T)

    def body(x_ref, w_ref, t_ref, loss_ref, dx_ref, dw_ref):
        t = pl.program_id(0)
        xv = x_ref[...]
        r = lax.rsqrt(jnp.mean(xv * xv, axis=1, keepdims=True) + EPS)
        xh = xv * r
        err = xh * w_ref[...] - t_ref[...]
        lp = 0.5 * jnp.sum(jnp.mean(err * err, axis=1, keepdims=True), axis=0, keepdims=True)
        dy = err * (1.0 / D)
        dxh = dy * w_ref[...]
        dx_ref[...] = r * (dxh - xh * jnp.mean(dxh * xh, axis=1, keepdims=True))
        part = jnp.sum(dy * xh, axis=0, keepdims=True)
        lpb = jnp.broadcast_to(lp, (1, 128))

        @pl.when(t == 0)
        def _():
            dw_ref[...] = part
            loss_ref[...] = lpb

        @pl.when(t > 0)
        def _():
            dw_ref[...] += part
            loss_ref[...] += lpb

    row = pl.BlockSpec((tt, D), lambda t: (t, 0))
    vec = pl.BlockSpec((1, D), lambda t: (0, 0))
    return pl.pallas_call(
        body, name="loss_head", grid=(T // tt,), in_specs=[row, vec, row],
        out_specs=[pl.BlockSpec((1, 128), lambda t: (0, 0)), row, vec],
        out_shape=[jax.ShapeDtypeStruct((1, 128), F32), jax.ShapeDtypeStruct((T, D), F32),
                   jax.ShapeDtypeStruct((1, D), F32)],
        compiler_params=_cp(("arbitrary",)))(x, w, target)


def _conv_fwd(x, col0, C, w, b, cb, name):
    T = x.shape[0]
    K = w.shape[0]
    tt = _tile(T)
    nt, nc, c0 = T // tt, C // cb, col0 // cb
    has_b = b is not None

    def body(*refs):
        if has_b:
            x_ref, halo_ref, w_ref, b_ref, y_ref = refs
        else:
            x_ref, halo_ref, w_ref, y_ref = refs
        t = pl.program_id(1)
        halo = jnp.where(t == 0, 0.0, halo_ref[...])
        xe = jnp.concatenate([halo, x_ref[...]], axis=0)
        acc = xe * w_ref[K - 1:K, :]
        for j in range(K - 1):
            acc = acc + pltpu.roll(xe, K - 1 - j, 0) * w_ref[j:j + 1, :]
        if has_b:
            acc = acc + b_ref[...]
        y_ref[...] = acc[8:, :]

    in_specs = [pl.BlockSpec((tt, cb), lambda j, t: (t, c0 + j)),
                pl.BlockSpec((8, cb), lambda j, t: (jnp.maximum(t * (tt // 8) - 1, 0), c0 + j)),
                pl.BlockSpec((K, cb), lambda j, t: (0, j))]
    args = [x, x, w]
    if has_b:
        in_specs.append(pl.BlockSpec((1, cb), lambda j, t: (0, j)))
        args.append(b)
    return pl.pallas_call(
        body, name=name, grid=(nc, nt), in_specs=in_specs,
        out_specs=pl.BlockSpec((tt, cb), lambda j, t: (t, j)),
        out_shape=jax.ShapeDtypeStruct((T, C), F32), compiler_params=_cp(("parallel", "parallel")))(*args)


def _conv_bwd(dy, x, col0, w, cb, name, into, want_db=False):
    T, C = dy.shape
    K = w.shape[0]
    tt = _tile(T)
    nt, nc, c0 = T // tt, C // cb, col0 // cb

    def body(*refs):
        if want_db:
            dy_ref, dyn_ref, x_ref, xp_ref, w_ref, _, dx_ref, dw_ref, db_ref = refs
        else:
            dy_ref, dyn_ref, x_ref, xp_ref, w_ref, _, dx_ref, dw_ref = refs
        t = pl.program_id(1)
        dyv = dy_ref[...]
        nxt = jnp.where(t == nt - 1, 0.0, dyn_ref[...])
        dye = jnp.concatenate([dyv, nxt], axis=0)
        n = tt + 8
        acc = dye * w_ref[K - 1:K, :]
        for j in range(K - 1):
            acc = acc + pltpu.roll(dye, n - (K - 1 - j), 0) * w_ref[j:j + 1, :]
        dx_ref[...] = acc[:tt, :].astype(dx_ref.dtype)
        prev = jnp.where(t == 0, 0.0, xp_ref[...])
        xe = jnp.concatenate([prev, x_ref[...]], axis=0)

        @pl.when(t == 0)
        def _():
            dw_ref[...] = jnp.zeros_like(dw_ref)
            if want_db:
                db_ref[...] = jnp.zeros_like(db_ref)

        for j in range(K):
            sh = K - 1 - j
            xs = xe[8:, :] if sh == 0 else pltpu.roll(xe, sh, 0)[8:, :]
            dw_ref[j:j + 1, :] += jnp.sum(dyv * xs, axis=0, keepdims=True)
        if want_db:
            db_ref[...] += jnp.sum(dyv, axis=0, keepdims=True)

    h8 = tt // 8
    in_specs = [pl.BlockSpec((tt, cb), lambda j, t: (t, j)),
                pl.BlockSpec((8, cb), lambda j, t: (jnp.minimum((t + 1) * h8, T // 8 - 1), j)),
                pl.BlockSpec((tt, cb), lambda j, t: (t, c0 + j)),
                pl.BlockSpec((8, cb), lambda j, t: (jnp.maximum(t * h8 - 1, 0), c0 + j)),
                pl.BlockSpec((K, cb), lambda j, t: (0, j)), pl.BlockSpec(memory_space=pl.ANY)]
    out_specs = [pl.BlockSpec((tt, cb), lambda j, t: (t, c0 + j)), pl.BlockSpec((K, cb), lambda j, t: (0, j))]
    out_shape = [jax.ShapeDtypeStruct(into.shape, into.dtype), jax.ShapeDtypeStruct((K, C), F32)]
    if want_db:
        out_specs.append(pl.BlockSpec((1, cb), lambda j, t: (0, j)))
        out_shape.append(jax.ShapeDtypeStruct((1, C), F32))
    return pl.pallas_call(
        body, name=name, grid=(nc, nt), in_specs=in_specs, out_specs=out_specs, out_shape=out_shape,
        input_output_aliases={5: 0}, compiler_params=_cp(("parallel", "arbitrary")))(dy, dy, x, x, w, into)


def _pool_d(ue, g, pos, tt):
    win = POOL_WINDOWS[g]
    ug = ue[:, g * 128:(g + 1) * 128]
    s = ug
    k = 1
    while k < win:
        s = s + pltpu.roll(s, k, 0)
        k *= 2
    cnt = jnp.minimum(pos + 1, win).astype(F32)
    return s[16:, :] / cnt - ug[16:, :]


def _pool_fwd(proj, pw, pb, ps):
    T = proj.shape[0]
    tt = _tile(T)

    def body(u_ref, halo_ref, w_ref, b_ref, s_ref, y_ref):
        t = pl.program_id(0)
        halo = jnp.where(t == 0, 0.0, halo_ref[...])
        ue = jnp.concatenate([halo, u_ref[...]], axis=0)
        pos = t * tt + lax.broadcasted_iota(jnp.int32, (tt, 1), 0)
        for g in range(4):
            sl = slice(g * 128, (g + 1) * 128)
            d = _pool_d(ue, g, pos, tt)
            yg = _mm(d, w_ref[g]) + b_ref[:, sl]
            y_ref[:, sl] = (yg * s_ref[:, sl]).astype(BF16)

    vec = pl.BlockSpec((1, POOL_W), lambda t: (0, 0))
    return pl.pallas_call(
        body, name="pool_fwd", grid=(T // tt,),
        in_specs=[pl.BlockSpec((tt, POOL_W), lambda t: (t, 0)),
                  pl.BlockSpec((16, POOL_W), lambda t: (jnp.maximum(t * (tt // 16) - 1, 0), 0)),
                  pl.BlockSpec((4, 128, 128), lambda t: (0, 0, 0)), vec, vec],
        out_specs=pl.BlockSpec((tt, POOL_W), lambda t: (t, 0)),
        out_shape=jax.ShapeDtypeStruct((T, D_MODEL), BF16), compiler_params=_cp(("parallel",)))(
            proj, proj, pw, pb, ps)


def _pool_bwd(dmixed, proj, pw, pb, ps):
    T = proj.shape[0]
    tt = _tile(T)
    nt = T // tt

    def body(dy_ref, dyn_ref, u_ref, halo_ref, w_ref, b_ref, s_ref, du_ref, dw_ref, db_ref, ds_ref):
        t = pl.program_id(0)
        halo = jnp.where(t == 0, 0.0, halo_ref[...])
        ue = jnp.concatenate([halo, u_ref[...]], axis=0)
        dyv = dy_ref[...]
        nxt = jnp.where(t == nt - 1, 0.0, dyn_ref[...])
        dye = jnp.concatenate([dyv, nxt], axis=0)
        n = tt + 16
        pos = t * tt + lax.broadcasted_iota(jnp.int32, (tt, 1), 0)
        pos_e = t * tt + lax.broadcasted_iota(jnp.int32, (n, 1), 0)

        @pl.when(t == 0)
        def _():
            dw_ref[...] = jnp.zeros_like(dw_ref)
            db_ref[...] = jnp.zeros_like(db_ref)
            ds_ref[...] = jnp.zeros_like(ds_ref)

        for g in range(4):
            win = POOL_WINDOWS[g]
            sl = slice(g * 128, (g + 1) * 128)
            d = _pool_d(ue, g, pos, tt)
            wg = w_ref[g]
            ypre = _mm(d, wg) + b_ref[:, sl]
            sc = s_ref[:, sl]
            ds_ref[:, sl] += jnp.sum(dyv[:, sl] * ypre, axis=0, keepdims=True)
            dyp_e = dye[:, sl] * sc
            dyp = dyp_e[:tt, :]
            db_ref[:, sl] += jnp.sum(dyp, axis=0, keepdims=True)
            dw_ref[g] += _mm(d, dyp, 0, 0)
            dd_e = _mm(dyp_e, wg, 1, 1)
            cnt_e = jnp.minimum(pos_e + 1, win).astype(F32)
            s = dd_e / cnt_e
            k = 1
            while k < win:
                s = s + pltpu.roll(s, n - k, 0)
                k *= 2
            du_ref[:, sl] = (s[:tt, :] - dd_e[:tt, :]).astype(BF16)

    vec = pl.BlockSpec((1, POOL_W), lambda t: (0, 0))
    h16 = tt // 16
    return pl.pallas_call(
        body, name="pool_bwd", grid=(nt,),
        in_specs=[pl.BlockSpec((tt, POOL_W), lambda t: (t, 0)),
                  pl.BlockSpec((16, POOL_W), lambda t: (jnp.minimum((t + 1) * h16, T // 16 - 1), 0)),
                  pl.BlockSpec((tt, POOL_W), lambda t: (t, 0)),
                  pl.BlockSpec((16, POOL_W), lambda t: (jnp.maximum(t * h16 - 1, 0), 0)),
                  pl.BlockSpec((4, 128, 128), lambda t: (0, 0, 0)), vec, vec],
        out_specs=[pl.BlockSpec((tt, POOL_W), lambda t: (t, 0)),
                   pl.BlockSpec((4, 128, 128), lambda t: (0, 0, 0)), vec, vec],
        out_shape=[jax.ShapeDtypeStruct((T, PCOLS), BF16), jax.ShapeDtypeStruct((4, 128, 128), F32),
                   jax.ShapeDtypeStruct((1, POOL_W), F32), jax.ShapeDtypeStruct((1, POOL_W), F32)],
        compiler_params=_cp(("arbitrary",)))(dmixed, dmixed, proj, proj, pw, pb, ps)


def _gdn_pre_fwd(cpre, proj, alog, dtb):
    T = cpre.shape[0]
    tt = _tile(T)

    def body(c_ref, ab_ref, alog_ref, dtb_ref, qkv_ref, bb_ref, gb_ref):
        for p in range(3):
            for h in range(HEADS):
                cc = c_ref[:, (p * HEADS + h) * HD:(p * HEADS + h + 1) * HD]
                s = cc * _sigmoid(cc)
                if p < 2:
                    s = s * lax.rsqrt(jnp.sum(s * s, axis=1, keepdims=True) + EPS)
                if p == 0:
                    s = s * (HD ** -0.5)
                qkv_ref[p, h] = s
        ab = ab_ref[...]
        g = -jnp.exp(alog_ref[...]) * _softplus(ab + dtb_ref[...])
        r64 = lax.broadcasted_iota(jnp.int32, (tt, 1), 0) & (CHUNK - 1)
        k = 1
        while k < CHUNK:
            g = g + jnp.where(r64 >= k, pltpu.roll(g, k, 0), 0.0)
            k *= 2
        sb = _sigmoid(ab)
        for h in range(HEADS):
            gb_ref[h] = jnp.broadcast_to(g[:, h:h + 1], (tt, HD))
            bb_ref[h] = jnp.broadcast_to(sb[:, HEADS + h:HEADS + h + 1], (tt, HD))

    vec = pl.BlockSpec((1, 128), lambda t: (0, 0))
    hb = pl.BlockSpec((HEADS, tt, HD), lambda t: (0, t, 0))
    return pl.pallas_call(
        body, name="gdn_pre_fwd", grid=(T // tt,),
        in_specs=[pl.BlockSpec((tt, 3 * GDN_W), lambda t: (t, 0)),
                  pl.BlockSpec((tt, 128), lambda t: (t, C_AB // 128)), vec, vec],
        out_specs=[pl.BlockSpec((3, HEADS, tt, HD), lambda t: (0, 0, t, 0)), hb, hb],
        out_shape=[jax.ShapeDtypeStruct((3, HEADS, T, HD), F32), jax.ShapeDtypeStruct((HEADS, T, HD), F32),
                   jax.ShapeDtypeStruct((HEADS, T, HD), F32)],
        compiler_params=_cp(("parallel",)))(cpre, proj, alog, dtb)


def _gdn_pre_bwd(dq, dk, dv, cpre, dbb, dgb, proj, alog, dtb, dproj):
    T = cpre.shape[0]
    tt = _tile(T)

    def body(dq_ref, dk_ref, dv_ref, c_ref, dbb_ref, dgb_ref, ab_ref, alog_ref, dtb_ref, _,
             dc_ref, dab_ref, dalog_ref, ddtb_ref):
        t = pl.program_id(0)
        srcs = (dq_ref, dk_ref, dv_ref)
        for p in range(3):
            for h in range(HEADS):
                sl = slice((p * HEADS + h) * HD, (p * HEADS + h + 1) * HD)
                cc = c_ref[:, sl]
                sg = _sigmoid(cc)
                s = cc * sg
                dyv = srcs[p][h]
                if p < 2:
                    r = lax.rsqrt(jnp.sum(s * s, axis=1, keepdims=True) + EPS)
                    y = s * r
                    if p == 0:
                        dyv = dyv * (HD ** -0.5)
                    ds = r * (dyv - y * jnp.sum(dyv * y, axis=1, keepdims=True))
                else:
                    ds = dyv
                dc_ref[:, sl] = ds * sg * (1.0 + cc * (1.0 - sg))
        lane = lax.broadcasted_iota(jnp.int32, (tt, 128), 1)
        dg = jnp.zeros((tt, 128), F32)
        dbeta = jnp.zeros((tt, 128), F32)
        for h in range(HEADS):
            dg = jnp.where(lane == h, dgb_ref[h], dg)
            dbeta = jnp.where(lane == HEADS + h, dbb_ref[h], dbeta)
        r64 = lax.broadcasted_iota(jnp.int32, (tt, 1), 0) & (CHUNK - 1)
        k = 1
        while k < CHUNK:
            dg = dg + jnp.where(r64 < CHUNK - k, pltpu.roll(dg, tt - k, 0), 0.0)
            k *= 2
        ab = ab_ref[...]
        e = jnp.exp(alog_ref[...])
        xx = ab + dtb_ref[...]
        g = -e * _softplus(xx)
        da = jnp.where(lane < HEADS, dg * (-e) * _sigmoid(xx), 0.0)
        pa = jnp.sum(jnp.where(lane < HEADS, dg * g, 0.0), axis=0, keepdims=True)
        pd = jnp.sum(da, axis=0, keepdims=True)

        @pl.when(t == 0)
        def _():
            dalog_ref[...] = pa
            ddtb_ref[...] = pd

        @pl.when(t > 0)
        def _():
            dalog_ref[...] += pa
            ddtb_ref[...] += pd

        sb = _sigmoid(ab)
        dab_ref[:, :128] = jnp.where(lane < HEADS, da, dbeta * sb * (1.0 - sb)).astype(BF16)
        dab_ref[:, 128:] = jnp.zeros((tt, 128), BF16)

    vec = pl.BlockSpec((1, 128), lambda t: (0, 0))
    hb = pl.BlockSpec((HEADS, tt, HD), lambda t: (0, t, 0))
    return pl.pallas_call(
        body, name="gdn_pre_bwd", grid=(T // tt,),
        in_specs=[hb, hb, hb, pl.BlockSpec((tt, 3 * GDN_W), lambda t: (t, 0)), hb, hb,
                  pl.BlockSpec((tt, 128), lambda t: (t, C_AB // 128)), vec, vec, pl.BlockSpec(memory_space=pl.ANY)],
        out_specs=[pl.BlockSpec((tt, 3 * GDN_W), lambda t: (t, 0)),
                   pl.BlockSpec((tt, 256), lambda t: (t, C_AB // 256)), vec, vec],
        out_shape=[jax.ShapeDtypeStruct((T, 3 * GDN_W), F32), jax.ShapeDtypeStruct(dproj.shape, dproj.dtype),
                   jax.ShapeDtypeStruct((1, 128), F32), jax.ShapeDtypeStruct((1, 128), F32)],
        input_output_aliases={9: 1},
        compiler_params=_cp(("arbitrary",)))(dq, dk, dv, cpre, dbb, dgb, proj, alog, dtb, dproj)


def _split2(x):
    hi = x.astype(BF16)
    return hi, (x - hi.astype(F32)).astype(BF16)


def _bmm3s(a2, b2, ca=2, cb=1):
    def f(x, y):
        return lax.dot_general(x, y, (((ca,), (cb,)), ((0,), (0,))), preferred_element_type=F32)

    return f(a2[0], b2[0]) + (f(a2[0], b2[1]) + f(a2[1], b2[0]))


def _bmm3(a, b, ca=2, cb=1):
    return _bmm3s(_split2(a), _split2(b), ca, cb)


def _tri_inv(a):
    nb = a.shape[0]
    ri = lax.broadcasted_iota(jnp.int32, (nb, CHUNK, CHUNK), 1)
    ci = lax.broadcasted_iota(jnp.int32, (nb, CHUNK, CHUNK), 2)
    n = -a
    p = jnp.where(ri == ci, 1.0, 0.0) + n
    n2 = _split2(n)
    for _ in range(5):
        n2 = _split2(_bmm3s(n2, n2))
        p = p + _bmm3s(_split2(p), n2)
    return p


def _gdn_chunk_common(q, k, v, bb3, gb3, tm_saved=None):
    nb = q.shape[0]
    need_t = tm_saved is not None
    beta = bb3[:, :, 0:1]
    gcol = gb3[:, :, 0:1]
    bcol = bb3[:, :, :CHUNK]
    gcm = gb3[:, :, :CHUNK]
    oh = jnp.where(lax.broadcasted_iota(jnp.int32, (nb, CHUNK, HD), 2) == 0, 1.0, 0.0)
    grow = _bmm(oh, gb3, 2, 2, HI, False)
    ri = lax.broadcasted_iota(jnp.int32, (nb, CHUNK, CHUNK), 1)
    ci = lax.broadcasted_iota(jnp.int32, (nb, CHUNK, CHUNK), 2)
    tril, stl = ri >= ci, ri > ci
    dg = gcm - grow
    dec = jnp.where(tril, jnp.exp(jnp.where(tril, dg, 0.0)), 0.0)
    kk = _bmm(k, k, 2, 2)
    qk = _bmm(q, k, 2, 2)
    tm = tm_saved if need_t else _tri_inv(jnp.where(stl, bcol * kk * dec, 0.0))
    gam = jnp.exp(gcol)
    glast = gb3[:, CHUNK - 1:CHUNK, 0:1]
    egl = jnp.exp(glast)
    rw = k * (beta * gam)
    ru = v * beta
    wu = _bmm3(tm, jnp.concatenate([rw, ru], axis=2), 2, 1)
    kdf = jnp.exp(glast - gcol)
    out = dict(beta=beta, bcol=bcol, tril=tril, stl=stl, dec=dec, kk=kk, qk=qk, tm=tm, gam=gam, egl=egl,
               rw=rw, wu=wu, at=qk * dec, qd=q * gam, kdf=kdf, kd=k * kdf)
    if need_t:
        brow = _bmm(oh, bb3, 2, 2, HI, False)
        triu, stu = ri <= ci, ri < ci
        dect = jnp.where(triu, jnp.exp(jnp.where(triu, -dg, 0.0)), 0.0)
        qkt = _bmm(k, q, 2, 2)
        eye = jnp.where(ri == ci, 1.0, 0.0)
        out.update(brow=brow, triu=triu, stu=stu, dect=dect, qkt=qkt, tmt=_bmm3(eye, tm, 2, 2),
                   att=qkt * dect)
    return out


def _gdn_rows(T):
    return min(T, 512)


def _gdn_fwd(qkv, bb, gb, proj, nw, mixed):
    T = proj.shape[0]
    R = _gdn_rows(T)
    nb = R // CHUNK

    def body(q_ref, k_ref, v_ref, bb_ref, gb_ref, z_ref, nw_ref, _, y_ref, st_ref, tm_ref,
             s_ref, w_s, u_s, at_s, qd_s, kd_s):
        t = pl.program_id(1)

        @pl.when(t == 0)
        def _():
            s_ref[...] = jnp.zeros_like(s_ref)

        sh = (nb, CHUNK, HD)
        q, k, v = q_ref[...].reshape(sh), k_ref[...].reshape(sh), v_ref[...].reshape(sh)
        c = _gdn_chunk_common(q, k, v, bb_ref[...].reshape(sh), gb_ref[...].reshape(sh))
        tm_ref[...] = c["tm"]
        w_s[...] = c["wu"][:, :, :HD]
        u_s[...] = c["wu"][:, :, HD:]
        at_s[...] = c["at"]
        qd_s[...] = c["qd"]
        kd_s[...] = c["kd"]
        egl = c["egl"]
        nwv = nw_ref[...]
        for n in range(nb):
            s = s_ref[...]
            st_ref[n] = s
            vn = u_s[n] - _mm(w_s[n], s)
            o = _mm(qd_s[n], s) + _mm(at_s[n], vn)
            s_ref[...] = s * egl[n] + _mm(kd_s[n], vn, 0, 0)
            rows = slice(n * CHUNK, (n + 1) * CHUNK)
            zz = z_ref[rows, :]
            on = o * lax.rsqrt(jnp.mean(o * o, axis=1, keepdims=True) + EPS)
            y_ref[rows, :] = (on * nwv * (zz * _sigmoid(zz))).astype(BF16)

    def hm(p):
        return pl.BlockSpec((None, None, R, HD), lambda h, t: (p, h, t, 0))

    hb = pl.BlockSpec((None, R, HD), lambda h, t: (h, t, 0))
    cs = pltpu.VMEM((nb, CHUNK, HD), F32)
    return pl.pallas_call(
        body, name="gdn_fwd", grid=(HEADS, T // R),
        in_specs=[hm(0), hm(1), hm(2), hb, hb, pl.BlockSpec((R, HD), lambda h, t: (t, C_Z // HD + h)),
                  pl.BlockSpec((1, HD), lambda h, t: (0, 0)), pl.BlockSpec(memory_space=pl.ANY)],
        out_specs=[pl.BlockSpec((R, HD), lambda h, t: (t, M_GDN // HD + h)),
                   pl.BlockSpec((None, nb, HD, HD), lambda h, t: (h, t, 0, 0)),
                   pl.BlockSpec((None, nb, CHUNK, CHUNK), lambda h, t: (h, t, 0, 0))],
        out_shape=[jax.ShapeDtypeStruct(mixed.shape, mixed.dtype),
                   jax.ShapeDtypeStruct((HEADS, T // CHUNK, HD, HD), F32),
                   jax.ShapeDtypeStruct((HEADS, T // CHUNK, CHUNK, CHUNK), F32)],
        scratch_shapes=[pltpu.VMEM((HD, HD), F32), cs, cs, pltpu.VMEM((nb, CHUNK, CHUNK), F32), cs, cs],
        input_output_aliases={7: 0},
        compiler_params=_cp(("parallel", "arbitrary")))(qkv, qkv, qkv, bb, gb, proj, nw, mixed)


def _gdn_bwd(qkv, bb, gb, proj, nw, states, tms, dmixed, dproj):
    T = proj.shape[0]
    R = _gdn_rows(T)
    nb = R // CHUNK
    ntb = T // R

    def body(q_ref, k_ref, v_ref, bb_ref, gb_ref, z_ref, nw_ref, st_ref, tm_ref, dy_ref, _,
             dq_ref, dk_ref, dv_ref, dbb_ref, dgb_ref, dz_ref, dnw_ref,
             ds_ref, att_s, do_s, kd_s, vn_s, qd_s, w_s, dvn_s, dkd_s, dgl_s):
        hh = pl.program_id(0)
        t = pl.program_id(1)

        @pl.when(t == 0)
        def _():
            ds_ref[...] = jnp.zeros_like(ds_ref)

        @pl.when((t == 0) & (hh == 0))
        def _():
            dnw_ref[...] = jnp.zeros_like(dnw_ref)

        sh = (nb, CHUNK, HD)
        q, k, v = q_ref[...].reshape(sh), k_ref[...].reshape(sh), v_ref[...].reshape(sh)
        c = _gdn_chunk_common(q, k, v, bb_ref[...].reshape(sh), gb_ref[...].reshape(sh), tm_ref[...])
        w, u = c["wu"][:, :, :HD], c["wu"][:, :, HD:]
        sall = st_ref[...]
        vn = u - _bmm(w, sall, 2, 1)
        o = _bmm(c["qd"], sall, 2, 1) + _bmm(c["at"], vn, 2, 1)
        z = z_ref[...].reshape(sh)
        dy = dy_ref[...].reshape(sh)
        nwv = nw_ref[...].reshape(1, 1, HD)
        rs = lax.rsqrt(jnp.mean(o * o, axis=2, keepdims=True) + EPS)
        on = o * rs
        sg = _sigmoid(z)
        sz = z * sg
        dnw_ref[...] += jnp.sum(jnp.sum(dy * on * sz, axis=0), axis=0, keepdims=True)
        dz_ref[...] = (dy * on * nwv * (sg * (1.0 + z * (1.0 - sg)))).reshape(R, HD).astype(BF16)
        don = dy * nwv * sz
        do = rs * (don - on * jnp.mean(don * on, axis=2, keepdims=True))
        dqd = _bmm(do, sall, 2, 2)
        dat = jnp.where(c["tril"], _bmm(do, vn, 2, 2), 0.0)
        datt = jnp.where(c["triu"], _bmm(vn, do, 2, 2), 0.0)
        att_s[...] = c["att"]
        do_s[...] = do
        kd_s[...] = c["kd"]
        vn_s[...] = vn
        qd_s[...] = c["qd"]
        w_s[...] = w
        egl = c["egl"]
        for n in reversed(range(nb)):
            dso = ds_ref[...]
            dvn_n = _mm(att_s[n], do_s[n]) + _mm(kd_s[n], dso)
            dkd_s[n] = _mm(vn_s[n], dso, 1, 1)
            dgl = egl[n] * jnp.sum(jnp.sum(st_ref[n] * dso, axis=1, keepdims=True), axis=0, keepdims=True)
            dgl_s[n] = jnp.broadcast_to(dgl, (8, HD))
            ds_ref[...] = egl[n] * dso + _mm(qd_s[n], do_s[n], 0, 0) - _mm(w_s[n], dvn_n, 0, 0)
            dvn_s[n] = dvn_n
        dvn = dvn_s[...]
        dkd = dkd_s[...]
        dgl = dgl_s[...][:, 0:1, 0:1]
        dw = -_bmm(dvn, sall, 2, 2)
        dr = _bmm3(c["tmt"], jnp.concatenate([dw, dvn], axis=2), 2, 1)
        drw, dru = dr[:, :, :HD], dr[:, :, HD:]
        wu = c["wu"]
        dr2, wu2 = _split2(dr), _split2(wu)
        da = -jnp.where(c["stl"], _bmm3s(dr2, wu2, 2, 2), 0.0)
        da_t = -jnp.where(c["stu"], _bmm3s(wu2, dr2, 2, 2), 0.0)
        beta, gam, dec, dect, kk = c["beta"], c["gam"], c["dec"], c["dect"], c["kk"]
        bcol, brow = c["bcol"], c["brow"]
        dbeta = (jnp.sum(da * kk * dec, axis=2, keepdims=True)
                 + jnp.sum(drw * k * gam + dru * v, axis=2, keepdims=True))
        dkk = bcol * da * dec
        dkk_t = brow * da_t * dect
        e = (bcol * da * kk + dat * c["qk"]) * dec
        e_t = (brow * da_t * kk + datt * c["qkt"]) * dect
        kd = c["kd"]
        dq_ref[...] = (_bmm(dat * dec, k, 2, 1) + dqd * gam).reshape(R, HD)
        dk_ref[...] = (_bmm(datt * dect, q, 2, 1) + _bmm(dkk + dkk_t, k, 2, 1) + dkd * c["kdf"]
                       + drw * (beta * gam)).reshape(R, HD)
        dv_ref[...] = (dru * beta).reshape(R, HD)
        skd = jnp.sum(dkd * kd, axis=2, keepdims=True)
        dgc = (jnp.sum(e, axis=2, keepdims=True) - jnp.sum(e_t, axis=2, keepdims=True)
               + jnp.sum(drw * c["rw"] + dqd * c["qd"], axis=2, keepdims=True) - skd)
        tot = jnp.sum(skd, axis=1, keepdims=True) + dgl
        rowi = lax.broadcasted_iota(jnp.int32, (nb, CHUNK, 1), 1)
        dgc = dgc + jnp.where(rowi == CHUNK - 1, tot, 0.0)
        dbb_ref[...] = jnp.broadcast_to(dbeta, sh).reshape(R, HD)
        dgb_ref[...] = jnp.broadcast_to(dgc, sh).reshape(R, HD)

    def rt(t):
        return ntb - 1 - t

    def hm(p):
        return pl.BlockSpec((None, None, R, HD), lambda h, t: (p, h, rt(t), 0))

    hb = pl.BlockSpec((None, R, HD), lambda h, t: (h, rt(t), 0))
    cs = pltpu.VMEM((nb, CHUNK, HD), F32)
    ob = jax.ShapeDtypeStruct((HEADS, T, HD), F32)
    return pl.pallas_call(
        body, name="gdn_bwd", grid=(HEADS, ntb),
        in_specs=[hm(0), hm(1), hm(2), hb, hb, pl.BlockSpec((R, HD), lambda h, t: (rt(t), C_Z // HD + h)),
                  pl.BlockSpec((1, HD), lambda h, t: (0, 0)),
                  pl.BlockSpec((None, nb, HD, HD), lambda h, t: (h, rt(t), 0, 0)),
                  pl.BlockSpec((None, nb, CHUNK, CHUNK), lambda h, t: (h, rt(t), 0, 0)),
                  pl.BlockSpec((R, HD), lambda h, t: (rt(t), M_GDN // HD + h)), pl.BlockSpec(memory_space=pl.ANY)],
        out_specs=[hb, hb, hb, hb, hb, pl.BlockSpec((R, HD), lambda h, t: (rt(t), C_Z // HD + h)),
                   pl.BlockSpec((1, HD), lambda h, t: (0, 0))],
        out_shape=[ob, ob, ob, ob, ob, jax.ShapeDtypeStruct(dproj.shape, dproj.dtype),
                   jax.ShapeDtypeStruct((1, HD), F32)],
        scratch_shapes=[pltpu.VMEM((HD, HD), F32), pltpu.VMEM((nb, CHUNK, CHUNK), F32), cs, cs, cs, cs, cs, cs, cs,
                        pltpu.VMEM((nb, 8, HD), F32)],
        input_output_aliases={10: 5},
        compiler_params=_cp(("arbitrary", "arbitrary")))(qkv, qkv, qkv, bb, gb, proj, nw, states, tms, dmixed,
                                                         dproj)


def _lru_gates(xc, wa, ba, wx, bx, lam, gpos):
    xb = xc.astype(BF16)
    r = _sigmoid(_mm(xb, wa) + ba)
    i = _sigmoid(_mm(xb, wx) + bx)
    sp = _softplus(-lam)
    log_a = -LRU_C * r * sp
    a = jnp.exp(log_a)
    mult = jnp.where(gpos == 0, 1.0, jnp.sqrt(-_expm1(2.0 * log_a)))
    return r, i, sp, a, mult


def _lru_fwd(xc, proj, wa, ba, wx, bx, lam, mixed):
    T = xc.shape[0]
    tt = _tile(T)

    def body(xc_ref, gr_ref, wa_ref, ba_ref, wx_ref, bx_ref, lam_ref, _, y_ref, h_ref, carry_ref):
        t = pl.program_id(1)

        @pl.when(t == 0)
        def _():
            carry_ref[...] = jnp.zeros_like(carry_ref)

        row = lax.broadcasted_iota(jnp.int32, (tt, 1), 0)
        xcv = xc_ref[...]
        r, i, sp, a, mult = _lru_gates(xcv, wa_ref[...], ba_ref[...], wx_ref[...], bx_ref[...], lam_ref[...],
                                       t * tt + row)
        av, bv = a, mult * i * xcv
        k = 1
        while k < tt:
            a_s = jnp.where(row >= k, pltpu.roll(av, k, 0), 1.0)
            b_s = jnp.where(row >= k, pltpu.roll(bv, k, 0), 0.0)
            bv = bv + av * b_s
            av = av * a_s
            k *= 2
        h = bv + av * carry_ref[0:1, :]
        carry_ref[...] = jnp.broadcast_to(h[tt - 1:tt, :], (8, 128))
        h_ref[...] = h
        y_ref[...] = (h * _gelu(gr_ref[...])).astype(BF16)

    blk = pl.BlockSpec((tt, 128), lambda j, t: (t, j))
    vec = pl.BlockSpec((1, 128), lambda j, t: (0, j))
    mat = pl.BlockSpec((None, 128, 128), lambda j, t: (j, 0, 0))
    return pl.pallas_call(
        body, name="lru_fwd", grid=(LRU_W // 128, T // tt),
        in_specs=[blk, pl.BlockSpec((tt, 128), lambda j, t: (t, C_GR // 128 + j)), mat, vec, mat, vec, vec,
                  pl.BlockSpec(memory_space=pl.ANY)],
        out_specs=[pl.BlockSpec((tt, 128), lambda j, t: (t, M_LRU // 128 + j)), blk],
        out_shape=[jax.ShapeDtypeStruct(mixed.shape, mixed.dtype), jax.ShapeDtypeStruct((T, LRU_W), F32)],
        scratch_shapes=[pltpu.VMEM((8, 128), F32)], input_output_aliases={7: 0},
        compiler_params=_cp(("parallel", "arbitrary")))(xc, proj, wa, ba, wx, bx, lam, mixed)


def _lru_bwd(dmixed, xc, proj, hst, wa, ba, wx, bx, lam, dproj):
    T = xc.shape[0]
    tt = _tile(T)
    nt = T // tt

    def body(dy_ref, xc_ref, gr_ref, h_ref, hp_ref, wa_ref, ba_ref, wx_ref, bx_ref, lam_ref, _,
             dxc_ref, dgr_ref, dwa_ref, dwx_ref, dba_ref, dbx_ref, dlam_ref, lc_ref, ac_ref):
        t = pl.program_id(1)
        tr = nt - 1 - t

        @pl.when(t == 0)
        def _():
            lc_ref[...] = jnp.zeros_like(lc_ref)
            ac_ref[...] = jnp.zeros_like(ac_ref)
            dwa_ref[...] = jnp.zeros_like(dwa_ref)
            dwx_ref[...] = jnp.zeros_like(dwx_ref)
            dba_ref[...] = jnp.zeros_like(dba_ref)
            dbx_ref[...] = jnp.zeros_like(dbx_ref)
            dlam_ref[...] = jnp.zeros_like(dlam_ref)

        row = lax.broadcasted_iota(jnp.int32, (tt, 1), 0)
        gpos = tr * tt + row
        xcv = xc_ref[...]
        wav, wxv, lamv = wa_ref[...], wx_ref[...], lam_ref[...]
        r, i, sp, a, mult = _lru_gates(xcv, wav, ba_ref[...], wxv, bx_ref[...], lamv, gpos)
        h = h_ref[...]
        dy = dy_ref[...]
        gg, dgg = _gelu_and_grad(gr_ref[...])
        dgr_ref[...] = (dy * h * dgg).astype(BF16)
        bv = dy * gg
        cv = jnp.where(row < tt - 1, pltpu.roll(a, tt - 1, 0), ac_ref[0:1, :])
        k = 1
        while k < tt:
            c_s = jnp.where(row < tt - k, pltpu.roll(cv, tt - k, 0), 1.0)
            b_s = jnp.where(row < tt - k, pltpu.roll(bv, tt - k, 0), 0.0)
            bv = bv + cv * b_s
            cv = cv * c_s
            k *= 2
        lm = bv + cv * lc_ref[0:1, :]
        lc_ref[...] = jnp.broadcast_to(lm[0:1, :], (8, 128))
        ac_ref[...] = jnp.broadcast_to(a[0:1, :], (8, 128))
        hp = jnp.where(tr == 0, 0.0, hp_ref[...])
        hs = pltpu.roll(jnp.concatenate([hp, h], axis=0), 1, 0)[8:, :]
        da = lm * hs
        dmult = lm * i * xcv
        di = lm * mult * xcv
        dxc = lm * mult * i
        dlog_a = a * da - jnp.where(gpos == 0, 0.0, dmult * a * a / mult)
        dr = dlog_a * (-LRU_C * sp)
        dsp = jnp.sum(dlog_a * (-LRU_C * r), axis=0, keepdims=True)
        dlam_ref[...] += dsp * (-_sigmoid(-lamv))
        dpr = dr * r * (1.0 - r)
        dpi = di * i * (1.0 - i)
        dba_ref[...] += jnp.sum(dpr, axis=0, keepdims=True)
        dbx_ref[...] += jnp.sum(dpi, axis=0, keepdims=True)
        dwa_ref[...] += _mm(xcv, dpr, 0, 0)
        dwx_ref[...] += _mm(xcv, dpi, 0, 0)
        dxc_ref[...] = dxc + _mm(dpr, wav, 1, 1) + _mm(dpi, wxv, 1, 1)

    def rt(t):
        return nt - 1 - t

    blk = pl.BlockSpec((tt, 128), lambda j, t: (rt(t), j))
    vec = pl.BlockSpec((1, 128), lambda j, t: (0, j))
    mat = pl.BlockSpec((None, 128, 128), lambda j, t: (j, 0, 0))
    h8 = tt // 8
    mshape = jax.ShapeDtypeStruct((LRU_W // 128, 128, 128), F32)
    vshape = jax.ShapeDtypeStruct((1, LRU_W), F32)
    return pl.pallas_call(
        body, name="lru_bwd", grid=(LRU_W // 128, nt),
        in_specs=[pl.BlockSpec((tt, 128), lambda j, t: (rt(t), M_LRU // 128 + j)), blk,
                  pl.BlockSpec((tt, 128), lambda j, t: (rt(t), C_GR // 128 + j)), blk,
                  pl.BlockSpec((8, 128), lambda j, t: (jnp.maximum(rt(t) * h8 - 1, 0), j)),
                  mat, vec, mat, vec, vec, pl.BlockSpec(memory_space=pl.ANY)],
        out_specs=[blk, pl.BlockSpec((tt, 128), lambda j, t: (rt(t), C_GR // 128 + j)), mat, mat, vec, vec, vec],
        out_shape=[jax.ShapeDtypeStruct((T, LRU_W), F32), jax.ShapeDtypeStruct(dproj.shape, dproj.dtype),
                   mshape, mshape, vshape, vshape, vshape],
        scratch_shapes=[pltpu.VMEM((8, 128), F32), pltpu.VMEM((8, 128), F32)], input_output_aliases={10: 1},
        compiler_params=_cp(("parallel", "arbitrary")))(dmixed, xc, proj, hst, hst, wa, ba, wx, bx, lam, dproj)


FFN_CB = 512
FFN_K = 3


def _ffn_conv(ge, w_ref):
    acc = ge * w_ref[FFN_K - 1:FFN_K, :]
    for j in range(FFN_K - 1):
        acc = acc + pltpu.roll(ge, FFN_K - 1 - j, 0) * w_ref[j:j + 1, :]
    return acc


def _ffn_gate_fwd(up, w):
    T = up.shape[0]
    tt = _tile(T)
    cb = FFN_CB
    nc = D_FF // cb
    h8 = tt // 8

    def body(g_ref, gp_ref, v_ref, w_ref, o_ref):
        t = pl.program_id(1)
        prev = jnp.where(t == 0, 0.0, gp_ref[...])
        gc = _ffn_conv(jnp.concatenate([prev, g_ref[...]], axis=0), w_ref)[8:, :]
        o_ref[...] = (_gelu(gc) * v_ref[...]).astype(BF16)

    return pl.pallas_call(
        body, name="ffn_gate_fwd", grid=(nc, T // tt),
        in_specs=[pl.BlockSpec((tt, cb), lambda j, t: (t, j)),
                  pl.BlockSpec((8, cb), lambda j, t: (jnp.maximum(t * h8 - 1, 0), j)),
                  pl.BlockSpec((tt, cb), lambda j, t: (t, nc + j)),
                  pl.BlockSpec((FFN_K, cb), lambda j, t: (0, j))],
        out_specs=pl.BlockSpec((tt, cb), lambda j, t: (t, j)),
        out_shape=jax.ShapeDtypeStruct((T, D_FF), BF16),
        compiler_params=_cp(("parallel", "parallel")))(up, up, up, w)


def _ffn_gate_bwd(dact, up, w):
    T = up.shape[0]
    tt = _tile(T)
    cb = FFN_CB
    nc = D_FF // cb
    nt = T // tt
    h8 = tt // 8

    def body(d_ref, dn_ref, g_ref, gp_ref, gn_ref, v_ref, vn_ref, w_ref, dup_ref, dw_ref):
        t = pl.program_id(1)
        prev = jnp.where(t == 0, 0.0, gp_ref[...])
        ge = jnp.concatenate([prev, g_ref[...], gn_ref[...]], axis=0)
        gc = _ffn_conv(ge, w_ref)[8:, :]
        gg, dgg = _gelu_and_grad(gc)
        de = jnp.concatenate([d_ref[...], jnp.where(t == nt - 1, 0.0, dn_ref[...])], axis=0)
        ve = jnp.concatenate([v_ref[...], vn_ref[...]], axis=0)
        dup_ref[1] = (de * gg)[:tt, :].astype(BF16)
        dgc = de * ve * dgg
        n = tt + 8
        acc = dgc * w_ref[FFN_K - 1:FFN_K, :]
        for j in range(FFN_K - 1):
            acc = acc + pltpu.roll(dgc, n - (FFN_K - 1 - j), 0) * w_ref[j:j + 1, :]
        dup_ref[0] = acc[:tt, :].astype(BF16)

        @pl.when(t == 0)
        def _():
            dw_ref[...] = jnp.zeros_like(dw_ref)

        dgm = dgc[:tt, :]
        for j in range(FFN_K):
            sh = FFN_K - 1 - j
            xs = ge[8:8 + tt, :] if sh == 0 else pltpu.roll(ge, sh, 0)[8:8 + tt, :]
            dw_ref[j:j + 1, :] += jnp.sum(dgm * xs, axis=0, keepdims=True)

    def nxt(t):
        return jnp.minimum((t + 1) * h8, T // 8 - 1)

    return pl.pallas_call(
        body, name="ffn_gate_bwd", grid=(nc, nt),
        in_specs=[pl.BlockSpec((tt, cb), lambda j, t: (t, j)),
                  pl.BlockSpec((8, cb), lambda j, t: (nxt(t), j)),
                  pl.BlockSpec((tt, cb), lambda j, t: (t, j)),
                  pl.BlockSpec((8, cb), lambda j, t: (jnp.maximum(t * h8 - 1, 0), j)),
                  pl.BlockSpec((8, cb), lambda j, t: (nxt(t), j)),
                  pl.BlockSpec((tt, cb), lambda j, t: (t, nc + j)),
                  pl.BlockSpec((8, cb), lambda j, t: (nxt(t), nc + j)),
                  pl.BlockSpec((FFN_K, cb), lambda j, t: (0, j))],
        out_specs=[pl.BlockSpec((2, tt, cb), lambda j, t: (0, t, j)),
                   pl.BlockSpec((FFN_K, cb), lambda j, t: (0, j))],
        out_shape=[jax.ShapeDtypeStruct((2, T, D_FF), BF16), jax.ShapeDtypeStruct((FFN_K, D_FF), F32)],
        compiler_params=_cp(("parallel", "arbitrary")))(dact, dact, up, up, up, up, up, w)


def _row_tile(rows, cap):
    best = 8
    for r in range(8, min(rows, cap) + 1, 8):
        if rows % r == 0:
            best = r
    return best


def _adamw(parts, w, m, v, rt, name, layer=None, prev=None):
    P, R, C = parts.shape

    def body(p_ref, w_ref, m_ref, v_ref, *rest):
        g_ref, d_ref, mo_ref, vo_ref = rest[-4:]
        g = p_ref[0].astype(F32)
        for i in range(1, P):
            g = g + p_ref[i].astype(F32)
        wv = w_ref[...]
        mn = ADAM_B1 * m_ref[...] + (1.0 - ADAM_B1) * g
        vn = ADAM_B2 * v_ref[...] + (1.0 - ADAM_B2) * (g * g)
        m_hat = mn / (1.0 - ADAM_B1 ** ADAM_STEP)
        v_hat = vn / (1.0 - ADAM_B2 ** ADAM_STEP)
        g_ref[...] = g
        d_ref[...] = -ADAM_LR * (m_hat / (jnp.sqrt(v_hat) + ADAM_EPS) + ADAM_WD * wv)
        mo_ref[...] = mn
        vo_ref[...] = vn

    if layer is None:
        blk = pl.BlockSpec((rt, C), lambda r: (r, 0))
        sh = jax.ShapeDtypeStruct((R, C), F32)
    else:
        blk = pl.BlockSpec((None, rt, C), lambda r: (layer, r, 0))
        sh = jax.ShapeDtypeStruct(w.shape, F32)
    extra = list(prev) if prev is not None else []
    return pl.pallas_call(
        body, name=name, grid=(R // rt,),
        in_specs=[pl.BlockSpec((P, rt, C), lambda r: (0, r, 0)), blk, blk, blk]
        + [pl.BlockSpec(memory_space=pl.ANY)] * len(extra),
        out_specs=[blk, blk, blk, blk], out_shape=[sh, sh, sh, sh],
        input_output_aliases={4 + i: i for i in range(len(extra))},
        compiler_params=_cp(("parallel",)))(parts, w, m, v, *extra)


def _peer(k):
    x, y, c = lax.axis_index("x"), lax.axis_index("y"), lax.axis_index("c")
    px = 1 - x if k & 4 else x
    py = 1 - y if k & 2 else y
    pc = 1 - c if k & 1 else c
    return (px, py, pc), 4 * px + 2 * py + pc


def _all_gather(x, name):
    R, C = x.shape

    def body(x_ref, o_ref, send_sems, recv_sems, local_sem):
        me = 4 * lax.axis_index("x") + 2 * lax.axis_index("y") + lax.axis_index("c")
        mine = pltpu.make_async_copy(x_ref, o_ref.at[me], local_sem)
        mine.start()
        sends = []
        for k in range(1, N_DEV):
            dev, _ = _peer(k)
            cp = pltpu.make_async_remote_copy(src_ref=x_ref, dst_ref=o_ref.at[me], send_sem=send_sems.at[k - 1],
                                              recv_sem=recv_sems.at[k - 1], device_id=dev, device_id_type=MESH_IDS)
            cp.start()
            sends.append(cp)
        for k in range(1, N_DEV):
            dev, idx = _peer(k)
            pltpu.make_async_remote_copy(src_ref=x_ref, dst_ref=o_ref.at[idx], send_sem=send_sems.at[k - 1],
                                         recv_sem=recv_sems.at[k - 1], device_id=dev,
                                         device_id_type=MESH_IDS).wait_recv()
        for cp in sends:
            cp.wait_send()
        mine.wait()

    return pl.pallas_call(
        body, name=name, in_specs=[pl.BlockSpec(memory_space=pl.ANY)], out_specs=pl.BlockSpec(memory_space=pl.ANY),
        out_shape=jax.ShapeDtypeStruct((N_DEV, R, C), x.dtype),
        scratch_shapes=[pltpu.SemaphoreType.DMA((N_DEV - 1,)), pltpu.SemaphoreType.DMA((N_DEV - 1,)),
                        pltpu.SemaphoreType.DMA],
        compiler_params=pltpu.CompilerParams(has_side_effects=True))(x)


HBM_SPEC = pl.BlockSpec(memory_space=pltpu.HBM)
SEM_SPEC = pl.BlockSpec(memory_space=pltpu.SEMAPHORE)
EFFECT = pltpu.SideEffectType.DATAFLOW_SIDE_EFFECTING
OTHER_CHIPS = ((1, 0), (0, 1), (1, 1))


def _split_start(bufs, plan, n, name):
    nb = len(bufs)

    def body(*refs):
        send_sems, recv_sems, token = refs[nb], refs[nb + 1], refs[2 * nb + 2]
        for i, (src, dst, _, dev) in enumerate(plan(refs[:nb])):
            pltpu.make_async_remote_copy(src_ref=src, dst_ref=dst, send_sem=send_sems.at[i],
                                         recv_sem=recv_sems.at[i], device_id=dev, device_id_type=MESH_IDS).start()
        token[...] = jnp.zeros_like(token)

    outs = pl.pallas_call(
        body, name=name,
        out_shape=(pltpu.SemaphoreType.DMA((n,)), pltpu.SemaphoreType.DMA((n,)),
                   *[pltpu.HBM(b.shape, b.dtype) for b in bufs], jax.ShapeDtypeStruct((8, 128), F32)),
        in_specs=[HBM_SPEC] * nb,
        out_specs=(SEM_SPEC, SEM_SPEC, *[HBM_SPEC] * nb, pl.BlockSpec(memory_space=pltpu.VMEM)),
        input_output_aliases={i: 2 + i for i in range(nb)},
        compiler_params=pltpu.CompilerParams(has_side_effects=EFFECT),
    )(*[pltpu.with_memory_space_constraint(b, pltpu.HBM) for b in bufs])
    return dict(send=outs[0], recv=outs[1], bufs=list(outs[2:2 + nb]), token=outs[2 + nb], plan=plan, n=n)


def _split_wait(st, after, name):
    bufs = st["bufs"]
    nb = len(bufs)
    plan = st["plan"]
    afters = list(after) if isinstance(after, (list, tuple)) else [after]

    def body(*refs):
        send_sems, recv_sems = refs[nb], refs[nb + 1]
        for i, (src, dst, land, dev) in enumerate(plan(refs[:nb])):
            pltpu.make_async_remote_copy(src_ref=src, dst_ref=dst, send_sem=send_sems.at[i],
                                         recv_sem=recv_sems.at[i], device_id=dev,
                                         device_id_type=MESH_IDS).wait_send()
            pltpu.make_async_remote_copy(src_ref=src, dst_ref=land, send_sem=send_sems.at[i],
                                         recv_sem=recv_sems.at[i], device_id=dev,
                                         device_id_type=MESH_IDS).wait_recv()

    outs = pl.pallas_call(
        body, name=name, out_shape=tuple(pltpu.HBM(b.shape, b.dtype) for b in bufs),
        in_specs=[HBM_SPEC] * nb + [SEM_SPEC, SEM_SPEC] + [pl.BlockSpec(memory_space=pl.ANY)] * len(afters),
        out_specs=tuple([HBM_SPEC] * nb), input_output_aliases={i: i for i in range(nb)},
        compiler_params=pltpu.CompilerParams(has_side_effects=EFFECT),
    )(*bufs, st["send"], st["recv"], *afters)
    return list(outs)


def _xyc():
    return lax.axis_index("x"), lax.axis_index("y"), lax.axis_index("c")


def _flip(x, y, a, b):
    return (1 - x if a else x), (1 - y if b else y)


def _ag1_plan(outs):
    x, y, c = _xyc()
    me = 4 * x + 2 * y + c
    copies = []
    for o in outs:
        copies.append((o.at[me], o.at[me], o.at[4 * x + 2 * y + 1 - c], (x, y, 1 - c)))
        for a, b in OTHER_CHIPS:
            px, py = _flip(x, y, a, b)
            copies.append((o.at[me], o.at[me], o.at[4 * px + 2 * py + c], (px, py, c)))
    return copies


def _ag2_plan(outs):
    x, y, c = _xyc()
    copies = []
    for o in outs:
        for a, b in OTHER_CHIPS:
            px, py = _flip(x, y, a, b)
            mine, sibs = 4 * px + 2 * py + c, 4 * px + 2 * py + 1 - c
            copies.append((o.at[mine], o.at[mine], o.at[sibs], (x, y, 1 - c)))
    return copies


def _rs1_plan(refs):
    x, y, c = _xyc()
    copies = []
    for g, land in zip(refs[0::2], refs[1::2]):
        for j in range(4):
            copies.append((g.at[2 * j + 1 - c], land.at[j], land.at[j], (x, y, 1 - c)))
    return copies


def _rs2_plan(refs):
    x, y, c = _xyc()
    mychip = 2 * x + y
    copies = []
    for s, land in zip(refs[0::2], refs[1::2]):
        for a, b in OTHER_CHIPS:
            px, py = _flip(x, y, a, b)
            copies.append((s.at[2 * px + py], land.at[mychip], land.at[2 * px + py], (px, py, c)))
    return copies


def _landing_like(g, name):
    def body(g_ref, o_ref):
        del g_ref, o_ref

    anyspec = pl.BlockSpec(memory_space=pl.ANY)
    return pl.pallas_call(body, name=name, in_specs=[anyspec], out_specs=anyspec,
                          out_shape=jax.ShapeDtypeStruct((4,) + g.shape[1:], g.dtype))(g)


def _place(x, slots, by_chip, name):
    R, C = x.shape[-2:]
    rt = _row_tile(R, 512)
    xi, yi, ci = _xyc()
    idx = (2 * xi + yi if by_chip else 4 * xi + 2 * yi + ci).astype(jnp.int32).reshape(1)

    def body(i_ref, x_ref, o_ref):
        o_ref[...] = x_ref[...]

    if by_chip:
        in_spec = pl.BlockSpec((None, rt, C), lambda r, i: (i[0], r, 0))
    else:
        in_spec = pl.BlockSpec((rt, C), lambda r, i: (r, 0))
    grid_spec = pltpu.PrefetchScalarGridSpec(
        num_scalar_prefetch=1, grid=(R // rt,), in_specs=[in_spec],
        out_specs=pl.BlockSpec((None, rt, C), lambda r, i: (i[0], r, 0)))
    return pl.pallas_call(body, name=name, grid_spec=grid_spec,
                          out_shape=jax.ShapeDtypeStruct((slots, R, C), x.dtype),
                          compiler_params=_cp(("parallel",)))(idx, x)


def _pair_sum(g, land, cidx, name):
    _, R, C = land.shape
    rt = _row_tile(R, 512)
    g4 = g.reshape(4, 2, R, C)

    def body(c_ref, g_ref, l_ref, o_ref):
        o_ref[...] = (g_ref[...].astype(F32) + l_ref[...].astype(F32)).astype(o_ref.dtype)

    grid_spec = pltpu.PrefetchScalarGridSpec(
        num_scalar_prefetch=1, grid=(4, R // rt),
        in_specs=[pl.BlockSpec((None, None, rt, C), lambda j, r, c_ref: (j, c_ref[0], r, 0)),
                  pl.BlockSpec((None, rt, C), lambda j, r, c_ref: (j, r, 0))],
        out_specs=pl.BlockSpec((None, rt, C), lambda j, r, c_ref: (j, r, 0)))
    return pl.pallas_call(body, name=name, grid_spec=grid_spec, out_shape=jax.ShapeDtypeStruct(land.shape, land.dtype),
                          compiler_params=_cp(("parallel", "parallel")))(cidx, g4, land)


def _no_hook(event, l, after, payload=None):
    return None


def _tie(x, token):
    if token is None:
        return x

    def body(x_ref, t_ref, o_ref):
        del x_ref, t_ref, o_ref

    anyspec = pl.BlockSpec(memory_space=pl.ANY)
    return pl.pallas_call(body, name="tie", in_specs=[anyspec, anyspec], out_specs=anyspec,
                          out_shape=jax.ShapeDtypeStruct(x.shape, x.dtype), input_output_aliases={0: 0})(x, token)


def _layer_fwd(x, W, l, hook=_no_hook):
    T = x.shape[0]
    n = f"l{l}_"
    h1, h1t = _norm_fwd(x, W["norm1"], n + "norm1_fwd")
    proj = _mm_nn(h1, W["win"], F32, n + "mm_in", tn_c=(768,))
    y_pool = _pool_fwd(proj, W["pool_w"], W["pool_b"], W["pool_s"])
    cpre = _conv_fwd(proj, C_QKV, 3 * GDN_W, W["gconv_w"], None, 256, n + "gdn_conv_fwd")
    qkv, bb, gb = _gdn_pre_fwd(cpre, proj, W["alog"], W["dtb"])
    mixed, states, tms = _gdn_fwd(qkv, bb, gb, proj, W["gnorm"], y_pool)
    lconv_w = _tie(W["lconv_w"], hook("f_mix", l, states))
    xc = _conv_fwd(proj, C_XR, LRU_W, lconv_w, W["lconv_b"], 128, n + "lru_conv_fwd")
    mixed, hst = _lru_fwd(xc, proj, W["wa"], W["ba"], W["wx"], W["bx"], W["lam"], mixed)
    hook("f_out", l, hst)
    x1 = _mm_nn(mixed, W["wout"], F32, n + "mm_out", add=x)
    h2, h2t = _norm_fwd(x1, W["norm2"], n + "norm2_fwd")
    up = _mm_up(h2, W["wup"], n + "mm_up")
    act = _ffn_gate_fwd(up, W["fconv_w"])
    act = _tie(act, hook("f_act", l, act))
    x2 = _mm_nn(act, W["wdown"], F32, n + "mm_down", add=x1)
    hook("f_end", l, x2)
    saved = dict(x=x, h1t=h1t, proj=proj, cpre=cpre, qkv=qkv, bb=bb, gb=gb, states=states, tms=tms, xc=xc, hst=hst,
                 mixed=mixed, x1=x1, h2t=h2t, up=up, act=act)
    return x2, saved


def _layer_bwd(dx2, W, S, l, hook=_no_hook):
    T = dx2.shape[0]
    n = f"l{l}_"
    dact = _mm_nt(dx2, W["wdown"], F32, n + "mm_down_dx", tk_c=(2048,))
    g_wdown = _mm_tn(S["act"], dx2, BF16, n + "mm_down_dw")
    dup, g_fconv = _ffn_gate_bwd(dact, S["up"], W["fconv_w"])
    ns = W["wup"].shape[2]
    dh2 = _mm_up_t(dup, W["wup"], n + "mm_up_dx")
    g_wup = _mm_dup(S["h2t"], dup, ns, n + "mm_up_dw")
    tok = hook("b_ffn", l, g_wup, dict(ffn_down=g_wdown, ffn_up=g_wup))
    dx1, g_norm2 = _norm_bwd(S["x1"], _tie(W["norm2"], tok), dh2, dx2, n + "norm2_bwd")
    dmixed = _mm_nt(dx1, W["wout"], F32, n + "mm_out_dx", tk_c=(2048,))
    g_wout = _mm_tn(S["mixed"], dx1, BF16, n + "mm_out_dw")
    tok = hook("b_mid", l, g_wout)
    proj = S["proj"]
    dproj, g_pool_w, g_pool_b, g_pool_s = _pool_bwd(dmixed, proj, W["pool_w"], W["pool_b"], _tie(W["pool_s"], tok))
    dq, dk, dv, dbb, dgb, dproj, g_gnorm = _gdn_bwd(S["qkv"], S["bb"], S["gb"], proj, W["gnorm"], S["states"],
                                                    S["tms"], dmixed, dproj)
    dc, dproj, g_alog, g_dtb = _gdn_pre_bwd(dq, dk, dv, S["cpre"], dbb, dgb, proj, W["alog"], W["dtb"], dproj)
    dproj, g_gconv = _conv_bwd(dc, proj, C_QKV, W["gconv_w"], 256, n + "gdn_conv_bwd", dproj)
    dxc, dproj, g_wa, g_wx, g_ba, g_bx, g_lam = _lru_bwd(dmixed, S["xc"], proj, S["hst"], W["wa"], W["ba"], W["wx"],
                                                          W["bx"], W["lam"], dproj)
    dproj, g_lconv, g_lconv_b = _conv_bwd(dxc, proj, C_XR, W["lconv_w"], 128, n + "lru_conv_bwd", dproj,
                                          want_db=True)
    dh1 = _mm_nt(dproj, W["win"], F32, n + "mm_in_dx", tk_c=(1792,))
    g_win = _mm_nn(S["h1t"], dproj, BF16, n + "mm_in_dw", tn_c=(768,))
    tok = hook("b_in", l, g_win, dict(w_out=g_wout, w_in=g_win))
    dx, g_norm1 = _norm_bwd(S["x"], _tie(W["norm1"], tok), dh1, dx1, n + "norm1_bwd")
    big = dict(w_in=g_win, w_out=g_wout, ffn_up=g_wup, ffn_down=g_wdown)
    small = dict(norm1_w=g_norm1[0], pool_w=g_pool_w, pool_b=g_pool_b.reshape(4, 128), pool_scale=g_pool_s[0],
                 gdn_conv_w=g_gconv, gdn_a_log=g_alog[0, :HEADS], gdn_dt_bias=g_dtb[0, :HEADS],
                 gdn_norm_w=g_gnorm[0], lru_conv_w=g_lconv, lru_conv_b=g_lconv_b[0], lru_wa=g_wa, lru_ba=g_ba[0],
                 lru_wx=g_wx, lru_bx=g_bx[0], lru_lambda=g_lam[0], norm2_w=g_norm2[0], ffn_conv_w=g_fconv)
    dx = _tie(dx, hook("b_end", l, dx, small))
    return dx, big, small


def _pad_lane(v):
    return jnp.pad(v, (0, 128 - v.shape[0])).reshape(1, 128)


def _layer_weights(l, big, P, conv_full):
    return dict(
        win=big.get("w_in"), wout=big.get("w_out"), wup=big.get("ffn_up"), wdown=big.get("ffn_down"),
        norm1=P["norm1_w"][l].reshape(1, D_MODEL), norm2=P["norm2_w"][l].reshape(1, D_MODEL),
        pool_w=P["pool_w"][l], pool_b=P["pool_b"][l].reshape(1, POOL_W), pool_s=P["pool_scale"][l].reshape(1, POOL_W),
        gconv_w=conv_full["gdn_conv_w"][l], alog=_pad_lane(P["gdn_a_log"][l]), dtb=_pad_lane(P["gdn_dt_bias"][l]),
        gnorm=P["gdn_norm_w"][l].reshape(1, HD),
        lconv_w=conv_full["lru_conv_w"][l], lconv_b=P["lru_conv_b"][l].reshape(1, LRU_W),
        wa=P["lru_wa"][l], ba=P["lru_ba"][l].reshape(1, LRU_W), wx=P["lru_wx"][l],
        bx=P["lru_bx"][l].reshape(1, LRU_W), lam=P["lru_lambda"][l].reshape(1, LRU_W),
        fconv_w=conv_full["ffn_conv_w"][l])


def _local_step(x, target, Ws, final_norm_w, hook=_no_hook):
    saved = []
    for l in range(DEPTH):
        x, s = _layer_fwd(x, Ws[l], l, hook)
        saved.append(s)
    loss, dx, g_final = _loss_head(x, final_norm_w.reshape(1, D_MODEL), target)
    bigs, smalls = [None] * DEPTH, [None] * DEPTH
    for l in reversed(range(DEPTH)):
        dx, bigs[l], smalls[l] = _layer_bwd(dx, Ws[l], saved[l], l, hook)
    return loss, dx, g_final[0], bigs, smalls


SMALL_REPL = ("norm1_w", "pool_w", "pool_b", "pool_scale", "gdn_a_log", "gdn_dt_bias", "gdn_norm_w", "lru_conv_b",
              "lru_wa", "lru_ba", "lru_wx", "lru_bx", "lru_lambda", "norm2_w", "final_norm_w")
SMALL_SHARD = ("gdn_conv_w", "lru_conv_w", "ffn_conv_w")
BIG = ("w_in", "w_out", "ffn_up", "ffn_down")
WEIGHTS = ("norm1_w", "w_in", "pool_w", "pool_b", "pool_scale", "gdn_conv_w", "gdn_a_log", "gdn_dt_bias",
           "gdn_norm_w", "lru_conv_w", "lru_conv_b", "lru_wa", "lru_ba", "lru_wx", "lru_bx", "lru_lambda", "w_out",
           "norm2_w", "ffn_up", "ffn_conv_w", "ffn_down", "final_norm_w")
SEG = 1024


def _pack(arrs):
    pieces, table, off = [], [], 0
    for a in arrs:
        n = a.size
        npad = -(-n // SEG) * SEG
        pieces.append(jnp.pad(a.reshape(-1).astype(F32), (0, npad - n)))
        table.append((off, n, a.shape))
        off += npad
    return jnp.concatenate(pieces).reshape(off // 128, 128), table


def _unpack(buf, table):
    flat = buf.reshape(-1)
    return [flat[off:off + n].reshape(shape) for off, n, shape in table]


def _pad_in(w):
    z1 = jnp.zeros(w.shape[:-1] + (C_XR - AB_ORIG_END,), w.dtype)
    return jnp.concatenate([w[..., :AB_ORIG_END], z1, w[..., AB_ORIG_END:]], axis=-1)


def _unpad_in(w):
    return jnp.concatenate([w[..., :AB_ORIG_END], w[..., C_XR:C_GR + LRU_W]], axis=-1)


def kernel(x, norm1_w, w_in, pool_w, pool_b, pool_scale, gdn_conv_w, gdn_a_log, gdn_dt_bias, gdn_norm_w, lru_conv_w, lru_conv_b, lru_wa, lru_ba, lru_wx, lru_bx, lru_lambda, w_out, norm2_w, ffn_up, ffn_conv_w, ffn_down, final_norm_w, loss_target, m_norm1_w, m_w_in, m_pool_w, m_pool_b, m_pool_scale, m_gdn_conv_w, m_gdn_a_log, m_gdn_dt_bias, m_gdn_norm_w, m_lru_conv_w, m_lru_conv_b, m_lru_wa, m_lru_ba, m_lru_wx, m_lru_bx, m_lru_lambda, m_w_out, m_norm2_w, m_ffn_up, m_ffn_conv_w, m_ffn_down, m_final_norm_w, v_norm1_w, v_w_in, v_pool_w, v_pool_b, v_pool_scale, v_gdn_conv_w, v_gdn_a_log, v_gdn_dt_bias, v_gdn_norm_w, v_lru_conv_w, v_lru_conv_b, v_lru_wa, v_lru_ba, v_lru_wx, v_lru_bx, v_lru_lambda, v_w_out, v_norm2_w, v_ffn_up, v_ffn_conv_w, v_ffn_down, v_final_norm_w):
    loc = dict(locals())
    Wp = {n: loc[n] for n in WEIGHTS}
    Mp = {n: loc["m_" + n] for n in WEIGHTS}
    Vp = {n: loc["v_" + n] for n in WEIGHTS}
    xi, yi, ci = _xyc()
    me = 4 * xi + 2 * yi + ci
    mychip = 2 * xi + yi
    cidx = ci.astype(jnp.int32).reshape(1)
    keys = dict(w_in="win", w_out="wout", ffn_up="wup", ffn_down="wdown")

    def shard2d(d, name, l):
        a = d[name][l]
        return _pad_in(a) if name == "w_in" else a

    def wshard(l, name):
        return shard2d(Wp, name, l).astype(BF16)

    def full2d(name, full):
        return full if name == "ffn_up" else full.reshape(-1, full.shape[2])

    def ag_start(shards, tag, token=None):
        if token is not None:
            shards = [_tie(shards[0], token)] + list(shards[1:])
        bufs = [_place(s, N_DEV, False, f"place_{tag}{i}") for i, s in enumerate(shards)]
        return _split_start(bufs, _ag1_plan, 4 * len(bufs), f"ag1s_{tag}")

    def ag_mid(st, after, tag):
        bufs = _split_wait(st, after, f"ag1w_{tag}")
        return _split_start(bufs, _ag2_plan, 3 * len(bufs), f"ag2s_{tag}")

    def ag_end(st, after, tag):
        return _split_wait(st, after, f"ag2w_{tag}")

    def rs_start(gs, tag):
        bufs = []
        for nm, g in gs.items():
            if nm != "ffn_up":
                g = g.reshape(N_DEV, g.shape[0] // N_DEV, g.shape[1])
            bufs += [g, _landing_like(g, f"land_{nm}_{tag}")]
        st = _split_start(bufs, _rs1_plan, 4 * len(gs), f"rs1s_{tag}")
        st["names"] = list(gs)
        return st

    def rs_mid(st, after, tag):
        bufs = _split_wait(st, after, f"rs1w_{tag}")
        out = []
        for i, nm in enumerate(st["names"]):
            s = _pair_sum(bufs[2 * i], bufs[2 * i + 1], cidx, f"pairsum_{nm}_{tag}")
            out += [s, _place(s, 4, True, f"place_{nm}_{tag}")]
        st2 = _split_start(out, _rs2_plan, 3 * len(st["names"]), f"rs2s_{tag}")
        st2["names"] = st["names"]
        return st2

    def rs_end(st, after, tag):
        bufs = _split_wait(st, after, f"rs2w_{tag}")
        return dict(zip(st["names"], bufs[1::2]))

    lnames = tuple(n for n in SMALL_REPL if n != "final_norm_w") + SMALL_SHARD

    def small_pack(l, gs, extra):
        return _pack([gs[nm] for nm in lnames] + extra)

    def small_state(d, l, gs):
        arrs = [d[nm][l] if nm in SMALL_REPL else jnp.zeros(gs[nm].shape, F32) for nm in lnames]
        if l == 0:
            arrs += [d["final_norm_w"], jnp.zeros((1,), F32)]
        return _pack(arrs)[0]

    stA = ag_start([wshard(0, "w_in")], "a")
    stA2 = ag_mid(stA, stA["token"], "a")
    stB = ag_start([wshard(0, n) for n in BIG[1:]], "b", stA2["token"])
    (w_in0,) = ag_end(stA2, stB["token"], "a")

    cbuf, ctable = _pack([Wp[n] for n in SMALL_SHARD])
    call = _all_gather(cbuf, "ag_conv_w")
    parts = [_unpack(call[i], ctable) for i in range(N_DEV)]
    conv_full = {n: jnp.concatenate([parts[i][j] for i in range(N_DEV)], axis=-1) for j, n in enumerate(SMALL_SHARD)}

    Ws = [_layer_weights(l, {}, Wp, conv_full) for l in range(DEPTH)]
    Ws[0]["win"] = full2d("w_in", w_in0)
    st = {}

    def hook(event, l, after, payload=None):
        if event == "f_mix" and l == 0:
            st["b2"] = ag_mid(stB, after, "b")
            st["c"] = ag_start([wshard(1, n) for n in BIG], "c", st["b2"]["token"])
            return st["c"]["token"]
        if event == "f_out" and l == 0:
            for n, b in zip(BIG[1:], ag_end(st["b2"], after, "b")):
                Ws[0][keys[n]] = full2d(n, b)
        if event == "f_act" and l == 0:
            st["c2"] = ag_mid(st["c"], after, "c")
            return st["c2"]["token"]
        if event == "f_end" and l == 0:
            for n, b in zip(BIG, ag_end(st["c2"], after, "c")):
                Ws[1][keys[n]] = full2d(n, b)
        if event == "b_ffn":
            st["ffn", l] = rs_start(payload, f"ffn{l}")
            return st["ffn", l]["token"]
        if event == "b_mid":
            st["ffn2", l] = rs_mid(st["ffn", l], after, f"ffn{l}")
            if l == 0:
                st["sm1b"] = ag_mid(st["sm1"], st["ffn2", l]["token"], "sm1")
                return st["sm1b"]["token"]
            return st["ffn2", l]["token"]
        if event == "b_in":
            st["io", l] = rs_start(payload, f"io{l}")
            if l == 0:
                st["sm1g"] = ag_end(st["sm1b"], st["io", l]["token"], "sm1")[0]
            return st["io", l]["token"]
        if event == "b_end" and l == 1:
            st["io2", 1] = rs_mid(st["io", 1], after, "io1")
            gbuf1, st["table1"] = small_pack(1, payload, [])
            st["sm1"] = ag_start([gbuf1], "sm1", st["io2", 1]["token"])
            return st["sm1"]["token"]
        return None

    loss, dx, g_final, _, gsmall = _local_step(x[0], loss_target[0], Ws, final_norm_w, hook)

    out_g, out_d, out_m, out_v = {}, {}, {}, {}
    outs4 = (out_g, out_d, out_m, out_v)
    rts = dict(w_in=64, w_out=128, ffn_up=256, ffn_down=128)
    big_res = {}

    def stacked(d, name):
        return _pad_in(d[name]) if name == "w_in" else d[name]

    wmv = {name: [stacked(d, name) for d in (Wp, Mp, Vp)] for name in BIG}

    def adam_big(l, parts):
        for name, p in parts.items():
            big_res[name] = _adamw(p, *wmv[name], rts[name], f"adamw_{name}_{l}", layer=l, prev=big_res.get(name))
        return [big_res[name][0] for name in parts]

    def adam_small(l, gall, gs):
        rs = gall.shape[1]
        return _adamw(gall, small_state(Wp, l, gs), small_state(Mp, l, gs), small_state(Vp, l, gs),
                      _row_tile(rs, 512), f"adamw_small_{l}")

    gbuf0, table0 = small_pack(0, gsmall[0], [g_final, loss[0, :1]])
    sm0 = ag_start([gbuf0], "sm0", st["io", 0]["token"])
    o = adam_big(1, rs_end(st["ffn2", 1], sm0["token"], "ffn1"))
    st["io2", 0] = rs_mid(st["io", 0], o, "io0")
    o = adam_big(1, rs_end(st["io2", 1], st["io2", 0]["token"], "io1"))
    o = adam_big(0, rs_end(st["ffn2", 0], o, "ffn0"))
    small_res = {1: adam_small(1, _tie(st["sm1g"], o[-1]), gsmall[1])}
    sm0b = ag_mid(sm0, o + [small_res[1][0]], "sm0")
    small_res[0] = adam_small(0, ag_end(sm0b, sm0b["token"], "sm0")[0], gsmall[0])
    adam_big(0, rs_end(st["io2", 0], small_res[0][0], "io0"))

    for name in BIG:
        for i, dst in enumerate(outs4):
            dst[name] = _unpad_in(big_res[name][i]) if name == "w_in" else big_res[name][i]

    unp = {0: [_unpack(r, table0) for r in small_res[0]], 1: [_unpack(r, st["table1"]) for r in small_res[1]]}
    for j, nm in enumerate(lnames):
        if nm in SMALL_REPL:
            for i, dst in enumerate(outs4):
                dst[nm] = jnp.stack([unp[l][i][j] for l in range(DEPTH)])
    for i, dst in enumerate(outs4):
        dst["final_norm_w"] = unp[0][i][len(lnames)]
    loss_total = unp[0][0][len(lnames) + 1][0]

    gsh = []
    for nm in SMALL_SHARD:
        j = lnames.index(nm)
        width = Wp[nm].shape[-1]
        gsh.append(jnp.stack([lax.dynamic_slice_in_dim(unp[l][0][j], me * width, width, axis=1)
                              for l in range(DEPTH)]))
    sbuf, stable = _pack(gsh)
    res = _adamw(sbuf[None], _pack([Wp[n] for n in SMALL_SHARD])[0], _pack([Mp[n] for n in SMALL_SHARD])[0],
                 _pack([Vp[n] for n in SMALL_SHARD])[0], sbuf.shape[0], "adamw_conv_w")
    unp2 = [_unpack(r, stable) for r in res]
    for j, nm in enumerate(SMALL_SHARD):
        for i, dst in enumerate((out_g, out_d, out_m, out_v)):
            dst[nm] = unp2[i][j]

    return (loss_total, dx[None], *[out_g[n] for n in WEIGHTS], *[out_d[n] for n in WEIGHTS],
            *[out_m[n] for n in WEIGHTS], *[out_v[n] for n in WEIGHTS])
```

```python
import functools

import jax
import jax.numpy as jnp
from jax import lax
from jax.experimental import pallas as pl
from jax.experimental.pallas import tpu as pltpu

F32 = jnp.float32
BF16 = jnp.bfloat16
HI = lax.Precision.HIGHEST
MESH_IDS = pl.DeviceIdType.MESH

N_DEV = 8
D_MODEL = 2048
DEPTH = 2
POOL_WINDOWS = (2, 4, 8, 16)
POOL_W = 512
HEADS = 6
HD = 128
GDN_W = HEADS * HD
CHUNK = 64
LRU_W = 768
LRU_C = 8.0
D_FF = 3 * D_MODEL
EPS = 1e-6
IN_COLS = 5132
PCOLS = 5376
C_QKV, C_Z, C_AB, C_XR, C_GR = 512, 2816, 3584, 3840, 4608
AB_ORIG_END = 3596
M_GDN, M_LRU = 512, 1280

ADAM_LR, ADAM_B1, ADAM_B2, ADAM_EPS, ADAM_WD, ADAM_STEP = 0.001, 0.9, 0.999, 1e-08, 0.01, 10

VMEM_LIMIT = 56 * 1024 * 1024


def _cp(sem):
    return pltpu.CompilerParams(dimension_semantics=sem, vmem_limit_bytes=VMEM_LIMIT)


def _mm(a, b, ca=1, cb=0, prec=None, cast=True):
    if cast:
        a = a.astype(BF16)
        b = b.astype(BF16)
    return lax.dot_general(a, b, (((ca,), (cb,)), ((), ())), preferred_element_type=F32, precision=prec)


def _bmm(a, b, ca=2, cb=1, prec=None, cast=True):
    if cast:
        a = a.astype(BF16)
        b = b.astype(BF16)
    return lax.dot_general(a, b, (((ca,), (cb,)), ((0,), (0,))), preferred_element_type=F32, precision=prec)


def _sigmoid(x):
    return 1.0 / (1.0 + jnp.exp(-x))


def _log1p(e):
    u = 1.0 + e
    return jnp.where(u == 1.0, e, jnp.log(u) * e / jnp.where(u == 1.0, 1.0, u - 1.0))


def _softplus(x):
    return jnp.maximum(x, 0.0) + _log1p(jnp.exp(-jnp.abs(x)))


def _expm1(x):
    u = jnp.exp(x)
    um = u - 1.0
    safe = jnp.where((u == 1.0) | (um == -1.0), 1.0, jnp.log(u))
    return jnp.where(u == 1.0, x, jnp.where(um == -1.0, -1.0, um * x / safe))


_G0 = 0.7978845608028654
_G1 = 0.044715


def _gelu(x):
    return 0.5 * x * (1.0 + jnp.tanh(_G0 * (x + _G1 * x * x * x)))


def _gelu_and_grad(x):
    th = jnp.tanh(_G0 * (x + _G1 * x * x * x))
    g = 0.5 * x * (1.0 + th)
    dg = 0.5 * (1.0 + th) + 0.5 * x * (1.0 - th * th) * _G0 * (1.0 + 3.0 * _G1 * x * x)
    return g, dg


def _tile(T):
    return min(T, 512)


def _matmul(a, b, *, grid, a_spec, b_spec, out_shape, out_spec, dims, acc_shape, name, add=None, add_spec=None):
    nk = grid[2]
    has_add = add is not None

    def body(*refs):
        if has_add:
            a_ref, b_ref, add_ref, o_ref, acc_ref = refs
        else:
            a_ref, b_ref, o_ref, acc_ref = refs
            add_ref = None
        k = pl.program_id(2)
        p = lax.dot_general(a_ref[...].astype(BF16), b_ref[...].astype(BF16), (dims, ((), ())),
                            preferred_element_type=F32)

        def finish(r):
            if has_add:
                r = r + add_ref[...]
            o_ref[...] = r.astype(o_ref.dtype)

        if nk == 1:
            finish(p)
        else:
            @pl.when(k == 0)
            def _():
                acc_ref[...] = p

            @pl.when(k > 0)
            def _():
                acc_ref[...] += p

            @pl.when(k == nk - 1)
            def _():
                finish(acc_ref[...])

    in_specs = [a_spec, b_spec] + ([add_spec] if has_add else [])
    args = (a, b) + ((add,) if has_add else ())
    return pl.pallas_call(
        body, name=name, grid=grid, in_specs=in_specs, out_specs=out_spec, out_shape=out_shape,
        scratch_shapes=[pltpu.VMEM(acc_shape, F32)],
        compiler_params=_cp(("parallel", "parallel", "arbitrary")),
    )(*args)


def _pick(n, cands):
    for c in cands:
        if n % c == 0:
            return c
    raise ValueError(f"no tile for {n}")


def _mm_nn(a, b, out_dtype, name, add=None, tn_c=(1024, 768, 512)):
    M, K = a.shape
    N = b.shape[1]
    tm = _pick(M, (1024, 512, 256))
    tn = _pick(N, tn_c)
    tk = _pick(K, (2048, 1536, 1024, 512, 256))
    return _matmul(
        a, b, grid=(M // tm, N // tn, K // tk),
        a_spec=pl.BlockSpec((tm, tk), lambda i, j, k: (i, k)),
        b_spec=pl.BlockSpec((tk, tn), lambda i, j, k: (k, j)),
        out_shape=jax.ShapeDtypeStruct((M, N), out_dtype),
        out_spec=pl.BlockSpec((tm, tn), lambda i, j, k: (i, j)),
        dims=((1,), (0,)), acc_shape=(tm, tn), name=name, add=add,
        add_spec=pl.BlockSpec((tm, tn), lambda i, j, k: (i, j)))


def _mm_nt(a, b, out_dtype, name, tk_c=(2048, 1536, 1024, 768, 512)):
    M, K = a.shape
    N = b.shape[0]
    tm = _pick(M, (1024, 512, 256))
    tn = _pick(N, (1024, 768, 512))
    tk = _pick(K, tk_c)
    return _matmul(
        a, b, grid=(M // tm, N // tn, K // tk),
        a_spec=pl.BlockSpec((tm, tk), lambda i, j, k: (i, k)),
        b_spec=pl.BlockSpec((tn, tk), lambda i, j, k: (j, k)),
        out_shape=jax.ShapeDtypeStruct((M, N), out_dtype),
        out_spec=pl.BlockSpec((tm, tn), lambda i, j, k: (i, j)),
        dims=((1,), (1,)), acc_shape=(tm, tn), name=name)


def _mm_tn(a, b, out_dtype, name, tn_c=(1024, 768, 512)):
    K, M = a.shape
    N = b.shape[1]
    tm = _pick(M, (1024, 768, 512))
    tn = _pick(N, tn_c)
    tk = _pick(K, (1024, 512, 256))
    return _matmul(
        a, b, grid=(M // tm, N // tn, K // tk),
        a_spec=pl.BlockSpec((tk, tm), lambda i, j, k: (k, i)),
        b_spec=pl.BlockSpec((tk, tn), lambda i, j, k: (k, j)),
        out_shape=jax.ShapeDtypeStruct((M, N), out_dtype),
        out_spec=pl.BlockSpec((tm, tn), lambda i, j, k: (i, j)),
        dims=((0,), (0,)), acc_shape=(tm, tn), name=name)


def _mm_up(h, wup, name):
    M, K = h.shape
    ns = wup.shape[2]
    tm = _pick(M, (1024, 512, 256))
    tn = 768
    per = ns // tn
    return _matmul(
        h, wup, grid=(M // tm, N_DEV * per, 1),
        a_spec=pl.BlockSpec((tm, K), lambda i, j, k: (i, 0)),
        b_spec=pl.BlockSpec((None, K, tn), lambda i, j, k: (j // per, 0, j % per)),
        out_shape=jax.ShapeDtypeStruct((M, N_DEV * ns), BF16),
        out_spec=pl.BlockSpec((tm, tn), lambda i, j, k: (i, j)),
        dims=((1,), (0,)), acc_shape=(tm, tn), name=name)


def _mm_up_t(dup, wup, name):
    M = dup.shape[1]
    D, ns = wup.shape[1], wup.shape[2]
    tm = _pick(M, (1024, 512, 256))
    tn = 1024
    tk = ns
    return _matmul(
        dup, wup, grid=(M // tm, D // tn, N_DEV),
        a_spec=pl.BlockSpec((None, tm, tk), lambda i, j, k: (k // 4, i, k % 4)),
        b_spec=pl.BlockSpec((None, tn, tk), lambda i, j, k: (k, j, 0)),
        out_shape=jax.ShapeDtypeStruct((M, D), F32),
        out_spec=pl.BlockSpec((tm, tn), lambda i, j, k: (i, j)),
        dims=((1,), (1,)), acc_shape=(tm, tn), name=name)


def _mm_dup(ht, dup, ns, name):
    M, K = ht.shape
    tm = 1024
    tn = 768
    per = ns // tn
    half = 4 * per
    tk = _pick(K, (2048, 1024, 512, 256))
    return _matmul(
        ht, dup, grid=(M // tm, N_DEV * per, K // tk),
        a_spec=pl.BlockSpec((tm, tk), lambda i, j, k: (i, k)),
        b_spec=pl.BlockSpec((None, tk, tn), lambda i, j, k: (j // half, k, j % half)),
        out_shape=jax.ShapeDtypeStruct((N_DEV, M, ns), BF16),
        out_spec=pl.BlockSpec((None, tm, tn), lambda i, j, k: (j // per, i, j % per)),
        dims=((1,), (0,)), acc_shape=(tm, tn), name=name)


def _norm_fwd(x, w, name):
    T, D = x.shape
    tt = _tile(T)

    def body(x_ref, w_ref, h_ref, ht_ref):
        xv = x_ref[...]
        r = lax.rsqrt(jnp.mean(xv * xv, axis=1, keepdims=True) + EPS)
        hv = xv * r * w_ref[...]
        h_ref[...] = hv.astype(BF16)
        ht_ref[...] = hv.T.astype(BF16)

    return pl.pallas_call(
        body, name=name, grid=(T // tt,),
        in_specs=[pl.BlockSpec((tt, D), lambda t: (t, 0)), pl.BlockSpec((1, D), lambda t: (0, 0))],
        out_specs=[pl.BlockSpec((tt, D), lambda t: (t, 0)), pl.BlockSpec((D, tt), lambda t: (0, t))],
        out_shape=[jax.ShapeDtypeStruct((T, D), BF16), jax.ShapeDtypeStruct((D, T), BF16)],
        compiler_params=_cp(("parallel",)))(x, w)


def _norm_bwd(x, w, dh, dres, name):
    T, D = x.shape
    tt = _tile(T)

    def body(x_ref, w_ref, dh_ref, dres_ref, dx_ref, dxb_ref, dw_ref):
        t = pl.program_id(0)
        xv = x_ref[...]
        r = lax.rsqrt(jnp.mean(xv * xv, axis=1, keepdims=True) + EPS)
        xh = xv * r
        dh_v = dh_ref[...]
        dxh = dh_v * w_ref[...]
        dxv = dres_ref[...] + r * (dxh - xh * jnp.mean(dxh * xh, axis=1, keepdims=True))
        dx_ref[...] = dxv
        dxb_ref[...] = dxv.astype(BF16)
        part = jnp.sum(dh_v * xh, axis=0, keepdims=True)

        @pl.when(t == 0)
        def _():
            dw_ref[...] = part

        @pl.when(t > 0)
        def _():
            dw_ref[...] += part

    row = pl.BlockSpec((tt, D), lambda t: (t, 0))
    vec = pl.BlockSpec((1, D), lambda t: (0, 0))
    return pl.pallas_call(
        body, name=name, grid=(T // tt,), in_specs=[row, vec, row, row], out_specs=[row, row, vec],
        out_shape=[jax.ShapeDtypeStruct((T, D), F32), jax.ShapeDtypeStruct((T, D), BF16),
                   jax.ShapeDtypeStruct((1, D), F32)],
        compiler_params=_cp(("arbitrary",)))(x, w, dh, dres)


def _loss_head(x, w, target):
    T, D = x.shape
    tt = _tile(T)

    def body(x_ref, w_ref, t_ref, loss_ref, dx_ref, dxb_ref, dw_ref):
        t = pl.program_id(0)
        xv = x_ref[...]
        r = lax.rsqrt(jnp.mean(xv * xv, axis=1, keepdims=True) + EPS)
        xh = xv * r
        err = xh * w_ref[...] - t_ref[...]
        lp = 0.5 * jnp.sum(jnp.mean(err * err, axis=1, keepdims=True), axis=0, keepdims=True)
        dy = err * (1.0 / D)
        dxh = dy * w_ref[...]
        dxv = r * (dxh - xh * jnp.mean(dxh * xh, axis=1, keepdims=True))
        dx_ref[...] = dxv
        dxb_ref[...] = dxv.astype(BF16)
        part = jnp.sum(dy * xh, axis=0, keepdims=True)
        lpb = jnp.broadcast_to(lp, (1, 128))

        @pl.when(t == 0)
        def _():
            dw_ref[...] = part
            loss_ref[...] = lpb

        @pl.when(t > 0)
        def _():
            dw_ref[...] += part
            loss_ref[...] += lpb

    row = pl.BlockSpec((tt, D), lambda t: (t, 0))
    vec = pl.BlockSpec((1, D), lambda t: (0, 0))
    return pl.pallas_call(
        body, name="loss_head", grid=(T // tt,), in_specs=[row, vec, row],
        out_specs=[pl.BlockSpec((1, 128), lambda t: (0, 0)), row, row, vec],
        out_shape=[jax.ShapeDtypeStruct((1, 128), F32), jax.ShapeDtypeStruct((T, D), F32),
                   jax.ShapeDtypeStruct((T, D), BF16), jax.ShapeDtypeStruct((1, D), F32)],
        compiler_params=_cp(("arbitrary",)))(x, w, target)


def _conv_fwd(x, col0, C, w, b, cb, name):
    T = x.shape[0]
    K = w.shape[0]
    tt = _tile(T)
    nt, nc, c0 = T // tt, C // cb, col0 // cb
    has_b = b is not None

    def body(*refs):
        if has_b:
            x_ref, halo_ref, w_ref, b_ref, y_ref = refs
        else:
            x_ref, halo_ref, w_ref, y_ref = refs
        t = pl.program_id(1)
        halo = jnp.where(t == 0, 0.0, halo_ref[...])
        xe = jnp.concatenate([halo, x_ref[...]], axis=0)
        acc = xe * w_ref[K - 1:K, :]
        for j in range(K - 1):
            acc = acc + pltpu.roll(xe, K - 1 - j, 0) * w_ref[j:j + 1, :]
        if has_b:
            acc = acc + b_ref[...]
        y_ref[...] = acc[8:, :]

    in_specs = [pl.BlockSpec((tt, cb), lambda j, t: (t, c0 + j)),
                pl.BlockSpec((8, cb), lambda j, t: (jnp.maximum(t * (tt // 8) - 1, 0), c0 + j)),
                pl.BlockSpec((K, cb), lambda j, t: (0, j))]
    args = [x, x, w]
    if has_b:
        in_specs.append(pl.BlockSpec((1, cb), lambda j, t: (0, j)))
        args.append(b)
    return pl.pallas_call(
        body, name=name, grid=(nc, nt), in_specs=in_specs,
        out_specs=pl.BlockSpec((tt, cb), lambda j, t: (t, j)),
        out_shape=jax.ShapeDtypeStruct((T, C), F32), compiler_params=_cp(("parallel", "parallel")))(*args)


def _conv_bwd(dy, x, col0, w, cb, name, into, want_db=False):
    T, C = dy.shape
    K = w.shape[0]
    tt = _tile(T)
    nt, nc, c0 = T // tt, C // cb, col0 // cb

    def body(*refs):
        if want_db:
            dy_ref, dyn_ref, x_ref, xp_ref, w_ref, _, dx_ref, dw_ref, db_ref = refs
        else:
            dy_ref, dyn_ref, x_ref, xp_ref, w_ref, _, dx_ref, dw_ref = refs
        t = pl.program_id(1)
        dyv = dy_ref[...]
        nxt = jnp.where(t == nt - 1, 0.0, dyn_ref[...])
        dye = jnp.concatenate([dyv, nxt], axis=0)
        n = tt + 8
        acc = dye * w_ref[K - 1:K, :]
        for j in range(K - 1):
            acc = acc + pltpu.roll(dye, n - (K - 1 - j), 0) * w_ref[j:j + 1, :]
        dx_ref[...] = acc[:tt, :].astype(dx_ref.dtype)
        prev = jnp.where(t == 0, 0.0, xp_ref[...])
        xe = jnp.concatenate([prev, x_ref[...]], axis=0)

        @pl.when(t == 0)
        def _():
            dw_ref[...] = jnp.zeros_like(dw_ref)
            if want_db:
                db_ref[...] = jnp.zeros_like(db_ref)

        for j in range(K):
            sh = K - 1 - j
            xs = xe[8:, :] if sh == 0 else pltpu.roll(xe, sh, 0)[8:, :]
            dw_ref[j:j + 1, :] += jnp.sum(dyv * xs, axis=0, keepdims=True)
        if want_db:
            db_ref[...] += jnp.sum(dyv, axis=0, keepdims=True)

    h8 = tt // 8
    in_specs = [pl.BlockSpec((tt, cb), lambda j, t: (t, j)),
                pl.BlockSpec((8, cb), lambda j, t: (jnp.minimum((t + 1) * h8, T // 8 - 1), j)),
                pl.BlockSpec((tt, cb), lambda j, t: (t, c0 + j)),
                pl.BlockSpec((8, cb), lambda j, t: (jnp.maximum(t * h8 - 1, 0), c0 + j)),
                pl.BlockSpec((K, cb), lambda j, t: (0, j)), pl.BlockSpec(memory_space=pl.ANY)]
    out_specs = [pl.BlockSpec((tt, cb), lambda j, t: (t, c0 + j)), pl.BlockSpec((K, cb), lambda j, t: (0, j))]
    out_shape = [jax.ShapeDtypeStruct(into.shape, into.dtype), jax.ShapeDtypeStruct((K, C), F32)]
    if want_db:
        out_specs.append(pl.BlockSpec((1, cb), lambda j, t: (0, j)))
        out_shape.append(jax.ShapeDtypeStruct((1, C), F32))
    return pl.pallas_call(
        body, name=name, grid=(nc, nt), in_specs=in_specs, out_specs=out_specs, out_shape=out_shape,
        input_output_aliases={5: 0}, compiler_params=_cp(("parallel", "arbitrary")))(dy, dy, x, x, w, into)


def _pool_d(ue, g, pos, tt):
    win = POOL_WINDOWS[g]
    ug = ue[:, g * 128:(g + 1) * 128]
    s = ug
    k = 1
    while k < win:
        s = s + pltpu.roll(s, k, 0)
        k *= 2
    cnt = jnp.minimum(pos + 1, win).astype(F32)
    return s[16:, :] / cnt - ug[16:, :]


def _pool_fwd(proj, pw, pb, ps):
    T = proj.shape[0]
    tt = _tile(T)

    def body(u_ref, halo_ref, w_ref, b_ref, s_ref, y_ref):
        t = pl.program_id(0)
        halo = jnp.where(t == 0, 0.0, halo_ref[...])
        ue = jnp.concatenate([halo, u_ref[...]], axis=0)
        pos = t * tt + lax.broadcasted_iota(jnp.int32, (tt, 1), 0)
        for g in range(4):
            sl = slice(g * 128, (g + 1) * 128)
            d = _pool_d(ue, g, pos, tt)
            yg = _mm(d, w_ref[g]) + b_ref[:, sl]
            y_ref[:, sl] = (yg * s_ref[:, sl]).astype(BF16)

    vec = pl.BlockSpec((1, POOL_W), lambda t: (0, 0))
    return pl.pallas_call(
        body, name="pool_fwd", grid=(T // tt,),
        in_specs=[pl.BlockSpec((tt, POOL_W), lambda t: (t, 0)),
                  pl.BlockSpec((16, POOL_W), lambda t: (jnp.maximum(t * (tt // 16) - 1, 0), 0)),
                  pl.BlockSpec((4, 128, 128), lambda t: (0, 0, 0)), vec, vec],
        out_specs=pl.BlockSpec((tt, POOL_W), lambda t: (t, 0)),
        out_shape=jax.ShapeDtypeStruct((T, D_MODEL), BF16), compiler_params=_cp(("parallel",)))(
            proj, proj, pw, pb, ps)


def _pool_bwd(dmixed, proj, pw, pb, ps):
    T = proj.shape[0]
    tt = _tile(T)
    nt = T // tt

    def body(dy_ref, dyn_ref, u_ref, halo_ref, w_ref, b_ref, s_ref, du_ref, dw_ref, db_ref, ds_ref):
        t = pl.program_id(0)
        halo = jnp.where(t == 0, 0.0, halo_ref[...])
        ue = jnp.concatenate([halo, u_ref[...]], axis=0)
        dyv = dy_ref[...]
        nxt = jnp.where(t == nt - 1, 0.0, dyn_ref[...])
        dye = jnp.concatenate([dyv, nxt], axis=0)
        n = tt + 16
        pos = t * tt + lax.broadcasted_iota(jnp.int32, (tt, 1), 0)
        pos_e = t * tt + lax.broadcasted_iota(jnp.int32, (n, 1), 0)

        @pl.when(t == 0)
        def _():
            dw_ref[...] = jnp.zeros_like(dw_ref)
            db_ref[...] = jnp.zeros_like(db_ref)
            ds_ref[...] = jnp.zeros_like(ds_ref)

        for g in range(4):
            win = POOL_WINDOWS[g]
            sl = slice(g * 128, (g + 1) * 128)
            d = _pool_d(ue, g, pos, tt)
            wg = w_ref[g]
            ypre = _mm(d, wg) + b_ref[:, sl]
            sc = s_ref[:, sl]
            ds_ref[:, sl] += jnp.sum(dyv[:, sl] * ypre, axis=0, keepdims=True)
            dyp_e = dye[:, sl] * sc
            dyp = dyp_e[:tt, :]
            db_ref[:, sl] += jnp.sum(dyp, axis=0, keepdims=True)
            dw_ref[g] += _mm(d, dyp, 0, 0)
            dd_e = _mm(dyp_e, wg, 1, 1)
            cnt_e = jnp.minimum(pos_e + 1, win).astype(F32)
            s = dd_e / cnt_e
            k = 1
            while k < win:
                s = s + pltpu.roll(s, n - k, 0)
                k *= 2
            du_ref[:, sl] = (s[:tt, :] - dd_e[:tt, :]).astype(BF16)

    vec = pl.BlockSpec((1, POOL_W), lambda t: (0, 0))
    h16 = tt // 16
    return pl.pallas_call(
        body, name="pool_bwd", grid=(nt,),
        in_specs=[pl.BlockSpec((tt, POOL_W), lambda t: (t, 0)),
                  pl.BlockSpec((16, POOL_W), lambda t: (jnp.minimum((t + 1) * h16, T // 16 - 1), 0)),
                  pl.BlockSpec((tt, POOL_W), lambda t: (t, 0)),
                  pl.BlockSpec((16, POOL_W), lambda t: (jnp.maximum(t * h16 - 1, 0), 0)),
                  pl.BlockSpec((4, 128, 128), lambda t: (0, 0, 0)), vec, vec],
        out_specs=[pl.BlockSpec((tt, POOL_W), lambda t: (t, 0)),
                   pl.BlockSpec((4, 128, 128), lambda t: (0, 0, 0)), vec, vec],
        out_shape=[jax.ShapeDtypeStruct((T, PCOLS), BF16), jax.ShapeDtypeStruct((4, 128, 128), F32),
                   jax.ShapeDtypeStruct((1, POOL_W), F32), jax.ShapeDtypeStruct((1, POOL_W), F32)],
        compiler_params=_cp(("arbitrary",)))(dmixed, dmixed, proj, proj, pw, pb, ps)


def _gdn_pre_fwd(cpre, proj, alog, dtb):
    T = cpre.shape[0]
    tt = _tile(T)

    def body(c_ref, ab_ref, alog_ref, dtb_ref, qkv_ref, bb_ref, gb_ref):
        for p in range(3):
            for h in range(HEADS):
                cc = c_ref[:, (p * HEADS + h) * HD:(p * HEADS + h + 1) * HD]
                s = cc * _sigmoid(cc)
                if p < 2:
                    s = s * lax.rsqrt(jnp.sum(s * s, axis=1, keepdims=True) + EPS)
                if p == 0:
                    s = s * (HD ** -0.5)
                qkv_ref[p, h] = s
        ab = ab_ref[...]
        g = -jnp.exp(alog_ref[...]) * _softplus(ab + dtb_ref[...])
        r64 = lax.broadcasted_iota(jnp.int32, (tt, 1), 0) & (CHUNK - 1)
        k = 1
        while k < CHUNK:
            g = g + jnp.where(r64 >= k, pltpu.roll(g, k, 0), 0.0)
            k *= 2
        sb = _sigmoid(ab)
        for h in range(HEADS):
            gb_ref[h] = jnp.broadcast_to(g[:, h:h + 1], (tt, HD))
            bb_ref[h] = jnp.broadcast_to(sb[:, HEADS + h:HEADS + h + 1], (tt, HD))

    vec = pl.BlockSpec((1, 128), lambda t: (0, 0))
    hb = pl.BlockSpec((HEADS, tt, HD), lambda t: (0, t, 0))
    return pl.pallas_call(
        body, name="gdn_pre_fwd", grid=(T // tt,),
        in_specs=[pl.BlockSpec((tt, 3 * GDN_W), lambda t: (t, 0)),
                  pl.BlockSpec((tt, 128), lambda t: (t, C_AB // 128)), vec, vec],
        out_specs=[pl.BlockSpec((3, HEADS, tt, HD), lambda t: (0, 0, t, 0)), hb, hb],
        out_shape=[jax.ShapeDtypeStruct((3, HEADS, T, HD), F32), jax.ShapeDtypeStruct((HEADS, T, HD), F32),
                   jax.ShapeDtypeStruct((HEADS, T, HD), F32)],
        compiler_params=_cp(("parallel",)))(cpre, proj, alog, dtb)


def _gdn_pre_bwd(dq, dk, dv, cpre, dbb, dgb, proj, alog, dtb, dproj):
    T = cpre.shape[0]
    tt = _tile(T)

    def body(dq_ref, dk_ref, dv_ref, c_ref, dbb_ref, dgb_ref, ab_ref, alog_ref, dtb_ref, _,
             dc_ref, dab_ref, dalog_ref, ddtb_ref):
        t = pl.program_id(0)
        srcs = (dq_ref, dk_ref, dv_ref)
        for p in range(3):
            for h in range(HEADS):
                sl = slice((p * HEADS + h) * HD, (p * HEADS + h + 1) * HD)
                cc = c_ref[:, sl]
                sg = _sigmoid(cc)
                s = cc * sg
                dyv = srcs[p][h]
                if p < 2:
                    r = lax.rsqrt(jnp.sum(s * s, axis=1, keepdims=True) + EPS)
                    y = s * r
                    if p == 0:
                        dyv = dyv * (HD ** -0.5)
                    ds = r * (dyv - y * jnp.sum(dyv * y, axis=1, keepdims=True))
                else:
                    ds = dyv
                dc_ref[:, sl] = ds * sg * (1.0 + cc * (1.0 - sg))
        lane = lax.broadcasted_iota(jnp.int32, (tt, 128), 1)
        dg = jnp.zeros((tt, 128), F32)
        dbeta = jnp.zeros((tt, 128), F32)
        for h in range(HEADS):
            dg = jnp.where(lane == h, dgb_ref[h], dg)
            dbeta = jnp.where(lane == HEADS + h, dbb_ref[h], dbeta)
        r64 = lax.broadcasted_iota(jnp.int32, (tt, 1), 0) & (CHUNK - 1)
        k = 1
        while k < CHUNK:
            dg = dg + jnp.where(r64 < CHUNK - k, pltpu.roll(dg, tt - k, 0), 0.0)
            k *= 2
        ab = ab_ref[...]
        e = jnp.exp(alog_ref[...])
        xx = ab + dtb_ref[...]
        g = -e * _softplus(xx)
        da = jnp.where(lane < HEADS, dg * (-e) * _sigmoid(xx), 0.0)
        pa = jnp.sum(jnp.where(lane < HEADS, dg * g, 0.0), axis=0, keepdims=True)
        pd = jnp.sum(da, axis=0, keepdims=True)

        @pl.when(t == 0)
        def _():
            dalog_ref[...] = pa
            ddtb_ref[...] = pd

        @pl.when(t > 0)
        def _():
            dalog_ref[...] += pa
            ddtb_ref[...] += pd

        sb = _sigmoid(ab)
        dab_ref[:, :128] = jnp.where(lane < HEADS, da, dbeta * sb * (1.0 - sb)).astype(BF16)
        dab_ref[:, 128:] = jnp.zeros((tt, 128), BF16)

    vec = pl.BlockSpec((1, 128), lambda t: (0, 0))
    hb = pl.BlockSpec((HEADS, tt, HD), lambda t: (0, t, 0))
    return pl.pallas_call(
        body, name="gdn_pre_bwd", grid=(T // tt,),
        in_specs=[hb, hb, hb, pl.BlockSpec((tt, 3 * GDN_W), lambda t: (t, 0)), hb, hb,
                  pl.BlockSpec((tt, 128), lambda t: (t, C_AB // 128)), vec, vec, pl.BlockSpec(memory_space=pl.ANY)],
        out_specs=[pl.BlockSpec((tt, 3 * GDN_W), lambda t: (t, 0)),
                   pl.BlockSpec((tt, 256), lambda t: (t, C_AB // 256)), vec, vec],
        out_shape=[jax.ShapeDtypeStruct((T, 3 * GDN_W), F32), jax.ShapeDtypeStruct(dproj.shape, dproj.dtype),
                   jax.ShapeDtypeStruct((1, 128), F32), jax.ShapeDtypeStruct((1, 128), F32)],
        input_output_aliases={9: 1},
        compiler_params=_cp(("arbitrary",)))(dq, dk, dv, cpre, dbb, dgb, proj, alog, dtb, dproj)


def _split2(x):
    hi = x.astype(BF16)
    return hi, (x - hi.astype(F32)).astype(BF16)


def _bmm3s(a2, b2, ca=2, cb=1):
    def f(x, y):
        return lax.dot_general(x, y, (((ca,), (cb,)), ((0,), (0,))), preferred_element_type=F32)

    return f(a2[0], b2[0]) + (f(a2[0], b2[1]) + f(a2[1], b2[0]))


def _bmm3(a, b, ca=2, cb=1):
    return _bmm3s(_split2(a), _split2(b), ca, cb)


def _tri_inv(a):
    nb = a.shape[0]
    ri = lax.broadcasted_iota(jnp.int32, (nb, CHUNK, CHUNK), 1)
    ci = lax.broadcasted_iota(jnp.int32, (nb, CHUNK, CHUNK), 2)
    n = -a
    p = jnp.where(ri == ci, 1.0, 0.0) + n
    n2 = _split2(n)
    for _ in range(5):
        n2 = _split2(_bmm3s(n2, n2))
        p = p + _bmm3s(_split2(p), n2)
    return p


def _gdn_chunk_common(q, k, v, bb3, gb3, tm_saved=None):
    nb = q.shape[0]
    need_t = tm_saved is not None
    beta = bb3[:, :, 0:1]
    gcol = gb3[:, :, 0:1]
    bcol = bb3[:, :, :CHUNK]
    gcm = gb3[:, :, :CHUNK]
    oh = jnp.where(lax.broadcasted_iota(jnp.int32, (nb, CHUNK, HD), 2) == 0, 1.0, 0.0)
    grow = _bmm(oh, gb3, 2, 2, HI, False)
    ri = lax.broadcasted_iota(jnp.int32, (nb, CHUNK, CHUNK), 1)
    ci = lax.broadcasted_iota(jnp.int32, (nb, CHUNK, CHUNK), 2)
    tril, stl = ri >= ci, ri > ci
    dg = gcm - grow
    dec = jnp.where(tril, jnp.exp(jnp.where(tril, dg, 0.0)), 0.0)
    kk = _bmm(k, k, 2, 2)
    qk = _bmm(q, k, 2, 2)
    tm = tm_saved if need_t else _tri_inv(jnp.where(stl, bcol * kk * dec, 0.0))
    gam = jnp.exp(gcol)
    glast = gb3[:, CHUNK - 1:CHUNK, 0:1]
    egl = jnp.exp(glast)
    rw = k * (beta * gam)
    ru = v * beta
    wu = _bmm3(tm, jnp.concatenate([rw, ru], axis=2), 2, 1)
    kdf = jnp.exp(glast - gcol)
    out = dict(beta=beta, bcol=bcol, tril=tril, stl=stl, dec=dec, kk=kk, qk=qk, tm=tm, gam=gam, egl=egl,
               rw=rw, wu=wu, at=qk * dec, qd=q * gam, kdf=kdf, kd=k * kdf)
    if need_t:
        brow = _bmm(oh, bb3, 2, 2, HI, False)
        triu, stu = ri <= ci, ri < ci
        dect = jnp.where(triu, jnp.exp(jnp.where(triu, -dg, 0.0)), 0.0)
        qkt = _bmm(k, q, 2, 2)
        eye = jnp.where(ri == ci, 1.0, 0.0)
        out.update(brow=brow, triu=triu, stu=stu, dect=dect, qkt=qkt, tmt=_bmm3(eye, tm, 2, 2),
                   att=qkt * dect)
    return out


def _gdn_rows(T):
    return min(T, 512)


def _gdn_fwd(qkv, bb, gb, proj, nw, mixed):
    T = proj.shape[0]
    R = _gdn_rows(T)
    nb = R // CHUNK

    def body(q_ref, k_ref, v_ref, bb_ref, gb_ref, z_ref, nw_ref, _, y_ref, st_ref, tm_ref,
             s_ref, w_s, u_s, at_s, qd_s, kd_s):
        t = pl.program_id(1)

        @pl.when(t == 0)
        def _():
            s_ref[...] = jnp.zeros_like(s_ref)

        sh = (nb, CHUNK, HD)
        q, k, v = q_ref[...].reshape(sh), k_ref[...].reshape(sh), v_ref[...].reshape(sh)
        c = _gdn_chunk_common(q, k, v, bb_ref[...].reshape(sh), gb_ref[...].reshape(sh))
        tm_ref[...] = c["tm"]
        w_s[...] = c["wu"][:, :, :HD]
        u_s[...] = c["wu"][:, :, HD:]
        at_s[...] = c["at"]
        qd_s[...] = c["qd"]
        kd_s[...] = c["kd"]
        egl = c["egl"]
        nwv = nw_ref[...]
        for n in range(nb):
            s = s_ref[...]
            st_ref[n] = s
            vn = u_s[n] - _mm(w_s[n], s)
            o = _mm(qd_s[n], s) + _mm(at_s[n], vn)
            s_ref[...] = s * egl[n] + _mm(kd_s[n], vn, 0, 0)
            rows = slice(n * CHUNK, (n + 1) * CHUNK)
            zz = z_ref[rows, :]
            on = o * lax.rsqrt(jnp.mean(o * o, axis=1, keepdims=True) + EPS)
            y_ref[rows, :] = (on * nwv * (zz * _sigmoid(zz))).astype(BF16)

    def hm(p):
        return pl.BlockSpec((None, None, R, HD), lambda h, t: (p, h, t, 0))

    hb = pl.BlockSpec((None, R, HD), lambda h, t: (h, t, 0))
    cs = pltpu.VMEM((nb, CHUNK, HD), F32)
    return pl.pallas_call(
        body, name="gdn_fwd", grid=(HEADS, T // R),
        in_specs=[hm(0), hm(1), hm(2), hb, hb, pl.BlockSpec((R, HD), lambda h, t: (t, C_Z // HD + h)),
                  pl.BlockSpec((1, HD), lambda h, t: (0, 0)), pl.BlockSpec(memory_space=pl.ANY)],
        out_specs=[pl.BlockSpec((R, HD), lambda h, t: (t, M_GDN // HD + h)),
                   pl.BlockSpec((None, nb, HD, HD), lambda h, t: (h, t, 0, 0)),
                   pl.BlockSpec((None, nb, CHUNK, CHUNK), lambda h, t: (h, t, 0, 0))],
        out_shape=[jax.ShapeDtypeStruct(mixed.shape, mixed.dtype),
                   jax.ShapeDtypeStruct((HEADS, T // CHUNK, HD, HD), F32),
                   jax.ShapeDtypeStruct((HEADS, T // CHUNK, CHUNK, CHUNK), F32)],
        scratch_shapes=[pltpu.VMEM((HD, HD), F32), cs, cs, pltpu.VMEM((nb, CHUNK, CHUNK), F32), cs, cs],
        input_output_aliases={7: 0},
        compiler_params=_cp(("parallel", "arbitrary")))(qkv, qkv, qkv, bb, gb, proj, nw, mixed)


def _gdn_bwd(qkv, bb, gb, proj, nw, states, tms, dmixed, dproj):
    T = proj.shape[0]
    R = _gdn_rows(T)
    nb = R // CHUNK
    ntb = T // R

    def body(q_ref, k_ref, v_ref, bb_ref, gb_ref, z_ref, nw_ref, st_ref, tm_ref, dy_ref, _,
             dq_ref, dk_ref, dv_ref, dbb_ref, dgb_ref, dz_ref, dnw_ref,
             ds_ref, att_s, do_s, kd_s, vn_s, qd_s, w_s, dvn_s, dkd_s, dgl_s):
        hh = pl.program_id(0)
        t = pl.program_id(1)

        @pl.when(t == 0)
        def _():
            ds_ref[...] = jnp.zeros_like(ds_ref)

        @pl.when((t == 0) & (hh == 0))
        def _():
            dnw_ref[...] = jnp.zeros_like(dnw_ref)

        sh = (nb, CHUNK, HD)
        q, k, v = q_ref[...].reshape(sh), k_ref[...].reshape(sh), v_ref[...].reshape(sh)
        c = _gdn_chunk_common(q, k, v, bb_ref[...].reshape(sh), gb_ref[...].reshape(sh), tm_ref[...])
        w, u = c["wu"][:, :, :HD], c["wu"][:, :, HD:]
        sall = st_ref[...]
        vn = u - _bmm(w, sall, 2, 1)
        o = _bmm(c["qd"], sall, 2, 1) + _bmm(c["at"], vn, 2, 1)
        z = z_ref[...].reshape(sh)
        dy = dy_ref[...].reshape(sh)
        nwv = nw_ref[...].reshape(1, 1, HD)
        rs = lax.rsqrt(jnp.mean(o * o, axis=2, keepdims=True) + EPS)
        on = o * rs
        sg = _sigmoid(z)
        sz = z * sg
        dnw_ref[...] += jnp.sum(jnp.sum(dy * on * sz, axis=0), axis=0, keepdims=True)
        dz_ref[...] = (dy * on * nwv * (sg * (1.0 + z * (1.0 - sg)))).reshape(R, HD).astype(BF16)
        don = dy * nwv * sz
        do = rs * (don - on * jnp.mean(don * on, axis=2, keepdims=True))
        dqd = _bmm(do, sall, 2, 2)
        dat = jnp.where(c["tril"], _bmm(do, vn, 2, 2), 0.0)
        datt = jnp.where(c["triu"], _bmm(vn, do, 2, 2), 0.0)
        att_s[...] = c["att"]
        do_s[...] = do
        kd_s[...] = c["kd"]
        vn_s[...] = vn
        qd_s[...] = c["qd"]
        w_s[...] = w
        egl = c["egl"]
        for n in reversed(range(nb)):
            dso = ds_ref[...]
            dvn_n = _mm(att_s[n], do_s[n]) + _mm(kd_s[n], dso)
            dkd_s[n] = _mm(vn_s[n], dso, 1, 1)
            dgl = egl[n] * jnp.sum(jnp.sum(st_ref[n] * dso, axis=1, keepdims=True), axis=0, keepdims=True)
            dgl_s[n] = jnp.broadcast_to(dgl, (8, HD))
            ds_ref[...] = egl[n] * dso + _mm(qd_s[n], do_s[n], 0, 0) - _mm(w_s[n], dvn_n, 0, 0)
            dvn_s[n] = dvn_n
        dvn = dvn_s[...]
        dkd = dkd_s[...]
        dgl = dgl_s[...][:, 0:1, 0:1]
        dw = -_bmm(dvn, sall, 2, 2)
        dr = _bmm3(c["tmt"], jnp.concatenate([dw, dvn], axis=2), 2, 1)
        drw, dru = dr[:, :, :HD], dr[:, :, HD:]
        wu = c["wu"]
        dr2, wu2 = _split2(dr), _split2(wu)
        da = -jnp.where(c["stl"], _bmm3s(dr2, wu2, 2, 2), 0.0)
        da_t = -jnp.where(c["stu"], _bmm3s(wu2, dr2, 2, 2), 0.0)
        beta, gam, dec, dect, kk = c["beta"], c["gam"], c["dec"], c["dect"], c["kk"]
        bcol, brow = c["bcol"], c["brow"]
        dbeta = (jnp.sum(da * kk * dec, axis=2, keepdims=True)
                 + jnp.sum(drw * k * gam + dru * v, axis=2, keepdims=True))
        dkk = bcol * da * dec
        dkk_t = brow * da_t * dect
        e = (bcol * da * kk + dat * c["qk"]) * dec
        e_t = (brow * da_t * kk + datt * c["qkt"]) * dect
        kd = c["kd"]
        dq_ref[...] = (_bmm(dat * dec, k, 2, 1) + dqd * gam).reshape(R, HD)
        dk_ref[...] = (_bmm(datt * dect, q, 2, 1) + _bmm(dkk + dkk_t, k, 2, 1) + dkd * c["kdf"]
                       + drw * (beta * gam)).reshape(R, HD)
        dv_ref[...] = (dru * beta).reshape(R, HD)
        skd = jnp.sum(dkd * kd, axis=2, keepdims=True)
        dgc = (jnp.sum(e, axis=2, keepdims=True) - jnp.sum(e_t, axis=2, keepdims=True)
               + jnp.sum(drw * c["rw"] + dqd * c["qd"], axis=2, keepdims=True) - skd)
        tot = jnp.sum(skd, axis=1, keepdims=True) + dgl
        rowi = lax.broadcasted_iota(jnp.int32, (nb, CHUNK, 1), 1)
        dgc = dgc + jnp.where(rowi == CHUNK - 1, tot, 0.0)
        dbb_ref[...] = jnp.broadcast_to(dbeta, sh).reshape(R, HD)
        dgb_ref[...] = jnp.broadcast_to(dgc, sh).reshape(R, HD)

    def rt(t):
        return ntb - 1 - t

    def hm(p):
        return pl.BlockSpec((None, None, R, HD), lambda h, t: (p, h, rt(t), 0))

    hb = pl.BlockSpec((None, R, HD), lambda h, t: (h, rt(t), 0))
    cs = pltpu.VMEM((nb, CHUNK, HD), F32)
    ob = jax.ShapeDtypeStruct((HEADS, T, HD), F32)
    return pl.pallas_call(
        body, name="gdn_bwd", grid=(HEADS, ntb),
        in_specs=[hm(0), hm(1), hm(2), hb, hb, pl.BlockSpec((R, HD), lambda h, t: (rt(t), C_Z // HD + h)),
                  pl.BlockSpec((1, HD), lambda h, t: (0, 0)),
                  pl.BlockSpec((None, nb, HD, HD), lambda h, t: (h, rt(t), 0, 0)),
                  pl.BlockSpec((None, nb, CHUNK, CHUNK), lambda h, t: (h, rt(t), 0, 0)),
                  pl.BlockSpec((R, HD), lambda h, t: (rt(t), M_GDN // HD + h)), pl.BlockSpec(memory_space=pl.ANY)],
        out_specs=[hb, hb, hb, hb, hb, pl.BlockSpec((R, HD), lambda h, t: (rt(t), C_Z // HD + h)),
                   pl.BlockSpec((1, HD), lambda h, t: (0, 0))],
        out_shape=[ob, ob, ob, ob, ob, jax.ShapeDtypeStruct(dproj.shape, dproj.dtype),
                   jax.ShapeDtypeStruct((1, HD), F32)],
        scratch_shapes=[pltpu.VMEM((HD, HD), F32), pltpu.VMEM((nb, CHUNK, CHUNK), F32), cs, cs, cs, cs, cs, cs, cs,
                        pltpu.VMEM((nb, 8, HD), F32)],
        input_output_aliases={10: 5},
        compiler_params=_cp(("arbitrary", "arbitrary")))(qkv, qkv, qkv, bb, gb, proj, nw, states, tms, dmixed,
                                                         dproj)


def _lru_gates(xc, wa, ba, wx, bx, lam, gpos):
    xb = xc.astype(BF16)
    r = _sigmoid(_mm(xb, wa) + ba)
    i = _sigmoid(_mm(xb, wx) + bx)
    sp = _softplus(-lam)
    log_a = -LRU_C * r * sp
    a = jnp.exp(log_a)
    mult = jnp.where(gpos == 0, 1.0, jnp.sqrt(-_expm1(2.0 * log_a)))
    return r, i, sp, a, mult


def _lru_fwd(xc, proj, wa, ba, wx, bx, lam, mixed):
    T = xc.shape[0]
    tt = _tile(T)

    def body(xc_ref, gr_ref, wa_ref, ba_ref, wx_ref, bx_ref, lam_ref, _, y_ref, h_ref, carry_ref):
        t = pl.program_id(1)

        @pl.when(t == 0)
        def _():
            carry_ref[...] = jnp.zeros_like(carry_ref)

        row = lax.broadcasted_iota(jnp.int32, (tt, 1), 0)
        xcv = xc_ref[...]
        r, i, sp, a, mult = _lru_gates(xcv, wa_ref[...], ba_ref[...], wx_ref[...], bx_ref[...], lam_ref[...],
                                       t * tt + row)
        av, bv = a, mult * i * xcv
        k = 1
        while k < tt:
            a_s = jnp.where(row >= k, pltpu.roll(av, k, 0), 1.0)
            b_s = jnp.where(row >= k, pltpu.roll(bv, k, 0), 0.0)
            bv = bv + av * b_s
            av = av * a_s
            k *= 2
        h = bv + av * carry_ref[0:1, :]
        carry_ref[...] = jnp.broadcast_to(h[tt - 1:tt, :], (8, 128))
        h_ref[...] = h
        y_ref[...] = (h * _gelu(gr_ref[...])).astype(BF16)

    blk = pl.BlockSpec((tt, 128), lambda j, t: (t, j))
    vec = pl.BlockSpec((1, 128), lambda j, t: (0, j))
    mat = pl.BlockSpec((None, 128, 128), lambda j, t: (j, 0, 0))
    return pl.pallas_call(
        body, name="lru_fwd", grid=(LRU_W // 128, T // tt),
        in_specs=[blk, pl.BlockSpec((tt, 128), lambda j, t: (t, C_GR // 128 + j)), mat, vec, mat, vec, vec,
                  pl.BlockSpec(memory_space=pl.ANY)],
        out_specs=[pl.BlockSpec((tt, 128), lambda j, t: (t, M_LRU // 128 + j)), blk],
        out_shape=[jax.ShapeDtypeStruct(mixed.shape, mixed.dtype), jax.ShapeDtypeStruct((T, LRU_W), F32)],
        scratch_shapes=[pltpu.VMEM((8, 128), F32)], input_output_aliases={7: 0},
        compiler_params=_cp(("parallel", "arbitrary")))(xc, proj, wa, ba, wx, bx, lam, mixed)


def _lru_bwd(dmixed, xc, proj, hst, wa, ba, wx, bx, lam, dproj):
    T = xc.shape[0]
    tt = _tile(T)
    nt = T // tt

    def body(dy_ref, xc_ref, gr_ref, h_ref, hp_ref, wa_ref, ba_ref, wx_ref, bx_ref, lam_ref, _,
             dxc_ref, dgr_ref, dwa_ref, dwx_ref, dba_ref, dbx_ref, dlam_ref, lc_ref, ac_ref):
        t = pl.program_id(1)
        tr = nt - 1 - t

        @pl.when(t == 0)
        def _():
            lc_ref[...] = jnp.zeros_like(lc_ref)
            ac_ref[...] = jnp.zeros_like(ac_ref)
            dwa_ref[...] = jnp.zeros_like(dwa_ref)
            dwx_ref[...] = jnp.zeros_like(dwx_ref)
            dba_ref[...] = jnp.zeros_like(dba_ref)
            dbx_ref[...] = jnp.zeros_like(dbx_ref)
            dlam_ref[...] = jnp.zeros_like(dlam_ref)

        row = lax.broadcasted_iota(jnp.int32, (tt, 1), 0)
        gpos = tr * tt + row
        xcv = xc_ref[...]
        wav, wxv, lamv = wa_ref[...], wx_ref[...], lam_ref[...]
        r, i, sp, a, mult = _lru_gates(xcv, wav, ba_ref[...], wxv, bx_ref[...], lamv, gpos)
        h = h_ref[...]
        dy = dy_ref[...]
        gg, dgg = _gelu_and_grad(gr_ref[...])
        dgr_ref[...] = (dy * h * dgg).astype(BF16)
        bv = dy * gg
        cv = jnp.where(row < tt - 1, pltpu.roll(a, tt - 1, 0), ac_ref[0:1, :])
        k = 1
        while k < tt:
            c_s = jnp.where(row < tt - k, pltpu.roll(cv, tt - k, 0), 1.0)
            b_s = jnp.where(row < tt - k, pltpu.roll(bv, tt - k, 0), 0.0)
            bv = bv + cv * b_s
            cv = cv * c_s
            k *= 2
        lm = bv + cv * lc_ref[0:1, :]
        lc_ref[...] = jnp.broadcast_to(lm[0:1, :], (8, 128))
        ac_ref[...] = jnp.broadcast_to(a[0:1, :], (8, 128))
        hp = jnp.where(tr == 0, 0.0, hp_ref[...])
        hs = pltpu.roll(jnp.concatenate([hp, h], axis=0), 1, 0)[8:, :]
        da = lm * hs
        dmult = lm * i * xcv
        di = lm * mult * xcv
        dxc = lm * mult * i
        dlog_a = a * da - jnp.where(gpos == 0, 0.0, dmult * a * a / mult)
        dr = dlog_a * (-LRU_C * sp)
        dsp = jnp.sum(dlog_a * (-LRU_C * r), axis=0, keepdims=True)
        dlam_ref[...] += dsp * (-_sigmoid(-lamv))
        dpr = dr * r * (1.0 - r)
        dpi = di * i * (1.0 - i)
        dba_ref[...] += jnp.sum(dpr, axis=0, keepdims=True)
        dbx_ref[...] += jnp.sum(dpi, axis=0, keepdims=True)
        dwa_ref[...] += _mm(xcv, dpr, 0, 0)
        dwx_ref[...] += _mm(xcv, dpi, 0, 0)
        dxc_ref[...] = dxc + _mm(dpr, wav, 1, 1) + _mm(dpi, wxv, 1, 1)

    def rt(t):
        return nt - 1 - t

    blk = pl.BlockSpec((tt, 128), lambda j, t: (rt(t), j))
    vec = pl.BlockSpec((1, 128), lambda j, t: (0, j))
    mat = pl.BlockSpec((None, 128, 128), lambda j, t: (j, 0, 0))
    h8 = tt // 8
    mshape = jax.ShapeDtypeStruct((LRU_W // 128, 128, 128), F32)
    vshape = jax.ShapeDtypeStruct((1, LRU_W), F32)
    return pl.pallas_call(
        body, name="lru_bwd", grid=(LRU_W // 128, nt),
        in_specs=[pl.BlockSpec((tt, 128), lambda j, t: (rt(t), M_LRU // 128 + j)), blk,
                  pl.BlockSpec((tt, 128), lambda j, t: (rt(t), C_GR // 128 + j)), blk,
                  pl.BlockSpec((8, 128), lambda j, t: (jnp.maximum(rt(t) * h8 - 1, 0), j)),
                  mat, vec, mat, vec, vec, pl.BlockSpec(memory_space=pl.ANY)],
        out_specs=[blk, pl.BlockSpec((tt, 128), lambda j, t: (rt(t), C_GR // 128 + j)), mat, mat, vec, vec, vec],
        out_shape=[jax.ShapeDtypeStruct((T, LRU_W), F32), jax.ShapeDtypeStruct(dproj.shape, dproj.dtype),
                   mshape, mshape, vshape, vshape, vshape],
        scratch_shapes=[pltpu.VMEM((8, 128), F32), pltpu.VMEM((8, 128), F32)], input_output_aliases={10: 1},
        compiler_params=_cp(("parallel", "arbitrary")))(dmixed, xc, proj, hst, hst, wa, ba, wx, bx, lam, dproj)


FFN_CB = 512
FFN_K = 3


def _ffn_conv(ge, w_ref):
    acc = ge * w_ref[FFN_K - 1:FFN_K, :]
    for j in range(FFN_K - 1):
        acc = acc + pltpu.roll(ge, FFN_K - 1 - j, 0) * w_ref[j:j + 1, :]
    return acc


FFN_HALO = 16


def _ffn_gate_fwd(up, w):
    T = up.shape[0]
    tt = _tile(T)
    cb = FFN_CB
    nc = D_FF // cb
    hh = tt // FFN_HALO

    def body(g_ref, gp_ref, v_ref, w_ref, o_ref, ot_ref):
        t = pl.program_id(1)
        prev = jnp.where(t == 0, 0.0, gp_ref[...].astype(F32))
        ge = jnp.concatenate([prev, g_ref[...].astype(F32)], axis=0)
        gc = _ffn_conv(ge, w_ref)[FFN_HALO:, :]
        a = _gelu(gc) * v_ref[...].astype(F32)
        o_ref[...] = a.astype(BF16)
        ot_ref[...] = a.T.astype(BF16)

    return pl.pallas_call(
        body, name="ffn_gate_fwd", grid=(nc, T // tt),
        in_specs=[pl.BlockSpec((tt, cb), lambda j, t: (t, j)),
                  pl.BlockSpec((FFN_HALO, cb), lambda j, t: (jnp.maximum(t * hh - 1, 0), j)),
                  pl.BlockSpec((tt, cb), lambda j, t: (t, nc + j)),
                  pl.BlockSpec((FFN_K, cb), lambda j, t: (0, j))],
        out_specs=[pl.BlockSpec((tt, cb), lambda j, t: (t, j)), pl.BlockSpec((cb, tt), lambda j, t: (j, t))],
        out_shape=[jax.ShapeDtypeStruct((T, D_FF), BF16), jax.ShapeDtypeStruct((D_FF, T), BF16)],
        compiler_params=_cp(("parallel", "parallel")))(up, up, up, w)


def _ffn_gate_bwd(dact, up, w):
    T = up.shape[0]
    tt = _tile(T)
    cb = FFN_CB
    nc = D_FF // cb
    nt = T // tt
    hh = tt // FFN_HALO
    H = FFN_HALO

    def body(d_ref, dn_ref, g_ref, gp_ref, gn_ref, v_ref, vn_ref, w_ref, dup_ref, dw_ref):
        t = pl.program_id(1)
        prev = jnp.where(t == 0, 0.0, gp_ref[...].astype(F32))
        ge = jnp.concatenate([prev, g_ref[...].astype(F32), gn_ref[...].astype(F32)], axis=0)
        gc = _ffn_conv(ge, w_ref)[H:, :]
        gg, dgg = _gelu_and_grad(gc)
        de = jnp.concatenate([d_ref[...].astype(F32), jnp.where(t == nt - 1, 0.0, dn_ref[...].astype(F32))], axis=0)
        ve = jnp.concatenate([v_ref[...].astype(F32), vn_ref[...].astype(F32)], axis=0)
        dup_ref[1] = (de * gg)[:tt, :].astype(BF16)
        dgc = de * ve * dgg
        n = tt + H
        acc = dgc * w_ref[FFN_K - 1:FFN_K, :]
        for j in range(FFN_K - 1):
            acc = acc + pltpu.roll(dgc, n - (FFN_K - 1 - j), 0) * w_ref[j:j + 1, :]
        dup_ref[0] = acc[:tt, :].astype(BF16)

        @pl.when(t == 0)
        def _():
            dw_ref[...] = jnp.zeros_like(dw_ref)

        dgm = dgc[:tt, :]
        for j in range(FFN_K):
            sh = FFN_K - 1 - j
            xs = ge[H:H + tt, :] if sh == 0 else pltpu.roll(ge, sh, 0)[H:H + tt, :]
            dw_ref[j:j + 1, :] += jnp.sum(dgm * xs, axis=0, keepdims=True)

    def nxt(t):
        return jnp.minimum((t + 1) * hh, T // H - 1)

    return pl.pallas_call(
        body, name="ffn_gate_bwd", grid=(nc, nt),
        in_specs=[pl.BlockSpec((tt, cb), lambda j, t: (t, j)),
                  pl.BlockSpec((H, cb), lambda j, t: (nxt(t), j)),
                  pl.BlockSpec((tt, cb), lambda j, t: (t, j)),
                  pl.BlockSpec((H, cb), lambda j, t: (jnp.maximum(t * hh - 1, 0), j)),
                  pl.BlockSpec((H, cb), lambda j, t: (nxt(t), j)),
                  pl.BlockSpec((tt, cb), lambda j, t: (t, nc + j)),
                  pl.BlockSpec((H, cb), lambda j, t: (nxt(t), nc + j)),
                  pl.BlockSpec((FFN_K, cb), lambda j, t: (0, j))],
        out_specs=[pl.BlockSpec((2, tt, cb), lambda j, t: (0, t, j)),
                   pl.BlockSpec((FFN_K, cb), lambda j, t: (0, j))],
        out_shape=[jax.ShapeDtypeStruct((2, T, D_FF), BF16), jax.ShapeDtypeStruct((FFN_K, D_FF), F32)],
        compiler_params=_cp(("parallel", "arbitrary")))(dact, dact, up, up, up, up, up, w)


def _row_tile(rows, cap):
    best = 8
    for r in range(8, min(rows, cap) + 1, 8):
        if rows % r == 0:
            best = r
    return best


def _adamw(parts, w, m, v, rt, name, layer=None, prev=None):
    P, R, C = parts.shape

    def body(p_ref, w_ref, m_ref, v_ref, *rest):
        g_ref, d_ref, mo_ref, vo_ref = rest[-4:]
        g = p_ref[0].astype(F32)
        for i in range(1, P):
            g = g + p_ref[i].astype(F32)
        wv = w_ref[...]
        mn = ADAM_B1 * m_ref[...] + (1.0 - ADAM_B1) * g
        vn = ADAM_B2 * v_ref[...] + (1.0 - ADAM_B2) * (g * g)
        m_hat = mn / (1.0 - ADAM_B1 ** ADAM_STEP)
        v_hat = vn / (1.0 - ADAM_B2 ** ADAM_STEP)
        g_ref[...] = g
        d_ref[...] = -ADAM_LR * (m_hat / (jnp.sqrt(v_hat) + ADAM_EPS) + ADAM_WD * wv)
        mo_ref[...] = mn
        vo_ref[...] = vn

    if layer is None:
        blk = pl.BlockSpec((rt, C), lambda r: (r, 0))
        sh = jax.ShapeDtypeStruct((R, C), F32)
    else:
        blk = pl.BlockSpec((None, rt, C), lambda r: (layer, r, 0))
        sh = jax.ShapeDtypeStruct(w.shape, F32)
    extra = list(prev) if prev is not None else []
    return pl.pallas_call(
        body, name=name, grid=(R // rt,),
        in_specs=[pl.BlockSpec((P, rt, C), lambda r: (0, r, 0)), blk, blk, blk]
        + [pl.BlockSpec(memory_space=pl.ANY)] * len(extra),
        out_specs=[blk, blk, blk, blk], out_shape=[sh, sh, sh, sh],
        input_output_aliases={4 + i: i for i in range(len(extra))},
        compiler_params=_cp(("parallel",)))(parts, w, m, v, *extra)


def _peer(k):
    x, y, c = lax.axis_index("x"), lax.axis_index("y"), lax.axis_index("c")
    px = 1 - x if k & 4 else x
    py = 1 - y if k & 2 else y
    pc = 1 - c if k & 1 else c
    return (px, py, pc), 4 * px + 2 * py + pc


def _all_gather(x, name):
    R, C = x.shape

    def body(x_ref, o_ref, send_sems, recv_sems, local_sem):
        me = 4 * lax.axis_index("x") + 2 * lax.axis_index("y") + lax.axis_index("c")
        mine = pltpu.make_async_copy(x_ref, o_ref.at[me], local_sem)
        mine.start()
        sends = []
        for k in range(1, N_DEV):
            dev, _ = _peer(k)
            cp = pltpu.make_async_remote_copy(src_ref=x_ref, dst_ref=o_ref.at[me], send_sem=send_sems.at[k - 1],
                                              recv_sem=recv_sems.at[k - 1], device_id=dev, device_id_type=MESH_IDS)
            cp.start()
            sends.append(cp)
        for k in range(1, N_DEV):
            dev, idx = _peer(k)
            pltpu.make_async_remote_copy(src_ref=x_ref, dst_ref=o_ref.at[idx], send_sem=send_sems.at[k - 1],
                                         recv_sem=recv_sems.at[k - 1], device_id=dev,
                                         device_id_type=MESH_IDS).wait_recv()
        for cp in sends:
            cp.wait_send()
        mine.wait()

    return pl.pallas_call(
        body, name=name, in_specs=[pl.BlockSpec(memory_space=pl.ANY)], out_specs=pl.BlockSpec(memory_space=pl.ANY),
        out_shape=jax.ShapeDtypeStruct((N_DEV, R, C), x.dtype),
        scratch_shapes=[pltpu.SemaphoreType.DMA((N_DEV - 1,)), pltpu.SemaphoreType.DMA((N_DEV - 1,)),
                        pltpu.SemaphoreType.DMA],
        compiler_params=pltpu.CompilerParams(has_side_effects=True))(x)


HBM_SPEC = pl.BlockSpec(memory_space=pltpu.HBM)
SEM_SPEC = pl.BlockSpec(memory_space=pltpu.SEMAPHORE)
EFFECT = pltpu.SideEffectType.DATAFLOW_SIDE_EFFECTING
OTHER_CHIPS = ((1, 0), (0, 1), (1, 1))


def _split_start(bufs, plan, n, name):
    nb = len(bufs)

    def body(*refs):
        send_sems, recv_sems, token = refs[nb], refs[nb + 1], refs[2 * nb + 2]
        for i, (src, dst, _, dev) in enumerate(plan(refs[:nb])):
            pltpu.make_async_remote_copy(src_ref=src, dst_ref=dst, send_sem=send_sems.at[i],
                                         recv_sem=recv_sems.at[i], device_id=dev, device_id_type=MESH_IDS).start()
        token[...] = jnp.zeros_like(token)

    outs = pl.pallas_call(
        body, name=name,
        out_shape=(pltpu.SemaphoreType.DMA((n,)), pltpu.SemaphoreType.DMA((n,)),
                   *[pltpu.HBM(b.shape, b.dtype) for b in bufs], jax.ShapeDtypeStruct((8, 128), F32)),
        in_specs=[HBM_SPEC] * nb,
        out_specs=(SEM_SPEC, SEM_SPEC, *[HBM_SPEC] * nb, pl.BlockSpec(memory_space=pltpu.VMEM)),
        input_output_aliases={i: 2 + i for i in range(nb)},
        compiler_params=pltpu.CompilerParams(has_side_effects=EFFECT),
    )(*[pltpu.with_memory_space_constraint(b, pltpu.HBM) for b in bufs])
    return dict(send=outs[0], recv=outs[1], bufs=list(outs[2:2 + nb]), token=outs[2 + nb], plan=plan, n=n)


def _split_wait(st, after, name):
    bufs = st["bufs"]
    nb = len(bufs)
    plan = st["plan"]
    afters = list(after) if isinstance(after, (list, tuple)) else [after]

    def body(*refs):
        send_sems, recv_sems = refs[nb], refs[nb + 1]
        for i, (src, dst, land, dev) in enumerate(plan(refs[:nb])):
            pltpu.make_async_remote_copy(src_ref=src, dst_ref=dst, send_sem=send_sems.at[i],
                                         recv_sem=recv_sems.at[i], device_id=dev,
                                         device_id_type=MESH_IDS).wait_send()
            pltpu.make_async_remote_copy(src_ref=src, dst_ref=land, send_sem=send_sems.at[i],
                                         recv_sem=recv_sems.at[i], device_id=dev,
                                         device_id_type=MESH_IDS).wait_recv()

    outs = pl.pallas_call(
        body, name=name, out_shape=tuple(pltpu.HBM(b.shape, b.dtype) for b in bufs),
        in_specs=[HBM_SPEC] * nb + [SEM_SPEC, SEM_SPEC] + [pl.BlockSpec(memory_space=pl.ANY)] * len(afters),
        out_specs=tuple([HBM_SPEC] * nb), input_output_aliases={i: i for i in range(nb)},
        compiler_params=pltpu.CompilerParams(has_side_effects=EFFECT),
    )(*bufs, st["send"], st["recv"], *afters)
    return list(outs)


def _xyc():
    return lax.axis_index("x"), lax.axis_index("y"), lax.axis_index("c")


def _flip(x, y, a, b):
    return (1 - x if a else x), (1 - y if b else y)


def _ag1_plan(outs):
    x, y, c = _xyc()
    me = 4 * x + 2 * y + c
    copies = []
    for o in outs:
        copies.append((o.at[me], o.at[me], o.at[4 * x + 2 * y + 1 - c], (x, y, 1 - c)))
        for a, b in OTHER_CHIPS:
            px, py = _flip(x, y, a, b)
            copies.append((o.at[me], o.at[me], o.at[4 * px + 2 * py + c], (px, py, c)))
    return copies


def _ag2_plan(outs):
    x, y, c = _xyc()
    copies = []
    for o in outs:
        for a, b in OTHER_CHIPS:
            px, py = _flip(x, y, a, b)
            mine, sibs = 4 * px + 2 * py + c, 4 * px + 2 * py + 1 - c
            copies.append((o.at[mine], o.at[mine], o.at[sibs], (x, y, 1 - c)))
    return copies


def _rs1_plan(refs):
    x, y, c = _xyc()
    copies = []
    for g, land in zip(refs[0::2], refs[1::2]):
        for j in range(4):
            copies.append((g.at[2 * j + 1 - c], land.at[j], land.at[j], (x, y, 1 - c)))
    return copies


def _rs2_plan(refs):
    x, y, c = _xyc()
    mychip = 2 * x + y
    copies = []
    for s, land in zip(refs[0::2], refs[1::2]):
        for a, b in OTHER_CHIPS:
            px, py = _flip(x, y, a, b)
            copies.append((s.at[2 * px + py], land.at[mychip], land.at[2 * px + py], (px, py, c)))
    return copies


def _landing_like(g, name):
    def body(g_ref, o_ref):
        del g_ref, o_ref

    anyspec = pl.BlockSpec(memory_space=pl.ANY)
    return pl.pallas_call(body, name=name, in_specs=[anyspec], out_specs=anyspec,
                          out_shape=jax.ShapeDtypeStruct((4,) + g.shape[1:], g.dtype))(g)


def _place(x, slots, by_chip, name):
    R, C = x.shape[-2:]
    rt = _row_tile(R, 512)
    xi, yi, ci = _xyc()
    idx = (2 * xi + yi if by_chip else 4 * xi + 2 * yi + ci).astype(jnp.int32).reshape(1)

    def body(i_ref, x_ref, o_ref):
        o_ref[...] = x_ref[...]

    if by_chip:
        in_spec = pl.BlockSpec((None, rt, C), lambda r, i: (i[0], r, 0))
    else:
        in_spec = pl.BlockSpec((rt, C), lambda r, i: (r, 0))
    grid_spec = pltpu.PrefetchScalarGridSpec(
        num_scalar_prefetch=1, grid=(R // rt,), in_specs=[in_spec],
        out_specs=pl.BlockSpec((None, rt, C), lambda r, i: (i[0], r, 0)))
    return pl.pallas_call(body, name=name, grid_spec=grid_spec,
                          out_shape=jax.ShapeDtypeStruct((slots, R, C), x.dtype),
                          compiler_params=_cp(("parallel",)))(idx, x)


def _pair_sum(g, land, cidx, name):
    _, R, C = land.shape
    rt = _row_tile(R, 512)
    g4 = g.reshape(4, 2, R, C)

    def body(c_ref, g_ref, l_ref, o_ref):
        o_ref[...] = (g_ref[...].astype(F32) + l_ref[...].astype(F32)).astype(o_ref.dtype)

    grid_spec = pltpu.PrefetchScalarGridSpec(
        num_scalar_prefetch=1, grid=(4, R // rt),
        in_specs=[pl.BlockSpec((None, None, rt, C), lambda j, r, c_ref: (j, c_ref[0], r, 0)),
                  pl.BlockSpec((None, rt, C), lambda j, r, c_ref: (j, r, 0))],
        out_specs=pl.BlockSpec((None, rt, C), lambda j, r, c_ref: (j, r, 0)))
    return pl.pallas_call(body, name=name, grid_spec=grid_spec, out_shape=jax.ShapeDtypeStruct(land.shape, land.dtype),
                          compiler_params=_cp(("parallel", "parallel")))(cidx, g4, land)


def _no_hook(event, l, after, payload=None):
    return None


def _tie(x, token):
    if token is None:
        return x

    def body(x_ref, t_ref, o_ref):
        del x_ref, t_ref, o_ref

    anyspec = pl.BlockSpec(memory_space=pl.ANY)
    return pl.pallas_call(body, name="tie", in_specs=[anyspec, anyspec], out_specs=anyspec,
                          out_shape=jax.ShapeDtypeStruct(x.shape, x.dtype), input_output_aliases={0: 0})(x, token)


def _layer_fwd(x, W, l, hook=_no_hook):
    T = x.shape[0]
    n = f"l{l}_"
    h1, h1t = _norm_fwd(x, W["norm1"], n + "norm1_fwd")
    proj = _mm_nn(h1, W["win"], F32, n + "mm_in", tn_c=(768,))
    y_pool = _pool_fwd(proj, W["pool_w"], W["pool_b"], W["pool_s"])
    cpre = _conv_fwd(proj, C_QKV, 3 * GDN_W, W["gconv_w"], None, 256, n + "gdn_conv_fwd")
    qkv, bb, gb = _gdn_pre_fwd(cpre, proj, W["alog"], W["dtb"])
    mixed, states, tms = _gdn_fwd(qkv, bb, gb, proj, W["gnorm"], y_pool)
    lconv_w = _tie(W["lconv_w"], hook("f_mix", l, states))
    xc = _conv_fwd(proj, C_XR, LRU_W, lconv_w, W["lconv_b"], 128, n + "lru_conv_fwd")
    mixed, hst = _lru_fwd(xc, proj, W["wa"], W["ba"], W["wx"], W["bx"], W["lam"], mixed)
    hook("f_out", l, hst)
    x1 = _mm_nn(mixed, W["wout"], F32, n + "mm_out", add=x)
    h2, h2t = _norm_fwd(x1, W["norm2"], n + "norm2_fwd")
    up = _mm_up(h2, W["wup"], n + "mm_up")
    act, act_t = _ffn_gate_fwd(up, W["fconv_w"])
    act = _tie(act, hook("f_act", l, act_t))
    x2 = _mm_nn(act, W["wdown"], F32, n + "mm_down", add=x1)
    hook("f_end", l, x2)
    saved = dict(x=x, h1t=h1t, proj=proj, cpre=cpre, qkv=qkv, bb=bb, gb=gb, states=states, tms=tms, xc=xc, hst=hst,
                 mixed=mixed, x1=x1, h2t=h2t, up=up, act_t=act_t)
    return x2, saved


def _layer_bwd(dx2, dx2b, W, S, l, hook=_no_hook):
    T = dx2.shape[0]
    n = f"l{l}_"
    dact = _mm_nt(dx2b, W["wdown"], BF16, n + "mm_down_dx", tk_c=(2048,))
    g_wdown = _mm_nn(S["act_t"], dx2b, BF16, n + "mm_down_dw")
    dup, g_fconv = _ffn_gate_bwd(dact, S["up"], W["fconv_w"])
    ns = W["wup"].shape[2]
    dh2 = _mm_up_t(dup, W["wup"], n + "mm_up_dx")
    g_wup = _mm_dup(S["h2t"], dup, ns, n + "mm_up_dw")
    tok = hook("b_ffn", l, g_wup, dict(ffn_down=g_wdown, ffn_up=g_wup))
    dx1, dx1b, g_norm2 = _norm_bwd(S["x1"], _tie(W["norm2"], tok), dh2, dx2, n + "norm2_bwd")
    dmixed = _mm_nt(dx1b, W["wout"], F32, n + "mm_out_dx", tk_c=(2048,))
    g_wout = _mm_tn(S["mixed"], dx1b, BF16, n + "mm_out_dw")
    tok = hook("b_mid", l, g_wout)
    proj = S["proj"]
    dproj, g_pool_w, g_pool_b, g_pool_s = _pool_bwd(dmixed, proj, W["pool_w"], W["pool_b"], _tie(W["pool_s"], tok))
    dq, dk, dv, dbb, dgb, dproj, g_gnorm = _gdn_bwd(S["qkv"], S["bb"], S["gb"], proj, W["gnorm"], S["states"],
                                                    S["tms"], dmixed, dproj)
    dc, dproj, g_alog, g_dtb = _gdn_pre_bwd(dq, dk, dv, S["cpre"], dbb, dgb, proj, W["alog"], W["dtb"], dproj)
    dproj, g_gconv = _conv_bwd(dc, proj, C_QKV, W["gconv_w"], 256, n + "gdn_conv_bwd", dproj)
    dxc, dproj, g_wa, g_wx, g_ba, g_bx, g_lam = _lru_bwd(dmixed, S["xc"], proj, S["hst"], W["wa"], W["ba"], W["wx"],
                                                          W["bx"], W["lam"], dproj)
    dproj, g_lconv, g_lconv_b = _conv_bwd(dxc, proj, C_XR, W["lconv_w"], 128, n + "lru_conv_bwd", dproj,
                                          want_db=True)
    dh1 = _mm_nt(dproj, W["win"], F32, n + "mm_in_dx", tk_c=(1792,))
    g_win = _mm_nn(S["h1t"], dproj, BF16, n + "mm_in_dw", tn_c=(768,))
    tok = hook("b_in", l, g_win, dict(w_out=g_wout, w_in=g_win))
    dx, dxb, g_norm1 = _norm_bwd(S["x"], _tie(W["norm1"], tok), dh1, dx1, n + "norm1_bwd")
    big = dict(w_in=g_win, w_out=g_wout, ffn_up=g_wup, ffn_down=g_wdown)
    small = dict(norm1_w=g_norm1[0], pool_w=g_pool_w, pool_b=g_pool_b.reshape(4, 128), pool_scale=g_pool_s[0],
                 gdn_conv_w=g_gconv, gdn_a_log=g_alog[0, :HEADS], gdn_dt_bias=g_dtb[0, :HEADS],
                 gdn_norm_w=g_gnorm[0], lru_conv_w=g_lconv, lru_conv_b=g_lconv_b[0], lru_wa=g_wa, lru_ba=g_ba[0],
                 lru_wx=g_wx, lru_bx=g_bx[0], lru_lambda=g_lam[0], norm2_w=g_norm2[0], ffn_conv_w=g_fconv)
    dxb = _tie(dxb, hook("b_end", l, dx, small))
    return dx, dxb, big, small


def _pad_lane(v):
    return jnp.pad(v, (0, 128 - v.shape[0])).reshape(1, 128)


def _layer_weights(l, big, P, conv_full):
    return dict(
        win=big.get("w_in"), wout=big.get("w_out"), wup=big.get("ffn_up"), wdown=big.get("ffn_down"),
        norm1=P["norm1_w"][l].reshape(1, D_MODEL), norm2=P["norm2_w"][l].reshape(1, D_MODEL),
        pool_w=P["pool_w"][l], pool_b=P["pool_b"][l].reshape(1, POOL_W), pool_s=P["pool_scale"][l].reshape(1, POOL_W),
        gconv_w=conv_full["gdn_conv_w"][l], alog=_pad_lane(P["gdn_a_log"][l]), dtb=_pad_lane(P["gdn_dt_bias"][l]),
        gnorm=P["gdn_norm_w"][l].reshape(1, HD),
        lconv_w=conv_full["lru_conv_w"][l], lconv_b=P["lru_conv_b"][l].reshape(1, LRU_W),
        wa=P["lru_wa"][l], ba=P["lru_ba"][l].reshape(1, LRU_W), wx=P["lru_wx"][l],
        bx=P["lru_bx"][l].reshape(1, LRU_W), lam=P["lru_lambda"][l].reshape(1, LRU_W),
        fconv_w=conv_full["ffn_conv_w"][l])


def _local_step(x, target, Ws, final_norm_w, hook=_no_hook):
    saved = []
    for l in range(DEPTH):
        x, s = _layer_fwd(x, Ws[l], l, hook)
        saved.append(s)
    loss, dx, dxb, g_final = _loss_head(x, final_norm_w.reshape(1, D_MODEL), target)
    bigs, smalls = [None] * DEPTH, [None] * DEPTH
    for l in reversed(range(DEPTH)):
        dx, dxb, bigs[l], smalls[l] = _layer_bwd(dx, dxb, Ws[l], saved[l], l, hook)
    return loss, dx, g_final[0], bigs, smalls


SMALL_REPL = ("norm1_w", "pool_w", "pool_b", "pool_scale", "gdn_a_log", "gdn_dt_bias", "gdn_norm_w", "lru_conv_b",
              "lru_wa", "lru_ba", "lru_wx", "lru_bx", "lru_lambda", "norm2_w", "final_norm_w")
SMALL_SHARD = ("gdn_conv_w", "lru_conv_w", "ffn_conv_w")
BIG = ("w_in", "w_out", "ffn_up", "ffn_down")
WEIGHTS = ("norm1_w", "w_in", "pool_w", "pool_b", "pool_scale", "gdn_conv_w", "gdn_a_log", "gdn_dt_bias",
           "gdn_norm_w", "lru_conv_w", "lru_conv_b", "lru_wa", "lru_ba", "lru_wx", "lru_bx", "lru_lambda", "w_out",
           "norm2_w", "ffn_up", "ffn_conv_w", "ffn_down", "final_norm_w")
SEG = 1024
PACK_ROWS_MULT = 256 * 128


def _pack(arrs):
    pieces, table, off = [], [], 0
    for a in arrs:
        n = a.size
        npad = -(-n // SEG) * SEG
        pieces.append(jnp.pad(a.reshape(-1).astype(F32), (0, npad - n)))
        table.append((off, n, a.shape))
        off += npad
    tail = -off % PACK_ROWS_MULT
    if tail:
        pieces.append(jnp.zeros((tail,), F32))
        off += tail
    return jnp.concatenate(pieces).reshape(off // 128, 128), table


def _unpack(buf, table):
    flat = buf.reshape(-1)
    return [flat[off:off + n].reshape(shape) for off, n, shape in table]


def _pad_in(w):
    z1 = jnp.zeros(w.shape[:-1] + (C_XR - AB_ORIG_END,), w.dtype)
    return jnp.concatenate([w[..., :AB_ORIG_END], z1, w[..., AB_ORIG_END:]], axis=-1)


def _unpad_in(w):
    return jnp.concatenate([w[..., :AB_ORIG_END], w[..., C_XR:C_GR + LRU_W]], axis=-1)


def kernel(x, norm1_w, w_in, pool_w, pool_b, pool_scale, gdn_conv_w, gdn_a_log, gdn_dt_bias, gdn_norm_w, lru_conv_w, lru_conv_b, lru_wa, lru_ba, lru_wx, lru_bx, lru_lambda, w_out, norm2_w, ffn_up, ffn_conv_w, ffn_down, final_norm_w, loss_target, m_norm1_w, m_w_in, m_pool_w, m_pool_b, m_pool_scale, m_gdn_conv_w, m_gdn_a_log, m_gdn_dt_bias, m_gdn_norm_w, m_lru_conv_w, m_lru_conv_b, m_lru_wa, m_lru_ba, m_lru_wx, m_lru_bx, m_lru_lambda, m_w_out, m_norm2_w, m_ffn_up, m_ffn_conv_w, m_ffn_down, m_final_norm_w, v_norm1_w, v_w_in, v_pool_w, v_pool_b, v_pool_scale, v_gdn_conv_w, v_gdn_a_log, v_gdn_dt_bias, v_gdn_norm_w, v_lru_conv_w, v_lru_conv_b, v_lru_wa, v_lru_ba, v_lru_wx, v_lru_bx, v_lru_lambda, v_w_out, v_norm2_w, v_ffn_up, v_ffn_conv_w, v_ffn_down, v_final_norm_w):
    loc = dict(locals())
    Wp = {n: loc[n] for n in WEIGHTS}
    Mp = {n: loc["m_" + n] for n in WEIGHTS}
    Vp = {n: loc["v_" + n] for n in WEIGHTS}
    xi, yi, ci = _xyc()
    me = 4 * xi + 2 * yi + ci
    mychip = 2 * xi + yi
    cidx = ci.astype(jnp.int32).reshape(1)
    keys = dict(w_in="win", w_out="wout", ffn_up="wup", ffn_down="wdown")

    def shard2d(d, name, l):
        a = d[name][l]
        return _pad_in(a) if name == "w_in" else a

    def wshard(l, name):
        return shard2d(Wp, name, l).astype(BF16)

    def full2d(name, full):
        return full if name == "ffn_up" else full.reshape(-1, full.shape[2])

    def ag_start(shards, tag, token=None):
        if token is not None:
            shards = [_tie(shards[0], token)] + list(shards[1:])
        bufs = [_place(s, N_DEV, False, f"place_{tag}{i}") for i, s in enumerate(shards)]
        return _split_start(bufs, _ag1_plan, 4 * len(bufs), f"ag1s_{tag}")

    def ag_mid(st, after, tag):
        bufs = _split_wait(st, after, f"ag1w_{tag}")
        return _split_start(bufs, _ag2_plan, 3 * len(bufs), f"ag2s_{tag}")

    def ag_end(st, after, tag):
        return _split_wait(st, after, f"ag2w_{tag}")

    def rs_start(gs, tag):
        bufs = []
        for nm, g in gs.items():
            if nm != "ffn_up":
                g = g.reshape(N_DEV, g.shape[0] // N_DEV, g.shape[1])
            bufs += [g, _landing_like(g, f"land_{nm}_{tag}")]
        st = _split_start(bufs, _rs1_plan, 4 * len(gs), f"rs1s_{tag}")
        st["names"] = list(gs)
        return st

    def rs_mid(st, after, tag):
        bufs = _split_wait(st, after, f"rs1w_{tag}")
        out = []
        for i, nm in enumerate(st["names"]):
            s = _pair_sum(bufs[2 * i], bufs[2 * i + 1], cidx, f"pairsum_{nm}_{tag}")
            out += [s, _place(s, 4, True, f"place_{nm}_{tag}")]
        st2 = _split_start(out, _rs2_plan, 3 * len(st["names"]), f"rs2s_{tag}")
        st2["names"] = st["names"]
        return st2

    def rs_end(st, after, tag):
        bufs = _split_wait(st, after, f"rs2w_{tag}")
        return dict(zip(st["names"], bufs[1::2]))

    lnames = tuple(n for n in SMALL_REPL if n != "final_norm_w") + SMALL_SHARD

    def small_pack(l, gs, extra):
        return _pack([gs[nm] for nm in lnames] + extra)

    def small_state(d, l, gs):
        arrs = [d[nm][l] if nm in SMALL_REPL else jnp.zeros(gs[nm].shape, F32) for nm in lnames]
        if l == 0:
            arrs += [d["final_norm_w"], jnp.zeros((1,), F32)]
        return _pack(arrs)[0]

    stA = ag_start([wshard(0, "w_in")], "a")
    stA2 = ag_mid(stA, stA["token"], "a")
    stB = ag_start([wshard(0, n) for n in BIG[1:]], "b", stA2["token"])
    (w_in0,) = ag_end(stA2, stB["token"], "a")

    cbuf, ctable = _pack([Wp[n] for n in SMALL_SHARD])
    call = _all_gather(cbuf, "ag_conv_w")
    parts = [_unpack(call[i], ctable) for i in range(N_DEV)]
    conv_full = {n: jnp.concatenate([parts[i][j] for i in range(N_DEV)], axis=-1) for j, n in enumerate(SMALL_SHARD)}

    Ws = [_layer_weights(l, {}, Wp, conv_full) for l in range(DEPTH)]
    Ws[0]["win"] = full2d("w_in", w_in0)
    st = {}

    def hook(event, l, after, payload=None):
        if event == "f_mix" and l == 0:
            st["b2"] = ag_mid(stB, after, "b")
            st["c"] = ag_start([wshard(1, n) for n in BIG], "c", st["b2"]["token"])
            return st["c"]["token"]
        if event == "f_out" and l == 0:
            for n, b in zip(BIG[1:], ag_end(st["b2"], after, "b")):
                Ws[0][keys[n]] = full2d(n, b)
        if event == "f_act" and l == 0:
            st["c2"] = ag_mid(st["c"], after, "c")
            return st["c2"]["token"]
        if event == "f_end" and l == 0:
            for n, b in zip(BIG, ag_end(st["c2"], after, "c")):
                Ws[1][keys[n]] = full2d(n, b)
        if event == "b_ffn":
            st["ffn", l] = rs_start(payload, f"ffn{l}")
            return st["ffn", l]["token"]
        if event == "b_mid":
            st["ffn2", l] = rs_mid(st["ffn", l], after, f"ffn{l}")
            if l == 0:
                st["sm1b"] = ag_mid(st["sm1"], st["ffn2", l]["token"], "sm1")
                return st["sm1b"]["token"]
            return st["ffn2", l]["token"]
        if event == "b_in":
            st["io", l] = rs_start(payload, f"io{l}")
            if l == 0:
                st["sm1g"] = ag_end(st["sm1b"], st["io", l]["token"], "sm1")[0]
            return st["io", l]["token"]
        if event == "b_end" and l == 1:
            st["io2", 1] = rs_mid(st["io", 1], after, "io1")
            gbuf1, st["table1"] = small_pack(1, payload, [])
            st["sm1"] = ag_start([gbuf1], "sm1", st["io2", 1]["token"])
            return st["sm1"]["token"]
        return None

    loss, dx, g_final, _, gsmall = _local_step(x[0], loss_target[0], Ws, final_norm_w, hook)

    out_g, out_d, out_m, out_v = {}, {}, {}, {}
    outs4 = (out_g, out_d, out_m, out_v)
    rts = dict(w_in=64, w_out=128, ffn_up=256, ffn_down=128)
    big_res = {}

    def stacked(d, name):
        return _pad_in(d[name]) if name == "w_in" else d[name]

    wmv = {name: [stacked(d, name) for d in (Wp, Mp, Vp)] for name in BIG}

    def adam_big(l, parts):
        for name, p in parts.items():
            big_res[name] = _adamw(p, *wmv[name], rts[name], f"adamw_{name}_{l}", layer=l, prev=big_res.get(name))
        return [big_res[name][0] for name in parts]

    def adam_small(l, gall, gs):
        rs = gall.shape[1]
        return _adamw(gall, small_state(Wp, l, gs), small_state(Mp, l, gs), small_state(Vp, l, gs),
                      _row_tile(rs, 512), f"adamw_small_{l}")

    gbuf0, table0 = small_pack(0, gsmall[0], [g_final, loss[0, :1]])
    sm0 = ag_start([gbuf0], "sm0", st["io", 0]["token"])
    o = adam_big(1, rs_end(st["ffn2", 1], sm0["token"], "ffn1"))
    st["io2", 0] = rs_mid(st["io", 0], o, "io0")
    o = adam_big(1, rs_end(st["io2", 1], st["io2", 0]["token"], "io1"))
    o = adam_big(0, rs_end(st["ffn2", 0], o, "ffn0"))
    small_res = {1: adam_small(1, _tie(st["sm1g"], o[-1]), gsmall[1])}
    sm0b = ag_mid(sm0, o + [small_res[1][0]], "sm0")
    small_res[0] = adam_small(0, ag_end(sm0b, sm0b["token"], "sm0")[0], gsmall[0])
    adam_big(0, rs_end(st["io2", 0], small_res[0][0], "io0"))

    for name in BIG:
        for i, dst in enumerate(outs4):
            dst[name] = _unpad_in(big_res[name][i]) if name == "w_in" else big_res[name][i]

    unp = {0: [_unpack(r, table0) for r in small_res[0]], 1: [_unpack(r, st["table1"]) for r in small_res[1]]}
    for j, nm in enumerate(lnames):
        if nm in SMALL_REPL:
            for i, dst in enumerate(outs4):
                dst[nm] = jnp.stack([unp[l][i][j] for l in range(DEPTH)])
    for i, dst in enumerate(outs4):
        dst["final_norm_w"] = unp[0][i][len(lnames)]
    loss_total = unp[0][0][len(lnames) + 1][0]

    gsh = []
    for nm in SMALL_SHARD:
        j = lnames.index(nm)
        width = Wp[nm].shape[-1]
        gsh.append(jnp.stack([lax.dynamic_slice_in_dim(unp[l][0][j], me * width, width, axis=1)
                              for l in range(DEPTH)]))
    sbuf, stable = _pack(gsh)
    res = _adamw(sbuf[None], _pack([Wp[n] for n in SMALL_SHARD])[0], _pack([Mp[n] for n in SMALL_SHARD])[0],
                 _pack([Vp[n] for n in SMALL_SHARD])[0], sbuf.shape[0], "adamw_conv_w")
    unp2 = [_unpack(r, stable) for r in res]
    for j, nm in enumerate(SMALL_SHARD):
        for i, dst in enumerate((out_g, out_d, out_m, out_v)):
            dst[nm] = unp2[i][j]

    return (loss_total, dx[None], *[out_g[n] for n in WEIGHTS], *[out_d[n] for n in WEIGHTS],
            *[out_m[n] for n in WEIGHTS], *[out_v[n] for n in WEIGHTS])
```

```python
import functools

import jax
import jax.numpy as jnp
from jax import lax
from jax.experimental import pallas as pl
from jax.experimental.pallas import tpu as pltpu

F32 = jnp.float32
BF16 = jnp.bfloat16
HI = lax.Precision.HIGHEST
MESH_IDS = pl.DeviceIdType.MESH

N_DEV = 8
D_MODEL = 2048
DEPTH = 2
POOL_WINDOWS = (2, 4, 8, 16)
POOL_W = 512
HEADS = 6
HD = 128
GDN_W = HEADS * HD
CHUNK = 64
LRU_W = 768
LRU_C = 8.0
D_FF = 3 * D_MODEL
EPS = 1e-6
IN_COLS = 5132
PCOLS = 5376
C_QKV, C_Z, C_AB, C_XR, C_GR = 512, 2816, 3584, 3840, 4608
AB_ORIG_END = 3596
M_GDN, M_LRU = 512, 1280

ADAM_LR, ADAM_B1, ADAM_B2, ADAM_EPS, ADAM_WD, ADAM_STEP = 0.001, 0.9, 0.999, 1e-08, 0.01, 10

VMEM_LIMIT = 56 * 1024 * 1024


def _cp(sem):
    return pltpu.CompilerParams(dimension_semantics=sem, vmem_limit_bytes=VMEM_LIMIT)


def _mm(a, b, ca=1, cb=0, prec=None, cast=True):
    if cast:
        a = a.astype(BF16)
        b = b.astype(BF16)
    return lax.dot_general(a, b, (((ca,), (cb,)), ((), ())), preferred_element_type=F32, precision=prec)


def _bmm(a, b, ca=2, cb=1, prec=None, cast=True):
    if cast:
        a = a.astype(BF16)
        b = b.astype(BF16)
    return lax.dot_general(a, b, (((ca,), (cb,)), ((0,), (0,))), preferred_element_type=F32, precision=prec)


def _sigmoid(x):
    return 1.0 / (1.0 + jnp.exp(-x))


def _log1p(e):
    u = 1.0 + e
    return jnp.where(u == 1.0, e, jnp.log(u) * e / jnp.where(u == 1.0, 1.0, u - 1.0))


def _softplus(x):
    return jnp.maximum(x, 0.0) + _log1p(jnp.exp(-jnp.abs(x)))


def _expm1(x):
    u = jnp.exp(x)
    um = u - 1.0
    safe = jnp.where((u == 1.0) | (um == -1.0), 1.0, jnp.log(u))
    return jnp.where(u == 1.0, x, jnp.where(um == -1.0, -1.0, um * x / safe))


_G0 = 0.7978845608028654
_G1 = 0.044715


def _gelu(x):
    return 0.5 * x * (1.0 + jnp.tanh(_G0 * (x + _G1 * x * x * x)))


def _gelu_and_grad(x):
    th = jnp.tanh(_G0 * (x + _G1 * x * x * x))
    g = 0.5 * x * (1.0 + th)
    dg = 0.5 * (1.0 + th) + 0.5 * x * (1.0 - th * th) * _G0 * (1.0 + 3.0 * _G1 * x * x)
    return g, dg


def _tile(T):
    return min(T, 512)


def _matmul(a, b, *, grid, a_spec, b_spec, out_shape, out_spec, dims, acc_shape, name, add=None, add_spec=None):
    nk = grid[2]
    has_add = add is not None

    def body(*refs):
        if has_add:
            a_ref, b_ref, add_ref, o_ref, acc_ref = refs
        else:
            a_ref, b_ref, o_ref, acc_ref = refs
            add_ref = None
        k = pl.program_id(2)
        p = lax.dot_general(a_ref[...].astype(BF16), b_ref[...].astype(BF16), (dims, ((), ())),
                            preferred_element_type=F32)

        def finish(r):
            if has_add:
                r = r + add_ref[...]
            o_ref[...] = r.astype(o_ref.dtype)

        if nk == 1:
            finish(p)
        else:
            @pl.when(k == 0)
            def _():
                acc_ref[...] = p

            @pl.when(k > 0)
            def _():
                acc_ref[...] += p

            @pl.when(k == nk - 1)
            def _():
                finish(acc_ref[...])

    in_specs = [a_spec, b_spec] + ([add_spec] if has_add else [])
    args = (a, b) + ((add,) if has_add else ())
    return pl.pallas_call(
        body, name=name, grid=grid, in_specs=in_specs, out_specs=out_spec, out_shape=out_shape,
        scratch_shapes=[pltpu.VMEM(acc_shape, F32)],
        compiler_params=_cp(("parallel", "parallel", "arbitrary")),
    )(*args)


def _pick(n, cands):
    for c in cands:
        if n % c == 0:
            return c
    raise ValueError(f"no tile for {n}")


def _mm_nn(a, b, out_dtype, name, add=None, tn_c=(1024, 768, 512)):
    M, K = a.shape
    N = b.shape[1]
    tm = _pick(M, (1024, 512, 256))
    tn = _pick(N, tn_c)
    tk = _pick(K, (2048, 1536, 1024, 512, 256))
    return _matmul(
        a, b, grid=(M // tm, N // tn, K // tk),
        a_spec=pl.BlockSpec((tm, tk), lambda i, j, k: (i, k)),
        b_spec=pl.BlockSpec((tk, tn), lambda i, j, k: (k, j)),
        out_shape=jax.ShapeDtypeStruct((M, N), out_dtype),
        out_spec=pl.BlockSpec((tm, tn), lambda i, j, k: (i, j)),
        dims=((1,), (0,)), acc_shape=(tm, tn), name=name, add=add,
        add_spec=pl.BlockSpec((tm, tn), lambda i, j, k: (i, j)))


def _mm_nt(a, b, out_dtype, name, tk_c=(2048, 1536, 1024, 768, 512)):
    M, K = a.shape
    N = b.shape[0]
    tm = _pick(M, (1024, 512, 256))
    tn = _pick(N, (1024, 768, 512))
    tk = _pick(K, tk_c)
    return _matmul(
        a, b, grid=(M // tm, N // tn, K // tk),
        a_spec=pl.BlockSpec((tm, tk), lambda i, j, k: (i, k)),
        b_spec=pl.BlockSpec((tn, tk), lambda i, j, k: (j, k)),
        out_shape=jax.ShapeDtypeStruct((M, N), out_dtype),
        out_spec=pl.BlockSpec((tm, tn), lambda i, j, k: (i, j)),
        dims=((1,), (1,)), acc_shape=(tm, tn), name=name)


def _mm_tn(a, b, out_dtype, name, tn_c=(1024, 768, 512)):
    K, M = a.shape
    N = b.shape[1]
    tm = _pick(M, (1024, 768, 512))
    tn = _pick(N, tn_c)
    tk = _pick(K, (1024, 512, 256))
    return _matmul(
        a, b, grid=(M // tm, N // tn, K // tk),
        a_spec=pl.BlockSpec((tk, tm), lambda i, j, k: (k, i)),
        b_spec=pl.BlockSpec((tk, tn), lambda i, j, k: (k, j)),
        out_shape=jax.ShapeDtypeStruct((M, N), out_dtype),
        out_spec=pl.BlockSpec((tm, tn), lambda i, j, k: (i, j)),
        dims=((0,), (0,)), acc_shape=(tm, tn), name=name)


def _mm_up(h, wup, name):
    M, K = h.shape
    ns = wup.shape[2]
    tm = _pick(M, (1024, 512, 256))
    tn = 768
    per = ns // tn
    return _matmul(
        h, wup, grid=(M // tm, N_DEV * per, 1),
        a_spec=pl.BlockSpec((tm, K), lambda i, j, k: (i, 0)),
        b_spec=pl.BlockSpec((None, K, tn), lambda i, j, k: (j // per, 0, j % per)),
        out_shape=jax.ShapeDtypeStruct((M, N_DEV * ns), BF16),
        out_spec=pl.BlockSpec((tm, tn), lambda i, j, k: (i, j)),
        dims=((1,), (0,)), acc_shape=(tm, tn), name=name)


def _mm_up_t(dup, wup, name):
    M = dup.shape[1]
    D, ns = wup.shape[1], wup.shape[2]
    tm = _pick(M, (1024, 512, 256))
    tn = 1024
    tk = ns
    return _matmul(
        dup, wup, grid=(M // tm, D // tn, N_DEV),
        a_spec=pl.BlockSpec((None, tm, tk), lambda i, j, k: (k // 4, i, k % 4)),
        b_spec=pl.BlockSpec((None, tn, tk), lambda i, j, k: (k, j, 0)),
        out_shape=jax.ShapeDtypeStruct((M, D), F32),
        out_spec=pl.BlockSpec((tm, tn), lambda i, j, k: (i, j)),
        dims=((1,), (1,)), acc_shape=(tm, tn), name=name)


def _mm_dup(ht, dup, ns, name):
    M, K = ht.shape
    tm = 1024
    tn = 768
    per = ns // tn
    half = 4 * per
    tk = _pick(K, (2048, 1024, 512, 256))
    return _matmul(
        ht, dup, grid=(M // tm, N_DEV * per, K // tk),
        a_spec=pl.BlockSpec((tm, tk), lambda i, j, k: (i, k)),
        b_spec=pl.BlockSpec((None, tk, tn), lambda i, j, k: (j // half, k, j % half)),
        out_shape=jax.ShapeDtypeStruct((N_DEV, M, ns), BF16),
        out_spec=pl.BlockSpec((None, tm, tn), lambda i, j, k: (j // per, i, j % per)),
        dims=((1,), (0,)), acc_shape=(tm, tn), name=name)


def _norm_fwd(x, w, name):
    T, D = x.shape
    tt = _tile(T)

    def body(x_ref, w_ref, h_ref, ht_ref):
        xv = x_ref[...]
        r = lax.rsqrt(jnp.mean(xv * xv, axis=1, keepdims=True) + EPS)
        hv = xv * r * w_ref[...]
        h_ref[...] = hv.astype(BF16)
        ht_ref[...] = hv.T.astype(BF16)

    return pl.pallas_call(
        body, name=name, grid=(T // tt,),
        in_specs=[pl.BlockSpec((tt, D), lambda t: (t, 0)), pl.BlockSpec((1, D), lambda t: (0, 0))],
        out_specs=[pl.BlockSpec((tt, D), lambda t: (t, 0)), pl.BlockSpec((D, tt), lambda t: (0, t))],
        out_shape=[jax.ShapeDtypeStruct((T, D), BF16), jax.ShapeDtypeStruct((D, T), BF16)],
        compiler_params=_cp(("parallel",)))(x, w)


def _norm_bwd(x, w, dh, dres, name):
    T, D = x.shape
    tt = _tile(T)

    def body(x_ref, w_ref, dh_ref, dres_ref, dx_ref, dxb_ref, dw_ref):
        t = pl.program_id(0)
        xv = x_ref[...]
        r = lax.rsqrt(jnp.mean(xv * xv, axis=1, keepdims=True) + EPS)
        xh = xv * r
        dh_v = dh_ref[...]
        dxh = dh_v * w_ref[...]
        dxv = dres_ref[...] + r * (dxh - xh * jnp.mean(dxh * xh, axis=1, keepdims=True))
        dx_ref[...] = dxv
        dxb_ref[...] = dxv.astype(BF16)
        part = jnp.sum(dh_v * xh, axis=0, keepdims=True)

        @pl.when(t == 0)
        def _():
            dw_ref[...] = part

        @pl.when(t > 0)
        def _():
            dw_ref[...] += part

    row = pl.BlockSpec((tt, D), lambda t: (t, 0))
    vec = pl.BlockSpec((1, D), lambda t: (0, 0))
    return pl.pallas_call(
        body, name=name, grid=(T // tt,), in_specs=[row, vec, row, row], out_specs=[row, row, vec],
        out_shape=[jax.ShapeDtypeStruct((T, D), F32), jax.ShapeDtypeStruct((T, D), BF16),
                   jax.ShapeDtypeStruct((1, D), F32)],
        compiler_params=_cp(("arbitrary",)))(x, w, dh, dres)


def _loss_head(x, w, target):
    T, D = x.shape
    tt = _tile(T)

    def body(x_ref, w_ref, t_ref, loss_ref, dx_ref, dxb_ref, dw_ref):
        t = pl.program_id(0)
        xv = x_ref[...]
        r = lax.rsqrt(jnp.mean(xv * xv, axis=1, keepdims=True) + EPS)
        xh = xv * r
        err = xh * w_ref[...] - t_ref[...]
        lp = 0.5 * jnp.sum(jnp.mean(err * err, axis=1, keepdims=True), axis=0, keepdims=True)
        dy = err * (1.0 / D)
        dxh = dy * w_ref[...]
        dxv = r * (dxh - xh * jnp.mean(dxh * xh, axis=1, keepdims=True))
        dx_ref[...] = dxv
        dxb_ref[...] = dxv.astype(BF16)
        part = jnp.sum(dy * xh, axis=0, keepdims=True)
        lpb = jnp.broadcast_to(lp, (1, 128))

        @pl.when(t == 0)
        def _():
            dw_ref[...] = part
            loss_ref[...] = lpb

        @pl.when(t > 0)
        def _():
            dw_ref[...] += part
            loss_ref[...] += lpb

    row = pl.BlockSpec((tt, D), lambda t: (t, 0))
    vec = pl.BlockSpec((1, D), lambda t: (0, 0))
    return pl.pallas_call(
        body, name="loss_head", grid=(T // tt,), in_specs=[row, vec, row],
        out_specs=[pl.BlockSpec((1, 128), lambda t: (0, 0)), row, row, vec],
        out_shape=[jax.ShapeDtypeStruct((1, 128), F32), jax.ShapeDtypeStruct((T, D), F32),
                   jax.ShapeDtypeStruct((T, D), BF16), jax.ShapeDtypeStruct((1, D), F32)],
        compiler_params=_cp(("arbitrary",)))(x, w, target)


def _conv_fwd(x, col0, C, w, b, cb, name):
    T = x.shape[0]
    K = w.shape[0]
    tt = _tile(T)
    nt, nc, c0 = T // tt, C // cb, col0 // cb
    has_b = b is not None

    def body(*refs):
        if has_b:
            x_ref, halo_ref, w_ref, b_ref, y_ref = refs
        else:
            x_ref, halo_ref, w_ref, y_ref = refs
        t = pl.program_id(1)
        halo = jnp.where(t == 0, 0.0, halo_ref[...])
        xe = jnp.concatenate([halo, x_ref[...]], axis=0)
        acc = xe * w_ref[K - 1:K, :]
        for j in range(K - 1):
            acc = acc + pltpu.roll(xe, K - 1 - j, 0) * w_ref[j:j + 1, :]
        if has_b:
            acc = acc + b_ref[...]
        y_ref[...] = acc[8:, :]

    in_specs = [pl.BlockSpec((tt, cb), lambda j, t: (t, c0 + j)),
                pl.BlockSpec((8, cb), lambda j, t: (jnp.maximum(t * (tt // 8) - 1, 0), c0 + j)),
                pl.BlockSpec((K, cb), lambda j, t: (0, j))]
    args = [x, x, w]
    if has_b:
        in_specs.append(pl.BlockSpec((1, cb), lambda j, t: (0, j)))
        args.append(b)
    return pl.pallas_call(
        body, name=name, grid=(nc, nt), in_specs=in_specs,
        out_specs=pl.BlockSpec((tt, cb), lambda j, t: (t, j)),
        out_shape=jax.ShapeDtypeStruct((T, C), F32), compiler_params=_cp(("parallel", "parallel")))(*args)


def _conv_bwd(dy, x, col0, w, cb, name, into, want_db=False):
    T, C = dy.shape
    K = w.shape[0]
    tt = _tile(T)
    nt, nc, c0 = T // tt, C // cb, col0 // cb

    def body(*refs):
        if want_db:
            dy_ref, dyn_ref, x_ref, xp_ref, w_ref, _, dx_ref, dw_ref, db_ref = refs
        else:
            dy_ref, dyn_ref, x_ref, xp_ref, w_ref, _, dx_ref, dw_ref = refs
        t = pl.program_id(1)
        dyv = dy_ref[...]
        nxt = jnp.where(t == nt - 1, 0.0, dyn_ref[...])
        dye = jnp.concatenate([dyv, nxt], axis=0)
        n = tt + 8
        acc = dye * w_ref[K - 1:K, :]
        for j in range(K - 1):
            acc = acc + pltpu.roll(dye, n - (K - 1 - j), 0) * w_ref[j:j + 1, :]
        dx_ref[...] = acc[:tt, :].astype(dx_ref.dtype)
        prev = jnp.where(t == 0, 0.0, xp_ref[...])
        xe = jnp.concatenate([prev, x_ref[...]], axis=0)

        @pl.when(t == 0)
        def _():
            dw_ref[...] = jnp.zeros_like(dw_ref)
            if want_db:
                db_ref[...] = jnp.zeros_like(db_ref)

        for j in range(K):
            sh = K - 1 - j
            xs = xe[8:, :] if sh == 0 else pltpu.roll(xe, sh, 0)[8:, :]
            dw_ref[j:j + 1, :] += jnp.sum(dyv * xs, axis=0, keepdims=True)
        if want_db:
            db_ref[...] += jnp.sum(dyv, axis=0, keepdims=True)

    h8 = tt // 8
    in_specs = [pl.BlockSpec((tt, cb), lambda j, t: (t, j)),
                pl.BlockSpec((8, cb), lambda j, t: (jnp.minimum((t + 1) * h8, T // 8 - 1), j)),
                pl.BlockSpec((tt, cb), lambda j, t: (t, c0 + j)),
                pl.BlockSpec((8, cb), lambda j, t: (jnp.maximum(t * h8 - 1, 0), c0 + j)),
                pl.BlockSpec((K, cb), lambda j, t: (0, j)), pl.BlockSpec(memory_space=pl.ANY)]
    out_specs = [pl.BlockSpec((tt, cb), lambda j, t: (t, c0 + j)), pl.BlockSpec((K, cb), lambda j, t: (0, j))]
    out_shape = [jax.ShapeDtypeStruct(into.shape, into.dtype), jax.ShapeDtypeStruct((K, C), F32)]
    if want_db:
        out_specs.append(pl.BlockSpec((1, cb), lambda j, t: (0, j)))
        out_shape.append(jax.ShapeDtypeStruct((1, C), F32))
    return pl.pallas_call(
        body, name=name, grid=(nc, nt), in_specs=in_specs, out_specs=out_specs, out_shape=out_shape,
        input_output_aliases={5: 0}, compiler_params=_cp(("parallel", "arbitrary")))(dy, dy, x, x, w, into)


def _pool_d(ue, g, pos, tt):
    win = POOL_WINDOWS[g]
    ug = ue[:, g * 128:(g + 1) * 128]
    s = ug
    k = 1
    while k < win:
        s = s + pltpu.roll(s, k, 0)
        k *= 2
    cnt = jnp.minimum(pos + 1, win).astype(F32)
    return s[16:, :] / cnt - ug[16:, :]


def _pool_fwd(proj, pw, pb, ps):
    T = proj.shape[0]
    tt = _tile(T)

    def body(u_ref, halo_ref, w_ref, b_ref, s_ref, y_ref):
        t = pl.program_id(0)
        halo = jnp.where(t == 0, 0.0, halo_ref[...])
        ue = jnp.concatenate([halo, u_ref[...]], axis=0)
        pos = t * tt + lax.broadcasted_iota(jnp.int32, (tt, 1), 0)
        for g in range(4):
            sl = slice(g * 128, (g + 1) * 128)
            d = _pool_d(ue, g, pos, tt)
            yg = _mm(d, w_ref[g]) + b_ref[:, sl]
            y_ref[:, sl] = (yg * s_ref[:, sl]).astype(BF16)

    vec = pl.BlockSpec((1, POOL_W), lambda t: (0, 0))
    return pl.pallas_call(
        body, name="pool_fwd", grid=(T // tt,),
        in_specs=[pl.BlockSpec((tt, POOL_W), lambda t: (t, 0)),
                  pl.BlockSpec((16, POOL_W), lambda t: (jnp.maximum(t * (tt // 16) - 1, 0), 0)),
                  pl.BlockSpec((4, 128, 128), lambda t: (0, 0, 0)), vec, vec],
        out_specs=pl.BlockSpec((tt, POOL_W), lambda t: (t, 0)),
        out_shape=jax.ShapeDtypeStruct((T, D_MODEL), BF16), compiler_params=_cp(("parallel",)))(
            proj, proj, pw, pb, ps)


def _pool_bwd(dmixed, proj, pw, pb, ps):
    T = proj.shape[0]
    tt = _tile(T)
    nt = T // tt

    def body(dy_ref, dyn_ref, u_ref, halo_ref, w_ref, b_ref, s_ref, du_ref, dw_ref, db_ref, ds_ref):
        t = pl.program_id(0)
        halo = jnp.where(t == 0, 0.0, halo_ref[...])
        ue = jnp.concatenate([halo, u_ref[...]], axis=0)
        dyv = dy_ref[...]
        nxt = jnp.where(t == nt - 1, 0.0, dyn_ref[...])
        dye = jnp.concatenate([dyv, nxt], axis=0)
        n = tt + 16
        pos = t * tt + lax.broadcasted_iota(jnp.int32, (tt, 1), 0)
        pos_e = t * tt + lax.broadcasted_iota(jnp.int32, (n, 1), 0)

        @pl.when(t == 0)
        def _():
            dw_ref[...] = jnp.zeros_like(dw_ref)
            db_ref[...] = jnp.zeros_like(db_ref)
            ds_ref[...] = jnp.zeros_like(ds_ref)

        for g in range(4):
            win = POOL_WINDOWS[g]
            sl = slice(g * 128, (g + 1) * 128)
            d = _pool_d(ue, g, pos, tt)
            wg = w_ref[g]
            ypre = _mm(d, wg) + b_ref[:, sl]
            sc = s_ref[:, sl]
            ds_ref[:, sl] += jnp.sum(dyv[:, sl] * ypre, axis=0, keepdims=True)
            dyp_e = dye[:, sl] * sc
            dyp = dyp_e[:tt, :]
            db_ref[:, sl] += jnp.sum(dyp, axis=0, keepdims=True)
            dw_ref[g] += _mm(d, dyp, 0, 0)
            dd_e = _mm(dyp_e, wg, 1, 1)
            cnt_e = jnp.minimum(pos_e + 1, win).astype(F32)
            s = dd_e / cnt_e
            k = 1
            while k < win:
                s = s + pltpu.roll(s, n - k, 0)
                k *= 2
            du_ref[:, sl] = (s[:tt, :] - dd_e[:tt, :]).astype(BF16)

    vec = pl.BlockSpec((1, POOL_W), lambda t: (0, 0))
    h16 = tt // 16
    return pl.pallas_call(
        body, name="pool_bwd", grid=(nt,),
        in_specs=[pl.BlockSpec((tt, POOL_W), lambda t: (t, 0)),
                  pl.BlockSpec((16, POOL_W), lambda t: (jnp.minimum((t + 1) * h16, T // 16 - 1), 0)),
                  pl.BlockSpec((tt, POOL_W), lambda t: (t, 0)),
                  pl.BlockSpec((16, POOL_W), lambda t: (jnp.maximum(t * h16 - 1, 0), 0)),
                  pl.BlockSpec((4, 128, 128), lambda t: (0, 0, 0)), vec, vec],
        out_specs=[pl.BlockSpec((tt, POOL_W), lambda t: (t, 0)),
                   pl.BlockSpec((4, 128, 128), lambda t: (0, 0, 0)), vec, vec],
        out_shape=[jax.ShapeDtypeStruct((T, PCOLS), BF16), jax.ShapeDtypeStruct((4, 128, 128), F32),
                   jax.ShapeDtypeStruct((1, POOL_W), F32), jax.ShapeDtypeStruct((1, POOL_W), F32)],
        compiler_params=_cp(("arbitrary",)))(dmixed, dmixed, proj, proj, pw, pb, ps)


def _gdn_pre_fwd(cpre, proj, alog, dtb):
    T = cpre.shape[0]
    tt = _tile(T)

    def body(c_ref, ab_ref, alog_ref, dtb_ref, qkv_ref, bb_ref, gb_ref):
        for p in range(3):
            for h in range(HEADS):
                cc = c_ref[:, (p * HEADS + h) * HD:(p * HEADS + h + 1) * HD]
                s = cc * _sigmoid(cc)
                if p < 2:
                    s = s * lax.rsqrt(jnp.sum(s * s, axis=1, keepdims=True) + EPS)
                if p == 0:
                    s = s * (HD ** -0.5)
                qkv_ref[p, h] = s
        ab = ab_ref[...]
        g = -jnp.exp(alog_ref[...]) * _softplus(ab + dtb_ref[...])
        r64 = lax.broadcasted_iota(jnp.int32, (tt, 1), 0) & (CHUNK - 1)
        k = 1
        while k < CHUNK:
            g = g + jnp.where(r64 >= k, pltpu.roll(g, k, 0), 0.0)
            k *= 2
        sb = _sigmoid(ab)
        for h in range(HEADS):
            gb_ref[h] = jnp.broadcast_to(g[:, h:h + 1], (tt, HD))
            bb_ref[h] = jnp.broadcast_to(sb[:, HEADS + h:HEADS + h + 1], (tt, HD))

    vec = pl.BlockSpec((1, 128), lambda t: (0, 0))
    hb = pl.BlockSpec((HEADS, tt, HD), lambda t: (0, t, 0))
    return pl.pallas_call(
        body, name="gdn_pre_fwd", grid=(T // tt,),
        in_specs=[pl.BlockSpec((tt, 3 * GDN_W), lambda t: (t, 0)),
                  pl.BlockSpec((tt, 128), lambda t: (t, C_AB // 128)), vec, vec],
        out_specs=[pl.BlockSpec((3, HEADS, tt, HD), lambda t: (0, 0, t, 0)), hb, hb],
        out_shape=[jax.ShapeDtypeStruct((3, HEADS, T, HD), F32), jax.ShapeDtypeStruct((HEADS, T, HD), F32),
                   jax.ShapeDtypeStruct((HEADS, T, HD), F32)],
        compiler_params=_cp(("parallel",)))(cpre, proj, alog, dtb)


def _gdn_pre_bwd(dq, dk, dv, cpre, dbb, dgb, proj, alog, dtb, dproj):
    T = cpre.shape[0]
    tt = _tile(T)

    def body(dq_ref, dk_ref, dv_ref, c_ref, dbb_ref, dgb_ref, ab_ref, alog_ref, dtb_ref, _,
             dc_ref, dab_ref, dalog_ref, ddtb_ref):
        t = pl.program_id(0)
        srcs = (dq_ref, dk_ref, dv_ref)
        for p in range(3):
            for h in range(HEADS):
                sl = slice((p * HEADS + h) * HD, (p * HEADS + h + 1) * HD)
                cc = c_ref[:, sl]
                sg = _sigmoid(cc)
                s = cc * sg
                dyv = srcs[p][h]
                if p < 2:
                    r = lax.rsqrt(jnp.sum(s * s, axis=1, keepdims=True) + EPS)
                    y = s * r
                    if p == 0:
                        dyv = dyv * (HD ** -0.5)
                    ds = r * (dyv - y * jnp.sum(dyv * y, axis=1, keepdims=True))
                else:
                    ds = dyv
                dc_ref[:, sl] = ds * sg * (1.0 + cc * (1.0 - sg))
        lane = lax.broadcasted_iota(jnp.int32, (tt, 128), 1)
        dg = jnp.zeros((tt, 128), F32)
        dbeta = jnp.zeros((tt, 128), F32)
        for h in range(HEADS):
            dg = jnp.where(lane == h, dgb_ref[h], dg)
            dbeta = jnp.where(lane == HEADS + h, dbb_ref[h], dbeta)
        r64 = lax.broadcasted_iota(jnp.int32, (tt, 1), 0) & (CHUNK - 1)
        k = 1
        while k < CHUNK:
            dg = dg + jnp.where(r64 < CHUNK - k, pltpu.roll(dg, tt - k, 0), 0.0)
            k *= 2
        ab = ab_ref[...]
        e = jnp.exp(alog_ref[...])
        xx = ab + dtb_ref[...]
        g = -e * _softplus(xx)
        da = jnp.where(lane < HEADS, dg * (-e) * _sigmoid(xx), 0.0)
        pa = jnp.sum(jnp.where(lane < HEADS, dg * g, 0.0), axis=0, keepdims=True)
        pd = jnp.sum(da, axis=0, keepdims=True)

        @pl.when(t == 0)
        def _():
            dalog_ref[...] = pa
            ddtb_ref[...] = pd

        @pl.when(t > 0)
        def _():
            dalog_ref[...] += pa
            ddtb_ref[...] += pd

        sb = _sigmoid(ab)
        dab_ref[:, :128] = jnp.where(lane < HEADS, da, dbeta * sb * (1.0 - sb)).astype(BF16)
        dab_ref[:, 128:] = jnp.zeros((tt, 128), BF16)

    vec = pl.BlockSpec((1, 128), lambda t: (0, 0))
    hb = pl.BlockSpec((HEADS, tt, HD), lambda t: (0, t, 0))
    return pl.pallas_call(
        body, name="gdn_pre_bwd", grid=(T // tt,),
        in_specs=[hb, hb, hb, pl.BlockSpec((tt, 3 * GDN_W), lambda t: (t, 0)), hb, hb,
                  pl.BlockSpec((tt, 128), lambda t: (t, C_AB // 128)), vec, vec, pl.BlockSpec(memory_space=pl.ANY)],
        out_specs=[pl.BlockSpec((tt, 3 * GDN_W), lambda t: (t, 0)),
                   pl.BlockSpec((tt, 256), lambda t: (t, C_AB // 256)), vec, vec],
        out_shape=[jax.ShapeDtypeStruct((T, 3 * GDN_W), F32), jax.ShapeDtypeStruct(dproj.shape, dproj.dtype),
                   jax.ShapeDtypeStruct((1, 128), F32), jax.ShapeDtypeStruct((1, 128), F32)],
        input_output_aliases={9: 1},
        compiler_params=_cp(("arbitrary",)))(dq, dk, dv, cpre, dbb, dgb, proj, alog, dtb, dproj)


def _split2(x):
    hi = x.astype(BF16)
    return hi, (x - hi.astype(F32)).astype(BF16)


def _bmm3s(a2, b2, ca=2, cb=1):
    def f(x, y):
        return lax.dot_general(x, y, (((ca,), (cb,)), ((0,), (0,))), preferred_element_type=F32)

    return f(a2[0], b2[0]) + (f(a2[0], b2[1]) + f(a2[1], b2[0]))


def _bmm3(a, b, ca=2, cb=1):
    return _bmm3s(_split2(a), _split2(b), ca, cb)


def _tri_inv(a):
    nb = a.shape[0]
    ri = lax.broadcasted_iota(jnp.int32, (nb, CHUNK, CHUNK), 1)
    ci = lax.broadcasted_iota(jnp.int32, (nb, CHUNK, CHUNK), 2)
    n = -a
    p = jnp.where(ri == ci, 1.0, 0.0) + n
    n2 = _split2(n)
    for _ in range(5):
        n2 = _split2(_bmm3s(n2, n2))
        p = p + _bmm3s(_split2(p), n2)
    return p


def _gdn_chunk_common(q, k, v, bb3, gb3, tm_saved=None):
    nb = q.shape[0]
    need_t = tm_saved is not None
    beta = bb3[:, :, 0:1]
    gcol = gb3[:, :, 0:1]
    bcol = bb3[:, :, :CHUNK]
    gcm = gb3[:, :, :CHUNK]
    oh = jnp.where(lax.broadcasted_iota(jnp.int32, (nb, CHUNK, HD), 2) == 0, 1.0, 0.0)
    grow = _bmm(oh, gb3, 2, 2, HI, False)
    ri = lax.broadcasted_iota(jnp.int32, (nb, CHUNK, CHUNK), 1)
    ci = lax.broadcasted_iota(jnp.int32, (nb, CHUNK, CHUNK), 2)
    tril, stl = ri >= ci, ri > ci
    dg = gcm - grow
    dec = jnp.where(tril, jnp.exp(jnp.where(tril, dg, 0.0)), 0.0)
    kk = _bmm(k, k, 2, 2)
    qk = _bmm(q, k, 2, 2)
    tm = tm_saved if need_t else _tri_inv(jnp.where(stl, bcol * kk * dec, 0.0))
    gam = jnp.exp(gcol)
    glast = gb3[:, CHUNK - 1:CHUNK, 0:1]
    egl = jnp.exp(glast)
    rw = k * (beta * gam)
    ru = v * beta
    wu = _bmm3(tm, jnp.concatenate([rw, ru], axis=2), 2, 1)
    kdf = jnp.exp(glast - gcol)
    out = dict(beta=beta, bcol=bcol, tril=tril, stl=stl, dec=dec, kk=kk, qk=qk, tm=tm, gam=gam, egl=egl,
               rw=rw, wu=wu, at=qk * dec, qd=q * gam, kdf=kdf, kd=k * kdf)
    if need_t:
        brow = _bmm(oh, bb3, 2, 2, HI, False)
        triu, stu = ri <= ci, ri < ci
        dect = jnp.where(triu, jnp.exp(jnp.where(triu, -dg, 0.0)), 0.0)
        qkt = _bmm(k, q, 2, 2)
        eye = jnp.where(ri == ci, 1.0, 0.0)
        out.update(brow=brow, triu=triu, stu=stu, dect=dect, qkt=qkt, tmt=_bmm3(eye, tm, 2, 2),
                   att=qkt * dect)
    return out


def _gdn_rows(T):
    return min(T, 512)


HP = 2


def _gdn_fwd(qkv, bb, gb, proj, nw, mixed):
    T = proj.shape[0]
    R = _gdn_rows(T)
    nb = R // CHUNK
    B = HP * nb

    def body(q_ref, k_ref, v_ref, bb_ref, gb_ref, z_ref, nw_ref, _, y_ref, st_ref, tm_ref,
             s_ref, w_s, u_s, at_s, qd_s, kd_s):
        t = pl.program_id(1)

        @pl.when(t == 0)
        def _():
            s_ref[...] = jnp.zeros_like(s_ref)

        sh = (B, CHUNK, HD)
        q, k, v = q_ref[...].reshape(sh), k_ref[...].reshape(sh), v_ref[...].reshape(sh)
        c = _gdn_chunk_common(q, k, v, bb_ref[...].reshape(sh), gb_ref[...].reshape(sh))
        tm_ref[...] = c["tm"].reshape(HP, nb, CHUNK, CHUNK)
        w_s[...] = c["wu"][:, :, :HD]
        u_s[...] = c["wu"][:, :, HD:]
        at_s[...] = c["at"]
        qd_s[...] = c["qd"]
        kd_s[...] = c["kd"]
        egl = c["egl"]
        nwv = nw_ref[...]
        for n in range(nb):
            rows = slice(n * CHUNK, (n + 1) * CHUNK)
            for hh in range(HP):
                b = hh * nb + n
                cols = slice(hh * HD, (hh + 1) * HD)
                s = s_ref[hh]
                st_ref[hh, n] = s
                vn = u_s[b] - _mm(w_s[b], s)
                o = _mm(qd_s[b], s) + _mm(at_s[b], vn)
                s_ref[hh] = s * egl[b] + _mm(kd_s[b], vn, 0, 0)
                zz = z_ref[rows, cols]
                on = o * lax.rsqrt(jnp.mean(o * o, axis=1, keepdims=True) + EPS)
                y_ref[rows, cols] = (on * nwv * (zz * _sigmoid(zz))).astype(BF16)

    def hm(p):
        return pl.BlockSpec((None, HP, R, HD), lambda h, t: (p, h, t, 0))

    hb = pl.BlockSpec((HP, R, HD), lambda h, t: (h, t, 0))
    cs = pltpu.VMEM((B, CHUNK, HD), F32)
    return pl.pallas_call(
        body, name="gdn_fwd", grid=(HEADS // HP, T // R),
        in_specs=[hm(0), hm(1), hm(2), hb, hb,
                  pl.BlockSpec((R, HP * HD), lambda h, t: (t, C_Z // (HP * HD) + h)),
                  pl.BlockSpec((1, HD), lambda h, t: (0, 0)), pl.BlockSpec(memory_space=pl.ANY)],
        out_specs=[pl.BlockSpec((R, HP * HD), lambda h, t: (t, M_GDN // (HP * HD) + h)),
                   pl.BlockSpec((HP, nb, HD, HD), lambda h, t: (h, t, 0, 0)),
                   pl.BlockSpec((HP, nb, CHUNK, CHUNK), lambda h, t: (h, t, 0, 0))],
        out_shape=[jax.ShapeDtypeStruct(mixed.shape, mixed.dtype),
                   jax.ShapeDtypeStruct((HEADS, T // CHUNK, HD, HD), F32),
                   jax.ShapeDtypeStruct((HEADS, T // CHUNK, CHUNK, CHUNK), F32)],
        scratch_shapes=[pltpu.VMEM((HP, HD, HD), F32), cs, cs, pltpu.VMEM((B, CHUNK, CHUNK), F32), cs, cs],
        input_output_aliases={7: 0},
        compiler_params=_cp(("parallel", "arbitrary")))(qkv, qkv, qkv, bb, gb, proj, nw, mixed)


def _gdn_bwd(qkv, bb, gb, proj, nw, states, tms, dmixed, dproj):
    T = proj.shape[0]
    R = _gdn_rows(T)
    nb = R // CHUNK
    ntb = T // R
    B = HP * nb

    def by_head(ref):
        return jnp.concatenate([ref[:, hh * HD:(hh + 1) * HD].reshape(nb, CHUNK, HD) for hh in range(HP)], axis=0)

    def body(q_ref, k_ref, v_ref, bb_ref, gb_ref, z_ref, nw_ref, st_ref, tm_ref, dy_ref, _,
             dq_ref, dk_ref, dv_ref, dbb_ref, dgb_ref, dz_ref, dnw_ref,
             ds_ref, att_s, do_s, kd_s, vn_s, qd_s, w_s, dvn_s, dkd_s, dgl_s):
        hp = pl.program_id(0)
        t = pl.program_id(1)

        @pl.when(t == 0)
        def _():
            ds_ref[...] = jnp.zeros_like(ds_ref)

        @pl.when((t == 0) & (hp == 0))
        def _():
            dnw_ref[...] = jnp.zeros_like(dnw_ref)

        sh = (B, CHUNK, HD)
        q, k, v = q_ref[...].reshape(sh), k_ref[...].reshape(sh), v_ref[...].reshape(sh)
        c = _gdn_chunk_common(q, k, v, bb_ref[...].reshape(sh), gb_ref[...].reshape(sh),
                              tm_ref[...].reshape(B, CHUNK, CHUNK))
        w, u = c["wu"][:, :, :HD], c["wu"][:, :, HD:]
        sall = st_ref[...].reshape(B, HD, HD)
        vn = u - _bmm(w, sall, 2, 1)
        o = _bmm(c["qd"], sall, 2, 1) + _bmm(c["at"], vn, 2, 1)
        z = by_head(z_ref)
        dy = by_head(dy_ref)
        nwv = nw_ref[...].reshape(1, 1, HD)
        rs = lax.rsqrt(jnp.mean(o * o, axis=2, keepdims=True) + EPS)
        on = o * rs
        sg = _sigmoid(z)
        sz = z * sg
        dnw_ref[...] += jnp.sum(jnp.sum(dy * on * sz, axis=0), axis=0, keepdims=True)
        dz3 = dy * on * nwv * (sg * (1.0 + z * (1.0 - sg)))
        for hh in range(HP):
            dz_ref[:, hh * HD:(hh + 1) * HD] = dz3[hh * nb:(hh + 1) * nb].reshape(R, HD).astype(BF16)
        don = dy * nwv * sz
        do = rs * (don - on * jnp.mean(don * on, axis=2, keepdims=True))
        dqd = _bmm(do, sall, 2, 2)
        dat = jnp.where(c["tril"], _bmm(do, vn, 2, 2), 0.0)
        datt = jnp.where(c["triu"], _bmm(vn, do, 2, 2), 0.0)
        att_s[...] = c["att"]
        do_s[...] = do
        kd_s[...] = c["kd"]
        vn_s[...] = vn
        qd_s[...] = c["qd"]
        w_s[...] = w
        egl = c["egl"]
        for n in reversed(range(nb)):
            for hh in range(HP):
                b = hh * nb + n
                dso = ds_ref[hh]
                dvn_n = _mm(att_s[b], do_s[b]) + _mm(kd_s[b], dso)
                dkd_s[b] = _mm(vn_s[b], dso, 1, 1)
                dgl = egl[b] * jnp.sum(jnp.sum(st_ref[hh, n] * dso, axis=1, keepdims=True), axis=0, keepdims=True)
                dgl_s[b] = jnp.broadcast_to(dgl, (8, HD))
                ds_ref[hh] = egl[b] * dso + _mm(qd_s[b], do_s[b], 0, 0) - _mm(w_s[b], dvn_n, 0, 0)
                dvn_s[b] = dvn_n
        dvn = dvn_s[...]
        dkd = dkd_s[...]
        dgl = dgl_s[...][:, 0:1, 0:1]
        dw = -_bmm(dvn, sall, 2, 2)
        dr = _bmm3(c["tmt"], jnp.concatenate([dw, dvn], axis=2), 2, 1)
        drw, dru = dr[:, :, :HD], dr[:, :, HD:]
        wu = c["wu"]
        dr2, wu2 = _split2(dr), _split2(wu)
        da = -jnp.where(c["stl"], _bmm3s(dr2, wu2, 2, 2), 0.0)
        da_t = -jnp.where(c["stu"], _bmm3s(wu2, dr2, 2, 2), 0.0)
        beta, gam, dec, dect, kk = c["beta"], c["gam"], c["dec"], c["dect"], c["kk"]
        bcol, brow = c["bcol"], c["brow"]
        dbeta = (jnp.sum(da * kk * dec, axis=2, keepdims=True)
                 + jnp.sum(drw * k * gam + dru * v, axis=2, keepdims=True))
        dkk = bcol * da * dec
        dkk_t = brow * da_t * dect
        e = (bcol * da * kk + dat * c["qk"]) * dec
        e_t = (brow * da_t * kk + datt * c["qkt"]) * dect
        kd = c["kd"]
        dq_ref[...] = (_bmm(dat * dec, k, 2, 1) + dqd * gam).reshape(HP, R, HD)
        dk_ref[...] = (_bmm(datt * dect, q, 2, 1) + _bmm(dkk + dkk_t, k, 2, 1) + dkd * c["kdf"]
                       + drw * (beta * gam)).reshape(HP, R, HD)
        dv_ref[...] = (dru * beta).reshape(HP, R, HD)
        skd = jnp.sum(dkd * kd, axis=2, keepdims=True)
        dgc = (jnp.sum(e, axis=2, keepdims=True) - jnp.sum(e_t, axis=2, keepdims=True)
               + jnp.sum(drw * c["rw"] + dqd * c["qd"], axis=2, keepdims=True) - skd)
        tot = jnp.sum(skd, axis=1, keepdims=True) + dgl
        rowi = lax.broadcasted_iota(jnp.int32, (B, CHUNK, 1), 1)
        dgc = dgc + jnp.where(rowi == CHUNK - 1, tot, 0.0)
        dbb_ref[...] = jnp.broadcast_to(dbeta, sh).reshape(HP, R, HD)
        dgb_ref[...] = jnp.broadcast_to(dgc, sh).reshape(HP, R, HD)

    def rt(t):
        return ntb - 1 - t

    def hm(p):
        return pl.BlockSpec((None, HP, R, HD), lambda h, t: (p, h, rt(t), 0))

    hb = pl.BlockSpec((HP, R, HD), lambda h, t: (h, rt(t), 0))
    cs = pltpu.VMEM((B, CHUNK, HD), F32)
    ob = jax.ShapeDtypeStruct((HEADS, T, HD), F32)
    return pl.pallas_call(
        body, name="gdn_bwd", grid=(HEADS // HP, ntb),
        in_specs=[hm(0), hm(1), hm(2), hb, hb,
                  pl.BlockSpec((R, HP * HD), lambda h, t: (rt(t), C_Z // (HP * HD) + h)),
                  pl.BlockSpec((1, HD), lambda h, t: (0, 0)),
                  pl.BlockSpec((HP, nb, HD, HD), lambda h, t: (h, rt(t), 0, 0)),
                  pl.BlockSpec((HP, nb, CHUNK, CHUNK), lambda h, t: (h, rt(t), 0, 0)),
                  pl.BlockSpec((R, HP * HD), lambda h, t: (rt(t), M_GDN // (HP * HD) + h)),
                  pl.BlockSpec(memory_space=pl.ANY)],
        out_specs=[hb, hb, hb, hb, hb, pl.BlockSpec((R, HP * HD), lambda h, t: (rt(t), C_Z // (HP * HD) + h)),
                   pl.BlockSpec((1, HD), lambda h, t: (0, 0))],
        out_shape=[ob, ob, ob, ob, ob, jax.ShapeDtypeStruct(dproj.shape, dproj.dtype),
                   jax.ShapeDtypeStruct((1, HD), F32)],
        scratch_shapes=[pltpu.VMEM((HP, HD, HD), F32), pltpu.VMEM((B, CHUNK, CHUNK), F32), cs, cs, cs, cs, cs, cs,
                        cs, pltpu.VMEM((B, 8, HD), F32)],
        input_output_aliases={10: 5},
        compiler_params=_cp(("arbitrary", "arbitrary")))(qkv, qkv, qkv, bb, gb, proj, nw, states, tms, dmixed,
                                                         dproj)


def _lru_gates(xc, wa, ba, wx, bx, lam, gpos):
    xb = xc.astype(BF16)
    r = _sigmoid(_mm(xb, wa) + ba)
    i = _sigmoid(_mm(xb, wx) + bx)
    sp = _softplus(-lam)
    log_a = -LRU_C * r * sp
    a = jnp.exp(log_a)
    mult = jnp.where(gpos == 0, 1.0, jnp.sqrt(-_expm1(2.0 * log_a)))
    return r, i, sp, a, mult


def _lru_fwd(xc, proj, wa, ba, wx, bx, lam, mixed):
    T = xc.shape[0]
    tt = _tile(T)

    def body(xc_ref, gr_ref, wa_ref, ba_ref, wx_ref, bx_ref, lam_ref, _, y_ref, h_ref, carry_ref):
        t = pl.program_id(1)

        @pl.when(t == 0)
        def _():
            carry_ref[...] = jnp.zeros_like(carry_ref)

        row = lax.broadcasted_iota(jnp.int32, (tt, 1), 0)
        xcv = xc_ref[...]
        r, i, sp, a, mult = _lru_gates(xcv, wa_ref[...], ba_ref[...], wx_ref[...], bx_ref[...], lam_ref[...],
                                       t * tt + row)
        av, bv = a, mult * i * xcv
        k = 1
        while k < tt:
            a_s = jnp.where(row >= k, pltpu.roll(av, k, 0), 1.0)
            b_s = jnp.where(row >= k, pltpu.roll(bv, k, 0), 0.0)
            bv = bv + av * b_s
            av = av * a_s
            k *= 2
        h = bv + av * carry_ref[0:1, :]
        carry_ref[...] = jnp.broadcast_to(h[tt - 1:tt, :], (8, 128))
        h_ref[...] = h
        y_ref[...] = (h * _gelu(gr_ref[...])).astype(BF16)

    blk = pl.BlockSpec((tt, 128), lambda j, t: (t, j))
    vec = pl.BlockSpec((1, 128), lambda j, t: (0, j))
    mat = pl.BlockSpec((None, 128, 128), lambda j, t: (j, 0, 0))
    return pl.pallas_call(
        body, name="lru_fwd", grid=(LRU_W // 128, T // tt),
        in_specs=[blk, pl.BlockSpec((tt, 128), lambda j, t: (t, C_GR // 128 + j)), mat, vec, mat, vec, vec,
                  pl.BlockSpec(memory_space=pl.ANY)],
        out_specs=[pl.BlockSpec((tt, 128), lambda j, t: (t, M_LRU // 128 + j)), blk],
        out_shape=[jax.ShapeDtypeStruct(mixed.shape, mixed.dtype), jax.ShapeDtypeStruct((T, LRU_W), F32)],
        scratch_shapes=[pltpu.VMEM((8, 128), F32)], input_output_aliases={7: 0},
        compiler_params=_cp(("parallel", "arbitrary")))(xc, proj, wa, ba, wx, bx, lam, mixed)


def _lru_bwd(dmixed, xc, proj, hst, wa, ba, wx, bx, lam, dproj):
    T = xc.shape[0]
    tt = _tile(T)
    nt = T // tt

    def body(dy_ref, xc_ref, gr_ref, h_ref, hp_ref, wa_ref, ba_ref, wx_ref, bx_ref, lam_ref, _,
             dxc_ref, dgr_ref, dwa_ref, dwx_ref, dba_ref, dbx_ref, dlam_ref, lc_ref, ac_ref):
        t = pl.program_id(1)
        tr = nt - 1 - t

        @pl.when(t == 0)
        def _():
            lc_ref[...] = jnp.zeros_like(lc_ref)
            ac_ref[...] = jnp.zeros_like(ac_ref)
            dwa_ref[...] = jnp.zeros_like(dwa_ref)
            dwx_ref[...] = jnp.zeros_like(dwx_ref)
            dba_ref[...] = jnp.zeros_like(dba_ref)
            dbx_ref[...] = jnp.zeros_like(dbx_ref)
            dlam_ref[...] = jnp.zeros_like(dlam_ref)

        row = lax.broadcasted_iota(jnp.int32, (tt, 1), 0)
        gpos = tr * tt + row
        xcv = xc_ref[...]
        wav, wxv, lamv = wa_ref[...], wx_ref[...], lam_ref[...]
        r, i, sp, a, mult = _lru_gates(xcv, wav, ba_ref[...], wxv, bx_ref[...], lamv, gpos)
        h = h_ref[...]
        dy = dy_ref[...]
        gg, dgg = _gelu_and_grad(gr_ref[...])
        dgr_ref[...] = (dy * h * dgg).astype(BF16)
        bv = dy * gg
        cv = jnp.where(row < tt - 1, pltpu.roll(a, tt - 1, 0), ac_ref[0:1, :])
        k = 1
        while k < tt:
            c_s = jnp.where(row < tt - k, pltpu.roll(cv, tt - k, 0), 1.0)
            b_s = jnp.where(row < tt - k, pltpu.roll(bv, tt - k, 0), 0.0)
            bv = bv + cv * b_s
            cv = cv * c_s
            k *= 2
        lm = bv + cv * lc_ref[0:1, :]
        lc_ref[...] = jnp.broadcast_to(lm[0:1, :], (8, 128))
        ac_ref[...] = jnp.broadcast_to(a[0:1, :], (8, 128))
        hp = jnp.where(tr == 0, 0.0, hp_ref[...])
        hs = pltpu.roll(jnp.concatenate([hp, h], axis=0), 1, 0)[8:, :]
        da = lm * hs
        dmult = lm * i * xcv
        di = lm * mult * xcv
        dxc = lm * mult * i
        dlog_a = a * da - jnp.where(gpos == 0, 0.0, dmult * a * a / mult)
        dr = dlog_a * (-LRU_C * sp)
        dsp = jnp.sum(dlog_a * (-LRU_C * r), axis=0, keepdims=True)
        dlam_ref[...] += dsp * (-_sigmoid(-lamv))
        dpr = dr * r * (1.0 - r)
        dpi = di * i * (1.0 - i)
        dba_ref[...] += jnp.sum(dpr, axis=0, keepdims=True)
        dbx_ref[...] += jnp.sum(dpi, axis=0, keepdims=True)
        dwa_ref[...] += _mm(xcv, dpr, 0, 0)
        dwx_ref[...] += _mm(xcv, dpi, 0, 0)
        dxc_ref[...] = dxc + _mm(dpr, wav, 1, 1) + _mm(dpi, wxv, 1, 1)

    def rt(t):
        return nt - 1 - t

    blk = pl.BlockSpec((tt, 128), lambda j, t: (rt(t), j))
    vec = pl.BlockSpec((1, 128), lambda j, t: (0, j))
    mat = pl.BlockSpec((None, 128, 128), lambda j, t: (j, 0, 0))
    h8 = tt // 8
    mshape = jax.ShapeDtypeStruct((LRU_W // 128, 128, 128), F32)
    vshape = jax.ShapeDtypeStruct((1, LRU_W), F32)
    return pl.pallas_call(
        body, name="lru_bwd", grid=(LRU_W // 128, nt),
        in_specs=[pl.BlockSpec((tt, 128), lambda j, t: (rt(t), M_LRU // 128 + j)), blk,
                  pl.BlockSpec((tt, 128), lambda j, t: (rt(t), C_GR // 128 + j)), blk,
                  pl.BlockSpec((8, 128), lambda j, t: (jnp.maximum(rt(t) * h8 - 1, 0), j)),
                  mat, vec, mat, vec, vec, pl.BlockSpec(memory_space=pl.ANY)],
        out_specs=[blk, pl.BlockSpec((tt, 128), lambda j, t: (rt(t), C_GR // 128 + j)), mat, mat, vec, vec, vec],
        out_shape=[jax.ShapeDtypeStruct((T, LRU_W), F32), jax.ShapeDtypeStruct(dproj.shape, dproj.dtype),
                   mshape, mshape, vshape, vshape, vshape],
        scratch_shapes=[pltpu.VMEM((8, 128), F32), pltpu.VMEM((8, 128), F32)], input_output_aliases={10: 1},
        compiler_params=_cp(("parallel", "arbitrary")))(dmixed, xc, proj, hst, hst, wa, ba, wx, bx, lam, dproj)


FFN_CB = 512
FFN_K = 3


def _ffn_conv(ge, w_ref):
    acc = ge * w_ref[FFN_K - 1:FFN_K, :]
    for j in range(FFN_K - 1):
        acc = acc + pltpu.roll(ge, FFN_K - 1 - j, 0) * w_ref[j:j + 1, :]
    return acc


FFN_HALO = 16


def _ffn_gate_fwd(up, w):
    T = up.shape[0]
    tt = _tile(T)
    cb = FFN_CB
    nc = D_FF // cb
    hh = tt // FFN_HALO

    def body(g_ref, gp_ref, v_ref, w_ref, o_ref, ot_ref):
        t = pl.program_id(1)
        prev = jnp.where(t == 0, 0.0, gp_ref[...].astype(F32))
        ge = jnp.concatenate([prev, g_ref[...].astype(F32)], axis=0)
        gc = _ffn_conv(ge, w_ref)[FFN_HALO:, :]
        a = _gelu(gc) * v_ref[...].astype(F32)
        o_ref[...] = a.astype(BF16)
        ot_ref[...] = a.T.astype(BF16)

    return pl.pallas_call(
        body, name="ffn_gate_fwd", grid=(nc, T // tt),
        in_specs=[pl.BlockSpec((tt, cb), lambda j, t: (t, j)),
                  pl.BlockSpec((FFN_HALO, cb), lambda j, t: (jnp.maximum(t * hh - 1, 0), j)),
                  pl.BlockSpec((tt, cb), lambda j, t: (t, nc + j)),
                  pl.BlockSpec((FFN_K, cb), lambda j, t: (0, j))],
        out_specs=[pl.BlockSpec((tt, cb), lambda j, t: (t, j)), pl.BlockSpec((cb, tt), lambda j, t: (j, t))],
        out_shape=[jax.ShapeDtypeStruct((T, D_FF), BF16), jax.ShapeDtypeStruct((D_FF, T), BF16)],
        compiler_params=_cp(("parallel", "parallel")))(up, up, up, w)


def _ffn_gate_bwd(dact, up, w):
    T = up.shape[0]
    tt = _tile(T)
    cb = FFN_CB
    nc = D_FF // cb
    nt = T // tt
    hh = tt // FFN_HALO
    H = FFN_HALO

    def body(d_ref, dn_ref, g_ref, gp_ref, gn_ref, v_ref, vn_ref, w_ref, dup_ref, dw_ref):
        t = pl.program_id(1)
        prev = jnp.where(t == 0, 0.0, gp_ref[...].astype(F32))
        ge = jnp.concatenate([prev, g_ref[...].astype(F32), gn_ref[...].astype(F32)], axis=0)
        gc = _ffn_conv(ge, w_ref)[H:, :]
        gg, dgg = _gelu_and_grad(gc)
        de = jnp.concatenate([d_ref[...].astype(F32), jnp.where(t == nt - 1, 0.0, dn_ref[...].astype(F32))], axis=0)
        ve = jnp.concatenate([v_ref[...].astype(F32), vn_ref[...].astype(F32)], axis=0)
        dup_ref[1] = (de * gg)[:tt, :].astype(BF16)
        dgc = de * ve * dgg
        n = tt + H
        acc = dgc * w_ref[FFN_K - 1:FFN_K, :]
        for j in range(FFN_K - 1):
            acc = acc + pltpu.roll(dgc, n - (FFN_K - 1 - j), 0) * w_ref[j:j + 1, :]
        dup_ref[0] = acc[:tt, :].astype(BF16)

        @pl.when(t == 0)
        def _():
            dw_ref[...] = jnp.zeros_like(dw_ref)

        dgm = dgc[:tt, :]
        for j in range(FFN_K):
            sh = FFN_K - 1 - j
            xs = ge[H:H + tt, :] if sh == 0 else pltpu.roll(ge, sh, 0)[H:H + tt, :]
            dw_ref[j:j + 1, :] += jnp.sum(dgm * xs, axis=0, keepdims=True)

    def nxt(t):
        return jnp.minimum((t + 1) * hh, T // H - 1)

    return pl.pallas_call(
        body, name="ffn_gate_bwd", grid=(nc, nt),
        in_specs=[pl.BlockSpec((tt, cb), lambda j, t: (t, j)),
                  pl.BlockSpec((H, cb), lambda j, t: (nxt(t), j)),
                  pl.BlockSpec((tt, cb), lambda j, t: (t, j)),
                  pl.BlockSpec((H, cb), lambda j, t: (jnp.maximum(t * hh - 1, 0), j)),
                  pl.BlockSpec((H, cb), lambda j, t: (nxt(t), j)),
                  pl.BlockSpec((tt, cb), lambda j, t: (t, nc + j)),
                  pl.BlockSpec((H, cb), lambda j, t: (nxt(t), nc + j)),
                  pl.BlockSpec((FFN_K, cb), lambda j, t: (0, j))],
        out_specs=[pl.BlockSpec((2, tt, cb), lambda j, t: (0, t, j)),
                   pl.BlockSpec((FFN_K, cb), lambda j, t: (0, j))],
        out_shape=[jax.ShapeDtypeStruct((2, T, D_FF), BF16), jax.ShapeDtypeStruct((FFN_K, D_FF), F32)],
        compiler_params=_cp(("parallel", "arbitrary")))(dact, dact, up, up, up, up, up, w)


def _row_tile(rows, cap):
    best = 8
    for r in range(8, min(rows, cap) + 1, 8):
        if rows % r == 0:
            best = r
    return best


def _adamw(parts, w, m, v, rt, name, layer=None, prev=None):
    P, R, C = parts.shape

    def body(p_ref, w_ref, m_ref, v_ref, *rest):
        g_ref, d_ref, mo_ref, vo_ref = rest[-4:]
        g = p_ref[0].astype(F32)
        for i in range(1, P):
            g = g + p_ref[i].astype(F32)
        wv = w_ref[...]
        mn = ADAM_B1 * m_ref[...] + (1.0 - ADAM_B1) * g
        vn = ADAM_B2 * v_ref[...] + (1.0 - ADAM_B2) * (g * g)
        m_hat = mn / (1.0 - ADAM_B1 ** ADAM_STEP)
        v_hat = vn / (1.0 - ADAM_B2 ** ADAM_STEP)
        g_ref[...] = g
        d_ref[...] = -ADAM_LR * (m_hat / (jnp.sqrt(v_hat) + ADAM_EPS) + ADAM_WD * wv)
        mo_ref[...] = mn
        vo_ref[...] = vn

    if layer is None:
        blk = pl.BlockSpec((rt, C), lambda r: (r, 0))
        sh = jax.ShapeDtypeStruct((R, C), F32)
    else:
        blk = pl.BlockSpec((None, rt, C), lambda r: (layer, r, 0))
        sh = jax.ShapeDtypeStruct(w.shape, F32)
    extra = list(prev) if prev is not None else []
    return pl.pallas_call(
        body, name=name, grid=(R // rt,),
        in_specs=[pl.BlockSpec((P, rt, C), lambda r: (0, r, 0)), blk, blk, blk]
        + [pl.BlockSpec(memory_space=pl.ANY)] * len(extra),
        out_specs=[blk, blk, blk, blk], out_shape=[sh, sh, sh, sh],
        input_output_aliases={4 + i: i for i in range(len(extra))},
        compiler_params=_cp(("parallel",)))(parts, w, m, v, *extra)


def _peer(k):
    x, y, c = lax.axis_index("x"), lax.axis_index("y"), lax.axis_index("c")
    px = 1 - x if k & 4 else x
    py = 1 - y if k & 2 else y
    pc = 1 - c if k & 1 else c
    return (px, py, pc), 4 * px + 2 * py + pc


def _all_gather(x, name):
    R, C = x.shape

    def body(x_ref, o_ref, send_sems, recv_sems, local_sem):
        me = 4 * lax.axis_index("x") + 2 * lax.axis_index("y") + lax.axis_index("c")
        mine = pltpu.make_async_copy(x_ref, o_ref.at[me], local_sem)
        mine.start()
        sends = []
        for k in range(1, N_DEV):
            dev, _ = _peer(k)
            cp = pltpu.make_async_remote_copy(src_ref=x_ref, dst_ref=o_ref.at[me], send_sem=send_sems.at[k - 1],
                                              recv_sem=recv_sems.at[k - 1], device_id=dev, device_id_type=MESH_IDS)
            cp.start()
            sends.append(cp)
        for k in range(1, N_DEV):
            dev, idx = _peer(k)
            pltpu.make_async_remote_copy(src_ref=x_ref, dst_ref=o_ref.at[idx], send_sem=send_sems.at[k - 1],
                                         recv_sem=recv_sems.at[k - 1], device_id=dev,
                                         device_id_type=MESH_IDS).wait_recv()
        for cp in sends:
            cp.wait_send()
        mine.wait()

    return pl.pallas_call(
        body, name=name, in_specs=[pl.BlockSpec(memory_space=pl.ANY)], out_specs=pl.BlockSpec(memory_space=pl.ANY),
        out_shape=jax.ShapeDtypeStruct((N_DEV, R, C), x.dtype),
        scratch_shapes=[pltpu.SemaphoreType.DMA((N_DEV - 1,)), pltpu.SemaphoreType.DMA((N_DEV - 1,)),
                        pltpu.SemaphoreType.DMA],
        compiler_params=pltpu.CompilerParams(has_side_effects=True))(x)


HBM_SPEC = pl.BlockSpec(memory_space=pltpu.HBM)
SEM_SPEC = pl.BlockSpec(memory_space=pltpu.SEMAPHORE)
EFFECT = pltpu.SideEffectType.DATAFLOW_SIDE_EFFECTING
OTHER_CHIPS = ((1, 0), (0, 1), (1, 1))


def _split_start(bufs, plan, n, name):
    nb = len(bufs)

    def body(*refs):
        send_sems, recv_sems, token = refs[nb], refs[nb + 1], refs[2 * nb + 2]
        for i, (src, dst, _, dev) in enumerate(plan(refs[:nb])):
            pltpu.make_async_remote_copy(src_ref=src, dst_ref=dst, send_sem=send_sems.at[i],
                                         recv_sem=recv_sems.at[i], device_id=dev, device_id_type=MESH_IDS).start()
        token[...] = jnp.zeros_like(token)

    outs = pl.pallas_call(
        body, name=name,
        out_shape=(pltpu.SemaphoreType.DMA((n,)), pltpu.SemaphoreType.DMA((n,)),
                   *[pltpu.HBM(b.shape, b.dtype) for b in bufs], jax.ShapeDtypeStruct((8, 128), F32)),
        in_specs=[HBM_SPEC] * nb,
        out_specs=(SEM_SPEC, SEM_SPEC, *[HBM_SPEC] * nb, pl.BlockSpec(memory_space=pltpu.VMEM)),
        input_output_aliases={i: 2 + i for i in range(nb)},
        compiler_params=pltpu.CompilerParams(has_side_effects=EFFECT),
    )(*[pltpu.with_memory_space_constraint(b, pltpu.HBM) for b in bufs])
    return dict(send=outs[0], recv=outs[1], bufs=list(outs[2:2 + nb]), token=outs[2 + nb], plan=plan, n=n)


def _split_wait(st, after, name):
    bufs = st["bufs"]
    nb = len(bufs)
    plan = st["plan"]
    afters = list(after) if isinstance(after, (list, tuple)) else [after]

    def body(*refs):
        send_sems, recv_sems = refs[nb], refs[nb + 1]
        for i, (src, dst, land, dev) in enumerate(plan(refs[:nb])):
            pltpu.make_async_remote_copy(src_ref=src, dst_ref=dst, send_sem=send_sems.at[i],
                                         recv_sem=recv_sems.at[i], device_id=dev,
                                         device_id_type=MESH_IDS).wait_send()
            pltpu.make_async_remote_copy(src_ref=src, dst_ref=land, send_sem=send_sems.at[i],
                                         recv_sem=recv_sems.at[i], device_id=dev,
                                         device_id_type=MESH_IDS).wait_recv()

    outs = pl.pallas_call(
        body, name=name, out_shape=tuple(pltpu.HBM(b.shape, b.dtype) for b in bufs),
        in_specs=[HBM_SPEC] * nb + [SEM_SPEC, SEM_SPEC] + [pl.BlockSpec(memory_space=pl.ANY)] * len(afters),
        out_specs=tuple([HBM_SPEC] * nb), input_output_aliases={i: i for i in range(nb)},
        compiler_params=pltpu.CompilerParams(has_side_effects=EFFECT),
    )(*bufs, st["send"], st["recv"], *afters)
    return list(outs)


def _xyc():
    return lax.axis_index("x"), lax.axis_index("y"), lax.axis_index("c")


def _flip(x, y, a, b):
    return (1 - x if a else x), (1 - y if b else y)


def _ag1_plan(outs):
    x, y, c = _xyc()
    me = 4 * x + 2 * y + c
    copies = []
    for o in outs:
        copies.append((o.at[me], o.at[me], o.at[4 * x + 2 * y + 1 - c], (x, y, 1 - c)))
        for a, b in OTHER_CHIPS:
            px, py = _flip(x, y, a, b)
            copies.append((o.at[me], o.at[me], o.at[4 * px + 2 * py + c], (px, py, c)))
    return copies


def _ag2_plan(outs):
    x, y, c = _xyc()
    copies = []
    for o in outs:
        for a, b in OTHER_CHIPS:
            px, py = _flip(x, y, a, b)
            mine, sibs = 4 * px + 2 * py + c, 4 * px + 2 * py + 1 - c
            copies.append((o.at[mine], o.at[mine], o.at[sibs], (x, y, 1 - c)))
    return copies


def _rs1_plan(refs):
    x, y, c = _xyc()
    copies = []
    for g, land in zip(refs[0::2], refs[1::2]):
        for j in range(4):
            copies.append((g.at[2 * j + 1 - c], land.at[j], land.at[j], (x, y, 1 - c)))
    return copies


def _rs2_plan(refs):
    x, y, c = _xyc()
    mychip = 2 * x + y
    copies = []
    for s, land in zip(refs[0::2], refs[1::2]):
        for a, b in OTHER_CHIPS:
            px, py = _flip(x, y, a, b)
            copies.append((s.at[2 * px + py], land.at[mychip], land.at[2 * px + py], (px, py, c)))
    return copies


def _landing_like(g, name):
    def body(g_ref, o_ref):
        del g_ref, o_ref

    anyspec = pl.BlockSpec(memory_space=pl.ANY)
    return pl.pallas_call(body, name=name, in_specs=[anyspec], out_specs=anyspec,
                          out_shape=jax.ShapeDtypeStruct((4,) + g.shape[1:], g.dtype))(g)


def _place(x, slots, by_chip, name):
    R, C = x.shape[-2:]
    rt = _row_tile(R, 512)
    xi, yi, ci = _xyc()
    idx = (2 * xi + yi if by_chip else 4 * xi + 2 * yi + ci).astype(jnp.int32).reshape(1)

    def body(i_ref, x_ref, o_ref):
        o_ref[...] = x_ref[...]

    if by_chip:
        in_spec = pl.BlockSpec((None, rt, C), lambda r, i: (i[0], r, 0))
    else:
        in_spec = pl.BlockSpec((rt, C), lambda r, i: (r, 0))
    grid_spec = pltpu.PrefetchScalarGridSpec(
        num_scalar_prefetch=1, grid=(R // rt,), in_specs=[in_spec],
        out_specs=pl.BlockSpec((None, rt, C), lambda r, i: (i[0], r, 0)))
    return pl.pallas_call(body, name=name, grid_spec=grid_spec,
                          out_shape=jax.ShapeDtypeStruct((slots, R, C), x.dtype),
                          compiler_params=_cp(("parallel",)))(idx, x)


def _pair_sum(g, land, cidx, name):
    _, R, C = land.shape
    rt = _row_tile(R, 512)
    g4 = g.reshape(4, 2, R, C)

    def body(c_ref, g_ref, l_ref, o_ref):
        o_ref[...] = (g_ref[...].astype(F32) + l_ref[...].astype(F32)).astype(o_ref.dtype)

    grid_spec = pltpu.PrefetchScalarGridSpec(
        num_scalar_prefetch=1, grid=(4, R // rt),
        in_specs=[pl.BlockSpec((None, None, rt, C), lambda j, r, c_ref: (j, c_ref[0], r, 0)),
                  pl.BlockSpec((None, rt, C), lambda j, r, c_ref: (j, r, 0))],
        out_specs=pl.BlockSpec((None, rt, C), lambda j, r, c_ref: (j, r, 0)))
    return pl.pallas_call(body, name=name, grid_spec=grid_spec, out_shape=jax.ShapeDtypeStruct(land.shape, land.dtype),
                          compiler_params=_cp(("parallel", "parallel")))(cidx, g4, land)


def _no_hook(event, l, after, payload=None):
    return None


def _tie(x, token):
    if token is None:
        return x

    def body(x_ref, t_ref, o_ref):
        del x_ref, t_ref, o_ref

    anyspec = pl.BlockSpec(memory_space=pl.ANY)
    return pl.pallas_call(body, name="tie", in_specs=[anyspec, anyspec], out_specs=anyspec,
                          out_shape=jax.ShapeDtypeStruct(x.shape, x.dtype), input_output_aliases={0: 0})(x, token)


def _layer_fwd(x, W, l, hook=_no_hook):
    T = x.shape[0]
    n = f"l{l}_"
    h1, h1t = _norm_fwd(x, W["norm1"], n + "norm1_fwd")
    proj = _mm_nn(h1, W["win"], F32, n + "mm_in", tn_c=(768,))
    y_pool = _pool_fwd(proj, W["pool_w"], W["pool_b"], W["pool_s"])
    cpre = _conv_fwd(proj, C_QKV, 3 * GDN_W, W["gconv_w"], None, 256, n + "gdn_conv_fwd")
    qkv, bb, gb = _gdn_pre_fwd(cpre, proj, W["alog"], W["dtb"])
    mixed, states, tms = _gdn_fwd(qkv, bb, gb, proj, W["gnorm"], y_pool)
    lconv_w = _tie(W["lconv_w"], hook("f_mix", l, states))
    xc = _conv_fwd(proj, C_XR, LRU_W, lconv_w, W["lconv_b"], 128, n + "lru_conv_fwd")
    mixed, hst = _lru_fwd(xc, proj, W["wa"], W["ba"], W["wx"], W["bx"], W["lam"], mixed)
    hook("f_out", l, hst)
    x1 = _mm_nn(mixed, W["wout"], F32, n + "mm_out", add=x)
    h2, h2t = _norm_fwd(x1, W["norm2"], n + "norm2_fwd")
    up = _mm_up(h2, W["wup"], n + "mm_up")
    act, act_t = _ffn_gate_fwd(up, W["fconv_w"])
    act = _tie(act, hook("f_act", l, act_t))
    x2 = _mm_nn(act, W["wdown"], F32, n + "mm_down", add=x1)
    hook("f_end", l, x2)
    saved = dict(x=x, h1t=h1t, proj=proj, cpre=cpre, qkv=qkv, bb=bb, gb=gb, states=states, tms=tms, xc=xc, hst=hst,
                 mixed=mixed, x1=x1, h2t=h2t, up=up, act_t=act_t)
    return x2, saved


def _layer_bwd(dx2, dx2b, W, S, l, hook=_no_hook):
    T = dx2.shape[0]
    n = f"l{l}_"
    dact = _mm_nt(dx2b, W["wdown"], BF16, n + "mm_down_dx", tk_c=(2048,))
    g_wdown = _mm_nn(S["act_t"], dx2b, BF16, n + "mm_down_dw")
    dup, g_fconv = _ffn_gate_bwd(dact, S["up"], W["fconv_w"])
    ns = W["wup"].shape[2]
    dh2 = _mm_up_t(dup, W["wup"], n + "mm_up_dx")
    g_wup = _mm_dup(S["h2t"], dup, ns, n + "mm_up_dw")
    tok = hook("b_ffn", l, g_wup, dict(ffn_down=g_wdown, ffn_up=g_wup))
    dx1, dx1b, g_norm2 = _norm_bwd(S["x1"], _tie(W["norm2"], tok), dh2, dx2, n + "norm2_bwd")
    dmixed = _mm_nt(dx1b, W["wout"], F32, n + "mm_out_dx", tk_c=(2048,))
    g_wout = _mm_tn(S["mixed"], dx1b, BF16, n + "mm_out_dw")
    tok = hook("b_mid", l, g_wout)
    proj = S["proj"]
    dproj, g_pool_w, g_pool_b, g_pool_s = _pool_bwd(dmixed, proj, W["pool_w"], W["pool_b"], _tie(W["pool_s"], tok))
    dq, dk, dv, dbb, dgb, dproj, g_gnorm = _gdn_bwd(S["qkv"], S["bb"], S["gb"], proj, W["gnorm"], S["states"],
                                                    S["tms"], dmixed, dproj)
    dc, dproj, g_alog, g_dtb = _gdn_pre_bwd(dq, dk, dv, S["cpre"], dbb, dgb, proj, W["alog"], W["dtb"], dproj)
    dproj, g_gconv = _conv_bwd(dc, proj, C_QKV, W["gconv_w"], 256, n + "gdn_conv_bwd", dproj)
    dxc, dproj, g_wa, g_wx, g_ba, g_bx, g_lam = _lru_bwd(dmixed, S["xc"], proj, S["hst"], W["wa"], W["ba"], W["wx"],
                                                          W["bx"], W["lam"], dproj)
    dproj, g_lconv, g_lconv_b = _conv_bwd(dxc, proj, C_XR, W["lconv_w"], 128, n + "lru_conv_bwd", dproj,
                                          want_db=True)
    dh1 = _mm_nt(dproj, W["win"], F32, n + "mm_in_dx", tk_c=(1792,))
    g_win = _mm_nn(S["h1t"], dproj, BF16, n + "mm_in_dw", tn_c=(768,))
    tok = hook("b_in", l, g_win, dict(w_out=g_wout, w_in=g_win))
    dx, dxb, g_norm1 = _norm_bwd(S["x"], _tie(W["norm1"], tok), dh1, dx1, n + "norm1_bwd")
    big = dict(w_in=g_win, w_out=g_wout, ffn_up=g_wup, ffn_down=g_wdown)
    small = dict(norm1_w=g_norm1[0], pool_w=g_pool_w, pool_b=g_pool_b.reshape(4, 128), pool_scale=g_pool_s[0],
                 gdn_conv_w=g_gconv, gdn_a_log=g_alog[0, :HEADS], gdn_dt_bias=g_dtb[0, :HEADS],
                 gdn_norm_w=g_gnorm[0], lru_conv_w=g_lconv, lru_conv_b=g_lconv_b[0], lru_wa=g_wa, lru_ba=g_ba[0],
                 lru_wx=g_wx, lru_bx=g_bx[0], lru_lambda=g_lam[0], norm2_w=g_norm2[0], ffn_conv_w=g_fconv)
    dxb = _tie(dxb, hook("b_end", l, dx, small))
    return dx, dxb, big, small


def _pad_lane(v):
    return jnp.pad(v, (0, 128 - v.shape[0])).reshape(1, 128)


def _layer_weights(l, big, P, conv_full):
    return dict(
        win=big.get("w_in"), wout=big.get("w_out"), wup=big.get("ffn_up"), wdown=big.get("ffn_down"),
        norm1=P["norm1_w"][l].reshape(1, D_MODEL), norm2=P["norm2_w"][l].reshape(1, D_MODEL),
        pool_w=P["pool_w"][l], pool_b=P["pool_b"][l].reshape(1, POOL_W), pool_s=P["pool_scale"][l].reshape(1, POOL_W),
        gconv_w=conv_full["gdn_conv_w"][l], alog=_pad_lane(P["gdn_a_log"][l]), dtb=_pad_lane(P["gdn_dt_bias"][l]),
        gnorm=P["gdn_norm_w"][l].reshape(1, HD),
        lconv_w=conv_full["lru_conv_w"][l], lconv_b=P["lru_conv_b"][l].reshape(1, LRU_W),
        wa=P["lru_wa"][l], ba=P["lru_ba"][l].reshape(1, LRU_W), wx=P["lru_wx"][l],
        bx=P["lru_bx"][l].reshape(1, LRU_W), lam=P["lru_lambda"][l].reshape(1, LRU_W),
        fconv_w=conv_full["ffn_conv_w"][l])


def _local_step(x, target, Ws, final_norm_w, hook=_no_hook):
    saved = []
    for l in range(DEPTH):
        x, s = _layer_fwd(x, Ws[l], l, hook)
        saved.append(s)
    loss, dx, dxb, g_final = _loss_head(x, final_norm_w.reshape(1, D_MODEL), target)
    bigs, smalls = [None] * DEPTH, [None] * DEPTH
    for l in reversed(range(DEPTH)):
        dx, dxb, bigs[l], smalls[l] = _layer_bwd(dx, dxb, Ws[l], saved[l], l, hook)
    return loss, dx, g_final[0], bigs, smalls


SMALL_REPL = ("norm1_w", "pool_w", "pool_b", "pool_scale", "gdn_a_log", "gdn_dt_bias", "gdn_norm_w", "lru_conv_b",
              "lru_wa", "lru_ba", "lru_wx", "lru_bx", "lru_lambda", "norm2_w", "final_norm_w")
SMALL_SHARD = ("gdn_conv_w", "lru_conv_w", "ffn_conv_w")
BIG = ("w_in", "w_out", "ffn_up", "ffn_down")
WEIGHTS = ("norm1_w", "w_in", "pool_w", "pool_b", "pool_scale", "gdn_conv_w", "gdn_a_log", "gdn_dt_bias",
           "gdn_norm_w", "lru_conv_w", "lru_conv_b", "lru_wa", "lru_ba", "lru_wx", "lru_bx", "lru_lambda", "w_out",
           "norm2_w", "ffn_up", "ffn_conv_w", "ffn_down", "final_norm_w")
SEG = 1024
PACK_ROWS_MULT = 256 * 128


def _pack(arrs):
    pieces, table, off = [], [], 0
    for a in arrs:
        n = a.size
        npad = -(-n // SEG) * SEG
        pieces.append(jnp.pad(a.reshape(-1).astype(F32), (0, npad - n)))
        table.append((off, n, a.shape))
        off += npad
    tail = -off % PACK_ROWS_MULT
    if tail:
        pieces.append(jnp.zeros((tail,), F32))
        off += tail
    return jnp.concatenate(pieces).reshape(off // 128, 128), table


def _unpack(buf, table):
    flat = buf.reshape(-1)
    return [flat[off:off + n].reshape(shape) for off, n, shape in table]


def _pad_in(w):
    z1 = jnp.zeros(w.shape[:-1] + (C_XR - AB_ORIG_END,), w.dtype)
    return jnp.concatenate([w[..., :AB_ORIG_END], z1, w[..., AB_ORIG_END:]], axis=-1)


def _unpad_in(w):
    return jnp.concatenate([w[..., :AB_ORIG_END], w[..., C_XR:C_GR + LRU_W]], axis=-1)


def kernel(x, norm1_w, w_in, pool_w, pool_b, pool_scale, gdn_conv_w, gdn_a_log, gdn_dt_bias, gdn_norm_w, lru_conv_w, lru_conv_b, lru_wa, lru_ba, lru_wx, lru_bx, lru_lambda, w_out, norm2_w, ffn_up, ffn_conv_w, ffn_down, final_norm_w, loss_target, m_norm1_w, m_w_in, m_pool_w, m_pool_b, m_pool_scale, m_gdn_conv_w, m_gdn_a_log, m_gdn_dt_bias, m_gdn_norm_w, m_lru_conv_w, m_lru_conv_b, m_lru_wa, m_lru_ba, m_lru_wx, m_lru_bx, m_lru_lambda, m_w_out, m_norm2_w, m_ffn_up, m_ffn_conv_w, m_ffn_down, m_final_norm_w, v_norm1_w, v_w_in, v_pool_w, v_pool_b, v_pool_scale, v_gdn_conv_w, v_gdn_a_log, v_gdn_dt_bias, v_gdn_norm_w, v_lru_conv_w, v_lru_conv_b, v_lru_wa, v_lru_ba, v_lru_wx, v_lru_bx, v_lru_lambda, v_w_out, v_norm2_w, v_ffn_up, v_ffn_conv_w, v_ffn_down, v_final_norm_w):
    loc = dict(locals())
    Wp = {n: loc[n] for n in WEIGHTS}
    Mp = {n: loc["m_" + n] for n in WEIGHTS}
    Vp = {n: loc["v_" + n] for n in WEIGHTS}
    xi, yi, ci = _xyc()
    me = 4 * xi + 2 * yi + ci
    mychip = 2 * xi + yi
    cidx = ci.astype(jnp.int32).reshape(1)
    keys = dict(w_in="win", w_out="wout", ffn_up="wup", ffn_down="wdown")

    def shard2d(d, name, l):
        a = d[name][l]
        return _pad_in(a) if name == "w_in" else a

    def wshard(l, name):
        return shard2d(Wp, name, l).astype(BF16)

    def full2d(name, full):
        return full if name == "ffn_up" else full.reshape(-1, full.shape[2])

    def ag_start(shards, tag, token=None):
        if token is not None:
            shards = [_tie(shards[0], token)] + list(shards[1:])
        bufs = [_place(s, N_DEV, False, f"place_{tag}{i}") for i, s in enumerate(shards)]
        return _split_start(bufs, _ag1_plan, 4 * len(bufs), f"ag1s_{tag}")

    def ag_mid(st, after, tag):
        bufs = _split_wait(st, after, f"ag1w_{tag}")
        return _split_start(bufs, _ag2_plan, 3 * len(bufs), f"ag2s_{tag}")

    def ag_end(st, after, tag):
        return _split_wait(st, after, f"ag2w_{tag}")

    def rs_start(gs, tag):
        bufs = []
        for nm, g in gs.items():
            if nm != "ffn_up":
                g = g.reshape(N_DEV, g.shape[0] // N_DEV, g.shape[1])
            bufs += [g, _landing_like(g, f"land_{nm}_{tag}")]
        st = _split_start(bufs, _rs1_plan, 4 * len(gs), f"rs1s_{tag}")
        st["names"] = list(gs)
        return st

    def rs_mid(st, after, tag):
        bufs = _split_wait(st, after, f"rs1w_{tag}")
        out = []
        for i, nm in enumerate(st["names"]):
            s = _pair_sum(bufs[2 * i], bufs[2 * i + 1], cidx, f"pairsum_{nm}_{tag}")
            out += [s, _place(s, 4, True, f"place_{nm}_{tag}")]
        st2 = _split_start(out, _rs2_plan, 3 * len(st["names"]), f"rs2s_{tag}")
        st2["names"] = st["names"]
        return st2

    def rs_end(st, after, tag):
        bufs = _split_wait(st, after, f"rs2w_{tag}")
        return dict(zip(st["names"], bufs[1::2]))

    lnames = tuple(n for n in SMALL_REPL if n != "final_norm_w") + SMALL_SHARD

    def small_pack(l, gs, extra):
        return _pack([gs[nm] for nm in lnames] + extra)

    def small_state(d, l, gs):
        arrs = [d[nm][l] if nm in SMALL_REPL else jnp.zeros(gs[nm].shape, F32) for nm in lnames]
        if l == 0:
            arrs += [d["final_norm_w"], jnp.zeros((1,), F32)]
        return _pack(arrs)[0]

    stA = ag_start([wshard(0, "w_in")], "a")
    stA2 = ag_mid(stA, stA["token"], "a")
    stB = ag_start([wshard(0, n) for n in BIG[1:]], "b", stA2["token"])
    (w_in0,) = ag_end(stA2, stB["token"], "a")

    cbuf, ctable = _pack([Wp[n] for n in SMALL_SHARD])
    call = _all_gather(cbuf, "ag_conv_w")
    parts = [_unpack(call[i], ctable) for i in range(N_DEV)]
    conv_full = {n: jnp.concatenate([parts[i][j] for i in range(N_DEV)], axis=-1) for j, n in enumerate(SMALL_SHARD)}

    Ws = [_layer_weights(l, {}, Wp, conv_full) for l in range(DEPTH)]
    Ws[0]["win"] = full2d("w_in", w_in0)
    st = {}

    def hook(event, l, after, payload=None):
        if event == "f_mix" and l == 0:
            st["b2"] = ag_mid(stB, after, "b")
            st["c"] = ag_start([wshard(1, n) for n in BIG], "c", st["b2"]["token"])
            return st["c"]["token"]
        if event == "f_out" and l == 0:
            for n, b in zip(BIG[1:], ag_end(st["b2"], after, "b")):
                Ws[0][keys[n]] = full2d(n, b)
        if event == "f_act" and l == 0:
            st["c2"] = ag_mid(st["c"], after, "c")
            return st["c2"]["token"]
        if event == "f_end" and l == 0:
            for n, b in zip(BIG, ag_end(st["c2"], after, "c")):
                Ws[1][keys[n]] = full2d(n, b)
        if event == "b_ffn":
            st["ffn", l] = rs_start(payload, f"ffn{l}")
            return st["ffn", l]["token"]
        if event == "b_mid":
            st["ffn2", l] = rs_mid(st["ffn", l], after, f"ffn{l}")
            if l == 0:
                st["sm1b"] = ag_mid(st["sm1"], st["ffn2", l]["token"], "sm1")
                return st["sm1b"]["token"]
            return st["ffn2", l]["token"]
        if event == "b_in":
            st["io", l] = rs_start(payload, f"io{l}")
            if l == 0:
                st["sm1g"] = ag_end(st["sm1b"], st["io", l]["token"], "sm1")[0]
            return st["io", l]["token"]
        if event == "b_end" and l == 1:
            st["io2", 1] = rs_mid(st["io", 1], after, "io1")
            gbuf1, st["table1"] = small_pack(1, payload, [])
            st["sm1"] = ag_start([gbuf1], "sm1", st["io2", 1]["token"])
            return st["sm1"]["token"]
        return None

    loss, dx, g_final, _, gsmall = _local_step(x[0], loss_target[0], Ws, final_norm_w, hook)

    out_g, out_d, out_m, out_v = {}, {}, {}, {}
    outs4 = (out_g, out_d, out_m, out_v)
    rts = dict(w_in=64, w_out=128, ffn_up=256, ffn_down=128)
    big_res = {}

    def stacked(d, name):
        return _pad_in(d[name]) if name == "w_in" else d[name]

    wmv = {name: [stacked(d, name) for d in (Wp, Mp, Vp)] for name in BIG}

    def adam_big(l, parts):
        for name, p in parts.items():
            big_res[name] = _adamw(p, *wmv[name], rts[name], f"adamw_{name}_{l}", layer=l, prev=big_res.get(name))
        return [big_res[name][0] for name in parts]

    def adam_small(l, gall, gs):
        rs = gall.shape[1]
        return _adamw(gall, small_state(Wp, l, gs), small_state(Mp, l, gs), small_state(Vp, l, gs),
                      _row_tile(rs, 512), f"adamw_small_{l}")

    gbuf0, table0 = small_pack(0, gsmall[0], [g_final, loss[0, :1]])
    sm0 = ag_start([gbuf0], "sm0", st["io", 0]["token"])
    o = adam_big(1, rs_end(st["ffn2", 1], sm0["token"], "ffn1"))
    st["io2", 0] = rs_mid(st["io", 0], o, "io0")
    o = adam_big(1, rs_end(st["io2", 1], st["io2", 0]["token"], "io1"))
    o = adam_big(0, rs_end(st["ffn2", 0], o, "ffn0"))
    small_res = {1: adam_small(1, _tie(st["sm1g"], o[-1]), gsmall[1])}
    sm0b = ag_mid(sm0, o + [small_res[1][0]], "sm0")
    small_res[0] = adam_small(0, ag_end(sm0b, sm0b["token"], "sm0")[0], gsmall[0])
    adam_big(0, rs_end(st["io2", 0], small_res[0][0], "io0"))

    for name in BIG:
        for i, dst in enumerate(outs4):
            dst[name] = _unpad_in(big_res[name][i]) if name == "w_in" else big_res[name][i]

    unp = {0: [_unpack(r, table0) for r in small_res[0]], 1: [_unpack(r, st["table1"]) for r in small_res[1]]}
    for j, nm in enumerate(lnames):
        if nm in SMALL_REPL:
            for i, dst in enumerate(outs4):
                dst[nm] = jnp.stack([unp[l][i][j] for l in range(DEPTH)])
    for i, dst in enumerate(outs4):
        dst["final_norm_w"] = unp[0][i][len(lnames)]
    loss_total = unp[0][0][len(lnames) + 1][0]

    gsh = []
    for nm in SMALL_SHARD:
        j = lnames.index(nm)
        width = Wp[nm].shape[-1]
        gsh.append(jnp.stack([lax.dynamic_slice_in_dim(unp[l][0][j], me * width, width, axis=1)
                              for l in range(DEPTH)]))
    sbuf, stable = _pack(gsh)
    res = _adamw(sbuf[None], _pack([Wp[n] for n in SMALL_SHARD])[0], _pack([Mp[n] for n in SMALL_SHARD])[0],
                 _pack([Vp[n] for n in SMALL_SHARD])[0], sbuf.shape[0], "adamw_conv_w")
    unp2 = [_unpack(r, stable) for r in res]
    for j, nm in enumerate(SMALL_SHARD):
        for i, dst in enumerate((out_g, out_d, out_m, out_v)):
            dst[nm] = unp2[i][j]

    return (loss_total, dx[None], *[out_g[n] for n in WEIGHTS], *[out_d[n] for n in WEIGHTS],
            *[out_m[n] for n in WEIGHTS], *[out_v[n] for n in WEIGHTS])
```

```python
import functools

import jax
import jax.numpy as jnp
from jax import lax
from jax.experimental import pallas as pl
from jax.experimental.pallas import tpu as pltpu

F32 = jnp.float32
BF16 = jnp.bfloat16
HI = lax.Precision.HIGHEST
MESH_IDS = pl.DeviceIdType.MESH

N_DEV = 8
D_MODEL = 2048
DEPTH = 2
POOL_WINDOWS = (2, 4, 8, 16)
POOL_W = 512
HEADS = 6
HD = 128
GDN_W = HEADS * HD
CHUNK = 64
LRU_W = 768
LRU_C = 8.0
D_FF = 3 * D_MODEL
EPS = 1e-6
IN_COLS = 5132
PCOLS = 5376
C_QKV, C_Z, C_AB, C_XR, C_GR = 512, 2816, 3584, 3840, 4608
AB_ORIG_END = 3596
M_GDN, M_LRU = 512, 1280

ADAM_LR, ADAM_B1, ADAM_B2, ADAM_EPS, ADAM_WD, ADAM_STEP = 0.001, 0.9, 0.999, 1e-08, 0.01, 10

VMEM_LIMIT = 56 * 1024 * 1024


def _cp(sem):
    return pltpu.CompilerParams(dimension_semantics=sem, vmem_limit_bytes=VMEM_LIMIT)


def _mm(a, b, ca=1, cb=0, prec=None, cast=True):
    if cast:
        a = a.astype(BF16)
        b = b.astype(BF16)
    return lax.dot_general(a, b, (((ca,), (cb,)), ((), ())), preferred_element_type=F32, precision=prec)


def _bmm(a, b, ca=2, cb=1, prec=None, cast=True):
    if cast:
        a = a.astype(BF16)
        b = b.astype(BF16)
    return lax.dot_general(a, b, (((ca,), (cb,)), ((0,), (0,))), preferred_element_type=F32, precision=prec)


def _sigmoid(x):
    return 1.0 / (1.0 + jnp.exp(-x))


def _log1p(e):
    u = 1.0 + e
    return jnp.where(u == 1.0, e, jnp.log(u) * e / jnp.where(u == 1.0, 1.0, u - 1.0))


def _softplus(x):
    return jnp.maximum(x, 0.0) + _log1p(jnp.exp(-jnp.abs(x)))


def _expm1(x):
    u = jnp.exp(x)
    um = u - 1.0
    safe = jnp.where((u == 1.0) | (um == -1.0), 1.0, jnp.log(u))
    return jnp.where(u == 1.0, x, jnp.where(um == -1.0, -1.0, um * x / safe))


_G0 = 0.7978845608028654
_G1 = 0.044715


def _gelu(x):
    return 0.5 * x * (1.0 + jnp.tanh(_G0 * (x + _G1 * x * x * x)))


def _gelu_and_grad(x):
    th = jnp.tanh(_G0 * (x + _G1 * x * x * x))
    g = 0.5 * x * (1.0 + th)
    dg = 0.5 * (1.0 + th) + 0.5 * x * (1.0 - th * th) * _G0 * (1.0 + 3.0 * _G1 * x * x)
    return g, dg


def _tile(T):
    return min(T, 512)


def _matmul(a, b, *, grid, a_spec, b_spec, out_shape, out_spec, dims, acc_shape, name, add=None, add_spec=None):
    nk = grid[2]
    has_add = add is not None

    def body(*refs):
        if has_add:
            a_ref, b_ref, add_ref, o_ref, acc_ref = refs
        else:
            a_ref, b_ref, o_ref, acc_ref = refs
            add_ref = None
        k = pl.program_id(2)
        p = lax.dot_general(a_ref[...].astype(BF16), b_ref[...].astype(BF16), (dims, ((), ())),
                            preferred_element_type=F32)

        def finish(r):
            if has_add:
                r = r + add_ref[...]
            o_ref[...] = r.astype(o_ref.dtype)

        if nk == 1:
            finish(p)
        else:
            @pl.when(k == 0)
            def _():
                acc_ref[...] = p

            @pl.when(k > 0)
            def _():
                acc_ref[...] += p

            @pl.when(k == nk - 1)
            def _():
                finish(acc_ref[...])

    in_specs = [a_spec, b_spec] + ([add_spec] if has_add else [])
    args = (a, b) + ((add,) if has_add else ())
    return pl.pallas_call(
        body, name=name, grid=grid, in_specs=in_specs, out_specs=out_spec, out_shape=out_shape,
        scratch_shapes=[pltpu.VMEM(acc_shape if nk > 1 else (8, 128), F32)],
        compiler_params=_cp(("parallel", "parallel", "arbitrary")),
    )(*args)


def _pick(n, cands):
    for c in cands:
        if n % c == 0:
            return c
    raise ValueError(f"no tile for {n}")


def _mm_nn(a, b, out_dtype, name, add=None, tn_c=(1024, 768, 512), tk_c=(2048, 1536, 1024, 512, 256)):
    M, K = a.shape
    N = b.shape[1]
    tm = _pick(M, (1024, 512, 256))
    tn = _pick(N, tn_c)
    tk = _pick(K, tk_c)
    return _matmul(
        a, b, grid=(M // tm, N // tn, K // tk),
        a_spec=pl.BlockSpec((tm, tk), lambda i, j, k: (i, k)),
        b_spec=pl.BlockSpec((tk, tn), lambda i, j, k: (k, j)),
        out_shape=jax.ShapeDtypeStruct((M, N), out_dtype),
        out_spec=pl.BlockSpec((tm, tn), lambda i, j, k: (i, j)),
        dims=((1,), (0,)), acc_shape=(tm, tn), name=name, add=add,
        add_spec=pl.BlockSpec((tm, tn), lambda i, j, k: (i, j)))


def _mm_nt(a, b, out_dtype, name, tk_c=(2048, 1536, 1024, 768, 512), tn_c=(1024, 768, 512)):
    M, K = a.shape
    N = b.shape[0]
    tm = _pick(M, (1024, 512, 256))
    tn = _pick(N, tn_c)
    tk = _pick(K, tk_c)
    return _matmul(
        a, b, grid=(M // tm, N // tn, K // tk),
        a_spec=pl.BlockSpec((tm, tk), lambda i, j, k: (i, k)),
        b_spec=pl.BlockSpec((tn, tk), lambda i, j, k: (j, k)),
        out_shape=jax.ShapeDtypeStruct((M, N), out_dtype),
        out_spec=pl.BlockSpec((tm, tn), lambda i, j, k: (i, j)),
        dims=((1,), (1,)), acc_shape=(tm, tn), name=name)


def _mm_tn(a, b, out_dtype, name, tn_c=(1024, 768, 512)):
    K, M = a.shape
    N = b.shape[1]
    tm = _pick(M, (1024, 768, 512))
    tn = _pick(N, tn_c)
    tk = _pick(K, (1024, 512, 256))
    return _matmul(
        a, b, grid=(M // tm, N // tn, K // tk),
        a_spec=pl.BlockSpec((tk, tm), lambda i, j, k: (k, i)),
        b_spec=pl.BlockSpec((tk, tn), lambda i, j, k: (k, j)),
        out_shape=jax.ShapeDtypeStruct((M, N), out_dtype),
        out_spec=pl.BlockSpec((tm, tn), lambda i, j, k: (i, j)),
        dims=((0,), (0,)), acc_shape=(tm, tn), name=name)


def _mm_up(h, wup, name):
    M, K = h.shape
    ns = wup.shape[2]
    tm = _pick(M, (1024, 512, 256))
    tn = ns
    per = ns // tn
    return _matmul(
        h, wup, grid=(M // tm, N_DEV * per, 1),
        a_spec=pl.BlockSpec((tm, K), lambda i, j, k: (i, 0)),
        b_spec=pl.BlockSpec((None, K, tn), lambda i, j, k: (j // per, 0, j % per)),
        out_shape=jax.ShapeDtypeStruct((M, N_DEV * ns), BF16),
        out_spec=pl.BlockSpec((tm, tn), lambda i, j, k: (i, j)),
        dims=((1,), (0,)), acc_shape=(tm, tn), name=name)


def _mm_up_t(dup, wup, name):
    M = dup.shape[1]
    D, ns = wup.shape[1], wup.shape[2]
    tm = _pick(M, (1024, 512, 256))
    tn = D
    tk = ns
    return _matmul(
        dup, wup, grid=(M // tm, D // tn, N_DEV),
        a_spec=pl.BlockSpec((None, tm, tk), lambda i, j, k: (k // 4, i, k % 4)),
        b_spec=pl.BlockSpec((None, tn, tk), lambda i, j, k: (k, j, 0)),
        out_shape=jax.ShapeDtypeStruct((M, D), F32),
        out_spec=pl.BlockSpec((tm, tn), lambda i, j, k: (i, j)),
        dims=((1,), (1,)), acc_shape=(tm, tn), name=name)


def _mm_dup(ht, dup, ns, name):
    M, K = ht.shape
    tm = 1024
    tn = ns
    per = ns // tn
    half = 4 * per
    tk = _pick(K, (2048, 1024, 512, 256))
    return _matmul(
        ht, dup, grid=(M // tm, N_DEV * per, K // tk),
        a_spec=pl.BlockSpec((tm, tk), lambda i, j, k: (i, k)),
        b_spec=pl.BlockSpec((None, tk, tn), lambda i, j, k: (j // half, k, j % half)),
        out_shape=jax.ShapeDtypeStruct((N_DEV, M, ns), BF16),
        out_spec=pl.BlockSpec((None, tm, tn), lambda i, j, k: (j // per, i, j % per)),
        dims=((1,), (0,)), acc_shape=(tm, tn), name=name)


def _norm_fwd(x, w, name):
    T, D = x.shape
    tt = _tile(T)

    def body(x_ref, w_ref, h_ref, ht_ref):
        xv = x_ref[...]
        r = lax.rsqrt(jnp.mean(xv * xv, axis=1, keepdims=True) + EPS)
        hv = xv * r * w_ref[...]
        h_ref[...] = hv.astype(BF16)
        ht_ref[...] = hv.T.astype(BF16)

    return pl.pallas_call(
        body, name=name, grid=(T // tt,),
        in_specs=[pl.BlockSpec((tt, D), lambda t: (t, 0)), pl.BlockSpec((1, D), lambda t: (0, 0))],
        out_specs=[pl.BlockSpec((tt, D), lambda t: (t, 0)), pl.BlockSpec((D, tt), lambda t: (0, t))],
        out_shape=[jax.ShapeDtypeStruct((T, D), BF16), jax.ShapeDtypeStruct((D, T), BF16)],
        compiler_params=_cp(("parallel",)))(x, w)


def _norm_bwd(x, w, dh, dres, name):
    T, D = x.shape
    tt = _tile(T)

    def body(x_ref, w_ref, dh_ref, dres_ref, dx_ref, dxb_ref, dw_ref):
        t = pl.program_id(0)
        xv = x_ref[...]
        r = lax.rsqrt(jnp.mean(xv * xv, axis=1, keepdims=True) + EPS)
        xh = xv * r
        dh_v = dh_ref[...]
        dxh = dh_v * w_ref[...]
        dxv = dres_ref[...] + r * (dxh - xh * jnp.mean(dxh * xh, axis=1, keepdims=True))
        dx_ref[...] = dxv
        dxb_ref[...] = dxv.astype(BF16)
        part = jnp.sum(dh_v * xh, axis=0, keepdims=True)

        @pl.when(t == 0)
        def _():
            dw_ref[...] = part

        @pl.when(t > 0)
        def _():
            dw_ref[...] += part

    row = pl.BlockSpec((tt, D), lambda t: (t, 0))
    vec = pl.BlockSpec((1, D), lambda t: (0, 0))
    return pl.pallas_call(
        body, name=name, grid=(T // tt,), in_specs=[row, vec, row, row], out_specs=[row, row, vec],
        out_shape=[jax.ShapeDtypeStruct((T, D), F32), jax.ShapeDtypeStruct((T, D), BF16),
                   jax.ShapeDtypeStruct((1, D), F32)],
        compiler_params=_cp(("arbitrary",)))(x, w, dh, dres)


def _loss_head(x, w, target):
    T, D = x.shape
    tt = _tile(T)

    def body(x_ref, w_ref, t_ref, loss_ref, dx_ref, dxb_ref, dw_ref):
        t = pl.program_id(0)
        xv = x_ref[...]
        r = lax.rsqrt(jnp.mean(xv * xv, axis=1, keepdims=True) + EPS)
        xh = xv * r
        err = xh * w_ref[...] - t_ref[...]
        lp = 0.5 * jnp.sum(jnp.mean(err * err, axis=1, keepdims=True), axis=0, keepdims=True)
        dy = err * (1.0 / D)
        dxh = dy * w_ref[...]
        dxv = r * (dxh - xh * jnp.mean(dxh * xh, axis=1, keepdims=True))
        dx_ref[...] = dxv
        dxb_ref[...] = dxv.astype(BF16)
        part = jnp.sum(dy * xh, axis=0, keepdims=True)
        lpb = jnp.broadcast_to(lp, (1, 128))

        @pl.when(t == 0)
        def _():
            dw_ref[...] = part
            loss_ref[...] = lpb

        @pl.when(t > 0)
        def _():
            dw_ref[...] += part
            loss_ref[...] += lpb

    row = pl.BlockSpec((tt, D), lambda t: (t, 0))
    vec = pl.BlockSpec((1, D), lambda t: (0, 0))
    return pl.pallas_call(
        body, name="loss_head", grid=(T // tt,), in_specs=[row, vec, row],
        out_specs=[pl.BlockSpec((1, 128), lambda t: (0, 0)), row, row, vec],
        out_shape=[jax.ShapeDtypeStruct((1, 128), F32), jax.ShapeDtypeStruct((T, D), F32),
                   jax.ShapeDtypeStruct((T, D), BF16), jax.ShapeDtypeStruct((1, D), F32)],
        compiler_params=_cp(("arbitrary",)))(x, w, target)


CONV_TT = 4096
CONV_BWD_TT = 2048


def _conv_fwd(x, col0, C, w, b, cb, name):
    T = x.shape[0]
    K = w.shape[0]
    tt = min(T, CONV_TT)
    nt, nc, c0 = T // tt, C // cb, col0 // cb
    has_b = b is not None

    def body(*refs):
        if has_b:
            x_ref, halo_ref, w_ref, b_ref, y_ref = refs
        else:
            x_ref, halo_ref, w_ref, y_ref = refs
        t = pl.program_id(1)
        halo = jnp.where(t == 0, 0.0, halo_ref[...])
        xe = jnp.concatenate([halo, x_ref[...]], axis=0)
        acc = xe * w_ref[K - 1:K, :]
        for j in range(K - 1):
            acc = acc + pltpu.roll(xe, K - 1 - j, 0) * w_ref[j:j + 1, :]
        if has_b:
            acc = acc + b_ref[...]
        y_ref[...] = acc[8:, :]

    in_specs = [pl.BlockSpec((tt, cb), lambda j, t: (t, c0 + j)),
                pl.BlockSpec((8, cb), lambda j, t: (jnp.maximum(t * (tt // 8) - 1, 0), c0 + j)),
                pl.BlockSpec((K, cb), lambda j, t: (0, j))]
    args = [x, x, w]
    if has_b:
        in_specs.append(pl.BlockSpec((1, cb), lambda j, t: (0, j)))
        args.append(b)
    return pl.pallas_call(
        body, name=name, grid=(nc, nt), in_specs=in_specs,
        out_specs=pl.BlockSpec((tt, cb), lambda j, t: (t, j)),
        out_shape=jax.ShapeDtypeStruct((T, C), F32), compiler_params=_cp(("parallel", "parallel")))(*args)


def _conv_bwd(dy, x, col0, w, cb, name, into, want_db=False):
    T, C = dy.shape
    K = w.shape[0]
    tt = min(T, CONV_BWD_TT)
    nt, nc, c0 = T // tt, C // cb, col0 // cb

    def body(*refs):
        if want_db:
            dy_ref, dyn_ref, x_ref, xp_ref, w_ref, _, dx_ref, dw_ref, db_ref = refs
        else:
            dy_ref, dyn_ref, x_ref, xp_ref, w_ref, _, dx_ref, dw_ref = refs
        t = pl.program_id(1)
        dyv = dy_ref[...]
        nxt = jnp.where(t == nt - 1, 0.0, dyn_ref[...])
        dye = jnp.concatenate([dyv, nxt], axis=0)
        n = tt + 8
        acc = dye * w_ref[K - 1:K, :]
        for j in range(K - 1):
            acc = acc + pltpu.roll(dye, n - (K - 1 - j), 0) * w_ref[j:j + 1, :]
        dx_ref[...] = acc[:tt, :].astype(dx_ref.dtype)
        prev = jnp.where(t == 0, 0.0, xp_ref[...])
        xe = jnp.concatenate([prev, x_ref[...]], axis=0)

        @pl.when(t == 0)
        def _():
            dw_ref[...] = jnp.zeros_like(dw_ref)
            if want_db:
                db_ref[...] = jnp.zeros_like(db_ref)

        for j in range(K):
            sh = K - 1 - j
            xs = xe[8:, :] if sh == 0 else pltpu.roll(xe, sh, 0)[8:, :]
            dw_ref[j:j + 1, :] += jnp.sum(dyv * xs, axis=0, keepdims=True)
        if want_db:
            db_ref[...] += jnp.sum(dyv, axis=0, keepdims=True)

    h8 = tt // 8
    in_specs = [pl.BlockSpec((tt, cb), lambda j, t: (t, j)),
                pl.BlockSpec((8, cb), lambda j, t: (jnp.minimum((t + 1) * h8, T // 8 - 1), j)),
                pl.BlockSpec((tt, cb), lambda j, t: (t, c0 + j)),
                pl.BlockSpec((8, cb), lambda j, t: (jnp.maximum(t * h8 - 1, 0), c0 + j)),
                pl.BlockSpec((K, cb), lambda j, t: (0, j)), pl.BlockSpec(memory_space=pl.ANY)]
    out_specs = [pl.BlockSpec((tt, cb), lambda j, t: (t, c0 + j)), pl.BlockSpec((K, cb), lambda j, t: (0, j))]
    out_shape = [jax.ShapeDtypeStruct(into.shape, into.dtype), jax.ShapeDtypeStruct((K, C), F32)]
    if want_db:
        out_specs.append(pl.BlockSpec((1, cb), lambda j, t: (0, j)))
        out_shape.append(jax.ShapeDtypeStruct((1, C), F32))
    return pl.pallas_call(
        body, name=name, grid=(nc, nt), in_specs=in_specs, out_specs=out_specs, out_shape=out_shape,
        input_output_aliases={5: 0}, compiler_params=_cp(("parallel", "arbitrary")))(dy, dy, x, x, w, into)


def _pool_d(ue, g, pos, tt):
    win = POOL_WINDOWS[g]
    ug = ue[:, g * 128:(g + 1) * 128]
    s = ug
    k = 1
    while k < win:
        s = s + pltpu.roll(s, k, 0)
        k *= 2
    cnt = jnp.minimum(pos + 1, win).astype(F32)
    return s[16:, :] / cnt - ug[16:, :]


POOL_TT = 2048


def _pool_fwd(proj, pw, pb, ps):
    T = proj.shape[0]
    tt = min(T, POOL_TT)

    def body(u_ref, halo_ref, w_ref, b_ref, s_ref, y_ref):
        t = pl.program_id(0)
        halo = jnp.where(t == 0, 0.0, halo_ref[...])
        ue = jnp.concatenate([halo, u_ref[...]], axis=0)
        pos = t * tt + lax.broadcasted_iota(jnp.int32, (tt, 1), 0)
        for g in range(4):
            sl = slice(g * 128, (g + 1) * 128)
            d = _pool_d(ue, g, pos, tt)
            yg = _mm(d, w_ref[g]) + b_ref[:, sl]
            y_ref[:, sl] = (yg * s_ref[:, sl]).astype(BF16)

    vec = pl.BlockSpec((1, POOL_W), lambda t: (0, 0))
    return pl.pallas_call(
        body, name="pool_fwd", grid=(T // tt,),
        in_specs=[pl.BlockSpec((tt, POOL_W), lambda t: (t, 0)),
                  pl.BlockSpec((16, POOL_W), lambda t: (jnp.maximum(t * (tt // 16) - 1, 0), 0)),
                  pl.BlockSpec((4, 128, 128), lambda t: (0, 0, 0)), vec, vec],
        out_specs=pl.BlockSpec((tt, POOL_W), lambda t: (t, 0)),
        out_shape=jax.ShapeDtypeStruct((T, D_MODEL), BF16), compiler_params=_cp(("parallel",)))(
            proj, proj, pw, pb, ps)


def _pool_bwd(dmixed, proj, pw, pb, ps):
    T = proj.shape[0]
    tt = min(T, POOL_TT)
    nt = T // tt

    def body(dy_ref, dyn_ref, u_ref, halo_ref, w_ref, b_ref, s_ref, du_ref, dw_ref, db_ref, ds_ref):
        t = pl.program_id(0)
        halo = jnp.where(t == 0, 0.0, halo_ref[...])
        ue = jnp.concatenate([halo, u_ref[...]], axis=0)
        dyv = dy_ref[...]
        nxt = jnp.where(t == nt - 1, 0.0, dyn_ref[...])
        dye = jnp.concatenate([dyv, nxt], axis=0)
        n = tt + 16
        pos = t * tt + lax.broadcasted_iota(jnp.int32, (tt, 1), 0)
        pos_e = t * tt + lax.broadcasted_iota(jnp.int32, (n, 1), 0)

        @pl.when(t == 0)
        def _():
            dw_ref[...] = jnp.zeros_like(dw_ref)
            db_ref[...] = jnp.zeros_like(db_ref)
            ds_ref[...] = jnp.zeros_like(ds_ref)

        for g in range(4):
            win = POOL_WINDOWS[g]
            sl = slice(g * 128, (g + 1) * 128)
            d = _pool_d(ue, g, pos, tt)
            wg = w_ref[g]
            ypre = _mm(d, wg) + b_ref[:, sl]
            sc = s_ref[:, sl]
            ds_ref[:, sl] += jnp.sum(dyv[:, sl] * ypre, axis=0, keepdims=True)
            dyp_e = dye[:, sl] * sc
            dyp = dyp_e[:tt, :]
            db_ref[:, sl] += jnp.sum(dyp, axis=0, keepdims=True)
            dw_ref[g] += _mm(d, dyp, 0, 0)
            dd_e = _mm(dyp_e, wg, 1, 1)
            cnt_e = jnp.minimum(pos_e + 1, win).astype(F32)
            s = dd_e / cnt_e
            k = 1
            while k < win:
                s = s + pltpu.roll(s, n - k, 0)
                k *= 2
            du_ref[:, sl] = (s[:tt, :] - dd_e[:tt, :]).astype(BF16)

    vec = pl.BlockSpec((1, POOL_W), lambda t: (0, 0))
    h16 = tt // 16
    return pl.pallas_call(
        body, name="pool_bwd", grid=(nt,),
        in_specs=[pl.BlockSpec((tt, POOL_W), lambda t: (t, 0)),
                  pl.BlockSpec((16, POOL_W), lambda t: (jnp.minimum((t + 1) * h16, T // 16 - 1), 0)),
                  pl.BlockSpec((tt, POOL_W), lambda t: (t, 0)),
                  pl.BlockSpec((16, POOL_W), lambda t: (jnp.maximum(t * h16 - 1, 0), 0)),
                  pl.BlockSpec((4, 128, 128), lambda t: (0, 0, 0)), vec, vec],
        out_specs=[pl.BlockSpec((tt, POOL_W), lambda t: (t, 0)),
                   pl.BlockSpec((4, 128, 128), lambda t: (0, 0, 0)), vec, vec],
        out_shape=[jax.ShapeDtypeStruct((T, PCOLS), BF16), jax.ShapeDtypeStruct((4, 128, 128), F32),
                   jax.ShapeDtypeStruct((1, POOL_W), F32), jax.ShapeDtypeStruct((1, POOL_W), F32)],
        compiler_params=_cp(("arbitrary",)))(dmixed, dmixed, proj, proj, pw, pb, ps)


def _gdn_pre_fwd(cpre, proj, alog, dtb):
    T = cpre.shape[0]
    tt = _tile(T)

    def body(c_ref, ab_ref, alog_ref, dtb_ref, qkv_ref, bb_ref, gb_ref):
        for p in range(3):
            for h in range(HEADS):
                cc = c_ref[:, (p * HEADS + h) * HD:(p * HEADS + h + 1) * HD]
                s = cc * _sigmoid(cc)
                if p < 2:
                    s = s * lax.rsqrt(jnp.sum(s * s, axis=1, keepdims=True) + EPS)
                if p == 0:
                    s = s * (HD ** -0.5)
                qkv_ref[p, h] = s
        ab = ab_ref[...]
        g = -jnp.exp(alog_ref[...]) * _softplus(ab + dtb_ref[...])
        r64 = lax.broadcasted_iota(jnp.int32, (tt, 1), 0) & (CHUNK - 1)
        k = 1
        while k < CHUNK:
            g = g + jnp.where(r64 >= k, pltpu.roll(g, k, 0), 0.0)
            k *= 2
        sb = _sigmoid(ab)
        for h in range(HEADS):
            gb_ref[h] = jnp.broadcast_to(g[:, h:h + 1], (tt, HD))
            bb_ref[h] = jnp.broadcast_to(sb[:, HEADS + h:HEADS + h + 1], (tt, HD))

    vec = pl.BlockSpec((1, 128), lambda t: (0, 0))
    hb = pl.BlockSpec((HEADS, tt, HD), lambda t: (0, t, 0))
    return pl.pallas_call(
        body, name="gdn_pre_fwd", grid=(T // tt,),
        in_specs=[pl.BlockSpec((tt, 3 * GDN_W), lambda t: (t, 0)),
                  pl.BlockSpec((tt, 128), lambda t: (t, C_AB // 128)), vec, vec],
        out_specs=[pl.BlockSpec((3, HEADS, tt, HD), lambda t: (0, 0, t, 0)), hb, hb],
        out_shape=[jax.ShapeDtypeStruct((3, HEADS, T, HD), F32), jax.ShapeDtypeStruct((HEADS, T, HD), F32),
                   jax.ShapeDtypeStruct((HEADS, T, HD), F32)],
        compiler_params=_cp(("parallel",)))(cpre, proj, alog, dtb)


def _gdn_pre_bwd(dq, dk, dv, cpre, dbb, dgb, proj, alog, dtb, dproj):
    T = cpre.shape[0]
    tt = _tile(T)

    def body(dq_ref, dk_ref, dv_ref, c_ref, dbb_ref, dgb_ref, ab_ref, alog_ref, dtb_ref, _,
             dc_ref, dab_ref, dalog_ref, ddtb_ref):
        t = pl.program_id(0)
        srcs = (dq_ref, dk_ref, dv_ref)
        for p in range(3):
            for h in range(HEADS):
                sl = slice((p * HEADS + h) * HD, (p * HEADS + h + 1) * HD)
                cc = c_ref[:, sl]
                sg = _sigmoid(cc)
                s = cc * sg
                dyv = srcs[p][h]
                if p < 2:
                    r = lax.rsqrt(jnp.sum(s * s, axis=1, keepdims=True) + EPS)
                    y = s * r
                    if p == 0:
                        dyv = dyv * (HD ** -0.5)
                    ds = r * (dyv - y * jnp.sum(dyv * y, axis=1, keepdims=True))
                else:
                    ds = dyv
                dc_ref[:, sl] = ds * sg * (1.0 + cc * (1.0 - sg))
        lane = lax.broadcasted_iota(jnp.int32, (tt, 128), 1)
        dg = jnp.zeros((tt, 128), F32)
        dbeta = jnp.zeros((tt, 128), F32)
        for h in range(HEADS):
            dg = jnp.where(lane == h, dgb_ref[h], dg)
            dbeta = jnp.where(lane == HEADS + h, dbb_ref[h], dbeta)
        r64 = lax.broadcasted_iota(jnp.int32, (tt, 1), 0) & (CHUNK - 1)
        k = 1
        while k < CHUNK:
            dg = dg + jnp.where(r64 < CHUNK - k, pltpu.roll(dg, tt - k, 0), 0.0)
            k *= 2
        ab = ab_ref[...]
        e = jnp.exp(alog_ref[...])
        xx = ab + dtb_ref[...]
        g = -e * _softplus(xx)
        da = jnp.where(lane < HEADS, dg * (-e) * _sigmoid(xx), 0.0)
        pa = jnp.sum(jnp.where(lane < HEADS, dg * g, 0.0), axis=0, keepdims=True)
        pd = jnp.sum(da, axis=0, keepdims=True)

        @pl.when(t == 0)
        def _():
            dalog_ref[...] = pa
            ddtb_ref[...] = pd

        @pl.when(t > 0)
        def _():
            dalog_ref[...] += pa
            ddtb_ref[...] += pd

        sb = _sigmoid(ab)
        dab_ref[:, :128] = jnp.where(lane < HEADS, da, dbeta * sb * (1.0 - sb)).astype(BF16)
        dab_ref[:, 128:] = jnp.zeros((tt, 128), BF16)

    vec = pl.BlockSpec((1, 128), lambda t: (0, 0))
    hb = pl.BlockSpec((HEADS, tt, HD), lambda t: (0, t, 0))
    return pl.pallas_call(
        body, name="gdn_pre_bwd", grid=(T // tt,),
        in_specs=[hb, hb, hb, pl.BlockSpec((tt, 3 * GDN_W), lambda t: (t, 0)), hb, hb,
                  pl.BlockSpec((tt, 128), lambda t: (t, C_AB // 128)), vec, vec, pl.BlockSpec(memory_space=pl.ANY)],
        out_specs=[pl.BlockSpec((tt, 3 * GDN_W), lambda t: (t, 0)),
                   pl.BlockSpec((tt, 256), lambda t: (t, C_AB // 256)), vec, vec],
        out_shape=[jax.ShapeDtypeStruct((T, 3 * GDN_W), F32), jax.ShapeDtypeStruct(dproj.shape, dproj.dtype),
                   jax.ShapeDtypeStruct((1, 128), F32), jax.ShapeDtypeStruct((1, 128), F32)],
        input_output_aliases={9: 1},
        compiler_params=_cp(("arbitrary",)))(dq, dk, dv, cpre, dbb, dgb, proj, alog, dtb, dproj)


def _split2(x):
    hi = x.astype(BF16)
    return hi, (x - hi.astype(F32)).astype(BF16)


def _bmm3s(a2, b2, ca=2, cb=1):
    def f(x, y):
        return lax.dot_general(x, y, (((ca,), (cb,)), ((0,), (0,))), preferred_element_type=F32)

    return f(a2[0], b2[0]) + (f(a2[0], b2[1]) + f(a2[1], b2[0]))


def _bmm3(a, b, ca=2, cb=1):
    return _bmm3s(_split2(a), _split2(b), ca, cb)


def _tri_inv(a):
    nb = a.shape[0]
    ri = lax.broadcasted_iota(jnp.int32, (nb, CHUNK, CHUNK), 1)
    ci = lax.broadcasted_iota(jnp.int32, (nb, CHUNK, CHUNK), 2)
    n = -a
    p = jnp.where(ri == ci, 1.0, 0.0) + n
    n2 = _split2(n)
    for _ in range(5):
        n2 = _split2(_bmm3s(n2, n2))
        p = p + _bmm3s(_split2(p), n2)
    return p


def _gdn_chunk_common(q, k, v, bb3, gb3, tm_saved=None):
    nb = q.shape[0]
    need_t = tm_saved is not None
    beta = bb3[:, :, 0:1]
    gcol = gb3[:, :, 0:1]
    bcol = bb3[:, :, :CHUNK]
    gcm = gb3[:, :, :CHUNK]
    oh = jnp.where(lax.broadcasted_iota(jnp.int32, (nb, CHUNK, HD), 2) == 0, 1.0, 0.0)
    grow = _bmm(oh, gb3, 2, 2, HI, False)
    ri = lax.broadcasted_iota(jnp.int32, (nb, CHUNK, CHUNK), 1)
    ci = lax.broadcasted_iota(jnp.int32, (nb, CHUNK, CHUNK), 2)
    tril, stl = ri >= ci, ri > ci
    dg = gcm - grow
    dec = jnp.where(tril, jnp.exp(jnp.where(tril, dg, 0.0)), 0.0)
    kk = _bmm(k, k, 2, 2)
    qk = _bmm(q, k, 2, 2)
    tm = tm_saved if need_t else _tri_inv(jnp.where(stl, bcol * kk * dec, 0.0))
    gam = jnp.exp(gcol)
    glast = gb3[:, CHUNK - 1:CHUNK, 0:1]
    egl = jnp.exp(glast)
    rw = k * (beta * gam)
    ru = v * beta
    wu = _bmm3(tm, jnp.concatenate([rw, ru], axis=2), 2, 1)
    kdf = jnp.exp(glast - gcol)
    out = dict(beta=beta, bcol=bcol, tril=tril, stl=stl, dec=dec, kk=kk, qk=qk, tm=tm, gam=gam, egl=egl,
               rw=rw, wu=wu, at=qk * dec, qd=q * gam, kdf=kdf, kd=k * kdf)
    if need_t:
        brow = _bmm(oh, bb3, 2, 2, HI, False)
        triu, stu = ri <= ci, ri < ci
        dect = jnp.where(triu, jnp.exp(jnp.where(triu, -dg, 0.0)), 0.0)
        qkt = _bmm(k, q, 2, 2)
        eye = jnp.where(ri == ci, 1.0, 0.0)
        out.update(brow=brow, triu=triu, stu=stu, dect=dect, qkt=qkt, tmt=_bmm3(eye, tm, 2, 2),
                   att=qkt * dect)
    return out


def _gdn_rows(T, fwd=False):
    return min(T, 1024 if fwd else 512)


HP = 2


def _gdn_fwd(qkv, bb, gb, proj, nw, mixed):
    T = proj.shape[0]
    R = _gdn_rows(T, fwd=True)
    nb = R // CHUNK
    B = HP * nb

    def body(q_ref, k_ref, v_ref, bb_ref, gb_ref, z_ref, nw_ref, _, y_ref, st_ref, tm_ref,
             s_ref, w_s, u_s, at_s, qd_s, kd_s):
        t = pl.program_id(1)

        @pl.when(t == 0)
        def _():
            s_ref[...] = jnp.zeros_like(s_ref)

        sh = (B, CHUNK, HD)
        q, k, v = q_ref[...].reshape(sh), k_ref[...].reshape(sh), v_ref[...].reshape(sh)
        c = _gdn_chunk_common(q, k, v, bb_ref[...].reshape(sh), gb_ref[...].reshape(sh))
        tm_ref[...] = c["tm"].reshape(HP, nb, CHUNK, CHUNK)
        w_s[...] = c["wu"][:, :, :HD]
        u_s[...] = c["wu"][:, :, HD:]
        at_s[...] = c["at"]
        qd_s[...] = c["qd"]
        kd_s[...] = c["kd"]
        egl = c["egl"]
        nwv = nw_ref[...]
        for n in range(nb):
            rows = slice(n * CHUNK, (n + 1) * CHUNK)
            for hh in range(HP):
                b = hh * nb + n
                cols = slice(hh * HD, (hh + 1) * HD)
                s = s_ref[hh]
                st_ref[hh, n] = s
                vn = u_s[b] - _mm(w_s[b], s)
                o = _mm(qd_s[b], s) + _mm(at_s[b], vn)
                s_ref[hh] = s * egl[b] + _mm(kd_s[b], vn, 0, 0)
                zz = z_ref[rows, cols]
                on = o * lax.rsqrt(jnp.mean(o * o, axis=1, keepdims=True) + EPS)
                y_ref[rows, cols] = (on * nwv * (zz * _sigmoid(zz))).astype(BF16)

    def hm(p):
        return pl.BlockSpec((None, HP, R, HD), lambda h, t: (p, h, t, 0))

    hb = pl.BlockSpec((HP, R, HD), lambda h, t: (h, t, 0))
    cs = pltpu.VMEM((B, CHUNK, HD), F32)
    return pl.pallas_call(
        body, name="gdn_fwd", grid=(HEADS // HP, T // R),
        in_specs=[hm(0), hm(1), hm(2), hb, hb,
                  pl.BlockSpec((R, HP * HD), lambda h, t: (t, C_Z // (HP * HD) + h)),
                  pl.BlockSpec((1, HD), lambda h, t: (0, 0)), pl.BlockSpec(memory_space=pl.ANY)],
        out_specs=[pl.BlockSpec((R, HP * HD), lambda h, t: (t, M_GDN // (HP * HD) + h)),
                   pl.BlockSpec((HP, nb, HD, HD), lambda h, t: (h, t, 0, 0)),
                   pl.BlockSpec((HP, nb, CHUNK, CHUNK), lambda h, t: (h, t, 0, 0))],
        out_shape=[jax.ShapeDtypeStruct(mixed.shape, mixed.dtype),
                   jax.ShapeDtypeStruct((HEADS, T // CHUNK, HD, HD), F32),
                   jax.ShapeDtypeStruct((HEADS, T // CHUNK, CHUNK, CHUNK), F32)],
        scratch_shapes=[pltpu.VMEM((HP, HD, HD), F32), cs, cs, pltpu.VMEM((B, CHUNK, CHUNK), F32), cs, cs],
        input_output_aliases={7: 0},
        compiler_params=_cp(("parallel", "arbitrary")))(qkv, qkv, qkv, bb, gb, proj, nw, mixed)


def _gdn_bwd(qkv, bb, gb, proj, nw, states, tms, dmixed, dproj):
    T = proj.shape[0]
    R = _gdn_rows(T)
    nb = R // CHUNK
    ntb = T // R
    B = HP * nb

    def by_head(ref):
        return jnp.concatenate([ref[:, hh * HD:(hh + 1) * HD].reshape(nb, CHUNK, HD) for hh in range(HP)], axis=0)

    def body(q_ref, k_ref, v_ref, bb_ref, gb_ref, z_ref, nw_ref, st_ref, tm_ref, dy_ref, _,
             dq_ref, dk_ref, dv_ref, dbb_ref, dgb_ref, dz_ref, dnw_ref,
             ds_ref, att_s, do_s, kd_s, vn_s, qd_s, w_s, dvn_s, dkd_s, dgl_s):
        hp = pl.program_id(0)
        t = pl.program_id(1)

        @pl.when(t == 0)
        def _():
            ds_ref[...] = jnp.zeros_like(ds_ref)

        @pl.when((t == 0) & (hp == 0))
        def _():
            dnw_ref[...] = jnp.zeros_like(dnw_ref)

        sh = (B, CHUNK, HD)
        q, k, v = q_ref[...].reshape(sh), k_ref[...].reshape(sh), v_ref[...].reshape(sh)
        c = _gdn_chunk_common(q, k, v, bb_ref[...].reshape(sh), gb_ref[...].reshape(sh),
                              tm_ref[...].reshape(B, CHUNK, CHUNK))
        w, u = c["wu"][:, :, :HD], c["wu"][:, :, HD:]
        sall = st_ref[...].reshape(B, HD, HD)
        vn = u - _bmm(w, sall, 2, 1)
        o = _bmm(c["qd"], sall, 2, 1) + _bmm(c["at"], vn, 2, 1)
        z = by_head(z_ref)
        dy = by_head(dy_ref)
        nwv = nw_ref[...].reshape(1, 1, HD)
        rs = lax.rsqrt(jnp.mean(o * o, axis=2, keepdims=True) + EPS)
        on = o * rs
        sg = _sigmoid(z)
        sz = z * sg
        dnw_ref[...] += jnp.sum(jnp.sum(dy * on * sz, axis=0), axis=0, keepdims=True)
        dz3 = dy * on * nwv * (sg * (1.0 + z * (1.0 - sg)))
        for hh in range(HP):
            dz_ref[:, hh * HD:(hh + 1) * HD] = dz3[hh * nb:(hh + 1) * nb].reshape(R, HD).astype(BF16)
        don = dy * nwv * sz
        do = rs * (don - on * jnp.mean(don * on, axis=2, keepdims=True))
        dqd = _bmm(do, sall, 2, 2)
        dat = jnp.where(c["tril"], _bmm(do, vn, 2, 2), 0.0)
        datt = jnp.where(c["triu"], _bmm(vn, do, 2, 2), 0.0)
        att_s[...] = c["att"]
        do_s[...] = do
        kd_s[...] = c["kd"]
        vn_s[...] = vn
        qd_s[...] = c["qd"]
        w_s[...] = w
        egl = c["egl"]
        for n in reversed(range(nb)):
            for hh in range(HP):
                b = hh * nb + n
                dso = ds_ref[hh]
                dvn_n = _mm(att_s[b], do_s[b]) + _mm(kd_s[b], dso)
                dkd_s[b] = _mm(vn_s[b], dso, 1, 1)
                dgl = egl[b] * jnp.sum(jnp.sum(st_ref[hh, n] * dso, axis=1, keepdims=True), axis=0, keepdims=True)
                dgl_s[b] = jnp.broadcast_to(dgl, (8, HD))
                ds_ref[hh] = egl[b] * dso + _mm(qd_s[b], do_s[b], 0, 0) - _mm(w_s[b], dvn_n, 0, 0)
                dvn_s[b] = dvn_n
        dvn = dvn_s[...]
        dkd = dkd_s[...]
        dgl = dgl_s[...][:, 0:1, 0:1]
        dw = -_bmm(dvn, sall, 2, 2)
        dr = _bmm3(c["tmt"], jnp.concatenate([dw, dvn], axis=2), 2, 1)
        drw, dru = dr[:, :, :HD], dr[:, :, HD:]
        wu = c["wu"]
        dr2, wu2 = _split2(dr), _split2(wu)
        da = -jnp.where(c["stl"], _bmm3s(dr2, wu2, 2, 2), 0.0)
        da_t = -jnp.where(c["stu"], _bmm3s(wu2, dr2, 2, 2), 0.0)
        beta, gam, dec, dect, kk = c["beta"], c["gam"], c["dec"], c["dect"], c["kk"]
        bcol, brow = c["bcol"], c["brow"]
        dbeta = (jnp.sum(da * kk * dec, axis=2, keepdims=True)
                 + jnp.sum(drw * k * gam + dru * v, axis=2, keepdims=True))
        dkk = bcol * da * dec
        dkk_t = brow * da_t * dect
        e = (bcol * da * kk + dat * c["qk"]) * dec
        e_t = (brow * da_t * kk + datt * c["qkt"]) * dect
        kd = c["kd"]
        dq_ref[...] = (_bmm(dat * dec, k, 2, 1) + dqd * gam).reshape(HP, R, HD)
        dk_ref[...] = (_bmm(datt * dect, q, 2, 1) + _bmm(dkk + dkk_t, k, 2, 1) + dkd * c["kdf"]
                       + drw * (beta * gam)).reshape(HP, R, HD)
        dv_ref[...] = (dru * beta).reshape(HP, R, HD)
        skd = jnp.sum(dkd * kd, axis=2, keepdims=True)
        dgc = (jnp.sum(e, axis=2, keepdims=True) - jnp.sum(e_t, axis=2, keepdims=True)
               + jnp.sum(drw * c["rw"] + dqd * c["qd"], axis=2, keepdims=True) - skd)
        tot = jnp.sum(skd, axis=1, keepdims=True) + dgl
        rowi = lax.broadcasted_iota(jnp.int32, (B, CHUNK, 1), 1)
        dgc = dgc + jnp.where(rowi == CHUNK - 1, tot, 0.0)
        dbb_ref[...] = jnp.broadcast_to(dbeta, sh).reshape(HP, R, HD)
        dgb_ref[...] = jnp.broadcast_to(dgc, sh).reshape(HP, R, HD)

    def rt(t):
        return ntb - 1 - t

    def hm(p):
        return pl.BlockSpec((None, HP, R, HD), lambda h, t: (p, h, rt(t), 0))

    hb = pl.BlockSpec((HP, R, HD), lambda h, t: (h, rt(t), 0))
    cs = pltpu.VMEM((B, CHUNK, HD), F32)
    ob = jax.ShapeDtypeStruct((HEADS, T, HD), F32)
    return pl.pallas_call(
        body, name="gdn_bwd", grid=(HEADS // HP, ntb),
        in_specs=[hm(0), hm(1), hm(2), hb, hb,
                  pl.BlockSpec((R, HP * HD), lambda h, t: (rt(t), C_Z // (HP * HD) + h)),
                  pl.BlockSpec((1, HD), lambda h, t: (0, 0)),
                  pl.BlockSpec((HP, nb, HD, HD), lambda h, t: (h, rt(t), 0, 0)),
                  pl.BlockSpec((HP, nb, CHUNK, CHUNK), lambda h, t: (h, rt(t), 0, 0)),
                  pl.BlockSpec((R, HP * HD), lambda h, t: (rt(t), M_GDN // (HP * HD) + h)),
                  pl.BlockSpec(memory_space=pl.ANY)],
        out_specs=[hb, hb, hb, hb, hb, pl.BlockSpec((R, HP * HD), lambda h, t: (rt(t), C_Z // (HP * HD) + h)),
                   pl.BlockSpec((1, HD), lambda h, t: (0, 0))],
        out_shape=[ob, ob, ob, ob, ob, jax.ShapeDtypeStruct(dproj.shape, dproj.dtype),
                   jax.ShapeDtypeStruct((1, HD), F32)],
        scratch_shapes=[pltpu.VMEM((HP, HD, HD), F32), pltpu.VMEM((B, CHUNK, CHUNK), F32), cs, cs, cs, cs, cs, cs,
                        cs, pltpu.VMEM((B, 8, HD), F32)],
        input_output_aliases={10: 5},
        compiler_params=_cp(("arbitrary", "arbitrary")))(qkv, qkv, qkv, bb, gb, proj, nw, states, tms, dmixed,
                                                         dproj)


LRU_TT = 1024


def _lru_gates(xc, wa, ba, wx, bx, lam, gpos):
    xb = xc.astype(BF16)
    r = _sigmoid(_mm(xb, wa) + ba)
    i = _sigmoid(_mm(xb, wx) + bx)
    sp = _softplus(-lam)
    log_a = -LRU_C * r * sp
    a = jnp.exp(log_a)
    mult = jnp.where(gpos == 0, 1.0, jnp.sqrt(-_expm1(2.0 * log_a)))
    return r, i, sp, a, mult


def _lru_fwd(xc, proj, wa, ba, wx, bx, lam, mixed):
    T = xc.shape[0]
    tt = min(T, LRU_TT)

    def body(xc_ref, gr_ref, wa_ref, ba_ref, wx_ref, bx_ref, lam_ref, _, y_ref, h_ref, carry_ref):
        t = pl.program_id(1)

        @pl.when(t == 0)
        def _():
            carry_ref[...] = jnp.zeros_like(carry_ref)

        row = lax.broadcasted_iota(jnp.int32, (tt, 1), 0)
        xcv = xc_ref[...]
        r, i, sp, a, mult = _lru_gates(xcv, wa_ref[...], ba_ref[...], wx_ref[...], bx_ref[...], lam_ref[...],
                                       t * tt + row)
        av, bv = a, mult * i * xcv
        k = 1
        while k < tt:
            a_s = jnp.where(row >= k, pltpu.roll(av, k, 0), 1.0)
            b_s = jnp.where(row >= k, pltpu.roll(bv, k, 0), 0.0)
            bv = bv + av * b_s
            av = av * a_s
            k *= 2
        h = bv + av * carry_ref[0:1, :]
        carry_ref[...] = jnp.broadcast_to(h[tt - 1:tt, :], (8, 128))
        h_ref[...] = h
        y_ref[...] = (h * _gelu(gr_ref[...])).astype(BF16)

    blk = pl.BlockSpec((tt, 128), lambda j, t: (t, j))
    vec = pl.BlockSpec((1, 128), lambda j, t: (0, j))
    mat = pl.BlockSpec((None, 128, 128), lambda j, t: (j, 0, 0))
    return pl.pallas_call(
        body, name="lru_fwd", grid=(LRU_W // 128, T // tt),
        in_specs=[blk, pl.BlockSpec((tt, 128), lambda j, t: (t, C_GR // 128 + j)), mat, vec, mat, vec, vec,
                  pl.BlockSpec(memory_space=pl.ANY)],
        out_specs=[pl.BlockSpec((tt, 128), lambda j, t: (t, M_LRU // 128 + j)), blk],
        out_shape=[jax.ShapeDtypeStruct(mixed.shape, mixed.dtype), jax.ShapeDtypeStruct((T, LRU_W), F32)],
        scratch_shapes=[pltpu.VMEM((8, 128), F32)], input_output_aliases={7: 0},
        compiler_params=_cp(("parallel", "arbitrary")))(xc, proj, wa, ba, wx, bx, lam, mixed)


def _lru_bwd(dmixed, xc, proj, hst, wa, ba, wx, bx, lam, dproj):
    T = xc.shape[0]
    tt = min(T, LRU_TT)
    nt = T // tt

    def body(dy_ref, xc_ref, gr_ref, h_ref, hp_ref, wa_ref, ba_ref, wx_ref, bx_ref, lam_ref, _,
             dxc_ref, dgr_ref, dwa_ref, dwx_ref, dba_ref, dbx_ref, dlam_ref, lc_ref, ac_ref):
        t = pl.program_id(1)
        tr = nt - 1 - t

        @pl.when(t == 0)
        def _():
            lc_ref[...] = jnp.zeros_like(lc_ref)
            ac_ref[...] = jnp.zeros_like(ac_ref)
            dwa_ref[...] = jnp.zeros_like(dwa_ref)
            dwx_ref[...] = jnp.zeros_like(dwx_ref)
            dba_ref[...] = jnp.zeros_like(dba_ref)
            dbx_ref[...] = jnp.zeros_like(dbx_ref)
            dlam_ref[...] = jnp.zeros_like(dlam_ref)

        row = lax.broadcasted_iota(jnp.int32, (tt, 1), 0)
        gpos = tr * tt + row
        xcv = xc_ref[...]
        wav, wxv, lamv = wa_ref[...], wx_ref[...], lam_ref[...]
        r, i, sp, a, mult = _lru_gates(xcv, wav, ba_ref[...], wxv, bx_ref[...], lamv, gpos)
        h = h_ref[...]
        dy = dy_ref[...]
        gg, dgg = _gelu_and_grad(gr_ref[...])
        dgr_ref[...] = (dy * h * dgg).astype(BF16)
        bv = dy * gg
        cv = jnp.where(row < tt - 1, pltpu.roll(a, tt - 1, 0), ac_ref[0:1, :])
        k = 1
        while k < tt:
            c_s = jnp.where(row < tt - k, pltpu.roll(cv, tt - k, 0), 1.0)
            b_s = jnp.where(row < tt - k, pltpu.roll(bv, tt - k, 0), 0.0)
            bv = bv + cv * b_s
            cv = cv * c_s
            k *= 2
        lm = bv + cv * lc_ref[0:1, :]
        lc_ref[...] = jnp.broadcast_to(lm[0:1, :], (8, 128))
        ac_ref[...] = jnp.broadcast_to(a[0:1, :], (8, 128))
        hp = jnp.where(tr == 0, 0.0, hp_ref[...])
        hs = pltpu.roll(jnp.concatenate([hp, h], axis=0), 1, 0)[8:, :]
        da = lm * hs
        dmult = lm * i * xcv
        di = lm * mult * xcv
        dxc = lm * mult * i
        dlog_a = a * da - jnp.where(gpos == 0, 0.0, dmult * a * a / mult)
        dr = dlog_a * (-LRU_C * sp)
        dsp = jnp.sum(dlog_a * (-LRU_C * r), axis=0, keepdims=True)
        dlam_ref[...] += dsp * (-_sigmoid(-lamv))
        dpr = dr * r * (1.0 - r)
        dpi = di * i * (1.0 - i)
        dba_ref[...] += jnp.sum(dpr, axis=0, keepdims=True)
        dbx_ref[...] += jnp.sum(dpi, axis=0, keepdims=True)
        dwa_ref[...] += _mm(xcv, dpr, 0, 0)
        dwx_ref[...] += _mm(xcv, dpi, 0, 0)
        dxc_ref[...] = dxc + _mm(dpr, wav, 1, 1) + _mm(dpi, wxv, 1, 1)

    def rt(t):
        return nt - 1 - t

    blk = pl.BlockSpec((tt, 128), lambda j, t: (rt(t), j))
    vec = pl.BlockSpec((1, 128), lambda j, t: (0, j))
    mat = pl.BlockSpec((None, 128, 128), lambda j, t: (j, 0, 0))
    h8 = tt // 8
    mshape = jax.ShapeDtypeStruct((LRU_W // 128, 128, 128), F32)
    vshape = jax.ShapeDtypeStruct((1, LRU_W), F32)
    return pl.pallas_call(
        body, name="lru_bwd", grid=(LRU_W // 128, nt),
        in_specs=[pl.BlockSpec((tt, 128), lambda j, t: (rt(t), M_LRU // 128 + j)), blk,
                  pl.BlockSpec((tt, 128), lambda j, t: (rt(t), C_GR // 128 + j)), blk,
                  pl.BlockSpec((8, 128), lambda j, t: (jnp.maximum(rt(t) * h8 - 1, 0), j)),
                  mat, vec, mat, vec, vec, pl.BlockSpec(memory_space=pl.ANY)],
        out_specs=[blk, pl.BlockSpec((tt, 128), lambda j, t: (rt(t), C_GR // 128 + j)), mat, mat, vec, vec, vec],
        out_shape=[jax.ShapeDtypeStruct((T, LRU_W), F32), jax.ShapeDtypeStruct(dproj.shape, dproj.dtype),
                   mshape, mshape, vshape, vshape, vshape],
        scratch_shapes=[pltpu.VMEM((8, 128), F32), pltpu.VMEM((8, 128), F32)], input_output_aliases={10: 1},
        compiler_params=_cp(("parallel", "arbitrary")))(dmixed, xc, proj, hst, hst, wa, ba, wx, bx, lam, dproj)


FFN_CB = 512
FFN_K = 3


def _ffn_conv(ge, w_ref):
    acc = ge * w_ref[FFN_K - 1:FFN_K, :]
    for j in range(FFN_K - 1):
        acc = acc + pltpu.roll(ge, FFN_K - 1 - j, 0) * w_ref[j:j + 1, :]
    return acc


FFN_HALO = 16
FFN_TT = 2048


def _ffn_gate_fwd(up, w):
    T = up.shape[0]
    tt = min(T, FFN_TT)
    cb = FFN_CB
    nc = D_FF // cb
    hh = tt // FFN_HALO

    def body(g_ref, gp_ref, v_ref, w_ref, o_ref, ot_ref):
        t = pl.program_id(1)
        prev = jnp.where(t == 0, 0.0, gp_ref[...].astype(F32))
        ge = jnp.concatenate([prev, g_ref[...].astype(F32)], axis=0)
        gc = _ffn_conv(ge, w_ref)[FFN_HALO:, :]
        a = _gelu(gc) * v_ref[...].astype(F32)
        o_ref[...] = a.astype(BF16)
        ot_ref[...] = a.T.astype(BF16)

    return pl.pallas_call(
        body, name="ffn_gate_fwd", grid=(nc, T // tt),
        in_specs=[pl.BlockSpec((tt, cb), lambda j, t: (t, j)),
                  pl.BlockSpec((FFN_HALO, cb), lambda j, t: (jnp.maximum(t * hh - 1, 0), j)),
                  pl.BlockSpec((tt, cb), lambda j, t: (t, nc + j)),
                  pl.BlockSpec((FFN_K, cb), lambda j, t: (0, j))],
        out_specs=[pl.BlockSpec((tt, cb), lambda j, t: (t, j)), pl.BlockSpec((cb, tt), lambda j, t: (j, t))],
        out_shape=[jax.ShapeDtypeStruct((T, D_FF), BF16), jax.ShapeDtypeStruct((D_FF, T), BF16)],
        compiler_params=_cp(("parallel", "parallel")))(up, up, up, w)


def _ffn_gate_bwd(dact, up, w):
    T = up.shape[0]
    tt = min(T, FFN_TT)
    cb = FFN_CB
    nc = D_FF // cb
    nt = T // tt
    hh = tt // FFN_HALO
    H = FFN_HALO

    def body(d_ref, dn_ref, g_ref, gp_ref, gn_ref, v_ref, vn_ref, w_ref, dup_ref, dw_ref):
        t = pl.program_id(1)
        prev = jnp.where(t == 0, 0.0, gp_ref[...].astype(F32))
        ge = jnp.concatenate([prev, g_ref[...].astype(F32), gn_ref[...].astype(F32)], axis=0)
        gc = _ffn_conv(ge, w_ref)[H:, :]
        gg, dgg = _gelu_and_grad(gc)
        de = jnp.concatenate([d_ref[...].astype(F32), jnp.where(t == nt - 1, 0.0, dn_ref[...].astype(F32))], axis=0)
        ve = jnp.concatenate([v_ref[...].astype(F32), vn_ref[...].astype(F32)], axis=0)
        dup_ref[1] = (de * gg)[:tt, :].astype(BF16)
        dgc = de * ve * dgg
        n = tt + H
        acc = dgc * w_ref[FFN_K - 1:FFN_K, :]
        for j in range(FFN_K - 1):
            acc = acc + pltpu.roll(dgc, n - (FFN_K - 1 - j), 0) * w_ref[j:j + 1, :]
        dup_ref[0] = acc[:tt, :].astype(BF16)

        @pl.when(t == 0)
        def _():
            dw_ref[...] = jnp.zeros_like(dw_ref)

        dgm = dgc[:tt, :]
        for j in range(FFN_K):
            sh = FFN_K - 1 - j
            xs = ge[H:H + tt, :] if sh == 0 else pltpu.roll(ge, sh, 0)[H:H + tt, :]
            dw_ref[j:j + 1, :] += jnp.sum(dgm * xs, axis=0, keepdims=True)

    def nxt(t):
        return jnp.minimum((t + 1) * hh, T // H - 1)

    return pl.pallas_call(
        body, name="ffn_gate_bwd", grid=(nc, nt),
        in_specs=[pl.BlockSpec((tt, cb), lambda j, t: (t, j)),
                  pl.BlockSpec((H, cb), lambda j, t: (nxt(t), j)),
                  pl.BlockSpec((tt, cb), lambda j, t: (t, j)),
                  pl.BlockSpec((H, cb), lambda j, t: (jnp.maximum(t * hh - 1, 0), j)),
                  pl.BlockSpec((H, cb), lambda j, t: (nxt(t), j)),
                  pl.BlockSpec((tt, cb), lambda j, t: (t, nc + j)),
                  pl.BlockSpec((H, cb), lambda j, t: (nxt(t), nc + j)),
                  pl.BlockSpec((FFN_K, cb), lambda j, t: (0, j))],
        out_specs=[pl.BlockSpec((2, tt, cb), lambda j, t: (0, t, j)),
                   pl.BlockSpec((FFN_K, cb), lambda j, t: (0, j))],
        out_shape=[jax.ShapeDtypeStruct((2, T, D_FF), BF16), jax.ShapeDtypeStruct((FFN_K, D_FF), F32)],
        compiler_params=_cp(("parallel", "arbitrary")))(dact, dact, up, up, up, up, up, w)


def _row_tile(rows, cap):
    best = 8
    for r in range(8, min(rows, cap) + 1, 8):
        if rows % r == 0:
            best = r
    return best


def _adamw(parts, w, m, v, rt, name, layer=None, prev=None):
    P, R, C = parts.shape

    def body(p_ref, w_ref, m_ref, v_ref, *rest):
        g_ref, d_ref, mo_ref, vo_ref = rest[-4:]
        g = p_ref[0].astype(F32)
        for i in range(1, P):
            g = g + p_ref[i].astype(F32)
        wv = w_ref[...]
        mn = ADAM_B1 * m_ref[...] + (1.0 - ADAM_B1) * g
        vn = ADAM_B2 * v_ref[...] + (1.0 - ADAM_B2) * (g * g)
        m_hat = mn / (1.0 - ADAM_B1 ** ADAM_STEP)
        v_hat = vn / (1.0 - ADAM_B2 ** ADAM_STEP)
        g_ref[...] = g
        d_ref[...] = -ADAM_LR * (m_hat / (jnp.sqrt(v_hat) + ADAM_EPS) + ADAM_WD * wv)
        mo_ref[...] = mn
        vo_ref[...] = vn

    if layer is None:
        blk = pl.BlockSpec((rt, C), lambda r: (r, 0))
        sh = jax.ShapeDtypeStruct((R, C), F32)
    else:
        blk = pl.BlockSpec((None, rt, C), lambda r: (layer, r, 0))
        sh = jax.ShapeDtypeStruct(w.shape, F32)
    extra = list(prev) if prev is not None else []
    return pl.pallas_call(
        body, name=name, grid=(R // rt,),
        in_specs=[pl.BlockSpec((P, rt, C), lambda r: (0, r, 0)), blk, blk, blk]
        + [pl.BlockSpec(memory_space=pl.ANY)] * len(extra),
        out_specs=[blk, blk, blk, blk], out_shape=[sh, sh, sh, sh],
        input_output_aliases={4 + i: i for i in range(len(extra))},
        compiler_params=_cp(("parallel",)))(parts, w, m, v, *extra)


def _peer(k):
    x, y, c = lax.axis_index("x"), lax.axis_index("y"), lax.axis_index("c")
    px = 1 - x if k & 4 else x
    py = 1 - y if k & 2 else y
    pc = 1 - c if k & 1 else c
    return (px, py, pc), 4 * px + 2 * py + pc


def _all_gather(x, name):
    R, C = x.shape

    def body(x_ref, o_ref, send_sems, recv_sems, local_sem):
        me = 4 * lax.axis_index("x") + 2 * lax.axis_index("y") + lax.axis_index("c")
        mine = pltpu.make_async_copy(x_ref, o_ref.at[me], local_sem)
        mine.start()
        sends = []
        for k in range(1, N_DEV):
            dev, _ = _peer(k)
            cp = pltpu.make_async_remote_copy(src_ref=x_ref, dst_ref=o_ref.at[me], send_sem=send_sems.at[k - 1],
                                              recv_sem=recv_sems.at[k - 1], device_id=dev, device_id_type=MESH_IDS)
            cp.start()
            sends.append(cp)
        for k in range(1, N_DEV):
            dev, idx = _peer(k)
            pltpu.make_async_remote_copy(src_ref=x_ref, dst_ref=o_ref.at[idx], send_sem=send_sems.at[k - 1],
                                         recv_sem=recv_sems.at[k - 1], device_id=dev,
                                         device_id_type=MESH_IDS).wait_recv()
        for cp in sends:
            cp.wait_send()
        mine.wait()

    return pl.pallas_call(
        body, name=name, in_specs=[pl.BlockSpec(memory_space=pl.ANY)], out_specs=pl.BlockSpec(memory_space=pl.ANY),
        out_shape=jax.ShapeDtypeStruct((N_DEV, R, C), x.dtype),
        scratch_shapes=[pltpu.SemaphoreType.DMA((N_DEV - 1,)), pltpu.SemaphoreType.DMA((N_DEV - 1,)),
                        pltpu.SemaphoreType.DMA],
        compiler_params=pltpu.CompilerParams(has_side_effects=True))(x)


HBM_SPEC = pl.BlockSpec(memory_space=pltpu.HBM)
SEM_SPEC = pl.BlockSpec(memory_space=pltpu.SEMAPHORE)
EFFECT = pltpu.SideEffectType.DATAFLOW_SIDE_EFFECTING
OTHER_CHIPS = ((1, 0), (0, 1), (1, 1))


def _split_start(bufs, plan, n, name):
    nb = len(bufs)

    def body(*refs):
        send_sems, recv_sems, token = refs[nb], refs[nb + 1], refs[2 * nb + 2]
        for i, (src, dst, _, dev) in enumerate(plan(refs[:nb])):
            pltpu.make_async_remote_copy(src_ref=src, dst_ref=dst, send_sem=send_sems.at[i],
                                         recv_sem=recv_sems.at[i], device_id=dev, device_id_type=MESH_IDS).start()
        token[...] = jnp.zeros_like(token)

    outs = pl.pallas_call(
        body, name=name,
        out_shape=(pltpu.SemaphoreType.DMA((n,)), pltpu.SemaphoreType.DMA((n,)),
                   *[pltpu.HBM(b.shape, b.dtype) for b in bufs], jax.ShapeDtypeStruct((8, 128), F32)),
        in_specs=[HBM_SPEC] * nb,
        out_specs=(SEM_SPEC, SEM_SPEC, *[HBM_SPEC] * nb, pl.BlockSpec(memory_space=pltpu.VMEM)),
        input_output_aliases={i: 2 + i for i in range(nb)},
        compiler_params=pltpu.CompilerParams(has_side_effects=EFFECT),
    )(*[pltpu.with_memory_space_constraint(b, pltpu.HBM) for b in bufs])
    return dict(send=outs[0], recv=outs[1], bufs=list(outs[2:2 + nb]), token=outs[2 + nb], plan=plan, n=n)


def _split_wait(st, after, name):
    bufs = st["bufs"]
    nb = len(bufs)
    plan = st["plan"]
    afters = list(after) if isinstance(after, (list, tuple)) else [after]

    def body(*refs):
        send_sems, recv_sems = refs[nb], refs[nb + 1]
        for i, (src, dst, land, dev) in enumerate(plan(refs[:nb])):
            pltpu.make_async_remote_copy(src_ref=src, dst_ref=dst, send_sem=send_sems.at[i],
                                         recv_sem=recv_sems.at[i], device_id=dev,
                                         device_id_type=MESH_IDS).wait_send()
            pltpu.make_async_remote_copy(src_ref=src, dst_ref=land, send_sem=send_sems.at[i],
                                         recv_sem=recv_sems.at[i], device_id=dev,
                                         device_id_type=MESH_IDS).wait_recv()

    outs = pl.pallas_call(
        body, name=name, out_shape=tuple(pltpu.HBM(b.shape, b.dtype) for b in bufs),
        in_specs=[HBM_SPEC] * nb + [SEM_SPEC, SEM_SPEC] + [pl.BlockSpec(memory_space=pl.ANY)] * len(afters),
        out_specs=tuple([HBM_SPEC] * nb), input_output_aliases={i: i for i in range(nb)},
        compiler_params=pltpu.CompilerParams(has_side_effects=EFFECT),
    )(*bufs, st["send"], st["recv"], *afters)
    return list(outs)


def _xyc():
    return lax.axis_index("x"), lax.axis_index("y"), lax.axis_index("c")


def _flip(x, y, a, b):
    return (1 - x if a else x), (1 - y if b else y)


def _ag1_plan(outs):
    x, y, c = _xyc()
    me = 4 * x + 2 * y + c
    copies = []
    for o in outs:
        copies.append((o.at[me], o.at[me], o.at[4 * x + 2 * y + 1 - c], (x, y, 1 - c)))
        for a, b in OTHER_CHIPS:
            px, py = _flip(x, y, a, b)
            copies.append((o.at[me], o.at[me], o.at[4 * px + 2 * py + c], (px, py, c)))
    return copies


def _ag2_plan(outs):
    x, y, c = _xyc()
    copies = []
    for o in outs:
        for a, b in OTHER_CHIPS:
            px, py = _flip(x, y, a, b)
            mine, sibs = 4 * px + 2 * py + c, 4 * px + 2 * py + 1 - c
            copies.append((o.at[mine], o.at[mine], o.at[sibs], (x, y, 1 - c)))
    return copies


def _rs1_plan(refs):
    x, y, c = _xyc()
    copies = []
    for g, land in zip(refs[0::2], refs[1::2]):
        for j in range(4):
            copies.append((g.at[2 * j + 1 - c], land.at[j], land.at[j], (x, y, 1 - c)))
    return copies


def _rs2_plan(refs):
    x, y, c = _xyc()
    mychip = 2 * x + y
    copies = []
    for s, land in zip(refs[0::2], refs[1::2]):
        for a, b in OTHER_CHIPS:
            px, py = _flip(x, y, a, b)
            copies.append((s.at[2 * px + py], land.at[mychip], land.at[2 * px + py], (px, py, c)))
    return copies


def _landing_like(g, name):
    def body(g_ref, o_ref):
        del g_ref, o_ref

    anyspec = pl.BlockSpec(memory_space=pl.ANY)
    return pl.pallas_call(body, name=name, in_specs=[anyspec], out_specs=anyspec,
                          out_shape=jax.ShapeDtypeStruct((4,) + g.shape[1:], g.dtype))(g)


def _place(x, slots, by_chip, name):
    R, C = x.shape[-2:]
    rt = _row_tile(R, 512)
    xi, yi, ci = _xyc()
    idx = (2 * xi + yi if by_chip else 4 * xi + 2 * yi + ci).astype(jnp.int32).reshape(1)

    def body(i_ref, x_ref, o_ref):
        o_ref[...] = x_ref[...]

    if by_chip:
        in_spec = pl.BlockSpec((None, rt, C), lambda r, i: (i[0], r, 0))
    else:
        in_spec = pl.BlockSpec((rt, C), lambda r, i: (r, 0))
    grid_spec = pltpu.PrefetchScalarGridSpec(
        num_scalar_prefetch=1, grid=(R // rt,), in_specs=[in_spec],
        out_specs=pl.BlockSpec((None, rt, C), lambda r, i: (i[0], r, 0)))
    return pl.pallas_call(body, name=name, grid_spec=grid_spec,
                          out_shape=jax.ShapeDtypeStruct((slots, R, C), x.dtype),
                          compiler_params=_cp(("parallel",)))(idx, x)


def _pair_sum(g, land, cidx, name):
    _, R, C = land.shape
    rt = _row_tile(R, 512)
    g4 = g.reshape(4, 2, R, C)

    def body(c_ref, g_ref, l_ref, o_ref):
        o_ref[...] = (g_ref[...].astype(F32) + l_ref[...].astype(F32)).astype(o_ref.dtype)

    grid_spec = pltpu.PrefetchScalarGridSpec(
        num_scalar_prefetch=1, grid=(4, R // rt),
        in_specs=[pl.BlockSpec((None, None, rt, C), lambda j, r, c_ref: (j, c_ref[0], r, 0)),
                  pl.BlockSpec((None, rt, C), lambda j, r, c_ref: (j, r, 0))],
        out_specs=pl.BlockSpec((None, rt, C), lambda j, r, c_ref: (j, r, 0)))
    return pl.pallas_call(body, name=name, grid_spec=grid_spec, out_shape=jax.ShapeDtypeStruct(land.shape, land.dtype),
                          compiler_params=_cp(("parallel", "parallel")))(cidx, g4, land)


def _no_hook(event, l, after, payload=None):
    return None


def _tie(x, token):
    if token is None:
        return x

    def body(x_ref, t_ref, o_ref):
        del x_ref, t_ref, o_ref

    anyspec = pl.BlockSpec(memory_space=pl.ANY)
    return pl.pallas_call(body, name="tie", in_specs=[anyspec, anyspec], out_specs=anyspec,
                          out_shape=jax.ShapeDtypeStruct(x.shape, x.dtype), input_output_aliases={0: 0})(x, token)


def _layer_fwd(x, W, l, hook=_no_hook):
    T = x.shape[0]
    n = f"l{l}_"
    h1, h1t = _norm_fwd(x, W["norm1"], n + "norm1_fwd")
    proj = _mm_nn(h1, W["win"], F32, n + "mm_in", tn_c=(1792,))
    y_pool = _pool_fwd(proj, W["pool_w"], W["pool_b"], W["pool_s"])
    cpre = _conv_fwd(proj, C_QKV, 3 * GDN_W, W["gconv_w"], None, 256, n + "gdn_conv_fwd")
    qkv, bb, gb = _gdn_pre_fwd(cpre, proj, W["alog"], W["dtb"])
    mixed, states, tms = _gdn_fwd(qkv, bb, gb, proj, W["gnorm"], y_pool)
    lconv_w = _tie(W["lconv_w"], hook("f_mix", l, states))
    xc = _conv_fwd(proj, C_XR, LRU_W, lconv_w, W["lconv_b"], 128, n + "lru_conv_fwd")
    mixed, hst = _lru_fwd(xc, proj, W["wa"], W["ba"], W["wx"], W["bx"], W["lam"], mixed)
    hook("f_out", l, hst)
    x1 = _mm_nn(mixed, W["wout"], F32, n + "mm_out", add=x)
    h2, h2t = _norm_fwd(x1, W["norm2"], n + "norm2_fwd")
    up = _mm_up(h2, W["wup"], n + "mm_up")
    act, act_t = _ffn_gate_fwd(up, W["fconv_w"])
    act = _tie(act, hook("f_act", l, act_t))
    x2 = _mm_nn(act, W["wdown"], F32, n + "mm_down", add=x1, tk_c=(3072, 2048, 1536, 1024, 512, 256))
    hook("f_end", l, x2)
    saved = dict(x=x, h1t=h1t, proj=proj, cpre=cpre, qkv=qkv, bb=bb, gb=gb, states=states, tms=tms, xc=xc, hst=hst,
                 mixed=mixed, x1=x1, h2t=h2t, up=up, act_t=act_t)
    return x2, saved


def _layer_bwd(dx2, dx2b, W, S, l, hook=_no_hook):
    T = dx2.shape[0]
    n = f"l{l}_"
    dact = _mm_nt(dx2b, W["wdown"], BF16, n + "mm_down_dx", tk_c=(2048,), tn_c=(1536,))
    g_wdown = _mm_nn(S["act_t"], dx2b, BF16, n + "mm_down_dw", tn_c=(2048,))
    dup, g_fconv = _ffn_gate_bwd(dact, S["up"], W["fconv_w"])
    ns = W["wup"].shape[2]
    dh2 = _mm_up_t(dup, W["wup"], n + "mm_up_dx")
    g_wup = _mm_dup(S["h2t"], dup, ns, n + "mm_up_dw")
    tok = hook("b_ffn", l, g_wup, dict(ffn_down=g_wdown, ffn_up=g_wup))
    dx1, dx1b, g_norm2 = _norm_bwd(S["x1"], _tie(W["norm2"], tok), dh2, dx2, n + "norm2_bwd")
    dmixed = _mm_nt(dx1b, W["wout"], F32, n + "mm_out_dx", tk_c=(2048,), tn_c=(2048,))
    g_wout = _mm_tn(S["mixed"], dx1b, BF16, n + "mm_out_dw", tn_c=(2048,))
    tok = hook("b_mid", l, g_wout)
    proj = S["proj"]
    dproj, g_pool_w, g_pool_b, g_pool_s = _pool_bwd(dmixed, proj, W["pool_w"], W["pool_b"], _tie(W["pool_s"], tok))
    dq, dk, dv, dbb, dgb, dproj, g_gnorm = _gdn_bwd(S["qkv"], S["bb"], S["gb"], proj, W["gnorm"], S["states"],
                                                    S["tms"], dmixed, dproj)
    dc, dproj, g_alog, g_dtb = _gdn_pre_bwd(dq, dk, dv, S["cpre"], dbb, dgb, proj, W["alog"], W["dtb"], dproj)
    dproj, g_gconv = _conv_bwd(dc, proj, C_QKV, W["gconv_w"], 256, n + "gdn_conv_bwd", dproj)
    dxc, dproj, g_wa, g_wx, g_ba, g_bx, g_lam = _lru_bwd(dmixed, S["xc"], proj, S["hst"], W["wa"], W["ba"], W["wx"],
                                                          W["bx"], W["lam"], dproj)
    dproj, g_lconv, g_lconv_b = _conv_bwd(dxc, proj, C_XR, W["lconv_w"], 128, n + "lru_conv_bwd", dproj,
                                          want_db=True)
    dh1 = _mm_nt(dproj, W["win"], F32, n + "mm_in_dx", tk_c=(1792,))
    g_win = _mm_nn(S["h1t"], dproj, BF16, n + "mm_in_dw", tn_c=(1792,))
    tok = hook("b_in", l, g_win, dict(w_out=g_wout, w_in=g_win))
    dx, dxb, g_norm1 = _norm_bwd(S["x"], _tie(W["norm1"], tok), dh1, dx1, n + "norm1_bwd")
    big = dict(w_in=g_win, w_out=g_wout, ffn_up=g_wup, ffn_down=g_wdown)
    small = dict(norm1_w=g_norm1[0], pool_w=g_pool_w, pool_b=g_pool_b.reshape(4, 128), pool_scale=g_pool_s[0],
                 gdn_conv_w=g_gconv, gdn_a_log=g_alog[0, :HEADS], gdn_dt_bias=g_dtb[0, :HEADS],
                 gdn_norm_w=g_gnorm[0], lru_conv_w=g_lconv, lru_conv_b=g_lconv_b[0], lru_wa=g_wa, lru_ba=g_ba[0],
                 lru_wx=g_wx, lru_bx=g_bx[0], lru_lambda=g_lam[0], norm2_w=g_norm2[0], ffn_conv_w=g_fconv)
    dxb = _tie(dxb, hook("b_end", l, dx, small))
    return dx, dxb, big, small


def _pad_lane(v):
    return jnp.pad(v, (0, 128 - v.shape[0])).reshape(1, 128)


def _layer_weights(l, big, P, conv_full):
    return dict(
        win=big.get("w_in"), wout=big.get("w_out"), wup=big.get("ffn_up"), wdown=big.get("ffn_down"),
        norm1=P["norm1_w"][l].reshape(1, D_MODEL), norm2=P["norm2_w"][l].reshape(1, D_MODEL),
        pool_w=P["pool_w"][l], pool_b=P["pool_b"][l].reshape(1, POOL_W), pool_s=P["pool_scale"][l].reshape(1, POOL_W),
        gconv_w=conv_full["gdn_conv_w"][l], alog=_pad_lane(P["gdn_a_log"][l]), dtb=_pad_lane(P["gdn_dt_bias"][l]),
        gnorm=P["gdn_norm_w"][l].reshape(1, HD),
        lconv_w=conv_full["lru_conv_w"][l], lconv_b=P["lru_conv_b"][l].reshape(1, LRU_W),
        wa=P["lru_wa"][l], ba=P["lru_ba"][l].reshape(1, LRU_W), wx=P["lru_wx"][l],
        bx=P["lru_bx"][l].reshape(1, LRU_W), lam=P["lru_lambda"][l].reshape(1, LRU_W),
        fconv_w=conv_full["ffn_conv_w"][l])


def _local_step(x, target, Ws, final_norm_w, hook=_no_hook):
    saved = []
    for l in range(DEPTH):
        x, s = _layer_fwd(x, Ws[l], l, hook)
        saved.append(s)
    loss, dx, dxb, g_final = _loss_head(x, final_norm_w.reshape(1, D_MODEL), target)
    bigs, smalls = [None] * DEPTH, [None] * DEPTH
    for l in reversed(range(DEPTH)):
        dx, dxb, bigs[l], smalls[l] = _layer_bwd(dx, dxb, Ws[l], saved[l], l, hook)
    return loss, dx, g_final[0], bigs, smalls


SMALL_REPL = ("norm1_w", "pool_w", "pool_b", "pool_scale", "gdn_a_log", "gdn_dt_bias", "gdn_norm_w", "lru_conv_b",
              "lru_wa", "lru_ba", "lru_wx", "lru_bx", "lru_lambda", "norm2_w", "final_norm_w")
SMALL_SHARD = ("gdn_conv_w", "lru_conv_w", "ffn_conv_w")
BIG = ("w_in", "w_out", "ffn_up", "ffn_down")
WEIGHTS = ("norm1_w", "w_in", "pool_w", "pool_b", "pool_scale", "gdn_conv_w", "gdn_a_log", "gdn_dt_bias",
           "gdn_norm_w", "lru_conv_w", "lru_conv_b", "lru_wa", "lru_ba", "lru_wx", "lru_bx", "lru_lambda", "w_out",
           "norm2_w", "ffn_up", "ffn_conv_w", "ffn_down", "final_norm_w")
SEG = 1024
PACK_ROWS_MULT = 256 * 128


def _pack(arrs):
    pieces, table, off = [], [], 0
    for a in arrs:
        n = a.size
        npad = -(-n // SEG) * SEG
        pieces.append(jnp.pad(a.reshape(-1).astype(F32), (0, npad - n)))
        table.append((off, n, a.shape))
        off += npad
    tail = -off % PACK_ROWS_MULT
    if tail:
        pieces.append(jnp.zeros((tail,), F32))
        off += tail
    return jnp.concatenate(pieces).reshape(off // 128, 128), table


def _unpack(buf, table):
    flat = buf.reshape(-1)
    return [flat[off:off + n].reshape(shape) for off, n, shape in table]


def _pad_in(w):
    z1 = jnp.zeros(w.shape[:-1] + (C_XR - AB_ORIG_END,), w.dtype)
    return jnp.concatenate([w[..., :AB_ORIG_END], z1, w[..., AB_ORIG_END:]], axis=-1)


def _unpad_in(w):
    return jnp.concatenate([w[..., :AB_ORIG_END], w[..., C_XR:C_GR + LRU_W]], axis=-1)


def kernel(x, norm1_w, w_in, pool_w, pool_b, pool_scale, gdn_conv_w, gdn_a_log, gdn_dt_bias, gdn_norm_w, lru_conv_w, lru_conv_b, lru_wa, lru_ba, lru_wx, lru_bx, lru_lambda, w_out, norm2_w, ffn_up, ffn_conv_w, ffn_down, final_norm_w, loss_target, m_norm1_w, m_w_in, m_pool_w, m_pool_b, m_pool_scale, m_gdn_conv_w, m_gdn_a_log, m_gdn_dt_bias, m_gdn_norm_w, m_lru_conv_w, m_lru_conv_b, m_lru_wa, m_lru_ba, m_lru_wx, m_lru_bx, m_lru_lambda, m_w_out, m_norm2_w, m_ffn_up, m_ffn_conv_w, m_ffn_down, m_final_norm_w, v_norm1_w, v_w_in, v_pool_w, v_pool_b, v_pool_scale, v_gdn_conv_w, v_gdn_a_log, v_gdn_dt_bias, v_gdn_norm_w, v_lru_conv_w, v_lru_conv_b, v_lru_wa, v_lru_ba, v_lru_wx, v_lru_bx, v_lru_lambda, v_w_out, v_norm2_w, v_ffn_up, v_ffn_conv_w, v_ffn_down, v_final_norm_w):
    loc = dict(locals())
    Wp = {n: loc[n] for n in WEIGHTS}
    Mp = {n: loc["m_" + n] for n in WEIGHTS}
    Vp = {n: loc["v_" + n] for n in WEIGHTS}
    xi, yi, ci = _xyc()
    me = 4 * xi + 2 * yi + ci
    mychip = 2 * xi + yi
    cidx = ci.astype(jnp.int32).reshape(1)
    keys = dict(w_in="win", w_out="wout", ffn_up="wup", ffn_down="wdown")

    def shard2d(d, name, l):
        a = d[name][l]
        return _pad_in(a) if name == "w_in" else a

    def wshard(l, name):
        return shard2d(Wp, name, l).astype(BF16)

    def full2d(name, full):
        return full if name == "ffn_up" else full.reshape(-1, full.shape[2])

    def ag_start(shards, tag, token=None):
        if token is not None:
            shards = [_tie(shards[0], token)] + list(shards[1:])
        bufs = [_place(s, N_DEV, False, f"place_{tag}{i}") for i, s in enumerate(shards)]
        return _split_start(bufs, _ag1_plan, 4 * len(bufs), f"ag1s_{tag}")

    def ag_mid(st, after, tag):
        bufs = _split_wait(st, after, f"ag1w_{tag}")
        return _split_start(bufs, _ag2_plan, 3 * len(bufs), f"ag2s_{tag}")

    def ag_end(st, after, tag):
        return _split_wait(st, after, f"ag2w_{tag}")

    def rs_start(gs, tag):
        bufs = []
        for nm, g in gs.items():
            if nm != "ffn_up":
                g = g.reshape(N_DEV, g.shape[0] // N_DEV, g.shape[1])
            bufs += [g, _landing_like(g, f"land_{nm}_{tag}")]
        st = _split_start(bufs, _rs1_plan, 4 * len(gs), f"rs1s_{tag}")
        st["names"] = list(gs)
        return st

    def rs_mid(st, after, tag):
        bufs = _split_wait(st, after, f"rs1w_{tag}")
        out = []
        for i, nm in enumerate(st["names"]):
            s = _pair_sum(bufs[2 * i], bufs[2 * i + 1], cidx, f"pairsum_{nm}_{tag}")
            out += [s, _place(s, 4, True, f"place_{nm}_{tag}")]
        st2 = _split_start(out, _rs2_plan, 3 * len(st["names"]), f"rs2s_{tag}")
        st2["names"] = st["names"]
        return st2

    def rs_end(st, after, tag):
        bufs = _split_wait(st, after, f"rs2w_{tag}")
        return dict(zip(st["names"], bufs[1::2]))

    lnames = tuple(n for n in SMALL_REPL if n != "final_norm_w") + SMALL_SHARD

    def small_pack(l, gs, extra):
        return _pack([gs[nm] for nm in lnames] + extra)

    def small_state(d, l, gs):
        arrs = [d[nm][l] if nm in SMALL_REPL else jnp.zeros(gs[nm].shape, F32) for nm in lnames]
        if l == 0:
            arrs += [d["final_norm_w"], jnp.zeros((1,), F32)]
        return _pack(arrs)[0]

    def stacked(d, name):
        return _pad_in(d[name]) if name == "w_in" else d[name]

    wmv = {name: [stacked(d, name) for d in (Wp, Mp, Vp)] for name in BIG}
    shards_b = [wshard(0, n) for n in BIG[1:]]
    shards_c = [wshard(1, n) for n in BIG]
    stA = ag_start([wshard(0, "w_in")], "a")
    stA2 = ag_mid(stA, [stA["token"]] + shards_b + shards_c + wmv["w_in"], "a")
    stB = ag_start(shards_b, "b", stA2["token"])
    (w_in0,) = ag_end(stA2, stB["token"], "a")

    cbuf, ctable = _pack([Wp[n] for n in SMALL_SHARD])
    call = _all_gather(cbuf, "ag_conv_w")
    parts = [_unpack(call[i], ctable) for i in range(N_DEV)]
    conv_full = {n: jnp.concatenate([parts[i][j] for i in range(N_DEV)], axis=-1) for j, n in enumerate(SMALL_SHARD)}

    Ws = [_layer_weights(l, {}, Wp, conv_full) for l in range(DEPTH)]
    Ws[0]["win"] = full2d("w_in", w_in0)
    st = {}

    def hook(event, l, after, payload=None):
        if event == "f_mix" and l == 0:
            st["b2"] = ag_mid(stB, after, "b")
            st["c"] = ag_start(shards_c, "c", st["b2"]["token"])
            return st["c"]["token"]
        if event == "f_out" and l == 0:
            for n, b in zip(BIG[1:], ag_end(st["b2"], after, "b")):
                Ws[0][keys[n]] = full2d(n, b)
        if event == "f_act" and l == 0:
            st["c2"] = ag_mid(st["c"], after, "c")
            return st["c2"]["token"]
        if event == "f_end" and l == 0:
            for n, b in zip(BIG, ag_end(st["c2"], after, "c")):
                Ws[1][keys[n]] = full2d(n, b)
        if event == "b_ffn":
            st["ffn", l] = rs_start(payload, f"ffn{l}")
            return st["ffn", l]["token"]
        if event == "b_mid":
            st["ffn2", l] = rs_mid(st["ffn", l], after, f"ffn{l}")
            if l == 0:
                st["sm1b"] = ag_mid(st["sm1"], st["ffn2", l]["token"], "sm1")
                return st["sm1b"]["token"]
            return st["ffn2", l]["token"]
        if event == "b_in":
            st["io", l] = rs_start(payload, f"io{l}")
            if l == 0:
                st["sm1g"] = ag_end(st["sm1b"], st["io", l]["token"], "sm1")[0]
            return st["io", l]["token"]
        if event == "b_end" and l == 1:
            st["io2", 1] = rs_mid(st["io", 1], after, "io1")
            gbuf1, st["table1"] = small_pack(1, payload, [])
            st["sm1"] = ag_start([gbuf1], "sm1", st["io2", 1]["token"])
            return st["sm1"]["token"]
        return None

    loss, dx, g_final, _, gsmall = _local_step(x[0], loss_target[0], Ws, final_norm_w, hook)

    out_g, out_d, out_m, out_v = {}, {}, {}, {}
    outs4 = (out_g, out_d, out_m, out_v)
    rts = dict(w_in=64, w_out=128, ffn_up=256, ffn_down=128)
    big_res = {}

    def adam_big(l, parts):
        for name, p in parts.items():
            big_res[name] = _adamw(p, *wmv[name], rts[name], f"adamw_{name}_{l}", layer=l, prev=big_res.get(name))
        return [big_res[name][0] for name in parts]

    def adam_small(l, gall, gs):
        rs = gall.shape[1]
        return _adamw(gall, small_state(Wp, l, gs), small_state(Mp, l, gs), small_state(Vp, l, gs),
                      _row_tile(rs, 512), f"adamw_small_{l}")

    gbuf0, table0 = small_pack(0, gsmall[0], [g_final, loss[0, :1]])
    sm0 = ag_start([gbuf0], "sm0", st["io", 0]["token"])
    o = adam_big(1, rs_end(st["ffn2", 1], sm0["token"], "ffn1"))
    st["io2", 0] = rs_mid(st["io", 0], o, "io0")
    o = adam_big(1, rs_end(st["io2", 1], st["io2", 0]["token"], "io1"))
    o = adam_big(0, rs_end(st["ffn2", 0], o, "ffn0"))
    small_res = {1: adam_small(1, _tie(st["sm1g"], o[-1]), gsmall[1])}
    sm0b = ag_mid(sm0, o + [small_res[1][0]], "sm0")
    small_res[0] = adam_small(0, ag_end(sm0b, sm0b["token"], "sm0")[0], gsmall[0])
    adam_big(0, rs_end(st["io2", 0], small_res[0][0], "io0"))

    for name in BIG:
        for i, dst in enumerate(outs4):
            dst[name] = _unpad_in(big_res[name][i]) if name == "w_in" else big_res[name][i]

    unp = {0: [_unpack(r, table0) for r in small_res[0]], 1: [_unpack(r, st["table1"]) for r in small_res[1]]}
    for j, nm in enumerate(lnames):
        if nm in SMALL_REPL:
            for i, dst in enumerate(outs4):
                dst[nm] = jnp.stack([unp[l][i][j] for l in range(DEPTH)])
    for i, dst in enumerate(outs4):
        dst["final_norm_w"] = unp[0][i][len(lnames)]
    loss_total = unp[0][0][len(lnames) + 1][0]

    gsh = []
    for nm in SMALL_SHARD:
        j = lnames.index(nm)
        width = Wp[nm].shape[-1]
        gsh.append(jnp.stack([lax.dynamic_slice_in_dim(unp[l][0][j], me * width, width, axis=1)
                              for l in range(DEPTH)]))
    sbuf, stable = _pack(gsh)
    res = _adamw(sbuf[None], _pack([Wp[n] for n in SMALL_SHARD])[0], _pack([Mp[n] for n in SMALL_SHARD])[0],
                 _pack([Vp[n] for n in SMALL_SHARD])[0], sbuf.shape[0], "adamw_conv_w")
    unp2 = [_unpack(r, stable) for r in res]
    for j, nm in enumerate(SMALL_SHARD):
        for i, dst in enumerate((out_g, out_d, out_m, out_v)):
            dst[nm] = unp2[i][j]

    return (loss_total, dx[None], *[out_g[n] for n in WEIGHTS], *[out_d[n] for n in WEIGHTS],
            *[out_m[n] for n in WEIGHTS], *[out_v[n] for n in WEIGHTS])
```

```python
import functools

import jax
import jax.numpy as jnp
from jax import lax
from jax.experimental import pallas as pl
from jax.experimental.pallas import tpu as pltpu

F32 = jnp.float32
BF16 = jnp.bfloat16
HI = lax.Precision.HIGHEST
MESH_IDS = pl.DeviceIdType.MESH

N_DEV = 8
D_MODEL = 2048
DEPTH = 2
POOL_WINDOWS = (2, 4, 8, 16)
POOL_W = 512
HEADS = 6
HD = 128
GDN_W = HEADS * HD
CHUNK = 64
LRU_W = 768
LRU_C = 8.0
D_FF = 3 * D_MODEL
EPS = 1e-6
IN_COLS = 5132
PCOLS = 5376
C_QKV, C_Z, C_AB, C_XR, C_GR = 512, 2816, 3584, 3840, 4608
AB_ORIG_END = 3596
M_GDN, M_LRU = 512, 1280

ADAM_LR, ADAM_B1, ADAM_B2, ADAM_EPS, ADAM_WD, ADAM_STEP = 0.001, 0.9, 0.999, 1e-08, 0.01, 10

VMEM_LIMIT = 56 * 1024 * 1024


def _cp(sem):
    return pltpu.CompilerParams(dimension_semantics=sem, vmem_limit_bytes=VMEM_LIMIT)


def _mm(a, b, ca=1, cb=0, prec=None, cast=True):
    if cast:
        a = a.astype(BF16)
        b = b.astype(BF16)
    return lax.dot_general(a, b, (((ca,), (cb,)), ((), ())), preferred_element_type=F32, precision=prec)


def _bmm(a, b, ca=2, cb=1, prec=None, cast=True):
    if cast:
        a = a.astype(BF16)
        b = b.astype(BF16)
    return lax.dot_general(a, b, (((ca,), (cb,)), ((0,), (0,))), preferred_element_type=F32, precision=prec)


def _sigmoid(x):
    return 1.0 / (1.0 + jnp.exp(-x))


def _log1p(e):
    u = 1.0 + e
    return jnp.where(u == 1.0, e, jnp.log(u) * e / jnp.where(u == 1.0, 1.0, u - 1.0))


def _softplus(x):
    return jnp.maximum(x, 0.0) + _log1p(jnp.exp(-jnp.abs(x)))


def _expm1(x):
    u = jnp.exp(x)
    um = u - 1.0
    safe = jnp.where((u == 1.0) | (um == -1.0), 1.0, jnp.log(u))
    return jnp.where(u == 1.0, x, jnp.where(um == -1.0, -1.0, um * x / safe))


_G0 = 0.7978845608028654
_G1 = 0.044715


def _gelu(x):
    return 0.5 * x * (1.0 + jnp.tanh(_G0 * (x + _G1 * x * x * x)))


def _gelu_and_grad(x):
    th = jnp.tanh(_G0 * (x + _G1 * x * x * x))
    g = 0.5 * x * (1.0 + th)
    dg = 0.5 * (1.0 + th) + 0.5 * x * (1.0 - th * th) * _G0 * (1.0 + 3.0 * _G1 * x * x)
    return g, dg


def _tile(T):
    return min(T, 512)


def _matmul(a, b, *, grid, a_spec, b_spec, out_shape, out_spec, dims, acc_shape, name, add=None, add_spec=None):
    nk = grid[2]
    has_add = add is not None

    def body(*refs):
        if has_add:
            a_ref, b_ref, add_ref, o_ref, acc_ref = refs
        else:
            a_ref, b_ref, o_ref, acc_ref = refs
            add_ref = None
        k = pl.program_id(2)
        p = lax.dot_general(a_ref[...].astype(BF16), b_ref[...].astype(BF16), (dims, ((), ())),
                            preferred_element_type=F32)

        def finish(r):
            if has_add:
                r = r + add_ref[...]
            o_ref[...] = r.astype(o_ref.dtype)

        if nk == 1:
            finish(p)
        else:
            @pl.when(k == 0)
            def _():
                acc_ref[...] = p

            @pl.when(k > 0)
            def _():
                acc_ref[...] += p

            @pl.when(k == nk - 1)
            def _():
                finish(acc_ref[...])

    in_specs = [a_spec, b_spec] + ([add_spec] if has_add else [])
    args = (a, b) + ((add,) if has_add else ())
    return pl.pallas_call(
        body, name=name, grid=grid, in_specs=in_specs, out_specs=out_spec, out_shape=out_shape,
        scratch_shapes=[pltpu.VMEM(acc_shape if nk > 1 else (8, 128), F32)],
        compiler_params=_cp(("parallel", "parallel", "arbitrary")),
    )(*args)


def _pick(n, cands):
    for c in cands:
        if n % c == 0:
            return c
    raise ValueError(f"no tile for {n}")


def _mm_nn(a, b, out_dtype, name, add=None, tn_c=(1024, 768, 512), tk_c=(2048, 1536, 1024, 512, 256)):
    M, K = a.shape
    N = b.shape[1]
    tm = _pick(M, (1024, 512, 256))
    tn = _pick(N, tn_c)
    tk = _pick(K, tk_c)
    return _matmul(
        a, b, grid=(M // tm, N // tn, K // tk),
        a_spec=pl.BlockSpec((tm, tk), lambda i, j, k: (i, k)),
        b_spec=pl.BlockSpec((tk, tn), lambda i, j, k: (k, j)),
        out_shape=jax.ShapeDtypeStruct((M, N), out_dtype),
        out_spec=pl.BlockSpec((tm, tn), lambda i, j, k: (i, j)),
        dims=((1,), (0,)), acc_shape=(tm, tn), name=name, add=add,
        add_spec=pl.BlockSpec((tm, tn), lambda i, j, k: (i, j)))


def _mm_nt(a, b, out_dtype, name, tk_c=(2048, 1536, 1024, 768, 512), tn_c=(1024, 768, 512)):
    M, K = a.shape
    N = b.shape[0]
    tm = _pick(M, (1024, 512, 256))
    tn = _pick(N, tn_c)
    tk = _pick(K, tk_c)
    return _matmul(
        a, b, grid=(M // tm, N // tn, K // tk),
        a_spec=pl.BlockSpec((tm, tk), lambda i, j, k: (i, k)),
        b_spec=pl.BlockSpec((tn, tk), lambda i, j, k: (j, k)),
        out_shape=jax.ShapeDtypeStruct((M, N), out_dtype),
        out_spec=pl.BlockSpec((tm, tn), lambda i, j, k: (i, j)),
        dims=((1,), (1,)), acc_shape=(tm, tn), name=name)


def _mm_tn(a, b, out_dtype, name, tn_c=(1024, 768, 512)):
    K, M = a.shape
    N = b.shape[1]
    tm = _pick(M, (1024, 768, 512))
    tn = _pick(N, tn_c)
    tk = _pick(K, (1024, 512, 256))
    return _matmul(
        a, b, grid=(M // tm, N // tn, K // tk),
        a_spec=pl.BlockSpec((tk, tm), lambda i, j, k: (k, i)),
        b_spec=pl.BlockSpec((tk, tn), lambda i, j, k: (k, j)),
        out_shape=jax.ShapeDtypeStruct((M, N), out_dtype),
        out_spec=pl.BlockSpec((tm, tn), lambda i, j, k: (i, j)),
        dims=((0,), (0,)), acc_shape=(tm, tn), name=name)


def _mm_up(h, wup, name):
    M, K = h.shape
    ns = wup.shape[2]
    tm = _pick(M, (1024, 512, 256))
    tn = ns
    per = ns // tn
    return _matmul(
        h, wup, grid=(M // tm, N_DEV * per, 1),
        a_spec=pl.BlockSpec((tm, K), lambda i, j, k: (i, 0)),
        b_spec=pl.BlockSpec((None, K, tn), lambda i, j, k: (j // per, 0, j % per)),
        out_shape=jax.ShapeDtypeStruct((M, N_DEV * ns), BF16),
        out_spec=pl.BlockSpec((tm, tn), lambda i, j, k: (i, j)),
        dims=((1,), (0,)), acc_shape=(tm, tn), name=name)


def _mm_up_t(dup, wup, name):
    M = dup.shape[1]
    D, ns = wup.shape[1], wup.shape[2]
    tm = _pick(M, (1024, 512, 256))
    tn = D
    tk = ns
    return _matmul(
        dup, wup, grid=(M // tm, D // tn, N_DEV),
        a_spec=pl.BlockSpec((None, tm, tk), lambda i, j, k: (k // 4, i, k % 4)),
        b_spec=pl.BlockSpec((None, tn, tk), lambda i, j, k: (k, j, 0)),
        out_shape=jax.ShapeDtypeStruct((M, D), F32),
        out_spec=pl.BlockSpec((tm, tn), lambda i, j, k: (i, j)),
        dims=((1,), (1,)), acc_shape=(tm, tn), name=name)


def _mm_dup(ht, dup, ns, name):
    M, K = ht.shape
    tm = 1024
    tn = ns
    per = ns // tn
    half = 4 * per
    tk = _pick(K, (2048, 1024, 512, 256))
    return _matmul(
        ht, dup, grid=(M // tm, N_DEV * per, K // tk),
        a_spec=pl.BlockSpec((tm, tk), lambda i, j, k: (i, k)),
        b_spec=pl.BlockSpec((None, tk, tn), lambda i, j, k: (j // half, k, j % half)),
        out_shape=jax.ShapeDtypeStruct((N_DEV, M, ns), BF16),
        out_spec=pl.BlockSpec((None, tm, tn), lambda i, j, k: (j // per, i, j % per)),
        dims=((1,), (0,)), acc_shape=(tm, tn), name=name)


def _norm_fwd(x, w, name):
    T, D = x.shape
    tt = _tile(T)

    def body(x_ref, w_ref, h_ref, ht_ref):
        xv = x_ref[...]
        r = lax.rsqrt(jnp.mean(xv * xv, axis=1, keepdims=True) + EPS)
        hv = xv * r * w_ref[...]
        h_ref[...] = hv.astype(BF16)
        ht_ref[...] = hv.T.astype(BF16)

    return pl.pallas_call(
        body, name=name, grid=(T // tt,),
        in_specs=[pl.BlockSpec((tt, D), lambda t: (t, 0)), pl.BlockSpec((1, D), lambda t: (0, 0))],
        out_specs=[pl.BlockSpec((tt, D), lambda t: (t, 0)), pl.BlockSpec((D, tt), lambda t: (0, t))],
        out_shape=[jax.ShapeDtypeStruct((T, D), BF16), jax.ShapeDtypeStruct((D, T), BF16)],
        compiler_params=_cp(("parallel",)))(x, w)


def _norm_bwd(x, w, dh, dres, name):
    T, D = x.shape
    tt = _tile(T)

    def body(x_ref, w_ref, dh_ref, dres_ref, dx_ref, dxb_ref, dw_ref):
        t = pl.program_id(0)
        xv = x_ref[...]
        r = lax.rsqrt(jnp.mean(xv * xv, axis=1, keepdims=True) + EPS)
        xh = xv * r
        dh_v = dh_ref[...]
        dxh = dh_v * w_ref[...]
        dxv = dres_ref[...] + r * (dxh - xh * jnp.mean(dxh * xh, axis=1, keepdims=True))
        dx_ref[...] = dxv
        dxb_ref[...] = dxv.astype(BF16)
        part = jnp.sum(dh_v * xh, axis=0, keepdims=True)

        @pl.when(t == 0)
        def _():
            dw_ref[...] = part

        @pl.when(t > 0)
        def _():
            dw_ref[...] += part

    row = pl.BlockSpec((tt, D), lambda t: (t, 0))
    vec = pl.BlockSpec((1, D), lambda t: (0, 0))
    return pl.pallas_call(
        body, name=name, grid=(T // tt,), in_specs=[row, vec, row, row], out_specs=[row, row, vec],
        out_shape=[jax.ShapeDtypeStruct((T, D), F32), jax.ShapeDtypeStruct((T, D), BF16),
                   jax.ShapeDtypeStruct((1, D), F32)],
        compiler_params=_cp(("arbitrary",)))(x, w, dh, dres)


def _loss_head(x, w, target):
    T, D = x.shape
    tt = _tile(T)

    def body(x_ref, w_ref, t_ref, loss_ref, dx_ref, dxb_ref, dw_ref):
        t = pl.program_id(0)
        xv = x_ref[...]
        r = lax.rsqrt(jnp.mean(xv * xv, axis=1, keepdims=True) + EPS)
        xh = xv * r
        err = xh * w_ref[...] - t_ref[...]
        lp = 0.5 * jnp.sum(jnp.mean(err * err, axis=1, keepdims=True), axis=0, keepdims=True)
        dy = err * (1.0 / D)
        dxh = dy * w_ref[...]
        dxv = r * (dxh - xh * jnp.mean(dxh * xh, axis=1, keepdims=True))
        dx_ref[...] = dxv
        dxb_ref[...] = dxv.astype(BF16)
        part = jnp.sum(dy * xh, axis=0, keepdims=True)
        lpb = jnp.broadcast_to(lp, (1, 128))

        @pl.when(t == 0)
        def _():
            dw_ref[...] = part
            loss_ref[...] = lpb

        @pl.when(t > 0)
        def _():
            dw_ref[...] += part
            loss_ref[...] += lpb

    row = pl.BlockSpec((tt, D), lambda t: (t, 0))
    vec = pl.BlockSpec((1, D), lambda t: (0, 0))
    return pl.pallas_call(
        body, name="loss_head", grid=(T // tt,), in_specs=[row, vec, row],
        out_specs=[pl.BlockSpec((1, 128), lambda t: (0, 0)), row, row, vec],
        out_shape=[jax.ShapeDtypeStruct((1, 128), F32), jax.ShapeDtypeStruct((T, D), F32),
                   jax.ShapeDtypeStruct((T, D), BF16), jax.ShapeDtypeStruct((1, D), F32)],
        compiler_params=_cp(("arbitrary",)))(x, w, target)


CONV_TT = 4096
CONV_BWD_TT = 2048


def _conv_fwd(x, col0, C, w, b, cb, name):
    T = x.shape[0]
    K = w.shape[0]
    tt = min(T, CONV_TT)
    nt, nc, c0 = T // tt, C // cb, col0 // cb
    has_b = b is not None

    def body(*refs):
        if has_b:
            x_ref, halo_ref, w_ref, b_ref, y_ref = refs
        else:
            x_ref, halo_ref, w_ref, y_ref = refs
        t = pl.program_id(1)
        halo = jnp.where(t == 0, 0.0, halo_ref[...])
        xe = jnp.concatenate([halo, x_ref[...]], axis=0)
        acc = xe * w_ref[K - 1:K, :]
        for j in range(K - 1):
            acc = acc + pltpu.roll(xe, K - 1 - j, 0) * w_ref[j:j + 1, :]
        if has_b:
            acc = acc + b_ref[...]
        y_ref[...] = acc[8:, :]

    in_specs = [pl.BlockSpec((tt, cb), lambda j, t: (t, c0 + j)),
                pl.BlockSpec((8, cb), lambda j, t: (jnp.maximum(t * (tt // 8) - 1, 0), c0 + j)),
                pl.BlockSpec((K, cb), lambda j, t: (0, j))]
    args = [x, x, w]
    if has_b:
        in_specs.append(pl.BlockSpec((1, cb), lambda j, t: (0, j)))
        args.append(b)
    return pl.pallas_call(
        body, name=name, grid=(nc, nt), in_specs=in_specs,
        out_specs=pl.BlockSpec((tt, cb), lambda j, t: (t, j)),
        out_shape=jax.ShapeDtypeStruct((T, C), F32), compiler_params=_cp(("parallel", "parallel")))(*args)


def _conv_bwd(dy, x, col0, w, cb, name, into, want_db=False):
    T, C = dy.shape
    K = w.shape[0]
    tt = min(T, CONV_BWD_TT)
    nt, nc, c0 = T // tt, C // cb, col0 // cb

    def body(*refs):
        if want_db:
            dy_ref, dyn_ref, x_ref, xp_ref, w_ref, _, dx_ref, dw_ref, db_ref = refs
        else:
            dy_ref, dyn_ref, x_ref, xp_ref, w_ref, _, dx_ref, dw_ref = refs
        t = pl.program_id(1)
        dyv = dy_ref[...]
        nxt = jnp.where(t == nt - 1, 0.0, dyn_ref[...])
        dye = jnp.concatenate([dyv, nxt], axis=0)
        n = tt + 8
        acc = dye * w_ref[K - 1:K, :]
        for j in range(K - 1):
            acc = acc + pltpu.roll(dye, n - (K - 1 - j), 0) * w_ref[j:j + 1, :]
        dx_ref[...] = acc[:tt, :].astype(dx_ref.dtype)
        prev = jnp.where(t == 0, 0.0, xp_ref[...])
        xe = jnp.concatenate([prev, x_ref[...]], axis=0)

        @pl.when(t == 0)
        def _():
            dw_ref[...] = jnp.zeros_like(dw_ref)
            if want_db:
                db_ref[...] = jnp.zeros_like(db_ref)

        for j in range(K):
            sh = K - 1 - j
            xs = xe[8:, :] if sh == 0 else pltpu.roll(xe, sh, 0)[8:, :]
            dw_ref[j:j + 1, :] += jnp.sum(dyv * xs, axis=0, keepdims=True)
        if want_db:
            db_ref[...] += jnp.sum(dyv, axis=0, keepdims=True)

    h8 = tt // 8
    in_specs = [pl.BlockSpec((tt, cb), lambda j, t: (t, j)),
                pl.BlockSpec((8, cb), lambda j, t: (jnp.minimum((t + 1) * h8, T // 8 - 1), j)),
                pl.BlockSpec((tt, cb), lambda j, t: (t, c0 + j)),
                pl.BlockSpec((8, cb), lambda j, t: (jnp.maximum(t * h8 - 1, 0), c0 + j)),
                pl.BlockSpec((K, cb), lambda j, t: (0, j)), pl.BlockSpec(memory_space=pl.ANY)]
    out_specs = [pl.BlockSpec((tt, cb), lambda j, t: (t, c0 + j)), pl.BlockSpec((K, cb), lambda j, t: (0, j))]
    out_shape = [jax.ShapeDtypeStruct(into.shape, into.dtype), jax.ShapeDtypeStruct((K, C), F32)]
    if want_db:
        out_specs.append(pl.BlockSpec((1, cb), lambda j, t: (0, j)))
        out_shape.append(jax.ShapeDtypeStruct((1, C), F32))
    return pl.pallas_call(
        body, name=name, grid=(nc, nt), in_specs=in_specs, out_specs=out_specs, out_shape=out_shape,
        input_output_aliases={5: 0}, compiler_params=_cp(("parallel", "arbitrary")))(dy, dy, x, x, w, into)


def _pool_d(ue, g, pos, tt):
    win = POOL_WINDOWS[g]
    ug = ue[:, g * 128:(g + 1) * 128]
    s = ug
    k = 1
    while k < win:
        s = s + pltpu.roll(s, k, 0)
        k *= 2
    cnt = jnp.minimum(pos + 1, win).astype(F32)
    return s[16:, :] / cnt - ug[16:, :]


POOL_TT = 2048


def _pool_fwd(proj, pw, pb, ps):
    T = proj.shape[0]
    tt = min(T, POOL_TT)

    def body(u_ref, halo_ref, w_ref, b_ref, s_ref, y_ref):
        t = pl.program_id(0)
        halo = jnp.where(t == 0, 0.0, halo_ref[...])
        ue = jnp.concatenate([halo, u_ref[...]], axis=0)
        pos = t * tt + lax.broadcasted_iota(jnp.int32, (tt, 1), 0)
        for g in range(4):
            sl = slice(g * 128, (g + 1) * 128)
            d = _pool_d(ue, g, pos, tt)
            yg = _mm(d, w_ref[g]) + b_ref[:, sl]
            y_ref[:, sl] = (yg * s_ref[:, sl]).astype(BF16)

    vec = pl.BlockSpec((1, POOL_W), lambda t: (0, 0))
    return pl.pallas_call(
        body, name="pool_fwd", grid=(T // tt,),
        in_specs=[pl.BlockSpec((tt, POOL_W), lambda t: (t, 0)),
                  pl.BlockSpec((16, POOL_W), lambda t: (jnp.maximum(t * (tt // 16) - 1, 0), 0)),
                  pl.BlockSpec((4, 128, 128), lambda t: (0, 0, 0)), vec, vec],
        out_specs=pl.BlockSpec((tt, POOL_W), lambda t: (t, 0)),
        out_shape=jax.ShapeDtypeStruct((T, D_MODEL), BF16), compiler_params=_cp(("parallel",)))(
            proj, proj, pw, pb, ps)


def _pool_bwd(dmixed, proj, pw, pb, ps):
    T = proj.shape[0]
    tt = min(T, POOL_TT)
    nt = T // tt

    def body(dy_ref, dyn_ref, u_ref, halo_ref, w_ref, b_ref, s_ref, du_ref, dw_ref, db_ref, ds_ref):
        t = pl.program_id(0)
        halo = jnp.where(t == 0, 0.0, halo_ref[...])
        ue = jnp.concatenate([halo, u_ref[...]], axis=0)
        dyv = dy_ref[...]
        nxt = jnp.where(t == nt - 1, 0.0, dyn_ref[...])
        dye = jnp.concatenate([dyv, nxt], axis=0)
        n = tt + 16
        pos = t * tt + lax.broadcasted_iota(jnp.int32, (tt, 1), 0)
        pos_e = t * tt + lax.broadcasted_iota(jnp.int32, (n, 1), 0)

        @pl.when(t == 0)
        def _():
            dw_ref[...] = jnp.zeros_like(dw_ref)
            db_ref[...] = jnp.zeros_like(db_ref)
            ds_ref[...] = jnp.zeros_like(ds_ref)

        for g in range(4):
            win = POOL_WINDOWS[g]
            sl = slice(g * 128, (g + 1) * 128)
            d = _pool_d(ue, g, pos, tt)
            wg = w_ref[g]
            ypre = _mm(d, wg) + b_ref[:, sl]
            sc = s_ref[:, sl]
            ds_ref[:, sl] += jnp.sum(dyv[:, sl] * ypre, axis=0, keepdims=True)
            dyp_e = dye[:, sl] * sc
            dyp = dyp_e[:tt, :]
            db_ref[:, sl] += jnp.sum(dyp, axis=0, keepdims=True)
            dw_ref[g] += _mm(d, dyp, 0, 0)
            dd_e = _mm(dyp_e, wg, 1, 1)
            cnt_e = jnp.minimum(pos_e + 1, win).astype(F32)
            s = dd_e / cnt_e
            k = 1
            while k < win:
                s = s + pltpu.roll(s, n - k, 0)
                k *= 2
            du_ref[:, sl] = (s[:tt, :] - dd_e[:tt, :]).astype(BF16)

    vec = pl.BlockSpec((1, POOL_W), lambda t: (0, 0))
    h16 = tt // 16
    return pl.pallas_call(
        body, name="pool_bwd", grid=(nt,),
        in_specs=[pl.BlockSpec((tt, POOL_W), lambda t: (t, 0)),
                  pl.BlockSpec((16, POOL_W), lambda t: (jnp.minimum((t + 1) * h16, T // 16 - 1), 0)),
                  pl.BlockSpec((tt, POOL_W), lambda t: (t, 0)),
                  pl.BlockSpec((16, POOL_W), lambda t: (jnp.maximum(t * h16 - 1, 0), 0)),
                  pl.BlockSpec((4, 128, 128), lambda t: (0, 0, 0)), vec, vec],
        out_specs=[pl.BlockSpec((tt, POOL_W), lambda t: (t, 0)),
                   pl.BlockSpec((4, 128, 128), lambda t: (0, 0, 0)), vec, vec],
        out_shape=[jax.ShapeDtypeStruct((T, PCOLS), BF16), jax.ShapeDtypeStruct((4, 128, 128), F32),
                   jax.ShapeDtypeStruct((1, POOL_W), F32), jax.ShapeDtypeStruct((1, POOL_W), F32)],
        compiler_params=_cp(("arbitrary",)))(dmixed, dmixed, proj, proj, pw, pb, ps)


def _gdn_pre_fwd(cpre, proj, alog, dtb):
    T = cpre.shape[0]
    tt = _tile(T)

    def body(c_ref, ab_ref, alog_ref, dtb_ref, qkv_ref, bb_ref, gb_ref):
        for p in range(3):
            for h in range(HEADS):
                cc = c_ref[:, (p * HEADS + h) * HD:(p * HEADS + h + 1) * HD]
                s = cc * _sigmoid(cc)
                if p < 2:
                    s = s * lax.rsqrt(jnp.sum(s * s, axis=1, keepdims=True) + EPS)
                if p == 0:
                    s = s * (HD ** -0.5)
                qkv_ref[p, h] = s
        ab = ab_ref[...]
        g = -jnp.exp(alog_ref[...]) * _softplus(ab + dtb_ref[...])
        r64 = lax.broadcasted_iota(jnp.int32, (tt, 1), 0) & (CHUNK - 1)
        k = 1
        while k < CHUNK:
            g = g + jnp.where(r64 >= k, pltpu.roll(g, k, 0), 0.0)
            k *= 2
        sb = _sigmoid(ab)
        for h in range(HEADS):
            gb_ref[h] = jnp.broadcast_to(g[:, h:h + 1], (tt, HD))
            bb_ref[h] = jnp.broadcast_to(sb[:, HEADS + h:HEADS + h + 1], (tt, HD))

    vec = pl.BlockSpec((1, 128), lambda t: (0, 0))
    hb = pl.BlockSpec((HEADS, tt, HD), lambda t: (0, t, 0))
    return pl.pallas_call(
        body, name="gdn_pre_fwd", grid=(T // tt,),
        in_specs=[pl.BlockSpec((tt, 3 * GDN_W), lambda t: (t, 0)),
                  pl.BlockSpec((tt, 128), lambda t: (t, C_AB // 128)), vec, vec],
        out_specs=[pl.BlockSpec((3, HEADS, tt, HD), lambda t: (0, 0, t, 0)), hb, hb],
        out_shape=[jax.ShapeDtypeStruct((3, HEADS, T, HD), F32), jax.ShapeDtypeStruct((HEADS, T, HD), F32),
                   jax.ShapeDtypeStruct((HEADS, T, HD), F32)],
        compiler_params=_cp(("parallel",)))(cpre, proj, alog, dtb)


def _gdn_pre_bwd(dq, dk, dv, cpre, dbb, dgb, proj, alog, dtb, dproj):
    T = cpre.shape[0]
    tt = _tile(T)

    def body(dq_ref, dk_ref, dv_ref, c_ref, dbb_ref, dgb_ref, ab_ref, alog_ref, dtb_ref, _,
             dc_ref, dab_ref, dalog_ref, ddtb_ref):
        t = pl.program_id(0)
        srcs = (dq_ref, dk_ref, dv_ref)
        for p in range(3):
            for h in range(HEADS):
                sl = slice((p * HEADS + h) * HD, (p * HEADS + h + 1) * HD)
                cc = c_ref[:, sl]
                sg = _sigmoid(cc)
                s = cc * sg
                dyv = srcs[p][h]
                if p < 2:
                    r = lax.rsqrt(jnp.sum(s * s, axis=1, keepdims=True) + EPS)
                    y = s * r
                    if p == 0:
                        dyv = dyv * (HD ** -0.5)
                    ds = r * (dyv - y * jnp.sum(dyv * y, axis=1, keepdims=True))
                else:
                    ds = dyv
                dc_ref[:, sl] = ds * sg * (1.0 + cc * (1.0 - sg))
        lane = lax.broadcasted_iota(jnp.int32, (tt, 128), 1)
        dg = jnp.zeros((tt, 128), F32)
        dbeta = jnp.zeros((tt, 128), F32)
        for h in range(HEADS):
            dg = jnp.where(lane == h, dgb_ref[h], dg)
            dbeta = jnp.where(lane == HEADS + h, dbb_ref[h], dbeta)
        r64 = lax.broadcasted_iota(jnp.int32, (tt, 1), 0) & (CHUNK - 1)
        k = 1
        while k < CHUNK:
            dg = dg + jnp.where(r64 < CHUNK - k, pltpu.roll(dg, tt - k, 0), 0.0)
            k *= 2
        ab = ab_ref[...]
        e = jnp.exp(alog_ref[...])
        xx = ab + dtb_ref[...]
        g = -e * _softplus(xx)
        da = jnp.where(lane < HEADS, dg * (-e) * _sigmoid(xx), 0.0)
        pa = jnp.sum(jnp.where(lane < HEADS, dg * g, 0.0), axis=0, keepdims=True)
        pd = jnp.sum(da, axis=0, keepdims=True)

        @pl.when(t == 0)
        def _():
            dalog_ref[...] = pa
            ddtb_ref[...] = pd

        @pl.when(t > 0)
        def _():
            dalog_ref[...] += pa
            ddtb_ref[...] += pd

        sb = _sigmoid(ab)
        dab_ref[:, :128] = jnp.where(lane < HEADS, da, dbeta * sb * (1.0 - sb)).astype(BF16)
        dab_ref[:, 128:] = jnp.zeros((tt, 128), BF16)

    vec = pl.BlockSpec((1, 128), lambda t: (0, 0))
    hb = pl.BlockSpec((HEADS, tt, HD), lambda t: (0, t, 0))
    return pl.pallas_call(
        body, name="gdn_pre_bwd", grid=(T // tt,),
        in_specs=[hb, hb, hb, pl.BlockSpec((tt, 3 * GDN_W), lambda t: (t, 0)), hb, hb,
                  pl.BlockSpec((tt, 128), lambda t: (t, C_AB // 128)), vec, vec, pl.BlockSpec(memory_space=pl.ANY)],
        out_specs=[pl.BlockSpec((tt, 3 * GDN_W), lambda t: (t, 0)),
                   pl.BlockSpec((tt, 256), lambda t: (t, C_AB // 256)), vec, vec],
        out_shape=[jax.ShapeDtypeStruct((T, 3 * GDN_W), F32), jax.ShapeDtypeStruct(dproj.shape, dproj.dtype),
                   jax.ShapeDtypeStruct((1, 128), F32), jax.ShapeDtypeStruct((1, 128), F32)],
        input_output_aliases={9: 1},
        compiler_params=_cp(("arbitrary",)))(dq, dk, dv, cpre, dbb, dgb, proj, alog, dtb, dproj)


def _split2(x):
    hi = x.astype(BF16)
    return hi, (x - hi.astype(F32)).astype(BF16)


def _bmm3s(a2, b2, ca=2, cb=1):
    def f(x, y):
        return lax.dot_general(x, y, (((ca,), (cb,)), ((0,), (0,))), preferred_element_type=F32)

    return f(a2[0], b2[0]) + (f(a2[0], b2[1]) + f(a2[1], b2[0]))


def _bmm3(a, b, ca=2, cb=1):
    return _bmm3s(_split2(a), _split2(b), ca, cb)


def _tri_inv(a):
    nb = a.shape[0]
    ri = lax.broadcasted_iota(jnp.int32, (nb, CHUNK, CHUNK), 1)
    ci = lax.broadcasted_iota(jnp.int32, (nb, CHUNK, CHUNK), 2)
    n = -a
    p = jnp.where(ri == ci, 1.0, 0.0) + n
    n2 = _split2(n)
    for _ in range(5):
        n2 = _split2(_bmm3s(n2, n2))
        p = p + _bmm3s(_split2(p), n2)
    return p


def _gdn_chunk_common(q, k, v, bb3, gb3, tm_saved=None):
    nb = q.shape[0]
    need_t = tm_saved is not None
    beta = bb3[:, :, 0:1]
    gcol = gb3[:, :, 0:1]
    bcol = bb3[:, :, :CHUNK]
    gcm = gb3[:, :, :CHUNK]
    oh = jnp.where(lax.broadcasted_iota(jnp.int32, (nb, CHUNK, HD), 2) == 0, 1.0, 0.0)
    grow = _bmm(oh, gb3, 2, 2, HI, False)
    ri = lax.broadcasted_iota(jnp.int32, (nb, CHUNK, CHUNK), 1)
    ci = lax.broadcasted_iota(jnp.int32, (nb, CHUNK, CHUNK), 2)
    tril, stl = ri >= ci, ri > ci
    dg = gcm - grow
    dec = jnp.where(tril, jnp.exp(jnp.where(tril, dg, 0.0)), 0.0)
    kk = _bmm(k, k, 2, 2)
    qk = _bmm(q, k, 2, 2)
    tm = tm_saved if need_t else _tri_inv(jnp.where(stl, bcol * kk * dec, 0.0))
    gam = jnp.exp(gcol)
    glast = gb3[:, CHUNK - 1:CHUNK, 0:1]
    egl = jnp.exp(glast)
    rw = k * (beta * gam)
    ru = v * beta
    wu = _bmm3(tm, jnp.concatenate([rw, ru], axis=2), 2, 1)
    kdf = jnp.exp(glast - gcol)
    out = dict(beta=beta, bcol=bcol, tril=tril, stl=stl, dec=dec, kk=kk, qk=qk, tm=tm, gam=gam, egl=egl,
               rw=rw, wu=wu, at=qk * dec, qd=q * gam, kdf=kdf, kd=k * kdf)
    if need_t:
        brow = _bmm(oh, bb3, 2, 2, HI, False)
        triu, stu = ri <= ci, ri < ci
        dect = jnp.where(triu, jnp.exp(jnp.where(triu, -dg, 0.0)), 0.0)
        qkt = _bmm(k, q, 2, 2)
        eye = jnp.where(ri == ci, 1.0, 0.0)
        out.update(brow=brow, triu=triu, stu=stu, dect=dect, qkt=qkt, tmt=_bmm3(eye, tm, 2, 2),
                   att=qkt * dect)
    return out


def _gdn_rows(T, fwd=False):
    return min(T, 1024 if fwd else 512)


HP = 2


def _gdn_fwd(qkv, bb, gb, proj, nw, mixed):
    T = proj.shape[0]
    R = _gdn_rows(T, fwd=True)
    nb = R // CHUNK
    B = HP * nb

    def body(q_ref, k_ref, v_ref, bb_ref, gb_ref, z_ref, nw_ref, _, y_ref, st_ref, tm_ref,
             s_ref, w_s, u_s, at_s, qd_s, kd_s):
        t = pl.program_id(1)

        @pl.when(t == 0)
        def _():
            s_ref[...] = jnp.zeros_like(s_ref)

        sh = (B, CHUNK, HD)
        q, k, v = q_ref[...].reshape(sh), k_ref[...].reshape(sh), v_ref[...].reshape(sh)
        c = _gdn_chunk_common(q, k, v, bb_ref[...].reshape(sh), gb_ref[...].reshape(sh))
        tm_ref[...] = c["tm"].reshape(HP, nb, CHUNK, CHUNK)
        w_s[...] = c["wu"][:, :, :HD]
        u_s[...] = c["wu"][:, :, HD:]
        at_s[...] = c["at"]
        qd_s[...] = c["qd"]
        kd_s[...] = c["kd"]
        egl = c["egl"]
        nwv = nw_ref[...]
        for n in range(nb):
            rows = slice(n * CHUNK, (n + 1) * CHUNK)
            for hh in range(HP):
                b = hh * nb + n
                cols = slice(hh * HD, (hh + 1) * HD)
                s = s_ref[hh]
                st_ref[hh, n] = s
                vn = u_s[b] - _mm(w_s[b], s)
                o = _mm(qd_s[b], s) + _mm(at_s[b], vn)
                s_ref[hh] = s * egl[b] + _mm(kd_s[b], vn, 0, 0)
                zz = z_ref[rows, cols]
                on = o * lax.rsqrt(jnp.mean(o * o, axis=1, keepdims=True) + EPS)
                y_ref[rows, cols] = (on * nwv * (zz * _sigmoid(zz))).astype(BF16)

    def hm(p):
        return pl.BlockSpec((None, HP, R, HD), lambda h, t: (p, h, t, 0))

    hb = pl.BlockSpec((HP, R, HD), lambda h, t: (h, t, 0))
    cs = pltpu.VMEM((B, CHUNK, HD), F32)
    return pl.pallas_call(
        body, name="gdn_fwd", grid=(HEADS // HP, T // R),
        in_specs=[hm(0), hm(1), hm(2), hb, hb,
                  pl.BlockSpec((R, HP * HD), lambda h, t: (t, C_Z // (HP * HD) + h)),
                  pl.BlockSpec((1, HD), lambda h, t: (0, 0)), pl.BlockSpec(memory_space=pl.ANY)],
        out_specs=[pl.BlockSpec((R, HP * HD), lambda h, t: (t, M_GDN // (HP * HD) + h)),
                   pl.BlockSpec((HP, nb, HD, HD), lambda h, t: (h, t, 0, 0)),
                   pl.BlockSpec((HP, nb, CHUNK, CHUNK), lambda h, t: (h, t, 0, 0))],
        out_shape=[jax.ShapeDtypeStruct(mixed.shape, mixed.dtype),
                   jax.ShapeDtypeStruct((HEADS, T // CHUNK, HD, HD), F32),
                   jax.ShapeDtypeStruct((HEADS, T // CHUNK, CHUNK, CHUNK), F32)],
        scratch_shapes=[pltpu.VMEM((HP, HD, HD), F32), cs, cs, pltpu.VMEM((B, CHUNK, CHUNK), F32), cs, cs],
        input_output_aliases={7: 0},
        compiler_params=_cp(("parallel", "arbitrary")))(qkv, qkv, qkv, bb, gb, proj, nw, mixed)


def _gdn_bwd(qkv, bb, gb, proj, nw, states, tms, dmixed, dproj):
    T = proj.shape[0]
    R = _gdn_rows(T)
    nb = R // CHUNK
    ntb = T // R
    B = HP * nb

    def by_head(ref):
        return jnp.concatenate([ref[:, hh * HD:(hh + 1) * HD].reshape(nb, CHUNK, HD) for hh in range(HP)], axis=0)

    def body(q_ref, k_ref, v_ref, bb_ref, gb_ref, z_ref, nw_ref, st_ref, tm_ref, dy_ref, _,
             dq_ref, dk_ref, dv_ref, dbb_ref, dgb_ref, dz_ref, dnw_ref,
             ds_ref, att_s, do_s, kd_s, vn_s, qd_s, w_s, dvn_s, dkd_s, dgl_s):
        hp = pl.program_id(0)
        t = pl.program_id(1)

        @pl.when(t == 0)
        def _():
            ds_ref[...] = jnp.zeros_like(ds_ref)

        @pl.when((t == 0) & (hp == 0))
        def _():
            dnw_ref[...] = jnp.zeros_like(dnw_ref)

        sh = (B, CHUNK, HD)
        q, k, v = q_ref[...].reshape(sh), k_ref[...].reshape(sh), v_ref[...].reshape(sh)
        c = _gdn_chunk_common(q, k, v, bb_ref[...].reshape(sh), gb_ref[...].reshape(sh),
                              tm_ref[...].reshape(B, CHUNK, CHUNK))
        w, u = c["wu"][:, :, :HD], c["wu"][:, :, HD:]
        sall = st_ref[...].reshape(B, HD, HD)
        vn = u - _bmm(w, sall, 2, 1)
        o = _bmm(c["qd"], sall, 2, 1) + _bmm(c["at"], vn, 2, 1)
        z = by_head(z_ref)
        dy = by_head(dy_ref)
        nwv = nw_ref[...].reshape(1, 1, HD)
        rs = lax.rsqrt(jnp.mean(o * o, axis=2, keepdims=True) + EPS)
        on = o * rs
        sg = _sigmoid(z)
        sz = z * sg
        dnw_ref[...] += jnp.sum(jnp.sum(dy * on * sz, axis=0), axis=0, keepdims=True)
        dz3 = dy * on * nwv * (sg * (1.0 + z * (1.0 - sg)))
        for hh in range(HP):
            dz_ref[:, hh * HD:(hh + 1) * HD] = dz3[hh * nb:(hh + 1) * nb].reshape(R, HD).astype(BF16)
        don = dy * nwv * sz
        do = rs * (don - on * jnp.mean(don * on, axis=2, keepdims=True))
        dqd = _bmm(do, sall, 2, 2)
        dat = jnp.where(c["tril"], _bmm(do, vn, 2, 2), 0.0)
        datt = jnp.where(c["triu"], _bmm(vn, do, 2, 2), 0.0)
        att_s[...] = c["att"]
        do_s[...] = do
        kd_s[...] = c["kd"]
        vn_s[...] = vn
        qd_s[...] = c["qd"]
        w_s[...] = w
        egl = c["egl"]
        for n in reversed(range(nb)):
            for hh in range(HP):
                b = hh * nb + n
                dso = ds_ref[hh]
                dvn_n = _mm(att_s[b], do_s[b]) + _mm(kd_s[b], dso)
                dkd_s[b] = _mm(vn_s[b], dso, 1, 1)
                dgl = egl[b] * jnp.sum(jnp.sum(st_ref[hh, n] * dso, axis=1, keepdims=True), axis=0, keepdims=True)
                dgl_s[b] = jnp.broadcast_to(dgl, (8, HD))
                ds_ref[hh] = egl[b] * dso + _mm(qd_s[b], do_s[b], 0, 0) - _mm(w_s[b], dvn_n, 0, 0)
                dvn_s[b] = dvn_n
        dvn = dvn_s[...]
        dkd = dkd_s[...]
        dgl = dgl_s[...][:, 0:1, 0:1]
        dw = -_bmm(dvn, sall, 2, 2)
        dr = _bmm3(c["tmt"], jnp.concatenate([dw, dvn], axis=2), 2, 1)
        drw, dru = dr[:, :, :HD], dr[:, :, HD:]
        wu = c["wu"]
        dr2, wu2 = _split2(dr), _split2(wu)
        da = -jnp.where(c["stl"], _bmm3s(dr2, wu2, 2, 2), 0.0)
        da_t = -jnp.where(c["stu"], _bmm3s(wu2, dr2, 2, 2), 0.0)
        beta, gam, dec, dect, kk = c["beta"], c["gam"], c["dec"], c["dect"], c["kk"]
        bcol, brow = c["bcol"], c["brow"]
        dbeta = (jnp.sum(da * kk * dec, axis=2, keepdims=True)
                 + jnp.sum(drw * k * gam + dru * v, axis=2, keepdims=True))
        dkk = bcol * da * dec
        dkk_t = brow * da_t * dect
        e = (bcol * da * kk + dat * c["qk"]) * dec
        e_t = (brow * da_t * kk + datt * c["qkt"]) * dect
        kd = c["kd"]
        dq_ref[...] = (_bmm(dat * dec, k, 2, 1) + dqd * gam).reshape(HP, R, HD)
        dk_ref[...] = (_bmm(datt * dect, q, 2, 1) + _bmm(dkk + dkk_t, k, 2, 1) + dkd * c["kdf"]
                       + drw * (beta * gam)).reshape(HP, R, HD)
        dv_ref[...] = (dru * beta).reshape(HP, R, HD)
        skd = jnp.sum(dkd * kd, axis=2, keepdims=True)
        dgc = (jnp.sum(e, axis=2, keepdims=True) - jnp.sum(e_t, axis=2, keepdims=True)
               + jnp.sum(drw * c["rw"] + dqd * c["qd"], axis=2, keepdims=True) - skd)
        tot = jnp.sum(skd, axis=1, keepdims=True) + dgl
        rowi = lax.broadcasted_iota(jnp.int32, (B, CHUNK, 1), 1)
        dgc = dgc + jnp.where(rowi == CHUNK - 1, tot, 0.0)
        dbb_ref[...] = jnp.broadcast_to(dbeta, sh).reshape(HP, R, HD)
        dgb_ref[...] = jnp.broadcast_to(dgc, sh).reshape(HP, R, HD)

    def rt(t):
        return ntb - 1 - t

    def hm(p):
        return pl.BlockSpec((None, HP, R, HD), lambda h, t: (p, h, rt(t), 0))

    hb = pl.BlockSpec((HP, R, HD), lambda h, t: (h, rt(t), 0))
    cs = pltpu.VMEM((B, CHUNK, HD), F32)
    ob = jax.ShapeDtypeStruct((HEADS, T, HD), F32)
    return pl.pallas_call(
        body, name="gdn_bwd", grid=(HEADS // HP, ntb),
        in_specs=[hm(0), hm(1), hm(2), hb, hb,
                  pl.BlockSpec((R, HP * HD), lambda h, t: (rt(t), C_Z // (HP * HD) + h)),
                  pl.BlockSpec((1, HD), lambda h, t: (0, 0)),
                  pl.BlockSpec((HP, nb, HD, HD), lambda h, t: (h, rt(t), 0, 0)),
                  pl.BlockSpec((HP, nb, CHUNK, CHUNK), lambda h, t: (h, rt(t), 0, 0)),
                  pl.BlockSpec((R, HP * HD), lambda h, t: (rt(t), M_GDN // (HP * HD) + h)),
                  pl.BlockSpec(memory_space=pl.ANY)],
        out_specs=[hb, hb, hb, hb, hb, pl.BlockSpec((R, HP * HD), lambda h, t: (rt(t), C_Z // (HP * HD) + h)),
                   pl.BlockSpec((1, HD), lambda h, t: (0, 0))],
        out_shape=[ob, ob, ob, ob, ob, jax.ShapeDtypeStruct(dproj.shape, dproj.dtype),
                   jax.ShapeDtypeStruct((1, HD), F32)],
        scratch_shapes=[pltpu.VMEM((HP, HD, HD), F32), pltpu.VMEM((B, CHUNK, CHUNK), F32), cs, cs, cs, cs, cs, cs,
                        cs, pltpu.VMEM((B, 8, HD), F32)],
        input_output_aliases={10: 5},
        compiler_params=_cp(("arbitrary", "arbitrary")))(qkv, qkv, qkv, bb, gb, proj, nw, states, tms, dmixed,
                                                         dproj)


LRU_TT = 1024


def _lru_gates(xc, wa, ba, wx, bx, lam, gpos):
    xb = xc.astype(BF16)
    r = _sigmoid(_mm(xb, wa) + ba)
    i = _sigmoid(_mm(xb, wx) + bx)
    sp = _softplus(-lam)
    log_a = -LRU_C * r * sp
    a = jnp.exp(log_a)
    mult = jnp.where(gpos == 0, 1.0, jnp.sqrt(-_expm1(2.0 * log_a)))
    return r, i, sp, a, mult


def _lru_fwd(xc, proj, wa, ba, wx, bx, lam, mixed):
    T = xc.shape[0]
    tt = min(T, LRU_TT)

    def body(xc_ref, gr_ref, wa_ref, ba_ref, wx_ref, bx_ref, lam_ref, _, y_ref, h_ref, carry_ref):
        t = pl.program_id(1)

        @pl.when(t == 0)
        def _():
            carry_ref[...] = jnp.zeros_like(carry_ref)

        row = lax.broadcasted_iota(jnp.int32, (tt, 1), 0)
        xcv = xc_ref[...]
        r, i, sp, a, mult = _lru_gates(xcv, wa_ref[...], ba_ref[...], wx_ref[...], bx_ref[...], lam_ref[...],
                                       t * tt + row)
        av, bv = a, mult * i * xcv
        k = 1
        while k < tt:
            a_s = jnp.where(row >= k, pltpu.roll(av, k, 0), 1.0)
            b_s = jnp.where(row >= k, pltpu.roll(bv, k, 0), 0.0)
            bv = bv + av * b_s
            av = av * a_s
            k *= 2
        h = bv + av * carry_ref[0:1, :]
        carry_ref[...] = jnp.broadcast_to(h[tt - 1:tt, :], (8, 128))
        h_ref[...] = h
        y_ref[...] = (h * _gelu(gr_ref[...])).astype(BF16)

    blk = pl.BlockSpec((tt, 128), lambda j, t: (t, j))
    vec = pl.BlockSpec((1, 128), lambda j, t: (0, j))
    mat = pl.BlockSpec((None, 128, 128), lambda j, t: (j, 0, 0))
    return pl.pallas_call(
        body, name="lru_fwd", grid=(LRU_W // 128, T // tt),
        in_specs=[blk, pl.BlockSpec((tt, 128), lambda j, t: (t, C_GR // 128 + j)), mat, vec, mat, vec, vec,
                  pl.BlockSpec(memory_space=pl.ANY)],
        out_specs=[pl.BlockSpec((tt, 128), lambda j, t: (t, M_LRU // 128 + j)), blk],
        out_shape=[jax.ShapeDtypeStruct(mixed.shape, mixed.dtype), jax.ShapeDtypeStruct((T, LRU_W), F32)],
        scratch_shapes=[pltpu.VMEM((8, 128), F32)], input_output_aliases={7: 0},
        compiler_params=_cp(("parallel", "arbitrary")))(xc, proj, wa, ba, wx, bx, lam, mixed)


def _lru_bwd(dmixed, xc, proj, hst, wa, ba, wx, bx, lam, dproj):
    T = xc.shape[0]
    tt = min(T, LRU_TT)
    nt = T // tt

    def body(dy_ref, xc_ref, gr_ref, h_ref, hp_ref, wa_ref, ba_ref, wx_ref, bx_ref, lam_ref, _,
             dxc_ref, dgr_ref, dwa_ref, dwx_ref, dba_ref, dbx_ref, dlam_ref, lc_ref, ac_ref):
        t = pl.program_id(1)
        tr = nt - 1 - t

        @pl.when(t == 0)
        def _():
            lc_ref[...] = jnp.zeros_like(lc_ref)
            ac_ref[...] = jnp.zeros_like(ac_ref)
            dwa_ref[...] = jnp.zeros_like(dwa_ref)
            dwx_ref[...] = jnp.zeros_like(dwx_ref)
            dba_ref[...] = jnp.zeros_like(dba_ref)
            dbx_ref[...] = jnp.zeros_like(dbx_ref)
            dlam_ref[...] = jnp.zeros_like(dlam_ref)

        row = lax.broadcasted_iota(jnp.int32, (tt, 1), 0)
        gpos = tr * tt + row
        xcv = xc_ref[...]
        wav, wxv, lamv = wa_ref[...], wx_ref[...], lam_ref[...]
        r, i, sp, a, mult = _lru_gates(xcv, wav, ba_ref[...], wxv, bx_ref[...], lamv, gpos)
        h = h_ref[...]
        dy = dy_ref[...]
        gg, dgg = _gelu_and_grad(gr_ref[...])
        dgr_ref[...] = (dy * h * dgg).astype(BF16)
        bv = dy * gg
        cv = jnp.where(row < tt - 1, pltpu.roll(a, tt - 1, 0), ac_ref[0:1, :])
        k = 1
        while k < tt:
            c_s = jnp.where(row < tt - k, pltpu.roll(cv, tt - k, 0), 1.0)
            b_s = jnp.where(row < tt - k, pltpu.roll(bv, tt - k, 0), 0.0)
            bv = bv + cv * b_s
            cv = cv * c_s
            k *= 2
        lm = bv + cv * lc_ref[0:1, :]
        lc_ref[...] = jnp.broadcast_to(lm[0:1, :], (8, 128))
        ac_ref[...] = jnp.broadcast_to(a[0:1, :], (8, 128))
        hp = jnp.where(tr == 0, 0.0, hp_ref[...])
        hs = pltpu.roll(jnp.concatenate([hp, h], axis=0), 1, 0)[8:, :]
        da = lm * hs
        dmult = lm * i * xcv
        di = lm * mult * xcv
        dxc = lm * mult * i
        dlog_a = a * da - jnp.where(gpos == 0, 0.0, dmult * a * a / mult)
        dr = dlog_a * (-LRU_C * sp)
        dsp = jnp.sum(dlog_a * (-LRU_C * r), axis=0, keepdims=True)
        dlam_ref[...] += dsp * (-_sigmoid(-lamv))
        dpr = dr * r * (1.0 - r)
        dpi = di * i * (1.0 - i)
        dba_ref[...] += jnp.sum(dpr, axis=0, keepdims=True)
        dbx_ref[...] += jnp.sum(dpi, axis=0, keepdims=True)
        dwa_ref[...] += _mm(xcv, dpr, 0, 0)
        dwx_ref[...] += _mm(xcv, dpi, 0, 0)
        dxc_ref[...] = dxc + _mm(dpr, wav, 1, 1) + _mm(dpi, wxv, 1, 1)

    def rt(t):
        return nt - 1 - t

    blk = pl.BlockSpec((tt, 128), lambda j, t: (rt(t), j))
    vec = pl.BlockSpec((1, 128), lambda j, t: (0, j))
    mat = pl.BlockSpec((None, 128, 128), lambda j, t: (j, 0, 0))
    h8 = tt // 8
    mshape = jax.ShapeDtypeStruct((LRU_W // 128, 128, 128), F32)
    vshape = jax.ShapeDtypeStruct((1, LRU_W), F32)
    return pl.pallas_call(
        body, name="lru_bwd", grid=(LRU_W // 128, nt),
        in_specs=[pl.BlockSpec((tt, 128), lambda j, t: (rt(t), M_LRU // 128 + j)), blk,
                  pl.BlockSpec((tt, 128), lambda j, t: (rt(t), C_GR // 128 + j)), blk,
                  pl.BlockSpec((8, 128), lambda j, t: (jnp.maximum(rt(t) * h8 - 1, 0), j)),
                  mat, vec, mat, vec, vec, pl.BlockSpec(memory_space=pl.ANY)],
        out_specs=[blk, pl.BlockSpec((tt, 128), lambda j, t: (rt(t), C_GR // 128 + j)), mat, mat, vec, vec, vec],
        out_shape=[jax.ShapeDtypeStruct((T, LRU_W), F32), jax.ShapeDtypeStruct(dproj.shape, dproj.dtype),
                   mshape, mshape, vshape, vshape, vshape],
        scratch_shapes=[pltpu.VMEM((8, 128), F32), pltpu.VMEM((8, 128), F32)], input_output_aliases={10: 1},
        compiler_params=_cp(("parallel", "arbitrary")))(dmixed, xc, proj, hst, hst, wa, ba, wx, bx, lam, dproj)


FFN_CB = 512
FFN_K = 3


def _ffn_conv(ge, w_ref):
    acc = ge * w_ref[FFN_K - 1:FFN_K, :]
    for j in range(FFN_K - 1):
        acc = acc + pltpu.roll(ge, FFN_K - 1 - j, 0) * w_ref[j:j + 1, :]
    return acc


FFN_HALO = 16
FFN_TT = 2048


def _ffn_gate_fwd(up, w):
    T = up.shape[0]
    tt = min(T, FFN_TT)
    cb = FFN_CB
    nc = D_FF // cb
    hh = tt // FFN_HALO

    def body(g_ref, gp_ref, v_ref, w_ref, o_ref, ot_ref):
        t = pl.program_id(1)
        prev = jnp.where(t == 0, 0.0, gp_ref[...].astype(F32))
        ge = jnp.concatenate([prev, g_ref[...].astype(F32)], axis=0)
        gc = _ffn_conv(ge, w_ref)[FFN_HALO:, :]
        a = _gelu(gc) * v_ref[...].astype(F32)
        o_ref[...] = a.astype(BF16)
        ot_ref[...] = a.T.astype(BF16)

    return pl.pallas_call(
        body, name="ffn_gate_fwd", grid=(nc, T // tt),
        in_specs=[pl.BlockSpec((tt, cb), lambda j, t: (t, j)),
                  pl.BlockSpec((FFN_HALO, cb), lambda j, t: (jnp.maximum(t * hh - 1, 0), j)),
                  pl.BlockSpec((tt, cb), lambda j, t: (t, nc + j)),
                  pl.BlockSpec((FFN_K, cb), lambda j, t: (0, j))],
        out_specs=[pl.BlockSpec((tt, cb), lambda j, t: (t, j)), pl.BlockSpec((cb, tt), lambda j, t: (j, t))],
        out_shape=[jax.ShapeDtypeStruct((T, D_FF), BF16), jax.ShapeDtypeStruct((D_FF, T), BF16)],
        compiler_params=_cp(("parallel", "parallel")))(up, up, up, w)


def _ffn_gate_bwd(dact, up, w):
    T = up.shape[0]
    tt = min(T, FFN_TT)
    cb = FFN_CB
    nc = D_FF // cb
    nt = T // tt
    hh = tt // FFN_HALO
    H = FFN_HALO

    def body(d_ref, dn_ref, g_ref, gp_ref, gn_ref, v_ref, vn_ref, w_ref, dup_ref, dw_ref):
        t = pl.program_id(1)
        prev = jnp.where(t == 0, 0.0, gp_ref[...].astype(F32))
        ge = jnp.concatenate([prev, g_ref[...].astype(F32), gn_ref[...].astype(F32)], axis=0)
        gc = _ffn_conv(ge, w_ref)[H:, :]
        gg, dgg = _gelu_and_grad(gc)
        de = jnp.concatenate([d_ref[...].astype(F32), jnp.where(t == nt - 1, 0.0, dn_ref[...].astype(F32))], axis=0)
        ve = jnp.concatenate([v_ref[...].astype(F32), vn_ref[...].astype(F32)], axis=0)
        dup_ref[1] = (de * gg)[:tt, :].astype(BF16)
        dgc = de * ve * dgg
        n = tt + H
        acc = dgc * w_ref[FFN_K - 1:FFN_K, :]
        for j in range(FFN_K - 1):
            acc = acc + pltpu.roll(dgc, n - (FFN_K - 1 - j), 0) * w_ref[j:j + 1, :]
        dup_ref[0] = acc[:tt, :].astype(BF16)

        @pl.when(t == 0)
        def _():
            dw_ref[...] = jnp.zeros_like(dw_ref)

        dgm = dgc[:tt, :]
        for j in range(FFN_K):
            sh = FFN_K - 1 - j
            xs = ge[H:H + tt, :] if sh == 0 else pltpu.roll(ge, sh, 0)[H:H + tt, :]
            dw_ref[j:j + 1, :] += jnp.sum(dgm * xs, axis=0, keepdims=True)

    def nxt(t):
        return jnp.minimum((t + 1) * hh, T // H - 1)

    return pl.pallas_call(
        body, name="ffn_gate_bwd", grid=(nc, nt),
        in_specs=[pl.BlockSpec((tt, cb), lambda j, t: (t, j)),
                  pl.BlockSpec((H, cb), lambda j, t: (nxt(t), j)),
                  pl.BlockSpec((tt, cb), lambda j, t: (t, j)),
                  pl.BlockSpec((H, cb), lambda j, t: (jnp.maximum(t * hh - 1, 0), j)),
                  pl.BlockSpec((H, cb), lambda j, t: (nxt(t), j)),
                  pl.BlockSpec((tt, cb), lambda j, t: (t, nc + j)),
                  pl.BlockSpec((H, cb), lambda j, t: (nxt(t), nc + j)),
                  pl.BlockSpec((FFN_K, cb), lambda j, t: (0, j))],
        out_specs=[pl.BlockSpec((2, tt, cb), lambda j, t: (0, t, j)),
                   pl.BlockSpec((FFN_K, cb), lambda j, t: (0, j))],
        out_shape=[jax.ShapeDtypeStruct((2, T, D_FF), BF16), jax.ShapeDtypeStruct((FFN_K, D_FF), F32)],
        compiler_params=_cp(("parallel", "arbitrary")))(dact, dact, up, up, up, up, up, w)


def _row_tile(rows, cap):
    best = 8
    for r in range(8, min(rows, cap) + 1, 8):
        if rows % r == 0:
            best = r
    return best


def _adamw(parts, w, m, v, rt, name, layer=None, prev=None):
    P, R, C = parts.shape

    def body(p_ref, w_ref, m_ref, v_ref, *rest):
        g_ref, d_ref, mo_ref, vo_ref = rest[-4:]
        g = p_ref[0].astype(F32)
        for i in range(1, P):
            g = g + p_ref[i].astype(F32)
        wv = w_ref[...]
        mn = ADAM_B1 * m_ref[...] + (1.0 - ADAM_B1) * g
        vn = ADAM_B2 * v_ref[...] + (1.0 - ADAM_B2) * (g * g)
        m_hat = mn / (1.0 - ADAM_B1 ** ADAM_STEP)
        v_hat = vn / (1.0 - ADAM_B2 ** ADAM_STEP)
        g_ref[...] = g
        d_ref[...] = -ADAM_LR * (m_hat / (jnp.sqrt(v_hat) + ADAM_EPS) + ADAM_WD * wv)
        mo_ref[...] = mn
        vo_ref[...] = vn

    if layer is None:
        blk = pl.BlockSpec((rt, C), lambda r: (r, 0))
        sh = jax.ShapeDtypeStruct((R, C), F32)
    else:
        blk = pl.BlockSpec((None, rt, C), lambda r: (layer, r, 0))
        sh = jax.ShapeDtypeStruct(w.shape, F32)
    extra = list(prev) if prev is not None else []
    return pl.pallas_call(
        body, name=name, grid=(R // rt,),
        in_specs=[pl.BlockSpec((P, rt, C), lambda r: (0, r, 0)), blk, blk, blk]
        + [pl.BlockSpec(memory_space=pl.ANY)] * len(extra),
        out_specs=[blk, blk, blk, blk], out_shape=[sh, sh, sh, sh],
        input_output_aliases={4 + i: i for i in range(len(extra))},
        compiler_params=_cp(("parallel",)))(parts, w, m, v, *extra)


def _peer(k):
    x, y, c = lax.axis_index("x"), lax.axis_index("y"), lax.axis_index("c")
    px = 1 - x if k & 4 else x
    py = 1 - y if k & 2 else y
    pc = 1 - c if k & 1 else c
    return (px, py, pc), 4 * px + 2 * py + pc


def _all_gather(x, name):
    R, C = x.shape

    def body(x_ref, o_ref, send_sems, recv_sems, local_sem):
        me = 4 * lax.axis_index("x") + 2 * lax.axis_index("y") + lax.axis_index("c")
        mine = pltpu.make_async_copy(x_ref, o_ref.at[me], local_sem)
        mine.start()
        sends = []
        for k in range(1, N_DEV):
            dev, _ = _peer(k)
            cp = pltpu.make_async_remote_copy(src_ref=x_ref, dst_ref=o_ref.at[me], send_sem=send_sems.at[k - 1],
                                              recv_sem=recv_sems.at[k - 1], device_id=dev, device_id_type=MESH_IDS)
            cp.start()
            sends.append(cp)
        for k in range(1, N_DEV):
            dev, idx = _peer(k)
            pltpu.make_async_remote_copy(src_ref=x_ref, dst_ref=o_ref.at[idx], send_sem=send_sems.at[k - 1],
                                         recv_sem=recv_sems.at[k - 1], device_id=dev,
                                         device_id_type=MESH_IDS).wait_recv()
        for cp in sends:
            cp.wait_send()
        mine.wait()

    return pl.pallas_call(
        body, name=name, in_specs=[pl.BlockSpec(memory_space=pl.ANY)], out_specs=pl.BlockSpec(memory_space=pl.ANY),
        out_shape=jax.ShapeDtypeStruct((N_DEV, R, C), x.dtype),
        scratch_shapes=[pltpu.SemaphoreType.DMA((N_DEV - 1,)), pltpu.SemaphoreType.DMA((N_DEV - 1,)),
                        pltpu.SemaphoreType.DMA],
        compiler_params=pltpu.CompilerParams(has_side_effects=True))(x)


HBM_SPEC = pl.BlockSpec(memory_space=pltpu.HBM)
SEM_SPEC = pl.BlockSpec(memory_space=pltpu.SEMAPHORE)
EFFECT = pltpu.SideEffectType.DATAFLOW_SIDE_EFFECTING
OTHER_CHIPS = ((1, 0), (0, 1), (1, 1))


def _split_start(bufs, plan, n, name):
    nb = len(bufs)

    def body(*refs):
        send_sems, recv_sems, token = refs[nb], refs[nb + 1], refs[2 * nb + 2]
        for i, (src, dst, _, dev) in enumerate(plan(refs[:nb])):
            pltpu.make_async_remote_copy(src_ref=src, dst_ref=dst, send_sem=send_sems.at[i],
                                         recv_sem=recv_sems.at[i], device_id=dev, device_id_type=MESH_IDS).start()
        token[...] = jnp.zeros_like(token)

    outs = pl.pallas_call(
        body, name=name,
        out_shape=(pltpu.SemaphoreType.DMA((n,)), pltpu.SemaphoreType.DMA((n,)),
                   *[pltpu.HBM(b.shape, b.dtype) for b in bufs], jax.ShapeDtypeStruct((8, 128), F32)),
        in_specs=[HBM_SPEC] * nb,
        out_specs=(SEM_SPEC, SEM_SPEC, *[HBM_SPEC] * nb, pl.BlockSpec(memory_space=pltpu.VMEM)),
        input_output_aliases={i: 2 + i for i in range(nb)},
        compiler_params=pltpu.CompilerParams(has_side_effects=EFFECT),
    )(*[pltpu.with_memory_space_constraint(b, pltpu.HBM) for b in bufs])
    return dict(send=outs[0], recv=outs[1], bufs=list(outs[2:2 + nb]), token=outs[2 + nb], plan=plan, n=n)


def _split_wait(st, after, name):
    bufs = st["bufs"]
    nb = len(bufs)
    plan = st["plan"]
    afters = list(after) if isinstance(after, (list, tuple)) else [after]

    def body(*refs):
        send_sems, recv_sems = refs[nb], refs[nb + 1]
        for i, (src, dst, land, dev) in enumerate(plan(refs[:nb])):
            pltpu.make_async_remote_copy(src_ref=src, dst_ref=dst, send_sem=send_sems.at[i],
                                         recv_sem=recv_sems.at[i], device_id=dev,
                                         device_id_type=MESH_IDS).wait_send()
            pltpu.make_async_remote_copy(src_ref=src, dst_ref=land, send_sem=send_sems.at[i],
                                         recv_sem=recv_sems.at[i], device_id=dev,
                                         device_id_type=MESH_IDS).wait_recv()

    outs = pl.pallas_call(
        body, name=name, out_shape=tuple(pltpu.HBM(b.shape, b.dtype) for b in bufs),
        in_specs=[HBM_SPEC] * nb + [SEM_SPEC, SEM_SPEC] + [pl.BlockSpec(memory_space=pl.ANY)] * len(afters),
        out_specs=tuple([HBM_SPEC] * nb), input_output_aliases={i: i for i in range(nb)},
        compiler_params=pltpu.CompilerParams(has_side_effects=EFFECT),
    )(*bufs, st["send"], st["recv"], *afters)
    return list(outs)


def _xyc():
    return lax.axis_index("x"), lax.axis_index("y"), lax.axis_index("c")


def _flip(x, y, a, b):
    return (1 - x if a else x), (1 - y if b else y)


def _ag1_plan(outs):
    x, y, c = _xyc()
    me = 4 * x + 2 * y + c
    copies = []
    for o in outs:
        copies.append((o.at[me], o.at[me], o.at[4 * x + 2 * y + 1 - c], (x, y, 1 - c)))
        for a, b in OTHER_CHIPS:
            px, py = _flip(x, y, a, b)
            copies.append((o.at[me], o.at[me], o.at[4 * px + 2 * py + c], (px, py, c)))
    return copies


def _ag2_plan(outs):
    x, y, c = _xyc()
    copies = []
    for o in outs:
        for a, b in OTHER_CHIPS:
            px, py = _flip(x, y, a, b)
            mine, sibs = 4 * px + 2 * py + c, 4 * px + 2 * py + 1 - c
            copies.append((o.at[mine], o.at[mine], o.at[sibs], (x, y, 1 - c)))
    return copies


def _rs1_plan(refs):
    x, y, c = _xyc()
    copies = []
    for g, land in zip(refs[0::2], refs[1::2]):
        for j in range(4):
            copies.append((g.at[2 * j + 1 - c], land.at[j], land.at[j], (x, y, 1 - c)))
    return copies


def _rs2_plan(refs):
    x, y, c = _xyc()
    mychip = 2 * x + y
    copies = []
    for s, land in zip(refs[0::2], refs[1::2]):
        for a, b in OTHER_CHIPS:
            px, py = _flip(x, y, a, b)
            copies.append((s.at[2 * px + py], land.at[mychip], land.at[2 * px + py], (px, py, c)))
    return copies


def _landing_like(g, name):
    def body(g_ref, o_ref):
        del g_ref, o_ref

    anyspec = pl.BlockSpec(memory_space=pl.ANY)
    return pl.pallas_call(body, name=name, in_specs=[anyspec], out_specs=anyspec,
                          out_shape=jax.ShapeDtypeStruct((4,) + g.shape[1:], g.dtype))(g)


def _place(x, slots, by_chip, name):
    R, C = x.shape[-2:]
    rt = _row_tile(R, 512)
    xi, yi, ci = _xyc()
    idx = (2 * xi + yi if by_chip else 4 * xi + 2 * yi + ci).astype(jnp.int32).reshape(1)

    def body(i_ref, x_ref, o_ref):
        o_ref[...] = x_ref[...]

    if by_chip:
        in_spec = pl.BlockSpec((None, rt, C), lambda r, i: (i[0], r, 0))
    else:
        in_spec = pl.BlockSpec((rt, C), lambda r, i: (r, 0))
    grid_spec = pltpu.PrefetchScalarGridSpec(
        num_scalar_prefetch=1, grid=(R // rt,), in_specs=[in_spec],
        out_specs=pl.BlockSpec((None, rt, C), lambda r, i: (i[0], r, 0)))
    return pl.pallas_call(body, name=name, grid_spec=grid_spec,
                          out_shape=jax.ShapeDtypeStruct((slots, R, C), x.dtype),
                          compiler_params=_cp(("parallel",)))(idx, x)


def _pair_sum(g, land, cidx, name):
    _, R, C = land.shape
    rt = _row_tile(R, 512)
    g4 = g.reshape(4, 2, R, C)

    def body(c_ref, g_ref, l_ref, o_ref):
        o_ref[...] = (g_ref[...].astype(F32) + l_ref[...].astype(F32)).astype(o_ref.dtype)

    grid_spec = pltpu.PrefetchScalarGridSpec(
        num_scalar_prefetch=1, grid=(4, R // rt),
        in_specs=[pl.BlockSpec((None, None, rt, C), lambda j, r, c_ref: (j, c_ref[0], r, 0)),
                  pl.BlockSpec((None, rt, C), lambda j, r, c_ref: (j, r, 0))],
        out_specs=pl.BlockSpec((None, rt, C), lambda j, r, c_ref: (j, r, 0)))
    return pl.pallas_call(body, name=name, grid_spec=grid_spec, out_shape=jax.ShapeDtypeStruct(land.shape, land.dtype),
                          compiler_params=_cp(("parallel", "parallel")))(cidx, g4, land)


def _no_hook(event, l, after, payload=None):
    return None


def _tie(x, token):
    if token is None:
        return x

    def body(x_ref, t_ref, o_ref):
        del x_ref, t_ref, o_ref

    anyspec = pl.BlockSpec(memory_space=pl.ANY)
    return pl.pallas_call(body, name="tie", in_specs=[anyspec, anyspec], out_specs=anyspec,
                          out_shape=jax.ShapeDtypeStruct(x.shape, x.dtype), input_output_aliases={0: 0})(x, token)


def _layer_fwd(x, W, l, hook=_no_hook):
    T = x.shape[0]
    n = f"l{l}_"
    h1, h1t = _norm_fwd(x, W["norm1"], n + "norm1_fwd")
    proj = _mm_nn(h1, W["win"], F32, n + "mm_in", tn_c=(1792,))
    y_pool = _pool_fwd(proj, W["pool_w"], W["pool_b"], _tie(W["pool_s"], hook("f_in", l, proj)))
    cpre = _conv_fwd(proj, C_QKV, 3 * GDN_W, W["gconv_w"], None, 256, n + "gdn_conv_fwd")
    qkv, bb, gb = _gdn_pre_fwd(cpre, proj, W["alog"], W["dtb"])
    mixed, states, tms = _gdn_fwd(qkv, bb, gb, proj, W["gnorm"], y_pool)
    lconv_w = _tie(W["lconv_w"], hook("f_mix", l, states))
    xc = _conv_fwd(proj, C_XR, LRU_W, lconv_w, W["lconv_b"], 128, n + "lru_conv_fwd")
    mixed, hst = _lru_fwd(xc, proj, W["wa"], W["ba"], W["wx"], W["bx"], W["lam"], mixed)
    mixed = _tie(mixed, hook("f_out", l, hst))
    x1 = _mm_nn(mixed, W["wout"], F32, n + "mm_out", add=x)
    h2, h2t = _norm_fwd(x1, W["norm2"], n + "norm2_fwd")
    h2 = _tie(h2, hook("f_n2", l, h2t))
    up = _mm_up(h2, W["wup"], n + "mm_up")
    act, act_t = _ffn_gate_fwd(up, _tie(W["fconv_w"], hook("f_up", l, up)))
    act = _tie(act, hook("f_act", l, act_t))
    x2 = _mm_nn(act, W["wdown"], F32, n + "mm_down", add=x1, tk_c=(3072, 2048, 1536, 1024, 512, 256))
    hook("f_end", l, x2)
    saved = dict(x=x, h1t=h1t, proj=proj, cpre=cpre, qkv=qkv, bb=bb, gb=gb, states=states, tms=tms, xc=xc, hst=hst,
                 mixed=mixed, x1=x1, h2t=h2t, up=up, act_t=act_t)
    return x2, saved


def _layer_bwd(dx2, dx2b, W, S, l, hook=_no_hook):
    T = dx2.shape[0]
    n = f"l{l}_"
    dact = _mm_nt(dx2b, W["wdown"], BF16, n + "mm_down_dx", tk_c=(2048,), tn_c=(1536,))
    g_wdown = _mm_nn(S["act_t"], dx2b, BF16, n + "mm_down_dw", tn_c=(2048,))
    dup, g_fconv = _ffn_gate_bwd(dact, S["up"], W["fconv_w"])
    ns = W["wup"].shape[2]
    dh2 = _mm_up_t(dup, W["wup"], n + "mm_up_dx")
    g_wup = _mm_dup(S["h2t"], dup, ns, n + "mm_up_dw")
    tok = hook("b_ffn", l, g_wup, dict(ffn_down=g_wdown, ffn_up=g_wup))
    dx1, dx1b, g_norm2 = _norm_bwd(S["x1"], _tie(W["norm2"], tok), dh2, dx2, n + "norm2_bwd")
    dmixed = _mm_nt(dx1b, W["wout"], F32, n + "mm_out_dx", tk_c=(2048,), tn_c=(2048,))
    g_wout = _mm_tn(S["mixed"], dx1b, BF16, n + "mm_out_dw", tn_c=(2048,))
    tok = hook("b_mid", l, g_wout)
    proj = S["proj"]
    dproj, g_pool_w, g_pool_b, g_pool_s = _pool_bwd(dmixed, proj, W["pool_w"], W["pool_b"], _tie(W["pool_s"], tok))
    dq, dk, dv, dbb, dgb, dproj, g_gnorm = _gdn_bwd(S["qkv"], S["bb"], S["gb"], proj, W["gnorm"], S["states"],
                                                    S["tms"], dmixed, dproj)
    dc, dproj, g_alog, g_dtb = _gdn_pre_bwd(dq, dk, dv, S["cpre"], dbb, dgb, proj, W["alog"], W["dtb"], dproj)
    dproj, g_gconv = _conv_bwd(dc, proj, C_QKV, W["gconv_w"], 256, n + "gdn_conv_bwd", dproj)
    dxc, dproj, g_wa, g_wx, g_ba, g_bx, g_lam = _lru_bwd(dmixed, S["xc"], proj, S["hst"], W["wa"], W["ba"], W["wx"],
                                                          W["bx"], W["lam"], dproj)
    dproj, g_lconv, g_lconv_b = _conv_bwd(dxc, proj, C_XR, W["lconv_w"], 128, n + "lru_conv_bwd", dproj,
                                          want_db=True)
    dh1 = _mm_nt(dproj, W["win"], F32, n + "mm_in_dx", tk_c=(1792,))
    g_win = _mm_nn(S["h1t"], dproj, BF16, n + "mm_in_dw", tn_c=(1792,))
    tok = hook("b_in", l, g_win, dict(w_out=g_wout, w_in=g_win))
    dx, dxb, g_norm1 = _norm_bwd(S["x"], _tie(W["norm1"], tok), dh1, dx1, n + "norm1_bwd")
    big = dict(w_in=g_win, w_out=g_wout, ffn_up=g_wup, ffn_down=g_wdown)
    small = dict(norm1_w=g_norm1[0], pool_w=g_pool_w, pool_b=g_pool_b.reshape(4, 128), pool_scale=g_pool_s[0],
                 gdn_conv_w=g_gconv, gdn_a_log=g_alog[0, :HEADS], gdn_dt_bias=g_dtb[0, :HEADS],
                 gdn_norm_w=g_gnorm[0], lru_conv_w=g_lconv, lru_conv_b=g_lconv_b[0], lru_wa=g_wa, lru_ba=g_ba[0],
                 lru_wx=g_wx, lru_bx=g_bx[0], lru_lambda=g_lam[0], norm2_w=g_norm2[0], ffn_conv_w=g_fconv)
    dxb = _tie(dxb, hook("b_end", l, dx, small))
    return dx, dxb, big, small


def _pad_lane(v):
    return jnp.pad(v, (0, 128 - v.shape[0])).reshape(1, 128)


def _layer_weights(l, big, P, conv_full):
    return dict(
        win=big.get("w_in"), wout=big.get("w_out"), wup=big.get("ffn_up"), wdown=big.get("ffn_down"),
        norm1=P["norm1_w"][l].reshape(1, D_MODEL), norm2=P["norm2_w"][l].reshape(1, D_MODEL),
        pool_w=P["pool_w"][l], pool_b=P["pool_b"][l].reshape(1, POOL_W), pool_s=P["pool_scale"][l].reshape(1, POOL_W),
        gconv_w=conv_full["gdn_conv_w"][l], alog=_pad_lane(P["gdn_a_log"][l]), dtb=_pad_lane(P["gdn_dt_bias"][l]),
        gnorm=P["gdn_norm_w"][l].reshape(1, HD),
        lconv_w=conv_full["lru_conv_w"][l], lconv_b=P["lru_conv_b"][l].reshape(1, LRU_W),
        wa=P["lru_wa"][l], ba=P["lru_ba"][l].reshape(1, LRU_W), wx=P["lru_wx"][l],
        bx=P["lru_bx"][l].reshape(1, LRU_W), lam=P["lru_lambda"][l].reshape(1, LRU_W),
        fconv_w=conv_full["ffn_conv_w"][l])


def _local_step(x, target, Ws, final_norm_w, hook=_no_hook):
    saved = []
    for l in range(DEPTH):
        x, s = _layer_fwd(x, Ws[l], l, hook)
        saved.append(s)
    loss, dx, dxb, g_final = _loss_head(x, final_norm_w.reshape(1, D_MODEL), target)
    bigs, smalls = [None] * DEPTH, [None] * DEPTH
    for l in reversed(range(DEPTH)):
        dx, dxb, bigs[l], smalls[l] = _layer_bwd(dx, dxb, Ws[l], saved[l], l, hook)
    return loss, dx, g_final[0], bigs, smalls


SMALL_REPL = ("norm1_w", "pool_w", "pool_b", "pool_scale", "gdn_a_log", "gdn_dt_bias", "gdn_norm_w", "lru_conv_b",
              "lru_wa", "lru_ba", "lru_wx", "lru_bx", "lru_lambda", "norm2_w", "final_norm_w")
SMALL_SHARD = ("gdn_conv_w", "lru_conv_w", "ffn_conv_w")
BIG = ("w_in", "w_out", "ffn_up", "ffn_down")
WEIGHTS = ("norm1_w", "w_in", "pool_w", "pool_b", "pool_scale", "gdn_conv_w", "gdn_a_log", "gdn_dt_bias",
           "gdn_norm_w", "lru_conv_w", "lru_conv_b", "lru_wa", "lru_ba", "lru_wx", "lru_bx", "lru_lambda", "w_out",
           "norm2_w", "ffn_up", "ffn_conv_w", "ffn_down", "final_norm_w")
SEG = 1024
PACK_ROWS_MULT = 256 * 128


def _pack(arrs):
    pieces, table, off = [], [], 0
    for a in arrs:
        n = a.size
        npad = -(-n // SEG) * SEG
        pieces.append(jnp.pad(a.reshape(-1).astype(F32), (0, npad - n)))
        table.append((off, n, a.shape))
        off += npad
    tail = -off % PACK_ROWS_MULT
    if tail:
        pieces.append(jnp.zeros((tail,), F32))
        off += tail
    return jnp.concatenate(pieces).reshape(off // 128, 128), table


def _unpack(buf, table):
    flat = buf.reshape(-1)
    return [flat[off:off + n].reshape(shape) for off, n, shape in table]


def _pad_in(w):
    z1 = jnp.zeros(w.shape[:-1] + (C_XR - AB_ORIG_END,), w.dtype)
    return jnp.concatenate([w[..., :AB_ORIG_END], z1, w[..., AB_ORIG_END:]], axis=-1)


def _unpad_in(w):
    return jnp.concatenate([w[..., :AB_ORIG_END], w[..., C_XR:C_GR + LRU_W]], axis=-1)


def kernel(x, norm1_w, w_in, pool_w, pool_b, pool_scale, gdn_conv_w, gdn_a_log, gdn_dt_bias, gdn_norm_w, lru_conv_w, lru_conv_b, lru_wa, lru_ba, lru_wx, lru_bx, lru_lambda, w_out, norm2_w, ffn_up, ffn_conv_w, ffn_down, final_norm_w, loss_target, m_norm1_w, m_w_in, m_pool_w, m_pool_b, m_pool_scale, m_gdn_conv_w, m_gdn_a_log, m_gdn_dt_bias, m_gdn_norm_w, m_lru_conv_w, m_lru_conv_b, m_lru_wa, m_lru_ba, m_lru_wx, m_lru_bx, m_lru_lambda, m_w_out, m_norm2_w, m_ffn_up, m_ffn_conv_w, m_ffn_down, m_final_norm_w, v_norm1_w, v_w_in, v_pool_w, v_pool_b, v_pool_scale, v_gdn_conv_w, v_gdn_a_log, v_gdn_dt_bias, v_gdn_norm_w, v_lru_conv_w, v_lru_conv_b, v_lru_wa, v_lru_ba, v_lru_wx, v_lru_bx, v_lru_lambda, v_w_out, v_norm2_w, v_ffn_up, v_ffn_conv_w, v_ffn_down, v_final_norm_w):
    loc = dict(locals())
    Wp = {n: loc[n] for n in WEIGHTS}
    Mp = {n: loc["m_" + n] for n in WEIGHTS}
    Vp = {n: loc["v_" + n] for n in WEIGHTS}
    xi, yi, ci = _xyc()
    me = 4 * xi + 2 * yi + ci
    mychip = 2 * xi + yi
    cidx = ci.astype(jnp.int32).reshape(1)
    keys = dict(w_in="win", w_out="wout", ffn_up="wup", ffn_down="wdown")

    def shard2d(d, name, l):
        a = d[name][l]
        return _pad_in(a) if name == "w_in" else a

    def wshard(l, name):
        return shard2d(Wp, name, l).astype(BF16)

    def full2d(name, full):
        return full if name == "ffn_up" else full.reshape(-1, full.shape[2])

    def ag_start(shards, tag, token=None):
        if token is not None:
            shards = [_tie(shards[0], token)] + list(shards[1:])
        bufs = [_place(s, N_DEV, False, f"place_{tag}{i}") for i, s in enumerate(shards)]
        return _split_start(bufs, _ag1_plan, 4 * len(bufs), f"ag1s_{tag}")

    def ag_mid(st, after, tag):
        bufs = _split_wait(st, after, f"ag1w_{tag}")
        return _split_start(bufs, _ag2_plan, 3 * len(bufs), f"ag2s_{tag}")

    def ag_end(st, after, tag):
        return _split_wait(st, after, f"ag2w_{tag}")

    def rs_start(gs, tag):
        bufs = []
        for nm, g in gs.items():
            if nm != "ffn_up":
                g = g.reshape(N_DEV, g.shape[0] // N_DEV, g.shape[1])
            bufs += [g, _landing_like(g, f"land_{nm}_{tag}")]
        st = _split_start(bufs, _rs1_plan, 4 * len(gs), f"rs1s_{tag}")
        st["names"] = list(gs)
        return st

    def rs_mid(st, after, tag):
        bufs = _split_wait(st, after, f"rs1w_{tag}")
        out = []
        for i, nm in enumerate(st["names"]):
            s = _pair_sum(bufs[2 * i], bufs[2 * i + 1], cidx, f"pairsum_{nm}_{tag}")
            out += [s, _place(s, 4, True, f"place_{nm}_{tag}")]
        st2 = _split_start(out, _rs2_plan, 3 * len(st["names"]), f"rs2s_{tag}")
        st2["names"] = st["names"]
        return st2

    def rs_end(st, after, tag):
        bufs = _split_wait(st, after, f"rs2w_{tag}")
        return dict(zip(st["names"], bufs[1::2]))

    lnames = tuple(n for n in SMALL_REPL if n != "final_norm_w") + SMALL_SHARD

    def small_pack(l, gs, extra):
        return _pack([gs[nm] for nm in lnames] + extra)

    def small_state(d, l, gs):
        arrs = [d[nm][l] if nm in SMALL_REPL else jnp.zeros(gs[nm].shape, F32) for nm in lnames]
        if l == 0:
            arrs += [d["final_norm_w"], jnp.zeros((1,), F32)]
        return _pack(arrs)[0]

    def stacked(d, name):
        return _pad_in(d[name]) if name == "w_in" else d[name]

    wmv = {name: [stacked(d, name) for d in (Wp, Mp, Vp)] for name in BIG}
    groups = dict(b1=[(0, "w_out")], b2=[(0, "ffn_up")], b3=[(0, "ffn_down")],
                  c1=[(1, "w_in"), (1, "w_out")], c2=[(1, "ffn_up")], c3=[(1, "ffn_down")])
    gshards = {g: [wshard(l, n) for l, n in members] for g, members in groups.items()}
    fwd_plan = {("f_in", 0): [("mid", "b1"), ("start", "b2")],
                ("f_out", 0): [("end", "b1"), ("mid", "b2"), ("start", "b3")],
                ("f_n2", 0): [("end", "b2"), ("start", "c1")],
                ("f_up", 0): [("mid", "b3")],
                ("f_act", 0): [("end", "b3"), ("mid", "c1"), ("start", "c2")],
                ("f_end", 0): [("end", "c1")],
                ("f_mix", 1): [("mid", "c2"), ("start", "c3")],
                ("f_out", 1): [("end", "c2")],
                ("f_n2", 1): [("mid", "c3")],
                ("f_act", 1): [("end", "c3")]}
    later = [s for g in gshards.values() for s in g]
    stA = ag_start([wshard(0, "w_in")], "a")
    stA2 = ag_mid(stA, [stA["token"]] + later + wmv["w_in"], "a")
    gst = {"b1": ag_start(gshards["b1"], "b1", stA2["token"])}
    (w_in0,) = ag_end(stA2, gst["b1"]["token"], "a")

    cbuf, ctable = _pack([Wp[n] for n in SMALL_SHARD])
    call = _all_gather(cbuf, "ag_conv_w")
    parts = [_unpack(call[i], ctable) for i in range(N_DEV)]
    conv_full = {n: jnp.concatenate([parts[i][j] for i in range(N_DEV)], axis=-1) for j, n in enumerate(SMALL_SHARD)}

    Ws = [_layer_weights(l, {}, Wp, conv_full) for l in range(DEPTH)]
    Ws[0]["win"] = full2d("w_in", w_in0)
    st = {}

    def fwd_actions(actions, after):
        token = None
        for what, g in actions:
            dep = after if token is None else token
            if what == "start":
                gst[g] = ag_start(gshards[g], g, dep)
                token = gst[g]["token"]
            elif what == "mid":
                gst[g] = ag_mid(gst[g], dep, g)
                token = gst[g]["token"]
            else:
                bufs = ag_end(gst[g], dep, g)
                for (gl, n), b in zip(groups[g], bufs):
                    Ws[gl][keys[n]] = full2d(n, b)
                token = bufs[0]
        return token

    def hook(event, l, after, payload=None):
        if (event, l) in fwd_plan:
            return fwd_actions(fwd_plan[event, l], after)
        if event == "b_ffn":
            st["ffn", l] = rs_start(payload, f"ffn{l}")
            return st["ffn", l]["token"]
        if event == "b_mid":
            st["ffn2", l] = rs_mid(st["ffn", l], after, f"ffn{l}")
            if l == 0:
                st["sm1b"] = ag_mid(st["sm1"], st["ffn2", l]["token"], "sm1")
                return st["sm1b"]["token"]
            return st["ffn2", l]["token"]
        if event == "b_in":
            st["io", l] = rs_start(payload, f"io{l}")
            if l == 0:
                st["sm1g"] = ag_end(st["sm1b"], st["io", l]["token"], "sm1")[0]
            return st["io", l]["token"]
        if event == "b_end" and l == 1:
            st["io2", 1] = rs_mid(st["io", 1], after, "io1")
            gbuf1, st["table1"] = small_pack(1, payload, [])
            st["sm1"] = ag_start([gbuf1], "sm1", st["io2", 1]["token"])
            return st["sm1"]["token"]
        return None

    loss, dx, g_final, _, gsmall = _local_step(x[0], loss_target[0], Ws, final_norm_w, hook)

    out_g, out_d, out_m, out_v = {}, {}, {}, {}
    outs4 = (out_g, out_d, out_m, out_v)
    rts = dict(w_in=64, w_out=128, ffn_up=256, ffn_down=128)
    big_res = {}

    def adam_big(l, parts):
        for name, p in parts.items():
            big_res[name] = _adamw(p, *wmv[name], rts[name], f"adamw_{name}_{l}", layer=l, prev=big_res.get(name))
        return [big_res[name][0] for name in parts]

    def adam_small(l, gall, gs):
        rs = gall.shape[1]
        return _adamw(gall, small_state(Wp, l, gs), small_state(Mp, l, gs), small_state(Vp, l, gs),
                      _row_tile(rs, 512), f"adamw_small_{l}")

    gbuf0, table0 = small_pack(0, gsmall[0], [g_final, loss[0, :1]])
    sm0 = ag_start([gbuf0], "sm0", st["io", 0]["token"])
    o = adam_big(1, rs_end(st["ffn2", 1], sm0["token"], "ffn1"))
    st["io2", 0] = rs_mid(st["io", 0], o, "io0")
    o = adam_big(1, rs_end(st["io2", 1], st["io2", 0]["token"], "io1"))
    o = adam_big(0, rs_end(st["ffn2", 0], o, "ffn0"))
    small_res = {1: adam_small(1, _tie(st["sm1g"], o[-1]), gsmall[1])}
    sm0b = ag_mid(sm0, o + [small_res[1][0]], "sm0")
    small_res[0] = adam_small(0, ag_end(sm0b, sm0b["token"], "sm0")[0], gsmall[0])
    adam_big(0, rs_end(st["io2", 0], small_res[0][0], "io0"))

    for name in BIG:
        for i, dst in enumerate(outs4):
            dst[name] = _unpad_in(big_res[name][i]) if name == "w_in" else big_res[name][i]

    unp = {0: [_unpack(r, table0) for r in small_res[0]], 1: [_unpack(r, st["table1"]) for r in small_res[1]]}
    for j, nm in enumerate(lnames):
        if nm in SMALL_REPL:
            for i, dst in enumerate(outs4):
                dst[nm] = jnp.stack([unp[l][i][j] for l in range(DEPTH)])
    for i, dst in enumerate(outs4):
        dst["final_norm_w"] = unp[0][i][len(lnames)]
    loss_total = unp[0][0][len(lnames) + 1][0]

    gsh = []
    for nm in SMALL_SHARD:
        j = lnames.index(nm)
        width = Wp[nm].shape[-1]
        gsh.append(jnp.stack([lax.dynamic_slice_in_dim(unp[l][0][j], me * width, width, axis=1)
                              for l in range(DEPTH)]))
    sbuf, stable = _pack(gsh)
    res = _adamw(sbuf[None], _pack([Wp[n] for n in SMALL_SHARD])[0], _pack([Mp[n] for n in SMALL_SHARD])[0],
                 _pack([Vp[n] for n in SMALL_SHARD])[0], sbuf.shape[0], "adamw_conv_w")
    unp2 = [_unpack(r, stable) for r in res]
    for j, nm in enumerate(SMALL_SHARD):
        for i, dst in enumerate((out_g, out_d, out_m, out_v)):
            dst[nm] = unp2[i][j]

    return (loss_total, dx[None], *[out_g[n] for n in WEIGHTS], *[out_d[n] for n in WEIGHTS],
            *[out_m[n] for n in WEIGHTS], *[out_v[n] for n in WEIGHTS])
```

```python
import functools

import jax
import jax.numpy as jnp
from jax import lax
from jax.experimental import pallas as pl
from jax.experimental.pallas import tpu as pltpu

F32 = jnp.float32
BF16 = jnp.bfloat16
HI = lax.Precision.HIGHEST
MESH_IDS = pl.DeviceIdType.MESH

N_DEV = 8
D_MODEL = 2048
DEPTH = 2
POOL_WINDOWS = (2, 4, 8, 16)
POOL_W = 512
HEADS = 6
HD = 128
GDN_W = HEADS * HD
CHUNK = 64
LRU_W = 768
LRU_C = 8.0
D_FF = 3 * D_MODEL
EPS = 1e-6
IN_COLS = 5132
PCOLS = 5376
C_QKV, C_Z, C_AB, C_XR, C_GR = 512, 2816, 3584, 3840, 4608
AB_ORIG_END = 3596
M_GDN, M_LRU = 512, 1280

ADAM_LR, ADAM_B1, ADAM_B2, ADAM_EPS, ADAM_WD, ADAM_STEP = 0.001, 0.9, 0.999, 1e-08, 0.01, 10

VMEM_LIMIT = 56 * 1024 * 1024


def _cp(sem):
    return pltpu.CompilerParams(dimension_semantics=sem, vmem_limit_bytes=VMEM_LIMIT)


def _mm(a, b, ca=1, cb=0, prec=None, cast=True):
    if cast:
        a = a.astype(BF16)
        b = b.astype(BF16)
    return lax.dot_general(a, b, (((ca,), (cb,)), ((), ())), preferred_element_type=F32, precision=prec)


def _bmm(a, b, ca=2, cb=1, prec=None, cast=True):
    if cast:
        a = a.astype(BF16)
        b = b.astype(BF16)
    return lax.dot_general(a, b, (((ca,), (cb,)), ((0,), (0,))), preferred_element_type=F32, precision=prec)


def _sigmoid(x):
    return 1.0 / (1.0 + jnp.exp(-x))


def _log1p(e):
    u = 1.0 + e
    return jnp.where(u == 1.0, e, jnp.log(u) * e / jnp.where(u == 1.0, 1.0, u - 1.0))


def _softplus(x):
    return jnp.maximum(x, 0.0) + _log1p(jnp.exp(-jnp.abs(x)))


def _expm1(x):
    u = jnp.exp(x)
    um = u - 1.0
    safe = jnp.where((u == 1.0) | (um == -1.0), 1.0, jnp.log(u))
    return jnp.where(u == 1.0, x, jnp.where(um == -1.0, -1.0, um * x / safe))


_G0 = 0.7978845608028654
_G1 = 0.044715


def _gelu(x):
    return 0.5 * x * (1.0 + jnp.tanh(_G0 * (x + _G1 * x * x * x)))


def _gelu_and_grad(x):
    th = jnp.tanh(_G0 * (x + _G1 * x * x * x))
    g = 0.5 * x * (1.0 + th)
    dg = 0.5 * (1.0 + th) + 0.5 * x * (1.0 - th * th) * _G0 * (1.0 + 3.0 * _G1 * x * x)
    return g, dg


def _tile(T):
    return min(T, 512)


def _matmul(a, b, *, grid, a_spec, b_spec, out_shape, out_spec, dims, acc_shape, name, add=None, add_spec=None):
    nk = grid[2]
    has_add = add is not None

    def body(*refs):
        if has_add:
            a_ref, b_ref, add_ref, o_ref, acc_ref = refs
        else:
            a_ref, b_ref, o_ref, acc_ref = refs
            add_ref = None
        k = pl.program_id(2)
        p = lax.dot_general(a_ref[...].astype(BF16), b_ref[...].astype(BF16), (dims, ((), ())),
                            preferred_element_type=F32)

        def finish(r):
            if has_add:
                r = r + add_ref[...]
            o_ref[...] = r.astype(o_ref.dtype)

        if nk == 1:
            finish(p)
        else:
            @pl.when(k == 0)
            def _():
                acc_ref[...] = p

            @pl.when(k > 0)
            def _():
                acc_ref[...] += p

            @pl.when(k == nk - 1)
            def _():
                finish(acc_ref[...])

    in_specs = [a_spec, b_spec] + ([add_spec] if has_add else [])
    args = (a, b) + ((add,) if has_add else ())
    return pl.pallas_call(
        body, name=name, grid=grid, in_specs=in_specs, out_specs=out_spec, out_shape=out_shape,
        scratch_shapes=[pltpu.VMEM(acc_shape if nk > 1 else (8, 128), F32)],
        compiler_params=_cp(("parallel", "parallel", "arbitrary")),
    )(*args)


def _pick(n, cands):
    for c in cands:
        if n % c == 0:
            return c
    raise ValueError(f"no tile for {n}")


def _mm_nn(a, b, out_dtype, name, add=None, tn_c=(1024, 768, 512), tk_c=(2048, 1536, 1024, 512, 256)):
    M, K = a.shape
    N = b.shape[1]
    tm = _pick(M, (1024, 512, 256))
    tn = _pick(N, tn_c)
    tk = _pick(K, tk_c)
    return _matmul(
        a, b, grid=(M // tm, N // tn, K // tk),
        a_spec=pl.BlockSpec((tm, tk), lambda i, j, k: (i, k)),
        b_spec=pl.BlockSpec((tk, tn), lambda i, j, k: (k, j)),
        out_shape=jax.ShapeDtypeStruct((M, N), out_dtype),
        out_spec=pl.BlockSpec((tm, tn), lambda i, j, k: (i, j)),
        dims=((1,), (0,)), acc_shape=(tm, tn), name=name, add=add,
        add_spec=pl.BlockSpec((tm, tn), lambda i, j, k: (i, j)))


def _mm_nt(a, b, out_dtype, name, tk_c=(2048, 1536, 1024, 768, 512), tn_c=(1024, 768, 512)):
    M, K = a.shape
    N = b.shape[0]
    tm = _pick(M, (1024, 512, 256))
    tn = _pick(N, tn_c)
    tk = _pick(K, tk_c)
    return _matmul(
        a, b, grid=(M // tm, N // tn, K // tk),
        a_spec=pl.BlockSpec((tm, tk), lambda i, j, k: (i, k)),
        b_spec=pl.BlockSpec((tn, tk), lambda i, j, k: (j, k)),
        out_shape=jax.ShapeDtypeStruct((M, N), out_dtype),
        out_spec=pl.BlockSpec((tm, tn), lambda i, j, k: (i, j)),
        dims=((1,), (1,)), acc_shape=(tm, tn), name=name)


def _mm_tn(a, b, out_dtype, name, tn_c=(1024, 768, 512)):
    K, M = a.shape
    N = b.shape[1]
    tm = _pick(M, (1024, 768, 512))
    tn = _pick(N, tn_c)
    tk = _pick(K, (1024, 512, 256))
    return _matmul(
        a, b, grid=(M // tm, N // tn, K // tk),
        a_spec=pl.BlockSpec((tk, tm), lambda i, j, k: (k, i)),
        b_spec=pl.BlockSpec((tk, tn), lambda i, j, k: (k, j)),
        out_shape=jax.ShapeDtypeStruct((M, N), out_dtype),
        out_spec=pl.BlockSpec((tm, tn), lambda i, j, k: (i, j)),
        dims=((0,), (0,)), acc_shape=(tm, tn), name=name)


def _mm_up(h, wup, name):
    M, K = h.shape
    ns = wup.shape[2]
    tm = _pick(M, (1024, 512, 256))
    tn = ns
    per = ns // tn
    return _matmul(
        h, wup, grid=(M // tm, N_DEV * per, 1),
        a_spec=pl.BlockSpec((tm, K), lambda i, j, k: (i, 0)),
        b_spec=pl.BlockSpec((None, K, tn), lambda i, j, k: (j // per, 0, j % per)),
        out_shape=jax.ShapeDtypeStruct((M, N_DEV * ns), BF16),
        out_spec=pl.BlockSpec((tm, tn), lambda i, j, k: (i, j)),
        dims=((1,), (0,)), acc_shape=(tm, tn), name=name)


def _mm_up_t(dup, wup, name):
    M = dup.shape[1]
    D, ns = wup.shape[1], wup.shape[2]
    tm = _pick(M, (1024, 512, 256))
    tn = D
    tk = ns
    return _matmul(
        dup, wup, grid=(M // tm, D // tn, N_DEV),
        a_spec=pl.BlockSpec((None, tm, tk), lambda i, j, k: (k // 4, i, k % 4)),
        b_spec=pl.BlockSpec((None, tn, tk), lambda i, j, k: (k, j, 0)),
        out_shape=jax.ShapeDtypeStruct((M, D), F32),
        out_spec=pl.BlockSpec((tm, tn), lambda i, j, k: (i, j)),
        dims=((1,), (1,)), acc_shape=(tm, tn), name=name)


def _mm_dup(ht, dup, ns, name):
    M, K = ht.shape
    tm = 1024
    tn = ns
    per = ns // tn
    half = 4 * per
    tk = _pick(K, (2048, 1024, 512, 256))
    return _matmul(
        ht, dup, grid=(M // tm, N_DEV * per, K // tk),
        a_spec=pl.BlockSpec((tm, tk), lambda i, j, k: (i, k)),
        b_spec=pl.BlockSpec((None, tk, tn), lambda i, j, k: (j // half, k, j % half)),
        out_shape=jax.ShapeDtypeStruct((N_DEV, M, ns), BF16),
        out_spec=pl.BlockSpec((None, tm, tn), lambda i, j, k: (j // per, i, j % per)),
        dims=((1,), (0,)), acc_shape=(tm, tn), name=name)


def _norm_fwd(x, w, name):
    T, D = x.shape
    tt = _tile(T)

    def body(x_ref, w_ref, h_ref, ht_ref):
        xv = x_ref[...]
        r = lax.rsqrt(jnp.mean(xv * xv, axis=1, keepdims=True) + EPS)
        hv = xv * r * w_ref[...]
        h_ref[...] = hv.astype(BF16)
        ht_ref[...] = hv.T.astype(BF16)

    return pl.pallas_call(
        body, name=name, grid=(T // tt,),
        in_specs=[pl.BlockSpec((tt, D), lambda t: (t, 0)), pl.BlockSpec((1, D), lambda t: (0, 0))],
        out_specs=[pl.BlockSpec((tt, D), lambda t: (t, 0)), pl.BlockSpec((D, tt), lambda t: (0, t))],
        out_shape=[jax.ShapeDtypeStruct((T, D), BF16), jax.ShapeDtypeStruct((D, T), BF16)],
        compiler_params=_cp(("parallel",)))(x, w)


def _norm_bwd(x, w, dh, dres, name):
    T, D = x.shape
    tt = _tile(T)

    def body(x_ref, w_ref, dh_ref, dres_ref, dx_ref, dxb_ref, dw_ref):
        t = pl.program_id(0)
        xv = x_ref[...]
        r = lax.rsqrt(jnp.mean(xv * xv, axis=1, keepdims=True) + EPS)
        xh = xv * r
        dh_v = dh_ref[...]
        dxh = dh_v * w_ref[...]
        dxv = dres_ref[...] + r * (dxh - xh * jnp.mean(dxh * xh, axis=1, keepdims=True))
        dx_ref[...] = dxv
        dxb_ref[...] = dxv.astype(BF16)
        part = jnp.sum(dh_v * xh, axis=0, keepdims=True)

        @pl.when(t == 0)
        def _():
            dw_ref[...] = part

        @pl.when(t > 0)
        def _():
            dw_ref[...] += part

    row = pl.BlockSpec((tt, D), lambda t: (t, 0))
    vec = pl.BlockSpec((1, D), lambda t: (0, 0))
    return pl.pallas_call(
        body, name=name, grid=(T // tt,), in_specs=[row, vec, row, row], out_specs=[row, row, vec],
        out_shape=[jax.ShapeDtypeStruct((T, D), F32), jax.ShapeDtypeStruct((T, D), BF16),
                   jax.ShapeDtypeStruct((1, D), F32)],
        compiler_params=_cp(("arbitrary",)))(x, w, dh, dres)


def _loss_head(x, w, target):
    T, D = x.shape
    tt = _tile(T)

    def body(x_ref, w_ref, t_ref, loss_ref, dx_ref, dxb_ref, dw_ref):
        t = pl.program_id(0)
        xv = x_ref[...]
        r = lax.rsqrt(jnp.mean(xv * xv, axis=1, keepdims=True) + EPS)
        xh = xv * r
        err = xh * w_ref[...] - t_ref[...]
        lp = 0.5 * jnp.sum(jnp.mean(err * err, axis=1, keepdims=True), axis=0, keepdims=True)
        dy = err * (1.0 / D)
        dxh = dy * w_ref[...]
        dxv = r * (dxh - xh * jnp.mean(dxh * xh, axis=1, keepdims=True))
        dx_ref[...] = dxv
        dxb_ref[...] = dxv.astype(BF16)
        part = jnp.sum(dy * xh, axis=0, keepdims=True)
        lpb = jnp.broadcast_to(lp, (1, 128))

        @pl.when(t == 0)
        def _():
            dw_ref[...] = part
            loss_ref[...] = lpb

        @pl.when(t > 0)
        def _():
            dw_ref[...] += part
            loss_ref[...] += lpb

    row = pl.BlockSpec((tt, D), lambda t: (t, 0))
    vec = pl.BlockSpec((1, D), lambda t: (0, 0))
    return pl.pallas_call(
        body, name="loss_head", grid=(T // tt,), in_specs=[row, vec, row],
        out_specs=[pl.BlockSpec((1, 128), lambda t: (0, 0)), row, row, vec],
        out_shape=[jax.ShapeDtypeStruct((1, 128), F32), jax.ShapeDtypeStruct((T, D), F32),
                   jax.ShapeDtypeStruct((T, D), BF16), jax.ShapeDtypeStruct((1, D), F32)],
        compiler_params=_cp(("arbitrary",)))(x, w, target)


CONV_TT = 4096
CONV_BWD_TT = 2048


def _conv_fwd(x, col0, C, w, b, cb, name):
    T = x.shape[0]
    K = w.shape[0]
    tt = min(T, CONV_TT)
    nt, nc, c0 = T // tt, C // cb, col0 // cb
    has_b = b is not None

    def body(*refs):
        if has_b:
            x_ref, halo_ref, w_ref, b_ref, y_ref = refs
        else:
            x_ref, halo_ref, w_ref, y_ref = refs
        t = pl.program_id(1)
        halo = jnp.where(t == 0, 0.0, halo_ref[...])
        xe = jnp.concatenate([halo, x_ref[...]], axis=0)
        acc = xe * w_ref[K - 1:K, :]
        for j in range(K - 1):
            acc = acc + pltpu.roll(xe, K - 1 - j, 0) * w_ref[j:j + 1, :]
        if has_b:
            acc = acc + b_ref[...]
        y_ref[...] = acc[8:, :]

    in_specs = [pl.BlockSpec((tt, cb), lambda j, t: (t, c0 + j)),
                pl.BlockSpec((8, cb), lambda j, t: (jnp.maximum(t * (tt // 8) - 1, 0), c0 + j)),
                pl.BlockSpec((K, cb), lambda j, t: (0, j))]
    args = [x, x, w]
    if has_b:
        in_specs.append(pl.BlockSpec((1, cb), lambda j, t: (0, j)))
        args.append(b)
    return pl.pallas_call(
        body, name=name, grid=(nc, nt), in_specs=in_specs,
        out_specs=pl.BlockSpec((tt, cb), lambda j, t: (t, j)),
        out_shape=jax.ShapeDtypeStruct((T, C), F32), compiler_params=_cp(("parallel", "parallel")))(*args)


def _conv_bwd(dy, x, col0, w, cb, name, into, want_db=False):
    T, C = dy.shape
    K = w.shape[0]
    tt = min(T, CONV_BWD_TT)
    nt, nc, c0 = T // tt, C // cb, col0 // cb

    def body(*refs):
        if want_db:
            dy_ref, dyn_ref, x_ref, xp_ref, w_ref, _, dx_ref, dw_ref, db_ref = refs
        else:
            dy_ref, dyn_ref, x_ref, xp_ref, w_ref, _, dx_ref, dw_ref = refs
        t = pl.program_id(1)
        dyv = dy_ref[...]
        nxt = jnp.where(t == nt - 1, 0.0, dyn_ref[...])
        dye = jnp.concatenate([dyv, nxt], axis=0)
        n = tt + 8
        acc = dye * w_ref[K - 1:K, :]
        for j in range(K - 1):
            acc = acc + pltpu.roll(dye, n - (K - 1 - j), 0) * w_ref[j:j + 1, :]
        dx_ref[...] = acc[:tt, :].astype(dx_ref.dtype)
        prev = jnp.where(t == 0, 0.0, xp_ref[...])
        xe = jnp.concatenate([prev, x_ref[...]], axis=0)

        @pl.when(t == 0)
        def _():
            dw_ref[...] = jnp.zeros_like(dw_ref)
            if want_db:
                db_ref[...] = jnp.zeros_like(db_ref)

        for j in range(K):
            sh = K - 1 - j
            xs = xe[8:, :] if sh == 0 else pltpu.roll(xe, sh, 0)[8:, :]
            dw_ref[j:j + 1, :] += jnp.sum(dyv * xs, axis=0, keepdims=True)
        if want_db:
            db_ref[...] += jnp.sum(dyv, axis=0, keepdims=True)

    h8 = tt // 8
    in_specs = [pl.BlockSpec((tt, cb), lambda j, t: (t, j)),
                pl.BlockSpec((8, cb), lambda j, t: (jnp.minimum((t + 1) * h8, T // 8 - 1), j)),
                pl.BlockSpec((tt, cb), lambda j, t: (t, c0 + j)),
                pl.BlockSpec((8, cb), lambda j, t: (jnp.maximum(t * h8 - 1, 0), c0 + j)),
                pl.BlockSpec((K, cb), lambda j, t: (0, j)), pl.BlockSpec(memory_space=pl.ANY)]
    out_specs = [pl.BlockSpec((tt, cb), lambda j, t: (t, c0 + j)), pl.BlockSpec((K, cb), lambda j, t: (0, j))]
    out_shape = [jax.ShapeDtypeStruct(into.shape, into.dtype), jax.ShapeDtypeStruct((K, C), F32)]
    if want_db:
        out_specs.append(pl.BlockSpec((1, cb), lambda j, t: (0, j)))
        out_shape.append(jax.ShapeDtypeStruct((1, C), F32))
    return pl.pallas_call(
        body, name=name, grid=(nc, nt), in_specs=in_specs, out_specs=out_specs, out_shape=out_shape,
        input_output_aliases={5: 0}, compiler_params=_cp(("parallel", "arbitrary")))(dy, dy, x, x, w, into)


def _pool_d(ue, g, pos, tt):
    win = POOL_WINDOWS[g]
    ug = ue[:, g * 128:(g + 1) * 128]
    s = ug
    k = 1
    while k < win:
        s = s + pltpu.roll(s, k, 0)
        k *= 2
    cnt = jnp.minimum(pos + 1, win).astype(F32)
    return s[16:, :] / cnt - ug[16:, :]


POOL_TT = 2048


def _pool_fwd(proj, pw, pb, ps):
    T = proj.shape[0]
    tt = min(T, POOL_TT)

    def body(u_ref, halo_ref, w_ref, b_ref, s_ref, y_ref):
        t = pl.program_id(0)
        halo = jnp.where(t == 0, 0.0, halo_ref[...])
        ue = jnp.concatenate([halo, u_ref[...]], axis=0)
        pos = t * tt + lax.broadcasted_iota(jnp.int32, (tt, 1), 0)
        for g in range(4):
            sl = slice(g * 128, (g + 1) * 128)
            d = _pool_d(ue, g, pos, tt)
            yg = _mm(d, w_ref[g]) + b_ref[:, sl]
            y_ref[:, sl] = (yg * s_ref[:, sl]).astype(BF16)

    vec = pl.BlockSpec((1, POOL_W), lambda t: (0, 0))
    return pl.pallas_call(
        body, name="pool_fwd", grid=(T // tt,),
        in_specs=[pl.BlockSpec((tt, POOL_W), lambda t: (t, 0)),
                  pl.BlockSpec((16, POOL_W), lambda t: (jnp.maximum(t * (tt // 16) - 1, 0), 0)),
                  pl.BlockSpec((4, 128, 128), lambda t: (0, 0, 0)), vec, vec],
        out_specs=pl.BlockSpec((tt, POOL_W), lambda t: (t, 0)),
        out_shape=jax.ShapeDtypeStruct((T, D_MODEL), BF16), compiler_params=_cp(("parallel",)))(
            proj, proj, pw, pb, ps)


def _pool_bwd(dmixed, proj, pw, pb, ps):
    T = proj.shape[0]
    tt = min(T, POOL_TT)
    nt = T // tt

    def body(dy_ref, dyn_ref, u_ref, halo_ref, w_ref, b_ref, s_ref, du_ref, dw_ref, db_ref, ds_ref):
        t = pl.program_id(0)
        halo = jnp.where(t == 0, 0.0, halo_ref[...])
        ue = jnp.concatenate([halo, u_ref[...]], axis=0)
        dyv = dy_ref[...]
        nxt = jnp.where(t == nt - 1, 0.0, dyn_ref[...])
        dye = jnp.concatenate([dyv, nxt], axis=0)
        n = tt + 16
        pos = t * tt + lax.broadcasted_iota(jnp.int32, (tt, 1), 0)
        pos_e = t * tt + lax.broadcasted_iota(jnp.int32, (n, 1), 0)

        @pl.when(t == 0)
        def _():
            dw_ref[...] = jnp.zeros_like(dw_ref)
            db_ref[...] = jnp.zeros_like(db_ref)
            ds_ref[...] = jnp.zeros_like(ds_ref)

        for g in range(4):
            win = POOL_WINDOWS[g]
            sl = slice(g * 128, (g + 1) * 128)
            d = _pool_d(ue, g, pos, tt)
            wg = w_ref[g]
            ypre = _mm(d, wg) + b_ref[:, sl]
            sc = s_ref[:, sl]
            ds_ref[:, sl] += jnp.sum(dyv[:, sl] * ypre, axis=0, keepdims=True)
            dyp_e = dye[:, sl] * sc
            dyp = dyp_e[:tt, :]
            db_ref[:, sl] += jnp.sum(dyp, axis=0, keepdims=True)
            dw_ref[g] += _mm(d, dyp, 0, 0)
            dd_e = _mm(dyp_e, wg, 1, 1)
            cnt_e = jnp.minimum(pos_e + 1, win).astype(F32)
            s = dd_e / cnt_e
            k = 1
            while k < win:
                s = s + pltpu.roll(s, n - k, 0)
                k *= 2
            du_ref[:, sl] = (s[:tt, :] - dd_e[:tt, :]).astype(BF16)

    vec = pl.BlockSpec((1, POOL_W), lambda t: (0, 0))
    h16 = tt // 16
    return pl.pallas_call(
        body, name="pool_bwd", grid=(nt,),
        in_specs=[pl.BlockSpec((tt, POOL_W), lambda t: (t, 0)),
                  pl.BlockSpec((16, POOL_W), lambda t: (jnp.minimum((t + 1) * h16, T // 16 - 1), 0)),
                  pl.BlockSpec((tt, POOL_W), lambda t: (t, 0)),
                  pl.BlockSpec((16, POOL_W), lambda t: (jnp.maximum(t * h16 - 1, 0), 0)),
                  pl.BlockSpec((4, 128, 128), lambda t: (0, 0, 0)), vec, vec],
        out_specs=[pl.BlockSpec((tt, POOL_W), lambda t: (t, 0)),
                   pl.BlockSpec((4, 128, 128), lambda t: (0, 0, 0)), vec, vec],
        out_shape=[jax.ShapeDtypeStruct((T, PCOLS), BF16), jax.ShapeDtypeStruct((4, 128, 128), F32),
                   jax.ShapeDtypeStruct((1, POOL_W), F32), jax.ShapeDtypeStruct((1, POOL_W), F32)],
        compiler_params=_cp(("arbitrary",)))(dmixed, dmixed, proj, proj, pw, pb, ps)


def _gdn_pre_fwd(cpre, proj, alog, dtb):
    T = cpre.shape[0]
    tt = _tile(T)

    def body(c_ref, ab_ref, alog_ref, dtb_ref, qkv_ref, bb_ref, gb_ref):
        for p in range(3):
            for h in range(HEADS):
                cc = c_ref[:, (p * HEADS + h) * HD:(p * HEADS + h + 1) * HD]
                s = cc * _sigmoid(cc)
                if p < 2:
                    s = s * lax.rsqrt(jnp.sum(s * s, axis=1, keepdims=True) + EPS)
                if p == 0:
                    s = s * (HD ** -0.5)
                qkv_ref[p, h] = s
        ab = ab_ref[...]
        g = -jnp.exp(alog_ref[...]) * _softplus(ab + dtb_ref[...])
        r64 = lax.broadcasted_iota(jnp.int32, (tt, 1), 0) & (CHUNK - 1)
        k = 1
        while k < CHUNK:
            g = g + jnp.where(r64 >= k, pltpu.roll(g, k, 0), 0.0)
            k *= 2
        sb = _sigmoid(ab)
        for h in range(HEADS):
            gb_ref[h] = jnp.broadcast_to(g[:, h:h + 1], (tt, HD))
            bb_ref[h] = jnp.broadcast_to(sb[:, HEADS + h:HEADS + h + 1], (tt, HD))

    vec = pl.BlockSpec((1, 128), lambda t: (0, 0))
    hb = pl.BlockSpec((HEADS, tt, HD), lambda t: (0, t, 0))
    return pl.pallas_call(
        body, name="gdn_pre_fwd", grid=(T // tt,),
        in_specs=[pl.BlockSpec((tt, 3 * GDN_W), lambda t: (t, 0)),
                  pl.BlockSpec((tt, 128), lambda t: (t, C_AB // 128)), vec, vec],
        out_specs=[pl.BlockSpec((3, HEADS, tt, HD), lambda t: (0, 0, t, 0)), hb, hb],
        out_shape=[jax.ShapeDtypeStruct((3, HEADS, T, HD), F32), jax.ShapeDtypeStruct((HEADS, T, HD), F32),
                   jax.ShapeDtypeStruct((HEADS, T, HD), F32)],
        compiler_params=_cp(("parallel",)))(cpre, proj, alog, dtb)


def _gdn_pre_bwd(dq, dk, dv, cpre, dbb, dgb, proj, alog, dtb, dproj):
    T = cpre.shape[0]
    tt = _tile(T)

    def body(dq_ref, dk_ref, dv_ref, c_ref, dbb_ref, dgb_ref, ab_ref, alog_ref, dtb_ref, _,
             dc_ref, dab_ref, dalog_ref, ddtb_ref):
        t = pl.program_id(0)
        srcs = (dq_ref, dk_ref, dv_ref)
        for p in range(3):
            for h in range(HEADS):
                sl = slice((p * HEADS + h) * HD, (p * HEADS + h + 1) * HD)
                cc = c_ref[:, sl]
                sg = _sigmoid(cc)
                s = cc * sg
                dyv = srcs[p][h]
                if p < 2:
                    r = lax.rsqrt(jnp.sum(s * s, axis=1, keepdims=True) + EPS)
                    y = s * r
                    if p == 0:
                        dyv = dyv * (HD ** -0.5)
                    ds = r * (dyv - y * jnp.sum(dyv * y, axis=1, keepdims=True))
                else:
                    ds = dyv
                dc_ref[:, sl] = ds * sg * (1.0 + cc * (1.0 - sg))
        lane = lax.broadcasted_iota(jnp.int32, (tt, 128), 1)
        dg = jnp.zeros((tt, 128), F32)
        dbeta = jnp.zeros((tt, 128), F32)
        for h in range(HEADS):
            dg = jnp.where(lane == h, dgb_ref[h], dg)
            dbeta = jnp.where(lane == HEADS + h, dbb_ref[h], dbeta)
        r64 = lax.broadcasted_iota(jnp.int32, (tt, 1), 0) & (CHUNK - 1)
        k = 1
        while k < CHUNK:
            dg = dg + jnp.where(r64 < CHUNK - k, pltpu.roll(dg, tt - k, 0), 0.0)
            k *= 2
        ab = ab_ref[...]
        e = jnp.exp(alog_ref[...])
        xx = ab + dtb_ref[...]
        g = -e * _softplus(xx)
        da = jnp.where(lane < HEADS, dg * (-e) * _sigmoid(xx), 0.0)
        pa = jnp.sum(jnp.where(lane < HEADS, dg * g, 0.0), axis=0, keepdims=True)
        pd = jnp.sum(da, axis=0, keepdims=True)

        @pl.when(t == 0)
        def _():
            dalog_ref[...] = pa
            ddtb_ref[...] = pd

        @pl.when(t > 0)
        def _():
            dalog_ref[...] += pa
            ddtb_ref[...] += pd

        sb = _sigmoid(ab)
        dab_ref[:, :128] = jnp.where(lane < HEADS, da, dbeta * sb * (1.0 - sb)).astype(BF16)
        dab_ref[:, 128:] = jnp.zeros((tt, 128), BF16)

    vec = pl.BlockSpec((1, 128), lambda t: (0, 0))
    hb = pl.BlockSpec((HEADS, tt, HD), lambda t: (0, t, 0))
    return pl.pallas_call(
        body, name="gdn_pre_bwd", grid=(T // tt,),
        in_specs=[hb, hb, hb, pl.BlockSpec((tt, 3 * GDN_W), lambda t: (t, 0)), hb, hb,
                  pl.BlockSpec((tt, 128), lambda t: (t, C_AB // 128)), vec, vec, pl.BlockSpec(memory_space=pl.ANY)],
        out_specs=[pl.BlockSpec((tt, 3 * GDN_W), lambda t: (t, 0)),
                   pl.BlockSpec((tt, 256), lambda t: (t, C_AB // 256)), vec, vec],
        out_shape=[jax.ShapeDtypeStruct((T, 3 * GDN_W), F32), jax.ShapeDtypeStruct(dproj.shape, dproj.dtype),
                   jax.ShapeDtypeStruct((1, 128), F32), jax.ShapeDtypeStruct((1, 128), F32)],
        input_output_aliases={9: 1},
        compiler_params=_cp(("arbitrary",)))(dq, dk, dv, cpre, dbb, dgb, proj, alog, dtb, dproj)


def _split2(x):
    hi = x.astype(BF16)
    return hi, (x - hi.astype(F32)).astype(BF16)


def _bmm3s(a2, b2, ca=2, cb=1):
    def f(x, y):
        return lax.dot_general(x, y, (((ca,), (cb,)), ((0,), (0,))), preferred_element_type=F32)

    return f(a2[0], b2[0]) + (f(a2[0], b2[1]) + f(a2[1], b2[0]))


def _bmm3(a, b, ca=2, cb=1):
    return _bmm3s(_split2(a), _split2(b), ca, cb)


def _tri_inv(a):
    nb = a.shape[0]
    ri = lax.broadcasted_iota(jnp.int32, (nb, CHUNK, CHUNK), 1)
    ci = lax.broadcasted_iota(jnp.int32, (nb, CHUNK, CHUNK), 2)
    n = -a
    p = jnp.where(ri == ci, 1.0, 0.0) + n
    n2 = _split2(n)
    for _ in range(5):
        n2 = _split2(_bmm3s(n2, n2))
        p = p + _bmm3s(_split2(p), n2)
    return p


def _gdn_chunk_common(q, k, v, bb3, gb3, tm_saved=None):
    nb = q.shape[0]
    need_t = tm_saved is not None
    beta = bb3[:, :, 0:1]
    gcol = gb3[:, :, 0:1]
    bcol = bb3[:, :, :CHUNK]
    gcm = gb3[:, :, :CHUNK]
    oh = jnp.where(lax.broadcasted_iota(jnp.int32, (nb, CHUNK, HD), 2) == 0, 1.0, 0.0)
    grow = _bmm(oh, gb3, 2, 2, HI, False)
    ri = lax.broadcasted_iota(jnp.int32, (nb, CHUNK, CHUNK), 1)
    ci = lax.broadcasted_iota(jnp.int32, (nb, CHUNK, CHUNK), 2)
    tril, stl = ri >= ci, ri > ci
    dg = gcm - grow
    dec = jnp.where(tril, jnp.exp(jnp.where(tril, dg, 0.0)), 0.0)
    kk = _bmm(k, k, 2, 2)
    qk = _bmm(q, k, 2, 2)
    tm = tm_saved if need_t else _tri_inv(jnp.where(stl, bcol * kk * dec, 0.0))
    gam = jnp.exp(gcol)
    glast = gb3[:, CHUNK - 1:CHUNK, 0:1]
    egl = jnp.exp(glast)
    rw = k * (beta * gam)
    ru = v * beta
    wu = _bmm3(tm, jnp.concatenate([rw, ru], axis=2), 2, 1)
    kdf = jnp.exp(glast - gcol)
    out = dict(beta=beta, bcol=bcol, tril=tril, stl=stl, dec=dec, kk=kk, qk=qk, tm=tm, gam=gam, egl=egl,
               rw=rw, wu=wu, at=qk * dec, qd=q * gam, kdf=kdf, kd=k * kdf)
    if need_t:
        brow = _bmm(oh, bb3, 2, 2, HI, False)
        triu, stu = ri <= ci, ri < ci
        dect = jnp.where(triu, jnp.exp(jnp.where(triu, -dg, 0.0)), 0.0)
        qkt = _bmm(k, q, 2, 2)
        eye = jnp.where(ri == ci, 1.0, 0.0)
        out.update(brow=brow, triu=triu, stu=stu, dect=dect, qkt=qkt, tmt=_bmm3(eye, tm, 2, 2),
                   att=qkt * dect)
    return out


def _gdn_rows(T, fwd=False):
    return min(T, 1024 if fwd else 512)


HP = 2


def _gdn_fwd(qkv, bb, gb, proj, nw, mixed):
    T = proj.shape[0]
    R = _gdn_rows(T, fwd=True)
    nb = R // CHUNK
    B = HP * nb

    def body(q_ref, k_ref, v_ref, bb_ref, gb_ref, z_ref, nw_ref, _, y_ref, st_ref, tm_ref,
             s_ref, w_s, u_s, at_s, qd_s, kd_s):
        t = pl.program_id(1)

        @pl.when(t == 0)
        def _():
            s_ref[...] = jnp.zeros_like(s_ref)

        sh = (B, CHUNK, HD)
        q, k, v = q_ref[...].reshape(sh), k_ref[...].reshape(sh), v_ref[...].reshape(sh)
        c = _gdn_chunk_common(q, k, v, bb_ref[...].reshape(sh), gb_ref[...].reshape(sh))
        tm_ref[...] = c["tm"].reshape(HP, nb, CHUNK, CHUNK)
        w_s[...] = c["wu"][:, :, :HD]
        u_s[...] = c["wu"][:, :, HD:]
        at_s[...] = c["at"]
        qd_s[...] = c["qd"]
        kd_s[...] = c["kd"]
        egl = c["egl"]
        nwv = nw_ref[...]
        for n in range(nb):
            rows = slice(n * CHUNK, (n + 1) * CHUNK)
            for hh in range(HP):
                b = hh * nb + n
                cols = slice(hh * HD, (hh + 1) * HD)
                s = s_ref[hh]
                st_ref[hh, n] = s
                vn = u_s[b] - _mm(w_s[b], s)
                o = _mm(qd_s[b], s) + _mm(at_s[b], vn)
                s_ref[hh] = s * egl[b] + _mm(kd_s[b], vn, 0, 0)
                zz = z_ref[rows, cols]
                on = o * lax.rsqrt(jnp.mean(o * o, axis=1, keepdims=True) + EPS)
                y_ref[rows, cols] = (on * nwv * (zz * _sigmoid(zz))).astype(BF16)

    def hm(p):
        return pl.BlockSpec((None, HP, R, HD), lambda h, t: (p, h, t, 0))

    hb = pl.BlockSpec((HP, R, HD), lambda h, t: (h, t, 0))
    cs = pltpu.VMEM((B, CHUNK, HD), F32)
    return pl.pallas_call(
        body, name="gdn_fwd", grid=(HEADS // HP, T // R),
        in_specs=[hm(0), hm(1), hm(2), hb, hb,
                  pl.BlockSpec((R, HP * HD), lambda h, t: (t, C_Z // (HP * HD) + h)),
                  pl.BlockSpec((1, HD), lambda h, t: (0, 0)), pl.BlockSpec(memory_space=pl.ANY)],
        out_specs=[pl.BlockSpec((R, HP * HD), lambda h, t: (t, M_GDN // (HP * HD) + h)),
                   pl.BlockSpec((HP, nb, HD, HD), lambda h, t: (h, t, 0, 0)),
                   pl.BlockSpec((HP, nb, CHUNK, CHUNK), lambda h, t: (h, t, 0, 0))],
        out_shape=[jax.ShapeDtypeStruct(mixed.shape, mixed.dtype),
                   jax.ShapeDtypeStruct((HEADS, T // CHUNK, HD, HD), F32),
                   jax.ShapeDtypeStruct((HEADS, T // CHUNK, CHUNK, CHUNK), F32)],
        scratch_shapes=[pltpu.VMEM((HP, HD, HD), F32), cs, cs, pltpu.VMEM((B, CHUNK, CHUNK), F32), cs, cs],
        input_output_aliases={7: 0},
        compiler_params=_cp(("parallel", "arbitrary")))(qkv, qkv, qkv, bb, gb, proj, nw, mixed)


def _gdn_bwd(qkv, bb, gb, proj, nw, states, tms, dmixed, dproj):
    T = proj.shape[0]
    R = _gdn_rows(T)
    nb = R // CHUNK
    ntb = T // R
    B = HP * nb

    def by_head(ref):
        return jnp.concatenate([ref[:, hh * HD:(hh + 1) * HD].reshape(nb, CHUNK, HD) for hh in range(HP)], axis=0)

    def body(q_ref, k_ref, v_ref, bb_ref, gb_ref, z_ref, nw_ref, st_ref, tm_ref, dy_ref, _,
             dq_ref, dk_ref, dv_ref, dbb_ref, dgb_ref, dz_ref, dnw_ref,
             ds_ref, att_s, do_s, kd_s, vn_s, qd_s, w_s, dvn_s, dkd_s, dgl_s):
        hp = pl.program_id(0)
        t = pl.program_id(1)

        @pl.when(t == 0)
        def _():
            ds_ref[...] = jnp.zeros_like(ds_ref)

        @pl.when((t == 0) & (hp == 0))
        def _():
            dnw_ref[...] = jnp.zeros_like(dnw_ref)

        sh = (B, CHUNK, HD)
        q, k, v = q_ref[...].reshape(sh), k_ref[...].reshape(sh), v_ref[...].reshape(sh)
        c = _gdn_chunk_common(q, k, v, bb_ref[...].reshape(sh), gb_ref[...].reshape(sh),
                              tm_ref[...].reshape(B, CHUNK, CHUNK))
        w, u = c["wu"][:, :, :HD], c["wu"][:, :, HD:]
        sall = st_ref[...].reshape(B, HD, HD)
        vn = u - _bmm(w, sall, 2, 1)
        o = _bmm(c["qd"], sall, 2, 1) + _bmm(c["at"], vn, 2, 1)
        z = by_head(z_ref)
        dy = by_head(dy_ref)
        nwv = nw_ref[...].reshape(1, 1, HD)
        rs = lax.rsqrt(jnp.mean(o * o, axis=2, keepdims=True) + EPS)
        on = o * rs
        sg = _sigmoid(z)
        sz = z * sg
        dnw_ref[...] += jnp.sum(jnp.sum(dy * on * sz, axis=0), axis=0, keepdims=True)
        dz3 = dy * on * nwv * (sg * (1.0 + z * (1.0 - sg)))
        for hh in range(HP):
            dz_ref[:, hh * HD:(hh + 1) * HD] = dz3[hh * nb:(hh + 1) * nb].reshape(R, HD).astype(BF16)
        don = dy * nwv * sz
        do = rs * (don - on * jnp.mean(don * on, axis=2, keepdims=True))
        dqd = _bmm(do, sall, 2, 2)
        dat = jnp.where(c["tril"], _bmm(do, vn, 2, 2), 0.0)
        datt = jnp.where(c["triu"], _bmm(vn, do, 2, 2), 0.0)
        att_s[...] = c["att"]
        do_s[...] = do
        kd_s[...] = c["kd"]
        vn_s[...] = vn
        qd_s[...] = c["qd"]
        w_s[...] = w
        egl = c["egl"]
        for n in reversed(range(nb)):
            for hh in range(HP):
                b = hh * nb + n
                dso = ds_ref[hh]
                dvn_n = _mm(att_s[b], do_s[b]) + _mm(kd_s[b], dso)
                dkd_s[b] = _mm(vn_s[b], dso, 1, 1)
                dgl = egl[b] * jnp.sum(jnp.sum(st_ref[hh, n] * dso, axis=1, keepdims=True), axis=0, keepdims=True)
                dgl_s[b] = jnp.broadcast_to(dgl, (8, HD))
                ds_ref[hh] = egl[b] * dso + _mm(qd_s[b], do_s[b], 0, 0) - _mm(w_s[b], dvn_n, 0, 0)
                dvn_s[b] = dvn_n
        dvn = dvn_s[...]
        dkd = dkd_s[...]
        dgl = dgl_s[...][:, 0:1, 0:1]
        dw = -_bmm(dvn, sall, 2, 2)
        dr = _bmm3(c["tmt"], jnp.concatenate([dw, dvn], axis=2), 2, 1)
        drw, dru = dr[:, :, :HD], dr[:, :, HD:]
        wu = c["wu"]
        dr2, wu2 = _split2(dr), _split2(wu)
        da = -jnp.where(c["stl"], _bmm3s(dr2, wu2, 2, 2), 0.0)
        da_t = -jnp.where(c["stu"], _bmm3s(wu2, dr2, 2, 2), 0.0)
        beta, gam, dec, dect, kk = c["beta"], c["gam"], c["dec"], c["dect"], c["kk"]
        bcol, brow = c["bcol"], c["brow"]
        dbeta = (jnp.sum(da * kk * dec, axis=2, keepdims=True)
                 + jnp.sum(drw * k * gam + dru * v, axis=2, keepdims=True))
        dkk = bcol * da * dec
        dkk_t = brow * da_t * dect
        e = (bcol * da * kk + dat * c["qk"]) * dec
        e_t = (brow * da_t * kk + datt * c["qkt"]) * dect
        kd = c["kd"]
        dq_ref[...] = (_bmm(dat * dec, k, 2, 1) + dqd * gam).reshape(HP, R, HD)
        dk_ref[...] = (_bmm(datt * dect, q, 2, 1) + _bmm(dkk + dkk_t, k, 2, 1) + dkd * c["kdf"]
                       + drw * (beta * gam)).reshape(HP, R, HD)
        dv_ref[...] = (dru * beta).reshape(HP, R, HD)
        skd = jnp.sum(dkd * kd, axis=2, keepdims=True)
        dgc = (jnp.sum(e, axis=2, keepdims=True) - jnp.sum(e_t, axis=2, keepdims=True)
               + jnp.sum(drw * c["rw"] + dqd * c["qd"], axis=2, keepdims=True) - skd)
        tot = jnp.sum(skd, axis=1, keepdims=True) + dgl
        rowi = lax.broadcasted_iota(jnp.int32, (B, CHUNK, 1), 1)
        dgc = dgc + jnp.where(rowi == CHUNK - 1, tot, 0.0)
        dbb_ref[...] = jnp.broadcast_to(dbeta, sh).reshape(HP, R, HD)
        dgb_ref[...] = jnp.broadcast_to(dgc, sh).reshape(HP, R, HD)

    def rt(t):
        return ntb - 1 - t

    def hm(p):
        return pl.BlockSpec((None, HP, R, HD), lambda h, t: (p, h, rt(t), 0))

    hb = pl.BlockSpec((HP, R, HD), lambda h, t: (h, rt(t), 0))
    cs = pltpu.VMEM((B, CHUNK, HD), F32)
    ob = jax.ShapeDtypeStruct((HEADS, T, HD), F32)
    return pl.pallas_call(
        body, name="gdn_bwd", grid=(HEADS // HP, ntb),
        in_specs=[hm(0), hm(1), hm(2), hb, hb,
                  pl.BlockSpec((R, HP * HD), lambda h, t: (rt(t), C_Z // (HP * HD) + h)),
                  pl.BlockSpec((1, HD), lambda h, t: (0, 0)),
                  pl.BlockSpec((HP, nb, HD, HD), lambda h, t: (h, rt(t), 0, 0)),
                  pl.BlockSpec((HP, nb, CHUNK, CHUNK), lambda h, t: (h, rt(t), 0, 0)),
                  pl.BlockSpec((R, HP * HD), lambda h, t: (rt(t), M_GDN // (HP * HD) + h)),
                  pl.BlockSpec(memory_space=pl.ANY)],
        out_specs=[hb, hb, hb, hb, hb, pl.BlockSpec((R, HP * HD), lambda h, t: (rt(t), C_Z // (HP * HD) + h)),
                   pl.BlockSpec((1, HD), lambda h, t: (0, 0))],
        out_shape=[ob, ob, ob, ob, ob, jax.ShapeDtypeStruct(dproj.shape, dproj.dtype),
                   jax.ShapeDtypeStruct((1, HD), F32)],
        scratch_shapes=[pltpu.VMEM((HP, HD, HD), F32), pltpu.VMEM((B, CHUNK, CHUNK), F32), cs, cs, cs, cs, cs, cs,
                        cs, pltpu.VMEM((B, 8, HD), F32)],
        input_output_aliases={10: 5},
        compiler_params=_cp(("arbitrary", "arbitrary")))(qkv, qkv, qkv, bb, gb, proj, nw, states, tms, dmixed,
                                                         dproj)


LRU_TT = 1024


def _lru_gates(xc, wa, ba, wx, bx, lam, gpos):
    xb = xc.astype(BF16)
    r = _sigmoid(_mm(xb, wa) + ba)
    i = _sigmoid(_mm(xb, wx) + bx)
    sp = _softplus(-lam)
    log_a = -LRU_C * r * sp
    a = jnp.exp(log_a)
    mult = jnp.where(gpos == 0, 1.0, jnp.sqrt(-_expm1(2.0 * log_a)))
    return r, i, sp, a, mult


def _lru_fwd(xc, proj, wa, ba, wx, bx, lam, mixed):
    T = xc.shape[0]
    tt = min(T, LRU_TT)

    def body(xc_ref, gr_ref, wa_ref, ba_ref, wx_ref, bx_ref, lam_ref, _, y_ref, h_ref, carry_ref):
        t = pl.program_id(1)

        @pl.when(t == 0)
        def _():
            carry_ref[...] = jnp.zeros_like(carry_ref)

        row = lax.broadcasted_iota(jnp.int32, (tt, 1), 0)
        xcv = xc_ref[...]
        r, i, sp, a, mult = _lru_gates(xcv, wa_ref[...], ba_ref[...], wx_ref[...], bx_ref[...], lam_ref[...],
                                       t * tt + row)
        av, bv = a, mult * i * xcv
        k = 1
        while k < tt:
            a_s = jnp.where(row >= k, pltpu.roll(av, k, 0), 1.0)
            b_s = jnp.where(row >= k, pltpu.roll(bv, k, 0), 0.0)
            bv = bv + av * b_s
            av = av * a_s
            k *= 2
        h = bv + av * carry_ref[0:1, :]
        carry_ref[...] = jnp.broadcast_to(h[tt - 1:tt, :], (8, 128))
        h_ref[...] = h
        y_ref[...] = (h * _gelu(gr_ref[...])).astype(BF16)

    blk = pl.BlockSpec((tt, 128), lambda j, t: (t, j))
    vec = pl.BlockSpec((1, 128), lambda j, t: (0, j))
    mat = pl.BlockSpec((None, 128, 128), lambda j, t: (j, 0, 0))
    return pl.pallas_call(
        body, name="lru_fwd", grid=(LRU_W // 128, T // tt),
        in_specs=[blk, pl.BlockSpec((tt, 128), lambda j, t: (t, C_GR // 128 + j)), mat, vec, mat, vec, vec,
                  pl.BlockSpec(memory_space=pl.ANY)],
        out_specs=[pl.BlockSpec((tt, 128), lambda j, t: (t, M_LRU // 128 + j)), blk],
        out_shape=[jax.ShapeDtypeStruct(mixed.shape, mixed.dtype), jax.ShapeDtypeStruct((T, LRU_W), F32)],
        scratch_shapes=[pltpu.VMEM((8, 128), F32)], input_output_aliases={7: 0},
        compiler_params=_cp(("parallel", "arbitrary")))(xc, proj, wa, ba, wx, bx, lam, mixed)


def _lru_bwd(dmixed, xc, proj, hst, wa, ba, wx, bx, lam, dproj):
    T = xc.shape[0]
    tt = min(T, LRU_TT)
    nt = T // tt

    def body(dy_ref, xc_ref, gr_ref, h_ref, hp_ref, wa_ref, ba_ref, wx_ref, bx_ref, lam_ref, _,
             dxc_ref, dgr_ref, dwa_ref, dwx_ref, dba_ref, dbx_ref, dlam_ref, lc_ref, ac_ref):
        t = pl.program_id(1)
        tr = nt - 1 - t

        @pl.when(t == 0)
        def _():
            lc_ref[...] = jnp.zeros_like(lc_ref)
            ac_ref[...] = jnp.zeros_like(ac_ref)
            dwa_ref[...] = jnp.zeros_like(dwa_ref)
            dwx_ref[...] = jnp.zeros_like(dwx_ref)
            dba_ref[...] = jnp.zeros_like(dba_ref)
            dbx_ref[...] = jnp.zeros_like(dbx_ref)
            dlam_ref[...] = jnp.zeros_like(dlam_ref)

        row = lax.broadcasted_iota(jnp.int32, (tt, 1), 0)
        gpos = tr * tt + row
        xcv = xc_ref[...]
        wav, wxv, lamv = wa_ref[...], wx_ref[...], lam_ref[...]
        r, i, sp, a, mult = _lru_gates(xcv, wav, ba_ref[...], wxv, bx_ref[...], lamv, gpos)
        h = h_ref[...]
        dy = dy_ref[...]
        gg, dgg = _gelu_and_grad(gr_ref[...])
        dgr_ref[...] = (dy * h * dgg).astype(BF16)
        bv = dy * gg
        cv = jnp.where(row < tt - 1, pltpu.roll(a, tt - 1, 0), ac_ref[0:1, :])
        k = 1
        while k < tt:
            c_s = jnp.where(row < tt - k, pltpu.roll(cv, tt - k, 0), 1.0)
            b_s = jnp.where(row < tt - k, pltpu.roll(bv, tt - k, 0), 0.0)
            bv = bv + cv * b_s
            cv = cv * c_s
            k *= 2
        lm = bv + cv * lc_ref[0:1, :]
        lc_ref[...] = jnp.broadcast_to(lm[0:1, :], (8, 128))
        ac_ref[...] = jnp.broadcast_to(a[0:1, :], (8, 128))
        hp = jnp.where(tr == 0, 0.0, hp_ref[...])
        hs = pltpu.roll(jnp.concatenate([hp, h], axis=0), 1, 0)[8:, :]
        da = lm * hs
        dmult = lm * i * xcv
        di = lm * mult * xcv
        dxc = lm * mult * i
        dlog_a = a * da - jnp.where(gpos == 0, 0.0, dmult * a * a / mult)
        dr = dlog_a * (-LRU_C * sp)
        dsp = jnp.sum(dlog_a * (-LRU_C * r), axis=0, keepdims=True)
        dlam_ref[...] += dsp * (-_sigmoid(-lamv))
        dpr = dr * r * (1.0 - r)
        dpi = di * i * (1.0 - i)
        dba_ref[...] += jnp.sum(dpr, axis=0, keepdims=True)
        dbx_ref[...] += jnp.sum(dpi, axis=0, keepdims=True)
        dwa_ref[...] += _mm(xcv, dpr, 0, 0)
        dwx_ref[...] += _mm(xcv, dpi, 0, 0)
        dxc_ref[...] = dxc + _mm(dpr, wav, 1, 1) + _mm(dpi, wxv, 1, 1)

    def rt(t):
        return nt - 1 - t

    blk = pl.BlockSpec((tt, 128), lambda j, t: (rt(t), j))
    vec = pl.BlockSpec((1, 128), lambda j, t: (0, j))
    mat = pl.BlockSpec((None, 128, 128), lambda j, t: (j, 0, 0))
    h8 = tt // 8
    mshape = jax.ShapeDtypeStruct((LRU_W // 128, 128, 128), F32)
    vshape = jax.ShapeDtypeStruct((1, LRU_W), F32)
    return pl.pallas_call(
        body, name="lru_bwd", grid=(LRU_W // 128, nt),
        in_specs=[pl.BlockSpec((tt, 128), lambda j, t: (rt(t), M_LRU // 128 + j)), blk,
                  pl.BlockSpec((tt, 128), lambda j, t: (rt(t), C_GR // 128 + j)), blk,
                  pl.BlockSpec((8, 128), lambda j, t: (jnp.maximum(rt(t) * h8 - 1, 0), j)),
                  mat, vec, mat, vec, vec, pl.BlockSpec(memory_space=pl.ANY)],
        out_specs=[blk, pl.BlockSpec((tt, 128), lambda j, t: (rt(t), C_GR // 128 + j)), mat, mat, vec, vec, vec],
        out_shape=[jax.ShapeDtypeStruct((T, LRU_W), F32), jax.ShapeDtypeStruct(dproj.shape, dproj.dtype),
                   mshape, mshape, vshape, vshape, vshape],
        scratch_shapes=[pltpu.VMEM((8, 128), F32), pltpu.VMEM((8, 128), F32)], input_output_aliases={10: 1},
        compiler_params=_cp(("parallel", "arbitrary")))(dmixed, xc, proj, hst, hst, wa, ba, wx, bx, lam, dproj)


FFN_CB = 512
FFN_K = 3


def _ffn_conv(ge, w_ref):
    acc = ge * w_ref[FFN_K - 1:FFN_K, :]
    for j in range(FFN_K - 1):
        acc = acc + pltpu.roll(ge, FFN_K - 1 - j, 0) * w_ref[j:j + 1, :]
    return acc


FFN_HALO = 16
FFN_TT = 2048


def _ffn_gate_fwd(up, w):
    T = up.shape[0]
    tt = min(T, FFN_TT)
    cb = FFN_CB
    nc = D_FF // cb
    hh = tt // FFN_HALO

    def body(g_ref, gp_ref, v_ref, w_ref, o_ref, ot_ref):
        t = pl.program_id(1)
        prev = jnp.where(t == 0, 0.0, gp_ref[...].astype(F32))
        ge = jnp.concatenate([prev, g_ref[...].astype(F32)], axis=0)
        gc = _ffn_conv(ge, w_ref)[FFN_HALO:, :]
        a = _gelu(gc) * v_ref[...].astype(F32)
        o_ref[...] = a.astype(BF16)
        ot_ref[...] = a.T.astype(BF16)

    return pl.pallas_call(
        body, name="ffn_gate_fwd", grid=(nc, T // tt),
        in_specs=[pl.BlockSpec((tt, cb), lambda j, t: (t, j)),
                  pl.BlockSpec((FFN_HALO, cb), lambda j, t: (jnp.maximum(t * hh - 1, 0), j)),
                  pl.BlockSpec((tt, cb), lambda j, t: (t, nc + j)),
                  pl.BlockSpec((FFN_K, cb), lambda j, t: (0, j))],
        out_specs=[pl.BlockSpec((tt, cb), lambda j, t: (t, j)), pl.BlockSpec((cb, tt), lambda j, t: (j, t))],
        out_shape=[jax.ShapeDtypeStruct((T, D_FF), BF16), jax.ShapeDtypeStruct((D_FF, T), BF16)],
        compiler_params=_cp(("parallel", "parallel")))(up, up, up, w)


def _ffn_gate_bwd(dact, up, w):
    T = up.shape[0]
    tt = min(T, FFN_TT)
    cb = FFN_CB
    nc = D_FF // cb
    nt = T // tt
    hh = tt // FFN_HALO
    H = FFN_HALO

    def body(d_ref, dn_ref, g_ref, gp_ref, gn_ref, v_ref, vn_ref, w_ref, dup_ref, dw_ref):
        t = pl.program_id(1)
        prev = jnp.where(t == 0, 0.0, gp_ref[...].astype(F32))
        ge = jnp.concatenate([prev, g_ref[...].astype(F32), gn_ref[...].astype(F32)], axis=0)
        gc = _ffn_conv(ge, w_ref)[H:, :]
        gg, dgg = _gelu_and_grad(gc)
        de = jnp.concatenate([d_ref[...].astype(F32), jnp.where(t == nt - 1, 0.0, dn_ref[...].astype(F32))], axis=0)
        ve = jnp.concatenate([v_ref[...].astype(F32), vn_ref[...].astype(F32)], axis=0)
        dup_ref[1] = (de * gg)[:tt, :].astype(BF16)
        dgc = de * ve * dgg
        n = tt + H
        acc = dgc * w_ref[FFN_K - 1:FFN_K, :]
        for j in range(FFN_K - 1):
            acc = acc + pltpu.roll(dgc, n - (FFN_K - 1 - j), 0) * w_ref[j:j + 1, :]
        dup_ref[0] = acc[:tt, :].astype(BF16)

        @pl.when(t == 0)
        def _():
            dw_ref[...] = jnp.zeros_like(dw_ref)

        dgm = dgc[:tt, :]
        for j in range(FFN_K):
            sh = FFN_K - 1 - j
            xs = ge[H:H + tt, :] if sh == 0 else pltpu.roll(ge, sh, 0)[H:H + tt, :]
            dw_ref[j:j + 1, :] += jnp.sum(dgm * xs, axis=0, keepdims=True)

    def nxt(t):
        return jnp.minimum((t + 1) * hh, T // H - 1)

    return pl.pallas_call(
        body, name="ffn_gate_bwd", grid=(nc, nt),
        in_specs=[pl.BlockSpec((tt, cb), lambda j, t: (t, j)),
                  pl.BlockSpec((H, cb), lambda j, t: (nxt(t), j)),
                  pl.BlockSpec((tt, cb), lambda j, t: (t, j)),
                  pl.BlockSpec((H, cb), lambda j, t: (jnp.maximum(t * hh - 1, 0), j)),
                  pl.BlockSpec((H, cb), lambda j, t: (nxt(t), j)),
                  pl.BlockSpec((tt, cb), lambda j, t: (t, nc + j)),
                  pl.BlockSpec((H, cb), lambda j, t: (nxt(t), nc + j)),
                  pl.BlockSpec((FFN_K, cb), lambda j, t: (0, j))],
        out_specs=[pl.BlockSpec((2, tt, cb), lambda j, t: (0, t, j)),
                   pl.BlockSpec((FFN_K, cb), lambda j, t: (0, j))],
        out_shape=[jax.ShapeDtypeStruct((2, T, D_FF), BF16), jax.ShapeDtypeStruct((FFN_K, D_FF), F32)],
        compiler_params=_cp(("parallel", "arbitrary")))(dact, dact, up, up, up, up, up, w)


def _row_tile(rows, cap):
    best = 8
    for r in range(8, min(rows, cap) + 1, 8):
        if rows % r == 0:
            best = r
    return best


def _adamw(parts, w, m, v, rt, name, layer=None, prev=None):
    P, R, C = parts.shape
    cw = w.shape[-1]

    def body(p_ref, w_ref, m_ref, v_ref, *rest):
        g_ref, d_ref, mo_ref, vo_ref = rest[-4:]
        g = p_ref[0].astype(F32)
        for i in range(1, P):
            g = g + p_ref[i].astype(F32)
        if cw != C:
            g = jnp.concatenate([g[:, :AB_ORIG_END], g[:, C_XR:]], axis=1)
        wv = w_ref[...]
        mn = ADAM_B1 * m_ref[...] + (1.0 - ADAM_B1) * g
        vn = ADAM_B2 * v_ref[...] + (1.0 - ADAM_B2) * (g * g)
        m_hat = mn / (1.0 - ADAM_B1 ** ADAM_STEP)
        v_hat = vn / (1.0 - ADAM_B2 ** ADAM_STEP)
        g_ref[...] = g
        d_ref[...] = -ADAM_LR * (m_hat / (jnp.sqrt(v_hat) + ADAM_EPS) + ADAM_WD * wv)
        mo_ref[...] = mn
        vo_ref[...] = vn

    if layer is None:
        blk = pl.BlockSpec((rt, cw), lambda r: (r, 0))
        sh = jax.ShapeDtypeStruct((R, cw), F32)
    else:
        blk = pl.BlockSpec((None, rt, cw), lambda r: (layer, r, 0))
        sh = jax.ShapeDtypeStruct(w.shape, F32)
    extra = list(prev) if prev is not None else []
    return pl.pallas_call(
        body, name=name, grid=(R // rt,),
        in_specs=[pl.BlockSpec((P, rt, C), lambda r: (0, r, 0)), blk, blk, blk]
        + [pl.BlockSpec(memory_space=pl.ANY)] * len(extra),
        out_specs=[blk, blk, blk, blk], out_shape=[sh, sh, sh, sh],
        input_output_aliases={4 + i: i for i in range(len(extra))},
        compiler_params=_cp(("parallel",)))(parts, w, m, v, *extra)


def _peer(k):
    x, y, c = lax.axis_index("x"), lax.axis_index("y"), lax.axis_index("c")
    px = 1 - x if k & 4 else x
    py = 1 - y if k & 2 else y
    pc = 1 - c if k & 1 else c
    return (px, py, pc), 4 * px + 2 * py + pc


def _all_gather(x, name):
    R, C = x.shape

    def body(x_ref, o_ref, send_sems, recv_sems, local_sem):
        me = 4 * lax.axis_index("x") + 2 * lax.axis_index("y") + lax.axis_index("c")
        mine = pltpu.make_async_copy(x_ref, o_ref.at[me], local_sem)
        mine.start()
        sends = []
        for k in range(1, N_DEV):
            dev, _ = _peer(k)
            cp = pltpu.make_async_remote_copy(src_ref=x_ref, dst_ref=o_ref.at[me], send_sem=send_sems.at[k - 1],
                                              recv_sem=recv_sems.at[k - 1], device_id=dev, device_id_type=MESH_IDS)
            cp.start()
            sends.append(cp)
        for k in range(1, N_DEV):
            dev, idx = _peer(k)
            pltpu.make_async_remote_copy(src_ref=x_ref, dst_ref=o_ref.at[idx], send_sem=send_sems.at[k - 1],
                                         recv_sem=recv_sems.at[k - 1], device_id=dev,
                                         device_id_type=MESH_IDS).wait_recv()
        for cp in sends:
            cp.wait_send()
        mine.wait()

    return pl.pallas_call(
        body, name=name, in_specs=[pl.BlockSpec(memory_space=pl.ANY)], out_specs=pl.BlockSpec(memory_space=pl.ANY),
        out_shape=jax.ShapeDtypeStruct((N_DEV, R, C), x.dtype),
        scratch_shapes=[pltpu.SemaphoreType.DMA((N_DEV - 1,)), pltpu.SemaphoreType.DMA((N_DEV - 1,)),
                        pltpu.SemaphoreType.DMA],
        compiler_params=pltpu.CompilerParams(has_side_effects=True))(x)


HBM_SPEC = pl.BlockSpec(memory_space=pltpu.HBM)
SEM_SPEC = pl.BlockSpec(memory_space=pltpu.SEMAPHORE)
EFFECT = pltpu.SideEffectType.DATAFLOW_SIDE_EFFECTING
OTHER_CHIPS = ((1, 0), (0, 1), (1, 1))


def _split_start(bufs, plan, n, name):
    nb = len(bufs)

    def body(*refs):
        send_sems, recv_sems, token = refs[nb], refs[nb + 1], refs[2 * nb + 2]
        for i, (src, dst, _, dev) in enumerate(plan(refs[:nb])):
            pltpu.make_async_remote_copy(src_ref=src, dst_ref=dst, send_sem=send_sems.at[i],
                                         recv_sem=recv_sems.at[i], device_id=dev, device_id_type=MESH_IDS).start()
        token[...] = jnp.zeros_like(token)

    outs = pl.pallas_call(
        body, name=name,
        out_shape=(pltpu.SemaphoreType.DMA((n,)), pltpu.SemaphoreType.DMA((n,)),
                   *[pltpu.HBM(b.shape, b.dtype) for b in bufs], jax.ShapeDtypeStruct((8, 128), F32)),
        in_specs=[HBM_SPEC] * nb,
        out_specs=(SEM_SPEC, SEM_SPEC, *[HBM_SPEC] * nb, pl.BlockSpec(memory_space=pltpu.VMEM)),
        input_output_aliases={i: 2 + i for i in range(nb)},
        compiler_params=pltpu.CompilerParams(has_side_effects=EFFECT),
    )(*[pltpu.with_memory_space_constraint(b, pltpu.HBM) for b in bufs])
    return dict(send=outs[0], recv=outs[1], bufs=list(outs[2:2 + nb]), token=outs[2 + nb], plan=plan, n=n)


def _split_wait(st, after, name):
    bufs = st["bufs"]
    nb = len(bufs)
    plan = st["plan"]
    afters = list(after) if isinstance(after, (list, tuple)) else [after]

    def body(*refs):
        send_sems, recv_sems = refs[nb], refs[nb + 1]
        for i, (src, dst, land, dev) in enumerate(plan(refs[:nb])):
            pltpu.make_async_remote_copy(src_ref=src, dst_ref=dst, send_sem=send_sems.at[i],
                                         recv_sem=recv_sems.at[i], device_id=dev,
                                         device_id_type=MESH_IDS).wait_send()
            pltpu.make_async_remote_copy(src_ref=src, dst_ref=land, send_sem=send_sems.at[i],
                                         recv_sem=recv_sems.at[i], device_id=dev,
                                         device_id_type=MESH_IDS).wait_recv()

    outs = pl.pallas_call(
        body, name=name, out_shape=tuple(pltpu.HBM(b.shape, b.dtype) for b in bufs),
        in_specs=[HBM_SPEC] * nb + [SEM_SPEC, SEM_SPEC] + [pl.BlockSpec(memory_space=pl.ANY)] * len(afters),
        out_specs=tuple([HBM_SPEC] * nb), input_output_aliases={i: i for i in range(nb)},
        compiler_params=pltpu.CompilerParams(has_side_effects=EFFECT),
    )(*bufs, st["send"], st["recv"], *afters)
    return list(outs)


def _xyc():
    return lax.axis_index("x"), lax.axis_index("y"), lax.axis_index("c")


def _flip(x, y, a, b):
    return (1 - x if a else x), (1 - y if b else y)


def _ag1_plan(outs):
    x, y, c = _xyc()
    me = 4 * x + 2 * y + c
    copies = []
    for o in outs:
        copies.append((o.at[me], o.at[me], o.at[4 * x + 2 * y + 1 - c], (x, y, 1 - c)))
        for a, b in OTHER_CHIPS:
            px, py = _flip(x, y, a, b)
            copies.append((o.at[me], o.at[me], o.at[4 * px + 2 * py + c], (px, py, c)))
    return copies


def _ag2_plan(outs):
    x, y, c = _xyc()
    copies = []
    for o in outs:
        for a, b in OTHER_CHIPS:
            px, py = _flip(x, y, a, b)
            mine, sibs = 4 * px + 2 * py + c, 4 * px + 2 * py + 1 - c
            copies.append((o.at[mine], o.at[mine], o.at[sibs], (x, y, 1 - c)))
    return copies


def _rs1_plan(refs):
    x, y, c = _xyc()
    copies = []
    for g, land in zip(refs[0::2], refs[1::2]):
        for j in range(4):
            copies.append((g.at[2 * j + 1 - c], land.at[j], land.at[j], (x, y, 1 - c)))
    return copies


def _rs2_plan(refs):
    x, y, c = _xyc()
    mychip = 2 * x + y
    copies = []
    for s, land in zip(refs[0::2], refs[1::2]):
        for a, b in OTHER_CHIPS:
            px, py = _flip(x, y, a, b)
            copies.append((s.at[2 * px + py], land.at[mychip], land.at[2 * px + py], (px, py, c)))
    return copies


def _landing_like(g, name):
    def body(g_ref, o_ref):
        del g_ref, o_ref

    anyspec = pl.BlockSpec(memory_space=pl.ANY)
    return pl.pallas_call(body, name=name, in_specs=[anyspec], out_specs=anyspec,
                          out_shape=jax.ShapeDtypeStruct((4,) + g.shape[1:], g.dtype))(g)


def _place(x, slots, by_chip, name):
    R, C = x.shape[-2:]
    rt = _row_tile(R, 512)
    xi, yi, ci = _xyc()
    idx = (2 * xi + yi if by_chip else 4 * xi + 2 * yi + ci).astype(jnp.int32).reshape(1)

    def body(i_ref, x_ref, o_ref):
        o_ref[...] = x_ref[...]

    if by_chip:
        in_spec = pl.BlockSpec((None, rt, C), lambda r, i: (i[0], r, 0))
    else:
        in_spec = pl.BlockSpec((rt, C), lambda r, i: (r, 0))
    grid_spec = pltpu.PrefetchScalarGridSpec(
        num_scalar_prefetch=1, grid=(R // rt,), in_specs=[in_spec],
        out_specs=pl.BlockSpec((None, rt, C), lambda r, i: (i[0], r, 0)))
    return pl.pallas_call(body, name=name, grid_spec=grid_spec,
                          out_shape=jax.ShapeDtypeStruct((slots, R, C), x.dtype),
                          compiler_params=_cp(("parallel",)))(idx, x)


def _pair_sum(g, land, cidx, name):
    _, R, C = land.shape
    rt = _row_tile(R, 512)
    g4 = g.reshape(4, 2, R, C)

    def body(c_ref, g_ref, l_ref, o_ref):
        o_ref[...] = (g_ref[...].astype(F32) + l_ref[...].astype(F32)).astype(o_ref.dtype)

    grid_spec = pltpu.PrefetchScalarGridSpec(
        num_scalar_prefetch=1, grid=(4, R // rt),
        in_specs=[pl.BlockSpec((None, None, rt, C), lambda j, r, c_ref: (j, c_ref[0], r, 0)),
                  pl.BlockSpec((None, rt, C), lambda j, r, c_ref: (j, r, 0))],
        out_specs=pl.BlockSpec((None, rt, C), lambda j, r, c_ref: (j, r, 0)))
    return pl.pallas_call(body, name=name, grid_spec=grid_spec, out_shape=jax.ShapeDtypeStruct(land.shape, land.dtype),
                          compiler_params=_cp(("parallel", "parallel")))(cidx, g4, land)


def _no_hook(event, l, after, payload=None):
    return None


def _tie(x, token):
    if token is None:
        return x

    def body(x_ref, t_ref, o_ref):
        del x_ref, t_ref, o_ref

    anyspec = pl.BlockSpec(memory_space=pl.ANY)
    return pl.pallas_call(body, name="tie", in_specs=[anyspec, anyspec], out_specs=anyspec,
                          out_shape=jax.ShapeDtypeStruct(x.shape, x.dtype), input_output_aliases={0: 0})(x, token)


def _layer_fwd(x, W, l, hook=_no_hook):
    T = x.shape[0]
    n = f"l{l}_"
    h1, h1t = _norm_fwd(x, W["norm1"], n + "norm1_fwd")
    proj = _mm_nn(h1, W["win"], F32, n + "mm_in", tn_c=(1792,))
    y_pool = _pool_fwd(proj, W["pool_w"], W["pool_b"], _tie(W["pool_s"], hook("f_in", l, proj)))
    cpre = _conv_fwd(proj, C_QKV, 3 * GDN_W, W["gconv_w"], None, 256, n + "gdn_conv_fwd")
    qkv, bb, gb = _gdn_pre_fwd(cpre, proj, W["alog"], W["dtb"])
    mixed, states, tms = _gdn_fwd(qkv, bb, gb, proj, W["gnorm"], y_pool)
    lconv_w = _tie(W["lconv_w"], hook("f_mix", l, states))
    xc = _conv_fwd(proj, C_XR, LRU_W, lconv_w, W["lconv_b"], 128, n + "lru_conv_fwd")
    mixed, hst = _lru_fwd(xc, proj, W["wa"], W["ba"], W["wx"], W["bx"], W["lam"], mixed)
    mixed = _tie(mixed, hook("f_out", l, hst))
    x1 = _mm_nn(mixed, W["wout"], F32, n + "mm_out", add=x)
    h2, h2t = _norm_fwd(x1, W["norm2"], n + "norm2_fwd")
    h2 = _tie(h2, hook("f_n2", l, h2t))
    up = _mm_up(h2, W["wup"], n + "mm_up")
    act, act_t = _ffn_gate_fwd(up, _tie(W["fconv_w"], hook("f_up", l, up)))
    act = _tie(act, hook("f_act", l, act_t))
    x2 = _mm_nn(act, W["wdown"], F32, n + "mm_down", add=x1, tk_c=(3072, 2048, 1536, 1024, 512, 256))
    hook("f_end", l, x2)
    saved = dict(x=x, h1t=h1t, proj=proj, cpre=cpre, qkv=qkv, bb=bb, gb=gb, states=states, tms=tms, xc=xc, hst=hst,
                 mixed=mixed, x1=x1, h2t=h2t, up=up, act_t=act_t)
    return x2, saved


def _layer_bwd(dx2, dx2b, W, S, l, hook=_no_hook):
    T = dx2.shape[0]
    n = f"l{l}_"
    dact = _mm_nt(dx2b, W["wdown"], BF16, n + "mm_down_dx", tk_c=(2048,), tn_c=(1536,))
    g_wdown = _mm_nn(S["act_t"], dx2b, BF16, n + "mm_down_dw", tn_c=(2048,))
    dup, g_fconv = _ffn_gate_bwd(dact, S["up"], W["fconv_w"])
    ns = W["wup"].shape[2]
    dh2 = _mm_up_t(dup, W["wup"], n + "mm_up_dx")
    g_wup = _mm_dup(S["h2t"], dup, ns, n + "mm_up_dw")
    tok = hook("b_ffn", l, g_wup, dict(ffn_down=g_wdown, ffn_up=g_wup))
    dx1, dx1b, g_norm2 = _norm_bwd(S["x1"], _tie(W["norm2"], tok), dh2, dx2, n + "norm2_bwd")
    dmixed = _mm_nt(dx1b, W["wout"], F32, n + "mm_out_dx", tk_c=(2048,), tn_c=(2048,))
    g_wout = _mm_tn(S["mixed"], dx1b, BF16, n + "mm_out_dw", tn_c=(2048,))
    tok = hook("b_mid", l, g_wout)
    proj = S["proj"]
    dproj, g_pool_w, g_pool_b, g_pool_s = _pool_bwd(dmixed, proj, W["pool_w"], W["pool_b"], _tie(W["pool_s"], tok))
    dq, dk, dv, dbb, dgb, dproj, g_gnorm = _gdn_bwd(S["qkv"], S["bb"], S["gb"], proj, W["gnorm"], S["states"],
                                                    S["tms"], dmixed, dproj)
    dc, dproj, g_alog, g_dtb = _gdn_pre_bwd(dq, dk, dv, S["cpre"], dbb, dgb, proj, W["alog"], W["dtb"], dproj)
    dproj, g_gconv = _conv_bwd(dc, proj, C_QKV, W["gconv_w"], 256, n + "gdn_conv_bwd", dproj)
    dxc, dproj, g_wa, g_wx, g_ba, g_bx, g_lam = _lru_bwd(dmixed, S["xc"], proj, S["hst"], W["wa"], W["ba"], W["wx"],
                                                          W["bx"], W["lam"], dproj)
    dproj, g_lconv, g_lconv_b = _conv_bwd(dxc, proj, C_XR, W["lconv_w"], 128, n + "lru_conv_bwd", dproj,
                                          want_db=True)
    dh1 = _mm_nt(dproj, W["win"], F32, n + "mm_in_dx", tk_c=(1792,))
    g_win = _mm_nn(S["h1t"], dproj, BF16, n + "mm_in_dw", tn_c=(1792,))
    tok = hook("b_in", l, g_win, dict(w_out=g_wout, w_in=g_win))
    dx, dxb, g_norm1 = _norm_bwd(S["x"], _tie(W["norm1"], tok), dh1, dx1, n + "norm1_bwd")
    big = dict(w_in=g_win, w_out=g_wout, ffn_up=g_wup, ffn_down=g_wdown)
    small = dict(norm1_w=g_norm1[0], pool_w=g_pool_w, pool_b=g_pool_b.reshape(4, 128), pool_scale=g_pool_s[0],
                 gdn_conv_w=g_gconv, gdn_a_log=g_alog[0, :HEADS], gdn_dt_bias=g_dtb[0, :HEADS],
                 gdn_norm_w=g_gnorm[0], lru_conv_w=g_lconv, lru_conv_b=g_lconv_b[0], lru_wa=g_wa, lru_ba=g_ba[0],
                 lru_wx=g_wx, lru_bx=g_bx[0], lru_lambda=g_lam[0], norm2_w=g_norm2[0], ffn_conv_w=g_fconv)
    dxb = _tie(dxb, hook("b_end", l, dx, small))
    return dx, dxb, big, small


def _pad_lane(v):
    return jnp.pad(v, (0, 128 - v.shape[0])).reshape(1, 128)


def _layer_weights(l, big, P, conv_full):
    return dict(
        win=big.get("w_in"), wout=big.get("w_out"), wup=big.get("ffn_up"), wdown=big.get("ffn_down"),
        norm1=P["norm1_w"][l].reshape(1, D_MODEL), norm2=P["norm2_w"][l].reshape(1, D_MODEL),
        pool_w=P["pool_w"][l], pool_b=P["pool_b"][l].reshape(1, POOL_W), pool_s=P["pool_scale"][l].reshape(1, POOL_W),
        gconv_w=conv_full["gdn_conv_w"][l], alog=_pad_lane(P["gdn_a_log"][l]), dtb=_pad_lane(P["gdn_dt_bias"][l]),
        gnorm=P["gdn_norm_w"][l].reshape(1, HD),
        lconv_w=conv_full["lru_conv_w"][l], lconv_b=P["lru_conv_b"][l].reshape(1, LRU_W),
        wa=P["lru_wa"][l], ba=P["lru_ba"][l].reshape(1, LRU_W), wx=P["lru_wx"][l],
        bx=P["lru_bx"][l].reshape(1, LRU_W), lam=P["lru_lambda"][l].reshape(1, LRU_W),
        fconv_w=conv_full["ffn_conv_w"][l])


def _local_step(x, target, Ws, final_norm_w, hook=_no_hook):
    saved = []
    for l in range(DEPTH):
        x, s = _layer_fwd(x, Ws[l], l, hook)
        saved.append(s)
    loss, dx, dxb, g_final = _loss_head(x, final_norm_w.reshape(1, D_MODEL), target)
    bigs, smalls = [None] * DEPTH, [None] * DEPTH
    for l in reversed(range(DEPTH)):
        dx, dxb, bigs[l], smalls[l] = _layer_bwd(dx, dxb, Ws[l], saved[l], l, hook)
    return loss, dx, g_final[0], bigs, smalls


SMALL_REPL = ("norm1_w", "pool_w", "pool_b", "pool_scale", "gdn_a_log", "gdn_dt_bias", "gdn_norm_w", "lru_conv_b",
              "lru_wa", "lru_ba", "lru_wx", "lru_bx", "lru_lambda", "norm2_w", "final_norm_w")
SMALL_SHARD = ("gdn_conv_w", "lru_conv_w", "ffn_conv_w")
BIG = ("w_in", "w_out", "ffn_up", "ffn_down")
WEIGHTS = ("norm1_w", "w_in", "pool_w", "pool_b", "pool_scale", "gdn_conv_w", "gdn_a_log", "gdn_dt_bias",
           "gdn_norm_w", "lru_conv_w", "lru_conv_b", "lru_wa", "lru_ba", "lru_wx", "lru_bx", "lru_lambda", "w_out",
           "norm2_w", "ffn_up", "ffn_conv_w", "ffn_down", "final_norm_w")
SEG = 1024
PACK_ROWS_MULT = 256 * 128


def _pack(arrs):
    pieces, table, off = [], [], 0
    for a in arrs:
        n = a.size
        npad = -(-n // SEG) * SEG
        pieces.append(jnp.pad(a.reshape(-1).astype(F32), (0, npad - n)))
        table.append((off, n, a.shape))
        off += npad
    tail = -off % PACK_ROWS_MULT
    if tail:
        pieces.append(jnp.zeros((tail,), F32))
        off += tail
    return jnp.concatenate(pieces).reshape(off // 128, 128), table


def _unpack(buf, table):
    flat = buf.reshape(-1)
    return [flat[off:off + n].reshape(shape) for off, n, shape in table]


def _pad_in(w):
    z1 = jnp.zeros(w.shape[:-1] + (C_XR - AB_ORIG_END,), w.dtype)
    return jnp.concatenate([w[..., :AB_ORIG_END], z1, w[..., AB_ORIG_END:]], axis=-1)


def _unpad_in(w):
    return jnp.concatenate([w[..., :AB_ORIG_END], w[..., C_XR:C_GR + LRU_W]], axis=-1)


def kernel(x, norm1_w, w_in, pool_w, pool_b, pool_scale, gdn_conv_w, gdn_a_log, gdn_dt_bias, gdn_norm_w, lru_conv_w, lru_conv_b, lru_wa, lru_ba, lru_wx, lru_bx, lru_lambda, w_out, norm2_w, ffn_up, ffn_conv_w, ffn_down, final_norm_w, loss_target, m_norm1_w, m_w_in, m_pool_w, m_pool_b, m_pool_scale, m_gdn_conv_w, m_gdn_a_log, m_gdn_dt_bias, m_gdn_norm_w, m_lru_conv_w, m_lru_conv_b, m_lru_wa, m_lru_ba, m_lru_wx, m_lru_bx, m_lru_lambda, m_w_out, m_norm2_w, m_ffn_up, m_ffn_conv_w, m_ffn_down, m_final_norm_w, v_norm1_w, v_w_in, v_pool_w, v_pool_b, v_pool_scale, v_gdn_conv_w, v_gdn_a_log, v_gdn_dt_bias, v_gdn_norm_w, v_lru_conv_w, v_lru_conv_b, v_lru_wa, v_lru_ba, v_lru_wx, v_lru_bx, v_lru_lambda, v_w_out, v_norm2_w, v_ffn_up, v_ffn_conv_w, v_ffn_down, v_final_norm_w):
    loc = dict(locals())
    Wp = {n: loc[n] for n in WEIGHTS}
    Mp = {n: loc["m_" + n] for n in WEIGHTS}
    Vp = {n: loc["v_" + n] for n in WEIGHTS}
    xi, yi, ci = _xyc()
    me = 4 * xi + 2 * yi + ci
    mychip = 2 * xi + yi
    cidx = ci.astype(jnp.int32).reshape(1)
    keys = dict(w_in="win", w_out="wout", ffn_up="wup", ffn_down="wdown")

    def shard2d(d, name, l):
        a = d[name][l]
        return _pad_in(a) if name == "w_in" else a

    def wshard(l, name):
        return shard2d(Wp, name, l).astype(BF16)

    def full2d(name, full):
        return full if name == "ffn_up" else full.reshape(-1, full.shape[2])

    def ag_start(shards, tag, token=None):
        if token is not None:
            shards = [_tie(shards[0], token)] + list(shards[1:])
        bufs = [_place(s, N_DEV, False, f"place_{tag}{i}") for i, s in enumerate(shards)]
        return _split_start(bufs, _ag1_plan, 4 * len(bufs), f"ag1s_{tag}")

    def ag_mid(st, after, tag):
        bufs = _split_wait(st, after, f"ag1w_{tag}")
        return _split_start(bufs, _ag2_plan, 3 * len(bufs), f"ag2s_{tag}")

    def ag_end(st, after, tag):
        return _split_wait(st, after, f"ag2w_{tag}")

    def rs_start(gs, tag):
        bufs = []
        for nm, g in gs.items():
            if nm != "ffn_up":
                g = g.reshape(N_DEV, g.shape[0] // N_DEV, g.shape[1])
            bufs += [g, _landing_like(g, f"land_{nm}_{tag}")]
        st = _split_start(bufs, _rs1_plan, 4 * len(gs), f"rs1s_{tag}")
        st["names"] = list(gs)
        return st

    def rs_mid(st, after, tag):
        bufs = _split_wait(st, after, f"rs1w_{tag}")
        out = []
        for i, nm in enumerate(st["names"]):
            s = _pair_sum(bufs[2 * i], bufs[2 * i + 1], cidx, f"pairsum_{nm}_{tag}")
            out += [s, _place(s, 4, True, f"place_{nm}_{tag}")]
        st2 = _split_start(out, _rs2_plan, 3 * len(st["names"]), f"rs2s_{tag}")
        st2["names"] = st["names"]
        return st2

    def rs_end(st, after, tag):
        bufs = _split_wait(st, after, f"rs2w_{tag}")
        return dict(zip(st["names"], bufs[1::2]))

    lnames = tuple(n for n in SMALL_REPL if n != "final_norm_w") + SMALL_SHARD

    def small_pack(l, gs, extra):
        return _pack([gs[nm] for nm in lnames] + extra)

    def small_state(d, l, gs):
        arrs = [d[nm][l] if nm in SMALL_REPL else jnp.zeros(gs[nm].shape, F32) for nm in lnames]
        if l == 0:
            arrs += [d["final_norm_w"], jnp.zeros((1,), F32)]
        return _pack(arrs)[0]

    wmv = {name: [d[name] for d in (Wp, Mp, Vp)] for name in BIG}
    groups = dict(b1=[(0, "w_out")], b2=[(0, "ffn_up")], b3=[(0, "ffn_down")],
                  c1=[(1, "w_in"), (1, "w_out")], c2=[(1, "ffn_up")], c3=[(1, "ffn_down")])
    gshards = {g: [wshard(l, n) for l, n in members] for g, members in groups.items()}
    fwd_plan = {("f_in", 0): [("mid", "b1"), ("start", "b2")],
                ("f_out", 0): [("end", "b1"), ("mid", "b2"), ("start", "b3")],
                ("f_n2", 0): [("end", "b2"), ("start", "c1")],
                ("f_up", 0): [("mid", "b3")],
                ("f_act", 0): [("end", "b3"), ("mid", "c1"), ("start", "c2")],
                ("f_end", 0): [("end", "c1")],
                ("f_mix", 1): [("mid", "c2"), ("start", "c3")],
                ("f_out", 1): [("end", "c2")],
                ("f_n2", 1): [("mid", "c3")],
                ("f_act", 1): [("end", "c3")]}
    later = [s for g in gshards.values() for s in g]
    stA = ag_start([wshard(0, "w_in")], "a")
    stA2 = ag_mid(stA, [stA["token"]] + later, "a")
    gst = {"b1": ag_start(gshards["b1"], "b1", stA2["token"])}
    (w_in0,) = ag_end(stA2, gst["b1"]["token"], "a")

    cbuf, ctable = _pack([Wp[n] for n in SMALL_SHARD])
    call = _all_gather(cbuf, "ag_conv_w")
    parts = [_unpack(call[i], ctable) for i in range(N_DEV)]
    conv_full = {n: jnp.concatenate([parts[i][j] for i in range(N_DEV)], axis=-1) for j, n in enumerate(SMALL_SHARD)}

    Ws = [_layer_weights(l, {}, Wp, conv_full) for l in range(DEPTH)]
    Ws[0]["win"] = full2d("w_in", w_in0)
    st = {}

    def fwd_actions(actions, after):
        token = None
        for what, g in actions:
            dep = after if token is None else token
            if what == "start":
                gst[g] = ag_start(gshards[g], g, dep)
                token = gst[g]["token"]
            elif what == "mid":
                gst[g] = ag_mid(gst[g], dep, g)
                token = gst[g]["token"]
            else:
                bufs = ag_end(gst[g], dep, g)
                for (gl, n), b in zip(groups[g], bufs):
                    Ws[gl][keys[n]] = full2d(n, b)
                token = bufs[0]
        return token

    def hook(event, l, after, payload=None):
        if (event, l) in fwd_plan:
            return fwd_actions(fwd_plan[event, l], after)
        if event == "b_ffn":
            st["ffn", l] = rs_start(payload, f"ffn{l}")
            return st["ffn", l]["token"]
        if event == "b_mid":
            st["ffn2", l] = rs_mid(st["ffn", l], after, f"ffn{l}")
            if l == 0:
                st["sm1b"] = ag_mid(st["sm1"], st["ffn2", l]["token"], "sm1")
                return st["sm1b"]["token"]
            return st["ffn2", l]["token"]
        if event == "b_in":
            st["io", l] = rs_start(payload, f"io{l}")
            if l == 0:
                st["sm1g"] = ag_end(st["sm1b"], st["io", l]["token"], "sm1")[0]
            return st["io", l]["token"]
        if event == "b_end" and l == 1:
            st["io2", 1] = rs_mid(st["io", 1], after, "io1")
            gbuf1, st["table1"] = small_pack(1, payload, [])
            st["sm1"] = ag_start([gbuf1], "sm1", st["io2", 1]["token"])
            return st["sm1"]["token"]
        return None

    loss, dx, g_final, _, gsmall = _local_step(x[0], loss_target[0], Ws, final_norm_w, hook)

    out_g, out_d, out_m, out_v = {}, {}, {}, {}
    outs4 = (out_g, out_d, out_m, out_v)
    rts = dict(w_in=64, w_out=128, ffn_up=256, ffn_down=128)
    big_res = {}

    def adam_big(l, parts):
        for name, p in parts.items():
            big_res[name] = _adamw(p, *wmv[name], rts[name], f"adamw_{name}_{l}", layer=l, prev=big_res.get(name))
        return [big_res[name][0] for name in parts]

    def adam_small(l, gall, gs):
        rs = gall.shape[1]
        return _adamw(gall, small_state(Wp, l, gs), small_state(Mp, l, gs), small_state(Vp, l, gs),
                      _row_tile(rs, 512), f"adamw_small_{l}")

    gbuf0, table0 = small_pack(0, gsmall[0], [g_final, loss[0, :1]])
    sm0 = ag_start([gbuf0], "sm0", st["io", 0]["token"])
    o = adam_big(1, rs_end(st["ffn2", 1], sm0["token"], "ffn1"))
    st["io2", 0] = rs_mid(st["io", 0], o, "io0")
    o = adam_big(1, rs_end(st["io2", 1], st["io2", 0]["token"], "io1"))
    o = adam_big(0, rs_end(st["ffn2", 0], o, "ffn0"))
    small_res = {1: adam_small(1, _tie(st["sm1g"], o[-1]), gsmall[1])}
    sm0b = ag_mid(sm0, o + [small_res[1][0]], "sm0")
    small_res[0] = adam_small(0, ag_end(sm0b, sm0b["token"], "sm0")[0], gsmall[0])
    adam_big(0, rs_end(st["io2", 0], small_res[0][0], "io0"))

    for name in BIG:
        for i, dst in enumerate(outs4):
            dst[name] = big_res[name][i]

    unp = {0: [_unpack(r, table0) for r in small_res[0]], 1: [_unpack(r, st["table1"]) for r in small_res[1]]}
    for j, nm in enumerate(lnames):
        if nm in SMALL_REPL:
            for i, dst in enumerate(outs4):
                dst[nm] = jnp.stack([unp[l][i][j] for l in range(DEPTH)])
    for i, dst in enumerate(outs4):
        dst["final_norm_w"] = unp[0][i][len(lnames)]
    loss_total = unp[0][0][len(lnames) + 1][0]

    gsh = []
    for nm in SMALL_SHARD:
        j = lnames.index(nm)
        width = Wp[nm].shape[-1]
        gsh.append(jnp.stack([lax.dynamic_slice_in_dim(unp[l][0][j], me * width, width, axis=1)
                              for l in range(DEPTH)]))
    sbuf, stable = _pack(gsh)
    res = _adamw(sbuf[None], _pack([Wp[n] for n in SMALL_SHARD])[0], _pack([Mp[n] for n in SMALL_SHARD])[0],
                 _pack([Vp[n] for n in SMALL_SHARD])[0], sbuf.shape[0], "adamw_conv_w")
    unp2 = [_unpack(r, stable) for r in res]
    for j, nm in enumerate(SMALL_SHARD):
        for i, dst in enumerate((out_g, out_d, out_m, out_v)):
            dst[nm] = unp2[i][j]

    return (loss_total, dx[None], *[out_g[n] for n in WEIGHTS], *[out_d[n] for n in WEIGHTS],
            *[out_m[n] for n in WEIGHTS], *[out_v[n] for n in WEIGHTS])
```

```python
import functools

import jax
import jax.numpy as jnp
from jax import lax
from jax.experimental import pallas as pl
from jax.experimental.pallas import tpu as pltpu

F32 = jnp.float32
BF16 = jnp.bfloat16
HI = lax.Precision.HIGHEST
MESH_IDS = pl.DeviceIdType.MESH

N_DEV = 8
D_MODEL = 2048
DEPTH = 2
POOL_WINDOWS = (2, 4, 8, 16)
POOL_W = 512
HEADS = 6
HD = 128
GDN_W = HEADS * HD
CHUNK = 64
LRU_W = 768
LRU_C = 8.0
D_FF = 3 * D_MODEL
EPS = 1e-6
IN_COLS = 5132
PCOLS = 5376
C_QKV, C_Z, C_AB, C_XR, C_GR = 512, 2816, 3584, 3840, 4608
AB_ORIG_END = 3596
M_GDN, M_LRU = 512, 1280

ADAM_LR, ADAM_B1, ADAM_B2, ADAM_EPS, ADAM_WD, ADAM_STEP = 0.001, 0.9, 0.999, 1e-08, 0.01, 10

VMEM_LIMIT = 56 * 1024 * 1024


def _cp(sem):
    return pltpu.CompilerParams(dimension_semantics=sem, vmem_limit_bytes=VMEM_LIMIT)


def _mm(a, b, ca=1, cb=0, prec=None, cast=True):
    if cast:
        a = a.astype(BF16)
        b = b.astype(BF16)
    return lax.dot_general(a, b, (((ca,), (cb,)), ((), ())), preferred_element_type=F32, precision=prec)


def _bmm(a, b, ca=2, cb=1, prec=None, cast=True):
    if cast:
        a = a.astype(BF16)
        b = b.astype(BF16)
    return lax.dot_general(a, b, (((ca,), (cb,)), ((0,), (0,))), preferred_element_type=F32, precision=prec)


def _sigmoid(x):
    return 1.0 / (1.0 + jnp.exp(-x))


def _log1p(e):
    u = 1.0 + e
    return jnp.where(u == 1.0, e, jnp.log(u) * e / jnp.where(u == 1.0, 1.0, u - 1.0))


def _softplus(x):
    return jnp.maximum(x, 0.0) + _log1p(jnp.exp(-jnp.abs(x)))


def _expm1(x):
    u = jnp.exp(x)
    um = u - 1.0
    safe = jnp.where((u == 1.0) | (um == -1.0), 1.0, jnp.log(u))
    return jnp.where(u == 1.0, x, jnp.where(um == -1.0, -1.0, um * x / safe))


_G0 = 0.7978845608028654
_G1 = 0.044715


def _gelu(x):
    return (0.5 * x) * (1.0 + jnp.tanh(x * (_G0 + (_G0 * _G1) * (x * x))))


def _gelu_and_grad(x):
    x2 = x * x
    hx = 0.5 * x
    th = jnp.tanh(x * (_G0 + (_G0 * _G1) * x2))
    p = 1.0 + th
    dg = 0.5 * p + (hx * (1.0 - th * th)) * (_G0 + (3.0 * _G0 * _G1) * x2)
    return hx * p, dg


def _tile(T):
    return min(T, 512)


def _matmul(a, b, *, grid, a_spec, b_spec, out_shape, out_spec, dims, acc_shape, name, add=None, add_spec=None):
    nk = grid[2]
    has_add = add is not None

    def body(*refs):
        if has_add:
            a_ref, b_ref, add_ref, o_ref, acc_ref = refs
        else:
            a_ref, b_ref, o_ref, acc_ref = refs
            add_ref = None
        k = pl.program_id(2)
        p = lax.dot_general(a_ref[...].astype(BF16), b_ref[...].astype(BF16), (dims, ((), ())),
                            preferred_element_type=F32)

        def finish(r):
            if has_add:
                r = r + add_ref[...]
            o_ref[...] = r.astype(o_ref.dtype)

        if nk == 1:
            finish(p)
        else:
            @pl.when(k == 0)
            def _():
                acc_ref[...] = p

            @pl.when(k > 0)
            def _():
                acc_ref[...] += p

            @pl.when(k == nk - 1)
            def _():
                finish(acc_ref[...])

    in_specs = [a_spec, b_spec] + ([add_spec] if has_add else [])
    args = (a, b) + ((add,) if has_add else ())
    return pl.pallas_call(
        body, name=name, grid=grid, in_specs=in_specs, out_specs=out_spec, out_shape=out_shape,
        scratch_shapes=[pltpu.VMEM(acc_shape if nk > 1 else (8, 128), F32)],
        compiler_params=_cp(("parallel", "parallel", "arbitrary")),
    )(*args)


def _pick(n, cands):
    for c in cands:
        if n % c == 0:
            return c
    raise ValueError(f"no tile for {n}")


def _mm_nn(a, b, out_dtype, name, add=None, tn_c=(1024, 768, 512), tk_c=(2048, 1536, 1024, 512, 256)):
    M, K = a.shape
    N = b.shape[1]
    tm = _pick(M, (1024, 512, 256))
    tn = _pick(N, tn_c)
    tk = _pick(K, tk_c)
    return _matmul(
        a, b, grid=(M // tm, N // tn, K // tk),
        a_spec=pl.BlockSpec((tm, tk), lambda i, j, k: (i, k)),
        b_spec=pl.BlockSpec((tk, tn), lambda i, j, k: (k, j)),
        out_shape=jax.ShapeDtypeStruct((M, N), out_dtype),
        out_spec=pl.BlockSpec((tm, tn), lambda i, j, k: (i, j)),
        dims=((1,), (0,)), acc_shape=(tm, tn), name=name, add=add,
        add_spec=pl.BlockSpec((tm, tn), lambda i, j, k: (i, j)))


def _mm_nt(a, b, out_dtype, name, tk_c=(2048, 1536, 1024, 768, 512), tn_c=(1024, 768, 512)):
    M, K = a.shape
    N = b.shape[0]
    tm = _pick(M, (1024, 512, 256))
    tn = _pick(N, tn_c)
    tk = _pick(K, tk_c)
    return _matmul(
        a, b, grid=(M // tm, N // tn, K // tk),
        a_spec=pl.BlockSpec((tm, tk), lambda i, j, k: (i, k)),
        b_spec=pl.BlockSpec((tn, tk), lambda i, j, k: (j, k)),
        out_shape=jax.ShapeDtypeStruct((M, N), out_dtype),
        out_spec=pl.BlockSpec((tm, tn), lambda i, j, k: (i, j)),
        dims=((1,), (1,)), acc_shape=(tm, tn), name=name)


def _mm_tn(a, b, out_dtype, name, tn_c=(1024, 768, 512)):
    K, M = a.shape
    N = b.shape[1]
    tm = _pick(M, (1024, 768, 512))
    tn = _pick(N, tn_c)
    tk = _pick(K, (1024, 512, 256))
    return _matmul(
        a, b, grid=(M // tm, N // tn, K // tk),
        a_spec=pl.BlockSpec((tk, tm), lambda i, j, k: (k, i)),
        b_spec=pl.BlockSpec((tk, tn), lambda i, j, k: (k, j)),
        out_shape=jax.ShapeDtypeStruct((M, N), out_dtype),
        out_spec=pl.BlockSpec((tm, tn), lambda i, j, k: (i, j)),
        dims=((0,), (0,)), acc_shape=(tm, tn), name=name)


def _mm_up(h, wup, name):
    M, K = h.shape
    ns = wup.shape[2]
    tm = _pick(M, (1024, 512, 256))
    tn = ns
    per = ns // tn
    return _matmul(
        h, wup, grid=(M // tm, N_DEV * per, 1),
        a_spec=pl.BlockSpec((tm, K), lambda i, j, k: (i, 0)),
        b_spec=pl.BlockSpec((None, K, tn), lambda i, j, k: (j // per, 0, j % per)),
        out_shape=jax.ShapeDtypeStruct((M, N_DEV * ns), BF16),
        out_spec=pl.BlockSpec((tm, tn), lambda i, j, k: (i, j)),
        dims=((1,), (0,)), acc_shape=(tm, tn), name=name)


def _mm_up_t(dup, wup, name):
    M = dup.shape[1]
    D, ns = wup.shape[1], wup.shape[2]
    tm = _pick(M, (1024, 512, 256))
    tn = D
    tk = ns
    return _matmul(
        dup, wup, grid=(M // tm, D // tn, N_DEV),
        a_spec=pl.BlockSpec((None, tm, tk), lambda i, j, k: (k // 4, i, k % 4)),
        b_spec=pl.BlockSpec((None, tn, tk), lambda i, j, k: (k, j, 0)),
        out_shape=jax.ShapeDtypeStruct((M, D), F32),
        out_spec=pl.BlockSpec((tm, tn), lambda i, j, k: (i, j)),
        dims=((1,), (1,)), acc_shape=(tm, tn), name=name)


def _mm_dup(ht, dup, ns, name):
    M, K = ht.shape
    tm = 1024
    tn = ns
    per = ns // tn
    half = 4 * per
    tk = _pick(K, (2048, 1024, 512, 256))
    return _matmul(
        ht, dup, grid=(M // tm, N_DEV * per, K // tk),
        a_spec=pl.BlockSpec((tm, tk), lambda i, j, k: (i, k)),
        b_spec=pl.BlockSpec((None, tk, tn), lambda i, j, k: (j // half, k, j % half)),
        out_shape=jax.ShapeDtypeStruct((N_DEV, M, ns), BF16),
        out_spec=pl.BlockSpec((None, tm, tn), lambda i, j, k: (j // per, i, j % per)),
        dims=((1,), (0,)), acc_shape=(tm, tn), name=name)


def _norm_fwd(x, w, name):
    T, D = x.shape
    tt = _tile(T)

    def body(x_ref, w_ref, h_ref, ht_ref):
        xv = x_ref[...]
        r = lax.rsqrt(jnp.mean(xv * xv, axis=1, keepdims=True) + EPS)
        hv = xv * r * w_ref[...]
        h_ref[...] = hv.astype(BF16)
        ht_ref[...] = hv.T.astype(BF16)

    return pl.pallas_call(
        body, name=name, grid=(T // tt,),
        in_specs=[pl.BlockSpec((tt, D), lambda t: (t, 0)), pl.BlockSpec((1, D), lambda t: (0, 0))],
        out_specs=[pl.BlockSpec((tt, D), lambda t: (t, 0)), pl.BlockSpec((D, tt), lambda t: (0, t))],
        out_shape=[jax.ShapeDtypeStruct((T, D), BF16), jax.ShapeDtypeStruct((D, T), BF16)],
        compiler_params=_cp(("parallel",)))(x, w)


def _norm_bwd(x, w, dh, dres, name):
    T, D = x.shape
    tt = _tile(T)

    def body(x_ref, w_ref, dh_ref, dres_ref, dx_ref, dxb_ref, dw_ref):
        t = pl.program_id(0)
        xv = x_ref[...]
        r = lax.rsqrt(jnp.mean(xv * xv, axis=1, keepdims=True) + EPS)
        xh = xv * r
        dh_v = dh_ref[...]
        dxh = dh_v * w_ref[...]
        dxv = dres_ref[...] + r * (dxh - xh * jnp.mean(dxh * xh, axis=1, keepdims=True))
        dx_ref[...] = dxv
        dxb_ref[...] = dxv.astype(BF16)
        part = jnp.sum(dh_v * xh, axis=0, keepdims=True)

        @pl.when(t == 0)
        def _():
            dw_ref[...] = part

        @pl.when(t > 0)
        def _():
            dw_ref[...] += part

    row = pl.BlockSpec((tt, D), lambda t: (t, 0))
    vec = pl.BlockSpec((1, D), lambda t: (0, 0))
    return pl.pallas_call(
        body, name=name, grid=(T // tt,), in_specs=[row, vec, row, row], out_specs=[row, row, vec],
        out_shape=[jax.ShapeDtypeStruct((T, D), F32), jax.ShapeDtypeStruct((T, D), BF16),
                   jax.ShapeDtypeStruct((1, D), F32)],
        compiler_params=_cp(("arbitrary",)))(x, w, dh, dres)


def _loss_head(x, w, target):
    T, D = x.shape
    tt = _tile(T)

    def body(x_ref, w_ref, t_ref, loss_ref, dx_ref, dxb_ref, dw_ref):
        t = pl.program_id(0)
        xv = x_ref[...]
        r = lax.rsqrt(jnp.mean(xv * xv, axis=1, keepdims=True) + EPS)
        xh = xv * r
        err = xh * w_ref[...] - t_ref[...]
        lp = 0.5 * jnp.sum(jnp.mean(err * err, axis=1, keepdims=True), axis=0, keepdims=True)
        dy = err * (1.0 / D)
        dxh = dy * w_ref[...]
        dxv = r * (dxh - xh * jnp.mean(dxh * xh, axis=1, keepdims=True))
        dx_ref[...] = dxv
        dxb_ref[...] = dxv.astype(BF16)
        part = jnp.sum(dy * xh, axis=0, keepdims=True)
        lpb = jnp.broadcast_to(lp, (1, 128))

        @pl.when(t == 0)
        def _():
            dw_ref[...] = part
            loss_ref[...] = lpb

        @pl.when(t > 0)
        def _():
            dw_ref[...] += part
            loss_ref[...] += lpb

    row = pl.BlockSpec((tt, D), lambda t: (t, 0))
    vec = pl.BlockSpec((1, D), lambda t: (0, 0))
    return pl.pallas_call(
        body, name="loss_head", grid=(T // tt,), in_specs=[row, vec, row],
        out_specs=[pl.BlockSpec((1, 128), lambda t: (0, 0)), row, row, vec],
        out_shape=[jax.ShapeDtypeStruct((1, 128), F32), jax.ShapeDtypeStruct((T, D), F32),
                   jax.ShapeDtypeStruct((T, D), BF16), jax.ShapeDtypeStruct((1, D), F32)],
        compiler_params=_cp(("arbitrary",)))(x, w, target)


CONV_TT = 4096
CONV_BWD_TT = 2048


def _conv_fwd(x, col0, C, w, b, cb, name):
    T = x.shape[0]
    K = w.shape[0]
    tt = min(T, CONV_TT)
    nt, nc, c0 = T // tt, C // cb, col0 // cb
    has_b = b is not None

    def body(*refs):
        if has_b:
            x_ref, halo_ref, w_ref, b_ref, y_ref = refs
        else:
            x_ref, halo_ref, w_ref, y_ref = refs
        t = pl.program_id(1)
        halo = jnp.where(t == 0, 0.0, halo_ref[...])
        xe = jnp.concatenate([halo, x_ref[...]], axis=0)
        acc = xe * w_ref[K - 1:K, :]
        for j in range(K - 1):
            acc = acc + pltpu.roll(xe, K - 1 - j, 0) * w_ref[j:j + 1, :]
        if has_b:
            acc = acc + b_ref[...]
        y_ref[...] = acc[8:, :]

    in_specs = [pl.BlockSpec((tt, cb), lambda j, t: (t, c0 + j)),
                pl.BlockSpec((8, cb), lambda j, t: (jnp.maximum(t * (tt // 8) - 1, 0), c0 + j)),
                pl.BlockSpec((K, cb), lambda j, t: (0, j))]
    args = [x, x, w]
    if has_b:
        in_specs.append(pl.BlockSpec((1, cb), lambda j, t: (0, j)))
        args.append(b)
    return pl.pallas_call(
        body, name=name, grid=(nc, nt), in_specs=in_specs,
        out_specs=pl.BlockSpec((tt, cb), lambda j, t: (t, j)),
        out_shape=jax.ShapeDtypeStruct((T, C), F32), compiler_params=_cp(("parallel", "parallel")))(*args)


def _conv_bwd(dy, x, col0, w, cb, name, into, want_db=False):
    T, C = dy.shape
    K = w.shape[0]
    tt = min(T, CONV_BWD_TT)
    nt, nc, c0 = T // tt, C // cb, col0 // cb

    def body(*refs):
        if want_db:
            dy_ref, dyn_ref, x_ref, xp_ref, w_ref, _, dx_ref, dw_ref, db_ref = refs
        else:
            dy_ref, dyn_ref, x_ref, xp_ref, w_ref, _, dx_ref, dw_ref = refs
        t = pl.program_id(1)
        dyv = dy_ref[...]
        nxt = jnp.where(t == nt - 1, 0.0, dyn_ref[...])
        dye = jnp.concatenate([dyv, nxt], axis=0)
        n = tt + 8
        acc = dye * w_ref[K - 1:K, :]
        for j in range(K - 1):
            acc = acc + pltpu.roll(dye, n - (K - 1 - j), 0) * w_ref[j:j + 1, :]
        dx_ref[...] = acc[:tt, :].astype(dx_ref.dtype)
        prev = jnp.where(t == 0, 0.0, xp_ref[...])
        xe = jnp.concatenate([prev, x_ref[...]], axis=0)

        @pl.when(t == 0)
        def _():
            dw_ref[...] = jnp.zeros_like(dw_ref)
            if want_db:
                db_ref[...] = jnp.zeros_like(db_ref)

        for j in range(K):
            sh = K - 1 - j
            xs = xe[8:, :] if sh == 0 else pltpu.roll(xe, sh, 0)[8:, :]
            dw_ref[j:j + 1, :] += jnp.sum(dyv * xs, axis=0, keepdims=True)
        if want_db:
            db_ref[...] += jnp.sum(dyv, axis=0, keepdims=True)

    h8 = tt // 8
    in_specs = [pl.BlockSpec((tt, cb), lambda j, t: (t, j)),
                pl.BlockSpec((8, cb), lambda j, t: (jnp.minimum((t + 1) * h8, T // 8 - 1), j)),
                pl.BlockSpec((tt, cb), lambda j, t: (t, c0 + j)),
                pl.BlockSpec((8, cb), lambda j, t: (jnp.maximum(t * h8 - 1, 0), c0 + j)),
                pl.BlockSpec((K, cb), lambda j, t: (0, j)), pl.BlockSpec(memory_space=pl.ANY)]
    out_specs = [pl.BlockSpec((tt, cb), lambda j, t: (t, c0 + j)), pl.BlockSpec((K, cb), lambda j, t: (0, j))]
    out_shape = [jax.ShapeDtypeStruct(into.shape, into.dtype), jax.ShapeDtypeStruct((K, C), F32)]
    if want_db:
        out_specs.append(pl.BlockSpec((1, cb), lambda j, t: (0, j)))
        out_shape.append(jax.ShapeDtypeStruct((1, C), F32))
    return pl.pallas_call(
        body, name=name, grid=(nc, nt), in_specs=in_specs, out_specs=out_specs, out_shape=out_shape,
        input_output_aliases={5: 0}, compiler_params=_cp(("parallel", "arbitrary")))(dy, dy, x, x, w, into)


def _pool_d(ue, g, pos, tt):
    win = POOL_WINDOWS[g]
    ug = ue[:, g * 128:(g + 1) * 128]
    s = ug
    k = 1
    while k < win:
        s = s + pltpu.roll(s, k, 0)
        k *= 2
    cnt = jnp.minimum(pos + 1, win).astype(F32)
    return s[16:, :] / cnt - ug[16:, :]


POOL_TT = 2048


def _pool_fwd(proj, pw, pb, ps):
    T = proj.shape[0]
    tt = min(T, POOL_TT)

    def body(u_ref, halo_ref, w_ref, b_ref, s_ref, y_ref):
        t = pl.program_id(0)
        halo = jnp.where(t == 0, 0.0, halo_ref[...])
        ue = jnp.concatenate([halo, u_ref[...]], axis=0)
        pos = t * tt + lax.broadcasted_iota(jnp.int32, (tt, 1), 0)
        for g in range(4):
            sl = slice(g * 128, (g + 1) * 128)
            d = _pool_d(ue, g, pos, tt)
            yg = _mm(d, w_ref[g]) + b_ref[:, sl]
            y_ref[:, sl] = (yg * s_ref[:, sl]).astype(BF16)

    vec = pl.BlockSpec((1, POOL_W), lambda t: (0, 0))
    return pl.pallas_call(
        body, name="pool_fwd", grid=(T // tt,),
        in_specs=[pl.BlockSpec((tt, POOL_W), lambda t: (t, 0)),
                  pl.BlockSpec((16, POOL_W), lambda t: (jnp.maximum(t * (tt // 16) - 1, 0), 0)),
                  pl.BlockSpec((4, 128, 128), lambda t: (0, 0, 0)), vec, vec],
        out_specs=pl.BlockSpec((tt, POOL_W), lambda t: (t, 0)),
        out_shape=jax.ShapeDtypeStruct((T, D_MODEL), BF16), compiler_params=_cp(("parallel",)))(
            proj, proj, pw, pb, ps)


def _pool_bwd(dmixed, proj, pw, pb, ps):
    T = proj.shape[0]
    tt = min(T, POOL_TT)
    nt = T // tt

    def body(dy_ref, dyn_ref, u_ref, halo_ref, w_ref, b_ref, s_ref, du_ref, dw_ref, db_ref, ds_ref):
        t = pl.program_id(0)
        halo = jnp.where(t == 0, 0.0, halo_ref[...])
        ue = jnp.concatenate([halo, u_ref[...]], axis=0)
        dyv = dy_ref[...]
        nxt = jnp.where(t == nt - 1, 0.0, dyn_ref[...])
        dye = jnp.concatenate([dyv, nxt], axis=0)
        n = tt + 16
        pos = t * tt + lax.broadcasted_iota(jnp.int32, (tt, 1), 0)
        pos_e = t * tt + lax.broadcasted_iota(jnp.int32, (n, 1), 0)

        @pl.when(t == 0)
        def _():
            dw_ref[...] = jnp.zeros_like(dw_ref)
            db_ref[...] = jnp.zeros_like(db_ref)
            ds_ref[...] = jnp.zeros_like(ds_ref)

        for g in range(4):
            win = POOL_WINDOWS[g]
            sl = slice(g * 128, (g + 1) * 128)
            d = _pool_d(ue, g, pos, tt)
            wg = w_ref[g]
            ypre = _mm(d, wg) + b_ref[:, sl]
            sc = s_ref[:, sl]
            ds_ref[:, sl] += jnp.sum(dyv[:, sl] * ypre, axis=0, keepdims=True)
            dyp_e = dye[:, sl] * sc
            dyp = dyp_e[:tt, :]
            db_ref[:, sl] += jnp.sum(dyp, axis=0, keepdims=True)
            dw_ref[g] += _mm(d, dyp, 0, 0)
            dd_e = _mm(dyp_e, wg, 1, 1)
            cnt_e = jnp.minimum(pos_e + 1, win).astype(F32)
            s = dd_e / cnt_e
            k = 1
            while k < win:
                s = s + pltpu.roll(s, n - k, 0)
                k *= 2
            du_ref[:, sl] = (s[:tt, :] - dd_e[:tt, :]).astype(BF16)

    vec = pl.BlockSpec((1, POOL_W), lambda t: (0, 0))
    h16 = tt // 16
    return pl.pallas_call(
        body, name="pool_bwd", grid=(nt,),
        in_specs=[pl.BlockSpec((tt, POOL_W), lambda t: (t, 0)),
                  pl.BlockSpec((16, POOL_W), lambda t: (jnp.minimum((t + 1) * h16, T // 16 - 1), 0)),
                  pl.BlockSpec((tt, POOL_W), lambda t: (t, 0)),
                  pl.BlockSpec((16, POOL_W), lambda t: (jnp.maximum(t * h16 - 1, 0), 0)),
                  pl.BlockSpec((4, 128, 128), lambda t: (0, 0, 0)), vec, vec],
        out_specs=[pl.BlockSpec((tt, POOL_W), lambda t: (t, 0)),
                   pl.BlockSpec((4, 128, 128), lambda t: (0, 0, 0)), vec, vec],
        out_shape=[jax.ShapeDtypeStruct((T, PCOLS), BF16), jax.ShapeDtypeStruct((4, 128, 128), F32),
                   jax.ShapeDtypeStruct((1, POOL_W), F32), jax.ShapeDtypeStruct((1, POOL_W), F32)],
        compiler_params=_cp(("arbitrary",)))(dmixed, dmixed, proj, proj, pw, pb, ps)


def _gdn_pre_fwd(cpre, proj, alog, dtb):
    T = cpre.shape[0]
    tt = _tile(T)

    def body(c_ref, ab_ref, alog_ref, dtb_ref, qkv_ref, bb_ref, gb_ref):
        for p in range(3):
            for h in range(HEADS):
                cc = c_ref[:, (p * HEADS + h) * HD:(p * HEADS + h + 1) * HD]
                s = cc * _sigmoid(cc)
                if p < 2:
                    s = s * lax.rsqrt(jnp.sum(s * s, axis=1, keepdims=True) + EPS)
                if p == 0:
                    s = s * (HD ** -0.5)
                qkv_ref[p, h] = s
        ab = ab_ref[...]
        g = -jnp.exp(alog_ref[...]) * _softplus(ab + dtb_ref[...])
        r64 = lax.broadcasted_iota(jnp.int32, (tt, 1), 0) & (CHUNK - 1)
        k = 1
        while k < CHUNK:
            g = g + jnp.where(r64 >= k, pltpu.roll(g, k, 0), 0.0)
            k *= 2
        sb = _sigmoid(ab)
        for h in range(HEADS):
            gb_ref[h] = jnp.broadcast_to(g[:, h:h + 1], (tt, HD))
            bb_ref[h] = jnp.broadcast_to(sb[:, HEADS + h:HEADS + h + 1], (tt, HD))

    vec = pl.BlockSpec((1, 128), lambda t: (0, 0))
    hb = pl.BlockSpec((HEADS, tt, HD), lambda t: (0, t, 0))
    return pl.pallas_call(
        body, name="gdn_pre_fwd", grid=(T // tt,),
        in_specs=[pl.BlockSpec((tt, 3 * GDN_W), lambda t: (t, 0)),
                  pl.BlockSpec((tt, 128), lambda t: (t, C_AB // 128)), vec, vec],
        out_specs=[pl.BlockSpec((3, HEADS, tt, HD), lambda t: (0, 0, t, 0)), hb, hb],
        out_shape=[jax.ShapeDtypeStruct((3, HEADS, T, HD), F32), jax.ShapeDtypeStruct((HEADS, T, HD), F32),
                   jax.ShapeDtypeStruct((HEADS, T, HD), F32)],
        compiler_params=_cp(("parallel",)))(cpre, proj, alog, dtb)


def _gdn_pre_bwd(dq, dk, dv, cpre, dbb, dgb, proj, alog, dtb, dproj):
    T = cpre.shape[0]
    tt = _tile(T)

    def body(dq_ref, dk_ref, dv_ref, c_ref, dbb_ref, dgb_ref, ab_ref, alog_ref, dtb_ref, _,
             dc_ref, dab_ref, dalog_ref, ddtb_ref):
        t = pl.program_id(0)
        srcs = (dq_ref, dk_ref, dv_ref)
        for p in range(3):
            for h in range(HEADS):
                sl = slice((p * HEADS + h) * HD, (p * HEADS + h + 1) * HD)
                cc = c_ref[:, sl]
                sg = _sigmoid(cc)
                s = cc * sg
                dyv = srcs[p][h]
                if p < 2:
                    r = lax.rsqrt(jnp.sum(s * s, axis=1, keepdims=True) + EPS)
                    y = s * r
                    if p == 0:
                        dyv = dyv * (HD ** -0.5)
                    ds = r * (dyv - y * jnp.sum(dyv * y, axis=1, keepdims=True))
                else:
                    ds = dyv
                dc_ref[:, sl] = ds * sg * (1.0 + cc * (1.0 - sg))
        lane = lax.broadcasted_iota(jnp.int32, (tt, 128), 1)
        dg = jnp.zeros((tt, 128), F32)
        dbeta = jnp.zeros((tt, 128), F32)
        for h in range(HEADS):
            dg = jnp.where(lane == h, dgb_ref[h], dg)
            dbeta = jnp.where(lane == HEADS + h, dbb_ref[h], dbeta)
        r64 = lax.broadcasted_iota(jnp.int32, (tt, 1), 0) & (CHUNK - 1)
        k = 1
        while k < CHUNK:
            dg = dg + jnp.where(r64 < CHUNK - k, pltpu.roll(dg, tt - k, 0), 0.0)
            k *= 2
        ab = ab_ref[...]
        e = jnp.exp(alog_ref[...])
        xx = ab + dtb_ref[...]
        g = -e * _softplus(xx)
        da = jnp.where(lane < HEADS, dg * (-e) * _sigmoid(xx), 0.0)
        pa = jnp.sum(jnp.where(lane < HEADS, dg * g, 0.0), axis=0, keepdims=True)
        pd = jnp.sum(da, axis=0, keepdims=True)

        @pl.when(t == 0)
        def _():
            dalog_ref[...] = pa
            ddtb_ref[...] = pd

        @pl.when(t > 0)
        def _():
            dalog_ref[...] += pa
            ddtb_ref[...] += pd

        sb = _sigmoid(ab)
        dab_ref[:, :128] = jnp.where(lane < HEADS, da, dbeta * sb * (1.0 - sb)).astype(BF16)
        dab_ref[:, 128:] = jnp.zeros((tt, 128), BF16)

    vec = pl.BlockSpec((1, 128), lambda t: (0, 0))
    hb = pl.BlockSpec((HEADS, tt, HD), lambda t: (0, t, 0))
    return pl.pallas_call(
        body, name="gdn_pre_bwd", grid=(T // tt,),
        in_specs=[hb, hb, hb, pl.BlockSpec((tt, 3 * GDN_W), lambda t: (t, 0)), hb, hb,
                  pl.BlockSpec((tt, 128), lambda t: (t, C_AB // 128)), vec, vec, pl.BlockSpec(memory_space=pl.ANY)],
        out_specs=[pl.BlockSpec((tt, 3 * GDN_W), lambda t: (t, 0)),
                   pl.BlockSpec((tt, 256), lambda t: (t, C_AB // 256)), vec, vec],
        out_shape=[jax.ShapeDtypeStruct((T, 3 * GDN_W), F32), jax.ShapeDtypeStruct(dproj.shape, dproj.dtype),
                   jax.ShapeDtypeStruct((1, 128), F32), jax.ShapeDtypeStruct((1, 128), F32)],
        input_output_aliases={9: 1},
        compiler_params=_cp(("arbitrary",)))(dq, dk, dv, cpre, dbb, dgb, proj, alog, dtb, dproj)


def _split2(x):
    hi = x.astype(BF16)
    return hi, (x - hi.astype(F32)).astype(BF16)


def _bmm3s(a2, b2, ca=2, cb=1):
    def f(x, y):
        return lax.dot_general(x, y, (((ca,), (cb,)), ((0,), (0,))), preferred_element_type=F32)

    return f(a2[0], b2[0]) + (f(a2[0], b2[1]) + f(a2[1], b2[0]))


def _bmm3(a, b, ca=2, cb=1):
    return _bmm3s(_split2(a), _split2(b), ca, cb)


def _tri_inv(a):
    nb = a.shape[0]
    ri = lax.broadcasted_iota(jnp.int32, (nb, CHUNK, CHUNK), 1)
    ci = lax.broadcasted_iota(jnp.int32, (nb, CHUNK, CHUNK), 2)
    n = -a
    p = jnp.where(ri == ci, 1.0, 0.0) + n
    n2 = _split2(n)
    for _ in range(5):
        n2 = _split2(_bmm3s(n2, n2))
        p = p + _bmm3s(_split2(p), n2)
    return p


def _gdn_chunk_common(q, k, v, bb3, gb3, tm_saved=None):
    nb = q.shape[0]
    need_t = tm_saved is not None
    beta = bb3[:, :, 0:1]
    gcol = gb3[:, :, 0:1]
    bcol = bb3[:, :, :CHUNK]
    gcm = gb3[:, :, :CHUNK]
    oh = jnp.where(lax.broadcasted_iota(jnp.int32, (nb, CHUNK, HD), 2) == 0, 1.0, 0.0)
    grow = _bmm(oh, gb3, 2, 2, HI, False)
    ri = lax.broadcasted_iota(jnp.int32, (nb, CHUNK, CHUNK), 1)
    ci = lax.broadcasted_iota(jnp.int32, (nb, CHUNK, CHUNK), 2)
    tril, stl = ri >= ci, ri > ci
    dg = gcm - grow
    dec = jnp.where(tril, jnp.exp(jnp.where(tril, dg, 0.0)), 0.0)
    kk = _bmm(k, k, 2, 2)
    qk = _bmm(q, k, 2, 2)
    tm = tm_saved if need_t else _tri_inv(jnp.where(stl, bcol * kk * dec, 0.0))
    gam = jnp.exp(gcol)
    glast = gb3[:, CHUNK - 1:CHUNK, 0:1]
    egl = jnp.exp(glast)
    rw = k * (beta * gam)
    ru = v * beta
    wu = _bmm3(tm, jnp.concatenate([rw, ru], axis=2), 2, 1)
    kdf = jnp.exp(glast - gcol)
    out = dict(beta=beta, bcol=bcol, tril=tril, stl=stl, dec=dec, kk=kk, qk=qk, tm=tm, gam=gam, egl=egl,
               rw=rw, wu=wu, at=qk * dec, qd=q * gam, kdf=kdf, kd=k * kdf)
    if need_t:
        brow = _bmm(oh, bb3, 2, 2, HI, False)
        triu, stu = ri <= ci, ri < ci
        dect = jnp.where(triu, jnp.exp(jnp.where(triu, -dg, 0.0)), 0.0)
        qkt = _bmm(k, q, 2, 2)
        eye = jnp.where(ri == ci, 1.0, 0.0)
        out.update(brow=brow, triu=triu, stu=stu, dect=dect, qkt=qkt, tmt=_bmm3(eye, tm, 2, 2),
                   att=qkt * dect)
    return out


def _gdn_rows(T, fwd=False):
    return min(T, 1024 if fwd else 512)


HP = 2


def _gdn_fwd(qkv, bb, gb, proj, nw, mixed):
    T = proj.shape[0]
    R = _gdn_rows(T, fwd=True)
    nb = R // CHUNK
    B = HP * nb

    def body(q_ref, k_ref, v_ref, bb_ref, gb_ref, z_ref, nw_ref, _, y_ref, st_ref, tm_ref,
             s_ref, w_s, u_s, at_s, qd_s, kd_s):
        t = pl.program_id(1)

        @pl.when(t == 0)
        def _():
            s_ref[...] = jnp.zeros_like(s_ref)

        sh = (B, CHUNK, HD)
        q, k, v = q_ref[...].reshape(sh), k_ref[...].reshape(sh), v_ref[...].reshape(sh)
        c = _gdn_chunk_common(q, k, v, bb_ref[...].reshape(sh), gb_ref[...].reshape(sh))
        tm_ref[...] = c["tm"].reshape(HP, nb, CHUNK, CHUNK)
        w_s[...] = c["wu"][:, :, :HD]
        u_s[...] = c["wu"][:, :, HD:]
        at_s[...] = c["at"]
        qd_s[...] = c["qd"]
        kd_s[...] = c["kd"]
        egl = c["egl"]
        nwv = nw_ref[...]
        for n in range(nb):
            rows = slice(n * CHUNK, (n + 1) * CHUNK)
            for hh in range(HP):
                b = hh * nb + n
                cols = slice(hh * HD, (hh + 1) * HD)
                s = s_ref[hh]
                st_ref[hh, n] = s
                vn = u_s[b] - _mm(w_s[b], s)
                o = _mm(qd_s[b], s) + _mm(at_s[b], vn)
                s_ref[hh] = s * egl[b] + _mm(kd_s[b], vn, 0, 0)
                zz = z_ref[rows, cols]
                on = o * lax.rsqrt(jnp.mean(o * o, axis=1, keepdims=True) + EPS)
                y_ref[rows, cols] = (on * nwv * (zz * _sigmoid(zz))).astype(BF16)

    def hm(p):
        return pl.BlockSpec((None, HP, R, HD), lambda h, t: (p, h, t, 0))

    hb = pl.BlockSpec((HP, R, HD), lambda h, t: (h, t, 0))
    cs = pltpu.VMEM((B, CHUNK, HD), F32)
    return pl.pallas_call(
        body, name="gdn_fwd", grid=(HEADS // HP, T // R),
        in_specs=[hm(0), hm(1), hm(2), hb, hb,
                  pl.BlockSpec((R, HP * HD), lambda h, t: (t, C_Z // (HP * HD) + h)),
                  pl.BlockSpec((1, HD), lambda h, t: (0, 0)), pl.BlockSpec(memory_space=pl.ANY)],
        out_specs=[pl.BlockSpec((R, HP * HD), lambda h, t: (t, M_GDN // (HP * HD) + h)),
                   pl.BlockSpec((HP, nb, HD, HD), lambda h, t: (h, t, 0, 0)),
                   pl.BlockSpec((HP, nb, CHUNK, CHUNK), lambda h, t: (h, t, 0, 0))],
        out_shape=[jax.ShapeDtypeStruct(mixed.shape, mixed.dtype),
                   jax.ShapeDtypeStruct((HEADS, T // CHUNK, HD, HD), F32),
                   jax.ShapeDtypeStruct((HEADS, T // CHUNK, CHUNK, CHUNK), F32)],
        scratch_shapes=[pltpu.VMEM((HP, HD, HD), F32), cs, cs, pltpu.VMEM((B, CHUNK, CHUNK), F32), cs, cs],
        input_output_aliases={7: 0},
        compiler_params=_cp(("parallel", "arbitrary")))(qkv, qkv, qkv, bb, gb, proj, nw, mixed)


def _gdn_bwd(qkv, bb, gb, proj, nw, states, tms, dmixed, dproj):
    T = proj.shape[0]
    R = _gdn_rows(T)
    nb = R // CHUNK
    ntb = T // R
    B = HP * nb

    def by_head(ref):
        return jnp.concatenate([ref[:, hh * HD:(hh + 1) * HD].reshape(nb, CHUNK, HD) for hh in range(HP)], axis=0)

    def body(q_ref, k_ref, v_ref, bb_ref, gb_ref, z_ref, nw_ref, st_ref, tm_ref, dy_ref, _,
             dq_ref, dk_ref, dv_ref, dbb_ref, dgb_ref, dz_ref, dnw_ref,
             ds_ref, att_s, do_s, kd_s, vn_s, qd_s, w_s, dvn_s, dkd_s, dgl_s):
        hp = pl.program_id(0)
        t = pl.program_id(1)

        @pl.when(t == 0)
        def _():
            ds_ref[...] = jnp.zeros_like(ds_ref)

        @pl.when((t == 0) & (hp == 0))
        def _():
            dnw_ref[...] = jnp.zeros_like(dnw_ref)

        sh = (B, CHUNK, HD)
        q, k, v = q_ref[...].reshape(sh), k_ref[...].reshape(sh), v_ref[...].reshape(sh)
        c = _gdn_chunk_common(q, k, v, bb_ref[...].reshape(sh), gb_ref[...].reshape(sh),
                              tm_ref[...].reshape(B, CHUNK, CHUNK))
        w, u = c["wu"][:, :, :HD], c["wu"][:, :, HD:]
        sall = st_ref[...].reshape(B, HD, HD)
        vn = u - _bmm(w, sall, 2, 1)
        o = _bmm(c["qd"], sall, 2, 1) + _bmm(c["at"], vn, 2, 1)
        z = by_head(z_ref)
        dy = by_head(dy_ref)
        nwv = nw_ref[...].reshape(1, 1, HD)
        rs = lax.rsqrt(jnp.mean(o * o, axis=2, keepdims=True) + EPS)
        on = o * rs
        sg = _sigmoid(z)
        sz = z * sg
        dnw_ref[...] += jnp.sum(jnp.sum(dy * on * sz, axis=0), axis=0, keepdims=True)
        dz3 = dy * on * nwv * (sg * (1.0 + z * (1.0 - sg)))
        for hh in range(HP):
            dz_ref[:, hh * HD:(hh + 1) * HD] = dz3[hh * nb:(hh + 1) * nb].reshape(R, HD).astype(BF16)
        don = dy * nwv * sz
        do = rs * (don - on * jnp.mean(don * on, axis=2, keepdims=True))
        dqd = _bmm(do, sall, 2, 2)
        dat = jnp.where(c["tril"], _bmm(do, vn, 2, 2), 0.0)
        datt = jnp.where(c["triu"], _bmm(vn, do, 2, 2), 0.0)
        att_s[...] = c["att"]
        do_s[...] = do
        kd_s[...] = c["kd"]
        vn_s[...] = vn
        qd_s[...] = c["qd"]
        w_s[...] = w
        egl = c["egl"]
        for n in reversed(range(nb)):
            for hh in range(HP):
                b = hh * nb + n
                dso = ds_ref[hh]
                dvn_n = _mm(att_s[b], do_s[b]) + _mm(kd_s[b], dso)
                dkd_s[b] = _mm(vn_s[b], dso, 1, 1)
                dgl = egl[b] * jnp.sum(jnp.sum(st_ref[hh, n] * dso, axis=1, keepdims=True), axis=0, keepdims=True)
                dgl_s[b] = jnp.broadcast_to(dgl, (8, HD))
                ds_ref[hh] = egl[b] * dso + _mm(qd_s[b], do_s[b], 0, 0) - _mm(w_s[b], dvn_n, 0, 0)
                dvn_s[b] = dvn_n
        dvn = dvn_s[...]
        dkd = dkd_s[...]
        dgl = dgl_s[...][:, 0:1, 0:1]
        dw = -_bmm(dvn, sall, 2, 2)
        dr = _bmm3(c["tmt"], jnp.concatenate([dw, dvn], axis=2), 2, 1)
        drw, dru = dr[:, :, :HD], dr[:, :, HD:]
        wu = c["wu"]
        dr2, wu2 = _split2(dr), _split2(wu)
        da = -jnp.where(c["stl"], _bmm3s(dr2, wu2, 2, 2), 0.0)
        da_t = -jnp.where(c["stu"], _bmm3s(wu2, dr2, 2, 2), 0.0)
        beta, gam, dec, dect, kk = c["beta"], c["gam"], c["dec"], c["dect"], c["kk"]
        bcol, brow = c["bcol"], c["brow"]
        dbeta = (jnp.sum(da * kk * dec, axis=2, keepdims=True)
                 + jnp.sum(drw * k * gam + dru * v, axis=2, keepdims=True))
        dkk = bcol * da * dec
        dkk_t = brow * da_t * dect
        e = (bcol * da * kk + dat * c["qk"]) * dec
        e_t = (brow * da_t * kk + datt * c["qkt"]) * dect
        kd = c["kd"]
        dq_ref[...] = (_bmm(dat * dec, k, 2, 1) + dqd * gam).reshape(HP, R, HD)
        dk_ref[...] = (_bmm(datt * dect, q, 2, 1) + _bmm(dkk + dkk_t, k, 2, 1) + dkd * c["kdf"]
                       + drw * (beta * gam)).reshape(HP, R, HD)
        dv_ref[...] = (dru * beta).reshape(HP, R, HD)
        skd = jnp.sum(dkd * kd, axis=2, keepdims=True)
        dgc = (jnp.sum(e, axis=2, keepdims=True) - jnp.sum(e_t, axis=2, keepdims=True)
               + jnp.sum(drw * c["rw"] + dqd * c["qd"], axis=2, keepdims=True) - skd)
        tot = jnp.sum(skd, axis=1, keepdims=True) + dgl
        rowi = lax.broadcasted_iota(jnp.int32, (B, CHUNK, 1), 1)
        dgc = dgc + jnp.where(rowi == CHUNK - 1, tot, 0.0)
        dbb_ref[...] = jnp.broadcast_to(dbeta, sh).reshape(HP, R, HD)
        dgb_ref[...] = jnp.broadcast_to(dgc, sh).reshape(HP, R, HD)

    def rt(t):
        return ntb - 1 - t

    def hm(p):
        return pl.BlockSpec((None, HP, R, HD), lambda h, t: (p, h, rt(t), 0))

    hb = pl.BlockSpec((HP, R, HD), lambda h, t: (h, rt(t), 0))
    cs = pltpu.VMEM((B, CHUNK, HD), F32)
    ob = jax.ShapeDtypeStruct((HEADS, T, HD), F32)
    return pl.pallas_call(
        body, name="gdn_bwd", grid=(HEADS // HP, ntb),
        in_specs=[hm(0), hm(1), hm(2), hb, hb,
                  pl.BlockSpec((R, HP * HD), lambda h, t: (rt(t), C_Z // (HP * HD) + h)),
                  pl.BlockSpec((1, HD), lambda h, t: (0, 0)),
                  pl.BlockSpec((HP, nb, HD, HD), lambda h, t: (h, rt(t), 0, 0)),
                  pl.BlockSpec((HP, nb, CHUNK, CHUNK), lambda h, t: (h, rt(t), 0, 0)),
                  pl.BlockSpec((R, HP * HD), lambda h, t: (rt(t), M_GDN // (HP * HD) + h)),
                  pl.BlockSpec(memory_space=pl.ANY)],
        out_specs=[hb, hb, hb, hb, hb, pl.BlockSpec((R, HP * HD), lambda h, t: (rt(t), C_Z // (HP * HD) + h)),
                   pl.BlockSpec((1, HD), lambda h, t: (0, 0))],
        out_shape=[ob, ob, ob, ob, ob, jax.ShapeDtypeStruct(dproj.shape, dproj.dtype),
                   jax.ShapeDtypeStruct((1, HD), F32)],
        scratch_shapes=[pltpu.VMEM((HP, HD, HD), F32), pltpu.VMEM((B, CHUNK, CHUNK), F32), cs, cs, cs, cs, cs, cs,
                        cs, pltpu.VMEM((B, 8, HD), F32)],
        input_output_aliases={10: 5},
        compiler_params=_cp(("arbitrary", "arbitrary")))(qkv, qkv, qkv, bb, gb, proj, nw, states, tms, dmixed,
                                                         dproj)


LRU_TT = 1024


def _lru_gates(xc, wa, ba, wx, bx, lam, gpos):
    xb = xc.astype(BF16)
    r = _sigmoid(_mm(xb, wa) + ba)
    i = _sigmoid(_mm(xb, wx) + bx)
    sp = _softplus(-lam)
    log_a = -LRU_C * r * sp
    a = jnp.exp(log_a)
    mult = jnp.where(gpos == 0, 1.0, jnp.sqrt(-_expm1(2.0 * log_a)))
    return r, i, sp, a, mult


def _lru_fwd(xc, proj, wa, ba, wx, bx, lam, mixed):
    T = xc.shape[0]
    tt = min(T, LRU_TT)

    def body(xc_ref, gr_ref, wa_ref, ba_ref, wx_ref, bx_ref, lam_ref, _, y_ref, h_ref, carry_ref):
        t = pl.program_id(1)

        @pl.when(t == 0)
        def _():
            carry_ref[...] = jnp.zeros_like(carry_ref)

        row = lax.broadcasted_iota(jnp.int32, (tt, 1), 0)
        xcv = xc_ref[...]
        r, i, sp, a, mult = _lru_gates(xcv, wa_ref[...], ba_ref[...], wx_ref[...], bx_ref[...], lam_ref[...],
                                       t * tt + row)
        av, bv = a, mult * i * xcv
        k = 1
        while k < tt:
            a_s = jnp.where(row >= k, pltpu.roll(av, k, 0), 1.0)
            b_s = jnp.where(row >= k, pltpu.roll(bv, k, 0), 0.0)
            bv = bv + av * b_s
            av = av * a_s
            k *= 2
        h = bv + av * carry_ref[0:1, :]
        carry_ref[...] = jnp.broadcast_to(h[tt - 1:tt, :], (8, 128))
        h_ref[...] = h
        y_ref[...] = (h * _gelu(gr_ref[...])).astype(BF16)

    blk = pl.BlockSpec((tt, 128), lambda j, t: (t, j))
    vec = pl.BlockSpec((1, 128), lambda j, t: (0, j))
    mat = pl.BlockSpec((None, 128, 128), lambda j, t: (j, 0, 0))
    return pl.pallas_call(
        body, name="lru_fwd", grid=(LRU_W // 128, T // tt),
        in_specs=[blk, pl.BlockSpec((tt, 128), lambda j, t: (t, C_GR // 128 + j)), mat, vec, mat, vec, vec,
                  pl.BlockSpec(memory_space=pl.ANY)],
        out_specs=[pl.BlockSpec((tt, 128), lambda j, t: (t, M_LRU // 128 + j)), blk],
        out_shape=[jax.ShapeDtypeStruct(mixed.shape, mixed.dtype), jax.ShapeDtypeStruct((T, LRU_W), F32)],
        scratch_shapes=[pltpu.VMEM((8, 128), F32)], input_output_aliases={7: 0},
        compiler_params=_cp(("parallel", "arbitrary")))(xc, proj, wa, ba, wx, bx, lam, mixed)


def _lru_bwd(dmixed, xc, proj, hst, wa, ba, wx, bx, lam, dproj):
    T = xc.shape[0]
    tt = min(T, LRU_TT)
    nt = T // tt

    def body(dy_ref, xc_ref, gr_ref, h_ref, hp_ref, wa_ref, ba_ref, wx_ref, bx_ref, lam_ref, _,
             dxc_ref, dgr_ref, dwa_ref, dwx_ref, dba_ref, dbx_ref, dlam_ref, lc_ref, ac_ref):
        t = pl.program_id(1)
        tr = nt - 1 - t

        @pl.when(t == 0)
        def _():
            lc_ref[...] = jnp.zeros_like(lc_ref)
            ac_ref[...] = jnp.zeros_like(ac_ref)
            dwa_ref[...] = jnp.zeros_like(dwa_ref)
            dwx_ref[...] = jnp.zeros_like(dwx_ref)
            dba_ref[...] = jnp.zeros_like(dba_ref)
            dbx_ref[...] = jnp.zeros_like(dbx_ref)
            dlam_ref[...] = jnp.zeros_like(dlam_ref)

        row = lax.broadcasted_iota(jnp.int32, (tt, 1), 0)
        gpos = tr * tt + row
        xcv = xc_ref[...]
        wav, wxv, lamv = wa_ref[...], wx_ref[...], lam_ref[...]
        r, i, sp, a, mult = _lru_gates(xcv, wav, ba_ref[...], wxv, bx_ref[...], lamv, gpos)
        h = h_ref[...]
        dy = dy_ref[...]
        gg, dgg = _gelu_and_grad(gr_ref[...])
        dgr_ref[...] = (dy * h * dgg).astype(BF16)
        bv = dy * gg
        cv = jnp.where(row < tt - 1, pltpu.roll(a, tt - 1, 0), ac_ref[0:1, :])
        k = 1
        while k < tt:
            c_s = jnp.where(row < tt - k, pltpu.roll(cv, tt - k, 0), 1.0)
            b_s = jnp.where(row < tt - k, pltpu.roll(bv, tt - k, 0), 0.0)
            bv = bv + cv * b_s
            cv = cv * c_s
            k *= 2
        lm = bv + cv * lc_ref[0:1, :]
        lc_ref[...] = jnp.broadcast_to(lm[0:1, :], (8, 128))
        ac_ref[...] = jnp.broadcast_to(a[0:1, :], (8, 128))
        hp = jnp.where(tr == 0, 0.0, hp_ref[...])
        hs = pltpu.roll(jnp.concatenate([hp, h], axis=0), 1, 0)[8:, :]
        da = lm * hs
        dmult = lm * i * xcv
        di = lm * mult * xcv
        dxc = lm * mult * i
        dlog_a = a * da - jnp.where(gpos == 0, 0.0, dmult * a * a / mult)
        dr = dlog_a * (-LRU_C * sp)
        dsp = jnp.sum(dlog_a * (-LRU_C * r), axis=0, keepdims=True)
        dlam_ref[...] += dsp * (-_sigmoid(-lamv))
        dpr = dr * r * (1.0 - r)
        dpi = di * i * (1.0 - i)
        dba_ref[...] += jnp.sum(dpr, axis=0, keepdims=True)
        dbx_ref[...] += jnp.sum(dpi, axis=0, keepdims=True)
        dwa_ref[...] += _mm(xcv, dpr, 0, 0)
        dwx_ref[...] += _mm(xcv, dpi, 0, 0)
        dxc_ref[...] = dxc + _mm(dpr, wav, 1, 1) + _mm(dpi, wxv, 1, 1)

    def rt(t):
        return nt - 1 - t

    blk = pl.BlockSpec((tt, 128), lambda j, t: (rt(t), j))
    vec = pl.BlockSpec((1, 128), lambda j, t: (0, j))
    mat = pl.BlockSpec((None, 128, 128), lambda j, t: (j, 0, 0))
    h8 = tt // 8
    mshape = jax.ShapeDtypeStruct((LRU_W // 128, 128, 128), F32)
    vshape = jax.ShapeDtypeStruct((1, LRU_W), F32)
    return pl.pallas_call(
        body, name="lru_bwd", grid=(LRU_W // 128, nt),
        in_specs=[pl.BlockSpec((tt, 128), lambda j, t: (rt(t), M_LRU // 128 + j)), blk,
                  pl.BlockSpec((tt, 128), lambda j, t: (rt(t), C_GR // 128 + j)), blk,
                  pl.BlockSpec((8, 128), lambda j, t: (jnp.maximum(rt(t) * h8 - 1, 0), j)),
                  mat, vec, mat, vec, vec, pl.BlockSpec(memory_space=pl.ANY)],
        out_specs=[blk, pl.BlockSpec((tt, 128), lambda j, t: (rt(t), C_GR // 128 + j)), mat, mat, vec, vec, vec],
        out_shape=[jax.ShapeDtypeStruct((T, LRU_W), F32), jax.ShapeDtypeStruct(dproj.shape, dproj.dtype),
                   mshape, mshape, vshape, vshape, vshape],
        scratch_shapes=[pltpu.VMEM((8, 128), F32), pltpu.VMEM((8, 128), F32)], input_output_aliases={10: 1},
        compiler_params=_cp(("parallel", "arbitrary")))(dmixed, xc, proj, hst, hst, wa, ba, wx, bx, lam, dproj)


FFN_CB = 512
FFN_K = 3


def _ffn_conv(ge, w_ref):
    acc = ge * w_ref[FFN_K - 1:FFN_K, :]
    for j in range(FFN_K - 1):
        acc = acc + pltpu.roll(ge, FFN_K - 1 - j, 0) * w_ref[j:j + 1, :]
    return acc


FFN_HALO = 16
FFN_TT = 2048


def _ffn_gate_fwd(up, w):
    T = up.shape[0]
    tt = min(T, FFN_TT)
    cb = FFN_CB
    nc = D_FF // cb
    hh = tt // FFN_HALO

    def body(g_ref, gp_ref, v_ref, w_ref, o_ref, ot_ref):
        t = pl.program_id(1)
        prev = jnp.where(t == 0, 0.0, gp_ref[...].astype(F32))
        ge = jnp.concatenate([prev, g_ref[...].astype(F32)], axis=0)
        gc = _ffn_conv(ge, w_ref)[FFN_HALO:, :]
        a = _gelu(gc) * v_ref[...].astype(F32)
        o_ref[...] = a.astype(BF16)
        ot_ref[...] = a.T.astype(BF16)

    return pl.pallas_call(
        body, name="ffn_gate_fwd", grid=(nc, T // tt),
        in_specs=[pl.BlockSpec((tt, cb), lambda j, t: (t, j)),
                  pl.BlockSpec((FFN_HALO, cb), lambda j, t: (jnp.maximum(t * hh - 1, 0), j)),
                  pl.BlockSpec((tt, cb), lambda j, t: (t, nc + j)),
                  pl.BlockSpec((FFN_K, cb), lambda j, t: (0, j))],
        out_specs=[pl.BlockSpec((tt, cb), lambda j, t: (t, j)), pl.BlockSpec((cb, tt), lambda j, t: (j, t))],
        out_shape=[jax.ShapeDtypeStruct((T, D_FF), BF16), jax.ShapeDtypeStruct((D_FF, T), BF16)],
        compiler_params=_cp(("parallel", "parallel")))(up, up, up, w)


def _ffn_gate_bwd(dact, up, w):
    T = up.shape[0]
    tt = min(T, FFN_TT)
    cb = FFN_CB
    nc = D_FF // cb
    nt = T // tt
    hh = tt // FFN_HALO
    H = FFN_HALO

    def body(d_ref, dn_ref, g_ref, gp_ref, gn_ref, v_ref, vn_ref, w_ref, dup_ref, dw_ref):
        t = pl.program_id(1)
        prev = jnp.where(t == 0, 0.0, gp_ref[...].astype(F32))
        ge = jnp.concatenate([prev, g_ref[...].astype(F32), gn_ref[...].astype(F32)], axis=0)
        gc = _ffn_conv(ge, w_ref)[H:, :]
        gg, dgg = _gelu_and_grad(gc)
        de = jnp.concatenate([d_ref[...].astype(F32), jnp.where(t == nt - 1, 0.0, dn_ref[...].astype(F32))], axis=0)
        ve = jnp.concatenate([v_ref[...].astype(F32), vn_ref[...].astype(F32)], axis=0)
        dup_ref[1] = (de * gg)[:tt, :].astype(BF16)
        dgc = de * ve * dgg
        n = tt + H
        acc = dgc * w_ref[FFN_K - 1:FFN_K, :]
        for j in range(FFN_K - 1):
            acc = acc + pltpu.roll(dgc, n - (FFN_K - 1 - j), 0) * w_ref[j:j + 1, :]
        dup_ref[0] = acc[:tt, :].astype(BF16)

        @pl.when(t == 0)
        def _():
            dw_ref[...] = jnp.zeros_like(dw_ref)

        dgm = dgc[:tt, :]
        for j in range(FFN_K):
            sh = FFN_K - 1 - j
            xs = ge[H:H + tt, :] if sh == 0 else pltpu.roll(ge, sh, 0)[H:H + tt, :]
            dw_ref[j:j + 1, :] += jnp.sum(dgm * xs, axis=0, keepdims=True)

    def nxt(t):
        return jnp.minimum((t + 1) * hh, T // H - 1)

    return pl.pallas_call(
        body, name="ffn_gate_bwd", grid=(nc, nt),
        in_specs=[pl.BlockSpec((tt, cb), lambda j, t: (t, j)),
                  pl.BlockSpec((H, cb), lambda j, t: (nxt(t), j)),
                  pl.BlockSpec((tt, cb), lambda j, t: (t, j)),
                  pl.BlockSpec((H, cb), lambda j, t: (jnp.maximum(t * hh - 1, 0), j)),
                  pl.BlockSpec((H, cb), lambda j, t: (nxt(t), j)),
                  pl.BlockSpec((tt, cb), lambda j, t: (t, nc + j)),
                  pl.BlockSpec((H, cb), lambda j, t: (nxt(t), nc + j)),
                  pl.BlockSpec((FFN_K, cb), lambda j, t: (0, j))],
        out_specs=[pl.BlockSpec((2, tt, cb), lambda j, t: (0, t, j)),
                   pl.BlockSpec((FFN_K, cb), lambda j, t: (0, j))],
        out_shape=[jax.ShapeDtypeStruct((2, T, D_FF), BF16), jax.ShapeDtypeStruct((FFN_K, D_FF), F32)],
        compiler_params=_cp(("parallel", "arbitrary")))(dact, dact, up, up, up, up, up, w)


def _row_tile(rows, cap):
    best = 8
    for r in range(8, min(rows, cap) + 1, 8):
        if rows % r == 0:
            best = r
    return best


def _adamw(parts, w, m, v, rt, name, layer=None, prev=None):
    P, R, C = parts.shape
    cw = w.shape[-1]

    def body(p_ref, w_ref, m_ref, v_ref, *rest):
        g_ref, d_ref, mo_ref, vo_ref = rest[-4:]
        g = p_ref[0].astype(F32)
        for i in range(1, P):
            g = g + p_ref[i].astype(F32)
        if cw != C:
            g = jnp.concatenate([g[:, :AB_ORIG_END], g[:, C_XR:]], axis=1)
        wv = w_ref[...]
        mn = ADAM_B1 * m_ref[...] + (1.0 - ADAM_B1) * g
        vn = ADAM_B2 * v_ref[...] + (1.0 - ADAM_B2) * (g * g)
        m_hat = mn / (1.0 - ADAM_B1 ** ADAM_STEP)
        v_hat = vn / (1.0 - ADAM_B2 ** ADAM_STEP)
        g_ref[...] = g
        d_ref[...] = -ADAM_LR * (m_hat / (jnp.sqrt(v_hat) + ADAM_EPS) + ADAM_WD * wv)
        mo_ref[...] = mn
        vo_ref[...] = vn

    if layer is None:
        blk = pl.BlockSpec((rt, cw), lambda r: (r, 0))
        sh = jax.ShapeDtypeStruct((R, cw), F32)
    else:
        blk = pl.BlockSpec((None, rt, cw), lambda r: (layer, r, 0))
        sh = jax.ShapeDtypeStruct(w.shape, F32)
    extra = list(prev) if prev is not None else []
    return pl.pallas_call(
        body, name=name, grid=(R // rt,),
        in_specs=[pl.BlockSpec((P, rt, C), lambda r: (0, r, 0)), blk, blk, blk]
        + [pl.BlockSpec(memory_space=pl.ANY)] * len(extra),
        out_specs=[blk, blk, blk, blk], out_shape=[sh, sh, sh, sh],
        input_output_aliases={4 + i: i for i in range(len(extra))},
        compiler_params=_cp(("parallel",)))(parts, w, m, v, *extra)


def _peer(k):
    x, y, c = lax.axis_index("x"), lax.axis_index("y"), lax.axis_index("c")
    px = 1 - x if k & 4 else x
    py = 1 - y if k & 2 else y
    pc = 1 - c if k & 1 else c
    return (px, py, pc), 4 * px + 2 * py + pc


def _all_gather(x, name):
    R, C = x.shape

    def body(x_ref, o_ref, send_sems, recv_sems, local_sem):
        me = 4 * lax.axis_index("x") + 2 * lax.axis_index("y") + lax.axis_index("c")
        mine = pltpu.make_async_copy(x_ref, o_ref.at[me], local_sem)
        mine.start()
        sends = []
        for k in range(1, N_DEV):
            dev, _ = _peer(k)
            cp = pltpu.make_async_remote_copy(src_ref=x_ref, dst_ref=o_ref.at[me], send_sem=send_sems.at[k - 1],
                                              recv_sem=recv_sems.at[k - 1], device_id=dev, device_id_type=MESH_IDS)
            cp.start()
            sends.append(cp)
        for k in range(1, N_DEV):
            dev, idx = _peer(k)
            pltpu.make_async_remote_copy(src_ref=x_ref, dst_ref=o_ref.at[idx], send_sem=send_sems.at[k - 1],
                                         recv_sem=recv_sems.at[k - 1], device_id=dev,
                                         device_id_type=MESH_IDS).wait_recv()
        for cp in sends:
            cp.wait_send()
        mine.wait()

    return pl.pallas_call(
        body, name=name, in_specs=[pl.BlockSpec(memory_space=pl.ANY)], out_specs=pl.BlockSpec(memory_space=pl.ANY),
        out_shape=jax.ShapeDtypeStruct((N_DEV, R, C), x.dtype),
        scratch_shapes=[pltpu.SemaphoreType.DMA((N_DEV - 1,)), pltpu.SemaphoreType.DMA((N_DEV - 1,)),
                        pltpu.SemaphoreType.DMA],
        compiler_params=pltpu.CompilerParams(has_side_effects=True))(x)


HBM_SPEC = pl.BlockSpec(memory_space=pltpu.HBM)
SEM_SPEC = pl.BlockSpec(memory_space=pltpu.SEMAPHORE)
EFFECT = pltpu.SideEffectType.DATAFLOW_SIDE_EFFECTING
OTHER_CHIPS = ((1, 0), (0, 1), (1, 1))


def _split_start(bufs, plan, n, name):
    nb = len(bufs)

    def body(*refs):
        send_sems, recv_sems, token = refs[nb], refs[nb + 1], refs[2 * nb + 2]
        for i, (src, dst, _, dev) in enumerate(plan(refs[:nb])):
            pltpu.make_async_remote_copy(src_ref=src, dst_ref=dst, send_sem=send_sems.at[i],
                                         recv_sem=recv_sems.at[i], device_id=dev, device_id_type=MESH_IDS).start()
        token[...] = jnp.zeros_like(token)

    outs = pl.pallas_call(
        body, name=name,
        out_shape=(pltpu.SemaphoreType.DMA((n,)), pltpu.SemaphoreType.DMA((n,)),
                   *[pltpu.HBM(b.shape, b.dtype) for b in bufs], jax.ShapeDtypeStruct((8, 128), F32)),
        in_specs=[HBM_SPEC] * nb,
        out_specs=(SEM_SPEC, SEM_SPEC, *[HBM_SPEC] * nb, pl.BlockSpec(memory_space=pltpu.VMEM)),
        input_output_aliases={i: 2 + i for i in range(nb)},
        compiler_params=pltpu.CompilerParams(has_side_effects=EFFECT),
    )(*[pltpu.with_memory_space_constraint(b, pltpu.HBM) for b in bufs])
    return dict(send=outs[0], recv=outs[1], bufs=list(outs[2:2 + nb]), token=outs[2 + nb], plan=plan, n=n)


def _split_wait(st, after, name):
    bufs = st["bufs"]
    nb = len(bufs)
    plan = st["plan"]
    afters = list(after) if isinstance(after, (list, tuple)) else [after]

    def body(*refs):
        send_sems, recv_sems = refs[nb], refs[nb + 1]
        for i, (src, dst, land, dev) in enumerate(plan(refs[:nb])):
            pltpu.make_async_remote_copy(src_ref=src, dst_ref=dst, send_sem=send_sems.at[i],
                                         recv_sem=recv_sems.at[i], device_id=dev,
                                         device_id_type=MESH_IDS).wait_send()
            pltpu.make_async_remote_copy(src_ref=src, dst_ref=land, send_sem=send_sems.at[i],
                                         recv_sem=recv_sems.at[i], device_id=dev,
                                         device_id_type=MESH_IDS).wait_recv()

    outs = pl.pallas_call(
        body, name=name, out_shape=tuple(pltpu.HBM(b.shape, b.dtype) for b in bufs),
        in_specs=[HBM_SPEC] * nb + [SEM_SPEC, SEM_SPEC] + [pl.BlockSpec(memory_space=pl.ANY)] * len(afters),
        out_specs=tuple([HBM_SPEC] * nb), input_output_aliases={i: i for i in range(nb)},
        compiler_params=pltpu.CompilerParams(has_side_effects=EFFECT),
    )(*bufs, st["send"], st["recv"], *afters)
    return list(outs)


def _xyc():
    return lax.axis_index("x"), lax.axis_index("y"), lax.axis_index("c")


def _flip(x, y, a, b):
    return (1 - x if a else x), (1 - y if b else y)


def _ag1_plan(outs):
    x, y, c = _xyc()
    me = 4 * x + 2 * y + c
    copies = []
    for o in outs:
        copies.append((o.at[me], o.at[me], o.at[4 * x + 2 * y + 1 - c], (x, y, 1 - c)))
        for a, b in OTHER_CHIPS:
            px, py = _flip(x, y, a, b)
            copies.append((o.at[me], o.at[me], o.at[4 * px + 2 * py + c], (px, py, c)))
    return copies


def _ag2_plan(outs):
    x, y, c = _xyc()
    copies = []
    for o in outs:
        for a, b in OTHER_CHIPS:
            px, py = _flip(x, y, a, b)
            mine, sibs = 4 * px + 2 * py + c, 4 * px + 2 * py + 1 - c
            copies.append((o.at[mine], o.at[mine], o.at[sibs], (x, y, 1 - c)))
    return copies


def _rs1_plan(refs):
    x, y, c = _xyc()
    copies = []
    for g, land in zip(refs[0::2], refs[1::2]):
        for j in range(4):
            copies.append((g.at[2 * j + 1 - c], land.at[j], land.at[j], (x, y, 1 - c)))
    return copies


def _rs2_plan(refs):
    x, y, c = _xyc()
    mychip = 2 * x + y
    copies = []
    for s, land in zip(refs[0::2], refs[1::2]):
        for a, b in OTHER_CHIPS:
            px, py = _flip(x, y, a, b)
            copies.append((s.at[2 * px + py], land.at[mychip], land.at[2 * px + py], (px, py, c)))
    return copies


def _landing_like(g, name):
    def body(g_ref, o_ref):
        del g_ref, o_ref

    anyspec = pl.BlockSpec(memory_space=pl.ANY)
    return pl.pallas_call(body, name=name, in_specs=[anyspec], out_specs=anyspec,
                          out_shape=jax.ShapeDtypeStruct((4,) + g.shape[1:], g.dtype))(g)


def _place(x, slots, by_chip, name):
    R, C = x.shape[-2:]
    rt = _row_tile(R, 512)
    xi, yi, ci = _xyc()
    idx = (2 * xi + yi if by_chip else 4 * xi + 2 * yi + ci).astype(jnp.int32).reshape(1)

    def body(i_ref, x_ref, o_ref):
        o_ref[...] = x_ref[...]

    if by_chip:
        in_spec = pl.BlockSpec((None, rt, C), lambda r, i: (i[0], r, 0))
    else:
        in_spec = pl.BlockSpec((rt, C), lambda r, i: (r, 0))
    grid_spec = pltpu.PrefetchScalarGridSpec(
        num_scalar_prefetch=1, grid=(R // rt,), in_specs=[in_spec],
        out_specs=pl.BlockSpec((None, rt, C), lambda r, i: (i[0], r, 0)))
    return pl.pallas_call(body, name=name, grid_spec=grid_spec,
                          out_shape=jax.ShapeDtypeStruct((slots, R, C), x.dtype),
                          compiler_params=_cp(("parallel",)))(idx, x)


def _pair_sum(g, land, cidx, name):
    _, R, C = land.shape
    rt = _row_tile(R, 512)
    g4 = g.reshape(4, 2, R, C)

    def body(c_ref, g_ref, l_ref, o_ref):
        o_ref[...] = (g_ref[...].astype(F32) + l_ref[...].astype(F32)).astype(o_ref.dtype)

    grid_spec = pltpu.PrefetchScalarGridSpec(
        num_scalar_prefetch=1, grid=(4, R // rt),
        in_specs=[pl.BlockSpec((None, None, rt, C), lambda j, r, c_ref: (j, c_ref[0], r, 0)),
                  pl.BlockSpec((None, rt, C), lambda j, r, c_ref: (j, r, 0))],
        out_specs=pl.BlockSpec((None, rt, C), lambda j, r, c_ref: (j, r, 0)))
    return pl.pallas_call(body, name=name, grid_spec=grid_spec, out_shape=jax.ShapeDtypeStruct(land.shape, land.dtype),
                          compiler_params=_cp(("parallel", "parallel")))(cidx, g4, land)


def _no_hook(event, l, after, payload=None):
    return None


def _tie(x, token):
    if token is None:
        return x

    def body(x_ref, t_ref, o_ref):
        del x_ref, t_ref, o_ref

    anyspec = pl.BlockSpec(memory_space=pl.ANY)
    return pl.pallas_call(body, name="tie", in_specs=[anyspec, anyspec], out_specs=anyspec,
                          out_shape=jax.ShapeDtypeStruct(x.shape, x.dtype), input_output_aliases={0: 0})(x, token)


def _layer_fwd(x, W, l, hook=_no_hook):
    T = x.shape[0]
    n = f"l{l}_"
    h1, h1t = _norm_fwd(x, W["norm1"], n + "norm1_fwd")
    proj = _mm_nn(h1, W["win"], F32, n + "mm_in", tn_c=(1792,))
    y_pool = _pool_fwd(proj, W["pool_w"], W["pool_b"], _tie(W["pool_s"], hook("f_in", l, proj)))
    cpre = _conv_fwd(proj, C_QKV, 3 * GDN_W, W["gconv_w"], None, 256, n + "gdn_conv_fwd")
    qkv, bb, gb = _gdn_pre_fwd(cpre, proj, W["alog"], W["dtb"])
    mixed, states, tms = _gdn_fwd(qkv, bb, gb, proj, W["gnorm"], y_pool)
    lconv_w = _tie(W["lconv_w"], hook("f_mix", l, states))
    xc = _conv_fwd(proj, C_XR, LRU_W, lconv_w, W["lconv_b"], 128, n + "lru_conv_fwd")
    mixed, hst = _lru_fwd(xc, proj, W["wa"], W["ba"], W["wx"], W["bx"], W["lam"], mixed)
    mixed = _tie(mixed, hook("f_out", l, hst))
    x1 = _mm_nn(mixed, W["wout"], F32, n + "mm_out", add=x)
    h2, h2t = _norm_fwd(x1, W["norm2"], n + "norm2_fwd")
    h2 = _tie(h2, hook("f_n2", l, h2t))
    up = _mm_up(h2, W["wup"], n + "mm_up")
    act, act_t = _ffn_gate_fwd(up, _tie(W["fconv_w"], hook("f_up", l, up)))
    act = _tie(act, hook("f_act", l, act_t))
    x2 = _mm_nn(act, W["wdown"], F32, n + "mm_down", add=x1, tk_c=(3072, 2048, 1536, 1024, 512, 256))
    hook("f_end", l, x2)
    saved = dict(x=x, h1t=h1t, proj=proj, cpre=cpre, qkv=qkv, bb=bb, gb=gb, states=states, tms=tms, xc=xc, hst=hst,
                 mixed=mixed, x1=x1, h2t=h2t, up=up, act_t=act_t)
    return x2, saved


def _layer_bwd(dx2, dx2b, W, S, l, hook=_no_hook):
    T = dx2.shape[0]
    n = f"l{l}_"
    dact = _mm_nt(dx2b, W["wdown"], BF16, n + "mm_down_dx", tk_c=(2048,), tn_c=(1536,))
    g_wdown = _mm_nn(S["act_t"], dx2b, BF16, n + "mm_down_dw", tn_c=(2048,))
    dup, g_fconv = _ffn_gate_bwd(dact, S["up"], W["fconv_w"])
    ns = W["wup"].shape[2]
    dh2 = _mm_up_t(dup, W["wup"], n + "mm_up_dx")
    g_wup = _mm_dup(S["h2t"], dup, ns, n + "mm_up_dw")
    tok = hook("b_ffn", l, g_wup, dict(ffn_down=g_wdown, ffn_up=g_wup))
    dx1, dx1b, g_norm2 = _norm_bwd(S["x1"], _tie(W["norm2"], tok), dh2, dx2, n + "norm2_bwd")
    dmixed = _mm_nt(dx1b, W["wout"], F32, n + "mm_out_dx", tk_c=(2048,), tn_c=(2048,))
    g_wout = _mm_tn(S["mixed"], dx1b, BF16, n + "mm_out_dw", tn_c=(2048,))
    tok = hook("b_mid", l, g_wout)
    proj = S["proj"]
    dproj, g_pool_w, g_pool_b, g_pool_s = _pool_bwd(dmixed, proj, W["pool_w"], W["pool_b"], _tie(W["pool_s"], tok))
    dq, dk, dv, dbb, dgb, dproj, g_gnorm = _gdn_bwd(S["qkv"], S["bb"], S["gb"], proj, W["gnorm"], S["states"],
                                                    S["tms"], dmixed, dproj)
    dc, dproj, g_alog, g_dtb = _gdn_pre_bwd(dq, dk, dv, S["cpre"], dbb, dgb, proj, W["alog"], W["dtb"], dproj)
    dproj, g_gconv = _conv_bwd(dc, proj, C_QKV, W["gconv_w"], 256, n + "gdn_conv_bwd", dproj)
    dxc, dproj, g_wa, g_wx, g_ba, g_bx, g_lam = _lru_bwd(dmixed, S["xc"], proj, S["hst"], W["wa"], W["ba"], W["wx"],
                                                          W["bx"], W["lam"], dproj)
    dproj, g_lconv, g_lconv_b = _conv_bwd(dxc, proj, C_XR, W["lconv_w"], 128, n + "lru_conv_bwd", dproj,
                                          want_db=True)
    dh1 = _mm_nt(dproj, W["win"], F32, n + "mm_in_dx", tk_c=(1792,))
    g_win = _mm_nn(S["h1t"], dproj, BF16, n + "mm_in_dw", tn_c=(1792,))
    tok = hook("b_in", l, g_win, dict(w_out=g_wout, w_in=g_win))
    dx, dxb, g_norm1 = _norm_bwd(S["x"], _tie(W["norm1"], tok), dh1, dx1, n + "norm1_bwd")
    big = dict(w_in=g_win, w_out=g_wout, ffn_up=g_wup, ffn_down=g_wdown)
    small = dict(norm1_w=g_norm1[0], pool_w=g_pool_w, pool_b=g_pool_b.reshape(4, 128), pool_scale=g_pool_s[0],
                 gdn_conv_w=g_gconv, gdn_a_log=g_alog[0, :HEADS], gdn_dt_bias=g_dtb[0, :HEADS],
                 gdn_norm_w=g_gnorm[0], lru_conv_w=g_lconv, lru_conv_b=g_lconv_b[0], lru_wa=g_wa, lru_ba=g_ba[0],
                 lru_wx=g_wx, lru_bx=g_bx[0], lru_lambda=g_lam[0], norm2_w=g_norm2[0], ffn_conv_w=g_fconv)
    dxb = _tie(dxb, hook("b_end", l, dx, small))
    return dx, dxb, big, small


def _pad_lane(v):
    return jnp.pad(v, (0, 128 - v.shape[0])).reshape(1, 128)


def _layer_weights(l, big, P, conv_full):
    return dict(
        win=big.get("w_in"), wout=big.get("w_out"), wup=big.get("ffn_up"), wdown=big.get("ffn_down"),
        norm1=P["norm1_w"][l].reshape(1, D_MODEL), norm2=P["norm2_w"][l].reshape(1, D_MODEL),
        pool_w=P["pool_w"][l], pool_b=P["pool_b"][l].reshape(1, POOL_W), pool_s=P["pool_scale"][l].reshape(1, POOL_W),
        gconv_w=conv_full["gdn_conv_w"][l], alog=_pad_lane(P["gdn_a_log"][l]), dtb=_pad_lane(P["gdn_dt_bias"][l]),
        gnorm=P["gdn_norm_w"][l].reshape(1, HD),
        lconv_w=conv_full["lru_conv_w"][l], lconv_b=P["lru_conv_b"][l].reshape(1, LRU_W),
        wa=P["lru_wa"][l], ba=P["lru_ba"][l].reshape(1, LRU_W), wx=P["lru_wx"][l],
        bx=P["lru_bx"][l].reshape(1, LRU_W), lam=P["lru_lambda"][l].reshape(1, LRU_W),
        fconv_w=conv_full["ffn_conv_w"][l])


def _local_step(x, target, Ws, final_norm_w, hook=_no_hook):
    saved = []
    for l in range(DEPTH):
        x, s = _layer_fwd(x, Ws[l], l, hook)
        saved.append(s)
    loss, dx, dxb, g_final = _loss_head(x, final_norm_w.reshape(1, D_MODEL), target)
    bigs, smalls = [None] * DEPTH, [None] * DEPTH
    for l in reversed(range(DEPTH)):
        dx, dxb, bigs[l], smalls[l] = _layer_bwd(dx, dxb, Ws[l], saved[l], l, hook)
    return loss, dx, g_final[0], bigs, smalls


SMALL_REPL = ("norm1_w", "pool_w", "pool_b", "pool_scale", "gdn_a_log", "gdn_dt_bias", "gdn_norm_w", "lru_conv_b",
              "lru_wa", "lru_ba", "lru_wx", "lru_bx", "lru_lambda", "norm2_w", "final_norm_w")
SMALL_SHARD = ("gdn_conv_w", "lru_conv_w", "ffn_conv_w")
BIG = ("w_in", "w_out", "ffn_up", "ffn_down")
WEIGHTS = ("norm1_w", "w_in", "pool_w", "pool_b", "pool_scale", "gdn_conv_w", "gdn_a_log", "gdn_dt_bias",
           "gdn_norm_w", "lru_conv_w", "lru_conv_b", "lru_wa", "lru_ba", "lru_wx", "lru_bx", "lru_lambda", "w_out",
           "norm2_w", "ffn_up", "ffn_conv_w", "ffn_down", "final_norm_w")
SEG = 1024
PACK_ROWS_MULT = 256 * 128


def _pack(arrs):
    pieces, table, off = [], [], 0
    for a in arrs:
        n = a.size
        npad = -(-n // SEG) * SEG
        pieces.append(jnp.pad(a.reshape(-1).astype(F32), (0, npad - n)))
        table.append((off, n, a.shape))
        off += npad
    tail = -off % PACK_ROWS_MULT
    if tail:
        pieces.append(jnp.zeros((tail,), F32))
        off += tail
    return jnp.concatenate(pieces).reshape(off // 128, 128), table


def _unpack(buf, table):
    flat = buf.reshape(-1)
    return [flat[off:off + n].reshape(shape) for off, n, shape in table]


def _pad_in(w):
    z1 = jnp.zeros(w.shape[:-1] + (C_XR - AB_ORIG_END,), w.dtype)
    return jnp.concatenate([w[..., :AB_ORIG_END], z1, w[..., AB_ORIG_END:]], axis=-1)


def _unpad_in(w):
    return jnp.concatenate([w[..., :AB_ORIG_END], w[..., C_XR:C_GR + LRU_W]], axis=-1)


def kernel(x, norm1_w, w_in, pool_w, pool_b, pool_scale, gdn_conv_w, gdn_a_log, gdn_dt_bias, gdn_norm_w, lru_conv_w, lru_conv_b, lru_wa, lru_ba, lru_wx, lru_bx, lru_lambda, w_out, norm2_w, ffn_up, ffn_conv_w, ffn_down, final_norm_w, loss_target, m_norm1_w, m_w_in, m_pool_w, m_pool_b, m_pool_scale, m_gdn_conv_w, m_gdn_a_log, m_gdn_dt_bias, m_gdn_norm_w, m_lru_conv_w, m_lru_conv_b, m_lru_wa, m_lru_ba, m_lru_wx, m_lru_bx, m_lru_lambda, m_w_out, m_norm2_w, m_ffn_up, m_ffn_conv_w, m_ffn_down, m_final_norm_w, v_norm1_w, v_w_in, v_pool_w, v_pool_b, v_pool_scale, v_gdn_conv_w, v_gdn_a_log, v_gdn_dt_bias, v_gdn_norm_w, v_lru_conv_w, v_lru_conv_b, v_lru_wa, v_lru_ba, v_lru_wx, v_lru_bx, v_lru_lambda, v_w_out, v_norm2_w, v_ffn_up, v_ffn_conv_w, v_ffn_down, v_final_norm_w):
    loc = dict(locals())
    Wp = {n: loc[n] for n in WEIGHTS}
    Mp = {n: loc["m_" + n] for n in WEIGHTS}
    Vp = {n: loc["v_" + n] for n in WEIGHTS}
    xi, yi, ci = _xyc()
    me = 4 * xi + 2 * yi + ci
    mychip = 2 * xi + yi
    cidx = ci.astype(jnp.int32).reshape(1)
    keys = dict(w_in="win", w_out="wout", ffn_up="wup", ffn_down="wdown")

    def shard2d(d, name, l):
        a = d[name][l]
        return _pad_in(a) if name == "w_in" else a

    def wshard(l, name):
        return shard2d(Wp, name, l).astype(BF16)

    def full2d(name, full):
        return full if name == "ffn_up" else full.reshape(-1, full.shape[2])

    def ag_start(shards, tag, token=None):
        if token is not None:
            shards = [_tie(shards[0], token)] + list(shards[1:])
        bufs = [_place(s, N_DEV, False, f"place_{tag}{i}") for i, s in enumerate(shards)]
        return _split_start(bufs, _ag1_plan, 4 * len(bufs), f"ag1s_{tag}")

    def ag_mid(st, after, tag):
        bufs = _split_wait(st, after, f"ag1w_{tag}")
        return _split_start(bufs, _ag2_plan, 3 * len(bufs), f"ag2s_{tag}")

    def ag_end(st, after, tag):
        return _split_wait(st, after, f"ag2w_{tag}")

    def rs_start(gs, tag):
        bufs = []
        for nm, g in gs.items():
            if nm != "ffn_up":
                g = g.reshape(N_DEV, g.shape[0] // N_DEV, g.shape[1])
            bufs += [g, _landing_like(g, f"land_{nm}_{tag}")]
        st = _split_start(bufs, _rs1_plan, 4 * len(gs), f"rs1s_{tag}")
        st["names"] = list(gs)
        return st

    def rs_mid(st, after, tag):
        bufs = _split_wait(st, after, f"rs1w_{tag}")
        out = []
        for i, nm in enumerate(st["names"]):
            s = _pair_sum(bufs[2 * i], bufs[2 * i + 1], cidx, f"pairsum_{nm}_{tag}")
            out += [s, _place(s, 4, True, f"place_{nm}_{tag}")]
        st2 = _split_start(out, _rs2_plan, 3 * len(st["names"]), f"rs2s_{tag}")
        st2["names"] = st["names"]
        return st2

    def rs_end(st, after, tag):
        bufs = _split_wait(st, after, f"rs2w_{tag}")
        return dict(zip(st["names"], bufs[1::2]))

    lnames = tuple(n for n in SMALL_REPL if n != "final_norm_w") + SMALL_SHARD

    def small_pack(l, gs, extra):
        return _pack([gs[nm] for nm in lnames] + extra)

    def small_state(d, l, gs):
        arrs = [d[nm][l] if nm in SMALL_REPL else jnp.zeros(gs[nm].shape, F32) for nm in lnames]
        if l == 0:
            arrs += [d["final_norm_w"], jnp.zeros((1,), F32)]
        return _pack(arrs)[0]

    wmv = {name: [d[name] for d in (Wp, Mp, Vp)] for name in BIG}
    groups = dict(b1=[(0, "w_out")], b2=[(0, "ffn_up")], b3=[(0, "ffn_down")],
                  c1=[(1, "w_in"), (1, "w_out")], c2=[(1, "ffn_up")], c3=[(1, "ffn_down")])
    gshards = {g: [wshard(l, n) for l, n in members] for g, members in groups.items()}
    fwd_plan = {("f_in", 0): [("mid", "b1"), ("start", "b2")],
                ("f_out", 0): [("end", "b1"), ("mid", "b2"), ("start", "b3")],
                ("f_n2", 0): [("end", "b2"), ("start", "c1")],
                ("f_up", 0): [("mid", "b3")],
                ("f_act", 0): [("end", "b3"), ("mid", "c1"), ("start", "c2")],
                ("f_end", 0): [("end", "c1")],
                ("f_mix", 1): [("mid", "c2"), ("start", "c3")],
                ("f_out", 1): [("end", "c2")],
                ("f_n2", 1): [("mid", "c3")],
                ("f_act", 1): [("end", "c3")]}
    later = [s for g in gshards.values() for s in g]
    stA = ag_start([wshard(0, "w_in")], "a")
    stA2 = ag_mid(stA, [stA["token"]] + later, "a")
    gst = {"b1": ag_start(gshards["b1"], "b1", stA2["token"])}
    (w_in0,) = ag_end(stA2, gst["b1"]["token"], "a")

    cbuf, ctable = _pack([Wp[n] for n in SMALL_SHARD])
    call = _all_gather(cbuf, "ag_conv_w")
    parts = [_unpack(call[i], ctable) for i in range(N_DEV)]
    conv_full = {n: jnp.concatenate([parts[i][j] for i in range(N_DEV)], axis=-1) for j, n in enumerate(SMALL_SHARD)}

    Ws = [_layer_weights(l, {}, Wp, conv_full) for l in range(DEPTH)]
    Ws[0]["win"] = full2d("w_in", w_in0)
    st = {}

    def fwd_actions(actions, after):
        token = None
        for what, g in actions:
            dep = after if token is None else token
            if what == "start":
                gst[g] = ag_start(gshards[g], g, dep)
                token = gst[g]["token"]
            elif what == "mid":
                gst[g] = ag_mid(gst[g], dep, g)
                token = gst[g]["token"]
            else:
                bufs = ag_end(gst[g], dep, g)
                for (gl, n), b in zip(groups[g], bufs):
                    Ws[gl][keys[n]] = full2d(n, b)
                token = bufs[0]
        return token

    def hook(event, l, after, payload=None):
        if (event, l) in fwd_plan:
            return fwd_actions(fwd_plan[event, l], after)
        if event == "b_ffn":
            st["ffn", l] = rs_start(payload, f"ffn{l}")
            return st["ffn", l]["token"]
        if event == "b_mid":
            st["ffn2", l] = rs_mid(st["ffn", l], after, f"ffn{l}")
            if l == 0:
                st["sm1b"] = ag_mid(st["sm1"], st["ffn2", l]["token"], "sm1")
                return st["sm1b"]["token"]
            return st["ffn2", l]["token"]
        if event == "b_in":
            st["io", l] = rs_start(payload, f"io{l}")
            if l == 0:
                st["sm1g"] = ag_end(st["sm1b"], st["io", l]["token"], "sm1")[0]
            return st["io", l]["token"]
        if event == "b_end" and l == 1:
            st["io2", 1] = rs_mid(st["io", 1], after, "io1")
            gbuf1, st["table1"] = small_pack(1, payload, [])
            st["sm1"] = ag_start([gbuf1], "sm1", st["io2", 1]["token"])
            return st["sm1"]["token"]
        return None

    loss, dx, g_final, _, gsmall = _local_step(x[0], loss_target[0], Ws, final_norm_w, hook)

    out_g, out_d, out_m, out_v = {}, {}, {}, {}
    outs4 = (out_g, out_d, out_m, out_v)
    rts = dict(w_in=64, w_out=128, ffn_up=256, ffn_down=128)
    big_res = {}

    def adam_big(l, parts):
        for name, p in parts.items():
            big_res[name] = _adamw(p, *wmv[name], rts[name], f"adamw_{name}_{l}", layer=l, prev=big_res.get(name))
        return [big_res[name][0] for name in parts]

    def adam_small(l, gall, gs):
        rs = gall.shape[1]
        return _adamw(gall, small_state(Wp, l, gs), small_state(Mp, l, gs), small_state(Vp, l, gs),
                      _row_tile(rs, 512), f"adamw_small_{l}")

    gbuf0, table0 = small_pack(0, gsmall[0], [g_final, loss[0, :1]])
    sm0 = ag_start([gbuf0], "sm0", st["io", 0]["token"])
    o = adam_big(1, rs_end(st["ffn2", 1], sm0["token"], "ffn1"))
    st["io2", 0] = rs_mid(st["io", 0], o, "io0")
    o = adam_big(1, rs_end(st["io2", 1], st["io2", 0]["token"], "io1"))
    o = adam_big(0, rs_end(st["ffn2", 0], o, "ffn0"))
    small_res = {1: adam_small(1, _tie(st["sm1g"], o[-1]), gsmall[1])}
    sm0b = ag_mid(sm0, o + [small_res[1][0]], "sm0")
    small_res[0] = adam_small(0, ag_end(sm0b, sm0b["token"], "sm0")[0], gsmall[0])
    adam_big(0, rs_end(st["io2", 0], small_res[0][0], "io0"))

    for name in BIG:
        for i, dst in enumerate(outs4):
            dst[name] = big_res[name][i]

    unp = {0: [_unpack(r, table0) for r in small_res[0]], 1: [_unpack(r, st["table1"]) for r in small_res[1]]}
    for j, nm in enumerate(lnames):
        if nm in SMALL_REPL:
            for i, dst in enumerate(outs4):
                dst[nm] = jnp.stack([unp[l][i][j] for l in range(DEPTH)])
    for i, dst in enumerate(outs4):
        dst["final_norm_w"] = unp[0][i][len(lnames)]
    loss_total = unp[0][0][len(lnames) + 1][0]

    gsh = []
    for nm in SMALL_SHARD:
        j = lnames.index(nm)
        width = Wp[nm].shape[-1]
        gsh.append(jnp.stack([lax.dynamic_slice_in_dim(unp[l][0][j], me * width, width, axis=1)
                              for l in range(DEPTH)]))
    sbuf, stable = _pack(gsh)
    res = _adamw(sbuf[None], _pack([Wp[n] for n in SMALL_SHARD])[0], _pack([Mp[n] for n in SMALL_SHARD])[0],
                 _pack([Vp[n] for n in SMALL_SHARD])[0], sbuf.shape[0], "adamw_conv_w")
    unp2 = [_unpack(r, stable) for r in res]
    for j, nm in enumerate(SMALL_SHARD):
        for i, dst in enumerate((out_g, out_d, out_m, out_v)):
            dst[nm] = unp2[i][j]

    return (loss_total, dx[None], *[out_g[n] for n in WEIGHTS], *[out_d[n] for n in WEIGHTS],
            *[out_m[n] for n in WEIGHTS], *[out_v[n] for n in WEIGHTS])
```

```python
import functools

import jax
import jax.numpy as jnp
from jax import lax
from jax.experimental import pallas as pl
from jax.experimental.pallas import tpu as pltpu

F32 = jnp.float32
BF16 = jnp.bfloat16
HI = lax.Precision.HIGHEST
MESH_IDS = pl.DeviceIdType.MESH

N_DEV = 8
D_MODEL = 2048
DEPTH = 2
POOL_WINDOWS = (2, 4, 8, 16)
POOL_W = 512
HEADS = 6
HD = 128
GDN_W = HEADS * HD
CHUNK = 64
LRU_W = 768
LRU_C = 8.0
D_FF = 3 * D_MODEL
EPS = 1e-6
IN_COLS = 5132
PCOLS = 5376
C_QKV, C_Z, C_AB, C_XR, C_GR = 512, 2816, 3584, 3840, 4608
AB_ORIG_END = 3596
M_GDN, M_LRU = 512, 1280

ADAM_LR, ADAM_B1, ADAM_B2, ADAM_EPS, ADAM_WD, ADAM_STEP = 0.001, 0.9, 0.999, 1e-08, 0.01, 10

VMEM_LIMIT = 56 * 1024 * 1024


def _cp(sem):
    return pltpu.CompilerParams(dimension_semantics=sem, vmem_limit_bytes=VMEM_LIMIT)


def _mm(a, b, ca=1, cb=0, prec=None, cast=True):
    if cast:
        a = a.astype(BF16)
        b = b.astype(BF16)
    return lax.dot_general(a, b, (((ca,), (cb,)), ((), ())), preferred_element_type=F32, precision=prec)


def _bmm(a, b, ca=2, cb=1, prec=None, cast=True):
    if cast:
        a = a.astype(BF16)
        b = b.astype(BF16)
    return lax.dot_general(a, b, (((ca,), (cb,)), ((0,), (0,))), preferred_element_type=F32, precision=prec)


def _sigmoid(x):
    return 1.0 / (1.0 + jnp.exp(-x))


def _log1p(e):
    u = 1.0 + e
    return jnp.where(u == 1.0, e, jnp.log(u) * e / jnp.where(u == 1.0, 1.0, u - 1.0))


def _softplus(x):
    return jnp.maximum(x, 0.0) + _log1p(jnp.exp(-jnp.abs(x)))


def _expm1(x):
    u = jnp.exp(x)
    um = u - 1.0
    safe = jnp.where((u == 1.0) | (um == -1.0), 1.0, jnp.log(u))
    return jnp.where(u == 1.0, x, jnp.where(um == -1.0, -1.0, um * x / safe))


_G0 = 0.7978845608028654
_G1 = 0.044715


def _gelu(x):
    return (0.5 * x) * (1.0 + jnp.tanh(x * (_G0 + (_G0 * _G1) * (x * x))))


def _gelu_and_grad(x):
    x2 = x * x
    hx = 0.5 * x
    th = jnp.tanh(x * (_G0 + (_G0 * _G1) * x2))
    p = 1.0 + th
    dg = 0.5 * p + (hx * (1.0 - th * th)) * (_G0 + (3.0 * _G0 * _G1) * x2)
    return hx * p, dg


def _tile(T):
    return min(T, 512)


def _matmul(a, b, *, grid, a_spec, b_spec, out_shape, out_spec, dims, acc_shape, name, add=None, add_spec=None):
    nk = grid[2]
    has_add = add is not None

    def body(*refs):
        if has_add:
            a_ref, b_ref, add_ref, o_ref, acc_ref = refs
        else:
            a_ref, b_ref, o_ref, acc_ref = refs
            add_ref = None
        k = pl.program_id(2)
        p = lax.dot_general(a_ref[...].astype(BF16), b_ref[...].astype(BF16), (dims, ((), ())),
                            preferred_element_type=F32)

        def finish(r):
            if has_add:
                r = r + add_ref[...]
            o_ref[...] = r.astype(o_ref.dtype)

        if nk == 1:
            finish(p)
        else:
            @pl.when(k == 0)
            def _():
                acc_ref[...] = p

            @pl.when(k > 0)
            def _():
                acc_ref[...] += p

            @pl.when(k == nk - 1)
            def _():
                finish(acc_ref[...])

    in_specs = [a_spec, b_spec] + ([add_spec] if has_add else [])
    args = (a, b) + ((add,) if has_add else ())
    return pl.pallas_call(
        body, name=name, grid=grid, in_specs=in_specs, out_specs=out_spec, out_shape=out_shape,
        scratch_shapes=[pltpu.VMEM(acc_shape if nk > 1 else (8, 128), F32)],
        compiler_params=_cp(("parallel", "parallel", "arbitrary")),
    )(*args)


def _pick(n, cands):
    for c in cands:
        if n % c == 0:
            return c
    raise ValueError(f"no tile for {n}")


def _mm_nn(a, b, out_dtype, name, add=None, tn_c=(1024, 768, 512), tk_c=(2048, 1536, 1024, 512, 256)):
    M, K = a.shape
    N = b.shape[1]
    tm = _pick(M, (1024, 512, 256))
    tn = _pick(N, tn_c)
    tk = _pick(K, tk_c)
    return _matmul(
        a, b, grid=(M // tm, N // tn, K // tk),
        a_spec=pl.BlockSpec((tm, tk), lambda i, j, k: (i, k)),
        b_spec=pl.BlockSpec((tk, tn), lambda i, j, k: (k, j)),
        out_shape=jax.ShapeDtypeStruct((M, N), out_dtype),
        out_spec=pl.BlockSpec((tm, tn), lambda i, j, k: (i, j)),
        dims=((1,), (0,)), acc_shape=(tm, tn), name=name, add=add,
        add_spec=pl.BlockSpec((tm, tn), lambda i, j, k: (i, j)))


def _mm_nt(a, b, out_dtype, name, tk_c=(2048, 1536, 1024, 768, 512), tn_c=(1024, 768, 512)):
    M, K = a.shape
    N = b.shape[0]
    tm = _pick(M, (1024, 512, 256))
    tn = _pick(N, tn_c)
    tk = _pick(K, tk_c)
    return _matmul(
        a, b, grid=(M // tm, N // tn, K // tk),
        a_spec=pl.BlockSpec((tm, tk), lambda i, j, k: (i, k)),
        b_spec=pl.BlockSpec((tn, tk), lambda i, j, k: (j, k)),
        out_shape=jax.ShapeDtypeStruct((M, N), out_dtype),
        out_spec=pl.BlockSpec((tm, tn), lambda i, j, k: (i, j)),
        dims=((1,), (1,)), acc_shape=(tm, tn), name=name)


def _mm_tn(a, b, out_dtype, name, tn_c=(1024, 768, 512)):
    K, M = a.shape
    N = b.shape[1]
    tm = _pick(M, (1024, 768, 512))
    tn = _pick(N, tn_c)
    tk = _pick(K, (1024, 512, 256))
    return _matmul(
        a, b, grid=(M // tm, N // tn, K // tk),
        a_spec=pl.BlockSpec((tk, tm), lambda i, j, k: (k, i)),
        b_spec=pl.BlockSpec((tk, tn), lambda i, j, k: (k, j)),
        out_shape=jax.ShapeDtypeStruct((M, N), out_dtype),
        out_spec=pl.BlockSpec((tm, tn), lambda i, j, k: (i, j)),
        dims=((0,), (0,)), acc_shape=(tm, tn), name=name)


def _mm_up(h, wup, name):
    M, K = h.shape
    ns = wup.shape[2]
    tm = _pick(M, (1024, 512, 256))
    tn = ns
    per = ns // tn
    return _matmul(
        h, wup, grid=(M // tm, N_DEV * per, 1),
        a_spec=pl.BlockSpec((tm, K), lambda i, j, k: (i, 0)),
        b_spec=pl.BlockSpec((None, K, tn), lambda i, j, k: (j // per, 0, j % per)),
        out_shape=jax.ShapeDtypeStruct((M, N_DEV * ns), BF16),
        out_spec=pl.BlockSpec((tm, tn), lambda i, j, k: (i, j)),
        dims=((1,), (0,)), acc_shape=(tm, tn), name=name)


def _mm_up_t(dup, wup, name):
    M = dup.shape[1]
    D, ns = wup.shape[1], wup.shape[2]
    tm = _pick(M, (1024, 512, 256))
    tn = D
    tk = ns
    return _matmul(
        dup, wup, grid=(M // tm, D // tn, N_DEV),
        a_spec=pl.BlockSpec((None, tm, tk), lambda i, j, k: (k // 4, i, k % 4)),
        b_spec=pl.BlockSpec((None, tn, tk), lambda i, j, k: (k, j, 0)),
        out_shape=jax.ShapeDtypeStruct((M, D), F32),
        out_spec=pl.BlockSpec((tm, tn), lambda i, j, k: (i, j)),
        dims=((1,), (1,)), acc_shape=(tm, tn), name=name)


def _mm_dup(ht, dup, ns, name):
    M, K = ht.shape
    tm = 1024
    tn = ns
    per = ns // tn
    half = 4 * per
    tk = _pick(K, (2048, 1024, 512, 256))
    return _matmul(
        ht, dup, grid=(M // tm, N_DEV * per, K // tk),
        a_spec=pl.BlockSpec((tm, tk), lambda i, j, k: (i, k)),
        b_spec=pl.BlockSpec((None, tk, tn), lambda i, j, k: (j // half, k, j % half)),
        out_shape=jax.ShapeDtypeStruct((N_DEV, M, ns), BF16),
        out_spec=pl.BlockSpec((None, tm, tn), lambda i, j, k: (j // per, i, j % per)),
        dims=((1,), (0,)), acc_shape=(tm, tn), name=name)


def _norm_fwd(x, w, name):
    T, D = x.shape
    tt = _tile(T)

    def body(x_ref, w_ref, h_ref, ht_ref):
        xv = x_ref[...]
        r = lax.rsqrt(jnp.mean(xv * xv, axis=1, keepdims=True) + EPS)
        hv = xv * r * w_ref[...]
        h_ref[...] = hv.astype(BF16)
        ht_ref[...] = hv.T.astype(BF16)

    return pl.pallas_call(
        body, name=name, grid=(T // tt,),
        in_specs=[pl.BlockSpec((tt, D), lambda t: (t, 0)), pl.BlockSpec((1, D), lambda t: (0, 0))],
        out_specs=[pl.BlockSpec((tt, D), lambda t: (t, 0)), pl.BlockSpec((D, tt), lambda t: (0, t))],
        out_shape=[jax.ShapeDtypeStruct((T, D), BF16), jax.ShapeDtypeStruct((D, T), BF16)],
        compiler_params=_cp(("parallel",)))(x, w)


def _norm_bwd(x, w, dh, dres, name):
    T, D = x.shape
    tt = _tile(T)

    def body(x_ref, w_ref, dh_ref, dres_ref, dx_ref, dxb_ref, dw_ref):
        t = pl.program_id(0)
        xv = x_ref[...]
        r = lax.rsqrt(jnp.mean(xv * xv, axis=1, keepdims=True) + EPS)
        xh = xv * r
        dh_v = dh_ref[...]
        dxh = dh_v * w_ref[...]
        dxv = dres_ref[...] + r * (dxh - xh * jnp.mean(dxh * xh, axis=1, keepdims=True))
        dx_ref[...] = dxv
        dxb_ref[...] = dxv.astype(BF16)
        part = jnp.sum(dh_v * xh, axis=0, keepdims=True)

        @pl.when(t == 0)
        def _():
            dw_ref[...] = part

        @pl.when(t > 0)
        def _():
            dw_ref[...] += part

    row = pl.BlockSpec((tt, D), lambda t: (t, 0))
    vec = pl.BlockSpec((1, D), lambda t: (0, 0))
    return pl.pallas_call(
        body, name=name, grid=(T // tt,), in_specs=[row, vec, row, row], out_specs=[row, row, vec],
        out_shape=[jax.ShapeDtypeStruct((T, D), F32), jax.ShapeDtypeStruct((T, D), BF16),
                   jax.ShapeDtypeStruct((1, D), F32)],
        compiler_params=_cp(("arbitrary",)))(x, w, dh, dres)


def _loss_head(x, w, target):
    T, D = x.shape
    tt = _tile(T)

    def body(x_ref, w_ref, t_ref, loss_ref, dx_ref, dxb_ref, dw_ref):
        t = pl.program_id(0)
        xv = x_ref[...]
        r = lax.rsqrt(jnp.mean(xv * xv, axis=1, keepdims=True) + EPS)
        xh = xv * r
        err = xh * w_ref[...] - t_ref[...]
        lp = 0.5 * jnp.sum(jnp.mean(err * err, axis=1, keepdims=True), axis=0, keepdims=True)
        dy = err * (1.0 / D)
        dxh = dy * w_ref[...]
        dxv = r * (dxh - xh * jnp.mean(dxh * xh, axis=1, keepdims=True))
        dx_ref[...] = dxv
        dxb_ref[...] = dxv.astype(BF16)
        part = jnp.sum(dy * xh, axis=0, keepdims=True)
        lpb = jnp.broadcast_to(lp, (1, 128))

        @pl.when(t == 0)
        def _():
            dw_ref[...] = part
            loss_ref[...] = lpb

        @pl.when(t > 0)
        def _():
            dw_ref[...] += part
            loss_ref[...] += lpb

    row = pl.BlockSpec((tt, D), lambda t: (t, 0))
    vec = pl.BlockSpec((1, D), lambda t: (0, 0))
    return pl.pallas_call(
        body, name="loss_head", grid=(T // tt,), in_specs=[row, vec, row],
        out_specs=[pl.BlockSpec((1, 128), lambda t: (0, 0)), row, row, vec],
        out_shape=[jax.ShapeDtypeStruct((1, 128), F32), jax.ShapeDtypeStruct((T, D), F32),
                   jax.ShapeDtypeStruct((T, D), BF16), jax.ShapeDtypeStruct((1, D), F32)],
        compiler_params=_cp(("arbitrary",)))(x, w, target)


CONV_TT = 4096
CONV_BWD_TT = 2048


def _conv_fwd(x, col0, C, w, b, cb, name):
    T = x.shape[0]
    K = w.shape[0]
    tt = min(T, CONV_TT)
    nt, nc, c0 = T // tt, C // cb, col0 // cb
    has_b = b is not None

    def body(*refs):
        if has_b:
            x_ref, halo_ref, w_ref, b_ref, y_ref = refs
        else:
            x_ref, halo_ref, w_ref, y_ref = refs
        t = pl.program_id(1)
        halo = jnp.where(t == 0, 0.0, halo_ref[...])
        xe = jnp.concatenate([halo, x_ref[...]], axis=0)
        acc = xe * w_ref[K - 1:K, :]
        for j in range(K - 1):
            acc = acc + pltpu.roll(xe, K - 1 - j, 0) * w_ref[j:j + 1, :]
        if has_b:
            acc = acc + b_ref[...]
        y_ref[...] = acc[8:, :]

    in_specs = [pl.BlockSpec((tt, cb), lambda j, t: (t, c0 + j)),
                pl.BlockSpec((8, cb), lambda j, t: (jnp.maximum(t * (tt // 8) - 1, 0), c0 + j)),
                pl.BlockSpec((K, cb), lambda j, t: (0, j))]
    args = [x, x, w]
    if has_b:
        in_specs.append(pl.BlockSpec((1, cb), lambda j, t: (0, j)))
        args.append(b)
    return pl.pallas_call(
        body, name=name, grid=(nc, nt), in_specs=in_specs,
        out_specs=pl.BlockSpec((tt, cb), lambda j, t: (t, j)),
        out_shape=jax.ShapeDtypeStruct((T, C), F32), compiler_params=_cp(("parallel", "parallel")))(*args)


def _conv_bwd(dy, x, col0, w, cb, name, into, want_db=False):
    T, C = dy.shape
    K = w.shape[0]
    tt = min(T, CONV_BWD_TT)
    nt, nc, c0 = T // tt, C // cb, col0 // cb

    def body(*refs):
        if want_db:
            dy_ref, dyn_ref, x_ref, xp_ref, w_ref, _, dx_ref, dw_ref, db_ref = refs
        else:
            dy_ref, dyn_ref, x_ref, xp_ref, w_ref, _, dx_ref, dw_ref = refs
        t = pl.program_id(1)
        dyv = dy_ref[...]
        nxt = jnp.where(t == nt - 1, 0.0, dyn_ref[...])
        dye = jnp.concatenate([dyv, nxt], axis=0)
        n = tt + 8
        acc = dye * w_ref[K - 1:K, :]
        for j in range(K - 1):
            acc = acc + pltpu.roll(dye, n - (K - 1 - j), 0) * w_ref[j:j + 1, :]
        dx_ref[...] = acc[:tt, :].astype(dx_ref.dtype)
        prev = jnp.where(t == 0, 0.0, xp_ref[...])
        xe = jnp.concatenate([prev, x_ref[...]], axis=0)

        @pl.when(t == 0)
        def _():
            dw_ref[...] = jnp.zeros_like(dw_ref)
            if want_db:
                db_ref[...] = jnp.zeros_like(db_ref)

        for j in range(K):
            sh = K - 1 - j
            xs = xe[8:, :] if sh == 0 else pltpu.roll(xe, sh, 0)[8:, :]
            dw_ref[j:j + 1, :] += jnp.sum(dyv * xs, axis=0, keepdims=True)
        if want_db:
            db_ref[...] += jnp.sum(dyv, axis=0, keepdims=True)

    h8 = tt // 8
    in_specs = [pl.BlockSpec((tt, cb), lambda j, t: (t, j)),
                pl.BlockSpec((8, cb), lambda j, t: (jnp.minimum((t + 1) * h8, T // 8 - 1), j)),
                pl.BlockSpec((tt, cb), lambda j, t: (t, c0 + j)),
                pl.BlockSpec((8, cb), lambda j, t: (jnp.maximum(t * h8 - 1, 0), c0 + j)),
                pl.BlockSpec((K, cb), lambda j, t: (0, j)), pl.BlockSpec(memory_space=pl.ANY)]
    out_specs = [pl.BlockSpec((tt, cb), lambda j, t: (t, c0 + j)), pl.BlockSpec((K, cb), lambda j, t: (0, j))]
    out_shape = [jax.ShapeDtypeStruct(into.shape, into.dtype), jax.ShapeDtypeStruct((K, C), F32)]
    if want_db:
        out_specs.append(pl.BlockSpec((1, cb), lambda j, t: (0, j)))
        out_shape.append(jax.ShapeDtypeStruct((1, C), F32))
    return pl.pallas_call(
        body, name=name, grid=(nc, nt), in_specs=in_specs, out_specs=out_specs, out_shape=out_shape,
        input_output_aliases={5: 0}, compiler_params=_cp(("parallel", "arbitrary")))(dy, dy, x, x, w, into)


def _pool_d(ue, g, pos, tt):
    win = POOL_WINDOWS[g]
    ug = ue[:, g * 128:(g + 1) * 128]
    s = ug
    k = 1
    while k < win:
        s = s + pltpu.roll(s, k, 0)
        k *= 2
    cnt = jnp.minimum(pos + 1, win).astype(F32)
    return s[16:, :] / cnt - ug[16:, :]


POOL_TT = 2048


def _pool_fwd(proj, pw, pb, ps):
    T = proj.shape[0]
    tt = min(T, POOL_TT)

    def body(u_ref, halo_ref, w_ref, b_ref, s_ref, y_ref):
        t = pl.program_id(0)
        halo = jnp.where(t == 0, 0.0, halo_ref[...])
        ue = jnp.concatenate([halo, u_ref[...]], axis=0)
        pos = t * tt + lax.broadcasted_iota(jnp.int32, (tt, 1), 0)
        for g in range(4):
            sl = slice(g * 128, (g + 1) * 128)
            d = _pool_d(ue, g, pos, tt)
            yg = _mm(d, w_ref[g]) + b_ref[:, sl]
            y_ref[:, sl] = (yg * s_ref[:, sl]).astype(BF16)

    vec = pl.BlockSpec((1, POOL_W), lambda t: (0, 0))
    return pl.pallas_call(
        body, name="pool_fwd", grid=(T // tt,),
        in_specs=[pl.BlockSpec((tt, POOL_W), lambda t: (t, 0)),
                  pl.BlockSpec((16, POOL_W), lambda t: (jnp.maximum(t * (tt // 16) - 1, 0), 0)),
                  pl.BlockSpec((4, 128, 128), lambda t: (0, 0, 0)), vec, vec],
        out_specs=pl.BlockSpec((tt, POOL_W), lambda t: (t, 0)),
        out_shape=jax.ShapeDtypeStruct((T, D_MODEL), BF16), compiler_params=_cp(("parallel",)))(
            proj, proj, pw, pb, ps)


def _pool_bwd(dmixed, proj, pw, pb, ps):
    T = proj.shape[0]
    tt = min(T, POOL_TT)
    nt = T // tt

    def body(dy_ref, dyn_ref, u_ref, halo_ref, w_ref, b_ref, s_ref, du_ref, dw_ref, db_ref, ds_ref):
        t = pl.program_id(0)
        halo = jnp.where(t == 0, 0.0, halo_ref[...])
        ue = jnp.concatenate([halo, u_ref[...]], axis=0)
        dyv = dy_ref[...]
        nxt = jnp.where(t == nt - 1, 0.0, dyn_ref[...])
        dye = jnp.concatenate([dyv, nxt], axis=0)
        n = tt + 16
        pos = t * tt + lax.broadcasted_iota(jnp.int32, (tt, 1), 0)
        pos_e = t * tt + lax.broadcasted_iota(jnp.int32, (n, 1), 0)

        @pl.when(t == 0)
        def _():
            dw_ref[...] = jnp.zeros_like(dw_ref)
            db_ref[...] = jnp.zeros_like(db_ref)
            ds_ref[...] = jnp.zeros_like(ds_ref)

        for g in range(4):
            win = POOL_WINDOWS[g]
            sl = slice(g * 128, (g + 1) * 128)
            d = _pool_d(ue, g, pos, tt)
            wg = w_ref[g]
            ypre = _mm(d, wg) + b_ref[:, sl]
            sc = s_ref[:, sl]
            ds_ref[:, sl] += jnp.sum(dyv[:, sl] * ypre, axis=0, keepdims=True)
            dyp_e = dye[:, sl] * sc
            dyp = dyp_e[:tt, :]
            db_ref[:, sl] += jnp.sum(dyp, axis=0, keepdims=True)
            dw_ref[g] += _mm(d, dyp, 0, 0)
            dd_e = _mm(dyp_e, wg, 1, 1)
            cnt_e = jnp.minimum(pos_e + 1, win).astype(F32)
            s = dd_e / cnt_e
            k = 1
            while k < win:
                s = s + pltpu.roll(s, n - k, 0)
                k *= 2
            du_ref[:, sl] = (s[:tt, :] - dd_e[:tt, :]).astype(BF16)

    vec = pl.BlockSpec((1, POOL_W), lambda t: (0, 0))
    h16 = tt // 16
    return pl.pallas_call(
        body, name="pool_bwd", grid=(nt,),
        in_specs=[pl.BlockSpec((tt, POOL_W), lambda t: (t, 0)),
                  pl.BlockSpec((16, POOL_W), lambda t: (jnp.minimum((t + 1) * h16, T // 16 - 1), 0)),
                  pl.BlockSpec((tt, POOL_W), lambda t: (t, 0)),
                  pl.BlockSpec((16, POOL_W), lambda t: (jnp.maximum(t * h16 - 1, 0), 0)),
                  pl.BlockSpec((4, 128, 128), lambda t: (0, 0, 0)), vec, vec],
        out_specs=[pl.BlockSpec((tt, POOL_W), lambda t: (t, 0)),
                   pl.BlockSpec((4, 128, 128), lambda t: (0, 0, 0)), vec, vec],
        out_shape=[jax.ShapeDtypeStruct((T, PCOLS), BF16), jax.ShapeDtypeStruct((4, 128, 128), F32),
                   jax.ShapeDtypeStruct((1, POOL_W), F32), jax.ShapeDtypeStruct((1, POOL_W), F32)],
        compiler_params=_cp(("arbitrary",)))(dmixed, dmixed, proj, proj, pw, pb, ps)


def _gdn_pre_fwd(cpre, proj, alog, dtb):
    T = cpre.shape[0]
    tt = _tile(T)

    def body(c_ref, ab_ref, alog_ref, dtb_ref, qkv_ref, bb_ref, gb_ref):
        for p in range(3):
            for h in range(HEADS):
                cc = c_ref[:, (p * HEADS + h) * HD:(p * HEADS + h + 1) * HD]
                s = cc * _sigmoid(cc)
                if p < 2:
                    s = s * lax.rsqrt(jnp.sum(s * s, axis=1, keepdims=True) + EPS)
                if p == 0:
                    s = s * (HD ** -0.5)
                qkv_ref[p, h] = s
        ab = ab_ref[...]
        g = -jnp.exp(alog_ref[...]) * _softplus(ab + dtb_ref[...])
        r64 = lax.broadcasted_iota(jnp.int32, (tt, 1), 0) & (CHUNK - 1)
        k = 1
        while k < CHUNK:
            g = g + jnp.where(r64 >= k, pltpu.roll(g, k, 0), 0.0)
            k *= 2
        sb = _sigmoid(ab)
        for h in range(HEADS):
            gb_ref[h] = jnp.broadcast_to(g[:, h:h + 1], (tt, HD))
            bb_ref[h] = jnp.broadcast_to(sb[:, HEADS + h:HEADS + h + 1], (tt, HD))

    vec = pl.BlockSpec((1, 128), lambda t: (0, 0))
    hb = pl.BlockSpec((HEADS, tt, HD), lambda t: (0, t, 0))
    return pl.pallas_call(
        body, name="gdn_pre_fwd", grid=(T // tt,),
        in_specs=[pl.BlockSpec((tt, 3 * GDN_W), lambda t: (t, 0)),
                  pl.BlockSpec((tt, 128), lambda t: (t, C_AB // 128)), vec, vec],
        out_specs=[pl.BlockSpec((3, HEADS, tt, HD), lambda t: (0, 0, t, 0)), hb, hb],
        out_shape=[jax.ShapeDtypeStruct((3, HEADS, T, HD), F32), jax.ShapeDtypeStruct((HEADS, T, HD), F32),
                   jax.ShapeDtypeStruct((HEADS, T, HD), F32)],
        compiler_params=_cp(("parallel",)))(cpre, proj, alog, dtb)


def _gdn_pre_bwd(dq, dk, dv, cpre, dbb, dgb, proj, alog, dtb, dproj):
    T = cpre.shape[0]
    tt = _tile(T)

    def body(dq_ref, dk_ref, dv_ref, c_ref, dbb_ref, dgb_ref, ab_ref, alog_ref, dtb_ref, _,
             dc_ref, dab_ref, dalog_ref, ddtb_ref):
        t = pl.program_id(0)
        srcs = (dq_ref, dk_ref, dv_ref)
        for p in range(3):
            for h in range(HEADS):
                sl = slice((p * HEADS + h) * HD, (p * HEADS + h + 1) * HD)
                cc = c_ref[:, sl]
                sg = _sigmoid(cc)
                s = cc * sg
                dyv = srcs[p][h]
                if p < 2:
                    r = lax.rsqrt(jnp.sum(s * s, axis=1, keepdims=True) + EPS)
                    y = s * r
                    if p == 0:
                        dyv = dyv * (HD ** -0.5)
                    ds = r * (dyv - y * jnp.sum(dyv * y, axis=1, keepdims=True))
                else:
                    ds = dyv
                dc_ref[:, sl] = ds * sg * (1.0 + cc * (1.0 - sg))
        lane = lax.broadcasted_iota(jnp.int32, (tt, 128), 1)
        dg = jnp.zeros((tt, 128), F32)
        dbeta = jnp.zeros((tt, 128), F32)
        for h in range(HEADS):
            dg = jnp.where(lane == h, dgb_ref[h], dg)
            dbeta = jnp.where(lane == HEADS + h, dbb_ref[h], dbeta)
        r64 = lax.broadcasted_iota(jnp.int32, (tt, 1), 0) & (CHUNK - 1)
        k = 1
        while k < CHUNK:
            dg = dg + jnp.where(r64 < CHUNK - k, pltpu.roll(dg, tt - k, 0), 0.0)
            k *= 2
        ab = ab_ref[...]
        e = jnp.exp(alog_ref[...])
        xx = ab + dtb_ref[...]
        g = -e * _softplus(xx)
        da = jnp.where(lane < HEADS, dg * (-e) * _sigmoid(xx), 0.0)
        pa = jnp.sum(jnp.where(lane < HEADS, dg * g, 0.0), axis=0, keepdims=True)
        pd = jnp.sum(da, axis=0, keepdims=True)

        @pl.when(t == 0)
        def _():
            dalog_ref[...] = pa
            ddtb_ref[...] = pd

        @pl.when(t > 0)
        def _():
            dalog_ref[...] += pa
            ddtb_ref[...] += pd

        sb = _sigmoid(ab)
        dab_ref[:, :128] = jnp.where(lane < HEADS, da, dbeta * sb * (1.0 - sb)).astype(BF16)
        dab_ref[:, 128:] = jnp.zeros((tt, 128), BF16)

    vec = pl.BlockSpec((1, 128), lambda t: (0, 0))
    hb = pl.BlockSpec((HEADS, tt, HD), lambda t: (0, t, 0))
    return pl.pallas_call(
        body, name="gdn_pre_bwd", grid=(T // tt,),
        in_specs=[hb, hb, hb, pl.BlockSpec((tt, 3 * GDN_W), lambda t: (t, 0)), hb, hb,
                  pl.BlockSpec((tt, 128), lambda t: (t, C_AB // 128)), vec, vec, pl.BlockSpec(memory_space=pl.ANY)],
        out_specs=[pl.BlockSpec((tt, 3 * GDN_W), lambda t: (t, 0)),
                   pl.BlockSpec((tt, 256), lambda t: (t, C_AB // 256)), vec, vec],
        out_shape=[jax.ShapeDtypeStruct((T, 3 * GDN_W), F32), jax.ShapeDtypeStruct(dproj.shape, dproj.dtype),
                   jax.ShapeDtypeStruct((1, 128), F32), jax.ShapeDtypeStruct((1, 128), F32)],
        input_output_aliases={9: 1},
        compiler_params=_cp(("arbitrary",)))(dq, dk, dv, cpre, dbb, dgb, proj, alog, dtb, dproj)


def _split2(x):
    hi = x.astype(BF16)
    return hi, (x - hi.astype(F32)).astype(BF16)


def _bmm3s(a2, b2, ca=2, cb=1):
    def f(x, y):
        return lax.dot_general(x, y, (((ca,), (cb,)), ((0,), (0,))), preferred_element_type=F32)

    return f(a2[0], b2[0]) + (f(a2[0], b2[1]) + f(a2[1], b2[0]))


def _bmm3(a, b, ca=2, cb=1):
    return _bmm3s(_split2(a), _split2(b), ca, cb)


def _tri_inv(a):
    nb = a.shape[0]
    ri = lax.broadcasted_iota(jnp.int32, (nb, CHUNK, CHUNK), 1)
    ci = lax.broadcasted_iota(jnp.int32, (nb, CHUNK, CHUNK), 2)
    n = -a
    p = jnp.where(ri == ci, 1.0, 0.0) + n
    n2 = _split2(n)
    for _ in range(5):
        n2 = _split2(_bmm3s(n2, n2))
        p = p + _bmm3s(_split2(p), n2)
    return p


def _gdn_chunk_common(q, k, v, bb3, gb3, tm_saved=None):
    nb = q.shape[0]
    need_t = tm_saved is not None
    beta = bb3[:, :, 0:1]
    gcol = gb3[:, :, 0:1]
    bcol = bb3[:, :, :CHUNK]
    gcm = gb3[:, :, :CHUNK]
    oh = jnp.where(lax.broadcasted_iota(jnp.int32, (nb, CHUNK, HD), 2) == 0, 1.0, 0.0)
    grow = _bmm(oh, gb3, 2, 2, HI, False)
    ri = lax.broadcasted_iota(jnp.int32, (nb, CHUNK, CHUNK), 1)
    ci = lax.broadcasted_iota(jnp.int32, (nb, CHUNK, CHUNK), 2)
    tril, stl = ri >= ci, ri > ci
    dg = gcm - grow
    dec = jnp.where(tril, jnp.exp(jnp.where(tril, dg, 0.0)), 0.0)
    kk = _bmm(k, k, 2, 2)
    qk = _bmm(q, k, 2, 2)
    tm = tm_saved if need_t else _tri_inv(jnp.where(stl, bcol * kk * dec, 0.0))
    gam = jnp.exp(gcol)
    glast = gb3[:, CHUNK - 1:CHUNK, 0:1]
    egl = jnp.exp(glast)
    rw = k * (beta * gam)
    ru = v * beta
    wu = _bmm3(tm, jnp.concatenate([rw, ru], axis=2), 2, 1)
    kdf = jnp.exp(glast - gcol)
    out = dict(beta=beta, bcol=bcol, tril=tril, stl=stl, dec=dec, kk=kk, qk=qk, tm=tm, gam=gam, egl=egl,
               rw=rw, wu=wu, at=qk * dec, qd=q * gam, kdf=kdf, kd=k * kdf)
    if need_t:
        brow = _bmm(oh, bb3, 2, 2, HI, False)
        triu, stu = ri <= ci, ri < ci
        dect = jnp.where(triu, jnp.exp(jnp.where(triu, -dg, 0.0)), 0.0)
        qkt = _bmm(k, q, 2, 2)
        eye = jnp.where(ri == ci, 1.0, 0.0)
        out.update(brow=brow, triu=triu, stu=stu, dect=dect, qkt=qkt, tmt=_bmm3(eye, tm, 2, 2),
                   att=qkt * dect)
    return out


def _gdn_rows(T, fwd=False):
    return min(T, 1024 if fwd else 512)


HP = 2


def _gdn_fwd(qkv, bb, gb, proj, nw, mixed):
    T = proj.shape[0]
    R = _gdn_rows(T, fwd=True)
    nb = R // CHUNK
    B = HP * nb

    def body(q_ref, k_ref, v_ref, bb_ref, gb_ref, z_ref, nw_ref, _, y_ref, st_ref, tm_ref,
             s_ref, w_s, u_s, at_s, qd_s, kd_s):
        t = pl.program_id(1)

        @pl.when(t == 0)
        def _():
            s_ref[...] = jnp.zeros_like(s_ref)

        sh = (B, CHUNK, HD)
        q, k, v = q_ref[...].reshape(sh), k_ref[...].reshape(sh), v_ref[...].reshape(sh)
        c = _gdn_chunk_common(q, k, v, bb_ref[...].reshape(sh), gb_ref[...].reshape(sh))
        tm_ref[...] = c["tm"].reshape(HP, nb, CHUNK, CHUNK)
        w_s[...] = c["wu"][:, :, :HD]
        u_s[...] = c["wu"][:, :, HD:]
        at_s[...] = c["at"]
        qd_s[...] = c["qd"]
        kd_s[...] = c["kd"]
        egl = c["egl"]
        nwv = nw_ref[...]
        for n in range(nb):
            rows = slice(n * CHUNK, (n + 1) * CHUNK)
            for hh in range(HP):
                b = hh * nb + n
                cols = slice(hh * HD, (hh + 1) * HD)
                s = s_ref[hh]
                st_ref[hh, n] = s
                vn = u_s[b] - _mm(w_s[b], s)
                o = _mm(qd_s[b], s) + _mm(at_s[b], vn)
                s_ref[hh] = s * egl[b] + _mm(kd_s[b], vn, 0, 0)
                zz = z_ref[rows, cols]
                on = o * lax.rsqrt(jnp.mean(o * o, axis=1, keepdims=True) + EPS)
                y_ref[rows, cols] = (on * nwv * (zz * _sigmoid(zz))).astype(BF16)

    def hm(p):
        return pl.BlockSpec((None, HP, R, HD), lambda h, t: (p, h, t, 0))

    hb = pl.BlockSpec((HP, R, HD), lambda h, t: (h, t, 0))
    cs = pltpu.VMEM((B, CHUNK, HD), F32)
    return pl.pallas_call(
        body, name="gdn_fwd", grid=(HEADS // HP, T // R),
        in_specs=[hm(0), hm(1), hm(2), hb, hb,
                  pl.BlockSpec((R, HP * HD), lambda h, t: (t, C_Z // (HP * HD) + h)),
                  pl.BlockSpec((1, HD), lambda h, t: (0, 0)), pl.BlockSpec(memory_space=pl.ANY)],
        out_specs=[pl.BlockSpec((R, HP * HD), lambda h, t: (t, M_GDN // (HP * HD) + h)),
                   pl.BlockSpec((HP, nb, HD, HD), lambda h, t: (h, t, 0, 0)),
                   pl.BlockSpec((HP, nb, CHUNK, CHUNK), lambda h, t: (h, t, 0, 0))],
        out_shape=[jax.ShapeDtypeStruct(mixed.shape, mixed.dtype),
                   jax.ShapeDtypeStruct((HEADS, T // CHUNK, HD, HD), F32),
                   jax.ShapeDtypeStruct((HEADS, T // CHUNK, CHUNK, CHUNK), F32)],
        scratch_shapes=[pltpu.VMEM((HP, HD, HD), F32), cs, cs, pltpu.VMEM((B, CHUNK, CHUNK), F32), cs, cs],
        input_output_aliases={7: 0},
        compiler_params=_cp(("parallel", "arbitrary")))(qkv, qkv, qkv, bb, gb, proj, nw, mixed)


def _gdn_bwd(qkv, bb, gb, proj, nw, states, tms, dmixed, dproj):
    T = proj.shape[0]
    R = _gdn_rows(T)
    nb = R // CHUNK
    ntb = T // R
    B = HP * nb

    def by_head(ref):
        return jnp.concatenate([ref[:, hh * HD:(hh + 1) * HD].reshape(nb, CHUNK, HD) for hh in range(HP)], axis=0)

    def body(q_ref, k_ref, v_ref, bb_ref, gb_ref, z_ref, nw_ref, st_ref, tm_ref, dy_ref, _,
             dq_ref, dk_ref, dv_ref, dbb_ref, dgb_ref, dz_ref, dnw_ref,
             ds_ref, att_s, do_s, kd_s, vn_s, qd_s, w_s, dvn_s, dkd_s, dgl_s):
        hp = pl.program_id(0)
        t = pl.program_id(1)

        @pl.when(t == 0)
        def _():
            ds_ref[...] = jnp.zeros_like(ds_ref)

        @pl.when((t == 0) & (hp == 0))
        def _():
            dnw_ref[...] = jnp.zeros_like(dnw_ref)

        sh = (B, CHUNK, HD)
        q, k, v = q_ref[...].reshape(sh), k_ref[...].reshape(sh), v_ref[...].reshape(sh)
        c = _gdn_chunk_common(q, k, v, bb_ref[...].reshape(sh), gb_ref[...].reshape(sh),
                              tm_ref[...].reshape(B, CHUNK, CHUNK))
        w, u = c["wu"][:, :, :HD], c["wu"][:, :, HD:]
        sall = st_ref[...].reshape(B, HD, HD)
        vn = u - _bmm(w, sall, 2, 1)
        o = _bmm(c["qd"], sall, 2, 1) + _bmm(c["at"], vn, 2, 1)
        z = by_head(z_ref)
        dy = by_head(dy_ref)
        nwv = nw_ref[...].reshape(1, 1, HD)
        rs = lax.rsqrt(jnp.mean(o * o, axis=2, keepdims=True) + EPS)
        on = o * rs
        sg = _sigmoid(z)
        sz = z * sg
        dnw_ref[...] += jnp.sum(jnp.sum(dy * on * sz, axis=0), axis=0, keepdims=True)
        dz3 = dy * on * nwv * (sg * (1.0 + z * (1.0 - sg)))
        for hh in range(HP):
            dz_ref[:, hh * HD:(hh + 1) * HD] = dz3[hh * nb:(hh + 1) * nb].reshape(R, HD).astype(BF16)
        don = dy * nwv * sz
        do = rs * (don - on * jnp.mean(don * on, axis=2, keepdims=True))
        dqd = _bmm(do, sall, 2, 2)
        dat = jnp.where(c["tril"], _bmm(do, vn, 2, 2), 0.0)
        datt = jnp.where(c["triu"], _bmm(vn, do, 2, 2), 0.0)
        att_s[...] = c["att"]
        do_s[...] = do
        kd_s[...] = c["kd"]
        vn_s[...] = vn
        qd_s[...] = c["qd"]
        w_s[...] = w
        egl = c["egl"]
        for n in reversed(range(nb)):
            for hh in range(HP):
                b = hh * nb + n
                dso = ds_ref[hh]
                dvn_n = _mm(att_s[b], do_s[b]) + _mm(kd_s[b], dso)
                dkd_s[b] = _mm(vn_s[b], dso, 1, 1)
                dgl = egl[b] * jnp.sum(jnp.sum(st_ref[hh, n] * dso, axis=1, keepdims=True), axis=0, keepdims=True)
                dgl_s[b] = jnp.broadcast_to(dgl, (8, HD))
                ds_ref[hh] = egl[b] * dso + _mm(qd_s[b], do_s[b], 0, 0) - _mm(w_s[b], dvn_n, 0, 0)
                dvn_s[b] = dvn_n
        dvn = dvn_s[...]
        dkd = dkd_s[...]
        dgl = dgl_s[...][:, 0:1, 0:1]
        dw = -_bmm(dvn, sall, 2, 2)
        dr = _bmm3(c["tmt"], jnp.concatenate([dw, dvn], axis=2), 2, 1)
        drw, dru = dr[:, :, :HD], dr[:, :, HD:]
        wu = c["wu"]
        dr2, wu2 = _split2(dr), _split2(wu)
        da = -jnp.where(c["stl"], _bmm3s(dr2, wu2, 2, 2), 0.0)
        da_t = -jnp.where(c["stu"], _bmm3s(wu2, dr2, 2, 2), 0.0)
        beta, gam, dec, dect, kk = c["beta"], c["gam"], c["dec"], c["dect"], c["kk"]
        bcol, brow = c["bcol"], c["brow"]
        dbeta = (jnp.sum(da * kk * dec, axis=2, keepdims=True)
                 + jnp.sum(drw * k * gam + dru * v, axis=2, keepdims=True))
        dkk = bcol * da * dec
        dkk_t = brow * da_t * dect
        e = (bcol * da * kk + dat * c["qk"]) * dec
        e_t = (brow * da_t * kk + datt * c["qkt"]) * dect
        kd = c["kd"]
        dq_ref[...] = (_bmm(dat * dec, k, 2, 1) + dqd * gam).reshape(HP, R, HD)
        dk_ref[...] = (_bmm(datt * dect, q, 2, 1) + _bmm(dkk + dkk_t, k, 2, 1) + dkd * c["kdf"]
                       + drw * (beta * gam)).reshape(HP, R, HD)
        dv_ref[...] = (dru * beta).reshape(HP, R, HD)
        skd = jnp.sum(dkd * kd, axis=2, keepdims=True)
        dgc = (jnp.sum(e, axis=2, keepdims=True) - jnp.sum(e_t, axis=2, keepdims=True)
               + jnp.sum(drw * c["rw"] + dqd * c["qd"], axis=2, keepdims=True) - skd)
        tot = jnp.sum(skd, axis=1, keepdims=True) + dgl
        rowi = lax.broadcasted_iota(jnp.int32, (B, CHUNK, 1), 1)
        dgc = dgc + jnp.where(rowi == CHUNK - 1, tot, 0.0)
        dbb_ref[...] = jnp.broadcast_to(dbeta, sh).reshape(HP, R, HD)
        dgb_ref[...] = jnp.broadcast_to(dgc, sh).reshape(HP, R, HD)

    def rt(t):
        return ntb - 1 - t

    def hm(p):
        return pl.BlockSpec((None, HP, R, HD), lambda h, t: (p, h, rt(t), 0))

    hb = pl.BlockSpec((HP, R, HD), lambda h, t: (h, rt(t), 0))
    cs = pltpu.VMEM((B, CHUNK, HD), F32)
    ob = jax.ShapeDtypeStruct((HEADS, T, HD), F32)
    return pl.pallas_call(
        body, name="gdn_bwd", grid=(HEADS // HP, ntb),
        in_specs=[hm(0), hm(1), hm(2), hb, hb,
                  pl.BlockSpec((R, HP * HD), lambda h, t: (rt(t), C_Z // (HP * HD) + h)),
                  pl.BlockSpec((1, HD), lambda h, t: (0, 0)),
                  pl.BlockSpec((HP, nb, HD, HD), lambda h, t: (h, rt(t), 0, 0)),
                  pl.BlockSpec((HP, nb, CHUNK, CHUNK), lambda h, t: (h, rt(t), 0, 0)),
                  pl.BlockSpec((R, HP * HD), lambda h, t: (rt(t), M_GDN // (HP * HD) + h)),
                  pl.BlockSpec(memory_space=pl.ANY)],
        out_specs=[hb, hb, hb, hb, hb, pl.BlockSpec((R, HP * HD), lambda h, t: (rt(t), C_Z // (HP * HD) + h)),
                   pl.BlockSpec((1, HD), lambda h, t: (0, 0))],
        out_shape=[ob, ob, ob, ob, ob, jax.ShapeDtypeStruct(dproj.shape, dproj.dtype),
                   jax.ShapeDtypeStruct((1, HD), F32)],
        scratch_shapes=[pltpu.VMEM((HP, HD, HD), F32), pltpu.VMEM((B, CHUNK, CHUNK), F32), cs, cs, cs, cs, cs, cs,
                        cs, pltpu.VMEM((B, 8, HD), F32)],
        input_output_aliases={10: 5},
        compiler_params=_cp(("arbitrary", "arbitrary")))(qkv, qkv, qkv, bb, gb, proj, nw, states, tms, dmixed,
                                                         dproj)


LRU_TT = 1024


def _lru_gates(xc, wa, ba, wx, bx, lam, gpos):
    xb = xc.astype(BF16)
    r = _sigmoid(_mm(xb, wa) + ba)
    i = _sigmoid(_mm(xb, wx) + bx)
    sp = _softplus(-lam)
    log_a = -LRU_C * r * sp
    a = jnp.exp(log_a)
    mult = jnp.where(gpos == 0, 1.0, jnp.sqrt(-_expm1(2.0 * log_a)))
    return r, i, sp, a, mult


def _lru_fwd(xc, proj, wa, ba, wx, bx, lam, mixed):
    T = xc.shape[0]
    tt = min(T, LRU_TT)

    def body(xc_ref, gr_ref, wa_ref, ba_ref, wx_ref, bx_ref, lam_ref, _, y_ref, h_ref, carry_ref):
        t = pl.program_id(1)

        @pl.when(t == 0)
        def _():
            carry_ref[...] = jnp.zeros_like(carry_ref)

        row = lax.broadcasted_iota(jnp.int32, (tt, 1), 0)
        xcv = xc_ref[...]
        r, i, sp, a, mult = _lru_gates(xcv, wa_ref[...], ba_ref[...], wx_ref[...], bx_ref[...], lam_ref[...],
                                       t * tt + row)
        av, bv = a, mult * i * xcv
        k = 1
        while k < tt:
            a_s = jnp.where(row >= k, pltpu.roll(av, k, 0), 1.0)
            b_s = jnp.where(row >= k, pltpu.roll(bv, k, 0), 0.0)
            bv = bv + av * b_s
            av = av * a_s
            k *= 2
        h = bv + av * carry_ref[0:1, :]
        carry_ref[...] = jnp.broadcast_to(h[tt - 1:tt, :], (8, 128))
        h_ref[...] = h
        y_ref[...] = (h * _gelu(gr_ref[...])).astype(BF16)

    blk = pl.BlockSpec((tt, 128), lambda j, t: (t, j))
    vec = pl.BlockSpec((1, 128), lambda j, t: (0, j))
    mat = pl.BlockSpec((None, 128, 128), lambda j, t: (j, 0, 0))
    return pl.pallas_call(
        body, name="lru_fwd", grid=(LRU_W // 128, T // tt),
        in_specs=[blk, pl.BlockSpec((tt, 128), lambda j, t: (t, C_GR // 128 + j)), mat, vec, mat, vec, vec,
                  pl.BlockSpec(memory_space=pl.ANY)],
        out_specs=[pl.BlockSpec((tt, 128), lambda j, t: (t, M_LRU // 128 + j)), blk],
        out_shape=[jax.ShapeDtypeStruct(mixed.shape, mixed.dtype), jax.ShapeDtypeStruct((T, LRU_W), F32)],
        scratch_shapes=[pltpu.VMEM((8, 128), F32)], input_output_aliases={7: 0},
        compiler_params=_cp(("parallel", "arbitrary")))(xc, proj, wa, ba, wx, bx, lam, mixed)


def _lru_bwd(dmixed, xc, proj, hst, wa, ba, wx, bx, lam, dproj):
    T = xc.shape[0]
    tt = min(T, LRU_TT)
    nt = T // tt

    def body(dy_ref, xc_ref, gr_ref, h_ref, hp_ref, wa_ref, ba_ref, wx_ref, bx_ref, lam_ref, _,
             dxc_ref, dgr_ref, dwa_ref, dwx_ref, dba_ref, dbx_ref, dlam_ref, lc_ref, ac_ref):
        t = pl.program_id(1)
        tr = nt - 1 - t

        @pl.when(t == 0)
        def _():
            lc_ref[...] = jnp.zeros_like(lc_ref)
            ac_ref[...] = jnp.zeros_like(ac_ref)
            dwa_ref[...] = jnp.zeros_like(dwa_ref)
            dwx_ref[...] = jnp.zeros_like(dwx_ref)
            dba_ref[...] = jnp.zeros_like(dba_ref)
            dbx_ref[...] = jnp.zeros_like(dbx_ref)
            dlam_ref[...] = jnp.zeros_like(dlam_ref)

        row = lax.broadcasted_iota(jnp.int32, (tt, 1), 0)
        gpos = tr * tt + row
        xcv = xc_ref[...]
        wav, wxv, lamv = wa_ref[...], wx_ref[...], lam_ref[...]
        r, i, sp, a, mult = _lru_gates(xcv, wav, ba_ref[...], wxv, bx_ref[...], lamv, gpos)
        h = h_ref[...]
        dy = dy_ref[...]
        gg, dgg = _gelu_and_grad(gr_ref[...])
        dgr_ref[...] = (dy * h * dgg).astype(BF16)
        bv = dy * gg
        cv = jnp.where(row < tt - 1, pltpu.roll(a, tt - 1, 0), ac_ref[0:1, :])
        k = 1
        while k < tt:
            c_s = jnp.where(row < tt - k, pltpu.roll(cv, tt - k, 0), 1.0)
            b_s = jnp.where(row < tt - k, pltpu.roll(bv, tt - k, 0), 0.0)
            bv = bv + cv * b_s
            cv = cv * c_s
            k *= 2
        lm = bv + cv * lc_ref[0:1, :]
        lc_ref[...] = jnp.broadcast_to(lm[0:1, :], (8, 128))
        ac_ref[...] = jnp.broadcast_to(a[0:1, :], (8, 128))
        hp = jnp.where(tr == 0, 0.0, hp_ref[...])
        hs = pltpu.roll(jnp.concatenate([hp, h], axis=0), 1, 0)[8:, :]
        da = lm * hs
        dmult = lm * i * xcv
        di = lm * mult * xcv
        dxc = lm * mult * i
        dlog_a = a * da - jnp.where(gpos == 0, 0.0, dmult * a * a / mult)
        dr = dlog_a * (-LRU_C * sp)
        dsp = jnp.sum(dlog_a * (-LRU_C * r), axis=0, keepdims=True)
        dlam_ref[...] += dsp * (-_sigmoid(-lamv))
        dpr = dr * r * (1.0 - r)
        dpi = di * i * (1.0 - i)
        dba_ref[...] += jnp.sum(dpr, axis=0, keepdims=True)
        dbx_ref[...] += jnp.sum(dpi, axis=0, keepdims=True)
        dwa_ref[...] += _mm(xcv, dpr, 0, 0)
        dwx_ref[...] += _mm(xcv, dpi, 0, 0)
        dxc_ref[...] = dxc + _mm(dpr, wav, 1, 1) + _mm(dpi, wxv, 1, 1)

    def rt(t):
        return nt - 1 - t

    blk = pl.BlockSpec((tt, 128), lambda j, t: (rt(t), j))
    vec = pl.BlockSpec((1, 128), lambda j, t: (0, j))
    mat = pl.BlockSpec((None, 128, 128), lambda j, t: (j, 0, 0))
    h8 = tt // 8
    mshape = jax.ShapeDtypeStruct((LRU_W // 128, 128, 128), F32)
    vshape = jax.ShapeDtypeStruct((1, LRU_W), F32)
    return pl.pallas_call(
        body, name="lru_bwd", grid=(LRU_W // 128, nt),
        in_specs=[pl.BlockSpec((tt, 128), lambda j, t: (rt(t), M_LRU // 128 + j)), blk,
                  pl.BlockSpec((tt, 128), lambda j, t: (rt(t), C_GR // 128 + j)), blk,
                  pl.BlockSpec((8, 128), lambda j, t: (jnp.maximum(rt(t) * h8 - 1, 0), j)),
                  mat, vec, mat, vec, vec, pl.BlockSpec(memory_space=pl.ANY)],
        out_specs=[blk, pl.BlockSpec((tt, 128), lambda j, t: (rt(t), C_GR // 128 + j)), mat, mat, vec, vec, vec],
        out_shape=[jax.ShapeDtypeStruct((T, LRU_W), F32), jax.ShapeDtypeStruct(dproj.shape, dproj.dtype),
                   mshape, mshape, vshape, vshape, vshape],
        scratch_shapes=[pltpu.VMEM((8, 128), F32), pltpu.VMEM((8, 128), F32)], input_output_aliases={10: 1},
        compiler_params=_cp(("parallel", "arbitrary")))(dmixed, xc, proj, hst, hst, wa, ba, wx, bx, lam, dproj)


FFN_CB = 512
FFN_K = 3


def _ffn_conv(ge, w_ref):
    acc = ge * w_ref[FFN_K - 1:FFN_K, :]
    for j in range(FFN_K - 1):
        acc = acc + pltpu.roll(ge, FFN_K - 1 - j, 0) * w_ref[j:j + 1, :]
    return acc


FFN_HALO = 16
FFN_TT = 2048


def _ffn_gate_fwd(up, w):
    T = up.shape[0]
    tt = min(T, FFN_TT)
    cb = FFN_CB
    nc = D_FF // cb
    hh = tt // FFN_HALO

    def body(g_ref, gp_ref, v_ref, w_ref, o_ref, ot_ref):
        t = pl.program_id(1)
        prev = jnp.where(t == 0, 0.0, gp_ref[...].astype(F32))
        ge = jnp.concatenate([prev, g_ref[...].astype(F32)], axis=0)
        gc = _ffn_conv(ge, w_ref)[FFN_HALO:, :]
        a = _gelu(gc) * v_ref[...].astype(F32)
        o_ref[...] = a.astype(BF16)
        ot_ref[...] = a.T.astype(BF16)

    return pl.pallas_call(
        body, name="ffn_gate_fwd", grid=(nc, T // tt),
        in_specs=[pl.BlockSpec((tt, cb), lambda j, t: (t, j)),
                  pl.BlockSpec((FFN_HALO, cb), lambda j, t: (jnp.maximum(t * hh - 1, 0), j)),
                  pl.BlockSpec((tt, cb), lambda j, t: (t, nc + j)),
                  pl.BlockSpec((FFN_K, cb), lambda j, t: (0, j))],
        out_specs=[pl.BlockSpec((tt, cb), lambda j, t: (t, j)), pl.BlockSpec((cb, tt), lambda j, t: (j, t))],
        out_shape=[jax.ShapeDtypeStruct((T, D_FF), BF16), jax.ShapeDtypeStruct((D_FF, T), BF16)],
        compiler_params=_cp(("parallel", "parallel")))(up, up, up, w)


def _ffn_gate_bwd(dact, up, w):
    T = up.shape[0]
    tt = min(T, FFN_TT)
    cb = FFN_CB
    nc = D_FF // cb
    nt = T // tt
    hh = tt // FFN_HALO
    H = FFN_HALO

    def body(d_ref, dn_ref, g_ref, gp_ref, gn_ref, v_ref, vn_ref, w_ref, dup_ref, dw_ref):
        t = pl.program_id(1)
        prev = jnp.where(t == 0, 0.0, gp_ref[...].astype(F32))
        ge = jnp.concatenate([prev, g_ref[...].astype(F32), gn_ref[...].astype(F32)], axis=0)
        gc = _ffn_conv(ge, w_ref)[H:, :]
        gg, dgg = _gelu_and_grad(gc)
        de = jnp.concatenate([d_ref[...].astype(F32), jnp.where(t == nt - 1, 0.0, dn_ref[...].astype(F32))], axis=0)
        ve = jnp.concatenate([v_ref[...].astype(F32), vn_ref[...].astype(F32)], axis=0)
        dup_ref[1] = (de * gg)[:tt, :].astype(BF16)
        dgc = de * ve * dgg
        n = tt + H
        acc = dgc * w_ref[FFN_K - 1:FFN_K, :]
        for j in range(FFN_K - 1):
            acc = acc + pltpu.roll(dgc, n - (FFN_K - 1 - j), 0) * w_ref[j:j + 1, :]
        dup_ref[0] = acc[:tt, :].astype(BF16)

        @pl.when(t == 0)
        def _():
            dw_ref[...] = jnp.zeros_like(dw_ref)

        dgm = dgc[:tt, :]
        for j in range(FFN_K):
            sh = FFN_K - 1 - j
            xs = ge[H:H + tt, :] if sh == 0 else pltpu.roll(ge, sh, 0)[H:H + tt, :]
            dw_ref[j:j + 1, :] += jnp.sum(dgm * xs, axis=0, keepdims=True)

    def nxt(t):
        return jnp.minimum((t + 1) * hh, T // H - 1)

    return pl.pallas_call(
        body, name="ffn_gate_bwd", grid=(nc, nt),
        in_specs=[pl.BlockSpec((tt, cb), lambda j, t: (t, j)),
                  pl.BlockSpec((H, cb), lambda j, t: (nxt(t), j)),
                  pl.BlockSpec((tt, cb), lambda j, t: (t, j)),
                  pl.BlockSpec((H, cb), lambda j, t: (jnp.maximum(t * hh - 1, 0), j)),
                  pl.BlockSpec((H, cb), lambda j, t: (nxt(t), j)),
                  pl.BlockSpec((tt, cb), lambda j, t: (t, nc + j)),
                  pl.BlockSpec((H, cb), lambda j, t: (nxt(t), nc + j)),
                  pl.BlockSpec((FFN_K, cb), lambda j, t: (0, j))],
        out_specs=[pl.BlockSpec((2, tt, cb), lambda j, t: (0, t, j)),
                   pl.BlockSpec((FFN_K, cb), lambda j, t: (0, j))],
        out_shape=[jax.ShapeDtypeStruct((2, T, D_FF), BF16), jax.ShapeDtypeStruct((FFN_K, D_FF), F32)],
        compiler_params=_cp(("parallel", "arbitrary")))(dact, dact, up, up, up, up, up, w)


def _row_tile(rows, cap):
    best = 8
    for r in range(8, min(rows, cap) + 1, 8):
        if rows % r == 0:
            best = r
    return best


def _adamw(parts, w, m, v, rt, name, layer=None, prev=None):
    P, R, C = parts.shape
    cw = w.shape[-1]

    def body(p_ref, w_ref, m_ref, v_ref, *rest):
        g_ref, d_ref, mo_ref, vo_ref = rest[-4:]
        g = p_ref[0].astype(F32)
        for i in range(1, P):
            g = g + p_ref[i].astype(F32)
        if cw != C:
            g = jnp.concatenate([g[:, :AB_ORIG_END], g[:, C_XR:]], axis=1)
        wv = w_ref[...]
        mn = ADAM_B1 * m_ref[...] + (1.0 - ADAM_B1) * g
        vn = ADAM_B2 * v_ref[...] + (1.0 - ADAM_B2) * (g * g)
        m_hat = mn / (1.0 - ADAM_B1 ** ADAM_STEP)
        v_hat = vn / (1.0 - ADAM_B2 ** ADAM_STEP)
        g_ref[...] = g
        d_ref[...] = -ADAM_LR * (m_hat / (jnp.sqrt(v_hat) + ADAM_EPS) + ADAM_WD * wv)
        mo_ref[...] = mn
        vo_ref[...] = vn

    if layer is None:
        blk = pl.BlockSpec((rt, cw), lambda r: (r, 0))
        sh = jax.ShapeDtypeStruct((R, cw), F32)
    else:
        blk = pl.BlockSpec((None, rt, cw), lambda r: (layer, r, 0))
        sh = jax.ShapeDtypeStruct(w.shape, F32)
    extra = list(prev) if prev is not None else []
    return pl.pallas_call(
        body, name=name, grid=(R // rt,),
        in_specs=[pl.BlockSpec((P, rt, C), lambda r: (0, r, 0)), blk, blk, blk]
        + [pl.BlockSpec(memory_space=pl.ANY)] * len(extra),
        out_specs=[blk, blk, blk, blk], out_shape=[sh, sh, sh, sh],
        input_output_aliases={4 + i: i for i in range(len(extra))},
        compiler_params=_cp(("parallel",)))(parts, w, m, v, *extra)


def _peer(k):
    x, y, c = lax.axis_index("x"), lax.axis_index("y"), lax.axis_index("c")
    px = 1 - x if k & 4 else x
    py = 1 - y if k & 2 else y
    pc = 1 - c if k & 1 else c
    return (px, py, pc), 4 * px + 2 * py + pc


def _all_gather(x, name):
    R, C = x.shape

    def body(x_ref, o_ref, send_sems, recv_sems, local_sem):
        me = 4 * lax.axis_index("x") + 2 * lax.axis_index("y") + lax.axis_index("c")
        mine = pltpu.make_async_copy(x_ref, o_ref.at[me], local_sem)
        mine.start()
        sends = []
        for k in range(1, N_DEV):
            dev, _ = _peer(k)
            cp = pltpu.make_async_remote_copy(src_ref=x_ref, dst_ref=o_ref.at[me], send_sem=send_sems.at[k - 1],
                                              recv_sem=recv_sems.at[k - 1], device_id=dev, device_id_type=MESH_IDS)
            cp.start()
            sends.append(cp)
        for k in range(1, N_DEV):
            dev, idx = _peer(k)
            pltpu.make_async_remote_copy(src_ref=x_ref, dst_ref=o_ref.at[idx], send_sem=send_sems.at[k - 1],
                                         recv_sem=recv_sems.at[k - 1], device_id=dev,
                                         device_id_type=MESH_IDS).wait_recv()
        for cp in sends:
            cp.wait_send()
        mine.wait()

    return pl.pallas_call(
        body, name=name, in_specs=[pl.BlockSpec(memory_space=pl.ANY)], out_specs=pl.BlockSpec(memory_space=pl.ANY),
        out_shape=jax.ShapeDtypeStruct((N_DEV, R, C), x.dtype),
        scratch_shapes=[pltpu.SemaphoreType.DMA((N_DEV - 1,)), pltpu.SemaphoreType.DMA((N_DEV - 1,)),
                        pltpu.SemaphoreType.DMA],
        compiler_params=pltpu.CompilerParams(has_side_effects=True))(x)


HBM_SPEC = pl.BlockSpec(memory_space=pltpu.HBM)
SEM_SPEC = pl.BlockSpec(memory_space=pltpu.SEMAPHORE)
EFFECT = pltpu.SideEffectType.DATAFLOW_SIDE_EFFECTING
OTHER_CHIPS = ((1, 0), (0, 1), (1, 1))


def _split_start(bufs, plan, n, name):
    nb = len(bufs)

    def body(*refs):
        send_sems, recv_sems, token = refs[nb], refs[nb + 1], refs[2 * nb + 2]
        for i, (src, dst, _, dev) in enumerate(plan(refs[:nb])):
            pltpu.make_async_remote_copy(src_ref=src, dst_ref=dst, send_sem=send_sems.at[i],
                                         recv_sem=recv_sems.at[i], device_id=dev, device_id_type=MESH_IDS).start()
        token[...] = jnp.zeros_like(token)

    outs = pl.pallas_call(
        body, name=name,
        out_shape=(pltpu.SemaphoreType.DMA((n,)), pltpu.SemaphoreType.DMA((n,)),
                   *[pltpu.HBM(b.shape, b.dtype) for b in bufs], jax.ShapeDtypeStruct((8, 128), F32)),
        in_specs=[HBM_SPEC] * nb,
        out_specs=(SEM_SPEC, SEM_SPEC, *[HBM_SPEC] * nb, pl.BlockSpec(memory_space=pltpu.VMEM)),
        input_output_aliases={i: 2 + i for i in range(nb)},
        compiler_params=pltpu.CompilerParams(has_side_effects=EFFECT),
    )(*[pltpu.with_memory_space_constraint(b, pltpu.HBM) for b in bufs])
    return dict(send=outs[0], recv=outs[1], bufs=list(outs[2:2 + nb]), token=outs[2 + nb], plan=plan, n=n)


def _split_wait(st, after, name):
    bufs = st["bufs"]
    nb = len(bufs)
    plan = st["plan"]
    afters = list(after) if isinstance(after, (list, tuple)) else [after]

    def body(*refs):
        send_sems, recv_sems = refs[nb], refs[nb + 1]
        for i, (src, dst, land, dev) in enumerate(plan(refs[:nb])):
            pltpu.make_async_remote_copy(src_ref=src, dst_ref=dst, send_sem=send_sems.at[i],
                                         recv_sem=recv_sems.at[i], device_id=dev,
                                         device_id_type=MESH_IDS).wait_send()
            pltpu.make_async_remote_copy(src_ref=src, dst_ref=land, send_sem=send_sems.at[i],
                                         recv_sem=recv_sems.at[i], device_id=dev,
                                         device_id_type=MESH_IDS).wait_recv()

    outs = pl.pallas_call(
        body, name=name, out_shape=tuple(pltpu.HBM(b.shape, b.dtype) for b in bufs),
        in_specs=[HBM_SPEC] * nb + [SEM_SPEC, SEM_SPEC] + [pl.BlockSpec(memory_space=pl.ANY)] * len(afters),
        out_specs=tuple([HBM_SPEC] * nb), input_output_aliases={i: i for i in range(nb)},
        compiler_params=pltpu.CompilerParams(has_side_effects=EFFECT),
    )(*bufs, st["send"], st["recv"], *afters)
    return list(outs)


def _xyc():
    return lax.axis_index("x"), lax.axis_index("y"), lax.axis_index("c")


def _flip(x, y, a, b):
    return (1 - x if a else x), (1 - y if b else y)


def _ag1_plan(outs):
    x, y, c = _xyc()
    me = 4 * x + 2 * y + c
    copies = []
    for o in outs:
        copies.append((o.at[me], o.at[me], o.at[4 * x + 2 * y + 1 - c], (x, y, 1 - c)))
        for a, b in OTHER_CHIPS:
            px, py = _flip(x, y, a, b)
            copies.append((o.at[me], o.at[me], o.at[4 * px + 2 * py + c], (px, py, c)))
    return copies


def _ag2_plan(outs):
    x, y, c = _xyc()
    copies = []
    for o in outs:
        for a, b in OTHER_CHIPS:
            px, py = _flip(x, y, a, b)
            mine, sibs = 4 * px + 2 * py + c, 4 * px + 2 * py + 1 - c
            copies.append((o.at[mine], o.at[mine], o.at[sibs], (x, y, 1 - c)))
    return copies


def _rs1_plan(refs):
    x, y, c = _xyc()
    copies = []
    for g, land in zip(refs[0::2], refs[1::2]):
        for j in range(4):
            copies.append((g.at[2 * j + 1 - c], land.at[j], land.at[j], (x, y, 1 - c)))
    return copies


def _rs2_plan(refs):
    x, y, c = _xyc()
    mychip = 2 * x + y
    copies = []
    for s, land in zip(refs[0::2], refs[1::2]):
        for a, b in OTHER_CHIPS:
            px, py = _flip(x, y, a, b)
            copies.append((s.at[2 * px + py], land.at[mychip], land.at[2 * px + py], (px, py, c)))
    return copies


def _landing_like(g, name):
    def body(g_ref, o_ref):
        del g_ref, o_ref

    anyspec = pl.BlockSpec(memory_space=pl.ANY)
    return pl.pallas_call(body, name=name, in_specs=[anyspec], out_specs=anyspec,
                          out_shape=jax.ShapeDtypeStruct((4,) + g.shape[1:], g.dtype))(g)


def _place(x, slots, by_chip, name):
    R, C = x.shape[-2:]
    rt = _row_tile(R, 512)
    xi, yi, ci = _xyc()
    idx = (2 * xi + yi if by_chip else 4 * xi + 2 * yi + ci).astype(jnp.int32).reshape(1)

    def body(i_ref, x_ref, o_ref):
        o_ref[...] = x_ref[...]

    if by_chip:
        in_spec = pl.BlockSpec((None, rt, C), lambda r, i: (i[0], r, 0))
    else:
        in_spec = pl.BlockSpec((rt, C), lambda r, i: (r, 0))
    grid_spec = pltpu.PrefetchScalarGridSpec(
        num_scalar_prefetch=1, grid=(R // rt,), in_specs=[in_spec],
        out_specs=pl.BlockSpec((None, rt, C), lambda r, i: (i[0], r, 0)))
    return pl.pallas_call(body, name=name, grid_spec=grid_spec,
                          out_shape=jax.ShapeDtypeStruct((slots, R, C), x.dtype),
                          compiler_params=_cp(("parallel",)))(idx, x)


def _pair_sum(g, land, cidx, name):
    _, R, C = land.shape
    rt = _row_tile(R, 512)
    g4 = g.reshape(4, 2, R, C)

    def body(c_ref, g_ref, l_ref, o_ref):
        o_ref[...] = (g_ref[...].astype(F32) + l_ref[...].astype(F32)).astype(o_ref.dtype)

    grid_spec = pltpu.PrefetchScalarGridSpec(
        num_scalar_prefetch=1, grid=(4, R // rt),
        in_specs=[pl.BlockSpec((None, None, rt, C), lambda j, r, c_ref: (j, c_ref[0], r, 0)),
                  pl.BlockSpec((None, rt, C), lambda j, r, c_ref: (j, r, 0))],
        out_specs=pl.BlockSpec((None, rt, C), lambda j, r, c_ref: (j, r, 0)))
    return pl.pallas_call(body, name=name, grid_spec=grid_spec, out_shape=jax.ShapeDtypeStruct(land.shape, land.dtype),
                          compiler_params=_cp(("parallel", "parallel")))(cidx, g4, land)


def _no_hook(event, l, after, payload=None):
    return None


def _tie(x, token):
    if token is None:
        return x

    def body(x_ref, t_ref, o_ref):
        del x_ref, t_ref, o_ref

    anyspec = pl.BlockSpec(memory_space=pl.ANY)
    return pl.pallas_call(body, name="tie", in_specs=[anyspec, anyspec], out_specs=anyspec,
                          out_shape=jax.ShapeDtypeStruct(x.shape, x.dtype), input_output_aliases={0: 0})(x, token)


def _layer_fwd(x, W, l, hook=_no_hook, pre=None):
    T = x.shape[0]
    n = f"l{l}_"
    h1, h1t = pre if pre is not None else _norm_fwd(x, W["norm1"], n + "norm1_fwd")
    proj = _mm_nn(h1, W["win"], F32, n + "mm_in", tn_c=(1792,))
    y_pool = _pool_fwd(proj, W["pool_w"], W["pool_b"], _tie(W["pool_s"], hook("f_in", l, proj)))
    cpre = _conv_fwd(proj, C_QKV, 3 * GDN_W, W["gconv_w"], None, 256, n + "gdn_conv_fwd")
    qkv, bb, gb = _gdn_pre_fwd(cpre, proj, W["alog"], W["dtb"])
    mixed, states, tms = _gdn_fwd(qkv, bb, gb, proj, W["gnorm"], y_pool)
    lconv_w = _tie(W["lconv_w"], hook("f_mix", l, states))
    xc = _conv_fwd(proj, C_XR, LRU_W, lconv_w, W["lconv_b"], 128, n + "lru_conv_fwd")
    mixed, hst = _lru_fwd(xc, proj, W["wa"], W["ba"], W["wx"], W["bx"], W["lam"], mixed)
    mixed = _tie(mixed, hook("f_out", l, hst))
    x1 = _mm_nn(mixed, W["wout"], F32, n + "mm_out", add=x)
    h2, h2t = _norm_fwd(x1, W["norm2"], n + "norm2_fwd")
    h2 = _tie(h2, hook("f_n2", l, h2t))
    up = _mm_up(h2, W["wup"], n + "mm_up")
    act, act_t = _ffn_gate_fwd(up, _tie(W["fconv_w"], hook("f_up", l, up)))
    act = _tie(act, hook("f_act", l, act_t))
    x2 = _mm_nn(act, W["wdown"], F32, n + "mm_down", add=x1, tk_c=(3072, 2048, 1536, 1024, 512, 256))
    hook("f_end", l, x2)
    saved = dict(x=x, h1t=h1t, proj=proj, cpre=cpre, qkv=qkv, bb=bb, gb=gb, states=states, tms=tms, xc=xc, hst=hst,
                 mixed=mixed, x1=x1, h2t=h2t, up=up, act_t=act_t)
    return x2, saved


def _layer_bwd(dx2, dx2b, W, S, l, hook=_no_hook):
    T = dx2.shape[0]
    n = f"l{l}_"
    dact = _mm_nt(dx2b, W["wdown"], BF16, n + "mm_down_dx", tk_c=(2048,), tn_c=(1536,))
    g_wdown = _mm_nn(S["act_t"], dx2b, BF16, n + "mm_down_dw", tn_c=(2048,))
    dup, g_fconv = _ffn_gate_bwd(dact, S["up"], W["fconv_w"])
    ns = W["wup"].shape[2]
    dh2 = _mm_up_t(dup, W["wup"], n + "mm_up_dx")
    g_wup = _mm_dup(S["h2t"], dup, ns, n + "mm_up_dw")
    tok = hook("b_ffn", l, g_wup, dict(ffn_down=g_wdown, ffn_up=g_wup))
    dx1, dx1b, g_norm2 = _norm_bwd(S["x1"], _tie(W["norm2"], tok), dh2, dx2, n + "norm2_bwd")
    dmixed = _mm_nt(dx1b, W["wout"], F32, n + "mm_out_dx", tk_c=(2048,), tn_c=(2048,))
    g_wout = _mm_tn(S["mixed"], dx1b, BF16, n + "mm_out_dw", tn_c=(2048,))
    tok = hook("b_mid", l, g_wout)
    proj = S["proj"]
    dproj, g_pool_w, g_pool_b, g_pool_s = _pool_bwd(dmixed, proj, W["pool_w"], W["pool_b"], _tie(W["pool_s"], tok))
    dq, dk, dv, dbb, dgb, dproj, g_gnorm = _gdn_bwd(S["qkv"], S["bb"], S["gb"], proj, W["gnorm"], S["states"],
                                                    S["tms"], dmixed, dproj)
    dc, dproj, g_alog, g_dtb = _gdn_pre_bwd(dq, dk, dv, S["cpre"], dbb, dgb, proj, W["alog"], W["dtb"], dproj)
    dproj, g_gconv = _conv_bwd(dc, proj, C_QKV, W["gconv_w"], 256, n + "gdn_conv_bwd", dproj)
    dxc, dproj, g_wa, g_wx, g_ba, g_bx, g_lam = _lru_bwd(dmixed, S["xc"], proj, S["hst"], W["wa"], W["ba"], W["wx"],
                                                          W["bx"], W["lam"], dproj)
    dproj, g_lconv, g_lconv_b = _conv_bwd(dxc, proj, C_XR, W["lconv_w"], 128, n + "lru_conv_bwd", dproj,
                                          want_db=True)
    dh1 = _mm_nt(dproj, W["win"], F32, n + "mm_in_dx", tk_c=(1792,))
    g_win = _mm_nn(S["h1t"], dproj, BF16, n + "mm_in_dw", tn_c=(1792,))
    tok = hook("b_in", l, g_win, dict(w_out=g_wout, w_in=g_win))
    dx, dxb, g_norm1 = _norm_bwd(S["x"], _tie(W["norm1"], tok), dh1, dx1, n + "norm1_bwd")
    big = dict(w_in=g_win, w_out=g_wout, ffn_up=g_wup, ffn_down=g_wdown)
    small = dict(norm1_w=g_norm1[0], pool_w=g_pool_w, pool_b=g_pool_b.reshape(4, 128), pool_scale=g_pool_s[0],
                 gdn_conv_w=g_gconv, gdn_a_log=g_alog[0, :HEADS], gdn_dt_bias=g_dtb[0, :HEADS],
                 gdn_norm_w=g_gnorm[0], lru_conv_w=g_lconv, lru_conv_b=g_lconv_b[0], lru_wa=g_wa, lru_ba=g_ba[0],
                 lru_wx=g_wx, lru_bx=g_bx[0], lru_lambda=g_lam[0], norm2_w=g_norm2[0], ffn_conv_w=g_fconv)
    dxb = _tie(dxb, hook("b_end", l, dx, small))
    return dx, dxb, big, small


def _pad_lane(v):
    return jnp.pad(v, (0, 128 - v.shape[0])).reshape(1, 128)


def _layer_weights(l, big, P, conv_full):
    return dict(
        win=big.get("w_in"), wout=big.get("w_out"), wup=big.get("ffn_up"), wdown=big.get("ffn_down"),
        norm1=P["norm1_w"][l].reshape(1, D_MODEL), norm2=P["norm2_w"][l].reshape(1, D_MODEL),
        pool_w=P["pool_w"][l], pool_b=P["pool_b"][l].reshape(1, POOL_W), pool_s=P["pool_scale"][l].reshape(1, POOL_W),
        gconv_w=conv_full["gdn_conv_w"][l], alog=_pad_lane(P["gdn_a_log"][l]), dtb=_pad_lane(P["gdn_dt_bias"][l]),
        gnorm=P["gdn_norm_w"][l].reshape(1, HD),
        lconv_w=conv_full["lru_conv_w"][l], lconv_b=P["lru_conv_b"][l].reshape(1, LRU_W),
        wa=P["lru_wa"][l], ba=P["lru_ba"][l].reshape(1, LRU_W), wx=P["lru_wx"][l],
        bx=P["lru_bx"][l].reshape(1, LRU_W), lam=P["lru_lambda"][l].reshape(1, LRU_W),
        fconv_w=conv_full["ffn_conv_w"][l])


def _local_step(x, target, Ws, final_norm_w, hook=_no_hook, pre0=None):
    saved = []
    for l in range(DEPTH):
        x, s = _layer_fwd(x, Ws[l], l, hook, pre0 if l == 0 else None)
        saved.append(s)
    loss, dx, dxb, g_final = _loss_head(x, final_norm_w.reshape(1, D_MODEL), target)
    bigs, smalls = [None] * DEPTH, [None] * DEPTH
    for l in reversed(range(DEPTH)):
        dx, dxb, bigs[l], smalls[l] = _layer_bwd(dx, dxb, Ws[l], saved[l], l, hook)
    return loss, dx, g_final[0], bigs, smalls


SMALL_REPL = ("norm1_w", "pool_w", "pool_b", "pool_scale", "gdn_a_log", "gdn_dt_bias", "gdn_norm_w", "lru_conv_b",
              "lru_wa", "lru_ba", "lru_wx", "lru_bx", "lru_lambda", "norm2_w", "final_norm_w")
SMALL_SHARD = ("gdn_conv_w", "lru_conv_w", "ffn_conv_w")
BIG = ("w_in", "w_out", "ffn_up", "ffn_down")
WEIGHTS = ("norm1_w", "w_in", "pool_w", "pool_b", "pool_scale", "gdn_conv_w", "gdn_a_log", "gdn_dt_bias",
           "gdn_norm_w", "lru_conv_w", "lru_conv_b", "lru_wa", "lru_ba", "lru_wx", "lru_bx", "lru_lambda", "w_out",
           "norm2_w", "ffn_up", "ffn_conv_w", "ffn_down", "final_norm_w")
SEG = 1024
PACK_ROWS_MULT = 256 * 128


def _pack(arrs):
    pieces, table, off = [], [], 0
    for a in arrs:
        n = a.size
        npad = -(-n // SEG) * SEG
        pieces.append(jnp.pad(a.reshape(-1).astype(F32), (0, npad - n)))
        table.append((off, n, a.shape))
        off += npad
    tail = -off % PACK_ROWS_MULT
    if tail:
        pieces.append(jnp.zeros((tail,), F32))
        off += tail
    return jnp.concatenate(pieces).reshape(off // 128, 128), table


def _unpack(buf, table):
    flat = buf.reshape(-1)
    return [flat[off:off + n].reshape(shape) for off, n, shape in table]


def _pad_in(w):
    z1 = jnp.zeros(w.shape[:-1] + (C_XR - AB_ORIG_END,), w.dtype)
    return jnp.concatenate([w[..., :AB_ORIG_END], z1, w[..., AB_ORIG_END:]], axis=-1)


def _unpad_in(w):
    return jnp.concatenate([w[..., :AB_ORIG_END], w[..., C_XR:C_GR + LRU_W]], axis=-1)


def kernel(x, norm1_w, w_in, pool_w, pool_b, pool_scale, gdn_conv_w, gdn_a_log, gdn_dt_bias, gdn_norm_w, lru_conv_w, lru_conv_b, lru_wa, lru_ba, lru_wx, lru_bx, lru_lambda, w_out, norm2_w, ffn_up, ffn_conv_w, ffn_down, final_norm_w, loss_target, m_norm1_w, m_w_in, m_pool_w, m_pool_b, m_pool_scale, m_gdn_conv_w, m_gdn_a_log, m_gdn_dt_bias, m_gdn_norm_w, m_lru_conv_w, m_lru_conv_b, m_lru_wa, m_lru_ba, m_lru_wx, m_lru_bx, m_lru_lambda, m_w_out, m_norm2_w, m_ffn_up, m_ffn_conv_w, m_ffn_down, m_final_norm_w, v_norm1_w, v_w_in, v_pool_w, v_pool_b, v_pool_scale, v_gdn_conv_w, v_gdn_a_log, v_gdn_dt_bias, v_gdn_norm_w, v_lru_conv_w, v_lru_conv_b, v_lru_wa, v_lru_ba, v_lru_wx, v_lru_bx, v_lru_lambda, v_w_out, v_norm2_w, v_ffn_up, v_ffn_conv_w, v_ffn_down, v_final_norm_w):
    loc = dict(locals())
    Wp = {n: loc[n] for n in WEIGHTS}
    Mp = {n: loc["m_" + n] for n in WEIGHTS}
    Vp = {n: loc["v_" + n] for n in WEIGHTS}
    xi, yi, ci = _xyc()
    me = 4 * xi + 2 * yi + ci
    mychip = 2 * xi + yi
    cidx = ci.astype(jnp.int32).reshape(1)
    keys = dict(w_in="win", w_out="wout", ffn_up="wup", ffn_down="wdown")

    def shard2d(d, name, l):
        a = d[name][l]
        return _pad_in(a) if name == "w_in" else a

    def wshard(l, name):
        return shard2d(Wp, name, l).astype(BF16)

    def full2d(name, full):
        return full if name == "ffn_up" else full.reshape(-1, full.shape[2])

    def ag_start(shards, tag, token=None):
        if token is not None:
            shards = [_tie(shards[0], token)] + list(shards[1:])
        bufs = [_place(s, N_DEV, False, f"place_{tag}{i}") for i, s in enumerate(shards)]
        return _split_start(bufs, _ag1_plan, 4 * len(bufs), f"ag1s_{tag}")

    def ag_mid(st, after, tag):
        bufs = _split_wait(st, after, f"ag1w_{tag}")
        return _split_start(bufs, _ag2_plan, 3 * len(bufs), f"ag2s_{tag}")

    def ag_end(st, after, tag):
        return _split_wait(st, after, f"ag2w_{tag}")

    def rs_start(gs, tag):
        bufs = []
        for nm, g in gs.items():
            if nm != "ffn_up":
                g = g.reshape(N_DEV, g.shape[0] // N_DEV, g.shape[1])
            bufs += [g, _landing_like(g, f"land_{nm}_{tag}")]
        st = _split_start(bufs, _rs1_plan, 4 * len(gs), f"rs1s_{tag}")
        st["names"] = list(gs)
        return st

    def rs_mid(st, after, tag):
        bufs = _split_wait(st, after, f"rs1w_{tag}")
        out = []
        for i, nm in enumerate(st["names"]):
            s = _pair_sum(bufs[2 * i], bufs[2 * i + 1], cidx, f"pairsum_{nm}_{tag}")
            out += [s, _place(s, 4, True, f"place_{nm}_{tag}")]
        st2 = _split_start(out, _rs2_plan, 3 * len(st["names"]), f"rs2s_{tag}")
        st2["names"] = st["names"]
        return st2

    def rs_end(st, after, tag):
        bufs = _split_wait(st, after, f"rs2w_{tag}")
        return dict(zip(st["names"], bufs[1::2]))

    lnames = tuple(n for n in SMALL_REPL if n != "final_norm_w") + SMALL_SHARD

    def small_pack(l, gs, extra):
        return _pack([gs[nm] for nm in lnames] + extra)

    def small_state(d, l, gs):
        arrs = [d[nm][l] if nm in SMALL_REPL else jnp.zeros(gs[nm].shape, F32) for nm in lnames]
        if l == 0:
            arrs += [d["final_norm_w"], jnp.zeros((1,), F32)]
        return _pack(arrs)[0]

    wmv = {name: [d[name] for d in (Wp, Mp, Vp)] for name in BIG}
    groups = dict(b1=[(0, "w_out")], b2=[(0, "ffn_up")], b3=[(0, "ffn_down")],
                  c1=[(1, "w_in"), (1, "w_out")], c2=[(1, "ffn_up")], c3=[(1, "ffn_down")])
    gshards = {g: [wshard(l, n) for l, n in members] for g, members in groups.items()}
    fwd_plan = {("f_in", 0): [("mid", "b1"), ("start", "b2")],
                ("f_out", 0): [("end", "b1"), ("mid", "b2"), ("start", "b3")],
                ("f_n2", 0): [("end", "b2"), ("start", "c1")],
                ("f_up", 0): [("mid", "b3")],
                ("f_act", 0): [("end", "b3"), ("mid", "c1"), ("start", "c2")],
                ("f_end", 0): [("end", "c1")],
                ("f_mix", 1): [("mid", "c2"), ("start", "c3")],
                ("f_out", 1): [("end", "c2")],
                ("f_n2", 1): [("mid", "c3")],
                ("f_act", 1): [("end", "c3")]}
    later = [s for g in gshards.values() for s in g]
    stA = ag_start([wshard(0, "w_in")], "a")
    pre0 = _norm_fwd(x[0], _tie(Wp["norm1_w"][0].reshape(1, D_MODEL), stA["token"]), "l0_norm1_fwd")
    stA2 = ag_mid(stA, [stA["token"], pre0[1]] + later, "a")
    gst = {"b1": ag_start(gshards["b1"], "b1", stA2["token"])}
    (w_in0,) = ag_end(stA2, gst["b1"]["token"], "a")

    cbuf, ctable = _pack([Wp[n] for n in SMALL_SHARD])
    call = _all_gather(cbuf, "ag_conv_w")
    parts = [_unpack(call[i], ctable) for i in range(N_DEV)]
    conv_full = {n: jnp.concatenate([parts[i][j] for i in range(N_DEV)], axis=-1) for j, n in enumerate(SMALL_SHARD)}

    Ws = [_layer_weights(l, {}, Wp, conv_full) for l in range(DEPTH)]
    Ws[0]["win"] = full2d("w_in", w_in0)
    st = {}

    def fwd_actions(actions, after):
        token = None
        for what, g in actions:
            dep = after if token is None else token
            if what == "start":
                gst[g] = ag_start(gshards[g], g, dep)
                token = gst[g]["token"]
            elif what == "mid":
                gst[g] = ag_mid(gst[g], dep, g)
                token = gst[g]["token"]
            else:
                bufs = ag_end(gst[g], dep, g)
                for (gl, n), b in zip(groups[g], bufs):
                    Ws[gl][keys[n]] = full2d(n, b)
                token = bufs[0]
        return token

    def hook(event, l, after, payload=None):
        if (event, l) in fwd_plan:
            return fwd_actions(fwd_plan[event, l], after)
        if event == "b_ffn":
            st["ffn", l] = rs_start(payload, f"ffn{l}")
            return st["ffn", l]["token"]
        if event == "b_mid":
            st["ffn2", l] = rs_mid(st["ffn", l], after, f"ffn{l}")
            if l == 0:
                st["sm1b"] = ag_mid(st["sm1"], st["ffn2", l]["token"], "sm1")
                return st["sm1b"]["token"]
            return st["ffn2", l]["token"]
        if event == "b_in":
            st["io", l] = rs_start(payload, f"io{l}")
            if l == 0:
                st["sm1g"] = ag_end(st["sm1b"], st["io", l]["token"], "sm1")[0]
            return st["io", l]["token"]
        if event == "b_end" and l == 1:
            st["io2", 1] = rs_mid(st["io", 1], after, "io1")
            gbuf1, st["table1"] = small_pack(1, payload, [])
            st["sm1"] = ag_start([gbuf1], "sm1", st["io2", 1]["token"])
            return st["sm1"]["token"]
        return None

    loss, dx, g_final, _, gsmall = _local_step(x[0], loss_target[0], Ws, final_norm_w, hook, pre0)

    out_g, out_d, out_m, out_v = {}, {}, {}, {}
    outs4 = (out_g, out_d, out_m, out_v)
    rts = dict(w_in=64, w_out=128, ffn_up=256, ffn_down=128)
    big_res = {}

    def adam_big(l, parts):
        for name, p in parts.items():
            big_res[name] = _adamw(p, *wmv[name], rts[name], f"adamw_{name}_{l}", layer=l, prev=big_res.get(name))
        return [big_res[name][0] for name in parts]

    def adam_small(l, gall, gs):
        rs = gall.shape[1]
        return _adamw(gall, small_state(Wp, l, gs), small_state(Mp, l, gs), small_state(Vp, l, gs),
                      _row_tile(rs, 512), f"adamw_small_{l}")

    gbuf0, table0 = small_pack(0, gsmall[0], [g_final, loss[0, :1]])
    sm0 = ag_start([gbuf0], "sm0", st["io", 0]["token"])
    o = adam_big(1, rs_end(st["ffn2", 1], sm0["token"], "ffn1"))
    st["io2", 0] = rs_mid(st["io", 0], o, "io0")
    o = adam_big(1, rs_end(st["io2", 1], st["io2", 0]["token"], "io1"))
    o = adam_big(0, rs_end(st["ffn2", 0], o, "ffn0"))
    small_res = {1: adam_small(1, _tie(st["sm1g"], o[-1]), gsmall[1])}
    sm0b = ag_mid(sm0, o + [small_res[1][0]], "sm0")
    small_res[0] = adam_small(0, ag_end(sm0b, sm0b["token"], "sm0")[0], gsmall[0])
    adam_big(0, rs_end(st["io2", 0], small_res[0][0], "io0"))

    for name in BIG:
        for i, dst in enumerate(outs4):
            dst[name] = big_res[name][i]

    unp = {0: [_unpack(r, table0) for r in small_res[0]], 1: [_unpack(r, st["table1"]) for r in small_res[1]]}
    for j, nm in enumerate(lnames):
        if nm in SMALL_REPL:
            for i, dst in enumerate(outs4):
                dst[nm] = jnp.stack([unp[l][i][j] for l in range(DEPTH)])
    for i, dst in enumerate(outs4):
        dst["final_norm_w"] = unp[0][i][len(lnames)]
    loss_total = unp[0][0][len(lnames) + 1][0]

    gsh = []
    for nm in SMALL_SHARD:
        j = lnames.index(nm)
        width = Wp[nm].shape[-1]
        gsh.append(jnp.stack([lax.dynamic_slice_in_dim(unp[l][0][j], me * width, width, axis=1)
                              for l in range(DEPTH)]))
    sbuf, stable = _pack(gsh)
    res = _adamw(sbuf[None], _pack([Wp[n] for n in SMALL_SHARD])[0], _pack([Mp[n] for n in SMALL_SHARD])[0],
                 _pack([Vp[n] for n in SMALL_SHARD])[0], sbuf.shape[0], "adamw_conv_w")
    unp2 = [_unpack(r, stable) for r in res]
    for j, nm in enumerate(SMALL_SHARD):
        for i, dst in enumerate((out_g, out_d, out_m, out_v)):
            dst[nm] = unp2[i][j]

    return (loss_total, dx[None], *[out_g[n] for n in WEIGHTS], *[out_d[n] for n in WEIGHTS],
            *[out_m[n] for n in WEIGHTS], *[out_v[n] for n in WEIGHTS])
```
